```python
import jax
import jax.numpy as jnp
from jax import lax

D_MODEL = 1024
BATCH = 8
SEQ = 8192
DEPTH = 1

HEAD_DIM = 64
WIN_Q_HEADS = 8
WIN_KV_HEADS = 2
WIN_HALF = 128
DIL_SLOTS = 8
DIL_PAIRS = ((128, 1), (512, 4), (2048, 16))
N_DIL = len(DIL_PAIRS)
ROT_DIM = HEAD_DIM // 4
ROPE_THETA = 500000.0
MEM_LEN = 256
X_HEADS = 4
X_HEAD_DIM = D_MODEL // X_HEADS
D_FF = 2816
CONV_WIDTH = 3
WIN_WIDTH = WIN_Q_HEADS * HEAD_DIM
DIL_WIDTH = DIL_SLOTS * HEAD_DIM
MIX_WIDTH = WIN_WIDTH + DIL_WIDTH
A_Q = WIN_WIDTH
A_KV = WIN_KV_HEADS * HEAD_DIM
B_QKV = N_DIL * DIL_WIDTH
IN_WIDTH = A_Q + 2 * A_KV + 3 * B_QKV
SPLITS = (A_Q, A_Q + A_KV, A_Q + 2 * A_KV, A_Q + 2 * A_KV + B_QKV, A_Q + 2 * A_KV + 2 * B_QKV)
DEEPNORM_ALPHA = (2 * DEPTH) ** 0.25
DEEPNORM_BETA = (8 * DEPTH) ** -0.25
LN_EPS = 1e-5
NEG_INF = -1e30
POS_OFFSET_MAX = 4096

kernel_name = 'hymba_window_dilated_deepnorm_encoder'


def layer_norm(x, g, b):
    xf = x.astype(jnp.float32)
    mu = jnp.mean(xf, -1, keepdims=True)
    var = jnp.mean(jnp.square(xf - mu), -1, keepdims=True)
    return ((xf - mu) * lax.rsqrt(var + LN_EPS) * g + b).astype(x.dtype)


def rms_norm(x, g):
    xf = x.astype(jnp.float32)
    return (xf * lax.rsqrt(jnp.mean(jnp.square(xf), -1, keepdims=True) + LN_EPS) * g).astype(x.dtype)


def partial_rope(t, positions):
    half = ROT_DIM // 2
    inv_freq = ROPE_THETA ** (-jnp.arange(0, ROT_DIM, 2, dtype=jnp.float32) / ROT_DIM)
    ang = positions.astype(jnp.float32)[:, :, None] * inv_freq
    cos = jnp.cos(ang)[:, :, None, :]
    sin = jnp.sin(ang)[:, :, None, :]
    tr = t[..., :ROT_DIM].astype(jnp.float32)
    t1, t2 = tr[..., :half], tr[..., half:]
    rot = jnp.concatenate([t1 * cos - t2 * sin, t2 * cos + t1 * sin], -1).astype(t.dtype)
    return jnp.concatenate([rot, t[..., ROT_DIM:]], -1)


def banded_attention(q, k, v, n_side, sink=None):
    bt, seq_len, hkv, grp, dh = q.shape
    blk = n_side
    nb = -(-seq_len // blk)
    pad = nb * blk - seq_len
    qb = jnp.pad(q, ((0, 0), (0, pad), (0, 0), (0, 0), (0, 0))).reshape(bt, nb, blk, hkv, grp, dh)

    def neighbourhood(t):
        tp = jnp.pad(t, ((0, 0), (blk, blk + pad), (0, 0), (0, 0))).reshape(bt, nb + 2, blk, hkv, dh)
        return jnp.concatenate([tp[:, :-2], tp[:, 1:-1], tp[:, 2:]], axis=2)

    kw = neighbourhood(k)
    vw = neighbourhood(v)
    s = jnp.einsum('bnqhgd,bnkhd->bnhgqk', qb, kw).astype(jnp.float32) * (dh ** -0.5)
    qi = jnp.arange(blk)[:, None]
    kj = jnp.arange(3 * blk)[None, :]
    kabs = jnp.arange(nb)[:, None, None] * blk + kj[None] - blk
    mask = (jnp.abs(kj - blk - qi) <= n_side)[None] & (kabs >= 0) & (kabs < seq_len)
    mask = mask[None, :, None, None]
    s = jnp.where(mask, s, NEG_INF)
    m = jnp.max(s, -1)
    if sink is not None:
        sink_f = sink.astype(jnp.float32)[None, None, :, :, None]
        m = jnp.maximum(m, sink_f)
    p = jnp.where(mask, jnp.exp(s - m[..., None]), 0.0)
    denom = jnp.sum(p, -1)
    if sink is not None:
        denom = denom + jnp.exp(sink_f - m)
    o = jnp.einsum('bnhgqk,bnkhd->bnqhgd', p, vw.astype(jnp.float32))
    o = o / jnp.moveaxis(denom, -1, 2)[..., None]
    o = o.astype(q.dtype).reshape(bt, nb * blk, hkv, grp, dh)[:, :seq_len]
    lse = jnp.moveaxis(m + jnp.log(denom), -1, 2).reshape(bt, nb * blk, hkv, grp)[:, :seq_len]
    return o, lse


def to_residue(t, r):
    b, s = t.shape[:2]
    t = t.reshape((b, s // r, r) + t.shape[2:])
    t = jnp.moveaxis(t, 2, 1)
    return t.reshape((b * r, s // r) + t.shape[3:])


def from_residue(t, r, batch):
    t = t.reshape((batch, r) + t.shape[1:])
    t = jnp.moveaxis(t, 1, 2)
    return t.reshape((batch, t.shape[1] * r) + t.shape[3:])


def parallel_mixer(h, positions, w_in, attn_sink, g_win, g_dil, w_out):
    b, s, _ = h.shape
    z = h @ w_in
    qa, ka, va, qb, kb, vb = jnp.split(z, SPLITS, axis=-1)
    qa = partial_rope(qa.reshape(b, s, WIN_Q_HEADS, HEAD_DIM), positions)
    ka = partial_rope(ka.reshape(b, s, WIN_KV_HEADS, HEAD_DIM), positions)
    va = va.reshape(b, s, WIN_KV_HEADS, HEAD_DIM)
    qa = qa.reshape(b, s, WIN_KV_HEADS, WIN_Q_HEADS // WIN_KV_HEADS, HEAD_DIM)
    out_a, _ = banded_attention(qa, ka, va, WIN_HALF, attn_sink.reshape(WIN_KV_HEADS, -1))
    out_a = out_a.reshape(b, s, WIN_WIDTH)
    n_heads_b = N_DIL * DIL_SLOTS
    qb = partial_rope(qb.reshape(b, s, n_heads_b, HEAD_DIM), positions).reshape(b, s, N_DIL, DIL_SLOTS, HEAD_DIM)
    kb = partial_rope(kb.reshape(b, s, n_heads_b, HEAD_DIM), positions).reshape(b, s, N_DIL, DIL_SLOTS, HEAD_DIM)
    vb = vb.reshape(b, s, N_DIL, DIL_SLOTS, HEAD_DIM)
    outs = []
    lses = []
    for gi, (window, dil) in enumerate(DIL_PAIRS):
        n_side = window // (2 * dil)
        o, lse = banded_attention(to_residue(qb[:, :, gi], dil)[:, :, :, None],
                                  to_residue(kb[:, :, gi], dil),
                                  to_residue(vb[:, :, gi], dil), n_side)
        outs.append(from_residue(o[:, :, :, 0], dil, b))
        lses.append(from_residue(lse[..., 0], dil, b))
    wts = jax.nn.softmax(jnp.stack(lses), axis=0)[..., None]
    out_b = jnp.sum(wts * jnp.stack(outs).astype(jnp.float32), axis=0).astype(h.dtype).reshape(b, s, DIL_WIDTH)
    mixed = jnp.concatenate([rms_norm(out_a, g_win), rms_norm(out_b, g_dil)], -1)
    return mixed @ w_out


def memory_cross_attention(h, mem_n, w_q, w_k, w_v, w_o):
    b, s, _ = h.shape
    m_len = mem_n.shape[1]
    q = (h @ w_q).reshape(b, s, X_HEADS, X_HEAD_DIM)
    k = (mem_n @ w_k).reshape(b, m_len, X_HEADS, X_HEAD_DIM)
    v = (mem_n @ w_v).reshape(b, m_len, X_HEADS, X_HEAD_DIM)
    sc = jnp.einsum('bshd,bmhd->bhsm', q, k).astype(jnp.float32) * (X_HEAD_DIM ** -0.5)
    p = jax.nn.softmax(sc, axis=-1)
    o = jnp.einsum('bhsm,bmhd->bshd', p, v.astype(jnp.float32)).astype(h.dtype)
    return o.reshape(b, s, D_MODEL) @ w_o


def conv_glu(h, w_gate, w_up, conv_w, conv_b, w_down):
    s = h.shape[1]
    g = h @ w_gate
    half = CONV_WIDTH // 2
    gp = jnp.pad(g, ((0, 0), (half, half), (0, 0)))
    g = sum(gp[:, j:j + s] * conv_w[j] for j in range(CONV_WIDTH)) + conv_b
    return (jax.nn.gelu(g, approximate=False) * (h @ w_up)) @ w_down


def _fwd_setup_inputs(seed: int = 0) -> dict:
    key = jax.random.key(seed)
    keys = list(jax.random.split(key, 40))
    f32 = jnp.float32
    d = D_MODEL
    nl = DEPTH
    beta = DEEPNORM_BETA

    def normal(shape, scale):
        return jax.random.normal(keys.pop(), shape, f32) * scale

    def gain(shape):
        return 1.0 + normal(shape, 0.02)

    def bias(shape):
        return normal(shape, 0.02)

    in_scale = jnp.concatenate([jnp.ones((A_Q + A_KV,), f32), jnp.full((A_KV,), beta, f32),
                                jnp.ones((2 * B_QKV,), f32), jnp.full((B_QKV,), beta, f32)])
    x = normal((BATCH, SEQ, d), 1.0)
    mem = normal((BATCH, MEM_LEN, d), 1.0)
    positions = jnp.arange(SEQ, dtype=jnp.int32)[None, :] + jax.random.randint(
        keys.pop(), (BATCH, 1), 0, POS_OFFSET_MAX, dtype=jnp.int32)
    return {
        'x': x,
        'mem': mem,
        'positions': positions,
        'ln_in_g': gain((d,)),
        'ln_in_b': bias((d,)),
        'w_in': normal((nl, d, IN_WIDTH), d ** -0.5) * in_scale,
        'attn_sink': normal((nl, WIN_Q_HEADS), 0.5),
        'g_win': gain((nl, WIN_WIDTH)),
        'g_dil': gain((nl, DIL_WIDTH)),
        'w_mix_out': normal((nl, MIX_WIDTH, d), MIX_WIDTH ** -0.5 * beta),
        'ln1_g': gain((nl, d)),
        'ln1_b': bias((nl, d)),
        'mem_ln_g': gain((nl, d)),
        'mem_ln_b': bias((nl, d)),
        'w_xq': normal((nl, d, d), d ** -0.5),
        'w_xk': normal((nl, d, d), d ** -0.5),
        'w_xv': normal((nl, d, d), d ** -0.5 * beta),
        'w_xo': normal((nl, d, d), d ** -0.5 * beta),
        'ln2_g': gain((nl, d)),
        'ln2_b': bias((nl, d)),
        'w_gate': normal((nl, d, D_FF), d ** -0.5),
        'w_up': normal((nl, d, D_FF), d ** -0.5 * beta),
        'conv_w': normal((nl, CONV_WIDTH, D_FF), CONV_WIDTH ** -0.5),
        'conv_b': bias((nl, D_FF)),
        'w_down': normal((nl, D_FF, d), D_FF ** -0.5 * beta),
        'ln3_g': gain((nl, d)),
        'ln3_b': bias((nl, d)),
    }


def _fwd_reference(x, mem, positions, ln_in_g, ln_in_b, w_in, attn_sink, g_win, g_dil, w_mix_out,
              ln1_g, ln1_b, mem_ln_g, mem_ln_b, w_xq, w_xk, w_xv, w_xo, ln2_g, ln2_b,
              w_gate, w_up, conv_w, conv_b, w_down, ln3_g, ln3_b):
    h = layer_norm(x, ln_in_g, ln_in_b)
    for l in range(DEPTH):
        mix = parallel_mixer(h, positions, w_in[l], attn_sink[l], g_win[l], g_dil[l], w_mix_out[l])
        h = layer_norm(DEEPNORM_ALPHA * h + mix, ln1_g[l], ln1_b[l])
        mem_n = layer_norm(mem, mem_ln_g[l], mem_ln_b[l])
        xa = memory_cross_attention(h, mem_n, w_xq[l], w_xk[l], w_xv[l], w_xo[l])
        h = layer_norm(DEEPNORM_ALPHA * h + xa, ln2_g[l], ln2_b[l])
        ff = conv_glu(h, w_gate[l], w_up[l], conv_w[l], conv_b[l], w_down[l])
        h = layer_norm(DEEPNORM_ALPHA * h + ff, ln3_g[l], ln3_b[l])
    return h


import jax as _jax
import jax.numpy as _jnp

TWIN_FORMAT = 'train_step'
FWD_PARAMS = ['x', 'mem', 'positions', 'ln_in_g', 'ln_in_b', 'w_in', 'attn_sink', 'g_win', 'g_dil', 'w_mix_out', 'ln1_g', 'ln1_b', 'mem_ln_g', 'mem_ln_b', 'w_xq', 'w_xk', 'w_xv', 'w_xo', 'ln2_g', 'ln2_b', 'w_gate', 'w_up', 'conv_w', 'conv_b', 'w_down', 'ln3_g', 'ln3_b']
TWIN_WEIGHTS = ['ln_in_g', 'ln_in_b', 'w_in', 'attn_sink', 'g_win', 'g_dil', 'w_mix_out', 'ln1_g', 'ln1_b', 'mem_ln_g', 'mem_ln_b', 'w_xq', 'w_xk', 'w_xv', 'w_xo', 'ln2_g', 'ln2_b', 'w_gate', 'w_up', 'conv_w', 'conv_b', 'w_down', 'ln3_g', 'ln3_b']
TWIN_DIFF_INPUT = 'x'
TWIN_INPUTS = ['x', 'mem', 'positions', 'ln_in_g', 'ln_in_b', 'w_in', 'attn_sink', 'g_win', 'g_dil', 'w_mix_out', 'ln1_g', 'ln1_b', 'mem_ln_g', 'mem_ln_b', 'w_xq', 'w_xk', 'w_xv', 'w_xo', 'ln2_g', 'ln2_b', 'w_gate', 'w_up', 'conv_w', 'conv_b', 'w_down', 'ln3_g', 'ln3_b', 'loss_target', 'm_ln_in_g', 'm_ln_in_b', 'm_w_in', 'm_attn_sink', 'm_g_win', 'm_g_dil', 'm_w_mix_out', 'm_ln1_g', 'm_ln1_b', 'm_mem_ln_g', 'm_mem_ln_b', 'm_w_xq', 'm_w_xk', 'm_w_xv', 'm_w_xo', 'm_ln2_g', 'm_ln2_b', 'm_w_gate', 'm_w_up', 'm_conv_w', 'm_conv_b', 'm_w_down', 'm_ln3_g', 'm_ln3_b', 'v_ln_in_g', 'v_ln_in_b', 'v_w_in', 'v_attn_sink', 'v_g_win', 'v_g_dil', 'v_w_mix_out', 'v_ln1_g', 'v_ln1_b', 'v_mem_ln_g', 'v_mem_ln_b', 'v_w_xq', 'v_w_xk', 'v_w_xv', 'v_w_xo', 'v_ln2_g', 'v_ln2_b', 'v_w_gate', 'v_w_up', 'v_conv_w', 'v_conv_b', 'v_w_down', 'v_ln3_g', 'v_ln3_b']
TWIN_OUTPUTS = ['loss', 'grad_x', 'grad_ln_in_g', 'grad_ln_in_b', 'grad_w_in', 'grad_attn_sink', 'grad_g_win', 'grad_g_dil', 'grad_w_mix_out', 'grad_ln1_g', 'grad_ln1_b', 'grad_mem_ln_g', 'grad_mem_ln_b', 'grad_w_xq', 'grad_w_xk', 'grad_w_xv', 'grad_w_xo', 'grad_ln2_g', 'grad_ln2_b', 'grad_w_gate', 'grad_w_up', 'grad_conv_w', 'grad_conv_b', 'grad_w_down', 'grad_ln3_g', 'grad_ln3_b', 'delta_ln_in_g', 'delta_ln_in_b', 'delta_w_in', 'delta_attn_sink', 'delta_g_win', 'delta_g_dil', 'delta_w_mix_out', 'delta_ln1_g', 'delta_ln1_b', 'delta_mem_ln_g', 'delta_mem_ln_b', 'delta_w_xq', 'delta_w_xk', 'delta_w_xv', 'delta_w_xo', 'delta_ln2_g', 'delta_ln2_b', 'delta_w_gate', 'delta_w_up', 'delta_conv_w', 'delta_conv_b', 'delta_w_down', 'delta_ln3_g', 'delta_ln3_b', 'new_m_ln_in_g', 'new_m_ln_in_b', 'new_m_w_in', 'new_m_attn_sink', 'new_m_g_win', 'new_m_g_dil', 'new_m_w_mix_out', 'new_m_ln1_g', 'new_m_ln1_b', 'new_m_mem_ln_g', 'new_m_mem_ln_b', 'new_m_w_xq', 'new_m_w_xk', 'new_m_w_xv', 'new_m_w_xo', 'new_m_ln2_g', 'new_m_ln2_b', 'new_m_w_gate', 'new_m_w_up', 'new_m_conv_w', 'new_m_conv_b', 'new_m_w_down', 'new_m_ln3_g', 'new_m_ln3_b', 'new_v_ln_in_g', 'new_v_ln_in_b', 'new_v_w_in', 'new_v_attn_sink', 'new_v_g_win', 'new_v_g_dil', 'new_v_w_mix_out', 'new_v_ln1_g', 'new_v_ln1_b', 'new_v_mem_ln_g', 'new_v_mem_ln_b', 'new_v_w_xq', 'new_v_w_xk', 'new_v_w_xv', 'new_v_w_xo', 'new_v_ln2_g', 'new_v_ln2_b', 'new_v_w_gate', 'new_v_w_up', 'new_v_conv_w', 'new_v_conv_b', 'new_v_w_down', 'new_v_ln3_g', 'new_v_ln3_b']
TWIN_LEAF_KINDS = {'loss': 'loss', 'grad_x': 'grad_x', 'grad_ln_in_g': 'grad_w', 'grad_ln_in_b': 'grad_w', 'grad_w_in': 'grad_w', 'grad_attn_sink': 'grad_w', 'grad_g_win': 'grad_w', 'grad_g_dil': 'grad_w', 'grad_w_mix_out': 'grad_w', 'grad_ln1_g': 'grad_w', 'grad_ln1_b': 'grad_w', 'grad_mem_ln_g': 'grad_w', 'grad_mem_ln_b': 'grad_w', 'grad_w_xq': 'grad_w', 'grad_w_xk': 'grad_w', 'grad_w_xv': 'grad_w', 'grad_w_xo': 'grad_w', 'grad_ln2_g': 'grad_w', 'grad_ln2_b': 'grad_w', 'grad_w_gate': 'grad_w', 'grad_w_up': 'grad_w', 'grad_conv_w': 'grad_w', 'grad_conv_b': 'grad_w', 'grad_w_down': 'grad_w', 'grad_ln3_g': 'grad_w', 'grad_ln3_b': 'grad_w', 'delta_ln_in_g': 'delta_w', 'delta_ln_in_b': 'delta_w', 'delta_w_in': 'delta_w', 'delta_attn_sink': 'delta_w', 'delta_g_win': 'delta_w', 'delta_g_dil': 'delta_w', 'delta_w_mix_out': 'delta_w', 'delta_ln1_g': 'delta_w', 'delta_ln1_b': 'delta_w', 'delta_mem_ln_g': 'delta_w', 'delta_mem_ln_b': 'delta_w', 'delta_w_xq': 'delta_w', 'delta_w_xk': 'delta_w', 'delta_w_xv': 'delta_w', 'delta_w_xo': 'delta_w', 'delta_ln2_g': 'delta_w', 'delta_ln2_b': 'delta_w', 'delta_w_gate': 'delta_w', 'delta_w_up': 'delta_w', 'delta_conv_w': 'delta_w', 'delta_conv_b': 'delta_w', 'delta_w_down': 'delta_w', 'delta_ln3_g': 'delta_w', 'delta_ln3_b': 'delta_w', 'new_m_ln_in_g': 'new_m', 'new_m_ln_in_b': 'new_m', 'new_m_w_in': 'new_m', 'new_m_attn_sink': 'new_m', 'new_m_g_win': 'new_m', 'new_m_g_dil': 'new_m', 'new_m_w_mix_out': 'new_m', 'new_m_ln1_g': 'new_m', 'new_m_ln1_b': 'new_m', 'new_m_mem_ln_g': 'new_m', 'new_m_mem_ln_b': 'new_m', 'new_m_w_xq': 'new_m', 'new_m_w_xk': 'new_m', 'new_m_w_xv': 'new_m', 'new_m_w_xo': 'new_m', 'new_m_ln2_g': 'new_m', 'new_m_ln2_b': 'new_m', 'new_m_w_gate': 'new_m', 'new_m_w_up': 'new_m', 'new_m_conv_w': 'new_m', 'new_m_conv_b': 'new_m', 'new_m_w_down': 'new_m', 'new_m_ln3_g': 'new_m', 'new_m_ln3_b': 'new_m', 'new_v_ln_in_g': 'new_v', 'new_v_ln_in_b': 'new_v', 'new_v_w_in': 'new_v', 'new_v_attn_sink': 'new_v', 'new_v_g_win': 'new_v', 'new_v_g_dil': 'new_v', 'new_v_w_mix_out': 'new_v', 'new_v_ln1_g': 'new_v', 'new_v_ln1_b': 'new_v', 'new_v_mem_ln_g': 'new_v', 'new_v_mem_ln_b': 'new_v', 'new_v_w_xq': 'new_v', 'new_v_w_xk': 'new_v', 'new_v_w_xv': 'new_v', 'new_v_w_xo': 'new_v', 'new_v_ln2_g': 'new_v', 'new_v_ln2_b': 'new_v', 'new_v_w_gate': 'new_v', 'new_v_w_up': 'new_v', 'new_v_conv_w': 'new_v', 'new_v_conv_b': 'new_v', 'new_v_w_down': 'new_v', 'new_v_ln3_g': 'new_v', 'new_v_ln3_b': 'new_v'}


def _forward(args):
    return _fwd_reference(*[args[k] for k in FWD_PARAMS])


def _output_shape():
    def fwd():
        inp = _fwd_setup_inputs(0)
        return _fwd_reference(*[inp[k] for k in FWD_PARAMS])
    out = _jax.eval_shape(fwd)
    return out.shape, out.dtype

N_MICROBATCH = 1
ADAM_LR = 0.001
ADAM_B1 = 0.9
ADAM_B2 = 0.999
ADAM_EPS = 1e-08
ADAM_WD = 0.01
ADAM_STEP = 10
PER_EXAMPLE_BATCH_AXIS = {'x': 0, 'mem': 0, 'positions': 0, 'loss_target': 0}
SHARED_INPUTS = []
_WEIGHT_DTYPES = {'ln_in_g': _jnp.float32, 'ln_in_b': _jnp.float32, 'w_in': _jnp.float32, 'attn_sink': _jnp.float32, 'g_win': _jnp.float32, 'g_dil': _jnp.float32, 'w_mix_out': _jnp.float32, 'ln1_g': _jnp.float32, 'ln1_b': _jnp.float32, 'mem_ln_g': _jnp.float32, 'mem_ln_b': _jnp.float32, 'w_xq': _jnp.float32, 'w_xk': _jnp.float32, 'w_xv': _jnp.float32, 'w_xo': _jnp.float32, 'ln2_g': _jnp.float32, 'ln2_b': _jnp.float32, 'w_gate': _jnp.float32, 'w_up': _jnp.float32, 'conv_w': _jnp.float32, 'conv_b': _jnp.float32, 'w_down': _jnp.float32, 'ln3_g': _jnp.float32, 'ln3_b': _jnp.float32}
MOMENT_SCALE = {'ln_in_g': 1.630601e+00, 'ln_in_b': 4.100319e+00, 'w_in': 1.154575e-01, 'attn_sink': 2.832759e-03, 'g_win': 1.484046e-01, 'g_dil': 1.158017e-01, 'w_mix_out': 2.137776e-01, 'ln1_g': 2.238091e+00, 'ln1_b': 1.047814e+00, 'mem_ln_g': 1.303976e-02, 'mem_ln_b': 3.034763e-01, 'w_xq': 7.995400e-03, 'w_xk': 7.974008e-03, 'w_xv': 1.772295e-02, 'w_xo': 1.778776e-02, 'ln2_g': 2.245660e+00, 'ln2_b': 1.047709e+00, 'w_gate': 2.977877e-02, 'w_up': 4.923208e-02, 'conv_w': 3.045687e-02, 'conv_b': 2.934425e-02, 'w_down': 8.100616e-02, 'ln3_g': 6.404396e+01, 'ln3_b': 4.973831e+00}


def _to_microbatches(a, axis):
    t = _jnp.moveaxis(a, axis, 0)
    t = t.reshape((N_MICROBATCH, t.shape[0] // N_MICROBATCH) + t.shape[1:])
    return _jnp.moveaxis(t, 1, axis + 1)


def setup_inputs(seed: int = 0) -> dict:
    inp = _fwd_setup_inputs(seed)
    key = _jax.random.fold_in(_jax.random.key(seed), 7919)
    shape, _ = _output_shape()
    out = dict(inp)
    out["loss_target"] = _jax.random.normal(_jax.random.fold_in(key, 0), shape, _jnp.float32)
    for i, name in enumerate(TWIN_WEIGHTS):
        w = inp[name].astype(_jnp.float32)
        if MOMENT_SCALE is None:
            s = _jnp.sqrt(_jnp.mean(_jnp.square(w)) + 1e-30)
        else:
            s = MOMENT_SCALE[name]
        km, kv = _jax.random.split(_jax.random.fold_in(key, i + 1))
        out[name] = w
        out["m_" + name] = s * _jax.random.normal(km, w.shape, _jnp.float32)
        out["v_" + name] = (s * s) * _jax.random.uniform(kv, w.shape, _jnp.float32, 0.5, 1.5)
    if N_MICROBATCH > 1:
        for name, axis in PER_EXAMPLE_BATCH_AXIS.items():
            out[name] = _to_microbatches(out[name], axis)
    return {'x': out['x'], 'mem': out['mem'], 'positions': out['positions'], 'ln_in_g': out['ln_in_g'], 'ln_in_b': out['ln_in_b'], 'w_in': out['w_in'], 'attn_sink': out['attn_sink'], 'g_win': out['g_win'], 'g_dil': out['g_dil'], 'w_mix_out': out['w_mix_out'], 'ln1_g': out['ln1_g'], 'ln1_b': out['ln1_b'], 'mem_ln_g': out['mem_ln_g'], 'mem_ln_b': out['mem_ln_b'], 'w_xq': out['w_xq'], 'w_xk': out['w_xk'], 'w_xv': out['w_xv'], 'w_xo': out['w_xo'], 'ln2_g': out['ln2_g'], 'ln2_b': out['ln2_b'], 'w_gate': out['w_gate'], 'w_up': out['w_up'], 'conv_w': out['conv_w'], 'conv_b': out['conv_b'], 'w_down': out['w_down'], 'ln3_g': out['ln3_g'], 'ln3_b': out['ln3_b'], 'loss_target': out['loss_target'], 'm_ln_in_g': out['m_ln_in_g'], 'm_ln_in_b': out['m_ln_in_b'], 'm_w_in': out['m_w_in'], 'm_attn_sink': out['m_attn_sink'], 'm_g_win': out['m_g_win'], 'm_g_dil': out['m_g_dil'], 'm_w_mix_out': out['m_w_mix_out'], 'm_ln1_g': out['m_ln1_g'], 'm_ln1_b': out['m_ln1_b'], 'm_mem_ln_g': out['m_mem_ln_g'], 'm_mem_ln_b': out['m_mem_ln_b'], 'm_w_xq': out['m_w_xq'], 'm_w_xk': out['m_w_xk'], 'm_w_xv': out['m_w_xv'], 'm_w_xo': out['m_w_xo'], 'm_ln2_g': out['m_ln2_g'], 'm_ln2_b': out['m_ln2_b'], 'm_w_gate': out['m_w_gate'], 'm_w_up': out['m_w_up'], 'm_conv_w': out['m_conv_w'], 'm_conv_b': out['m_conv_b'], 'm_w_down': out['m_w_down'], 'm_ln3_g': out['m_ln3_g'], 'm_ln3_b': out['m_ln3_b'], 'v_ln_in_g': out['v_ln_in_g'], 'v_ln_in_b': out['v_ln_in_b'], 'v_w_in': out['v_w_in'], 'v_attn_sink': out['v_attn_sink'], 'v_g_win': out['v_g_win'], 'v_g_dil': out['v_g_dil'], 'v_w_mix_out': out['v_w_mix_out'], 'v_ln1_g': out['v_ln1_g'], 'v_ln1_b': out['v_ln1_b'], 'v_mem_ln_g': out['v_mem_ln_g'], 'v_mem_ln_b': out['v_mem_ln_b'], 'v_w_xq': out['v_w_xq'], 'v_w_xk': out['v_w_xk'], 'v_w_xv': out['v_w_xv'], 'v_w_xo': out['v_w_xo'], 'v_ln2_g': out['v_ln2_g'], 'v_ln2_b': out['v_ln2_b'], 'v_w_gate': out['v_w_gate'], 'v_w_up': out['v_w_up'], 'v_conv_w': out['v_conv_w'], 'v_conv_b': out['v_conv_b'], 'v_w_down': out['v_w_down'], 'v_ln3_g': out['v_ln3_g'], 'v_ln3_b': out['v_ln3_b']}


def _loss(weights, diff, rest, loss_target):
    with _jax.named_scope("forward"):
        args = {**rest, TWIN_DIFF_INPUT: diff, **{k: w.astype(_WEIGHT_DTYPES[k]) for k, w in weights.items()}}
        y = _forward(args)
    with _jax.named_scope("loss_head"):
        err = _jnp.square(y.astype(_jnp.float32) - loss_target)
        return 0.5 * _jnp.sum(_jnp.mean(err, axis=-1)) if err.ndim else 0.5 * err


def _adamw(w, g, m, v):
    m = ADAM_B1 * m + (1.0 - ADAM_B1) * g
    v = ADAM_B2 * v + (1.0 - ADAM_B2) * _jnp.square(g)
    m_hat = m / (1.0 - ADAM_B1 ** ADAM_STEP)
    v_hat = v / (1.0 - ADAM_B2 ** ADAM_STEP)
    delta = -ADAM_LR * (m_hat / (_jnp.sqrt(v_hat) + ADAM_EPS) + ADAM_WD * w)
    return delta, m, v


def reference(x, mem, positions, ln_in_g, ln_in_b, w_in, attn_sink, g_win, g_dil, w_mix_out, ln1_g, ln1_b, mem_ln_g, mem_ln_b, w_xq, w_xk, w_xv, w_xo, ln2_g, ln2_b, w_gate, w_up, conv_w, conv_b, w_down, ln3_g, ln3_b, loss_target, m_ln_in_g, m_ln_in_b, m_w_in, m_attn_sink, m_g_win, m_g_dil, m_w_mix_out, m_ln1_g, m_ln1_b, m_mem_ln_g, m_mem_ln_b, m_w_xq, m_w_xk, m_w_xv, m_w_xo, m_ln2_g, m_ln2_b, m_w_gate, m_w_up, m_conv_w, m_conv_b, m_w_down, m_ln3_g, m_ln3_b, v_ln_in_g, v_ln_in_b, v_w_in, v_attn_sink, v_g_win, v_g_dil, v_w_mix_out, v_ln1_g, v_ln1_b, v_mem_ln_g, v_mem_ln_b, v_w_xq, v_w_xk, v_w_xv, v_w_xo, v_ln2_g, v_ln2_b, v_w_gate, v_w_up, v_conv_w, v_conv_b, v_w_down, v_ln3_g, v_ln3_b):
    given = dict(x=x, mem=mem, positions=positions, ln_in_g=ln_in_g, ln_in_b=ln_in_b, w_in=w_in, attn_sink=attn_sink, g_win=g_win, g_dil=g_dil, w_mix_out=w_mix_out, ln1_g=ln1_g, ln1_b=ln1_b, mem_ln_g=mem_ln_g, mem_ln_b=mem_ln_b, w_xq=w_xq, w_xk=w_xk, w_xv=w_xv, w_xo=w_xo, ln2_g=ln2_g, ln2_b=ln2_b, w_gate=w_gate, w_up=w_up, conv_w=conv_w, conv_b=conv_b, w_down=w_down, ln3_g=ln3_g, ln3_b=ln3_b, loss_target=loss_target, m_ln_in_g=m_ln_in_g, m_ln_in_b=m_ln_in_b, m_w_in=m_w_in, m_attn_sink=m_attn_sink, m_g_win=m_g_win, m_g_dil=m_g_dil, m_w_mix_out=m_w_mix_out, m_ln1_g=m_ln1_g, m_ln1_b=m_ln1_b, m_mem_ln_g=m_mem_ln_g, m_mem_ln_b=m_mem_ln_b, m_w_xq=m_w_xq, m_w_xk=m_w_xk, m_w_xv=m_w_xv, m_w_xo=m_w_xo, m_ln2_g=m_ln2_g, m_ln2_b=m_ln2_b, m_w_gate=m_w_gate, m_w_up=m_w_up, m_conv_w=m_conv_w, m_conv_b=m_conv_b, m_w_down=m_w_down, m_ln3_g=m_ln3_g, m_ln3_b=m_ln3_b, v_ln_in_g=v_ln_in_g, v_ln_in_b=v_ln_in_b, v_w_in=v_w_in, v_attn_sink=v_attn_sink, v_g_win=v_g_win, v_g_dil=v_g_dil, v_w_mix_out=v_w_mix_out, v_ln1_g=v_ln1_g, v_ln1_b=v_ln1_b, v_mem_ln_g=v_mem_ln_g, v_mem_ln_b=v_mem_ln_b, v_w_xq=v_w_xq, v_w_xk=v_w_xk, v_w_xv=v_w_xv, v_w_xo=v_w_xo, v_ln2_g=v_ln2_g, v_ln2_b=v_ln2_b, v_w_gate=v_w_gate, v_w_up=v_w_up, v_conv_w=v_conv_w, v_conv_b=v_conv_b, v_w_down=v_w_down, v_ln3_g=v_ln3_g, v_ln3_b=v_ln3_b)
    weights = {n: given[n] for n in TWIN_WEIGHTS}
    shared = {n: given[n] for n in SHARED_INPUTS}
    per_example = {n: given[n] for n in ['x', 'mem', 'positions']}
    grad_fn = _jax.value_and_grad(_loss, argnums=(0, 1))

    def one_microbatch(ex, loss_target):
        ex = dict(ex)
        diff = ex.pop(TWIN_DIFF_INPUT)
        return grad_fn(weights, diff, {**shared, **ex}, loss_target)

    if N_MICROBATCH == 1:
        loss, (grad_w, grad_x) = one_microbatch(per_example, given["loss_target"])
    else:
        def body(carry, xs):
            loss_sum, grad_sum = carry
            l_k, (gw_k, gx_k) = one_microbatch(xs[0], xs[1])
            with _jax.named_scope("update"):
                return (loss_sum + l_k, _jax.tree.map(_jnp.add, grad_sum, gw_k)), gx_k

        init = (_jnp.zeros((), _jnp.float32), _jax.tree.map(_jnp.zeros_like, weights))
        (loss, grad_w), grad_x = _jax.lax.scan(body, init, (per_example, given["loss_target"]))
    with _jax.named_scope("update"):
        delta_w, new_m, new_v = {}, {}, {}
        for n in TWIN_WEIGHTS:
            delta_w[n], new_m[n], new_v[n] = _adamw(weights[n], grad_w[n], given["m_" + n], given["v_" + n])
    return (loss, grad_x, *[grad_w[n] for n in TWIN_WEIGHTS], *[delta_w[n] for n in TWIN_WEIGHTS],
            *[new_m[n] for n in TWIN_WEIGHTS], *[new_v[n] for n in TWIN_WEIGHTS])
```

```python
import functools
import math

import jax
import jax.numpy as jnp
from jax import lax
from jax.experimental import pallas as pl
from jax.experimental.pallas import tpu as pltpu

F32 = jnp.float32
BF16 = jnp.bfloat16
SDS = jax.ShapeDtypeStruct
_PALLAS_CALL = pl.pallas_call

D_MODEL = 1024
HEAD_DIM = 64
WIN_HALF = 128
DIL_PAIRS = ((128, 1), (512, 4), (2048, 16))
DIL_SIDE = 64
ROT_DIM = 16
ROPE_THETA = 500000.0
MEM_LEN = 256
X_HEADS = 4
X_HEAD_DIM = 256
D_FF = 2816
IN_WIDTH = 5376
ZW = 5632
Z_QB, Z_KB, Z_VB, Z_QA, Z_KA, Z_VA = 0, 1536, 3072, 4608, 5120, 5248
ALPHA = (2.0) ** 0.25
LN_EPS = 1e-5
NEG_INF = -1e30
ADAM_LR, ADAM_B1, ADAM_B2, ADAM_EPS, ADAM_WD, ADAM_STEP = 0.001, 0.9, 0.999, 1e-08, 0.01, 10
N_DEV = 8
MESH_AXES = ("x", "y", "c")
VMEM_LIMIT_BYTES = 52 * 1024 * 1024
TABW = 384

PACK_ROWS = (("w_in", 672), ("w_gate", 352), ("w_up", 352), ("w_mix_out", 128), ("w_xq", 128), ("w_xk", 128),
             ("w_xv", 128), ("w_xo", 128), ("w_down", 352))
PACK_R = sum(r for _, r in PACK_ROWS)
SMALL_ROWS = 24


def _pick(n, cands):
    for c in cands:
        if n % c == 0:
            return c
    return n


def _pcall(body, *, name, out_shape, grid=None, in_specs=None, out_specs=None, scratch_shapes=(), dims=None,
           aliases=None):
    kw = {}
    if grid is not None:
        kw["grid"] = grid
    if in_specs is not None:
        kw["in_specs"] = in_specs
    if out_specs is not None:
        kw["out_specs"] = out_specs
    if aliases:
        kw["input_output_aliases"] = aliases
    return _PALLAS_CALL(
        body, name=name, out_shape=out_shape, scratch_shapes=list(scratch_shapes),
        compiler_params=pltpu.CompilerParams(dimension_semantics=dims, vmem_limit_bytes=VMEM_LIMIT_BYTES), **kw)


def _mm(a, b, *, trans_b, out_dtype, name, addends=(), coefs=()):
    m, k = a.shape
    n = b.shape[0] if trans_b else b.shape[1]
    tm = _pick(m, (512, 256))
    tn = _pick(n, (512, 1408, 256, 128))
    n_add = len(addends)
    dn = (((1,), (1,)), ((), ())) if trans_b else (((1,), (0,)), ((), ()))

    def body(a_ref, b_ref, *rest):
        o_ref = rest[n_add]
        acc = lax.dot_general(a_ref[...].astype(BF16), b_ref[...].astype(BF16), dn, preferred_element_type=F32)
        for r_ref, c in zip(rest[:n_add], coefs):
            acc = acc + (r_ref[...] if c == 1.0 else c * r_ref[...])
        o_ref[...] = acc.astype(out_dtype)

    b_spec = pl.BlockSpec((tn, k), lambda i, j: (j, 0)) if trans_b else pl.BlockSpec((k, tn), lambda i, j: (0, j))
    in_specs = [pl.BlockSpec((tm, k), lambda i, j: (i, 0)), b_spec]
    in_specs += [pl.BlockSpec((tm, tn), lambda i, j: (i, j)) for _ in addends]
    return _pcall(body, name=name, out_shape=SDS((m, n), out_dtype), grid=(m // tm, n // tn), in_specs=in_specs,
                  out_specs=pl.BlockSpec((tm, tn), lambda i, j: (i, j)), dims=("parallel", "parallel"))(a, b, *addends)


def _mm_tn(a, b, *, name):
    s, m = a.shape
    n = b.shape[1]
    tm = _pick(m, (768, 1408, 1024, 512, 256, 128))
    tk = _pick(s, (512, 256))
    nk = s // tk

    def body(a_ref, b_ref, o_ref, acc_ref):
        kk = pl.program_id(1)

        @pl.when(kk == 0)
        def _():
            acc_ref[...] = jnp.zeros_like(acc_ref)

        acc_ref[...] += lax.dot_general(a_ref[...].astype(BF16), b_ref[...].astype(BF16), (((0,), (0,)), ((), ())),
                                        preferred_element_type=F32)

        @pl.when(kk == nk - 1)
        def _():
            o_ref[...] = acc_ref[...]

    return _pcall(body, name=name, out_shape=SDS((m, n), F32), grid=(m // tm, nk),
                  in_specs=[pl.BlockSpec((tk, tm), lambda i, kk: (kk, i)), pl.BlockSpec((tk, n), lambda i, kk: (kk, 0))],
                  out_specs=pl.BlockSpec((tm, n), lambda i, kk: (i, 0)), scratch_shapes=[pltpu.VMEM((tm, n), F32)],
                  dims=("parallel", "arbitrary"))(a, b)


def _rope_lane_consts():
    lane = jnp.arange(128)
    j = lane % HEAD_DIM
    inv_freq = ROPE_THETA ** (-jnp.arange(0, ROT_DIM, 2, dtype=F32) / ROT_DIM)
    freq = jnp.where(j < ROT_DIM, inv_freq[j % (ROT_DIM // 2)], 0.0).astype(F32)
    lo = (j < ROT_DIM // 2).astype(F32)
    hi = ((j >= ROT_DIM // 2) & (j < ROT_DIM)).astype(F32)
    return jnp.stack([freq, lo, hi] + [jnp.zeros((128,), F32)] * 5)


def _rope_tables(posf):
    s = posf.shape[0]
    tm = _pick(s, (1024, 512))

    def body(p_ref, c_ref, o_ref):
        ang = p_ref[...] * c_ref[0:1, :]
        lo = c_ref[1:2, :]
        hi = c_ref[2:3, :]
        cs = jnp.cos(ang)
        sn = jnp.sin(ang)
        o_ref[...] = jnp.concatenate([jnp.where(lo + hi > 0.0, cs, 1.0), -sn * lo, sn * hi], axis=1)

    return _pcall(body, name="rope_tables", out_shape=SDS((s, TABW), F32), grid=(s // tm,),
                  in_specs=[pl.BlockSpec((tm, 1), lambda i: (i, 0)), pl.BlockSpec((8, 128), lambda i: (0, 0))],
                  out_specs=pl.BlockSpec((tm, TABW), lambda i: (i, 0)), dims=("parallel",))(posf, _rope_lane_consts())


def _rope_apply(x, tab, sign):
    w = x.shape[1]
    rep = w // 128
    c = jnp.tile(tab[:, 0:128], (1, rep)) if rep > 1 else tab[:, 0:128]
    a = jnp.tile(tab[:, 128:256], (1, rep)) if rep > 1 else tab[:, 128:256]
    b = jnp.tile(tab[:, 256:384], (1, rep)) if rep > 1 else tab[:, 256:384]
    up = pltpu.roll(x, w - 8, 1)
    dn = pltpu.roll(x, 8, 1)
    if sign > 0:
        return x * c + up * a + dn * b
    return x * c - up * a - dn * b


def _rope_mask_row():
    col = jnp.arange(ZW)
    m = (col < Z_VB) | ((col >= Z_QA) & (col < Z_VA))
    return m.astype(F32).reshape(1, ZW)


def _proj_rope(h0, w_t, tab):
    s = h0.shape[0]
    tm = _pick(s, (512,))
    tn = 256

    def body(a_ref, w_ref, t_ref, m_ref, o_ref):
        z = lax.dot_general(a_ref[...].astype(BF16), w_ref[...], (((1,), (1,)), ((), ())), preferred_element_type=F32)
        zr = _rope_apply(z, t_ref[...], 1)
        msk = m_ref[...]
        o_ref[...] = (z + msk * (zr - z)).astype(BF16)

    return _pcall(body, name="proj_rope", out_shape=SDS((s, ZW), BF16), grid=(s // tm, IN_WIDTH // tn),
                  in_specs=[pl.BlockSpec((tm, D_MODEL), lambda i, j: (i, 0)), pl.BlockSpec((tn, D_MODEL), lambda i, j: (j, 0)),
                            pl.BlockSpec((tm, TABW), lambda i, j: (i, 0)), pl.BlockSpec((1, tn), lambda i, j: (0, j))],
                  out_specs=pl.BlockSpec((tm, tn), lambda i, j: (i, j)),
                  dims=("parallel", "parallel"))(h0, w_t, tab, _rope_mask_row())


def _band_specs(nb, blk, width, per_tok, cb):
    def mk(off):
        return pl.BlockSpec((blk, width), lambda c, j: (jnp.clip(j + off, 0, nb - 1), c * per_tok + cb))
    return [mk(-1), mk(0), mk(1)]


def _band_masks(j, blk, sd):
    r3 = 3 * blk
    qpos = j * blk + lax.broadcasted_iota(jnp.int32, (blk, r3), 0)
    kpos = (j - 1) * blk + lax.broadcasted_iota(jnp.int32, (blk, r3), 1)
    m1 = (jnp.abs(qpos - kpos) <= blk) & (kpos >= 0) & (kpos < sd)
    qpos2 = (j - 1) * blk + lax.broadcasted_iota(jnp.int32, (r3, blk), 0)
    kpos2 = j * blk + lax.broadcasted_iota(jnp.int32, (r3, blk), 1)
    m2 = (jnp.abs(qpos2 - kpos2) <= blk) & (qpos2 >= 0) & (qpos2 < sd)
    return m1, m2


_NT = (((1,), (1,)), ((), ()))
_NN = (((1,), (0,)), ((), ()))
_TN = (((0,), (0,)), ((), ()))


def _banded_fwd(zr, sink, *, d, blk, qw, kw, qcb, kcb, vcb, heads, name):
    s = zr.shape[0]
    sd = s // d
    nb = sd // blk
    zv = zr.reshape(sd, d * ZW)
    has_sink = sink is not None
    scale = HEAD_DIM ** -0.5

    def body(q_ref, kp, kc, kn, vp, vc, vn, *rest):
        if has_sink:
            sink_ref, o_ref, lse_ref = rest
        else:
            o_ref, lse_ref = rest
        j = pl.program_id(1)
        q = q_ref[...]
        k = jnp.concatenate([kp[...], kc[...], kn[...]], axis=0)
        v = jnp.concatenate([vp[...], vc[...], vn[...]], axis=0)
        mask, _ = _band_masks(j, blk, sd)
        outs, lses = [], []
        for ql, kl, vl, si in heads:
            sc = lax.dot_general(q[:, ql:ql + HEAD_DIM], k[:, kl:kl + HEAD_DIM], _NT, preferred_element_type=F32) * scale
            sc = jnp.where(mask, sc, NEG_INF)
            m = jnp.max(sc, axis=-1, keepdims=True)
            if has_sink:
                m = jnp.maximum(m, sink_ref[0, si])
            p = jnp.where(mask, jnp.exp(sc - m), 0.0)
            den = jnp.sum(p, axis=-1, keepdims=True)
            if has_sink:
                den = den + jnp.exp(sink_ref[0, si] - m)
            o = lax.dot_general(p.astype(BF16), v[:, vl:vl + HEAD_DIM], _NN, preferred_element_type=F32) / den
            outs.append(o)
            lses.append(jnp.broadcast_to(m + jnp.log(den), (blk, HEAD_DIM)))
        o_ref[...] = jnp.concatenate(outs, axis=1)
        lse_ref[...] = jnp.concatenate(lses, axis=1)

    in_specs = ([pl.BlockSpec((blk, qw), lambda c, j: (j, c * (ZW // qw) + qcb))]
                + _band_specs(nb, blk, kw, ZW // kw, kcb) + _band_specs(nb, blk, kw, ZW // kw, vcb))
    args = [zv] * 7
    if has_sink:
        in_specs.append(pl.BlockSpec(memory_space=pltpu.SMEM))
        args.append(sink)
    o_spec = pl.BlockSpec((blk, qw), lambda c, j: (j, c))
    o, lse = _pcall(body, name=name, out_shape=(SDS((sd, d * qw), F32), SDS((sd, d * qw), F32)), grid=(d, nb),
                    in_specs=in_specs, out_specs=(o_spec, o_spec), dims=("parallel", "parallel"))(*args)
    return o.reshape(s, qw), lse.reshape(s, qw)


def _banded_bwd(zr, o, lse, do, tab, sink, *, d, blk, qw, kw, qcb, kcb, vcb, heads, kv_heads, name):
    s = zr.shape[0]
    sd = s // d
    nb = sd // blk
    zv = zr.reshape(sd, d * ZW)
    ov, lv, dov = (t.reshape(sd, d * qw) for t in (o, lse, do))
    tv = tab.reshape(sd, d * TABW)
    has_sink = sink is not None
    scale = HEAD_DIM ** -0.5
    kvw = HEAD_DIM * len(kv_heads)
    b0, b1 = blk, 2 * blk

    def body(*refs):
        q3 = jnp.concatenate([r[...] for r in refs[0:3]], axis=0)
        k3 = jnp.concatenate([r[...] for r in refs[3:6]], axis=0)
        v3 = jnp.concatenate([r[...] for r in refs[6:9]], axis=0)
        o3 = jnp.concatenate([r[...] for r in refs[9:12]], axis=0)
        l3 = jnp.concatenate([r[...] for r in refs[12:15]], axis=0)
        do3 = jnp.concatenate([r[...] for r in refs[15:18]], axis=0)
        t_ref = refs[18]
        if has_sink:
            sink_ref, dq_ref, dkv_ref, dsink_ref = refs[19:]
        else:
            dq_ref, dkv_ref = refs[19:]
        j = pl.program_id(1)
        m1, m2 = _band_masks(j, blk, sd)
        dqs = []
        dks = [None] * len(kv_heads)
        dvs = [None] * len(kv_heads)
        dsink_row = jnp.zeros((1, 128), F32)
        lane = lax.broadcasted_iota(jnp.int32, (1, 128), 1)
        for ql, kl, vl, si in heads:
            kvi = kv_heads.index((kl, vl))
            qh3 = q3[:, ql:ql + HEAD_DIM]
            kh3 = k3[:, kl:kl + HEAD_DIM]
            vh3 = v3[:, vl:vl + HEAD_DIM]
            doh3 = do3[:, ql:ql + HEAD_DIM]
            delta3 = jnp.sum(doh3 * o3[:, ql:ql + HEAD_DIM], axis=-1, keepdims=True)
            lse3 = l3[:, ql:ql + 1]
            dob3 = doh3.astype(BF16)
            sc = lax.dot_general(qh3[b0:b1], kh3, _NT, preferred_element_type=F32) * scale
            p = jnp.where(m1, jnp.exp(sc - lse3[b0:b1]), 0.0)
            dp = lax.dot_general(dob3[b0:b1], vh3, _NT, preferred_element_type=F32)
            ds = p * (dp - delta3[b0:b1])
            dqs.append(lax.dot_general(ds.astype(BF16), kh3, _NN, preferred_element_type=F32) * scale)
            if has_sink:
                psink = jnp.exp(sink_ref[0, si] - lse3[b0:b1])
                dsink_row = dsink_row + jnp.where(lane == si, -jnp.sum(psink * delta3[b0:b1]), 0.0)
            sc2 = lax.dot_general(qh3, kh3[b0:b1], _NT, preferred_element_type=F32) * scale
            p2 = jnp.where(m2, jnp.exp(sc2 - lse3), 0.0)
            dv = lax.dot_general(p2.astype(BF16), dob3, _TN, preferred_element_type=F32)
            dp2 = lax.dot_general(dob3, vh3[b0:b1], _NT, preferred_element_type=F32)
            ds2 = p2 * (dp2 - delta3)
            dk = lax.dot_general(ds2.astype(BF16), qh3, _TN, preferred_element_type=F32) * scale
            dks[kvi] = dk if dks[kvi] is None else dks[kvi] + dk
            dvs[kvi] = dv if dvs[kvi] is None else dvs[kvi] + dv
        tabv = t_ref[...]
        dq_ref[...] = _rope_apply(jnp.concatenate(dqs, axis=1), tabv, -1).astype(BF16)
        dk_all = jnp.concatenate(dks, axis=1) if len(dks) > 1 else dks[0]
        dv_all = jnp.concatenate(dvs, axis=1) if len(dvs) > 1 else dvs[0]
        dkv_ref[...] = jnp.concatenate([_rope_apply(dk_all, tabv, -1), dv_all], axis=1).astype(BF16)
        if has_sink:
            first = (pl.program_id(0) == 0) & (j == 0)

            @pl.when(first)
            def _():
                dsink_ref[...] = jnp.zeros_like(dsink_ref)

            dsink_ref[0:1, :] += dsink_row

    in_specs = (_band_specs(nb, blk, qw, ZW // qw, qcb) + _band_specs(nb, blk, kw, ZW // kw, kcb)
                + _band_specs(nb, blk, kw, ZW // kw, vcb) + _band_specs(nb, blk, qw, 1, 0) * 3
                + [pl.BlockSpec((blk, TABW), lambda c, j: (j, c))])
    args = [zv] * 9 + [ov] * 3 + [lv] * 3 + [dov] * 3 + [tv]
    out_shape = [SDS((sd, d * qw), BF16), SDS((sd, d * 2 * kvw), BF16)]
    out_specs = [pl.BlockSpec((blk, qw), lambda c, j: (j, c)), pl.BlockSpec((blk, 2 * kvw), lambda c, j: (j, c))]
    dims = ("parallel", "parallel")
    if has_sink:
        in_specs.append(pl.BlockSpec(memory_space=pltpu.SMEM))
        args.append(sink)
        out_shape.append(SDS((8, 128), F32))
        out_specs.append(pl.BlockSpec((8, 128), lambda c, j: (0, 0)))
        dims = ("arbitrary", "arbitrary")
    res = _pcall(body, name=name, out_shape=tuple(out_shape), grid=(d, nb), in_specs=in_specs,
                 out_specs=tuple(out_specs), dims=dims)(*args)
    dq = res[0].reshape(s, qw)
    dkv = res[1].reshape(s, 2 * kvw)
    return (dq, dkv, res[2]) if has_sink else (dq, dkv)


_WIN_HEADS = tuple((h * HEAD_DIM, (h // 4) * HEAD_DIM, 128 + (h // 4) * HEAD_DIM, h) for h in range(8))
_WIN_KV = ((0, 128), (64, 192))
_WIN_CFG = dict(d=1, blk=WIN_HALF, qw=512, kw=256, qcb=Z_QA // 512, kcb=Z_KA // 256, vcb=Z_KA // 256, heads=_WIN_HEADS)
_DIL_HEADS = tuple((h * HEAD_DIM, h * HEAD_DIM, h * HEAD_DIM, h) for h in range(8))
_DIL_KV = tuple((h * HEAD_DIM, h * HEAD_DIM) for h in range(8))


def _dil_cfg(gi):
    return dict(d=DIL_PAIRS[gi][1], blk=DIL_SIDE, qw=512, kw=512, qcb=Z_QB // 512 + gi, kcb=Z_KB // 512 + gi,
                vcb=Z_VB // 512 + gi, heads=_DIL_HEADS)


def _mix_norm_fwd(oa, ogs, lgs, g_win, g_dil):
    s = oa.shape[0]
    tm = _pick(s, (512,))

    def body(oa_ref, o0, o1, o2, l0, l1, l2, gw_ref, gd_ref, mixed_ref, ob_ref, lb_ref):
        la, lb, lc = l0[...], l1[...], l2[...]
        mx = jnp.maximum(jnp.maximum(la, lb), lc)
        ea, eb, ec = jnp.exp(la - mx), jnp.exp(lb - mx), jnp.exp(lc - mx)
        den = ea + eb + ec
        ob = (ea / den) * o0[...] + (eb / den) * o1[...] + (ec / den) * o2[...]
        ob_ref[...] = ob
        lb_ref[...] = mx + jnp.log(den)
        a = oa_ref[...]
        ra = lax.rsqrt(jnp.mean(a * a, axis=-1, keepdims=True) + LN_EPS)
        rb = lax.rsqrt(jnp.mean(ob * ob, axis=-1, keepdims=True) + LN_EPS)
        mixed_ref[...] = jnp.concatenate([a * ra * gw_ref[...], ob * rb * gd_ref[...]], axis=1).astype(BF16)

    row = pl.BlockSpec((tm, 512), lambda i: (i, 0))
    vec = pl.BlockSpec((1, 512), lambda i: (0, 0))
    return _pcall(body, name="mix_norm_fwd", out_shape=(SDS((s, 1024), BF16), SDS((s, 512), F32), SDS((s, 512), F32)),
                  grid=(s // tm,), in_specs=[row] * 7 + [vec, vec],
                  out_specs=(pl.BlockSpec((tm, 1024), lambda i: (i, 0)), row, row),
                  dims=("parallel",))(oa, *ogs, *lgs, g_win, g_dil)


def _mix_norm_bwd(oa, ob, dmixed, g_win, g_dil):
    s = oa.shape[0]
    tm = _pick(s, (512,))
    nt = s // tm

    def body(oa_ref, ob_ref, dm_ref, gw_ref, gd_ref, doa_ref, dob_ref, dgw_ref, dgd_ref, acc_w, acc_d):
        i = pl.program_id(0)

        @pl.when(i == 0)
        def _():
            acc_w[...] = jnp.zeros_like(acc_w)
            acc_d[...] = jnp.zeros_like(acc_d)

        dm = dm_ref[...]
        for x_ref, g_ref, dy, dx_ref, acc in ((oa_ref, gw_ref, dm[:, :512], doa_ref, acc_w),
                                              (ob_ref, gd_ref, dm[:, 512:], dob_ref, acc_d)):
            x = x_ref[...]
            r = lax.rsqrt(jnp.mean(x * x, axis=-1, keepdims=True) + LN_EPS)
            dyg = dy * g_ref[...]
            dx_ref[...] = r * dyg - x * (r * r * r) * jnp.mean(dyg * x, axis=-1, keepdims=True)
            acc[...] += jnp.sum((dy * x * r).reshape(tm // 8, 8, 512), axis=0)

        @pl.when(i == nt - 1)
        def _():
            dgw_ref[...] = jnp.sum(acc_w[...], axis=0, keepdims=True)
            dgd_ref[...] = jnp.sum(acc_d[...], axis=0, keepdims=True)

    row = pl.BlockSpec((tm, 512), lambda i: (i, 0))
    vec = pl.BlockSpec((1, 512), lambda i: (0, 0))
    return _pcall(body, name="mix_norm_bwd",
                  out_shape=(SDS((s, 512), F32), SDS((s, 512), F32), SDS((1, 512), F32), SDS((1, 512), F32)),
                  grid=(nt,), in_specs=[row, row, pl.BlockSpec((tm, 1024), lambda i: (i, 0)), vec, vec],
                  out_specs=(row, row, vec, vec), scratch_shapes=[pltpu.VMEM((8, 512), F32), pltpu.VMEM((8, 512), F32)],
                  dims=("arbitrary",))(oa, ob, dmixed, g_win, g_dil)


def _ln_fwd(a, r, g, b, ca, name):
    s = a.shape[0]
    tm = _pick(s, (512, 256))
    has_r = r is not None

    def body(*refs):
        a_ref = refs[0]
        r_ref = refs[1] if has_r else None
        g_ref, b_ref, o_ref = refs[1 + has_r:]
        u = a_ref[...] if ca == 1.0 else ca * a_ref[...]
        if has_r:
            u = u + r_ref[...]
        mu = jnp.mean(u, axis=-1, keepdims=True)
        xc = u - mu
        var = jnp.mean(xc * xc, axis=-1, keepdims=True)
        o_ref[...] = xc * lax.rsqrt(var + LN_EPS) * g_ref[...] + b_ref[...]

    row = pl.BlockSpec((tm, D_MODEL), lambda i: (i, 0))
    vec = pl.BlockSpec((1, D_MODEL), lambda i: (0, 0))
    args = [a] + ([r] if has_r else []) + [g, b]
    return _pcall(body, name=name, out_shape=SDS((s, D_MODEL), F32), grid=(s // tm,),
                  in_specs=[row] * (1 + has_r) + [vec, vec], out_specs=row, dims=("parallel",))(*args)


def _ln_bwd(a, r, dy, g, b, ca, name, loss_mode=False):
    s = a.shape[0]
    tm = _pick(s, (512, 256))
    nt = s // tm
    has_r = r is not None

    def body(*refs):
        a_ref = refs[0]
        r_ref = refs[1] if has_r else None
        dy_ref, g_ref, b_ref = refs[1 + has_r:4 + has_r]
        outs = refs[4 + has_r:]
        if loss_mode:
            du_ref, dg_ref, db_ref, loss_ref, acc_g, acc_b, acc_l = outs
        else:
            du_ref, dg_ref, db_ref, acc_g, acc_b = outs
        i = pl.program_id(0)

        @pl.when(i == 0)
        def _():
            acc_g[...] = jnp.zeros_like(acc_g)
            acc_b[...] = jnp.zeros_like(acc_b)
            if loss_mode:
                acc_l[...] = jnp.zeros_like(acc_l)

        u = a_ref[...] if ca == 1.0 else ca * a_ref[...]
        if has_r:
            u = u + r_ref[...]
        mu = jnp.mean(u, axis=-1, keepdims=True)
        xc = u - mu
        var = jnp.mean(xc * xc, axis=-1, keepdims=True)
        rstd = lax.rsqrt(var + LN_EPS)
        xhat = xc * rstd
        gv = g_ref[...]
        if loss_mode:
            err = (xhat * gv + b_ref[...]) - dy_ref[...]
            acc_l[...] += jnp.sum((err * err).reshape(tm // 8, 8, D_MODEL), axis=0)
            dyv = err * (1.0 / D_MODEL)
        else:
            dyv = dy_ref[...]
        dxh = dyv * gv
        du_ref[...] = rstd * (dxh - jnp.mean(dxh, axis=-1, keepdims=True)
                              - xhat * jnp.mean(dxh * xhat, axis=-1, keepdims=True))
        acc_g[...] += jnp.sum((dyv * xhat).reshape(tm // 8, 8, D_MODEL), axis=0)
        acc_b[...] += jnp.sum(dyv.reshape(tm // 8, 8, D_MODEL), axis=0)

        @pl.when(i == nt - 1)
        def _():
            dg_ref[...] = jnp.sum(acc_g[...], axis=0, keepdims=True)
            db_ref[...] = jnp.sum(acc_b[...], axis=0, keepdims=True)
            if loss_mode:
                tot = jnp.sum(jnp.sum(acc_l[...], axis=0, keepdims=True), axis=1, keepdims=True)
                loss_ref[...] = tot * (0.5 / D_MODEL)

    row = pl.BlockSpec((tm, D_MODEL), lambda i: (i, 0))
    vec = pl.BlockSpec((1, D_MODEL), lambda i: (0, 0))
    out_shape = [SDS((s, D_MODEL), F32), SDS((1, D_MODEL), F32), SDS((1, D_MODEL), F32)]
    out_specs = [row, vec, vec]
    scratch = [pltpu.VMEM((8, D_MODEL), F32), pltpu.VMEM((8, D_MODEL), F32)]
    if loss_mode:
        out_shape.append(SDS((1, 1), F32))
        out_specs.append(pl.BlockSpec((1, 1), lambda i: (0, 0)))
        scratch.append(pltpu.VMEM((8, D_MODEL), F32))
    args = [a] + ([r] if has_r else []) + [dy, g, b]
    return _pcall(body, name=name, out_shape=tuple(out_shape), grid=(nt,), in_specs=[row] * (2 + has_r) + [vec, vec],
                  out_specs=tuple(out_specs), scratch_shapes=scratch, dims=("arbitrary",))(*args)


def _xattn_fwd(q, k, v):
    s = q.shape[0]
    tq = _pick(s, (512,))
    scale = X_HEAD_DIM ** -0.5

    def body(q_ref, k_ref, v_ref, o_ref):
        qv, kv, vv = q_ref[...], k_ref[...], v_ref[...]
        outs = []
        for h in range(X_HEADS):
            sl = slice(h * X_HEAD_DIM, (h + 1) * X_HEAD_DIM)
            sc = lax.dot_general(qv[:, sl], kv[:, sl], _NT, preferred_element_type=F32) * scale
            e = jnp.exp(sc - jnp.max(sc, axis=-1, keepdims=True))
            p = e / jnp.sum(e, axis=-1, keepdims=True)
            outs.append(lax.dot_general(p.astype(BF16), vv[:, sl], _NN, preferred_element_type=F32))
        o_ref[...] = jnp.concatenate(outs, axis=1)

    row = pl.BlockSpec((tq, D_MODEL), lambda i: (i, 0))
    full = pl.BlockSpec((MEM_LEN, D_MODEL), lambda i: (0, 0))
    return _pcall(body, name="xattn_fwd", out_shape=SDS((s, D_MODEL), F32), grid=(s // tq,), in_specs=[row, full, full],
                  out_specs=row, dims=("parallel",))(q, k, v)


def _xattn_bwd(q, k, v, o, do):
    s = q.shape[0]
    tq = _pick(s, (512,))
    scale = X_HEAD_DIM ** -0.5

    def body(q_ref, k_ref, v_ref, o_ref, do_ref, dq_ref, dk_ref, dv_ref):
        i = pl.program_id(0)

        @pl.when(i == 0)
        def _():
            dk_ref[...] = jnp.zeros_like(dk_ref)
            dv_ref[...] = jnp.zeros_like(dv_ref)

        qv, kv, vv, ov, dov = q_ref[...], k_ref[...], v_ref[...], o_ref[...], do_ref[...]
        dqs, dks, dvs = [], [], []
        for h in range(X_HEADS):
            sl = slice(h * X_HEAD_DIM, (h + 1) * X_HEAD_DIM)
            sc = lax.dot_general(qv[:, sl], kv[:, sl], _NT, preferred_element_type=F32) * scale
            e = jnp.exp(sc - jnp.max(sc, axis=-1, keepdims=True))
            p = e / jnp.sum(e, axis=-1, keepdims=True)
            doh = dov[:, sl]
            dob = doh.astype(BF16)
            delta = jnp.sum(doh * ov[:, sl], axis=-1, keepdims=True)
            dvs.append(lax.dot_general(p.astype(BF16), dob, _TN, preferred_element_type=F32))
            dp = lax.dot_general(dob, vv[:, sl], _NT, preferred_element_type=F32)
            ds = (p * (dp - delta)).astype(BF16)
            dqs.append(lax.dot_general(ds, kv[:, sl], _NN, preferred_element_type=F32) * scale)
            dks.append(lax.dot_general(ds, qv[:, sl], _TN, preferred_element_type=F32) * scale)
        dq_ref[...] = jnp.concatenate(dqs, axis=1).astype(BF16)
        dk_ref[...] += jnp.concatenate(dks, axis=1)
        dv_ref[...] += jnp.concatenate(dvs, axis=1)

    row = pl.BlockSpec((tq, D_MODEL), lambda i: (i, 0))
    full = pl.BlockSpec((MEM_LEN, D_MODEL), lambda i: (0, 0))
    return _pcall(body, name="xattn_bwd",
                  out_shape=(SDS((s, D_MODEL), BF16), SDS((MEM_LEN, D_MODEL), F32), SDS((MEM_LEN, D_MODEL), F32)),
                  grid=(s // tq,), in_specs=[row, full, full, row, row], out_specs=(row, full, full),
                  dims=("arbitrary",))(q, k, v, o, do)


_SQRT_HALF = 0.7071067811865476
_INV_SQRT_2PI = 0.3989422804014327


def _halo_specs(s, tm, width):
    n8 = s // 8
    r8 = tm // 8
    prev = pl.BlockSpec((8, width), lambda i: (jnp.maximum(i * r8 - 1, 0), 0))
    nxt = pl.BlockSpec((8, width), lambda i: (jnp.minimum((i + 1) * r8, n8 - 1), 0))
    return prev, nxt


def _shifted(x, prev8, next8, i, nt):
    tm = x.shape[0]
    row = lax.broadcasted_iota(jnp.int32, x.shape, 0)
    first = jnp.where(i == 0, 0.0, 1.0) * prev8[7:8, :]
    last = jnp.where(i == nt - 1, 0.0, 1.0) * next8[0:1, :]
    xm1 = jnp.where(row == 0, first, pltpu.roll(x, 1, 0))
    xp1 = jnp.where(row == tm - 1, last, pltpu.roll(x, tm - 1, 0))
    return xm1, xp1


def _glu_fwd(g, up, cw, cb):
    s = g.shape[0]
    tm = _pick(s, (256,))
    nt = s // tm

    def body(g_ref, gp_ref, gn_ref, up_ref, cw_ref, cb_ref, act_ref):
        i = pl.program_id(0)
        gv = g_ref[...]
        gm1, gp1 = _shifted(gv, gp_ref[...], gn_ref[...], i, nt)
        gc = gm1 * cw_ref[0:1, :] + gv * cw_ref[1:2, :] + gp1 * cw_ref[2:3, :] + cb_ref[...]
        gelu = 0.5 * gc * (1.0 + lax.erf(gc * _SQRT_HALF))
        act_ref[...] = (gelu * up_ref[...]).astype(BF16)

    row = pl.BlockSpec((tm, D_FF), lambda i: (i, 0))
    prev, nxt = _halo_specs(s, tm, D_FF)
    return _pcall(body, name="glu_fwd", out_shape=SDS((s, D_FF), BF16), grid=(nt,),
                  in_specs=[row, prev, nxt, row, pl.BlockSpec((8, D_FF), lambda i: (0, 0)),
                            pl.BlockSpec((1, D_FF), lambda i: (0, 0))],
                  out_specs=row, dims=("parallel",))(g, g, g, up, cw, cb)


def _glu_bwd(g, up, dact, cw, cb):
    s = g.shape[0]
    tm = _pick(s, (256,))
    nt = s // tm

    def body(g_ref, gp_ref, gn_ref, up_ref, da_ref, cw_ref, cb_ref, dgc_ref, dup_ref, dcw_ref, dcb_ref, a0, a1, a2, a3):
        i = pl.program_id(0)

        @pl.when(i == 0)
        def _():
            for a in (a0, a1, a2, a3):
                a[...] = jnp.zeros_like(a)

        gv = g_ref[...]
        gm1, gp1 = _shifted(gv, gp_ref[...], gn_ref[...], i, nt)
        gc = gm1 * cw_ref[0:1, :] + gv * cw_ref[1:2, :] + gp1 * cw_ref[2:3, :] + cb_ref[...]
        cdf = 0.5 * (1.0 + lax.erf(gc * _SQRT_HALF))
        pdf = jnp.exp(-0.5 * gc * gc) * _INV_SQRT_2PI
        da = da_ref[...]
        dup_ref[...] = (da * (gc * cdf)).astype(BF16)
        dgc = da * up_ref[...] * (cdf + gc * pdf)
        dgc_ref[...] = dgc

        def fold(t):
            return jnp.sum(t.reshape(tm // 8, 8, D_FF), axis=0)

        a0[...] += fold(dgc * gm1)
        a1[...] += fold(dgc * gv)
        a2[...] += fold(dgc * gp1)
        a3[...] += fold(dgc)

        @pl.when(i == nt - 1)
        def _():
            dcw_ref[...] = jnp.concatenate(
                [jnp.sum(a[...], axis=0, keepdims=True) for a in (a0, a1, a2)] + [jnp.zeros((5, D_FF), F32)], axis=0)
            dcb_ref[...] = jnp.sum(a3[...], axis=0, keepdims=True)

    row = pl.BlockSpec((tm, D_FF), lambda i: (i, 0))
    prev, nxt = _halo_specs(s, tm, D_FF)
    cw_spec = pl.BlockSpec((8, D_FF), lambda i: (0, 0))
    cb_spec = pl.BlockSpec((1, D_FF), lambda i: (0, 0))
    return _pcall(body, name="glu_bwd",
                  out_shape=(SDS((s, D_FF), F32), SDS((s, D_FF), BF16), SDS((8, D_FF), F32), SDS((1, D_FF), F32)),
                  grid=(nt,), in_specs=[row, prev, nxt, row, row, cw_spec, cb_spec],
                  out_specs=(row, row, cw_spec, cb_spec), scratch_shapes=[pltpu.VMEM((8, D_FF), F32)] * 4,
                  dims=("arbitrary",))(g, g, g, up, dact, cw, cb)


def _conv_bwd_input(dgc, cw):
    s = dgc.shape[0]
    tm = _pick(s, (256,))
    nt = s // tm

    def body(x_ref, xp_ref, xn_ref, cw_ref, o_ref):
        i = pl.program_id(0)
        xv = x_ref[...]
        xm1, xp1 = _shifted(xv, xp_ref[...], xn_ref[...], i, nt)
        o_ref[...] = (xp1 * cw_ref[0:1, :] + xv * cw_ref[1:2, :] + xm1 * cw_ref[2:3, :]).astype(BF16)

    row = pl.BlockSpec((tm, D_FF), lambda i: (i, 0))
    prev, nxt = _halo_specs(s, tm, D_FF)
    return _pcall(body, name="conv_bwd_input", out_shape=SDS((s, D_FF), BF16), grid=(nt,),
                  in_specs=[row, prev, nxt, pl.BlockSpec((8, D_FF), lambda i: (0, 0))], out_specs=row,
                  dims=("parallel",))(dgc, dgc, dgc, cw)


def _adamw(w, g, m, v, name):
    rows, cols = w.shape
    tr = _pick(rows, (256, 128, 64, 32, 16, 8))
    c1 = 1.0 - ADAM_B1 ** ADAM_STEP
    c2 = 1.0 - ADAM_B2 ** ADAM_STEP

    def body(w_ref, g_ref, m_ref, v_ref, d_ref, nm_ref, nv_ref):
        gv = g_ref[...]
        nm = ADAM_B1 * m_ref[...] + (1.0 - ADAM_B1) * gv
        nv = ADAM_B2 * v_ref[...] + (1.0 - ADAM_B2) * (gv * gv)
        d_ref[...] = -ADAM_LR * ((nm / c1) / (jnp.sqrt(nv / c2) + ADAM_EPS) + ADAM_WD * w_ref[...])
        nm_ref[...] = nm
        nv_ref[...] = nv

    blk = pl.BlockSpec((tr, cols), lambda i: (i, 0))
    return _pcall(body, name=name, out_shape=(SDS(w.shape, F32),) * 3, grid=(rows // tr,), in_specs=[blk] * 4,
                  out_specs=(blk,) * 3, dims=("parallel",))(w, g, m, v)


def _all_gather_rows(x_shard, *, name, in_vmem, sum_rows=False):
    m_per, n = x_shard.shape

    def body(x_ref, out_ref, *rest):
        if sum_rows:
            sum_ref, send_sems, recv_sems, local_sem = rest
        else:
            send_sems, recv_sems, local_sem = rest
        x, y, c = lax.axis_index("x"), lax.axis_index("y"), lax.axis_index("c")
        me, sibling = (x, y, c), (x, y, 1 - c)
        chips = [(1 - x, y), (x, 1 - y), (1 - x, 1 - y)]

        def rows(px, py, pc):
            return out_ref.at[pl.ds((4 * px + 2 * py + pc) * m_per, m_per), :]

        def copy(k, block, to, src=None):
            return pltpu.make_async_remote_copy(
                src_ref=rows(*block) if src is None else src, dst_ref=rows(*block), send_sem=send_sems.at[k],
                recv_sem=recv_sems.at[k], device_id=to, device_id_type=pl.DeviceIdType.MESH)

        mine = pltpu.make_async_copy(x_ref, rows(*me), local_sem)
        mine.start()
        first = [copy(0, me, sibling, src=x_ref)]
        first += [copy(1 + j, me, (*chip, c), src=x_ref) for j, chip in enumerate(chips)]
        for cp in first:
            cp.start()
        passed = [copy(4 + j, (*chip, c), sibling) for j, chip in enumerate(chips)]
        for j, chip in enumerate(chips):
            copy(1 + j, (*chip, c), me).wait_recv()
            passed[j].start()
        copy(0, sibling, me).wait_recv()
        for j, chip in enumerate(chips):
            copy(4 + j, (*chip, 1 - c), me).wait_recv()
        for cp in first + passed:
            cp.wait_send()
        mine.wait()
        if sum_rows:
            acc = out_ref[0:m_per, :]
            for dev in range(1, N_DEV):
                acc = acc + out_ref[dev * m_per:(dev + 1) * m_per, :]
            sum_ref[...] = acc

    space = pltpu.VMEM if in_vmem else pl.ANY
    out_shape = [SDS((N_DEV * m_per, n), x_shard.dtype)]
    out_specs = [pl.BlockSpec(memory_space=space)]
    if sum_rows:
        out_shape.append(SDS((m_per, n), x_shard.dtype))
        out_specs.append(pl.BlockSpec(memory_space=pltpu.VMEM))
    res = _PALLAS_CALL(
        body, name=name, out_shape=tuple(out_shape), in_specs=[pl.BlockSpec(memory_space=space)],
        out_specs=tuple(out_specs),
        scratch_shapes=[pltpu.SemaphoreType.DMA((7,)), pltpu.SemaphoreType.DMA((7,)), pltpu.SemaphoreType.DMA],
        compiler_params=pltpu.CompilerParams(vmem_limit_bytes=VMEM_LIMIT_BYTES),
    )(x_shard)
    return res if sum_rows else res[0]


def _reduce_scatter_exchange(gp):
    _, r, n = gp.shape

    def body(g_ref, recv_ref, send_sems, recv_sems, local_sem):
        x, y, c = lax.axis_index("x"), lax.axis_index("y"), lax.axis_index("c")
        me_lin = 4 * x + 2 * y + c
        mine = pltpu.make_async_copy(g_ref.at[me_lin], recv_ref.at[0], local_sem)
        mine.start()
        copies = []
        for k in range(1, N_DEV):
            px = 1 - x if (k >> 2) & 1 else x
            py = 1 - y if (k >> 1) & 1 else y
            pc = 1 - c if k & 1 else c
            cp = pltpu.make_async_remote_copy(
                src_ref=g_ref.at[4 * px + 2 * py + pc], dst_ref=recv_ref.at[k], send_sem=send_sems.at[k - 1],
                recv_sem=recv_sems.at[k - 1], device_id=(px, py, pc), device_id_type=pl.DeviceIdType.MESH)
            cp.start()
            copies.append(cp)
        for cp in copies:
            cp.wait_recv()
        for cp in copies:
            cp.wait_send()
        mine.wait()

    return _PALLAS_CALL(
        body, name="grad_exchange", out_shape=SDS(gp.shape, gp.dtype), in_specs=[pl.BlockSpec(memory_space=pl.ANY)],
        out_specs=pl.BlockSpec(memory_space=pl.ANY),
        scratch_shapes=[pltpu.SemaphoreType.DMA((7,)), pltpu.SemaphoreType.DMA((7,)), pltpu.SemaphoreType.DMA],
        compiler_params=pltpu.CompilerParams(vmem_limit_bytes=VMEM_LIMIT_BYTES),
    )(gp)


def _sum_slots(recv):
    _, r, n = recv.shape
    tr = _pick(r, (296, 128, 64, 32, 16, 8))

    def body(x_ref, o_ref):
        acc = x_ref[0]
        for k in range(1, N_DEV):
            acc = acc + x_ref[k]
        o_ref[...] = acc

    return _pcall(body, name="grad_sum", out_shape=SDS((r, n), F32), grid=(r // tr,),
                  in_specs=[pl.BlockSpec((N_DEV, tr, n), lambda i: (0, i, 0))],
                  out_specs=pl.BlockSpec((tr, n), lambda i: (i, 0)), dims=("parallel",))(recv)


def _pad_rows(a, rows):
    return jnp.pad(a, ((0, rows - a.shape[0]), (0, 0)))


def kernel(x, mem, positions, ln_in_g, ln_in_b, w_in, attn_sink, g_win, g_dil, w_mix_out, ln1_g, ln1_b, mem_ln_g, mem_ln_b, w_xq, w_xk, w_xv, w_xo, ln2_g, ln2_b, w_gate, w_up, conv_w, conv_b, w_down, ln3_g, ln3_b, loss_target, m_ln_in_g, m_ln_in_b, m_w_in, m_attn_sink, m_g_win, m_g_dil, m_w_mix_out, m_ln1_g, m_ln1_b, m_mem_ln_g, m_mem_ln_b, m_w_xq, m_w_xk, m_w_xv, m_w_xo, m_ln2_g, m_ln2_b, m_w_gate, m_w_up, m_conv_w, m_conv_b, m_w_down, m_ln3_g, m_ln3_b, v_ln_in_g, v_ln_in_b, v_w_in, v_attn_sink, v_g_win, v_g_dil, v_w_mix_out, v_ln1_g, v_ln1_b, v_mem_ln_g, v_mem_ln_b, v_w_xq, v_w_xk, v_w_xv, v_w_xo, v_ln2_g, v_ln2_b, v_w_gate, v_w_up, v_conv_w, v_conv_b, v_w_down, v_ln3_g, v_ln3_b):
    weights = dict(ln_in_g=ln_in_g, ln_in_b=ln_in_b, w_in=w_in, attn_sink=attn_sink, g_win=g_win, g_dil=g_dil, w_mix_out=w_mix_out, ln1_g=ln1_g, ln1_b=ln1_b, mem_ln_g=mem_ln_g, mem_ln_b=mem_ln_b, w_xq=w_xq, w_xk=w_xk, w_xv=w_xv, w_xo=w_xo, ln2_g=ln2_g, ln2_b=ln2_b, w_gate=w_gate, w_up=w_up, conv_w=conv_w, conv_b=conv_b, w_down=w_down, ln3_g=ln3_g, ln3_b=ln3_b)
    mom_m = dict(ln_in_g=m_ln_in_g, ln_in_b=m_ln_in_b, w_in=m_w_in, attn_sink=m_attn_sink, g_win=m_g_win, g_dil=m_g_dil, w_mix_out=m_w_mix_out, ln1_g=m_ln1_g, ln1_b=m_ln1_b, mem_ln_g=m_mem_ln_g, mem_ln_b=m_mem_ln_b, w_xq=m_w_xq, w_xk=m_w_xk, w_xv=m_w_xv, w_xo=m_w_xo, ln2_g=m_ln2_g, ln2_b=m_ln2_b, w_gate=m_w_gate, w_up=m_w_up, conv_w=m_conv_w, conv_b=m_conv_b, w_down=m_w_down, ln3_g=m_ln3_g, ln3_b=m_ln3_b)
    mom_v = dict(ln_in_g=v_ln_in_g, ln_in_b=v_ln_in_b, w_in=v_w_in, attn_sink=v_attn_sink, g_win=v_g_win, g_dil=v_g_dil, w_mix_out=v_w_mix_out, ln1_g=v_ln1_g, ln1_b=v_ln1_b, mem_ln_g=v_mem_ln_g, mem_ln_b=v_mem_ln_b, w_xq=v_w_xq, w_xk=v_w_xk, w_xv=v_w_xv, w_xo=v_w_xo, ln2_g=v_ln2_g, ln2_b=v_ln2_b, w_gate=v_w_gate, w_up=v_w_up, conv_w=v_conv_w, conv_b=v_conv_b, w_down=v_w_down, ln3_g=v_ln3_g, ln3_b=v_ln3_b)
    order = list(weights)
    s = x.shape[1]
    xs = x[0]
    mems = mem[0]
    target = loss_target[0]
    row = lambda a: a.reshape(1, -1)

    shard_rows = dict(w_in=w_in[0].T, w_gate=w_gate[0].T, w_up=w_up[0].T, w_mix_out=w_mix_out[0], w_xq=w_xq[0],
                      w_xk=w_xk[0], w_xv=w_xv[0], w_xo=w_xo[0], w_down=w_down[0])
    packed = jnp.concatenate([shard_rows[n].astype(BF16) for n, _ in PACK_ROWS], axis=0)
    gathered = _all_gather_rows(packed, name="weight_all_gather", in_vmem=False).reshape(N_DEV, PACK_R, D_MODEL)
    full = {}
    off = 0
    for n, r in PACK_ROWS:
        full[n] = gathered[:, off:off + r, :].reshape(N_DEV * r, D_MODEL)
        off += r
    w_in_t = jnp.concatenate([full["w_in"][768:], full["w_in"][:768]], axis=0)
    cw_pad = jnp.pad(conv_w[0], ((0, 5), (0, 32)))
    cw_all = _all_gather_rows(cw_pad, name="conv_w_all_gather", in_vmem=True).reshape(N_DEV, 8, 384)
    cw_full = jnp.transpose(cw_all[:, :3, :352], (1, 0, 2)).reshape(3, D_FF)
    cw8 = _pad_rows(cw_full, 8)

    tab = _rope_tables(positions.astype(F32).reshape(s, 1))
    h0 = _ln_fwd(xs, None, row(ln_in_g), row(ln_in_b), 1.0, "ln_in_fwd")
    zr = _proj_rope(h0, w_in_t, tab)
    oa, lse_a = _banded_fwd(zr, attn_sink, name="win_attn_fwd", **_WIN_CFG)
    ogs, lgs = [], []
    for gi in range(3):
        o_g, l_g = _banded_fwd(zr, None, name=f"dil_attn_fwd{gi}", **_dil_cfg(gi))
        ogs.append(o_g)
        lgs.append(l_g)
    mixed, ob, lse_b = _mix_norm_fwd(oa, ogs, lgs, g_win, g_dil)
    mix = _mm(mixed, full["w_mix_out"], trans_b=False, out_dtype=F32, name="mm_mix_out")
    h1 = _ln_fwd(h0, mix, ln1_g, ln1_b, ALPHA, "ln1_fwd")
    mem_n = _ln_fwd(mems, None, mem_ln_g, mem_ln_b, 1.0, "mem_ln_fwd")
    kx = _mm(mem_n, full["w_xk"], trans_b=False, out_dtype=BF16, name="mm_xk")
    vx = _mm(mem_n, full["w_xv"], trans_b=False, out_dtype=BF16, name="mm_xv")
    qx = _mm(h1, full["w_xq"], trans_b=False, out_dtype=BF16, name="mm_xq")
    ox = _xattn_fwd(qx, kx, vx)
    xa = _mm(ox, full["w_xo"], trans_b=False, out_dtype=F32, name="mm_xo")
    h2 = _ln_fwd(h1, xa, ln2_g, ln2_b, ALPHA, "ln2_fwd")
    gate = _mm(h2, full["w_gate"], trans_b=True, out_dtype=F32, name="mm_gate")
    up = _mm(h2, full["w_up"], trans_b=True, out_dtype=F32, name="mm_up")
    act = _glu_fwd(gate, up, cw8, conv_b)
    ff = _mm(act, full["w_down"], trans_b=False, out_dtype=F32, name="mm_down")

    du3, d_ln3_g, d_ln3_b, loss_local = _ln_bwd(h2, ff, target, ln3_g, ln3_b, ALPHA, "ln3_bwd_loss", loss_mode=True)
    dact = _mm(du3, full["w_down"], trans_b=True, out_dtype=F32, name="mm_d_act")
    dw_down = _mm_tn(act, du3, name="mm_dw_down")
    dgc, dup, dcw8, d_conv_b = _glu_bwd(gate, up, dact, cw8, conv_b)
    dgate = _conv_bwd_input(dgc, cw8)
    dh2 = _mm(dgate, full["w_gate"], trans_b=False, out_dtype=F32, name="mm_dh2_gate", addends=(du3,), coefs=(ALPHA,))
    dh2 = _mm(dup, full["w_up"], trans_b=False, out_dtype=F32, name="mm_dh2_up", addends=(dh2,), coefs=(1.0,))
    dw_gate_t = _mm_tn(dgate, h2, name="mm_dw_gate")
    dw_up_t = _mm_tn(dup, h2, name="mm_dw_up")
    du2, d_ln2_g, d_ln2_b = _ln_bwd(h1, xa, dh2, ln2_g, ln2_b, ALPHA, "ln2_bwd")
    dox = _mm(du2, full["w_xo"], trans_b=True, out_dtype=F32, name="mm_d_ox")
    dw_xo = _mm_tn(ox, du2, name="mm_dw_xo")
    dqx, dkx, dvx = _xattn_bwd(qx, kx, vx, ox, dox)
    dh1 = _mm(dqx, full["w_xq"], trans_b=True, out_dtype=F32, name="mm_dh1", addends=(du2,), coefs=(ALPHA,))
    dw_xq = _mm_tn(h1, dqx, name="mm_dw_xq")
    dw_xk = _mm_tn(mem_n, dkx, name="mm_dw_xk")
    dw_xv = _mm_tn(mem_n, dvx, name="mm_dw_xv")
    dmem_n = _mm(dkx, full["w_xk"], trans_b=True, out_dtype=F32, name="mm_dmem_k")
    dmem_n = _mm(dvx, full["w_xv"], trans_b=True, out_dtype=F32, name="mm_dmem_v", addends=(dmem_n,), coefs=(1.0,))
    _, d_mem_ln_g, d_mem_ln_b = _ln_bwd(mems, None, dmem_n, mem_ln_g, mem_ln_b, 1.0, "mem_ln_bwd")
    du1, d_ln1_g, d_ln1_b = _ln_bwd(h0, mix, dh1, ln1_g, ln1_b, ALPHA, "ln1_bwd")
    dmixed = _mm(du1, full["w_mix_out"], trans_b=True, out_dtype=F32, name="mm_d_mixed")
    dw_mix_out = _mm_tn(mixed, du1, name="mm_dw_mix_out")
    doa, dob, d_g_win, d_g_dil = _mix_norm_bwd(oa, ob, dmixed, g_win, g_dil)
    dqa, dkva, dsink8 = _banded_bwd(zr, oa, lse_a, doa, tab, attn_sink, kv_heads=_WIN_KV, name="win_attn_bwd", **_WIN_CFG)
    dqs, dks, dvs = [], [], []
    for gi in range(3):
        dq_g, dkv_g = _banded_bwd(zr, ob, lse_b, dob, tab, None, kv_heads=_DIL_KV, name=f"dil_attn_bwd{gi}", **_dil_cfg(gi))
        dqs.append(dq_g)
        dks.append(dkv_g[:, :512])
        dvs.append(dkv_g[:, 512:])
    dz = jnp.concatenate(dqs + dks + dvs + [dqa, dkva], axis=1)
    dh0 = _mm(dz, w_in_t, trans_b=False, out_dtype=F32, name="mm_dh0", addends=(du1,), coefs=(ALPHA,))
    dw_in_tz = _mm_tn(dz, h0, name="mm_dw_in")
    dw_in_t = jnp.concatenate([dw_in_tz[4608:], dw_in_tz[:4608]], axis=0)
    dx, d_ln_in_g, d_ln_in_b = _ln_bwd(xs, None, dh0, row(ln_in_g), row(ln_in_b), 1.0, "ln_in_bwd")

    partial = dict(w_in=dw_in_t, w_gate=dw_gate_t, w_up=dw_up_t, w_mix_out=dw_mix_out, w_xq=dw_xq, w_xk=dw_xk,
                   w_xv=dw_xv, w_xo=dw_xo, w_down=dw_down)
    gp = jnp.concatenate([partial[n].reshape(N_DEV, r, D_MODEL) for n, r in PACK_ROWS], axis=1)
    gsum = _sum_slots(_reduce_scatter_exchange(gp))
    grads = {}
    off = 0
    for n, r in PACK_ROWS:
        blk = gsum[off:off + r]
        off += r
        grads[n] = (blk.T if n in ("w_in", "w_gate", "w_up") else blk)[None]

    small = jnp.concatenate([
        d_ln_in_g, d_ln_in_b, d_ln1_g, d_ln1_b, d_mem_ln_g, d_mem_ln_b, d_ln2_g, d_ln2_b, d_ln3_g, d_ln3_b,
        jnp.concatenate([d_g_win, d_g_dil], axis=1),
        jnp.pad(d_conv_b, ((0, 0), (0, 3072 - D_FF))).reshape(3, 1024),
        jnp.pad(dsink8[0:1, :], ((0, 0), (0, 1024 - 128))),
        jnp.pad(dcw8[0:3], ((0, 0), (0, 3072 - D_FF))).reshape(9, 1024),
    ], axis=0)
    _, ssum = _all_gather_rows(small, name="small_grad_all_reduce", in_vmem=True, sum_rows=True)
    names10 = ["ln_in_g", "ln_in_b", "ln1_g", "ln1_b", "mem_ln_g", "mem_ln_b", "ln2_g", "ln2_b", "ln3_g", "ln3_b"]
    for i, n in enumerate(names10):
        grads[n] = ssum[i].reshape(weights[n].shape)
    grads["g_win"] = ssum[10:11, :512]
    grads["g_dil"] = ssum[10:11, 512:]
    grads["conv_b"] = ssum[11:14].reshape(1, 3072)[:, :D_FF]
    grads["attn_sink"] = ssum[14:15, :8]
    dcw_full = ssum[15:24].reshape(3, 3072)[:, :D_FF]
    me_lin = 4 * lax.axis_index("x") + 2 * lax.axis_index("y") + lax.axis_index("c")
    grads["conv_w"] = lax.dynamic_slice_in_dim(dcw_full, me_lin * 352, 352, axis=1)[None]

    delta, new_m, new_v = {}, {}, {}
    big = [n for n, _ in PACK_ROWS]
    for n in big:
        shp = weights[n].shape
        two_d = lambda a: a.reshape(shp[1], shp[2])
        d_, m_, v_ = _adamw(two_d(weights[n]), two_d(grads[n]), two_d(mom_m[n]), two_d(mom_v[n]), f"adamw_{n}")
        delta[n], new_m[n], new_v[n] = d_.reshape(shp), m_.reshape(shp), v_.reshape(shp)
    small_names = [n for n in order if n not in big]

    def pack_small(src):
        rows_ = []
        for n in small_names:
            flat = src[n].reshape(1, -1)
            width = -(-flat.shape[1] // 1024) * 1024
            rows_.append(jnp.pad(flat, ((0, 0), (0, width - flat.shape[1]))).reshape(-1, 1024))
        packed_ = jnp.concatenate(rows_, axis=0)
        return _pad_rows(packed_, -(-packed_.shape[0] // 8) * 8)

    d_s, m_s, v_s = _adamw(pack_small(weights), pack_small(grads), pack_small(mom_m), pack_small(mom_v), "adamw_small")
    r0 = 0
    for n in small_names:
        size = weights[n].size
        nrow = -(-size // 1024)
        for dst, src in ((delta, d_s), (new_m, m_s), (new_v, v_s)):
            dst[n] = src[r0:r0 + nrow].reshape(-1)[:size].reshape(weights[n].shape)
        r0 += nrow

    loss = lax.psum(loss_local[0, 0], MESH_AXES)
    return (loss, dx[None], *[grads[n] for n in order], *[delta[n] for n in order], *[new_m[n] for n in order],
            *[new_v[n] for n in order])
```

```python
import functools
import math

import jax
import jax.numpy as jnp
from jax import lax
from jax.experimental import pallas as pl
from jax.experimental.pallas import tpu as pltpu

F32 = jnp.float32
BF16 = jnp.bfloat16
SDS = jax.ShapeDtypeStruct
_PALLAS_CALL = pl.pallas_call

D_MODEL = 1024
HEAD_DIM = 64
WIN_HALF = 128
DIL_PAIRS = ((128, 1), (512, 4), (2048, 16))
DIL_SIDE = 64
ROT_DIM = 16
ROPE_THETA = 500000.0
MEM_LEN = 256
X_HEADS = 4
X_HEAD_DIM = 256
D_FF = 2816
IN_WIDTH = 5376
ZW = 5632
Z_QB, Z_KB, Z_VB, Z_QA, Z_KA, Z_VA = 0, 1536, 3072, 4608, 5120, 5248
ALPHA = (2.0) ** 0.25
LN_EPS = 1e-5
NEG_INF = -1e30
ADAM_LR, ADAM_B1, ADAM_B2, ADAM_EPS, ADAM_WD, ADAM_STEP = 0.001, 0.9, 0.999, 1e-08, 0.01, 10
N_DEV = 8
MESH_AXES = ("x", "y", "c")
VMEM_LIMIT_BYTES = 52 * 1024 * 1024
ATTN_TQ = 256
TABW = 384

PACK_ROWS = (("w_in", 672), ("w_gate", 352), ("w_up", 352), ("w_mix_out", 128), ("w_xq", 128), ("w_xk", 128),
             ("w_xv", 128), ("w_xo", 128), ("w_down", 352))
PACK_R = sum(r for _, r in PACK_ROWS)
SMALL_ROWS = 24


def _pick(n, cands):
    for c in cands:
        if n % c == 0:
            return c
    return n


def _pcall(body, *, name, out_shape, grid=None, in_specs=None, out_specs=None, scratch_shapes=(), dims=None,
           aliases=None):
    kw = {}
    if grid is not None:
        kw["grid"] = grid
    if in_specs is not None:
        kw["in_specs"] = in_specs
    if out_specs is not None:
        kw["out_specs"] = out_specs
    if aliases:
        kw["input_output_aliases"] = aliases
    return _PALLAS_CALL(
        body, name=name, out_shape=out_shape, scratch_shapes=list(scratch_shapes),
        compiler_params=pltpu.CompilerParams(dimension_semantics=dims, vmem_limit_bytes=VMEM_LIMIT_BYTES), **kw)


def _mm(a, b, *, trans_b, out_dtype, name, addends=(), coefs=()):
    m, k = a.shape
    n = b.shape[0] if trans_b else b.shape[1]
    tm = _pick(m, (512, 256))
    tn = _pick(n, (512, 1408, 256, 128))
    n_add = len(addends)
    dn = (((1,), (1,)), ((), ())) if trans_b else (((1,), (0,)), ((), ()))

    def body(a_ref, b_ref, *rest):
        o_ref = rest[n_add]
        acc = lax.dot_general(a_ref[...].astype(BF16), b_ref[...].astype(BF16), dn, preferred_element_type=F32)
        for r_ref, c in zip(rest[:n_add], coefs):
            acc = acc + (r_ref[...] if c == 1.0 else c * r_ref[...])
        o_ref[...] = acc.astype(out_dtype)

    b_spec = pl.BlockSpec((tn, k), lambda i, j: (j, 0)) if trans_b else pl.BlockSpec((k, tn), lambda i, j: (0, j))
    in_specs = [pl.BlockSpec((tm, k), lambda i, j: (i, 0)), b_spec]
    in_specs += [pl.BlockSpec((tm, tn), lambda i, j: (i, j)) for _ in addends]
    return _pcall(body, name=name, out_shape=SDS((m, n), out_dtype), grid=(m // tm, n // tn), in_specs=in_specs,
                  out_specs=pl.BlockSpec((tm, tn), lambda i, j: (i, j)), dims=("parallel", "parallel"))(a, b, *addends)


def _mm_tn(a, b, *, name):
    s, m = a.shape
    n = b.shape[1]
    tm = _pick(m, (768, 1408, 1024, 512, 256, 128))
    tk = _pick(s, (512, 256))
    nk = s // tk

    def body(a_ref, b_ref, o_ref, acc_ref):
        kk = pl.program_id(1)

        @pl.when(kk == 0)
        def _():
            acc_ref[...] = jnp.zeros_like(acc_ref)

        acc_ref[...] += lax.dot_general(a_ref[...].astype(BF16), b_ref[...].astype(BF16), (((0,), (0,)), ((), ())),
                                        preferred_element_type=F32)

        @pl.when(kk == nk - 1)
        def _():
            o_ref[...] = acc_ref[...]

    return _pcall(body, name=name, out_shape=SDS((m, n), F32), grid=(m // tm, nk),
                  in_specs=[pl.BlockSpec((tk, tm), lambda i, kk: (kk, i)), pl.BlockSpec((tk, n), lambda i, kk: (kk, 0))],
                  out_specs=pl.BlockSpec((tm, n), lambda i, kk: (i, 0)), scratch_shapes=[pltpu.VMEM((tm, n), F32)],
                  dims=("parallel", "arbitrary"))(a, b)


def _rope_lane_consts():
    lane = jnp.arange(128)
    j = lane % HEAD_DIM
    inv_freq = ROPE_THETA ** (-jnp.arange(0, ROT_DIM, 2, dtype=F32) / ROT_DIM)
    freq = jnp.where(j < ROT_DIM, inv_freq[j % (ROT_DIM // 2)], 0.0).astype(F32)
    lo = (j < ROT_DIM // 2).astype(F32)
    hi = ((j >= ROT_DIM // 2) & (j < ROT_DIM)).astype(F32)
    return jnp.stack([freq, lo, hi] + [jnp.zeros((128,), F32)] * 5)


def _rope_tables(posf):
    s = posf.shape[0]
    tm = _pick(s, (1024, 512))

    def body(p_ref, c_ref, o_ref):
        ang = p_ref[...] * c_ref[0:1, :]
        lo = c_ref[1:2, :]
        hi = c_ref[2:3, :]
        cs = jnp.cos(ang)
        sn = jnp.sin(ang)
        o_ref[...] = jnp.concatenate([jnp.where(lo + hi > 0.0, cs, 1.0), -sn * lo, sn * hi], axis=1)

    return _pcall(body, name="rope_tables", out_shape=SDS((s, TABW), F32), grid=(s // tm,),
                  in_specs=[pl.BlockSpec((tm, 1), lambda i: (i, 0)), pl.BlockSpec((8, 128), lambda i: (0, 0))],
                  out_specs=pl.BlockSpec((tm, TABW), lambda i: (i, 0)), dims=("parallel",))(posf, _rope_lane_consts())


def _rope_apply(x, tab, sign):
    w = x.shape[1]
    rep = w // 128
    c = jnp.tile(tab[:, 0:128], (1, rep)) if rep > 1 else tab[:, 0:128]
    a = jnp.tile(tab[:, 128:256], (1, rep)) if rep > 1 else tab[:, 128:256]
    b = jnp.tile(tab[:, 256:384], (1, rep)) if rep > 1 else tab[:, 256:384]
    up = pltpu.roll(x, w - 8, 1)
    dn = pltpu.roll(x, 8, 1)
    if sign > 0:
        return x * c + up * a + dn * b
    return x * c - up * a - dn * b


def _rope_mask_row():
    col = jnp.arange(ZW)
    m = (col < Z_VB) | ((col >= Z_QA) & (col < Z_VA))
    return m.astype(F32).reshape(1, ZW)


def _proj_rope(h0, w_t, tab):
    s = h0.shape[0]
    tm = _pick(s, (512,))
    tn = 256

    def body(a_ref, w_ref, t_ref, m_ref, o_ref):
        z = lax.dot_general(a_ref[...].astype(BF16), w_ref[...], (((1,), (1,)), ((), ())), preferred_element_type=F32)
        zr = _rope_apply(z, t_ref[...], 1)
        msk = m_ref[...]
        o_ref[...] = (z + msk * (zr - z)).astype(BF16)

    return _pcall(body, name="proj_rope", out_shape=SDS((s, ZW), BF16), grid=(s // tm, IN_WIDTH // tn),
                  in_specs=[pl.BlockSpec((tm, D_MODEL), lambda i, j: (i, 0)), pl.BlockSpec((tn, D_MODEL), lambda i, j: (j, 0)),
                            pl.BlockSpec((tm, TABW), lambda i, j: (i, 0)), pl.BlockSpec((1, tn), lambda i, j: (0, j))],
                  out_specs=pl.BlockSpec((tm, tn), lambda i, j: (i, j)),
                  dims=("parallel", "parallel"))(h0, w_t, tab, _rope_mask_row())


def _band_specs(sd, blk, tq, width, per_tok, cb):
    r = tq // blk
    nbk = sd // blk
    prev = pl.BlockSpec((blk, width), lambda c, j: (jnp.maximum(j * r - 1, 0), c * per_tok + cb))
    cur = pl.BlockSpec((tq, width), lambda c, j: (j, c * per_tok + cb))
    nxt = pl.BlockSpec((blk, width), lambda c, j: (jnp.minimum((j + 1) * r, nbk - 1), c * per_tok + cb))
    return [prev, cur, nxt]


def _band_bias(j, blk, tq, sd, rows_are_tile):
    w = tq + 2 * blk
    shape = (tq, w) if rows_are_tile else (w, tq)
    tile_pos = j * tq + lax.broadcasted_iota(jnp.int32, shape, 0 if rows_are_tile else 1)
    wide_pos = j * tq - blk + lax.broadcasted_iota(jnp.int32, shape, 1 if rows_are_tile else 0)
    ok = (jnp.abs(tile_pos - wide_pos) <= blk) & (wide_pos >= 0) & (wide_pos < sd)
    return jnp.where(ok, 0.0, NEG_INF)


_NT = (((1,), (1,)), ((), ()))
_NN = (((1,), (0,)), ((), ()))
_TN = (((0,), (0,)), ((), ()))


def _banded_fwd(zr, sink, *, d, blk, tq, qw, kw, qcb, kcb, vcb, heads, name):
    s = zr.shape[0]
    sd = s // d
    tq = min(tq, sd)
    zv = zr.reshape(sd, d * ZW)
    has_sink = sink is not None
    scale = HEAD_DIM ** -0.5

    def body(q_ref, kp, kc, kn, vp, vc, vn, *rest):
        if has_sink:
            sink_ref, o_ref, lse_ref = rest
        else:
            o_ref, lse_ref = rest
        j = pl.program_id(1)
        q = q_ref[...] * scale
        k = jnp.concatenate([kp[...], kc[...], kn[...]], axis=0)
        v = jnp.concatenate([vp[...], vc[...], vn[...]], axis=0)
        bias = _band_bias(j, blk, tq, sd, True)
        outs, lses = [], []
        for ql, kl, vl, si in heads:
            sc = lax.dot_general(q[:, ql:ql + HEAD_DIM], k[:, kl:kl + HEAD_DIM], _NT, preferred_element_type=F32) + bias
            m = jnp.max(sc, axis=-1, keepdims=True)
            if has_sink:
                m = jnp.maximum(m, sink_ref[0, si])
            p = jnp.exp(sc - m)
            den = jnp.sum(p, axis=-1, keepdims=True)
            if has_sink:
                den = den + jnp.exp(sink_ref[0, si] - m)
            o = lax.dot_general(p.astype(BF16), v[:, vl:vl + HEAD_DIM], _NN, preferred_element_type=F32) / den
            outs.append(o)
            lses.append(jnp.broadcast_to(m + jnp.log(den), (tq, HEAD_DIM)))
        o_ref[...] = jnp.concatenate(outs, axis=1)
        lse_ref[...] = jnp.concatenate(lses, axis=1)

    in_specs = ([pl.BlockSpec((tq, qw), lambda c, j: (j, c * (ZW // qw) + qcb))]
                + _band_specs(sd, blk, tq, kw, ZW // kw, kcb) + _band_specs(sd, blk, tq, kw, ZW // kw, vcb))
    args = [zv] * 7
    if has_sink:
        in_specs.append(pl.BlockSpec(memory_space=pltpu.SMEM))
        args.append(sink)
    o_spec = pl.BlockSpec((tq, qw), lambda c, j: (j, c))
    o, lse = _pcall(body, name=name, out_shape=(SDS((sd, d * qw), F32), SDS((sd, d * qw), F32)), grid=(d, sd // tq),
                    in_specs=in_specs, out_specs=(o_spec, o_spec), dims=("parallel", "parallel"))(*args)
    return o.reshape(s, qw), lse.reshape(s, qw)


def _banded_bwd(zr, o, lse, do, tab, sink, *, d, blk, tq, qw, kw, qcb, kcb, vcb, heads, kv_heads, name):
    s = zr.shape[0]
    sd = s // d
    tq = min(tq, sd)
    zv = zr.reshape(sd, d * ZW)
    ov, lv, dov = (t.reshape(sd, d * qw) for t in (o, lse, do))
    tv = tab.reshape(sd, d * TABW)
    has_sink = sink is not None
    scale = HEAD_DIM ** -0.5
    kvw = HEAD_DIM * len(kv_heads)
    b0, b1 = blk, blk + tq

    def body(*refs):
        q3 = jnp.concatenate([r[...] for r in refs[0:3]], axis=0) * scale
        k3 = jnp.concatenate([r[...] for r in refs[3:6]], axis=0)
        v3 = jnp.concatenate([r[...] for r in refs[6:9]], axis=0)
        o3 = jnp.concatenate([r[...] for r in refs[9:12]], axis=0)
        l3 = jnp.concatenate([r[...] for r in refs[12:15]], axis=0)
        do3 = jnp.concatenate([r[...] for r in refs[15:18]], axis=0)
        t_ref = refs[18]
        if has_sink:
            sink_ref, dq_ref, dkv_ref, dsink_ref = refs[19:]
        else:
            dq_ref, dkv_ref = refs[19:]
        j = pl.program_id(1)
        bias1 = _band_bias(j, blk, tq, sd, True)
        bias2 = _band_bias(j, blk, tq, sd, False)
        dqs = []
        dks = [None] * len(kv_heads)
        dvs = [None] * len(kv_heads)
        dsink_row = jnp.zeros((1, 128), F32)
        lane = lax.broadcasted_iota(jnp.int32, (1, 128), 1)
        for ql, kl, vl, si in heads:
            kvi = kv_heads.index((kl, vl))
            qh3 = q3[:, ql:ql + HEAD_DIM]
            kh3 = k3[:, kl:kl + HEAD_DIM]
            vh3 = v3[:, vl:vl + HEAD_DIM]
            doh3 = do3[:, ql:ql + HEAD_DIM]
            delta3 = jnp.sum(doh3 * o3[:, ql:ql + HEAD_DIM], axis=-1, keepdims=True)
            lse3 = l3[:, ql:ql + 1]
            dob3 = doh3.astype(BF16)
            sc = lax.dot_general(qh3[b0:b1], kh3, _NT, preferred_element_type=F32) + bias1
            p = jnp.exp(sc - lse3[b0:b1])
            dp = lax.dot_general(dob3[b0:b1], vh3, _NT, preferred_element_type=F32)
            ds = p * (dp - delta3[b0:b1])
            dqs.append(lax.dot_general(ds.astype(BF16), kh3, _NN, preferred_element_type=F32) * scale)
            if has_sink:
                psink = jnp.exp(sink_ref[0, si] - lse3[b0:b1])
                dsink_row = dsink_row + jnp.where(lane == si, -jnp.sum(psink * delta3[b0:b1]), 0.0)
            sc2 = lax.dot_general(qh3, kh3[b0:b1], _NT, preferred_element_type=F32) + bias2
            p2 = jnp.exp(sc2 - lse3)
            dv = lax.dot_general(p2.astype(BF16), dob3, _TN, preferred_element_type=F32)
            dp2 = lax.dot_general(dob3, vh3[b0:b1], _NT, preferred_element_type=F32)
            ds2 = p2 * (dp2 - delta3)
            dk = lax.dot_general(ds2.astype(BF16), qh3, _TN, preferred_element_type=F32)
            dks[kvi] = dk if dks[kvi] is None else dks[kvi] + dk
            dvs[kvi] = dv if dvs[kvi] is None else dvs[kvi] + dv
        tabv = t_ref[...]
        dq_ref[...] = _rope_apply(jnp.concatenate(dqs, axis=1), tabv, -1).astype(BF16)
        dk_all = jnp.concatenate(dks, axis=1) if len(dks) > 1 else dks[0]
        dv_all = jnp.concatenate(dvs, axis=1) if len(dvs) > 1 else dvs[0]
        dkv_ref[...] = jnp.concatenate([_rope_apply(dk_all, tabv, -1), dv_all], axis=1).astype(BF16)
        if has_sink:
            first = (pl.program_id(0) == 0) & (j == 0)

            @pl.when(first)
            def _():
                dsink_ref[...] = jnp.zeros_like(dsink_ref)

            dsink_ref[0:1, :] += dsink_row

    in_specs = (_band_specs(sd, blk, tq, qw, ZW // qw, qcb) + _band_specs(sd, blk, tq, kw, ZW // kw, kcb)
                + _band_specs(sd, blk, tq, kw, ZW // kw, vcb) + _band_specs(sd, blk, tq, qw, 1, 0) * 3
                + [pl.BlockSpec((tq, TABW), lambda c, j: (j, c))])
    args = [zv] * 9 + [ov] * 3 + [lv] * 3 + [dov] * 3 + [tv]
    out_shape = [SDS((sd, d * qw), BF16), SDS((sd, d * 2 * kvw), BF16)]
    out_specs = [pl.BlockSpec((tq, qw), lambda c, j: (j, c)), pl.BlockSpec((tq, 2 * kvw), lambda c, j: (j, c))]
    dims = ("parallel", "parallel")
    if has_sink:
        in_specs.append(pl.BlockSpec(memory_space=pltpu.SMEM))
        args.append(sink)
        out_shape.append(SDS((8, 128), F32))
        out_specs.append(pl.BlockSpec((8, 128), lambda c, j: (0, 0)))
        dims = ("arbitrary", "arbitrary")
    res = _pcall(body, name=name, out_shape=tuple(out_shape), grid=(d, sd // tq), in_specs=in_specs,
                 out_specs=tuple(out_specs), dims=dims)(*args)
    dq = res[0].reshape(s, qw)
    dkv = res[1].reshape(s, 2 * kvw)
    return (dq, dkv, res[2]) if has_sink else (dq, dkv)


_WIN_HEADS = tuple((h * HEAD_DIM, (h // 4) * HEAD_DIM, 128 + (h // 4) * HEAD_DIM, h) for h in range(8))
_WIN_KV = ((0, 128), (64, 192))
_WIN_CFG = dict(d=1, blk=WIN_HALF, tq=ATTN_TQ, qw=512, kw=256, qcb=Z_QA // 512, kcb=Z_KA // 256, vcb=Z_KA // 256,
                heads=_WIN_HEADS)
_DIL_HEADS = tuple((h * HEAD_DIM, h * HEAD_DIM, h * HEAD_DIM, h) for h in range(8))
_DIL_KV = tuple((h * HEAD_DIM, h * HEAD_DIM) for h in range(8))


def _dil_cfg(gi):
    return dict(d=DIL_PAIRS[gi][1], blk=DIL_SIDE, tq=ATTN_TQ, qw=512, kw=512, qcb=Z_QB // 512 + gi,
                kcb=Z_KB // 512 + gi, vcb=Z_VB // 512 + gi, heads=_DIL_HEADS)


def _mix_norm_fwd(oa, ogs, lgs, g_win, g_dil):
    s = oa.shape[0]
    tm = _pick(s, (512,))

    def body(oa_ref, o0, o1, o2, l0, l1, l2, gw_ref, gd_ref, mixed_ref, ob_ref, lb_ref):
        la, lb, lc = l0[...], l1[...], l2[...]
        mx = jnp.maximum(jnp.maximum(la, lb), lc)
        ea, eb, ec = jnp.exp(la - mx), jnp.exp(lb - mx), jnp.exp(lc - mx)
        den = ea + eb + ec
        ob = (ea / den) * o0[...] + (eb / den) * o1[...] + (ec / den) * o2[...]
        ob_ref[...] = ob
        lb_ref[...] = mx + jnp.log(den)
        a = oa_ref[...]
        ra = lax.rsqrt(jnp.mean(a * a, axis=-1, keepdims=True) + LN_EPS)
        rb = lax.rsqrt(jnp.mean(ob * ob, axis=-1, keepdims=True) + LN_EPS)
        mixed_ref[...] = jnp.concatenate([a * ra * gw_ref[...], ob * rb * gd_ref[...]], axis=1).astype(BF16)

    row = pl.BlockSpec((tm, 512), lambda i: (i, 0))
    vec = pl.BlockSpec((1, 512), lambda i: (0, 0))
    return _pcall(body, name="mix_norm_fwd", out_shape=(SDS((s, 1024), BF16), SDS((s, 512), F32), SDS((s, 512), F32)),
                  grid=(s // tm,), in_specs=[row] * 7 + [vec, vec],
                  out_specs=(pl.BlockSpec((tm, 1024), lambda i: (i, 0)), row, row),
                  dims=("parallel",))(oa, *ogs, *lgs, g_win, g_dil)


def _mix_norm_bwd(oa, ob, dmixed, g_win, g_dil):
    s = oa.shape[0]
    tm = _pick(s, (512,))
    nt = s // tm

    def body(oa_ref, ob_ref, dm_ref, gw_ref, gd_ref, doa_ref, dob_ref, dgw_ref, dgd_ref, acc_w, acc_d):
        i = pl.program_id(0)

        @pl.when(i == 0)
        def _():
            acc_w[...] = jnp.zeros_like(acc_w)
            acc_d[...] = jnp.zeros_like(acc_d)

        dm = dm_ref[...]
        for x_ref, g_ref, dy, dx_ref, acc in ((oa_ref, gw_ref, dm[:, :512], doa_ref, acc_w),
                                              (ob_ref, gd_ref, dm[:, 512:], dob_ref, acc_d)):
            x = x_ref[...]
            r = lax.rsqrt(jnp.mean(x * x, axis=-1, keepdims=True) + LN_EPS)
            dyg = dy * g_ref[...]
            dx_ref[...] = r * dyg - x * (r * r * r) * jnp.mean(dyg * x, axis=-1, keepdims=True)
            acc[...] += jnp.sum((dy * x * r).reshape(tm // 8, 8, 512), axis=0)

        @pl.when(i == nt - 1)
        def _():
            dgw_ref[...] = jnp.sum(acc_w[...], axis=0, keepdims=True)
            dgd_ref[...] = jnp.sum(acc_d[...], axis=0, keepdims=True)

    row = pl.BlockSpec((tm, 512), lambda i: (i, 0))
    vec = pl.BlockSpec((1, 512), lambda i: (0, 0))
    return _pcall(body, name="mix_norm_bwd",
                  out_shape=(SDS((s, 512), F32), SDS((s, 512), F32), SDS((1, 512), F32), SDS((1, 512), F32)),
                  grid=(nt,), in_specs=[row, row, pl.BlockSpec((tm, 1024), lambda i: (i, 0)), vec, vec],
                  out_specs=(row, row, vec, vec), scratch_shapes=[pltpu.VMEM((8, 512), F32), pltpu.VMEM((8, 512), F32)],
                  dims=("arbitrary",))(oa, ob, dmixed, g_win, g_dil)


def _ln_fwd(a, r, g, b, ca, name):
    s = a.shape[0]
    tm = _pick(s, (512, 256))
    has_r = r is not None

    def body(*refs):
        a_ref = refs[0]
        r_ref = refs[1] if has_r else None
        g_ref, b_ref, o_ref = refs[1 + has_r:]
        u = a_ref[...] if ca == 1.0 else ca * a_ref[...]
        if has_r:
            u = u + r_ref[...]
        mu = jnp.mean(u, axis=-1, keepdims=True)
        xc = u - mu
        var = jnp.mean(xc * xc, axis=-1, keepdims=True)
        o_ref[...] = xc * lax.rsqrt(var + LN_EPS) * g_ref[...] + b_ref[...]

    row = pl.BlockSpec((tm, D_MODEL), lambda i: (i, 0))
    vec = pl.BlockSpec((1, D_MODEL), lambda i: (0, 0))
    args = [a] + ([r] if has_r else []) + [g, b]
    return _pcall(body, name=name, out_shape=SDS((s, D_MODEL), F32), grid=(s // tm,),
                  in_specs=[row] * (1 + has_r) + [vec, vec], out_specs=row, dims=("parallel",))(*args)


def _ln_bwd(a, r, dy, g, b, ca, name, loss_mode=False):
    s = a.shape[0]
    tm = _pick(s, (512, 256))
    nt = s // tm
    has_r = r is not None

    def body(*refs):
        a_ref = refs[0]
        r_ref = refs[1] if has_r else None
        dy_ref, g_ref, b_ref = refs[1 + has_r:4 + has_r]
        outs = refs[4 + has_r:]
        if loss_mode:
            du_ref, dg_ref, db_ref, loss_ref, acc_g, acc_b, acc_l = outs
        else:
            du_ref, dg_ref, db_ref, acc_g, acc_b = outs
        i = pl.program_id(0)

        @pl.when(i == 0)
        def _():
            acc_g[...] = jnp.zeros_like(acc_g)
            acc_b[...] = jnp.zeros_like(acc_b)
            if loss_mode:
                acc_l[...] = jnp.zeros_like(acc_l)

        u = a_ref[...] if ca == 1.0 else ca * a_ref[...]
        if has_r:
            u = u + r_ref[...]
        mu = jnp.mean(u, axis=-1, keepdims=True)
        xc = u - mu
        var = jnp.mean(xc * xc, axis=-1, keepdims=True)
        rstd = lax.rsqrt(var + LN_EPS)
        xhat = xc * rstd
        gv = g_ref[...]
        if loss_mode:
            err = (xhat * gv + b_ref[...]) - dy_ref[...]
            acc_l[...] += jnp.sum((err * err).reshape(tm // 8, 8, D_MODEL), axis=0)
            dyv = err * (1.0 / D_MODEL)
        else:
            dyv = dy_ref[...]
        dxh = dyv * gv
        du_ref[...] = rstd * (dxh - jnp.mean(dxh, axis=-1, keepdims=True)
                              - xhat * jnp.mean(dxh * xhat, axis=-1, keepdims=True))
        acc_g[...] += jnp.sum((dyv * xhat).reshape(tm // 8, 8, D_MODEL), axis=0)
        acc_b[...] += jnp.sum(dyv.reshape(tm // 8, 8, D_MODEL), axis=0)

        @pl.when(i == nt - 1)
        def _():
            dg_ref[...] = jnp.sum(acc_g[...], axis=0, keepdims=True)
            db_ref[...] = jnp.sum(acc_b[...], axis=0, keepdims=True)
            if loss_mode:
                tot = jnp.sum(jnp.sum(acc_l[...], axis=0, keepdims=True), axis=1, keepdims=True)
                loss_ref[...] = tot * (0.5 / D_MODEL)

    row = pl.BlockSpec((tm, D_MODEL), lambda i: (i, 0))
    vec = pl.BlockSpec((1, D_MODEL), lambda i: (0, 0))
    out_shape = [SDS((s, D_MODEL), F32), SDS((1, D_MODEL), F32), SDS((1, D_MODEL), F32)]
    out_specs = [row, vec, vec]
    scratch = [pltpu.VMEM((8, D_MODEL), F32), pltpu.VMEM((8, D_MODEL), F32)]
    if loss_mode:
        out_shape.append(SDS((1, 1), F32))
        out_specs.append(pl.BlockSpec((1, 1), lambda i: (0, 0)))
        scratch.append(pltpu.VMEM((8, D_MODEL), F32))
    args = [a] + ([r] if has_r else []) + [dy, g, b]
    return _pcall(body, name=name, out_shape=tuple(out_shape), grid=(nt,), in_specs=[row] * (2 + has_r) + [vec, vec],
                  out_specs=tuple(out_specs), scratch_shapes=scratch, dims=("arbitrary",))(*args)


def _xattn_fwd(q, k, v):
    s = q.shape[0]
    tq = _pick(s, (512,))
    scale = X_HEAD_DIM ** -0.5

    def body(q_ref, k_ref, v_ref, o_ref):
        qv, kv, vv = q_ref[...], k_ref[...], v_ref[...]
        outs = []
        for h in range(X_HEADS):
            sl = slice(h * X_HEAD_DIM, (h + 1) * X_HEAD_DIM)
            sc = lax.dot_general(qv[:, sl], kv[:, sl], _NT, preferred_element_type=F32) * scale
            e = jnp.exp(sc - jnp.max(sc, axis=-1, keepdims=True))
            p = e / jnp.sum(e, axis=-1, keepdims=True)
            outs.append(lax.dot_general(p.astype(BF16), vv[:, sl], _NN, preferred_element_type=F32))
        o_ref[...] = jnp.concatenate(outs, axis=1)

    row = pl.BlockSpec((tq, D_MODEL), lambda i: (i, 0))
    full = pl.BlockSpec((MEM_LEN, D_MODEL), lambda i: (0, 0))
    return _pcall(body, name="xattn_fwd", out_shape=SDS((s, D_MODEL), F32), grid=(s // tq,), in_specs=[row, full, full],
                  out_specs=row, dims=("parallel",))(q, k, v)


def _xattn_bwd(q, k, v, o, do):
    s = q.shape[0]
    tq = _pick(s, (512,))
    scale = X_HEAD_DIM ** -0.5

    def body(q_ref, k_ref, v_ref, o_ref, do_ref, dq_ref, dk_ref, dv_ref):
        i = pl.program_id(0)

        @pl.when(i == 0)
        def _():
            dk_ref[...] = jnp.zeros_like(dk_ref)
            dv_ref[...] = jnp.zeros_like(dv_ref)

        qv, kv, vv, ov, dov = q_ref[...], k_ref[...], v_ref[...], o_ref[...], do_ref[...]
        dqs, dks, dvs = [], [], []
        for h in range(X_HEADS):
            sl = slice(h * X_HEAD_DIM, (h + 1) * X_HEAD_DIM)
            sc = lax.dot_general(qv[:, sl], kv[:, sl], _NT, preferred_element_type=F32) * scale
            e = jnp.exp(sc - jnp.max(sc, axis=-1, keepdims=True))
            p = e / jnp.sum(e, axis=-1, keepdims=True)
            doh = dov[:, sl]
            dob = doh.astype(BF16)
            delta = jnp.sum(doh * ov[:, sl], axis=-1, keepdims=True)
            dvs.append(lax.dot_general(p.astype(BF16), dob, _TN, preferred_element_type=F32))
            dp = lax.dot_general(dob, vv[:, sl], _NT, preferred_element_type=F32)
            ds = (p * (dp - delta)).astype(BF16)
            dqs.append(lax.dot_general(ds, kv[:, sl], _NN, preferred_element_type=F32) * scale)
            dks.append(lax.dot_general(ds, qv[:, sl], _TN, preferred_element_type=F32) * scale)
        dq_ref[...] = jnp.concatenate(dqs, axis=1).astype(BF16)
        dk_ref[...] += jnp.concatenate(dks, axis=1)
        dv_ref[...] += jnp.concatenate(dvs, axis=1)

    row = pl.BlockSpec((tq, D_MODEL), lambda i: (i, 0))
    full = pl.BlockSpec((MEM_LEN, D_MODEL), lambda i: (0, 0))
    return _pcall(body, name="xattn_bwd",
                  out_shape=(SDS((s, D_MODEL), BF16), SDS((MEM_LEN, D_MODEL), F32), SDS((MEM_LEN, D_MODEL), F32)),
                  grid=(s // tq,), in_specs=[row, full, full, row, row], out_specs=(row, full, full),
                  dims=("arbitrary",))(q, k, v, o, do)


_SQRT_HALF = 0.7071067811865476
_INV_SQRT_2PI = 0.3989422804014327


def _halo_specs(s, tm, width):
    n8 = s // 8
    r8 = tm // 8
    prev = pl.BlockSpec((8, width), lambda i: (jnp.maximum(i * r8 - 1, 0), 0))
    nxt = pl.BlockSpec((8, width), lambda i: (jnp.minimum((i + 1) * r8, n8 - 1), 0))
    return prev, nxt


def _shifted(x, prev8, next8, i, nt):
    tm = x.shape[0]
    row = lax.broadcasted_iota(jnp.int32, x.shape, 0)
    first = jnp.where(i == 0, 0.0, 1.0) * prev8[7:8, :]
    last = jnp.where(i == nt - 1, 0.0, 1.0) * next8[0:1, :]
    xm1 = jnp.where(row == 0, first, pltpu.roll(x, 1, 0))
    xp1 = jnp.where(row == tm - 1, last, pltpu.roll(x, tm - 1, 0))
    return xm1, xp1


def _glu_fwd(g, up, cw, cb):
    s = g.shape[0]
    tm = _pick(s, (256,))
    nt = s // tm

    def body(g_ref, gp_ref, gn_ref, up_ref, cw_ref, cb_ref, act_ref):
        i = pl.program_id(0)
        gv = g_ref[...]
        gm1, gp1 = _shifted(gv, gp_ref[...], gn_ref[...], i, nt)
        gc = gm1 * cw_ref[0:1, :] + gv * cw_ref[1:2, :] + gp1 * cw_ref[2:3, :] + cb_ref[...]
        gelu = 0.5 * gc * (1.0 + lax.erf(gc * _SQRT_HALF))
        act_ref[...] = (gelu * up_ref[...]).astype(BF16)

    row = pl.BlockSpec((tm, D_FF), lambda i: (i, 0))
    prev, nxt = _halo_specs(s, tm, D_FF)
    return _pcall(body, name="glu_fwd", out_shape=SDS((s, D_FF), BF16), grid=(nt,),
                  in_specs=[row, prev, nxt, row, pl.BlockSpec((8, D_FF), lambda i: (0, 0)),
                            pl.BlockSpec((1, D_FF), lambda i: (0, 0))],
                  out_specs=row, dims=("parallel",))(g, g, g, up, cw, cb)


def _glu_bwd(g, up, dact, cw, cb):
    s = g.shape[0]
    tm = _pick(s, (256,))
    nt = s // tm

    def body(g_ref, gp_ref, gn_ref, up_ref, da_ref, cw_ref, cb_ref, dgc_ref, dup_ref, dcw_ref, dcb_ref, a0, a1, a2, a3):
        i = pl.program_id(0)

        @pl.when(i == 0)
        def _():
            for a in (a0, a1, a2, a3):
                a[...] = jnp.zeros_like(a)

        gv = g_ref[...]
        gm1, gp1 = _shifted(gv, gp_ref[...], gn_ref[...], i, nt)
        gc = gm1 * cw_ref[0:1, :] + gv * cw_ref[1:2, :] + gp1 * cw_ref[2:3, :] + cb_ref[...]
        cdf = 0.5 * (1.0 + lax.erf(gc * _SQRT_HALF))
        pdf = jnp.exp(-0.5 * gc * gc) * _INV_SQRT_2PI
        da = da_ref[...]
        dup_ref[...] = (da * (gc * cdf)).astype(BF16)
        dgc = da * up_ref[...] * (cdf + gc * pdf)
        dgc_ref[...] = dgc

        def fold(t):
            return jnp.sum(t.reshape(tm // 8, 8, D_FF), axis=0)

        a0[...] += fold(dgc * gm1)
        a1[...] += fold(dgc * gv)
        a2[...] += fold(dgc * gp1)
        a3[...] += fold(dgc)

        @pl.when(i == nt - 1)
        def _():
            dcw_ref[...] = jnp.concatenate(
                [jnp.sum(a[...], axis=0, keepdims=True) for a in (a0, a1, a2)] + [jnp.zeros((5, D_FF), F32)], axis=0)
            dcb_ref[...] = jnp.sum(a3[...], axis=0, keepdims=True)

    row = pl.BlockSpec((tm, D_FF), lambda i: (i, 0))
    prev, nxt = _halo_specs(s, tm, D_FF)
    cw_spec = pl.BlockSpec((8, D_FF), lambda i: (0, 0))
    cb_spec = pl.BlockSpec((1, D_FF), lambda i: (0, 0))
    return _pcall(body, name="glu_bwd",
                  out_shape=(SDS((s, D_FF), F32), SDS((s, D_FF), BF16), SDS((8, D_FF), F32), SDS((1, D_FF), F32)),
                  grid=(nt,), in_specs=[row, prev, nxt, row, row, cw_spec, cb_spec],
                  out_specs=(row, row, cw_spec, cb_spec), scratch_shapes=[pltpu.VMEM((8, D_FF), F32)] * 4,
                  dims=("arbitrary",))(g, g, g, up, dact, cw, cb)


def _conv_bwd_input(dgc, cw):
    s = dgc.shape[0]
    tm = _pick(s, (256,))
    nt = s // tm

    def body(x_ref, xp_ref, xn_ref, cw_ref, o_ref):
        i = pl.program_id(0)
        xv = x_ref[...]
        xm1, xp1 = _shifted(xv, xp_ref[...], xn_ref[...], i, nt)
        o_ref[...] = (xp1 * cw_ref[0:1, :] + xv * cw_ref[1:2, :] + xm1 * cw_ref[2:3, :]).astype(BF16)

    row = pl.BlockSpec((tm, D_FF), lambda i: (i, 0))
    prev, nxt = _halo_specs(s, tm, D_FF)
    return _pcall(body, name="conv_bwd_input", out_shape=SDS((s, D_FF), BF16), grid=(nt,),
                  in_specs=[row, prev, nxt, pl.BlockSpec((8, D_FF), lambda i: (0, 0))], out_specs=row,
                  dims=("parallel",))(dgc, dgc, dgc, cw)


def _adamw(w, g, m, v, name):
    rows, cols = w.shape
    tr = _pick(rows, (256, 128, 64, 32, 16, 8))
    c1 = 1.0 - ADAM_B1 ** ADAM_STEP
    c2 = 1.0 - ADAM_B2 ** ADAM_STEP

    def body(w_ref, g_ref, m_ref, v_ref, d_ref, nm_ref, nv_ref):
        gv = g_ref[...]
        nm = ADAM_B1 * m_ref[...] + (1.0 - ADAM_B1) * gv
        nv = ADAM_B2 * v_ref[...] + (1.0 - ADAM_B2) * (gv * gv)
        d_ref[...] = -ADAM_LR * ((nm / c1) / (jnp.sqrt(nv / c2) + ADAM_EPS) + ADAM_WD * w_ref[...])
        nm_ref[...] = nm
        nv_ref[...] = nv

    blk = pl.BlockSpec((tr, cols), lambda i: (i, 0))
    return _pcall(body, name=name, out_shape=(SDS(w.shape, F32),) * 3, grid=(rows // tr,), in_specs=[blk] * 4,
                  out_specs=(blk,) * 3, dims=("parallel",))(w, g, m, v)


def _all_gather_rows(x_shard, *, name, in_vmem, sum_rows=False):
    m_per, n = x_shard.shape

    def body(x_ref, out_ref, *rest):
        if sum_rows:
            sum_ref, send_sems, recv_sems, local_sem = rest
        else:
            send_sems, recv_sems, local_sem = rest
        x, y, c = lax.axis_index("x"), lax.axis_index("y"), lax.axis_index("c")
        me, sibling = (x, y, c), (x, y, 1 - c)
        chips = [(1 - x, y), (x, 1 - y), (1 - x, 1 - y)]

        def rows(px, py, pc):
            return out_ref.at[pl.ds((4 * px + 2 * py + pc) * m_per, m_per), :]

        def copy(k, block, to, src=None):
            return pltpu.make_async_remote_copy(
                src_ref=rows(*block) if src is None else src, dst_ref=rows(*block), send_sem=send_sems.at[k],
                recv_sem=recv_sems.at[k], device_id=to, device_id_type=pl.DeviceIdType.MESH)

        mine = pltpu.make_async_copy(x_ref, rows(*me), local_sem)
        mine.start()
        first = [copy(0, me, sibling, src=x_ref)]
        first += [copy(1 + j, me, (*chip, c), src=x_ref) for j, chip in enumerate(chips)]
        for cp in first:
            cp.start()
        passed = [copy(4 + j, (*chip, c), sibling) for j, chip in enumerate(chips)]
        for j, chip in enumerate(chips):
            copy(1 + j, (*chip, c), me).wait_recv()
            passed[j].start()
        copy(0, sibling, me).wait_recv()
        for j, chip in enumerate(chips):
            copy(4 + j, (*chip, 1 - c), me).wait_recv()
        for cp in first + passed:
            cp.wait_send()
        mine.wait()
        if sum_rows:
            acc = out_ref[0:m_per, :]
            for dev in range(1, N_DEV):
                acc = acc + out_ref[dev * m_per:(dev + 1) * m_per, :]
            sum_ref[...] = acc

    space = pltpu.VMEM if in_vmem else pl.ANY
    out_shape = [SDS((N_DEV * m_per, n), x_shard.dtype)]
    out_specs = [pl.BlockSpec(memory_space=space)]
    if sum_rows:
        out_shape.append(SDS((m_per, n), x_shard.dtype))
        out_specs.append(pl.BlockSpec(memory_space=pltpu.VMEM))
    res = _PALLAS_CALL(
        body, name=name, out_shape=tuple(out_shape), in_specs=[pl.BlockSpec(memory_space=space)],
        out_specs=tuple(out_specs),
        scratch_shapes=[pltpu.SemaphoreType.DMA((7,)), pltpu.SemaphoreType.DMA((7,)), pltpu.SemaphoreType.DMA],
        compiler_params=pltpu.CompilerParams(vmem_limit_bytes=VMEM_LIMIT_BYTES),
    )(x_shard)
    return res if sum_rows else res[0]


def _reduce_scatter_exchange(gp):
    _, r, n = gp.shape

    def body(g_ref, recv_ref, send_sems, recv_sems, local_sem):
        x, y, c = lax.axis_index("x"), lax.axis_index("y"), lax.axis_index("c")
        me_lin = 4 * x + 2 * y + c
        mine = pltpu.make_async_copy(g_ref.at[me_lin], recv_ref.at[0], local_sem)
        mine.start()
        copies = []
        for k in range(1, N_DEV):
            px = 1 - x if (k >> 2) & 1 else x
            py = 1 - y if (k >> 1) & 1 else y
            pc = 1 - c if k & 1 else c
            cp = pltpu.make_async_remote_copy(
                src_ref=g_ref.at[4 * px + 2 * py + pc], dst_ref=recv_ref.at[k], send_sem=send_sems.at[k - 1],
                recv_sem=recv_sems.at[k - 1], device_id=(px, py, pc), device_id_type=pl.DeviceIdType.MESH)
            cp.start()
            copies.append(cp)
        for cp in copies:
            cp.wait_recv()
        for cp in copies:
            cp.wait_send()
        mine.wait()

    return _PALLAS_CALL(
        body, name="grad_exchange", out_shape=SDS(gp.shape, gp.dtype), in_specs=[pl.BlockSpec(memory_space=pl.ANY)],
        out_specs=pl.BlockSpec(memory_space=pl.ANY),
        scratch_shapes=[pltpu.SemaphoreType.DMA((7,)), pltpu.SemaphoreType.DMA((7,)), pltpu.SemaphoreType.DMA],
        compiler_params=pltpu.CompilerParams(vmem_limit_bytes=VMEM_LIMIT_BYTES),
    )(gp)


def _sum_slots(recv):
    _, r, n = recv.shape
    tr = _pick(r, (296, 128, 64, 32, 16, 8))

    def body(x_ref, o_ref):
        acc = x_ref[0]
        for k in range(1, N_DEV):
            acc = acc + x_ref[k]
        o_ref[...] = acc

    return _pcall(body, name="grad_sum", out_shape=SDS((r, n), F32), grid=(r // tr,),
                  in_specs=[pl.BlockSpec((N_DEV, tr, n), lambda i: (0, i, 0))],
                  out_specs=pl.BlockSpec((tr, n), lambda i: (i, 0)), dims=("parallel",))(recv)


def _pad_rows(a, rows):
    return jnp.pad(a, ((0, rows - a.shape[0]), (0, 0)))


def kernel(x, mem, positions, ln_in_g, ln_in_b, w_in, attn_sink, g_win, g_dil, w_mix_out, ln1_g, ln1_b, mem_ln_g, mem_ln_b, w_xq, w_xk, w_xv, w_xo, ln2_g, ln2_b, w_gate, w_up, conv_w, conv_b, w_down, ln3_g, ln3_b, loss_target, m_ln_in_g, m_ln_in_b, m_w_in, m_attn_sink, m_g_win, m_g_dil, m_w_mix_out, m_ln1_g, m_ln1_b, m_mem_ln_g, m_mem_ln_b, m_w_xq, m_w_xk, m_w_xv, m_w_xo, m_ln2_g, m_ln2_b, m_w_gate, m_w_up, m_conv_w, m_conv_b, m_w_down, m_ln3_g, m_ln3_b, v_ln_in_g, v_ln_in_b, v_w_in, v_attn_sink, v_g_win, v_g_dil, v_w_mix_out, v_ln1_g, v_ln1_b, v_mem_ln_g, v_mem_ln_b, v_w_xq, v_w_xk, v_w_xv, v_w_xo, v_ln2_g, v_ln2_b, v_w_gate, v_w_up, v_conv_w, v_conv_b, v_w_down, v_ln3_g, v_ln3_b):
    weights = dict(ln_in_g=ln_in_g, ln_in_b=ln_in_b, w_in=w_in, attn_sink=attn_sink, g_win=g_win, g_dil=g_dil, w_mix_out=w_mix_out, ln1_g=ln1_g, ln1_b=ln1_b, mem_ln_g=mem_ln_g, mem_ln_b=mem_ln_b, w_xq=w_xq, w_xk=w_xk, w_xv=w_xv, w_xo=w_xo, ln2_g=ln2_g, ln2_b=ln2_b, w_gate=w_gate, w_up=w_up, conv_w=conv_w, conv_b=conv_b, w_down=w_down, ln3_g=ln3_g, ln3_b=ln3_b)
    mom_m = dict(ln_in_g=m_ln_in_g, ln_in_b=m_ln_in_b, w_in=m_w_in, attn_sink=m_attn_sink, g_win=m_g_win, g_dil=m_g_dil, w_mix_out=m_w_mix_out, ln1_g=m_ln1_g, ln1_b=m_ln1_b, mem_ln_g=m_mem_ln_g, mem_ln_b=m_mem_ln_b, w_xq=m_w_xq, w_xk=m_w_xk, w_xv=m_w_xv, w_xo=m_w_xo, ln2_g=m_ln2_g, ln2_b=m_ln2_b, w_gate=m_w_gate, w_up=m_w_up, conv_w=m_conv_w, conv_b=m_conv_b, w_down=m_w_down, ln3_g=m_ln3_g, ln3_b=m_ln3_b)
    mom_v = dict(ln_in_g=v_ln_in_g, ln_in_b=v_ln_in_b, w_in=v_w_in, attn_sink=v_attn_sink, g_win=v_g_win, g_dil=v_g_dil, w_mix_out=v_w_mix_out, ln1_g=v_ln1_g, ln1_b=v_ln1_b, mem_ln_g=v_mem_ln_g, mem_ln_b=v_mem_ln_b, w_xq=v_w_xq, w_xk=v_w_xk, w_xv=v_w_xv, w_xo=v_w_xo, ln2_g=v_ln2_g, ln2_b=v_ln2_b, w_gate=v_w_gate, w_up=v_w_up, conv_w=v_conv_w, conv_b=v_conv_b, w_down=v_w_down, ln3_g=v_ln3_g, ln3_b=v_ln3_b)
    order = list(weights)
    s = x.shape[1]
    xs = x[0]
    mems = mem[0]
    target = loss_target[0]
    row = lambda a: a.reshape(1, -1)

    shard_rows = dict(w_in=w_in[0].T, w_gate=w_gate[0].T, w_up=w_up[0].T, w_mix_out=w_mix_out[0], w_xq=w_xq[0],
                      w_xk=w_xk[0], w_xv=w_xv[0], w_xo=w_xo[0], w_down=w_down[0])
    packed = jnp.concatenate([shard_rows[n].astype(BF16) for n, _ in PACK_ROWS], axis=0)
    gathered = _all_gather_rows(packed, name="weight_all_gather", in_vmem=False).reshape(N_DEV, PACK_R, D_MODEL)
    full = {}
    off = 0
    for n, r in PACK_ROWS:
        full[n] = gathered[:, off:off + r, :].reshape(N_DEV * r, D_MODEL)
        off += r
    w_in_t = jnp.concatenate([full["w_in"][768:], full["w_in"][:768]], axis=0)
    cw_pad = jnp.pad(conv_w[0], ((0, 5), (0, 32)))
    cw_all = _all_gather_rows(cw_pad, name="conv_w_all_gather", in_vmem=True).reshape(N_DEV, 8, 384)
    cw_full = jnp.transpose(cw_all[:, :3, :352], (1, 0, 2)).reshape(3, D_FF)
    cw8 = _pad_rows(cw_full, 8)

    tab = _rope_tables(positions.astype(F32).reshape(s, 1))
    h0 = _ln_fwd(xs, None, row(ln_in_g), row(ln_in_b), 1.0, "ln_in_fwd")
    zr = _proj_rope(h0, w_in_t, tab)
    oa, lse_a = _banded_fwd(zr, attn_sink, name="win_attn_fwd", **_WIN_CFG)
    ogs, lgs = [], []
    for gi in range(3):
        o_g, l_g = _banded_fwd(zr, None, name=f"dil_attn_fwd{gi}", **_dil_cfg(gi))
        ogs.append(o_g)
        lgs.append(l_g)
    mixed, ob, lse_b = _mix_norm_fwd(oa, ogs, lgs, g_win, g_dil)
    mix = _mm(mixed, full["w_mix_out"], trans_b=False, out_dtype=F32, name="mm_mix_out")
    h1 = _ln_fwd(h0, mix, ln1_g, ln1_b, ALPHA, "ln1_fwd")
    mem_n = _ln_fwd(mems, None, mem_ln_g, mem_ln_b, 1.0, "mem_ln_fwd")
    kx = _mm(mem_n, full["w_xk"], trans_b=False, out_dtype=BF16, name="mm_xk")
    vx = _mm(mem_n, full["w_xv"], trans_b=False, out_dtype=BF16, name="mm_xv")
    qx = _mm(h1, full["w_xq"], trans_b=False, out_dtype=BF16, name="mm_xq")
    ox = _xattn_fwd(qx, kx, vx)
    xa = _mm(ox, full["w_xo"], trans_b=False, out_dtype=F32, name="mm_xo")
    h2 = _ln_fwd(h1, xa, ln2_g, ln2_b, ALPHA, "ln2_fwd")
    gate = _mm(h2, full["w_gate"], trans_b=True, out_dtype=F32, name="mm_gate")
    up = _mm(h2, full["w_up"], trans_b=True, out_dtype=F32, name="mm_up")
    act = _glu_fwd(gate, up, cw8, conv_b)
    ff = _mm(act, full["w_down"], trans_b=False, out_dtype=F32, name="mm_down")

    du3, d_ln3_g, d_ln3_b, loss_local = _ln_bwd(h2, ff, target, ln3_g, ln3_b, ALPHA, "ln3_bwd_loss", loss_mode=True)
    dact = _mm(du3, full["w_down"], trans_b=True, out_dtype=F32, name="mm_d_act")
    dw_down = _mm_tn(act, du3, name="mm_dw_down")
    dgc, dup, dcw8, d_conv_b = _glu_bwd(gate, up, dact, cw8, conv_b)
    dgate = _conv_bwd_input(dgc, cw8)
    dh2 = _mm(dgate, full["w_gate"], trans_b=False, out_dtype=F32, name="mm_dh2_gate", addends=(du3,), coefs=(ALPHA,))
    dh2 = _mm(dup, full["w_up"], trans_b=False, out_dtype=F32, name="mm_dh2_up", addends=(dh2,), coefs=(1.0,))
    dw_gate_t = _mm_tn(dgate, h2, name="mm_dw_gate")
    dw_up_t = _mm_tn(dup, h2, name="mm_dw_up")
    du2, d_ln2_g, d_ln2_b = _ln_bwd(h1, xa, dh2, ln2_g, ln2_b, ALPHA, "ln2_bwd")
    dox = _mm(du2, full["w_xo"], trans_b=True, out_dtype=F32, name="mm_d_ox")
    dw_xo = _mm_tn(ox, du2, name="mm_dw_xo")
    dqx, dkx, dvx = _xattn_bwd(qx, kx, vx, ox, dox)
    dh1 = _mm(dqx, full["w_xq"], trans_b=True, out_dtype=F32, name="mm_dh1", addends=(du2,), coefs=(ALPHA,))
    dw_xq = _mm_tn(h1, dqx, name="mm_dw_xq")
    dw_xk = _mm_tn(mem_n, dkx, name="mm_dw_xk")
    dw_xv = _mm_tn(mem_n, dvx, name="mm_dw_xv")
    dmem_n = _mm(dkx, full["w_xk"], trans_b=True, out_dtype=F32, name="mm_dmem_k")
    dmem_n = _mm(dvx, full["w_xv"], trans_b=True, out_dtype=F32, name="mm_dmem_v", addends=(dmem_n,), coefs=(1.0,))
    _, d_mem_ln_g, d_mem_ln_b = _ln_bwd(mems, None, dmem_n, mem_ln_g, mem_ln_b, 1.0, "mem_ln_bwd")
    du1, d_ln1_g, d_ln1_b = _ln_bwd(h0, mix, dh1, ln1_g, ln1_b, ALPHA, "ln1_bwd")
    dmixed = _mm(du1, full["w_mix_out"], trans_b=True, out_dtype=F32, name="mm_d_mixed")
    dw_mix_out = _mm_tn(mixed, du1, name="mm_dw_mix_out")
    doa, dob, d_g_win, d_g_dil = _mix_norm_bwd(oa, ob, dmixed, g_win, g_dil)
    dqa, dkva, dsink8 = _banded_bwd(zr, oa, lse_a, doa, tab, attn_sink, kv_heads=_WIN_KV, name="win_attn_bwd", **_WIN_CFG)
    dqs, dks, dvs = [], [], []
    for gi in range(3):
        dq_g, dkv_g = _banded_bwd(zr, ob, lse_b, dob, tab, None, kv_heads=_DIL_KV, name=f"dil_attn_bwd{gi}", **_dil_cfg(gi))
        dqs.append(dq_g)
        dks.append(dkv_g[:, :512])
        dvs.append(dkv_g[:, 512:])
    dz = jnp.concatenate(dqs + dks + dvs + [dqa, dkva], axis=1)
    dh0 = _mm(dz, w_in_t, trans_b=False, out_dtype=F32, name="mm_dh0", addends=(du1,), coefs=(ALPHA,))
    dw_in_tz = _mm_tn(dz, h0, name="mm_dw_in")
    dw_in_t = jnp.concatenate([dw_in_tz[4608:], dw_in_tz[:4608]], axis=0)
    dx, d_ln_in_g, d_ln_in_b = _ln_bwd(xs, None, dh0, row(ln_in_g), row(ln_in_b), 1.0, "ln_in_bwd")

    partial = dict(w_in=dw_in_t, w_gate=dw_gate_t, w_up=dw_up_t, w_mix_out=dw_mix_out, w_xq=dw_xq, w_xk=dw_xk,
                   w_xv=dw_xv, w_xo=dw_xo, w_down=dw_down)
    gp = jnp.concatenate([partial[n].reshape(N_DEV, r, D_MODEL) for n, r in PACK_ROWS], axis=1)
    gsum = _sum_slots(_reduce_scatter_exchange(gp))
    grads = {}
    off = 0
    for n, r in PACK_ROWS:
        blk = gsum[off:off + r]
        off += r
        grads[n] = (blk.T if n in ("w_in", "w_gate", "w_up") else blk)[None]

    small = jnp.concatenate([
        d_ln_in_g, d_ln_in_b, d_ln1_g, d_ln1_b, d_mem_ln_g, d_mem_ln_b, d_ln2_g, d_ln2_b, d_ln3_g, d_ln3_b,
        jnp.concatenate([d_g_win, d_g_dil], axis=1),
        jnp.pad(d_conv_b, ((0, 0), (0, 3072 - D_FF))).reshape(3, 1024),
        jnp.pad(dsink8[0:1, :], ((0, 0), (0, 1024 - 128))),
        jnp.pad(dcw8[0:3], ((0, 0), (0, 3072 - D_FF))).reshape(9, 1024),
    ], axis=0)
    _, ssum = _all_gather_rows(small, name="small_grad_all_reduce", in_vmem=True, sum_rows=True)
    names10 = ["ln_in_g", "ln_in_b", "ln1_g", "ln1_b", "mem_ln_g", "mem_ln_b", "ln2_g", "ln2_b", "ln3_g", "ln3_b"]
    for i, n in enumerate(names10):
        grads[n] = ssum[i].reshape(weights[n].shape)
    grads["g_win"] = ssum[10:11, :512]
    grads["g_dil"] = ssum[10:11, 512:]
    grads["conv_b"] = ssum[11:14].reshape(1, 3072)[:, :D_FF]
    grads["attn_sink"] = ssum[14:15, :8]
    dcw_full = ssum[15:24].reshape(3, 3072)[:, :D_FF]
    me_lin = 4 * lax.axis_index("x") + 2 * lax.axis_index("y") + lax.axis_index("c")
    grads["conv_w"] = lax.dynamic_slice_in_dim(dcw_full, me_lin * 352, 352, axis=1)[None]

    delta, new_m, new_v = {}, {}, {}
    big = [n for n, _ in PACK_ROWS]
    for n in big:
        shp = weights[n].shape
        two_d = lambda a: a.reshape(shp[1], shp[2])
        d_, m_, v_ = _adamw(two_d(weights[n]), two_d(grads[n]), two_d(mom_m[n]), two_d(mom_v[n]), f"adamw_{n}")
        delta[n], new_m[n], new_v[n] = d_.reshape(shp), m_.reshape(shp), v_.reshape(shp)
    small_names = [n for n in order if n not in big]

    def pack_small(src):
        rows_ = []
        for n in small_names:
            flat = src[n].reshape(1, -1)
            width = -(-flat.shape[1] // 1024) * 1024
            rows_.append(jnp.pad(flat, ((0, 0), (0, width - flat.shape[1]))).reshape(-1, 1024))
        packed_ = jnp.concatenate(rows_, axis=0)
        return _pad_rows(packed_, -(-packed_.shape[0] // 8) * 8)

    d_s, m_s, v_s = _adamw(pack_small(weights), pack_small(grads), pack_small(mom_m), pack_small(mom_v), "adamw_small")
    r0 = 0
    for n in small_names:
        size = weights[n].size
        nrow = -(-size // 1024)
        for dst, src in ((delta, d_s), (new_m, m_s), (new_v, v_s)):
            dst[n] = src[r0:r0 + nrow].reshape(-1)[:size].reshape(weights[n].shape)
        r0 += nrow

    loss = lax.psum(loss_local[0, 0], MESH_AXES)
    return (loss, dx[None], *[grads[n] for n in order], *[delta[n] for n in order], *[new_m[n] for n in order],
            *[new_v[n] for n in order])
```

```python
import functools
import math

import jax
import jax.numpy as jnp
from jax import lax
from jax.experimental import pallas as pl
from jax.experimental.pallas import tpu as pltpu

F32 = jnp.float32
BF16 = jnp.bfloat16
SDS = jax.ShapeDtypeStruct
_PALLAS_CALL = pl.pallas_call

D_MODEL = 1024
HEAD_DIM = 64
WIN_HALF = 128
DIL_PAIRS = ((128, 1), (512, 4), (2048, 16))
DIL_SIDE = 64
ROT_DIM = 16
ROPE_THETA = 500000.0
MEM_LEN = 256
X_HEADS = 4
X_HEAD_DIM = 256
D_FF = 2816
IN_WIDTH = 5376
ZW = 5632
Z_QB, Z_KB, Z_VB, Z_QA, Z_KA, Z_VA = 0, 1536, 3072, 4608, 5120, 5248
ALPHA = (2.0) ** 0.25
LN_EPS = 1e-5
NEG_INF = -1e30
ADAM_LR, ADAM_B1, ADAM_B2, ADAM_EPS, ADAM_WD, ADAM_STEP = 0.001, 0.9, 0.999, 1e-08, 0.01, 10
N_DEV = 8
MESH_AXES = ("x", "y", "c")
VMEM_LIMIT_BYTES = 52 * 1024 * 1024
ATTN_TQ = 256
TABW = 384

PACK_ROWS = (("w_in", 672), ("w_gate", 352), ("w_up", 352), ("w_mix_out", 128), ("w_xq", 128), ("w_xk", 128),
             ("w_xv", 128), ("w_xo", 128), ("w_down", 352))
SMALL_ROWS = 24


def _pick(n, cands):
    for c in cands:
        if n % c == 0:
            return c
    return n


def _pcall(body, *, name, out_shape, grid=None, in_specs=None, out_specs=None, scratch_shapes=(), dims=None,
           aliases=None):
    kw = {}
    if grid is not None:
        kw["grid"] = grid
    if in_specs is not None:
        kw["in_specs"] = in_specs
    if out_specs is not None:
        kw["out_specs"] = out_specs
    if aliases:
        kw["input_output_aliases"] = aliases
    return _PALLAS_CALL(
        body, name=name, out_shape=out_shape, scratch_shapes=list(scratch_shapes),
        compiler_params=pltpu.CompilerParams(dimension_semantics=dims, vmem_limit_bytes=VMEM_LIMIT_BYTES), **kw)


def _mm(a, b, *, trans_b, out_dtype, name, addends=(), coefs=(), after=None):
    m, k = a.shape
    n = b.shape[0] if trans_b else b.shape[1]
    tm = _pick(m, (512, 256))
    tn = _pick(n, (512, 1408, 256, 128))
    n_add = len(addends)
    dn = (((1,), (1,)), ((), ())) if trans_b else (((1,), (0,)), ((), ()))
    extra = [] if after is None else [after]

    def body(a_ref, b_ref, *rest):
        o_ref = rest[n_add + len(extra)]
        acc = lax.dot_general(a_ref[...].astype(BF16), b_ref[...].astype(BF16), dn, preferred_element_type=F32)
        for r_ref, c in zip(rest[:n_add], coefs):
            acc = acc + (r_ref[...] if c == 1.0 else c * r_ref[...])
        o_ref[...] = acc.astype(out_dtype)

    b_spec = pl.BlockSpec((tn, k), lambda i, j: (j, 0)) if trans_b else pl.BlockSpec((k, tn), lambda i, j: (0, j))
    in_specs = [pl.BlockSpec((tm, k), lambda i, j: (i, 0)), b_spec]
    in_specs += [pl.BlockSpec((tm, tn), lambda i, j: (i, j)) for _ in addends]
    in_specs += [pl.BlockSpec((8, 128), lambda i, j: (0, 0)) for _ in extra]
    return _pcall(body, name=name, out_shape=SDS((m, n), out_dtype), grid=(m // tm, n // tn), in_specs=in_specs,
                  out_specs=pl.BlockSpec((tm, tn), lambda i, j: (i, j)),
                  dims=("parallel", "parallel"))(a, b, *addends, *extra)


def _mm_tn(a, b, *, name):
    s, m = a.shape
    n = b.shape[1]
    tm = _pick(m, (768, 1408, 1024, 512, 256, 128))
    tk = _pick(s, (512, 256))
    nk = s // tk

    def body(a_ref, b_ref, o_ref, acc_ref):
        kk = pl.program_id(1)

        @pl.when(kk == 0)
        def _():
            acc_ref[...] = jnp.zeros_like(acc_ref)

        acc_ref[...] += lax.dot_general(a_ref[...].astype(BF16), b_ref[...].astype(BF16), (((0,), (0,)), ((), ())),
                                        preferred_element_type=F32)

        @pl.when(kk == nk - 1)
        def _():
            o_ref[...] = acc_ref[...]

    return _pcall(body, name=name, out_shape=SDS((m, n), F32), grid=(m // tm, nk),
                  in_specs=[pl.BlockSpec((tk, tm), lambda i, kk: (kk, i)), pl.BlockSpec((tk, n), lambda i, kk: (kk, 0))],
                  out_specs=pl.BlockSpec((tm, n), lambda i, kk: (i, 0)), scratch_shapes=[pltpu.VMEM((tm, n), F32)],
                  dims=("parallel", "arbitrary"))(a, b)


def _rope_lane_consts():
    lane = jnp.arange(128)
    j = lane % HEAD_DIM
    inv_freq = ROPE_THETA ** (-jnp.arange(0, ROT_DIM, 2, dtype=F32) / ROT_DIM)
    freq = jnp.where(j < ROT_DIM, inv_freq[j % (ROT_DIM // 2)], 0.0).astype(F32)
    lo = (j < ROT_DIM // 2).astype(F32)
    hi = ((j >= ROT_DIM // 2) & (j < ROT_DIM)).astype(F32)
    return jnp.stack([freq, lo, hi] + [jnp.zeros((128,), F32)] * 5)


def _rope_tables(posf):
    s = posf.shape[0]
    tm = _pick(s, (1024, 512))

    def body(p_ref, c_ref, o_ref):
        ang = p_ref[...] * c_ref[0:1, :]
        lo = c_ref[1:2, :]
        hi = c_ref[2:3, :]
        cs = jnp.cos(ang)
        sn = jnp.sin(ang)
        o_ref[...] = jnp.concatenate([jnp.where(lo + hi > 0.0, cs, 1.0), -sn * lo, sn * hi], axis=1)

    return _pcall(body, name="rope_tables", out_shape=SDS((s, TABW), F32), grid=(s // tm,),
                  in_specs=[pl.BlockSpec((tm, 1), lambda i: (i, 0)), pl.BlockSpec((8, 128), lambda i: (0, 0))],
                  out_specs=pl.BlockSpec((tm, TABW), lambda i: (i, 0)), dims=("parallel",))(posf, _rope_lane_consts())


def _rope_apply(x, tab, sign):
    w = x.shape[1]
    rep = w // 128
    c = jnp.tile(tab[:, 0:128], (1, rep)) if rep > 1 else tab[:, 0:128]
    a = jnp.tile(tab[:, 128:256], (1, rep)) if rep > 1 else tab[:, 128:256]
    b = jnp.tile(tab[:, 256:384], (1, rep)) if rep > 1 else tab[:, 256:384]
    up = pltpu.roll(x, w - 8, 1)
    dn = pltpu.roll(x, 8, 1)
    if sign > 0:
        return x * c + up * a + dn * b
    return x * c - up * a - dn * b


def _rope_mask_row():
    col = jnp.arange(ZW)
    m = (col < Z_VB) | ((col >= Z_QA) & (col < Z_VA))
    return m.astype(F32).reshape(1, ZW)


def _proj_rope(h0, w_t, tab):
    s = h0.shape[0]
    tm = _pick(s, (512,))
    tn = 256

    def body(a_ref, w_ref, t_ref, m_ref, o_ref):
        z = lax.dot_general(a_ref[...].astype(BF16), w_ref[...], (((1,), (1,)), ((), ())), preferred_element_type=F32)
        zr = _rope_apply(z, t_ref[...], 1)
        msk = m_ref[...]
        o_ref[...] = (z + msk * (zr - z)).astype(BF16)

    return _pcall(body, name="proj_rope", out_shape=SDS((s, ZW), BF16), grid=(s // tm, IN_WIDTH // tn),
                  in_specs=[pl.BlockSpec((tm, D_MODEL), lambda i, j: (i, 0)), pl.BlockSpec((tn, D_MODEL), lambda i, j: (j, 0)),
                            pl.BlockSpec((tm, TABW), lambda i, j: (i, 0)), pl.BlockSpec((1, tn), lambda i, j: (0, j))],
                  out_specs=pl.BlockSpec((tm, tn), lambda i, j: (i, j)),
                  dims=("parallel", "parallel"))(h0, w_t, tab, _rope_mask_row())


def _band_specs(sd, blk, tq, width, per_tok, cb):
    r = tq // blk
    nbk = sd // blk
    prev = pl.BlockSpec((blk, width), lambda c, j: (jnp.maximum(j * r - 1, 0), c * per_tok + cb))
    cur = pl.BlockSpec((tq, width), lambda c, j: (j, c * per_tok + cb))
    nxt = pl.BlockSpec((blk, width), lambda c, j: (jnp.minimum((j + 1) * r, nbk - 1), c * per_tok + cb))
    return [prev, cur, nxt]


def _band_bias(j, blk, tq, sd, rows_are_tile):
    w = tq + 2 * blk
    shape = (tq, w) if rows_are_tile else (w, tq)
    tile_pos = j * tq + lax.broadcasted_iota(jnp.int32, shape, 0 if rows_are_tile else 1)
    wide_pos = j * tq - blk + lax.broadcasted_iota(jnp.int32, shape, 1 if rows_are_tile else 0)
    ok = (jnp.abs(tile_pos - wide_pos) <= blk) & (wide_pos >= 0) & (wide_pos < sd)
    return jnp.where(ok, 0.0, NEG_INF)


_NT = (((1,), (1,)), ((), ()))
_NN = (((1,), (0,)), ((), ()))
_TN = (((0,), (0,)), ((), ()))


def _banded_fwd(zr, sink, *, d, blk, tq, qw, kw, qcb, kcb, vcb, heads, name):
    s = zr.shape[0]
    sd = s // d
    tq = min(tq, sd)
    zv = zr.reshape(sd, d * ZW)
    has_sink = sink is not None
    scale = HEAD_DIM ** -0.5

    def body(q_ref, kp, kc, kn, vp, vc, vn, *rest):
        if has_sink:
            sink_ref, o_ref, lse_ref = rest
        else:
            o_ref, lse_ref = rest
        j = pl.program_id(1)
        q = q_ref[...] * scale
        k = jnp.concatenate([kp[...], kc[...], kn[...]], axis=0)
        v = jnp.concatenate([vp[...], vc[...], vn[...]], axis=0)
        bias = _band_bias(j, blk, tq, sd, True)
        outs, lses = [], []
        for ql, kl, vl, si in heads:
            sc = lax.dot_general(q[:, ql:ql + HEAD_DIM], k[:, kl:kl + HEAD_DIM], _NT, preferred_element_type=F32) + bias
            m = jnp.max(sc, axis=-1, keepdims=True)
            if has_sink:
                m = jnp.maximum(m, sink_ref[0, si])
            p = jnp.exp(sc - m)
            den = jnp.sum(p, axis=-1, keepdims=True)
            if has_sink:
                den = den + jnp.exp(sink_ref[0, si] - m)
            o = lax.dot_general(p.astype(BF16), v[:, vl:vl + HEAD_DIM], _NN, preferred_element_type=F32) / den
            outs.append(o)
            lses.append(jnp.broadcast_to(m + jnp.log(den), (tq, HEAD_DIM)))
        o_ref[...] = jnp.concatenate(outs, axis=1)
        lse_ref[...] = jnp.concatenate(lses, axis=1)

    in_specs = ([pl.BlockSpec((tq, qw), lambda c, j: (j, c * (ZW // qw) + qcb))]
                + _band_specs(sd, blk, tq, kw, ZW // kw, kcb) + _band_specs(sd, blk, tq, kw, ZW // kw, vcb))
    args = [zv] * 7
    if has_sink:
        in_specs.append(pl.BlockSpec(memory_space=pltpu.SMEM))
        args.append(sink)
    o_spec = pl.BlockSpec((tq, qw), lambda c, j: (j, c))
    o, lse = _pcall(body, name=name, out_shape=(SDS((sd, d * qw), F32), SDS((sd, d * qw), F32)), grid=(d, sd // tq),
                    in_specs=in_specs, out_specs=(o_spec, o_spec), dims=("parallel", "parallel"))(*args)
    return o.reshape(s, qw), lse.reshape(s, qw)


def _banded_bwd(zr, o, lse, do, tab, sink, *, d, blk, tq, qw, kw, qcb, kcb, vcb, heads, kv_heads, name):
    s = zr.shape[0]
    sd = s // d
    tq = min(tq, sd)
    zv = zr.reshape(sd, d * ZW)
    ov, lv, dov = (t.reshape(sd, d * qw) for t in (o, lse, do))
    tv = tab.reshape(sd, d * TABW)
    has_sink = sink is not None
    scale = HEAD_DIM ** -0.5
    kvw = HEAD_DIM * len(kv_heads)
    b0, b1 = blk, blk + tq

    def body(*refs):
        q3 = jnp.concatenate([r[...] for r in refs[0:3]], axis=0) * scale
        k3 = jnp.concatenate([r[...] for r in refs[3:6]], axis=0)
        v3 = jnp.concatenate([r[...] for r in refs[6:9]], axis=0)
        o3 = jnp.concatenate([r[...] for r in refs[9:12]], axis=0)
        l3 = jnp.concatenate([r[...] for r in refs[12:15]], axis=0)
        do3 = jnp.concatenate([r[...] for r in refs[15:18]], axis=0)
        t_ref = refs[18]
        if has_sink:
            sink_ref, dq_ref, dkv_ref, dsink_ref = refs[19:]
        else:
            dq_ref, dkv_ref = refs[19:]
        j = pl.program_id(1)
        bias1 = _band_bias(j, blk, tq, sd, True)
        bias2 = _band_bias(j, blk, tq, sd, False)
        dqs = []
        dks = [None] * len(kv_heads)
        dvs = [None] * len(kv_heads)
        dsink_row = jnp.zeros((1, 128), F32)
        lane = lax.broadcasted_iota(jnp.int32, (1, 128), 1)
        for ql, kl, vl, si in heads:
            kvi = kv_heads.index((kl, vl))
            qh3 = q3[:, ql:ql + HEAD_DIM]
            kh3 = k3[:, kl:kl + HEAD_DIM]
            vh3 = v3[:, vl:vl + HEAD_DIM]
            doh3 = do3[:, ql:ql + HEAD_DIM]
            delta3 = jnp.sum(doh3 * o3[:, ql:ql + HEAD_DIM], axis=-1, keepdims=True)
            lse3 = l3[:, ql:ql + 1]
            dob3 = doh3.astype(BF16)
            sc = lax.dot_general(qh3[b0:b1], kh3, _NT, preferred_element_type=F32) + bias1
            p = jnp.exp(sc - lse3[b0:b1])
            dp = lax.dot_general(dob3[b0:b1], vh3, _NT, preferred_element_type=F32)
            ds = p * (dp - delta3[b0:b1])
            dqs.append(lax.dot_general(ds.astype(BF16), kh3, _NN, preferred_element_type=F32) * scale)
            if has_sink:
                psink = jnp.exp(sink_ref[0, si] - lse3[b0:b1])
                dsink_row = dsink_row + jnp.where(lane == si, -jnp.sum(psink * delta3[b0:b1]), 0.0)
            sc2 = lax.dot_general(qh3, kh3[b0:b1], _NT, preferred_element_type=F32) + bias2
            p2 = jnp.exp(sc2 - lse3)
            dv = lax.dot_general(p2.astype(BF16), dob3, _TN, preferred_element_type=F32)
            dp2 = lax.dot_general(dob3, vh3[b0:b1], _NT, preferred_element_type=F32)
            ds2 = p2 * (dp2 - delta3)
            dk = lax.dot_general(ds2.astype(BF16), qh3, _TN, preferred_element_type=F32)
            dks[kvi] = dk if dks[kvi] is None else dks[kvi] + dk
            dvs[kvi] = dv if dvs[kvi] is None else dvs[kvi] + dv
        tabv = t_ref[...]
        dq_ref[...] = _rope_apply(jnp.concatenate(dqs, axis=1), tabv, -1).astype(BF16)
        dk_all = jnp.concatenate(dks, axis=1) if len(dks) > 1 else dks[0]
        dv_all = jnp.concatenate(dvs, axis=1) if len(dvs) > 1 else dvs[0]
        dkv_ref[...] = jnp.concatenate([_rope_apply(dk_all, tabv, -1), dv_all], axis=1).astype(BF16)
        if has_sink:
            first = (pl.program_id(0) == 0) & (j == 0)

            @pl.when(first)
            def _():
                dsink_ref[...] = jnp.zeros_like(dsink_ref)

            dsink_ref[0:1, :] += dsink_row

    in_specs = (_band_specs(sd, blk, tq, qw, ZW // qw, qcb) + _band_specs(sd, blk, tq, kw, ZW // kw, kcb)
                + _band_specs(sd, blk, tq, kw, ZW // kw, vcb) + _band_specs(sd, blk, tq, qw, 1, 0) * 3
                + [pl.BlockSpec((tq, TABW), lambda c, j: (j, c))])
    args = [zv] * 9 + [ov] * 3 + [lv] * 3 + [dov] * 3 + [tv]
    out_shape = [SDS((sd, d * qw), BF16), SDS((sd, d * 2 * kvw), BF16)]
    out_specs = [pl.BlockSpec((tq, qw), lambda c, j: (j, c)), pl.BlockSpec((tq, 2 * kvw), lambda c, j: (j, c))]
    dims = ("parallel", "parallel")
    if has_sink:
        in_specs.append(pl.BlockSpec(memory_space=pltpu.SMEM))
        args.append(sink)
        out_shape.append(SDS((8, 128), F32))
        out_specs.append(pl.BlockSpec((8, 128), lambda c, j: (0, 0)))
        dims = ("arbitrary", "arbitrary")
    res = _pcall(body, name=name, out_shape=tuple(out_shape), grid=(d, sd // tq), in_specs=in_specs,
                 out_specs=tuple(out_specs), dims=dims)(*args)
    dq = res[0].reshape(s, qw)
    dkv = res[1].reshape(s, 2 * kvw)
    return (dq, dkv, res[2]) if has_sink else (dq, dkv)


_WIN_HEADS = tuple((h * HEAD_DIM, (h // 4) * HEAD_DIM, 128 + (h // 4) * HEAD_DIM, h) for h in range(8))
_WIN_KV = ((0, 128), (64, 192))
_WIN_CFG = dict(d=1, blk=WIN_HALF, tq=ATTN_TQ, qw=512, kw=256, qcb=Z_QA // 512, kcb=Z_KA // 256, vcb=Z_KA // 256,
                heads=_WIN_HEADS)
_DIL_HEADS = tuple((h * HEAD_DIM, h * HEAD_DIM, h * HEAD_DIM, h) for h in range(8))
_DIL_KV = tuple((h * HEAD_DIM, h * HEAD_DIM) for h in range(8))


def _dil_cfg(gi):
    return dict(d=DIL_PAIRS[gi][1], blk=DIL_SIDE, tq=ATTN_TQ, qw=512, kw=512, qcb=Z_QB // 512 + gi,
                kcb=Z_KB // 512 + gi, vcb=Z_VB // 512 + gi, heads=_DIL_HEADS)


def _mix_norm_fwd(oa, ogs, lgs, g_win, g_dil):
    s = oa.shape[0]
    tm = _pick(s, (512,))

    def body(oa_ref, o0, o1, o2, l0, l1, l2, gw_ref, gd_ref, mixed_ref, ob_ref, lb_ref):
        la, lb, lc = l0[...], l1[...], l2[...]
        mx = jnp.maximum(jnp.maximum(la, lb), lc)
        ea, eb, ec = jnp.exp(la - mx), jnp.exp(lb - mx), jnp.exp(lc - mx)
        den = ea + eb + ec
        ob = (ea / den) * o0[...] + (eb / den) * o1[...] + (ec / den) * o2[...]
        ob_ref[...] = ob
        lb_ref[...] = mx + jnp.log(den)
        a = oa_ref[...]
        ra = lax.rsqrt(jnp.mean(a * a, axis=-1, keepdims=True) + LN_EPS)
        rb = lax.rsqrt(jnp.mean(ob * ob, axis=-1, keepdims=True) + LN_EPS)
        mixed_ref[...] = jnp.concatenate([a * ra * gw_ref[...], ob * rb * gd_ref[...]], axis=1).astype(BF16)

    row = pl.BlockSpec((tm, 512), lambda i: (i, 0))
    vec = pl.BlockSpec((1, 512), lambda i: (0, 0))
    return _pcall(body, name="mix_norm_fwd", out_shape=(SDS((s, 1024), BF16), SDS((s, 512), F32), SDS((s, 512), F32)),
                  grid=(s // tm,), in_specs=[row] * 7 + [vec, vec],
                  out_specs=(pl.BlockSpec((tm, 1024), lambda i: (i, 0)), row, row),
                  dims=("parallel",))(oa, *ogs, *lgs, g_win, g_dil)


def _mix_norm_bwd(oa, ob, dmixed, g_win, g_dil):
    s = oa.shape[0]
    tm = _pick(s, (512,))
    nt = s // tm

    def body(oa_ref, ob_ref, dm_ref, gw_ref, gd_ref, doa_ref, dob_ref, dgw_ref, dgd_ref, acc_w, acc_d):
        i = pl.program_id(0)

        @pl.when(i == 0)
        def _():
            acc_w[...] = jnp.zeros_like(acc_w)
            acc_d[...] = jnp.zeros_like(acc_d)

        dm = dm_ref[...]
        for x_ref, g_ref, dy, dx_ref, acc in ((oa_ref, gw_ref, dm[:, :512], doa_ref, acc_w),
                                              (ob_ref, gd_ref, dm[:, 512:], dob_ref, acc_d)):
            x = x_ref[...]
            r = lax.rsqrt(jnp.mean(x * x, axis=-1, keepdims=True) + LN_EPS)
            dyg = dy * g_ref[...]
            dx_ref[...] = r * dyg - x * (r * r * r) * jnp.mean(dyg * x, axis=-1, keepdims=True)
            acc[...] += jnp.sum((dy * x * r).reshape(tm // 8, 8, 512), axis=0)

        @pl.when(i == nt - 1)
        def _():
            dgw_ref[...] = jnp.sum(acc_w[...], axis=0, keepdims=True)
            dgd_ref[...] = jnp.sum(acc_d[...], axis=0, keepdims=True)

    row = pl.BlockSpec((tm, 512), lambda i: (i, 0))
    vec = pl.BlockSpec((1, 512), lambda i: (0, 0))
    return _pcall(body, name="mix_norm_bwd",
                  out_shape=(SDS((s, 512), F32), SDS((s, 512), F32), SDS((1, 512), F32), SDS((1, 512), F32)),
                  grid=(nt,), in_specs=[row, row, pl.BlockSpec((tm, 1024), lambda i: (i, 0)), vec, vec],
                  out_specs=(row, row, vec, vec), scratch_shapes=[pltpu.VMEM((8, 512), F32), pltpu.VMEM((8, 512), F32)],
                  dims=("arbitrary",))(oa, ob, dmixed, g_win, g_dil)


def _ln_fwd(a, r, g, b, ca, name):
    s = a.shape[0]
    tm = _pick(s, (512, 256))
    has_r = r is not None

    def body(*refs):
        a_ref = refs[0]
        r_ref = refs[1] if has_r else None
        g_ref, b_ref, o_ref = refs[1 + has_r:]
        u = a_ref[...] if ca == 1.0 else ca * a_ref[...]
        if has_r:
            u = u + r_ref[...]
        mu = jnp.mean(u, axis=-1, keepdims=True)
        xc = u - mu
        var = jnp.mean(xc * xc, axis=-1, keepdims=True)
        o_ref[...] = xc * lax.rsqrt(var + LN_EPS) * g_ref[...] + b_ref[...]

    row = pl.BlockSpec((tm, D_MODEL), lambda i: (i, 0))
    vec = pl.BlockSpec((1, D_MODEL), lambda i: (0, 0))
    args = [a] + ([r] if has_r else []) + [g, b]
    return _pcall(body, name=name, out_shape=SDS((s, D_MODEL), F32), grid=(s // tm,),
                  in_specs=[row] * (1 + has_r) + [vec, vec], out_specs=row, dims=("parallel",))(*args)


def _ln_bwd(a, r, dy, g, b, ca, name, loss_mode=False):
    s = a.shape[0]
    tm = _pick(s, (512, 256))
    nt = s // tm
    has_r = r is not None

    def body(*refs):
        a_ref = refs[0]
        r_ref = refs[1] if has_r else None
        dy_ref, g_ref, b_ref = refs[1 + has_r:4 + has_r]
        outs = refs[4 + has_r:]
        if loss_mode:
            du_ref, dg_ref, db_ref, loss_ref, acc_g, acc_b, acc_l = outs
        else:
            du_ref, dg_ref, db_ref, acc_g, acc_b = outs
        i = pl.program_id(0)

        @pl.when(i == 0)
        def _():
            acc_g[...] = jnp.zeros_like(acc_g)
            acc_b[...] = jnp.zeros_like(acc_b)
            if loss_mode:
                acc_l[...] = jnp.zeros_like(acc_l)

        u = a_ref[...] if ca == 1.0 else ca * a_ref[...]
        if has_r:
            u = u + r_ref[...]
        mu = jnp.mean(u, axis=-1, keepdims=True)
        xc = u - mu
        var = jnp.mean(xc * xc, axis=-1, keepdims=True)
        rstd = lax.rsqrt(var + LN_EPS)
        xhat = xc * rstd
        gv = g_ref[...]
        if loss_mode:
            err = (xhat * gv + b_ref[...]) - dy_ref[...]
            acc_l[...] += jnp.sum((err * err).reshape(tm // 8, 8, D_MODEL), axis=0)
            dyv = err * (1.0 / D_MODEL)
        else:
            dyv = dy_ref[...]
        dxh = dyv * gv
        du_ref[...] = rstd * (dxh - jnp.mean(dxh, axis=-1, keepdims=True)
                              - xhat * jnp.mean(dxh * xhat, axis=-1, keepdims=True))
        acc_g[...] += jnp.sum((dyv * xhat).reshape(tm // 8, 8, D_MODEL), axis=0)
        acc_b[...] += jnp.sum(dyv.reshape(tm // 8, 8, D_MODEL), axis=0)

        @pl.when(i == nt - 1)
        def _():
            dg_ref[...] = jnp.sum(acc_g[...], axis=0, keepdims=True)
            db_ref[...] = jnp.sum(acc_b[...], axis=0, keepdims=True)
            if loss_mode:
                tot = jnp.sum(jnp.sum(acc_l[...], axis=0, keepdims=True), axis=1, keepdims=True)
                loss_ref[...] = tot * (0.5 / D_MODEL)

    row = pl.BlockSpec((tm, D_MODEL), lambda i: (i, 0))
    vec = pl.BlockSpec((1, D_MODEL), lambda i: (0, 0))
    out_shape = [SDS((s, D_MODEL), F32), SDS((1, D_MODEL), F32), SDS((1, D_MODEL), F32)]
    out_specs = [row, vec, vec]
    scratch = [pltpu.VMEM((8, D_MODEL), F32), pltpu.VMEM((8, D_MODEL), F32)]
    if loss_mode:
        out_shape.append(SDS((1, 1), F32))
        out_specs.append(pl.BlockSpec((1, 1), lambda i: (0, 0)))
        scratch.append(pltpu.VMEM((8, D_MODEL), F32))
    args = [a] + ([r] if has_r else []) + [dy, g, b]
    return _pcall(body, name=name, out_shape=tuple(out_shape), grid=(nt,), in_specs=[row] * (2 + has_r) + [vec, vec],
                  out_specs=tuple(out_specs), scratch_shapes=scratch, dims=("arbitrary",))(*args)


def _xattn_fwd(q, k, v):
    s = q.shape[0]
    tq = _pick(s, (512,))
    scale = X_HEAD_DIM ** -0.5

    def body(q_ref, k_ref, v_ref, o_ref):
        qv, kv, vv = q_ref[...], k_ref[...], v_ref[...]
        outs = []
        for h in range(X_HEADS):
            sl = slice(h * X_HEAD_DIM, (h + 1) * X_HEAD_DIM)
            sc = lax.dot_general(qv[:, sl], kv[:, sl], _NT, preferred_element_type=F32) * scale
            e = jnp.exp(sc - jnp.max(sc, axis=-1, keepdims=True))
            p = e / jnp.sum(e, axis=-1, keepdims=True)
            outs.append(lax.dot_general(p.astype(BF16), vv[:, sl], _NN, preferred_element_type=F32))
        o_ref[...] = jnp.concatenate(outs, axis=1)

    row = pl.BlockSpec((tq, D_MODEL), lambda i: (i, 0))
    full = pl.BlockSpec((MEM_LEN, D_MODEL), lambda i: (0, 0))
    return _pcall(body, name="xattn_fwd", out_shape=SDS((s, D_MODEL), F32), grid=(s // tq,), in_specs=[row, full, full],
                  out_specs=row, dims=("parallel",))(q, k, v)


def _xattn_bwd(q, k, v, o, do):
    s = q.shape[0]
    tq = _pick(s, (512,))
    scale = X_HEAD_DIM ** -0.5

    def body(q_ref, k_ref, v_ref, o_ref, do_ref, dq_ref, dk_ref, dv_ref):
        i = pl.program_id(0)

        @pl.when(i == 0)
        def _():
            dk_ref[...] = jnp.zeros_like(dk_ref)
            dv_ref[...] = jnp.zeros_like(dv_ref)

        qv, kv, vv, ov, dov = q_ref[...], k_ref[...], v_ref[...], o_ref[...], do_ref[...]
        dqs, dks, dvs = [], [], []
        for h in range(X_HEADS):
            sl = slice(h * X_HEAD_DIM, (h + 1) * X_HEAD_DIM)
            sc = lax.dot_general(qv[:, sl], kv[:, sl], _NT, preferred_element_type=F32) * scale
            e = jnp.exp(sc - jnp.max(sc, axis=-1, keepdims=True))
            p = e / jnp.sum(e, axis=-1, keepdims=True)
            doh = dov[:, sl]
            dob = doh.astype(BF16)
            delta = jnp.sum(doh * ov[:, sl], axis=-1, keepdims=True)
            dvs.append(lax.dot_general(p.astype(BF16), dob, _TN, preferred_element_type=F32))
            dp = lax.dot_general(dob, vv[:, sl], _NT, preferred_element_type=F32)
            ds = (p * (dp - delta)).astype(BF16)
            dqs.append(lax.dot_general(ds, kv[:, sl], _NN, preferred_element_type=F32) * scale)
            dks.append(lax.dot_general(ds, qv[:, sl], _TN, preferred_element_type=F32) * scale)
        dq_ref[...] = jnp.concatenate(dqs, axis=1).astype(BF16)
        dk_ref[...] += jnp.concatenate(dks, axis=1)
        dv_ref[...] += jnp.concatenate(dvs, axis=1)

    row = pl.BlockSpec((tq, D_MODEL), lambda i: (i, 0))
    full = pl.BlockSpec((MEM_LEN, D_MODEL), lambda i: (0, 0))
    return _pcall(body, name="xattn_bwd",
                  out_shape=(SDS((s, D_MODEL), BF16), SDS((MEM_LEN, D_MODEL), F32), SDS((MEM_LEN, D_MODEL), F32)),
                  grid=(s // tq,), in_specs=[row, full, full, row, row], out_specs=(row, full, full),
                  dims=("arbitrary",))(q, k, v, o, do)


_SQRT_HALF = 0.7071067811865476
_INV_SQRT_2PI = 0.3989422804014327


def _halo_specs(s, tm, width):
    n8 = s // 8
    r8 = tm // 8
    prev = pl.BlockSpec((8, width), lambda i: (jnp.maximum(i * r8 - 1, 0), 0))
    nxt = pl.BlockSpec((8, width), lambda i: (jnp.minimum((i + 1) * r8, n8 - 1), 0))
    return prev, nxt


def _shifted(x, prev8, next8, i, nt):
    tm = x.shape[0]
    row = lax.broadcasted_iota(jnp.int32, x.shape, 0)
    first = jnp.where(i == 0, 0.0, 1.0) * prev8[7:8, :]
    last = jnp.where(i == nt - 1, 0.0, 1.0) * next8[0:1, :]
    xm1 = jnp.where(row == 0, first, pltpu.roll(x, 1, 0))
    xp1 = jnp.where(row == tm - 1, last, pltpu.roll(x, tm - 1, 0))
    return xm1, xp1


def _glu_fwd(g, up, cw, cb):
    s = g.shape[0]
    tm = _pick(s, (256,))
    nt = s // tm

    def body(g_ref, gp_ref, gn_ref, up_ref, cw_ref, cb_ref, act_ref):
        i = pl.program_id(0)
        gv = g_ref[...]
        gm1, gp1 = _shifted(gv, gp_ref[...], gn_ref[...], i, nt)
        gc = gm1 * cw_ref[0:1, :] + gv * cw_ref[1:2, :] + gp1 * cw_ref[2:3, :] + cb_ref[...]
        gelu = 0.5 * gc * (1.0 + lax.erf(gc * _SQRT_HALF))
        act_ref[...] = (gelu * up_ref[...]).astype(BF16)

    row = pl.BlockSpec((tm, D_FF), lambda i: (i, 0))
    prev, nxt = _halo_specs(s, tm, D_FF)
    return _pcall(body, name="glu_fwd", out_shape=SDS((s, D_FF), BF16), grid=(nt,),
                  in_specs=[row, prev, nxt, row, pl.BlockSpec((8, D_FF), lambda i: (0, 0)),
                            pl.BlockSpec((1, D_FF), lambda i: (0, 0))],
                  out_specs=row, dims=("parallel",))(g, g, g, up, cw, cb)


def _glu_bwd(g, up, dact, cw, cb):
    s = g.shape[0]
    tm = _pick(s, (256,))
    nt = s // tm

    def body(g_ref, gp_ref, gn_ref, up_ref, da_ref, cw_ref, cb_ref, dgc_ref, dup_ref, dcw_ref, dcb_ref, a0, a1, a2, a3):
        i = pl.program_id(0)

        @pl.when(i == 0)
        def _():
            for a in (a0, a1, a2, a3):
                a[...] = jnp.zeros_like(a)

        gv = g_ref[...]
        gm1, gp1 = _shifted(gv, gp_ref[...], gn_ref[...], i, nt)
        gc = gm1 * cw_ref[0:1, :] + gv * cw_ref[1:2, :] + gp1 * cw_ref[2:3, :] + cb_ref[...]
        cdf = 0.5 * (1.0 + lax.erf(gc * _SQRT_HALF))
        pdf = jnp.exp(-0.5 * gc * gc) * _INV_SQRT_2PI
        da = da_ref[...]
        dup_ref[...] = (da * (gc * cdf)).astype(BF16)
        dgc = da * up_ref[...] * (cdf + gc * pdf)
        dgc_ref[...] = dgc

        def fold(t):
            return jnp.sum(t.reshape(tm // 8, 8, D_FF), axis=0)

        a0[...] += fold(dgc * gm1)
        a1[...] += fold(dgc * gv)
        a2[...] += fold(dgc * gp1)
        a3[...] += fold(dgc)

        @pl.when(i == nt - 1)
        def _():
            dcw_ref[...] = jnp.concatenate(
                [jnp.sum(a[...], axis=0, keepdims=True) for a in (a0, a1, a2)] + [jnp.zeros((5, D_FF), F32)], axis=0)
            dcb_ref[...] = jnp.sum(a3[...], axis=0, keepdims=True)

    row = pl.BlockSpec((tm, D_FF), lambda i: (i, 0))
    prev, nxt = _halo_specs(s, tm, D_FF)
    cw_spec = pl.BlockSpec((8, D_FF), lambda i: (0, 0))
    cb_spec = pl.BlockSpec((1, D_FF), lambda i: (0, 0))
    return _pcall(body, name="glu_bwd",
                  out_shape=(SDS((s, D_FF), F32), SDS((s, D_FF), BF16), SDS((8, D_FF), F32), SDS((1, D_FF), F32)),
                  grid=(nt,), in_specs=[row, prev, nxt, row, row, cw_spec, cb_spec],
                  out_specs=(row, row, cw_spec, cb_spec), scratch_shapes=[pltpu.VMEM((8, D_FF), F32)] * 4,
                  dims=("arbitrary",))(g, g, g, up, dact, cw, cb)


def _conv_bwd_input(dgc, cw):
    s = dgc.shape[0]
    tm = _pick(s, (256,))
    nt = s // tm

    def body(x_ref, xp_ref, xn_ref, cw_ref, o_ref):
        i = pl.program_id(0)
        xv = x_ref[...]
        xm1, xp1 = _shifted(xv, xp_ref[...], xn_ref[...], i, nt)
        o_ref[...] = (xp1 * cw_ref[0:1, :] + xv * cw_ref[1:2, :] + xm1 * cw_ref[2:3, :]).astype(BF16)

    row = pl.BlockSpec((tm, D_FF), lambda i: (i, 0))
    prev, nxt = _halo_specs(s, tm, D_FF)
    return _pcall(body, name="conv_bwd_input", out_shape=SDS((s, D_FF), BF16), grid=(nt,),
                  in_specs=[row, prev, nxt, pl.BlockSpec((8, D_FF), lambda i: (0, 0))], out_specs=row,
                  dims=("parallel",))(dgc, dgc, dgc, cw)


def _adamw(w, g, m, v, name):
    rows, cols = w.shape
    tr = _pick(rows, (256, 128, 64, 32, 16, 8))
    c1 = 1.0 - ADAM_B1 ** ADAM_STEP
    c2 = 1.0 - ADAM_B2 ** ADAM_STEP

    def body(w_ref, g_ref, m_ref, v_ref, d_ref, nm_ref, nv_ref):
        gv = g_ref[...]
        nm = ADAM_B1 * m_ref[...] + (1.0 - ADAM_B1) * gv
        nv = ADAM_B2 * v_ref[...] + (1.0 - ADAM_B2) * (gv * gv)
        d_ref[...] = -ADAM_LR * ((nm / c1) / (jnp.sqrt(nv / c2) + ADAM_EPS) + ADAM_WD * w_ref[...])
        nm_ref[...] = nm
        nv_ref[...] = nv

    blk = pl.BlockSpec((tr, cols), lambda i: (i, 0))
    return _pcall(body, name=name, out_shape=(SDS(w.shape, F32),) * 3, grid=(rows // tr,), in_specs=[blk] * 4,
                  out_specs=(blk,) * 3, dims=("parallel",))(w, g, m, v)


def _all_gather_rows(x_shard, *, name, in_vmem, sum_rows=False):
    m_per, n = x_shard.shape

    def body(x_ref, out_ref, *rest):
        if sum_rows:
            sum_ref, send_sems, recv_sems, local_sem = rest
        else:
            send_sems, recv_sems, local_sem = rest
        x, y, c = lax.axis_index("x"), lax.axis_index("y"), lax.axis_index("c")
        me, sibling = (x, y, c), (x, y, 1 - c)
        chips = [(1 - x, y), (x, 1 - y), (1 - x, 1 - y)]

        def rows(px, py, pc):
            return out_ref.at[pl.ds((4 * px + 2 * py + pc) * m_per, m_per), :]

        def copy(k, block, to, src=None):
            return pltpu.make_async_remote_copy(
                src_ref=rows(*block) if src is None else src, dst_ref=rows(*block), send_sem=send_sems.at[k],
                recv_sem=recv_sems.at[k], device_id=to, device_id_type=pl.DeviceIdType.MESH)

        mine = pltpu.make_async_copy(x_ref, rows(*me), local_sem)
        mine.start()
        first = [copy(0, me, sibling, src=x_ref)]
        first += [copy(1 + j, me, (*chip, c), src=x_ref) for j, chip in enumerate(chips)]
        for cp in first:
            cp.start()
        passed = [copy(4 + j, (*chip, c), sibling) for j, chip in enumerate(chips)]
        for j, chip in enumerate(chips):
            copy(1 + j, (*chip, c), me).wait_recv()
            passed[j].start()
        copy(0, sibling, me).wait_recv()
        for j, chip in enumerate(chips):
            copy(4 + j, (*chip, 1 - c), me).wait_recv()
        for cp in first + passed:
            cp.wait_send()
        mine.wait()
        if sum_rows:
            acc = out_ref[0:m_per, :]
            for dev in range(1, N_DEV):
                acc = acc + out_ref[dev * m_per:(dev + 1) * m_per, :]
            sum_ref[...] = acc

    space = pltpu.VMEM if in_vmem else pl.ANY
    out_shape = [SDS((N_DEV * m_per, n), x_shard.dtype)]
    out_specs = [pl.BlockSpec(memory_space=space)]
    if sum_rows:
        out_shape.append(SDS((m_per, n), x_shard.dtype))
        out_specs.append(pl.BlockSpec(memory_space=pltpu.VMEM))
    res = _PALLAS_CALL(
        body, name=name, out_shape=tuple(out_shape), in_specs=[pl.BlockSpec(memory_space=space)],
        out_specs=tuple(out_specs),
        scratch_shapes=[pltpu.SemaphoreType.DMA((7,)), pltpu.SemaphoreType.DMA((7,)), pltpu.SemaphoreType.DMA],
        compiler_params=pltpu.CompilerParams(vmem_limit_bytes=VMEM_LIMIT_BYTES),
    )(x_shard)
    return res if sum_rows else res[0]


_HBM = pl.BlockSpec(memory_space=pltpu.HBM)
_SEM = pl.BlockSpec(memory_space=pltpu.SEMAPHORE)
_SPLIT_PARAMS = dict(has_side_effects=pltpu.SideEffectType.DATAFLOW_SIDE_EFFECTING)


def _split_copies(src_ref, land_ref, send_sems, recv_sems, gather):
    x, y, c = lax.axis_index("x"), lax.axis_index("y"), lax.axis_index("c")
    copies = []
    for k in range(1, N_DEV):
        px = 1 - x if k & 4 else x
        py = 1 - y if k & 2 else y
        pc = 1 - c if k & 1 else c
        if gather:
            rows = src_ref.shape[0]
            src, dst = src_ref, land_ref.at[pl.ds((4 * x + 2 * y + c) * rows, rows), :]
        else:
            src, dst = src_ref.at[4 * px + 2 * py + pc], land_ref.at[k - 1]
        copies.append(pltpu.make_async_remote_copy(
            src_ref=src, dst_ref=dst, send_sem=send_sems.at[k - 1], recv_sem=recv_sems.at[k - 1],
            device_id=(px, py, pc), device_id_type=pl.DeviceIdType.MESH))
    return copies


def _exchange_start(src, land_shape, *, gather, name):
    def body(src_ref, land_ref, send_sems, recv_sems, src_thru, land_thru, token):
        for cp in _split_copies(src_ref, land_ref, send_sems, recv_sems, gather):
            cp.start()
        token[...] = jnp.zeros_like(token)

    land = pltpu.with_memory_space_constraint(lax.empty(land_shape, src.dtype), pltpu.HBM)
    return _PALLAS_CALL(
        body, name=name,
        out_shape=(pltpu.SemaphoreType.DMA((N_DEV - 1,)), pltpu.SemaphoreType.DMA((N_DEV - 1,)),
                   pltpu.HBM(src.shape, src.dtype), pltpu.HBM(land_shape, src.dtype), SDS((8, 128), F32)),
        in_specs=(_HBM, _HBM), out_specs=(_SEM, _SEM, _HBM, _HBM, pl.BlockSpec(memory_space=pltpu.VMEM)),
        input_output_aliases={0: 2, 1: 3}, compiler_params=pltpu.CompilerParams(**_SPLIT_PARAMS),
    )(pltpu.with_memory_space_constraint(src, pltpu.HBM), land)


def _exchange_wait(started, after, *, gather, name):
    send_sems, recv_sems, src_thru, land_thru, _ = started

    def body(src_ref, land_ref, send_sems, recv_sems, after_ref, src_out, land_out):
        copies = _split_copies(src_ref, land_ref, send_sems, recv_sems, gather)
        for cp in copies:
            cp.wait_send()
        for cp in copies:
            cp.wait_recv()

    return _PALLAS_CALL(
        body, name=name,
        out_shape=(pltpu.HBM(src_thru.shape, src_thru.dtype), pltpu.HBM(land_thru.shape, land_thru.dtype)),
        in_specs=(_HBM, _HBM, _SEM, _SEM, pl.BlockSpec(memory_space=pl.ANY)), out_specs=(_HBM, _HBM),
        input_output_aliases={0: 0, 1: 1}, compiler_params=pltpu.CompilerParams(**_SPLIT_PARAMS),
    )(src_thru, land_thru, send_sems, recv_sems, after)


def _sum_parts(own, land, name):
    r, n = own.shape
    tr = _pick(r, (264, 320, 336, 128, 64, 32, 16, 8))

    def body(own_ref, x_ref, o_ref):
        acc = own_ref[...]
        for k in range(N_DEV - 1):
            acc = acc + x_ref[k]
        o_ref[...] = acc

    return _pcall(body, name=name, out_shape=SDS((r, n), F32), grid=(r // tr,),
                  in_specs=[pl.BlockSpec((tr, n), lambda i: (i, 0)), pl.BlockSpec((N_DEV - 1, tr, n), lambda i: (0, i, 0))],
                  out_specs=pl.BlockSpec((tr, n), lambda i: (i, 0)), dims=("parallel",))(own, land)


def _pad_rows(a, rows):
    return jnp.pad(a, ((0, rows - a.shape[0]), (0, 0)))


def kernel(x, mem, positions, ln_in_g, ln_in_b, w_in, attn_sink, g_win, g_dil, w_mix_out, ln1_g, ln1_b, mem_ln_g, mem_ln_b, w_xq, w_xk, w_xv, w_xo, ln2_g, ln2_b, w_gate, w_up, conv_w, conv_b, w_down, ln3_g, ln3_b, loss_target, m_ln_in_g, m_ln_in_b, m_w_in, m_attn_sink, m_g_win, m_g_dil, m_w_mix_out, m_ln1_g, m_ln1_b, m_mem_ln_g, m_mem_ln_b, m_w_xq, m_w_xk, m_w_xv, m_w_xo, m_ln2_g, m_ln2_b, m_w_gate, m_w_up, m_conv_w, m_conv_b, m_w_down, m_ln3_g, m_ln3_b, v_ln_in_g, v_ln_in_b, v_w_in, v_attn_sink, v_g_win, v_g_dil, v_w_mix_out, v_ln1_g, v_ln1_b, v_mem_ln_g, v_mem_ln_b, v_w_xq, v_w_xk, v_w_xv, v_w_xo, v_ln2_g, v_ln2_b, v_w_gate, v_w_up, v_conv_w, v_conv_b, v_w_down, v_ln3_g, v_ln3_b):
    weights = dict(ln_in_g=ln_in_g, ln_in_b=ln_in_b, w_in=w_in, attn_sink=attn_sink, g_win=g_win, g_dil=g_dil, w_mix_out=w_mix_out, ln1_g=ln1_g, ln1_b=ln1_b, mem_ln_g=mem_ln_g, mem_ln_b=mem_ln_b, w_xq=w_xq, w_xk=w_xk, w_xv=w_xv, w_xo=w_xo, ln2_g=ln2_g, ln2_b=ln2_b, w_gate=w_gate, w_up=w_up, conv_w=conv_w, conv_b=conv_b, w_down=w_down, ln3_g=ln3_g, ln3_b=ln3_b)
    mom_m = dict(ln_in_g=m_ln_in_g, ln_in_b=m_ln_in_b, w_in=m_w_in, attn_sink=m_attn_sink, g_win=m_g_win, g_dil=m_g_dil, w_mix_out=m_w_mix_out, ln1_g=m_ln1_g, ln1_b=m_ln1_b, mem_ln_g=m_mem_ln_g, mem_ln_b=m_mem_ln_b, w_xq=m_w_xq, w_xk=m_w_xk, w_xv=m_w_xv, w_xo=m_w_xo, ln2_g=m_ln2_g, ln2_b=m_ln2_b, w_gate=m_w_gate, w_up=m_w_up, conv_w=m_conv_w, conv_b=m_conv_b, w_down=m_w_down, ln3_g=m_ln3_g, ln3_b=m_ln3_b)
    mom_v = dict(ln_in_g=v_ln_in_g, ln_in_b=v_ln_in_b, w_in=v_w_in, attn_sink=v_attn_sink, g_win=v_g_win, g_dil=v_g_dil, w_mix_out=v_w_mix_out, ln1_g=v_ln1_g, ln1_b=v_ln1_b, mem_ln_g=v_mem_ln_g, mem_ln_b=v_mem_ln_b, w_xq=v_w_xq, w_xk=v_w_xk, w_xv=v_w_xv, w_xo=v_w_xo, ln2_g=v_ln2_g, ln2_b=v_ln2_b, w_gate=v_w_gate, w_up=v_w_up, conv_w=v_conv_w, conv_b=v_conv_b, w_down=v_w_down, ln3_g=v_ln3_g, ln3_b=v_ln3_b)
    order = list(weights)
    s = x.shape[1]
    xs = x[0]
    mems = mem[0]
    target = loss_target[0]
    row = lambda a: a.reshape(1, -1)

    shard_rows = dict(w_in=w_in[0].T, w_gate=w_gate[0].T, w_up=w_up[0].T, w_mix_out=w_mix_out[0], w_xq=w_xq[0],
                      w_xk=w_xk[0], w_xv=w_xv[0], w_xo=w_xo[0], w_down=w_down[0])
    me_lin = 4 * lax.axis_index("x") + 2 * lax.axis_index("y") + lax.axis_index("c")
    w_in_full = _all_gather_rows(shard_rows["w_in"].astype(BF16), name="w_in_all_gather", in_vmem=False)
    w_in_t = jnp.concatenate([w_in_full[768:], w_in_full[:768]], axis=0)
    late_rows = PACK_ROWS[1:]
    late_r = sum(r for _, r in late_rows)
    packed = jnp.concatenate([shard_rows[n].astype(BF16) for n, _ in late_rows], axis=0)
    w_started = _exchange_start(packed, (N_DEV * late_r, D_MODEL), gather=True, name="weight_gather_start")
    cw_pad = jnp.pad(conv_w[0], ((0, 5), (0, 32)))
    cw_all = _all_gather_rows(cw_pad, name="conv_w_all_gather", in_vmem=True).reshape(N_DEV, 8, 384)
    cw_full = jnp.transpose(cw_all[:, :3, :352], (1, 0, 2)).reshape(3, D_FF)
    cw8 = _pad_rows(cw_full, 8)

    tab = _rope_tables(positions.astype(F32).reshape(s, 1) + w_started[4][0, 0])
    h0 = _ln_fwd(xs, None, row(ln_in_g), row(ln_in_b), 1.0, "ln_in_fwd")
    zr = _proj_rope(h0, w_in_t, tab)
    oa, lse_a = _banded_fwd(zr, attn_sink, name="win_attn_fwd", **_WIN_CFG)
    ogs, lgs = [], []
    for gi in range(3):
        o_g, l_g = _banded_fwd(zr, None, name=f"dil_attn_fwd{gi}", **_dil_cfg(gi))
        ogs.append(o_g)
        lgs.append(l_g)
    mixed, ob, lse_b = _mix_norm_fwd(oa, ogs, lgs, g_win, g_dil)
    packed_thru, land = _exchange_wait(w_started, mixed, gather=True, name="weight_gather_wait")
    gathered = lax.dynamic_update_slice(land, packed_thru, (me_lin * late_r, 0)).reshape(N_DEV, late_r, D_MODEL)
    full = {}
    off = 0
    for n, r in late_rows:
        full[n] = gathered[:, off:off + r, :].reshape(N_DEV * r, D_MODEL)
        off += r
    mix = _mm(mixed, full["w_mix_out"], trans_b=False, out_dtype=F32, name="mm_mix_out")
    h1 = _ln_fwd(h0, mix, ln1_g, ln1_b, ALPHA, "ln1_fwd")
    mem_n = _ln_fwd(mems, None, mem_ln_g, mem_ln_b, 1.0, "mem_ln_fwd")
    kx = _mm(mem_n, full["w_xk"], trans_b=False, out_dtype=BF16, name="mm_xk")
    vx = _mm(mem_n, full["w_xv"], trans_b=False, out_dtype=BF16, name="mm_xv")
    qx = _mm(h1, full["w_xq"], trans_b=False, out_dtype=BF16, name="mm_xq")
    ox = _xattn_fwd(qx, kx, vx)
    xa = _mm(ox, full["w_xo"], trans_b=False, out_dtype=F32, name="mm_xo")
    h2 = _ln_fwd(h1, xa, ln2_g, ln2_b, ALPHA, "ln2_fwd")
    gate = _mm(h2, full["w_gate"], trans_b=True, out_dtype=F32, name="mm_gate")
    up = _mm(h2, full["w_up"], trans_b=True, out_dtype=F32, name="mm_up")
    act = _glu_fwd(gate, up, cw8, conv_b)
    ff = _mm(act, full["w_down"], trans_b=False, out_dtype=F32, name="mm_down")

    du3, d_ln3_g, d_ln3_b, loss_local = _ln_bwd(h2, ff, target, ln3_g, ln3_b, ALPHA, "ln3_bwd_loss", loss_mode=True)
    dact = _mm(du3, full["w_down"], trans_b=True, out_dtype=F32, name="mm_d_act")
    dw_down = _mm_tn(act, du3, name="mm_dw_down")
    dgc, dup, dcw8, d_conv_b = _glu_bwd(gate, up, dact, cw8, conv_b)
    dgate = _conv_bwd_input(dgc, cw8)
    dh2 = _mm(dgate, full["w_gate"], trans_b=False, out_dtype=F32, name="mm_dh2_gate", addends=(du3,), coefs=(ALPHA,))
    dh2 = _mm(dup, full["w_up"], trans_b=False, out_dtype=F32, name="mm_dh2_up", addends=(dh2,), coefs=(1.0,))
    dw_gate_t = _mm_tn(dgate, h2, name="mm_dw_gate")
    dw_up_t = _mm_tn(dup, h2, name="mm_dw_up")
    rows_of = dict(PACK_ROWS)

    def start_grad_exchange(parts, name):
        gp = jnp.concatenate([g.reshape(N_DEV, rows_of[n], D_MODEL) for n, g in parts], axis=1)
        return _exchange_start(gp, (N_DEV - 1,) + gp.shape[1:], gather=False, name=name)

    ffn_parts = (("w_gate", dw_gate_t), ("w_up", dw_up_t), ("w_down", dw_down))
    ffn_started = start_grad_exchange(ffn_parts, "grad_start_ffn")
    du2, d_ln2_g, d_ln2_b = _ln_bwd(h1, xa, dh2, ln2_g + ffn_started[4][0, 0], ln2_b, ALPHA, "ln2_bwd")
    dox = _mm(du2, full["w_xo"], trans_b=True, out_dtype=F32, name="mm_d_ox")
    dw_xo = _mm_tn(ox, du2, name="mm_dw_xo")
    dqx, dkx, dvx = _xattn_bwd(qx, kx, vx, ox, dox)
    dh1 = _mm(dqx, full["w_xq"], trans_b=True, out_dtype=F32, name="mm_dh1", addends=(du2,), coefs=(ALPHA,))
    dw_xq = _mm_tn(h1, dqx, name="mm_dw_xq")
    dw_xk = _mm_tn(mem_n, dkx, name="mm_dw_xk")
    dw_xv = _mm_tn(mem_n, dvx, name="mm_dw_xv")
    dmem_n = _mm(dkx, full["w_xk"], trans_b=True, out_dtype=F32, name="mm_dmem_k")
    dmem_n = _mm(dvx, full["w_xv"], trans_b=True, out_dtype=F32, name="mm_dmem_v", addends=(dmem_n,), coefs=(1.0,))
    _, d_mem_ln_g, d_mem_ln_b = _ln_bwd(mems, None, dmem_n, mem_ln_g, mem_ln_b, 1.0, "mem_ln_bwd")
    du1, d_ln1_g, d_ln1_b = _ln_bwd(h0, mix, dh1, ln1_g, ln1_b, ALPHA, "ln1_bwd")
    dmixed = _mm(du1, full["w_mix_out"], trans_b=True, out_dtype=F32, name="mm_d_mixed")
    dw_mix_out = _mm_tn(mixed, du1, name="mm_dw_mix_out")
    attn_parts = (("w_mix_out", dw_mix_out), ("w_xq", dw_xq), ("w_xk", dw_xk), ("w_xv", dw_xv), ("w_xo", dw_xo))
    attn_started = start_grad_exchange(attn_parts, "grad_start_attn")
    doa, dob, d_g_win, d_g_dil = _mix_norm_bwd(oa, ob, dmixed, g_win + attn_started[4][0, 0], g_dil)
    dqa, dkva, dsink8 = _banded_bwd(zr, oa, lse_a, doa, tab, attn_sink, kv_heads=_WIN_KV, name="win_attn_bwd", **_WIN_CFG)
    dqs, dks, dvs = [], [], []
    for gi in range(3):
        dq_g, dkv_g = _banded_bwd(zr, ob, lse_b, dob, tab, None, kv_heads=_DIL_KV, name=f"dil_attn_bwd{gi}", **_dil_cfg(gi))
        dqs.append(dq_g)
        dks.append(dkv_g[:, :512])
        dvs.append(dkv_g[:, 512:])
    dz = jnp.concatenate(dqs + dks + dvs + [dqa, dkva], axis=1)
    dw_in_tz = _mm_tn(dz, h0, name="mm_dw_in")
    dw_in_t = jnp.concatenate([dw_in_tz[4608:], dw_in_tz[:4608]], axis=0)
    in_parts = (("w_in", dw_in_t),)
    in_started = start_grad_exchange(in_parts, "grad_start_in")
    dh0 = _mm(dz, w_in_t, trans_b=False, out_dtype=F32, name="mm_dh0", addends=(du1,), coefs=(ALPHA,),
              after=in_started[4])
    dx, d_ln_in_g, d_ln_in_b = _ln_bwd(xs, None, dh0, row(ln_in_g), row(ln_in_b), 1.0, "ln_in_bwd")

    grads = {}
    small = jnp.concatenate([
        d_ln_in_g, d_ln_in_b, d_ln1_g, d_ln1_b, d_mem_ln_g, d_mem_ln_b, d_ln2_g, d_ln2_b, d_ln3_g, d_ln3_b,
        jnp.concatenate([d_g_win, d_g_dil], axis=1),
        jnp.pad(d_conv_b, ((0, 0), (0, 3072 - D_FF))).reshape(3, 1024),
        jnp.pad(dsink8[0:1, :], ((0, 0), (0, 1024 - 128))),
        jnp.pad(dcw8[0:3], ((0, 0), (0, 3072 - D_FF))).reshape(9, 1024),
    ], axis=0)
    _, ssum = _all_gather_rows(small, name="small_grad_all_reduce", in_vmem=True, sum_rows=True)
    names10 = ["ln_in_g", "ln_in_b", "ln1_g", "ln1_b", "mem_ln_g", "mem_ln_b", "ln2_g", "ln2_b", "ln3_g", "ln3_b"]
    for i, n in enumerate(names10):
        grads[n] = ssum[i].reshape(weights[n].shape)
    grads["g_win"] = ssum[10:11, :512]
    grads["g_dil"] = ssum[10:11, 512:]
    grads["conv_b"] = ssum[11:14].reshape(1, 3072)[:, :D_FF]
    grads["attn_sink"] = ssum[14:15, :8]
    dcw_full = ssum[15:24].reshape(3, 3072)[:, :D_FF]
    grads["conv_w"] = lax.dynamic_slice_in_dim(dcw_full, me_lin * 352, 352, axis=1)[None]

    delta, new_m, new_v = {}, {}, {}
    big = [n for n, _ in PACK_ROWS]
    small_names = [n for n in order if n not in big]

    def pack_small(src):
        rows_ = []
        for n in small_names:
            flat = src[n].reshape(1, -1)
            width = -(-flat.shape[1] // 1024) * 1024
            rows_.append(jnp.pad(flat, ((0, 0), (0, width - flat.shape[1]))).reshape(-1, 1024))
        packed_ = jnp.concatenate(rows_, axis=0)
        return _pad_rows(packed_, -(-packed_.shape[0] // 8) * 8)

    d_s, m_s, v_s = _adamw(pack_small(weights), pack_small(grads), pack_small(mom_m), pack_small(mom_v), "adamw_small")
    r0 = 0
    for n in small_names:
        size = weights[n].size
        nrow = -(-size // 1024)
        for dst, src in ((delta, d_s), (new_m, m_s), (new_v, v_s)):
            dst[n] = src[r0:r0 + nrow].reshape(-1)[:size].reshape(weights[n].shape)
        r0 += nrow

    after = d_s
    for parts, started, tag in ((ffn_parts, ffn_started, "ffn"), (attn_parts, attn_started, "attn"),
                                (in_parts, in_started, "in")):
        gp_thru, land = _exchange_wait(started, after, gather=False, name=f"grad_wait_{tag}")
        own = lax.dynamic_index_in_dim(gp_thru, me_lin, axis=0, keepdims=False)
        gsum = _sum_parts(own, land, f"grad_sum_{tag}")
        off = 0
        for n, _ in parts:
            blk = gsum[off:off + rows_of[n]]
            off += rows_of[n]
            grads[n] = (blk.T if n in ("w_in", "w_gate", "w_up") else blk)[None]
            shp = weights[n].shape
            d_, m_, v_ = _adamw(weights[n].reshape(shp[1:]), grads[n].reshape(shp[1:]), mom_m[n].reshape(shp[1:]),
                                mom_v[n].reshape(shp[1:]), f"adamw_{n}")
            delta[n], new_m[n], new_v[n] = d_.reshape(shp), m_.reshape(shp), v_.reshape(shp)
            after = d_

    loss = lax.psum(loss_local[0, 0], MESH_AXES)
    return (loss, dx[None], *[grads[n] for n in order], *[delta[n] for n in order], *[new_m[n] for n in order],
            *[new_v[n] for n in order])
```

```python
import functools
import math

import jax
import jax.numpy as jnp
from jax import lax
from jax.experimental import pallas as pl
from jax.experimental.pallas import tpu as pltpu

F32 = jnp.float32
BF16 = jnp.bfloat16
SDS = jax.ShapeDtypeStruct
_PALLAS_CALL = pl.pallas_call

D_MODEL = 1024
HEAD_DIM = 64
WIN_HALF = 128
DIL_PAIRS = ((128, 1), (512, 4), (2048, 16))
DIL_SIDE = 64
ROT_DIM = 16
ROPE_THETA = 500000.0
MEM_LEN = 256
X_HEADS = 4
X_HEAD_DIM = 256
D_FF = 2816
IN_WIDTH = 5376
ZW = 5632
Z_QB, Z_KB, Z_VB, Z_QA, Z_KA, Z_VA = 0, 1536, 3072, 4608, 5120, 5248
ALPHA = (2.0) ** 0.25
LN_EPS = 1e-5
NEG_INF = -1e30
ADAM_LR, ADAM_B1, ADAM_B2, ADAM_EPS, ADAM_WD, ADAM_STEP = 0.001, 0.9, 0.999, 1e-08, 0.01, 10
N_DEV = 8
MESH_AXES = ("x", "y", "c")
VMEM_LIMIT_BYTES = 52 * 1024 * 1024
ATTN_TQ = 256
TABW = 384

PACK_ROWS = (("w_in", 672), ("w_gate", 352), ("w_up", 352), ("w_mix_out", 128), ("w_xq", 128), ("w_xk", 128),
             ("w_xv", 128), ("w_xo", 128), ("w_down", 352))
SMALL_ROWS = 24


def _pick(n, cands):
    for c in cands:
        if n % c == 0:
            return c
    return n


def _pcall(body, *, name, out_shape, grid=None, in_specs=None, out_specs=None, scratch_shapes=(), dims=None,
           aliases=None):
    kw = {}
    if grid is not None:
        kw["grid"] = grid
    if in_specs is not None:
        kw["in_specs"] = in_specs
    if out_specs is not None:
        kw["out_specs"] = out_specs
    if aliases:
        kw["input_output_aliases"] = aliases
    return _PALLAS_CALL(
        body, name=name, out_shape=out_shape, scratch_shapes=list(scratch_shapes),
        compiler_params=pltpu.CompilerParams(dimension_semantics=dims, vmem_limit_bytes=VMEM_LIMIT_BYTES), **kw)


MM_VMEM_BUDGET = 40 * 1024 * 1024


def _mm(a, b, *, trans_b, out_dtype, name, addends=(), coefs=(), after=None, more=()):
    pairs = ((a, b, trans_b),) + tuple(more)
    m = a.shape[0]
    n = b.shape[0] if trans_b else b.shape[1]
    n_add = len(addends)
    extra = [] if after is None else [after]
    out_bytes = jnp.dtype(out_dtype).itemsize

    def vmem(tm, tn):
        tot = tm * tn * (out_bytes + 4 * n_add)
        for pa, pb, _ in pairs:
            tot += tm * pa.shape[1] * pa.dtype.itemsize + pa.shape[1] * tn * pb.dtype.itemsize
        return 2 * tot

    tm, tn = next(((cm, cn) for cn in (n, 1408, 1024, 512, 256, 128) if n % cn == 0
                   for cm in (1024, 512, 256, 128) if m % cm == 0 and vmem(cm, cn) <= MM_VMEM_BUDGET))
    n_pairs = len(pairs)

    def body(*refs):
        o_ref = refs[2 * n_pairs + n_add + len(extra)]
        acc = None
        for p, (_, _, tb) in enumerate(pairs):
            dn = _NT if tb else _NN
            part = lax.dot_general(refs[2 * p][...].astype(BF16), refs[2 * p + 1][...].astype(BF16), dn,
                                   preferred_element_type=F32)
            acc = part if acc is None else acc + part
        for r_ref, c in zip(refs[2 * n_pairs:2 * n_pairs + n_add], coefs):
            acc = acc + (r_ref[...] if c == 1.0 else c * r_ref[...])
        o_ref[...] = acc.astype(out_dtype)

    in_specs, args = [], []
    for pa, pb, tb in pairs:
        k = pa.shape[1]
        in_specs.append(pl.BlockSpec((tm, k), lambda j, i: (i, 0)))
        in_specs.append(pl.BlockSpec((tn, k), lambda j, i: (j, 0)) if tb else pl.BlockSpec((k, tn), lambda j, i: (0, j)))
        args += [pa, pb]
    in_specs += [pl.BlockSpec((tm, tn), lambda j, i: (i, j)) for _ in addends]
    in_specs += [pl.BlockSpec((8, 128), lambda j, i: (0, 0)) for _ in extra]
    return _pcall(body, name=name, out_shape=SDS((m, n), out_dtype), grid=(n // tn, m // tm), in_specs=in_specs,
                  out_specs=pl.BlockSpec((tm, tn), lambda j, i: (i, j)),
                  dims=("parallel", "parallel"))(*args, *addends, *extra)


def _mm_tn(a, b, *, name):
    s, m = a.shape
    n = b.shape[1]
    tm = _pick(m, (768, 1408, 1024, 512, 256, 128))
    tk = _pick(s, (1024, 512, 256))
    nk = s // tk

    def body(a_ref, b_ref, o_ref, acc_ref):
        kk = pl.program_id(1)

        @pl.when(kk == 0)
        def _():
            acc_ref[...] = jnp.zeros_like(acc_ref)

        acc_ref[...] += lax.dot_general(a_ref[...].astype(BF16), b_ref[...].astype(BF16), (((0,), (0,)), ((), ())),
                                        preferred_element_type=F32)

        @pl.when(kk == nk - 1)
        def _():
            o_ref[...] = acc_ref[...]

    return _pcall(body, name=name, out_shape=SDS((m, n), F32), grid=(m // tm, nk),
                  in_specs=[pl.BlockSpec((tk, tm), lambda i, kk: (kk, i)), pl.BlockSpec((tk, n), lambda i, kk: (kk, 0))],
                  out_specs=pl.BlockSpec((tm, n), lambda i, kk: (i, 0)), scratch_shapes=[pltpu.VMEM((tm, n), F32)],
                  dims=("parallel", "arbitrary"))(a, b)


def _rope_lane_consts():
    lane = jnp.arange(128)
    j = lane % HEAD_DIM
    inv_freq = ROPE_THETA ** (-jnp.arange(0, ROT_DIM, 2, dtype=F32) / ROT_DIM)
    freq = jnp.where(j < ROT_DIM, inv_freq[j % (ROT_DIM // 2)], 0.0).astype(F32)
    lo = (j < ROT_DIM // 2).astype(F32)
    hi = ((j >= ROT_DIM // 2) & (j < ROT_DIM)).astype(F32)
    return jnp.stack([freq, lo, hi] + [jnp.zeros((128,), F32)] * 5)


def _rope_tables(posf):
    s = posf.shape[0]
    tm = _pick(s, (1024, 512))

    def body(p_ref, c_ref, o_ref):
        ang = p_ref[...] * c_ref[0:1, :]
        lo = c_ref[1:2, :]
        hi = c_ref[2:3, :]
        cs = jnp.cos(ang)
        sn = jnp.sin(ang)
        o_ref[...] = jnp.concatenate([jnp.where(lo + hi > 0.0, cs, 1.0), -sn * lo, sn * hi], axis=1)

    return _pcall(body, name="rope_tables", out_shape=SDS((s, TABW), F32), grid=(s // tm,),
                  in_specs=[pl.BlockSpec((tm, 1), lambda i: (i, 0)), pl.BlockSpec((8, 128), lambda i: (0, 0))],
                  out_specs=pl.BlockSpec((tm, TABW), lambda i: (i, 0)), dims=("parallel",))(posf, _rope_lane_consts())


def _rope_apply(x, tab, sign):
    w = x.shape[1]
    rep = w // 128
    c = jnp.tile(tab[:, 0:128], (1, rep)) if rep > 1 else tab[:, 0:128]
    a = jnp.tile(tab[:, 128:256], (1, rep)) if rep > 1 else tab[:, 128:256]
    b = jnp.tile(tab[:, 256:384], (1, rep)) if rep > 1 else tab[:, 256:384]
    up = pltpu.roll(x, w - 8, 1)
    dn = pltpu.roll(x, 8, 1)
    if sign > 0:
        return x * c + up * a + dn * b
    return x * c - up * a - dn * b


def _proj_rope(h0b, w_t, tab):
    s = h0b.shape[0]
    tm = _pick(s, (512,))
    tn = 256

    def body(a_ref, w_ref, t_ref, o_ref):
        a = a_ref[...]
        tabv = t_ref[...]
        for c0 in range(0, IN_WIDTH, tn):
            z = lax.dot_general(a, w_ref[c0:c0 + tn, :], _NT, preferred_element_type=F32)
            for g0 in range(c0, c0 + tn, 128):
                zg = z[:, g0 - c0:g0 - c0 + 128]
                if g0 < Z_VB or Z_QA <= g0 < Z_VA:
                    zg = _rope_apply(zg, tabv, 1)
                o_ref[:, g0:g0 + 128] = zg.astype(BF16)
        o_ref[:, IN_WIDTH:] = jnp.zeros((tm, ZW - IN_WIDTH), BF16)

    return _pcall(body, name="proj_rope", out_shape=SDS((s, ZW), BF16), grid=(s // tm,),
                  in_specs=[pl.BlockSpec((tm, D_MODEL), lambda i: (i, 0)), pl.BlockSpec((IN_WIDTH, D_MODEL), lambda i: (0, 0)),
                            pl.BlockSpec((tm, TABW), lambda i: (i, 0))],
                  out_specs=pl.BlockSpec((tm, ZW), lambda i: (i, 0)), dims=("parallel",))(h0b, w_t, tab)


def _band_specs(sd, blk, tq, width, per_tok, cb):
    r = tq // blk
    nbk = sd // blk
    prev = pl.BlockSpec((blk, width), lambda c, j: (jnp.maximum(j * r - 1, 0), c * per_tok + cb))
    cur = pl.BlockSpec((tq, width), lambda c, j: (j, c * per_tok + cb))
    nxt = pl.BlockSpec((blk, width), lambda c, j: (jnp.minimum((j + 1) * r, nbk - 1), c * per_tok + cb))
    return [prev, cur, nxt]


def _band_bias(j, blk, tq, sd, rows_are_tile):
    w = tq + 2 * blk
    shape = (tq, w) if rows_are_tile else (w, tq)
    tile_pos = j * tq + lax.broadcasted_iota(jnp.int32, shape, 0 if rows_are_tile else 1)
    wide_pos = j * tq - blk + lax.broadcasted_iota(jnp.int32, shape, 1 if rows_are_tile else 0)
    ok = (jnp.abs(tile_pos - wide_pos) <= blk) & (wide_pos >= 0) & (wide_pos < sd)
    return jnp.where(ok, 0.0, NEG_INF)


_NT = (((1,), (1,)), ((), ()))
_NN = (((1,), (0,)), ((), ()))
_TN = (((0,), (0,)), ((), ()))


def _banded_fwd(zr, sink, *, d, blk, tq, qw, kw, qcb, kcb, vcb, heads, name):
    s = zr.shape[0]
    sd = s // d
    tq = min(tq, sd)
    zv = zr.reshape(sd, d * ZW)
    has_sink = sink is not None
    scale = HEAD_DIM ** -0.5

    def body(q_ref, kp, kc, kn, vp, vc, vn, *rest):
        if has_sink:
            sink_ref, o_ref, lse_ref = rest
        else:
            o_ref, lse_ref = rest
        j = pl.program_id(1)
        q = q_ref[...] * scale
        k = jnp.concatenate([kp[...], kc[...], kn[...]], axis=0)
        v = jnp.concatenate([vp[...], vc[...], vn[...]], axis=0)
        bias = _band_bias(j, blk, tq, sd, True)
        outs, lses = [], []
        for ql, kl, vl, si in heads:
            sc = lax.dot_general(q[:, ql:ql + HEAD_DIM], k[:, kl:kl + HEAD_DIM], _NT, preferred_element_type=F32) + bias
            m = jnp.max(sc, axis=-1, keepdims=True)
            if has_sink:
                m = jnp.maximum(m, sink_ref[0, si])
            p = jnp.exp(sc - m)
            den = jnp.sum(p, axis=-1, keepdims=True)
            if has_sink:
                den = den + jnp.exp(sink_ref[0, si] - m)
            o = lax.dot_general(p.astype(BF16), v[:, vl:vl + HEAD_DIM], _NN, preferred_element_type=F32) / den
            outs.append(o)
            lses.append(jnp.broadcast_to(m + jnp.log(den), (tq, HEAD_DIM)))
        o_ref[...] = jnp.concatenate(outs, axis=1)
        lse_ref[...] = jnp.concatenate(lses, axis=1)

    in_specs = ([pl.BlockSpec((tq, qw), lambda c, j: (j, c * (ZW // qw) + qcb))]
                + _band_specs(sd, blk, tq, kw, ZW // kw, kcb) + _band_specs(sd, blk, tq, kw, ZW // kw, vcb))
    args = [zv] * 7
    if has_sink:
        in_specs.append(pl.BlockSpec(memory_space=pltpu.SMEM))
        args.append(sink)
    o_spec = pl.BlockSpec((tq, qw), lambda c, j: (j, c))
    o, lse = _pcall(body, name=name, out_shape=(SDS((sd, d * qw), F32), SDS((sd, d * qw), F32)), grid=(d, sd // tq),
                    in_specs=in_specs, out_specs=(o_spec, o_spec), dims=("parallel", "parallel"))(*args)
    return o.reshape(s, qw), lse.reshape(s, qw)


def _banded_bwd(zr, o, lse, do, tab, sink, *, d, blk, tq, qw, kw, qcb, kcb, vcb, heads, kv_heads, name):
    s = zr.shape[0]
    sd = s // d
    tq = min(tq, sd)
    zv = zr.reshape(sd, d * ZW)
    ov, lv, dov = (t.reshape(sd, d * qw) for t in (o, lse, do))
    tv = tab.reshape(sd, d * TABW)
    has_sink = sink is not None
    scale = HEAD_DIM ** -0.5
    kvw = HEAD_DIM * len(kv_heads)
    b0, b1 = blk, blk + tq

    def body(*refs):
        q3 = jnp.concatenate([r[...] for r in refs[0:3]], axis=0) * scale
        k3 = jnp.concatenate([r[...] for r in refs[3:6]], axis=0)
        v3 = jnp.concatenate([r[...] for r in refs[6:9]], axis=0)
        o3 = jnp.concatenate([r[...] for r in refs[9:12]], axis=0)
        l3 = jnp.concatenate([r[...] for r in refs[12:15]], axis=0)
        do3 = jnp.concatenate([r[...] for r in refs[15:18]], axis=0)
        t_ref = refs[18]
        if has_sink:
            sink_ref, dq_ref, dkv_ref, dsink_ref = refs[19:]
        else:
            dq_ref, dkv_ref = refs[19:]
        j = pl.program_id(1)
        bias1 = _band_bias(j, blk, tq, sd, True)
        bias2 = _band_bias(j, blk, tq, sd, False)
        dqs = []
        dks = [None] * len(kv_heads)
        dvs = [None] * len(kv_heads)
        dsink_row = jnp.zeros((1, 128), F32)
        lane = lax.broadcasted_iota(jnp.int32, (1, 128), 1)
        for ql, kl, vl, si in heads:
            kvi = kv_heads.index((kl, vl))
            qh3 = q3[:, ql:ql + HEAD_DIM]
            kh3 = k3[:, kl:kl + HEAD_DIM]
            vh3 = v3[:, vl:vl + HEAD_DIM]
            doh3 = do3[:, ql:ql + HEAD_DIM]
            delta3 = jnp.sum(doh3 * o3[:, ql:ql + HEAD_DIM], axis=-1, keepdims=True)
            lse3 = l3[:, ql:ql + 1]
            dob3 = doh3.astype(BF16)
            sc = lax.dot_general(qh3[b0:b1], kh3, _NT, preferred_element_type=F32) + bias1
            p = jnp.exp(sc - lse3[b0:b1])
            dp = lax.dot_general(dob3[b0:b1], vh3, _NT, preferred_element_type=F32)
            ds = p * (dp - delta3[b0:b1])
            dqs.append(lax.dot_general(ds.astype(BF16), kh3, _NN, preferred_element_type=F32) * scale)
            if has_sink:
                psink = jnp.exp(sink_ref[0, si] - lse3[b0:b1])
                dsink_row = dsink_row + jnp.where(lane == si, -jnp.sum(psink * delta3[b0:b1]), 0.0)
            sc2 = lax.dot_general(qh3, kh3[b0:b1], _NT, preferred_element_type=F32) + bias2
            p2 = jnp.exp(sc2 - lse3)
            dv = lax.dot_general(p2.astype(BF16), dob3, _TN, preferred_element_type=F32)
            dp2 = lax.dot_general(dob3, vh3[b0:b1], _NT, preferred_element_type=F32)
            ds2 = p2 * (dp2 - delta3)
            dk = lax.dot_general(ds2.astype(BF16), qh3, _TN, preferred_element_type=F32)
            dks[kvi] = dk if dks[kvi] is None else dks[kvi] + dk
            dvs[kvi] = dv if dvs[kvi] is None else dvs[kvi] + dv
        tabv = t_ref[...]
        dq_ref[...] = _rope_apply(jnp.concatenate(dqs, axis=1), tabv, -1).astype(BF16)
        dk_all = jnp.concatenate(dks, axis=1) if len(dks) > 1 else dks[0]
        dv_all = jnp.concatenate(dvs, axis=1) if len(dvs) > 1 else dvs[0]
        dkv_ref[...] = jnp.concatenate([_rope_apply(dk_all, tabv, -1), dv_all], axis=1).astype(BF16)
        if has_sink:
            first = (pl.program_id(0) == 0) & (j == 0)

            @pl.when(first)
            def _():
                dsink_ref[...] = jnp.zeros_like(dsink_ref)

            dsink_ref[0:1, :] += dsink_row

    in_specs = (_band_specs(sd, blk, tq, qw, ZW // qw, qcb) + _band_specs(sd, blk, tq, kw, ZW // kw, kcb)
                + _band_specs(sd, blk, tq, kw, ZW // kw, vcb) + _band_specs(sd, blk, tq, qw, 1, 0) * 3
                + [pl.BlockSpec((tq, TABW), lambda c, j: (j, c))])
    args = [zv] * 9 + [ov] * 3 + [lv] * 3 + [dov] * 3 + [tv]
    out_shape = [SDS((sd, d * qw), BF16), SDS((sd, d * 2 * kvw), BF16)]
    out_specs = [pl.BlockSpec((tq, qw), lambda c, j: (j, c)), pl.BlockSpec((tq, 2 * kvw), lambda c, j: (j, c))]
    dims = ("parallel", "parallel")
    if has_sink:
        in_specs.append(pl.BlockSpec(memory_space=pltpu.SMEM))
        args.append(sink)
        out_shape.append(SDS((8, 128), F32))
        out_specs.append(pl.BlockSpec((8, 128), lambda c, j: (0, 0)))
        dims = ("arbitrary", "arbitrary")
    res = _pcall(body, name=name, out_shape=tuple(out_shape), grid=(d, sd // tq), in_specs=in_specs,
                 out_specs=tuple(out_specs), dims=dims)(*args)
    dq = res[0].reshape(s, qw)
    dkv = res[1].reshape(s, 2 * kvw)
    return (dq, dkv, res[2]) if has_sink else (dq, dkv)


_WIN_HEADS = tuple((h * HEAD_DIM, (h // 4) * HEAD_DIM, 128 + (h // 4) * HEAD_DIM, h) for h in range(8))
_WIN_KV = ((0, 128), (64, 192))
_WIN_CFG = dict(d=1, blk=WIN_HALF, tq=ATTN_TQ, qw=512, kw=256, qcb=Z_QA // 512, kcb=Z_KA // 256, vcb=Z_KA // 256,
                heads=_WIN_HEADS)
_DIL_HEADS = tuple((h * HEAD_DIM, h * HEAD_DIM, h * HEAD_DIM, h) for h in range(8))
_DIL_KV = tuple((h * HEAD_DIM, h * HEAD_DIM) for h in range(8))


def _dil_cfg(gi):
    return dict(d=DIL_PAIRS[gi][1], blk=DIL_SIDE, tq=ATTN_TQ, qw=512, kw=512, qcb=Z_QB // 512 + gi,
                kcb=Z_KB // 512 + gi, vcb=Z_VB // 512 + gi, heads=_DIL_HEADS)


def _mix_norm_fwd(oa, ogs, lgs, g_win, g_dil):
    s = oa.shape[0]
    tm = _pick(s, (512,))

    def body(oa_ref, o0, o1, o2, l0, l1, l2, gw_ref, gd_ref, mixed_ref, ob_ref, lb_ref):
        la, lb, lc = l0[...], l1[...], l2[...]
        mx = jnp.maximum(jnp.maximum(la, lb), lc)
        ea, eb, ec = jnp.exp(la - mx), jnp.exp(lb - mx), jnp.exp(lc - mx)
        den = ea + eb + ec
        ob = (ea / den) * o0[...] + (eb / den) * o1[...] + (ec / den) * o2[...]
        ob_ref[...] = ob
        lb_ref[...] = mx + jnp.log(den)
        a = oa_ref[...]
        ra = lax.rsqrt(jnp.mean(a * a, axis=-1, keepdims=True) + LN_EPS)
        rb = lax.rsqrt(jnp.mean(ob * ob, axis=-1, keepdims=True) + LN_EPS)
        mixed_ref[...] = jnp.concatenate([a * ra * gw_ref[...], ob * rb * gd_ref[...]], axis=1).astype(BF16)

    row = pl.BlockSpec((tm, 512), lambda i: (i, 0))
    vec = pl.BlockSpec((1, 512), lambda i: (0, 0))
    return _pcall(body, name="mix_norm_fwd", out_shape=(SDS((s, 1024), BF16), SDS((s, 512), F32), SDS((s, 512), F32)),
                  grid=(s // tm,), in_specs=[row] * 7 + [vec, vec],
                  out_specs=(pl.BlockSpec((tm, 1024), lambda i: (i, 0)), row, row),
                  dims=("parallel",))(oa, *ogs, *lgs, g_win, g_dil)


def _mix_norm_bwd(oa, ob, dmixed, g_win, g_dil):
    s = oa.shape[0]
    tm = _pick(s, (512,))
    nt = s // tm

    def body(oa_ref, ob_ref, dm_ref, gw_ref, gd_ref, doa_ref, dob_ref, dgw_ref, dgd_ref, acc_w, acc_d):
        i = pl.program_id(0)

        @pl.when(i == 0)
        def _():
            acc_w[...] = jnp.zeros_like(acc_w)
            acc_d[...] = jnp.zeros_like(acc_d)

        dm = dm_ref[...]
        for x_ref, g_ref, dy, dx_ref, acc in ((oa_ref, gw_ref, dm[:, :512], doa_ref, acc_w),
                                              (ob_ref, gd_ref, dm[:, 512:], dob_ref, acc_d)):
            x = x_ref[...]
            r = lax.rsqrt(jnp.mean(x * x, axis=-1, keepdims=True) + LN_EPS)
            dyg = dy * g_ref[...]
            dx_ref[...] = r * dyg - x * (r * r * r) * jnp.mean(dyg * x, axis=-1, keepdims=True)
            acc[...] += jnp.sum((dy * x * r).reshape(tm // 8, 8, 512), axis=0)

        @pl.when(i == nt - 1)
        def _():
            dgw_ref[...] = jnp.sum(acc_w[...], axis=0, keepdims=True)
            dgd_ref[...] = jnp.sum(acc_d[...], axis=0, keepdims=True)

    row = pl.BlockSpec((tm, 512), lambda i: (i, 0))
    vec = pl.BlockSpec((1, 512), lambda i: (0, 0))
    return _pcall(body, name="mix_norm_bwd",
                  out_shape=(SDS((s, 512), F32), SDS((s, 512), F32), SDS((1, 512), F32), SDS((1, 512), F32)),
                  grid=(nt,), in_specs=[row, row, pl.BlockSpec((tm, 1024), lambda i: (i, 0)), vec, vec],
                  out_specs=(row, row, vec, vec), scratch_shapes=[pltpu.VMEM((8, 512), F32), pltpu.VMEM((8, 512), F32)],
                  dims=("arbitrary",))(oa, ob, dmixed, g_win, g_dil)


def _ln_fwd(a, r, g, b, ca, name):
    s = a.shape[0]
    tm = _pick(s, (512, 256))
    has_r = r is not None

    def body(*refs):
        a_ref = refs[0]
        r_ref = refs[1] if has_r else None
        g_ref, b_ref, o_ref, ob_ref = refs[1 + has_r:]
        u = a_ref[...] if ca == 1.0 else ca * a_ref[...]
        if has_r:
            u = u + r_ref[...]
        mu = jnp.mean(u, axis=-1, keepdims=True)
        xc = u - mu
        var = jnp.mean(xc * xc, axis=-1, keepdims=True)
        y = xc * lax.rsqrt(var + LN_EPS) * g_ref[...] + b_ref[...]
        o_ref[...] = y
        ob_ref[...] = y.astype(BF16)

    row = pl.BlockSpec((tm, D_MODEL), lambda i: (i, 0))
    vec = pl.BlockSpec((1, D_MODEL), lambda i: (0, 0))
    args = [a] + ([r] if has_r else []) + [g, b]
    return _pcall(body, name=name, out_shape=(SDS((s, D_MODEL), F32), SDS((s, D_MODEL), BF16)), grid=(s // tm,),
                  in_specs=[row] * (1 + has_r) + [vec, vec], out_specs=(row, row), dims=("parallel",))(*args)


def _ln_bwd(a, r, dy, g, b, ca, name, loss_mode=False):
    s = a.shape[0]
    tm = _pick(s, (512, 256))
    nt = s // tm
    has_r = r is not None

    def body(*refs):
        a_ref = refs[0]
        r_ref = refs[1] if has_r else None
        dy_ref, g_ref, b_ref = refs[1 + has_r:4 + has_r]
        outs = refs[4 + has_r:]
        if loss_mode:
            du_ref, dub_ref, dg_ref, db_ref, loss_ref, acc_g, acc_b, acc_l = outs
        else:
            du_ref, dub_ref, dg_ref, db_ref, acc_g, acc_b = outs
        i = pl.program_id(0)

        @pl.when(i == 0)
        def _():
            acc_g[...] = jnp.zeros_like(acc_g)
            acc_b[...] = jnp.zeros_like(acc_b)
            if loss_mode:
                acc_l[...] = jnp.zeros_like(acc_l)

        u = a_ref[...] if ca == 1.0 else ca * a_ref[...]
        if has_r:
            u = u + r_ref[...]
        mu = jnp.mean(u, axis=-1, keepdims=True)
        xc = u - mu
        var = jnp.mean(xc * xc, axis=-1, keepdims=True)
        rstd = lax.rsqrt(var + LN_EPS)
        xhat = xc * rstd
        gv = g_ref[...]
        if loss_mode:
            err = (xhat * gv + b_ref[...]) - dy_ref[...]
            acc_l[...] += jnp.sum((err * err).reshape(tm // 8, 8, D_MODEL), axis=0)
            dyv = err * (1.0 / D_MODEL)
        else:
            dyv = dy_ref[...]
        dxh = dyv * gv
        du = rstd * (dxh - jnp.mean(dxh, axis=-1, keepdims=True) - xhat * jnp.mean(dxh * xhat, axis=-1, keepdims=True))
        du_ref[...] = du
        dub_ref[...] = du.astype(BF16)
        acc_g[...] += jnp.sum((dyv * xhat).reshape(tm // 8, 8, D_MODEL), axis=0)
        acc_b[...] += jnp.sum(dyv.reshape(tm // 8, 8, D_MODEL), axis=0)

        @pl.when(i == nt - 1)
        def _():
            dg_ref[...] = jnp.sum(acc_g[...], axis=0, keepdims=True)
            db_ref[...] = jnp.sum(acc_b[...], axis=0, keepdims=True)
            if loss_mode:
                tot = jnp.sum(jnp.sum(acc_l[...], axis=0, keepdims=True), axis=1, keepdims=True)
                loss_ref[...] = tot * (0.5 / D_MODEL)

    row = pl.BlockSpec((tm, D_MODEL), lambda i: (i, 0))
    vec = pl.BlockSpec((1, D_MODEL), lambda i: (0, 0))
    out_shape = [SDS((s, D_MODEL), F32), SDS((s, D_MODEL), BF16), SDS((1, D_MODEL), F32), SDS((1, D_MODEL), F32)]
    out_specs = [row, row, vec, vec]
    scratch = [pltpu.VMEM((8, D_MODEL), F32), pltpu.VMEM((8, D_MODEL), F32)]
    if loss_mode:
        out_shape.append(SDS((1, 1), F32))
        out_specs.append(pl.BlockSpec((1, 1), lambda i: (0, 0)))
        scratch.append(pltpu.VMEM((8, D_MODEL), F32))
    args = [a] + ([r] if has_r else []) + [dy, g, b]
    return _pcall(body, name=name, out_shape=tuple(out_shape), grid=(nt,), in_specs=[row] * (2 + has_r) + [vec, vec],
                  out_specs=tuple(out_specs), scratch_shapes=scratch, dims=("arbitrary",))(*args)


def _xattn_fwd(q, k, v):
    s = q.shape[0]
    tq = _pick(s, (512,))
    scale = X_HEAD_DIM ** -0.5

    def body(q_ref, k_ref, v_ref, o_ref, ob_ref):
        qv, kv, vv = q_ref[...], k_ref[...], v_ref[...]
        outs = []
        for h in range(X_HEADS):
            sl = slice(h * X_HEAD_DIM, (h + 1) * X_HEAD_DIM)
            sc = lax.dot_general(qv[:, sl], kv[:, sl], _NT, preferred_element_type=F32) * scale
            e = jnp.exp(sc - jnp.max(sc, axis=-1, keepdims=True))
            p = e / jnp.sum(e, axis=-1, keepdims=True)
            outs.append(lax.dot_general(p.astype(BF16), vv[:, sl], _NN, preferred_element_type=F32))
        o = jnp.concatenate(outs, axis=1)
        o_ref[...] = o
        ob_ref[...] = o.astype(BF16)

    row = pl.BlockSpec((tq, D_MODEL), lambda i: (i, 0))
    full = pl.BlockSpec((MEM_LEN, D_MODEL), lambda i: (0, 0))
    return _pcall(body, name="xattn_fwd", out_shape=(SDS((s, D_MODEL), F32), SDS((s, D_MODEL), BF16)), grid=(s // tq,),
                  in_specs=[row, full, full], out_specs=(row, row), dims=("parallel",))(q, k, v)


def _xattn_bwd(q, k, v, o, do):
    s = q.shape[0]
    tq = _pick(s, (512,))
    scale = X_HEAD_DIM ** -0.5

    def body(q_ref, k_ref, v_ref, o_ref, do_ref, dq_ref, dk_ref, dv_ref):
        i = pl.program_id(0)

        @pl.when(i == 0)
        def _():
            dk_ref[...] = jnp.zeros_like(dk_ref)
            dv_ref[...] = jnp.zeros_like(dv_ref)

        qv, kv, vv, ov, dov = q_ref[...], k_ref[...], v_ref[...], o_ref[...], do_ref[...]
        dqs, dks, dvs = [], [], []
        for h in range(X_HEADS):
            sl = slice(h * X_HEAD_DIM, (h + 1) * X_HEAD_DIM)
            sc = lax.dot_general(qv[:, sl], kv[:, sl], _NT, preferred_element_type=F32) * scale
            e = jnp.exp(sc - jnp.max(sc, axis=-1, keepdims=True))
            p = e / jnp.sum(e, axis=-1, keepdims=True)
            doh = dov[:, sl]
            dob = doh.astype(BF16)
            delta = jnp.sum(doh * ov[:, sl], axis=-1, keepdims=True)
            dvs.append(lax.dot_general(p.astype(BF16), dob, _TN, preferred_element_type=F32))
            dp = lax.dot_general(dob, vv[:, sl], _NT, preferred_element_type=F32)
            ds = (p * (dp - delta)).astype(BF16)
            dqs.append(lax.dot_general(ds, kv[:, sl], _NN, preferred_element_type=F32) * scale)
            dks.append(lax.dot_general(ds, qv[:, sl], _TN, preferred_element_type=F32) * scale)
        dq_ref[...] = jnp.concatenate(dqs, axis=1).astype(BF16)
        dk_ref[...] += jnp.concatenate(dks, axis=1)
        dv_ref[...] += jnp.concatenate(dvs, axis=1)

    row = pl.BlockSpec((tq, D_MODEL), lambda i: (i, 0))
    full = pl.BlockSpec((MEM_LEN, D_MODEL), lambda i: (0, 0))
    return _pcall(body, name="xattn_bwd",
                  out_shape=(SDS((s, D_MODEL), BF16), SDS((MEM_LEN, D_MODEL), F32), SDS((MEM_LEN, D_MODEL), F32)),
                  grid=(s // tq,), in_specs=[row, full, full, row, row], out_specs=(row, full, full),
                  dims=("arbitrary",))(q, k, v, o, do)


_SQRT_HALF = 0.7071067811865476
_INV_SQRT_2PI = 0.3989422804014327


def _halo_specs(s, tm, width):
    n8 = s // 8
    r8 = tm // 8
    prev = pl.BlockSpec((8, width), lambda i: (jnp.maximum(i * r8 - 1, 0), 0))
    nxt = pl.BlockSpec((8, width), lambda i: (jnp.minimum((i + 1) * r8, n8 - 1), 0))
    return prev, nxt


def _shifted(x, prev8, next8, i, nt):
    tm = x.shape[0]
    row = lax.broadcasted_iota(jnp.int32, x.shape, 0)
    first = jnp.where(i == 0, 0.0, 1.0) * prev8[7:8, :]
    last = jnp.where(i == nt - 1, 0.0, 1.0) * next8[0:1, :]
    xm1 = jnp.where(row == 0, first, pltpu.roll(x, 1, 0))
    xp1 = jnp.where(row == tm - 1, last, pltpu.roll(x, tm - 1, 0))
    return xm1, xp1


def _glu_fwd(g, up, cw, cb):
    s = g.shape[0]
    tm = _pick(s, (256,))
    nt = s // tm

    def body(g_ref, gp_ref, gn_ref, up_ref, cw_ref, cb_ref, act_ref):
        i = pl.program_id(0)
        gv = g_ref[...]
        gm1, gp1 = _shifted(gv, gp_ref[...], gn_ref[...], i, nt)
        gc = gm1 * cw_ref[0:1, :] + gv * cw_ref[1:2, :] + gp1 * cw_ref[2:3, :] + cb_ref[...]
        gelu = 0.5 * gc * (1.0 + lax.erf(gc * _SQRT_HALF))
        act_ref[...] = (gelu * up_ref[...]).astype(BF16)

    row = pl.BlockSpec((tm, D_FF), lambda i: (i, 0))
    prev, nxt = _halo_specs(s, tm, D_FF)
    return _pcall(body, name="glu_fwd", out_shape=SDS((s, D_FF), BF16), grid=(nt,),
                  in_specs=[row, prev, nxt, row, pl.BlockSpec((8, D_FF), lambda i: (0, 0)),
                            pl.BlockSpec((1, D_FF), lambda i: (0, 0))],
                  out_specs=row, dims=("parallel",))(g, g, g, up, cw, cb)


def _glu_bwd(g, up, dact, cw, cb):
    s = g.shape[0]
    tm = _pick(s, (256,))
    nt = s // tm

    def body(g_ref, gp_ref, gn_ref, up_ref, da_ref, cw_ref, cb_ref, dgc_ref, dup_ref, dcw_ref, dcb_ref, a0, a1, a2, a3):
        i = pl.program_id(0)

        @pl.when(i == 0)
        def _():
            for a in (a0, a1, a2, a3):
                a[...] = jnp.zeros_like(a)

        gv = g_ref[...]
        gm1, gp1 = _shifted(gv, gp_ref[...], gn_ref[...], i, nt)
        gc = gm1 * cw_ref[0:1, :] + gv * cw_ref[1:2, :] + gp1 * cw_ref[2:3, :] + cb_ref[...]
        cdf = 0.5 * (1.0 + lax.erf(gc * _SQRT_HALF))
        pdf = jnp.exp(-0.5 * gc * gc) * _INV_SQRT_2PI
        da = da_ref[...]
        dup_ref[...] = (da * (gc * cdf)).astype(BF16)
        dgc = da * up_ref[...] * (cdf + gc * pdf)
        dgc_ref[...] = dgc

        def fold(t):
            return jnp.sum(t.reshape(tm // 8, 8, D_FF), axis=0)

        a0[...] += fold(dgc * gm1)
        a1[...] += fold(dgc * gv)
        a2[...] += fold(dgc * gp1)
        a3[...] += fold(dgc)

        @pl.when(i == nt - 1)
        def _():
            dcw_ref[...] = jnp.concatenate(
                [jnp.sum(a[...], axis=0, keepdims=True) for a in (a0, a1, a2)] + [jnp.zeros((5, D_FF), F32)], axis=0)
            dcb_ref[...] = jnp.sum(a3[...], axis=0, keepdims=True)

    row = pl.BlockSpec((tm, D_FF), lambda i: (i, 0))
    prev, nxt = _halo_specs(s, tm, D_FF)
    cw_spec = pl.BlockSpec((8, D_FF), lambda i: (0, 0))
    cb_spec = pl.BlockSpec((1, D_FF), lambda i: (0, 0))
    return _pcall(body, name="glu_bwd",
                  out_shape=(SDS((s, D_FF), F32), SDS((s, D_FF), BF16), SDS((8, D_FF), F32), SDS((1, D_FF), F32)),
                  grid=(nt,), in_specs=[row, prev, nxt, row, row, cw_spec, cb_spec],
                  out_specs=(row, row, cw_spec, cb_spec), scratch_shapes=[pltpu.VMEM((8, D_FF), F32)] * 4,
                  dims=("arbitrary",))(g, g, g, up, dact, cw, cb)


def _conv_bwd_input(dgc, cw):
    s = dgc.shape[0]
    tm = _pick(s, (256,))
    nt = s // tm

    def body(x_ref, xp_ref, xn_ref, cw_ref, o_ref):
        i = pl.program_id(0)
        xv = x_ref[...]
        xm1, xp1 = _shifted(xv, xp_ref[...], xn_ref[...], i, nt)
        o_ref[...] = (xp1 * cw_ref[0:1, :] + xv * cw_ref[1:2, :] + xm1 * cw_ref[2:3, :]).astype(BF16)

    row = pl.BlockSpec((tm, D_FF), lambda i: (i, 0))
    prev, nxt = _halo_specs(s, tm, D_FF)
    return _pcall(body, name="conv_bwd_input", out_shape=SDS((s, D_FF), BF16), grid=(nt,),
                  in_specs=[row, prev, nxt, pl.BlockSpec((8, D_FF), lambda i: (0, 0))], out_specs=row,
                  dims=("parallel",))(dgc, dgc, dgc, cw)


def _adamw(w, g, m, v, name):
    rows, cols = w.shape
    tr = _pick(rows, (256, 128, 64, 32, 16, 8))
    c1 = 1.0 - ADAM_B1 ** ADAM_STEP
    c2 = 1.0 - ADAM_B2 ** ADAM_STEP

    def body(w_ref, g_ref, m_ref, v_ref, d_ref, nm_ref, nv_ref):
        gv = g_ref[...]
        nm = ADAM_B1 * m_ref[...] + (1.0 - ADAM_B1) * gv
        nv = ADAM_B2 * v_ref[...] + (1.0 - ADAM_B2) * (gv * gv)
        d_ref[...] = -ADAM_LR * ((nm / c1) / (jnp.sqrt(nv / c2) + ADAM_EPS) + ADAM_WD * w_ref[...])
        nm_ref[...] = nm
        nv_ref[...] = nv

    blk = pl.BlockSpec((tr, cols), lambda i: (i, 0))
    return _pcall(body, name=name, out_shape=(SDS(w.shape, F32),) * 3, grid=(rows // tr,), in_specs=[blk] * 4,
                  out_specs=(blk,) * 3, dims=("parallel",))(w, g, m, v)


def _all_gather_rows(x_shard, *, name, in_vmem, sum_rows=False):
    m_per, n = x_shard.shape

    def body(x_ref, out_ref, *rest):
        if sum_rows:
            sum_ref, send_sems, recv_sems, local_sem = rest
        else:
            send_sems, recv_sems, local_sem = rest
        x, y, c = lax.axis_index("x"), lax.axis_index("y"), lax.axis_index("c")
        me, sibling = (x, y, c), (x, y, 1 - c)
        chips = [(1 - x, y), (x, 1 - y), (1 - x, 1 - y)]

        def rows(px, py, pc):
            return out_ref.at[pl.ds((4 * px + 2 * py + pc) * m_per, m_per), :]

        def copy(k, block, to, src=None):
            return pltpu.make_async_remote_copy(
                src_ref=rows(*block) if src is None else src, dst_ref=rows(*block), send_sem=send_sems.at[k],
                recv_sem=recv_sems.at[k], device_id=to, device_id_type=pl.DeviceIdType.MESH)

        mine = pltpu.make_async_copy(x_ref, rows(*me), local_sem)
        mine.start()
        first = [copy(0, me, sibling, src=x_ref)]
        first += [copy(1 + j, me, (*chip, c), src=x_ref) for j, chip in enumerate(chips)]
        for cp in first:
            cp.start()
        passed = [copy(4 + j, (*chip, c), sibling) for j, chip in enumerate(chips)]
        for j, chip in enumerate(chips):
            copy(1 + j, (*chip, c), me).wait_recv()
            passed[j].start()
        copy(0, sibling, me).wait_recv()
        for j, chip in enumerate(chips):
            copy(4 + j, (*chip, 1 - c), me).wait_recv()
        for cp in first + passed:
            cp.wait_send()
        mine.wait()
        if sum_rows:
            acc = out_ref[0:m_per, :]
            for dev in range(1, N_DEV):
                acc = acc + out_ref[dev * m_per:(dev + 1) * m_per, :]
            sum_ref[...] = acc

    space = pltpu.VMEM if in_vmem else pl.ANY
    out_shape = [SDS((N_DEV * m_per, n), x_shard.dtype)]
    out_specs = [pl.BlockSpec(memory_space=space)]
    if sum_rows:
        out_shape.append(SDS((m_per, n), x_shard.dtype))
        out_specs.append(pl.BlockSpec(memory_space=pltpu.VMEM))
    res = _PALLAS_CALL(
        body, name=name, out_shape=tuple(out_shape), in_specs=[pl.BlockSpec(memory_space=space)],
        out_specs=tuple(out_specs),
        scratch_shapes=[pltpu.SemaphoreType.DMA((7,)), pltpu.SemaphoreType.DMA((7,)), pltpu.SemaphoreType.DMA],
        compiler_params=pltpu.CompilerParams(vmem_limit_bytes=VMEM_LIMIT_BYTES),
    )(x_shard)
    return res if sum_rows else res[0]


_HBM = pl.BlockSpec(memory_space=pltpu.HBM)
_SEM = pl.BlockSpec(memory_space=pltpu.SEMAPHORE)
_SPLIT_PARAMS = dict(has_side_effects=pltpu.SideEffectType.DATAFLOW_SIDE_EFFECTING)


def _split_copies(src_ref, land_ref, send_sems, recv_sems, gather):
    x, y, c = lax.axis_index("x"), lax.axis_index("y"), lax.axis_index("c")
    copies = []
    for k in range(1, N_DEV):
        px = 1 - x if k & 4 else x
        py = 1 - y if k & 2 else y
        pc = 1 - c if k & 1 else c
        if gather:
            rows = src_ref.shape[0]
            src, dst = src_ref, land_ref.at[pl.ds((4 * x + 2 * y + c) * rows, rows), :]
        else:
            src, dst = src_ref.at[4 * px + 2 * py + pc], land_ref.at[k - 1]
        copies.append(pltpu.make_async_remote_copy(
            src_ref=src, dst_ref=dst, send_sem=send_sems.at[k - 1], recv_sem=recv_sems.at[k - 1],
            device_id=(px, py, pc), device_id_type=pl.DeviceIdType.MESH))
    return copies


def _exchange_start(src, land_shape, *, gather, name):
    def body(src_ref, land_ref, send_sems, recv_sems, src_thru, land_thru, token):
        for cp in _split_copies(src_ref, land_ref, send_sems, recv_sems, gather):
            cp.start()
        token[...] = jnp.zeros_like(token)

    land = pltpu.with_memory_space_constraint(lax.empty(land_shape, src.dtype), pltpu.HBM)
    return _PALLAS_CALL(
        body, name=name,
        out_shape=(pltpu.SemaphoreType.DMA((N_DEV - 1,)), pltpu.SemaphoreType.DMA((N_DEV - 1,)),
                   pltpu.HBM(src.shape, src.dtype), pltpu.HBM(land_shape, src.dtype), SDS((8, 128), F32)),
        in_specs=(_HBM, _HBM), out_specs=(_SEM, _SEM, _HBM, _HBM, pl.BlockSpec(memory_space=pltpu.VMEM)),
        input_output_aliases={0: 2, 1: 3}, compiler_params=pltpu.CompilerParams(**_SPLIT_PARAMS),
    )(pltpu.with_memory_space_constraint(src, pltpu.HBM), land)


def _exchange_wait(started, after, *, gather, name):
    send_sems, recv_sems, src_thru, land_thru, _ = started

    def body(src_ref, land_ref, send_sems, recv_sems, after_ref, src_out, land_out):
        copies = _split_copies(src_ref, land_ref, send_sems, recv_sems, gather)
        for cp in copies:
            cp.wait_send()
        for cp in copies:
            cp.wait_recv()

    return _PALLAS_CALL(
        body, name=name,
        out_shape=(pltpu.HBM(src_thru.shape, src_thru.dtype), pltpu.HBM(land_thru.shape, land_thru.dtype)),
        in_specs=(_HBM, _HBM, _SEM, _SEM, pl.BlockSpec(memory_space=pl.ANY)), out_specs=(_HBM, _HBM),
        input_output_aliases={0: 0, 1: 1}, compiler_params=pltpu.CompilerParams(**_SPLIT_PARAMS),
    )(src_thru, land_thru, send_sems, recv_sems, after)


def _sum_parts(own, land, name):
    r, n = own.shape
    tr = _pick(r, (264, 320, 336, 128, 64, 32, 16, 8))

    def body(own_ref, x_ref, o_ref):
        acc = own_ref[...]
        for k in range(N_DEV - 1):
            acc = acc + x_ref[k]
        o_ref[...] = acc

    return _pcall(body, name=name, out_shape=SDS((r, n), F32), grid=(r // tr,),
                  in_specs=[pl.BlockSpec((tr, n), lambda i: (i, 0)), pl.BlockSpec((N_DEV - 1, tr, n), lambda i: (0, i, 0))],
                  out_specs=pl.BlockSpec((tr, n), lambda i: (i, 0)), dims=("parallel",))(own, land)


def _pad_rows(a, rows):
    return jnp.pad(a, ((0, rows - a.shape[0]), (0, 0)))


def kernel(x, mem, positions, ln_in_g, ln_in_b, w_in, attn_sink, g_win, g_dil, w_mix_out, ln1_g, ln1_b, mem_ln_g, mem_ln_b, w_xq, w_xk, w_xv, w_xo, ln2_g, ln2_b, w_gate, w_up, conv_w, conv_b, w_down, ln3_g, ln3_b, loss_target, m_ln_in_g, m_ln_in_b, m_w_in, m_attn_sink, m_g_win, m_g_dil, m_w_mix_out, m_ln1_g, m_ln1_b, m_mem_ln_g, m_mem_ln_b, m_w_xq, m_w_xk, m_w_xv, m_w_xo, m_ln2_g, m_ln2_b, m_w_gate, m_w_up, m_conv_w, m_conv_b, m_w_down, m_ln3_g, m_ln3_b, v_ln_in_g, v_ln_in_b, v_w_in, v_attn_sink, v_g_win, v_g_dil, v_w_mix_out, v_ln1_g, v_ln1_b, v_mem_ln_g, v_mem_ln_b, v_w_xq, v_w_xk, v_w_xv, v_w_xo, v_ln2_g, v_ln2_b, v_w_gate, v_w_up, v_conv_w, v_conv_b, v_w_down, v_ln3_g, v_ln3_b):
    weights = dict(ln_in_g=ln_in_g, ln_in_b=ln_in_b, w_in=w_in, attn_sink=attn_sink, g_win=g_win, g_dil=g_dil, w_mix_out=w_mix_out, ln1_g=ln1_g, ln1_b=ln1_b, mem_ln_g=mem_ln_g, mem_ln_b=mem_ln_b, w_xq=w_xq, w_xk=w_xk, w_xv=w_xv, w_xo=w_xo, ln2_g=ln2_g, ln2_b=ln2_b, w_gate=w_gate, w_up=w_up, conv_w=conv_w, conv_b=conv_b, w_down=w_down, ln3_g=ln3_g, ln3_b=ln3_b)
    mom_m = dict(ln_in_g=m_ln_in_g, ln_in_b=m_ln_in_b, w_in=m_w_in, attn_sink=m_attn_sink, g_win=m_g_win, g_dil=m_g_dil, w_mix_out=m_w_mix_out, ln1_g=m_ln1_g, ln1_b=m_ln1_b, mem_ln_g=m_mem_ln_g, mem_ln_b=m_mem_ln_b, w_xq=m_w_xq, w_xk=m_w_xk, w_xv=m_w_xv, w_xo=m_w_xo, ln2_g=m_ln2_g, ln2_b=m_ln2_b, w_gate=m_w_gate, w_up=m_w_up, conv_w=m_conv_w, conv_b=m_conv_b, w_down=m_w_down, ln3_g=m_ln3_g, ln3_b=m_ln3_b)
    mom_v = dict(ln_in_g=v_ln_in_g, ln_in_b=v_ln_in_b, w_in=v_w_in, attn_sink=v_attn_sink, g_win=v_g_win, g_dil=v_g_dil, w_mix_out=v_w_mix_out, ln1_g=v_ln1_g, ln1_b=v_ln1_b, mem_ln_g=v_mem_ln_g, mem_ln_b=v_mem_ln_b, w_xq=v_w_xq, w_xk=v_w_xk, w_xv=v_w_xv, w_xo=v_w_xo, ln2_g=v_ln2_g, ln2_b=v_ln2_b, w_gate=v_w_gate, w_up=v_w_up, conv_w=v_conv_w, conv_b=v_conv_b, w_down=v_w_down, ln3_g=v_ln3_g, ln3_b=v_ln3_b)
    order = list(weights)
    s = x.shape[1]
    xs = x[0]
    mems = mem[0]
    target = loss_target[0]
    row = lambda a: a.reshape(1, -1)

    shard_rows = dict(w_in=w_in[0].T, w_gate=w_gate[0].T, w_up=w_up[0].T, w_mix_out=w_mix_out[0], w_xq=w_xq[0],
                      w_xk=w_xk[0], w_xv=w_xv[0], w_xo=w_xo[0], w_down=w_down[0])
    me_lin = 4 * lax.axis_index("x") + 2 * lax.axis_index("y") + lax.axis_index("c")
    w_in_full = _all_gather_rows(shard_rows["w_in"].astype(BF16), name="w_in_all_gather", in_vmem=False)
    w_in_t = jnp.concatenate([w_in_full[768:], w_in_full[:768]], axis=0)
    late_rows = PACK_ROWS[1:]
    late_r = sum(r for _, r in late_rows)
    packed = jnp.concatenate([shard_rows[n].astype(BF16) for n, _ in late_rows], axis=0)
    w_started = _exchange_start(packed, (N_DEV * late_r, D_MODEL), gather=True, name="weight_gather_start")
    cw_pad = jnp.pad(conv_w[0], ((0, 5), (0, 32)))
    cw_all = _all_gather_rows(cw_pad, name="conv_w_all_gather", in_vmem=True).reshape(N_DEV, 8, 384)
    cw_full = jnp.transpose(cw_all[:, :3, :352], (1, 0, 2)).reshape(3, D_FF)
    cw8 = _pad_rows(cw_full, 8)

    tab = _rope_tables(positions.astype(F32).reshape(s, 1) + w_started[4][0, 0])
    h0, h0b = _ln_fwd(xs, None, row(ln_in_g), row(ln_in_b), 1.0, "ln_in_fwd")
    zr = _proj_rope(h0b, w_in_t, tab)
    oa, lse_a = _banded_fwd(zr, attn_sink, name="win_attn_fwd", **_WIN_CFG)
    ogs, lgs = [], []
    for gi in range(3):
        o_g, l_g = _banded_fwd(zr, None, name=f"dil_attn_fwd{gi}", **_dil_cfg(gi))
        ogs.append(o_g)
        lgs.append(l_g)
    mixed, ob, lse_b = _mix_norm_fwd(oa, ogs, lgs, g_win, g_dil)
    packed_thru, land = _exchange_wait(w_started, mixed, gather=True, name="weight_gather_wait")
    gathered = lax.dynamic_update_slice(land, packed_thru, (me_lin * late_r, 0)).reshape(N_DEV, late_r, D_MODEL)
    full = {}
    off = 0
    for n, r in late_rows:
        full[n] = gathered[:, off:off + r, :].reshape(N_DEV * r, D_MODEL)
        off += r
    mix = _mm(mixed, full["w_mix_out"], trans_b=False, out_dtype=F32, name="mm_mix_out")
    h1, h1b = _ln_fwd(h0, mix, ln1_g, ln1_b, ALPHA, "ln1_fwd")
    _, mem_nb = _ln_fwd(mems, None, mem_ln_g, mem_ln_b, 1.0, "mem_ln_fwd")
    kx = _mm(mem_nb, full["w_xk"], trans_b=False, out_dtype=BF16, name="mm_xk")
    vx = _mm(mem_nb, full["w_xv"], trans_b=False, out_dtype=BF16, name="mm_xv")
    qx = _mm(h1b, full["w_xq"], trans_b=False, out_dtype=BF16, name="mm_xq")
    ox, oxb = _xattn_fwd(qx, kx, vx)
    xa = _mm(oxb, full["w_xo"], trans_b=False, out_dtype=F32, name="mm_xo")
    h2, h2b = _ln_fwd(h1, xa, ln2_g, ln2_b, ALPHA, "ln2_fwd")
    gate = _mm(h2b, full["w_gate"], trans_b=True, out_dtype=F32, name="mm_gate")
    up = _mm(h2b, full["w_up"], trans_b=True, out_dtype=F32, name="mm_up")
    act = _glu_fwd(gate, up, cw8, conv_b)
    ff = _mm(act, full["w_down"], trans_b=False, out_dtype=F32, name="mm_down")

    du3, du3b, d_ln3_g, d_ln3_b, loss_local = _ln_bwd(h2, ff, target, ln3_g, ln3_b, ALPHA, "ln3_bwd_loss",
                                                      loss_mode=True)
    dact = _mm(du3b, full["w_down"], trans_b=True, out_dtype=F32, name="mm_d_act")
    dw_down = _mm_tn(act, du3b, name="mm_dw_down")
    dgc, dup, dcw8, d_conv_b = _glu_bwd(gate, up, dact, cw8, conv_b)
    dgate = _conv_bwd_input(dgc, cw8)
    dh2 = _mm(dgate, full["w_gate"], trans_b=False, out_dtype=F32, name="mm_dh2", more=((dup, full["w_up"], False),),
              addends=(du3,), coefs=(ALPHA,))
    dw_gate_t = _mm_tn(dgate, h2b, name="mm_dw_gate")
    dw_up_t = _mm_tn(dup, h2b, name="mm_dw_up")
    rows_of = dict(PACK_ROWS)

    def start_grad_exchange(parts, name):
        gp = jnp.concatenate([g.reshape(N_DEV, rows_of[n], D_MODEL) for n, g in parts], axis=1)
        return _exchange_start(gp, (N_DEV - 1,) + gp.shape[1:], gather=False, name=name)

    ffn_parts = (("w_gate", dw_gate_t), ("w_up", dw_up_t), ("w_down", dw_down))
    ffn_started = start_grad_exchange(ffn_parts, "grad_start_ffn")
    du2, du2b, d_ln2_g, d_ln2_b = _ln_bwd(h1, xa, dh2, ln2_g + ffn_started[4][0, 0], ln2_b, ALPHA, "ln2_bwd")
    dox = _mm(du2b, full["w_xo"], trans_b=True, out_dtype=F32, name="mm_d_ox")
    dw_xo = _mm_tn(oxb, du2b, name="mm_dw_xo")
    dqx, dkx, dvx = _xattn_bwd(qx, kx, vx, ox, dox)
    dh1 = _mm(dqx, full["w_xq"], trans_b=True, out_dtype=F32, name="mm_dh1", addends=(du2,), coefs=(ALPHA,))
    dw_xq = _mm_tn(h1b, dqx, name="mm_dw_xq")
    dw_xk = _mm_tn(mem_nb, dkx, name="mm_dw_xk")
    dw_xv = _mm_tn(mem_nb, dvx, name="mm_dw_xv")
    dmem_n = _mm(dkx, full["w_xk"], trans_b=True, out_dtype=F32, name="mm_dmem", more=((dvx, full["w_xv"], True),))
    _, _, d_mem_ln_g, d_mem_ln_b = _ln_bwd(mems, None, dmem_n, mem_ln_g, mem_ln_b, 1.0, "mem_ln_bwd")
    du1, du1b, d_ln1_g, d_ln1_b = _ln_bwd(h0, mix, dh1, ln1_g, ln1_b, ALPHA, "ln1_bwd")
    dmixed = _mm(du1b, full["w_mix_out"], trans_b=True, out_dtype=F32, name="mm_d_mixed")
    dw_mix_out = _mm_tn(mixed, du1b, name="mm_dw_mix_out")
    attn_parts = (("w_mix_out", dw_mix_out), ("w_xq", dw_xq), ("w_xk", dw_xk), ("w_xv", dw_xv), ("w_xo", dw_xo))
    attn_started = start_grad_exchange(attn_parts, "grad_start_attn")
    doa, dob, d_g_win, d_g_dil = _mix_norm_bwd(oa, ob, dmixed, g_win + attn_started[4][0, 0], g_dil)
    dqa, dkva, dsink8 = _banded_bwd(zr, oa, lse_a, doa, tab, attn_sink, kv_heads=_WIN_KV, name="win_attn_bwd", **_WIN_CFG)
    dqs, dks, dvs = [], [], []
    for gi in range(3):
        dq_g, dkv_g = _banded_bwd(zr, ob, lse_b, dob, tab, None, kv_heads=_DIL_KV, name=f"dil_attn_bwd{gi}", **_dil_cfg(gi))
        dqs.append(dq_g)
        dks.append(dkv_g[:, :512])
        dvs.append(dkv_g[:, 512:])
    dz = jnp.concatenate(dqs + dks + dvs + [dqa, dkva], axis=1)
    dw_in_tz = _mm_tn(dz, h0b, name="mm_dw_in")
    dw_in_t = jnp.concatenate([dw_in_tz[4608:], dw_in_tz[:4608]], axis=0)
    in_parts = (("w_in", dw_in_t),)
    in_started = start_grad_exchange(in_parts, "grad_start_in")
    dh0 = _mm(dz, w_in_t, trans_b=False, out_dtype=F32, name="mm_dh0", addends=(du1,), coefs=(ALPHA,),
              after=in_started[4])
    dx, _, d_ln_in_g, d_ln_in_b = _ln_bwd(xs, None, dh0, row(ln_in_g), row(ln_in_b), 1.0, "ln_in_bwd")

    grads = {}
    small = jnp.concatenate([
        d_ln_in_g, d_ln_in_b, d_ln1_g, d_ln1_b, d_mem_ln_g, d_mem_ln_b, d_ln2_g, d_ln2_b, d_ln3_g, d_ln3_b,
        jnp.concatenate([d_g_win, d_g_dil], axis=1),
        jnp.pad(d_conv_b, ((0, 0), (0, 3072 - D_FF))).reshape(3, 1024),
        jnp.pad(dsink8[0:1, :], ((0, 0), (0, 1024 - 128))),
        jnp.pad(dcw8[0:3], ((0, 0), (0, 3072 - D_FF))).reshape(9, 1024),
    ], axis=0)
    _, ssum = _all_gather_rows(small, name="small_grad_all_reduce", in_vmem=True, sum_rows=True)
    names10 = ["ln_in_g", "ln_in_b", "ln1_g", "ln1_b", "mem_ln_g", "mem_ln_b", "ln2_g", "ln2_b", "ln3_g", "ln3_b"]
    for i, n in enumerate(names10):
        grads[n] = ssum[i].reshape(weights[n].shape)
    grads["g_win"] = ssum[10:11, :512]
    grads["g_dil"] = ssum[10:11, 512:]
    grads["conv_b"] = ssum[11:14].reshape(1, 3072)[:, :D_FF]
    grads["attn_sink"] = ssum[14:15, :8]
    dcw_full = ssum[15:24].reshape(3, 3072)[:, :D_FF]
    grads["conv_w"] = lax.dynamic_slice_in_dim(dcw_full, me_lin * 352, 352, axis=1)[None]

    delta, new_m, new_v = {}, {}, {}
    big = [n for n, _ in PACK_ROWS]
    small_names = [n for n in order if n not in big]

    def pack_small(src):
        rows_ = []
        for n in small_names:
            flat = src[n].reshape(1, -1)
            width = -(-flat.shape[1] // 1024) * 1024
            rows_.append(jnp.pad(flat, ((0, 0), (0, width - flat.shape[1]))).reshape(-1, 1024))
        packed_ = jnp.concatenate(rows_, axis=0)
        return _pad_rows(packed_, -(-packed_.shape[0] // 8) * 8)

    d_s, m_s, v_s = _adamw(pack_small(weights), pack_small(grads), pack_small(mom_m), pack_small(mom_v), "adamw_small")
    r0 = 0
    for n in small_names:
        size = weights[n].size
        nrow = -(-size // 1024)
        for dst, src in ((delta, d_s), (new_m, m_s), (new_v, v_s)):
            dst[n] = src[r0:r0 + nrow].reshape(-1)[:size].reshape(weights[n].shape)
        r0 += nrow

    after = d_s
    for parts, started, tag in ((ffn_parts, ffn_started, "ffn"), (attn_parts, attn_started, "attn"),
                                (in_parts, in_started, "in")):
        gp_thru, land = _exchange_wait(started, after, gather=False, name=f"grad_wait_{tag}")
        own = lax.dynamic_index_in_dim(gp_thru, me_lin, axis=0, keepdims=False)
        gsum = _sum_parts(own, land, f"grad_sum_{tag}")
        off = 0
        for n, _ in parts:
            blk = gsum[off:off + rows_of[n]]
            off += rows_of[n]
            grads[n] = (blk.T if n in ("w_in", "w_gate", "w_up") else blk)[None]
            shp = weights[n].shape
            d_, m_, v_ = _adamw(weights[n].reshape(shp[1:]), grads[n].reshape(shp[1:]), mom_m[n].reshape(shp[1:]),
                                mom_v[n].reshape(shp[1:]), f"adamw_{n}")
            delta[n], new_m[n], new_v[n] = d_.reshape(shp), m_.reshape(shp), v_.reshape(shp)
            after = d_

    loss = lax.psum(loss_local[0, 0], MESH_AXES)
    return (loss, dx[None], *[grads[n] for n in order], *[delta[n] for n in order], *[new_m[n] for n in order],
            *[new_v[n] for n in order])
```

```python
import functools
import math

import jax
import jax.numpy as jnp
from jax import lax
from jax.experimental import pallas as pl
from jax.experimental.pallas import tpu as pltpu

F32 = jnp.float32
BF16 = jnp.bfloat16
SDS = jax.ShapeDtypeStruct
_PALLAS_CALL = pl.pallas_call

D_MODEL = 1024
HEAD_DIM = 64
WIN_HALF = 128
DIL_PAIRS = ((128, 1), (512, 4), (2048, 16))
DIL_SIDE = 64
ROT_DIM = 16
ROPE_THETA = 500000.0
MEM_LEN = 256
X_HEADS = 4
X_HEAD_DIM = 256
D_FF = 2816
IN_WIDTH = 5376
Z_QB, Z_KB, Z_VB, Z_QA, Z_KA, Z_VA = 0, 1536, 3072, 4608, 5120, 5248
ALPHA = (2.0) ** 0.25
LN_EPS = 1e-5
NEG_INF = -1e30
ADAM_LR, ADAM_B1, ADAM_B2, ADAM_EPS, ADAM_WD, ADAM_STEP = 0.001, 0.9, 0.999, 1e-08, 0.01, 10
N_DEV = 8
MESH_AXES = ("x", "y", "c")
VMEM_LIMIT_BYTES = 52 * 1024 * 1024
ATTN_TQ = 256
TABW = 384

PACK_ROWS = (("w_in", 672), ("w_gate", 352), ("w_up", 352), ("w_mix_out", 128), ("w_xq", 128), ("w_xk", 128),
             ("w_xv", 128), ("w_xo", 128), ("w_down", 352))
SMALL_ROWS = 24


def _pick(n, cands):
    for c in cands:
        if n % c == 0:
            return c
    return n


def _pcall(body, *, name, out_shape, grid=None, in_specs=None, out_specs=None, scratch_shapes=(), dims=None,
           aliases=None):
    kw = {}
    if grid is not None:
        kw["grid"] = grid
    if in_specs is not None:
        kw["in_specs"] = in_specs
    if out_specs is not None:
        kw["out_specs"] = out_specs
    if aliases:
        kw["input_output_aliases"] = aliases
    return _PALLAS_CALL(
        body, name=name, out_shape=out_shape, scratch_shapes=list(scratch_shapes),
        compiler_params=pltpu.CompilerParams(dimension_semantics=dims, vmem_limit_bytes=VMEM_LIMIT_BYTES), **kw)


MM_VMEM_BUDGET = 40 * 1024 * 1024


def _mm(a, b, *, trans_b, out_dtype, name, addends=(), coefs=(), after=None, more=()):
    pairs = ((a, b, trans_b),) + tuple(more)
    m = a.shape[0]
    n = b.shape[0] if trans_b else b.shape[1]
    n_add = len(addends)
    extra = [] if after is None else [after]
    out_bytes = jnp.dtype(out_dtype).itemsize

    def vmem(tm, tn):
        tot = tm * tn * (out_bytes + 4 * n_add)
        for pa, pb, _ in pairs:
            tot += tm * pa.shape[1] * pa.dtype.itemsize + pa.shape[1] * tn * pb.dtype.itemsize
        return 2 * tot

    tm, tn = next(((cm, cn) for cn in (n, 1408, 1024, 512, 256, 128) if n % cn == 0
                   for cm in (1024, 512, 256, 128) if m % cm == 0 and vmem(cm, cn) <= MM_VMEM_BUDGET))
    n_pairs = len(pairs)

    def body(*refs):
        o_ref = refs[2 * n_pairs + n_add + len(extra)]
        acc = None
        for p, (_, _, tb) in enumerate(pairs):
            dn = _NT if tb else _NN
            part = lax.dot_general(refs[2 * p][...].astype(BF16), refs[2 * p + 1][...].astype(BF16), dn,
                                   preferred_element_type=F32)
            acc = part if acc is None else acc + part
        for r_ref, c in zip(refs[2 * n_pairs:2 * n_pairs + n_add], coefs):
            acc = acc + (r_ref[...] if c == 1.0 else c * r_ref[...])
        o_ref[...] = acc.astype(out_dtype)

    in_specs, args = [], []
    for pa, pb, tb in pairs:
        k = pa.shape[1]
        in_specs.append(pl.BlockSpec((tm, k), lambda j, i: (i, 0)))
        in_specs.append(pl.BlockSpec((tn, k), lambda j, i: (j, 0)) if tb else pl.BlockSpec((k, tn), lambda j, i: (0, j)))
        args += [pa, pb]
    in_specs += [pl.BlockSpec((tm, tn), lambda j, i: (i, j)) for _ in addends]
    in_specs += [pl.BlockSpec((8, 128), lambda j, i: (0, 0)) for _ in extra]
    return _pcall(body, name=name, out_shape=SDS((m, n), out_dtype), grid=(n // tn, m // tm), in_specs=in_specs,
                  out_specs=pl.BlockSpec((tm, tn), lambda j, i: (i, j)),
                  dims=("parallel", "parallel"))(*args, *addends, *extra)


def _mm_tn(a, b, *, name):
    s, m = a.shape
    n = b.shape[1]
    tm = _pick(m, (768, 1408, 1024, 512, 256, 128))
    tk = _pick(s, (1024, 512, 256))
    nk = s // tk

    def body(a_ref, b_ref, o_ref, acc_ref):
        kk = pl.program_id(1)

        @pl.when(kk == 0)
        def _():
            acc_ref[...] = jnp.zeros_like(acc_ref)

        acc_ref[...] += lax.dot_general(a_ref[...].astype(BF16), b_ref[...].astype(BF16), (((0,), (0,)), ((), ())),
                                        preferred_element_type=F32)

        @pl.when(kk == nk - 1)
        def _():
            o_ref[...] = acc_ref[...]

    return _pcall(body, name=name, out_shape=SDS((m, n), F32), grid=(m // tm, nk),
                  in_specs=[pl.BlockSpec((tk, tm), lambda i, kk: (kk, i)), pl.BlockSpec((tk, n), lambda i, kk: (kk, 0))],
                  out_specs=pl.BlockSpec((tm, n), lambda i, kk: (i, 0)), scratch_shapes=[pltpu.VMEM((tm, n), F32)],
                  dims=("parallel", "arbitrary"))(a, b)


def _rope_lane_consts():
    lane = jnp.arange(128)
    j = lane % HEAD_DIM
    inv_freq = ROPE_THETA ** (-jnp.arange(0, ROT_DIM, 2, dtype=F32) / ROT_DIM)
    freq = jnp.where(j < ROT_DIM, inv_freq[j % (ROT_DIM // 2)], 0.0).astype(F32)
    lo = (j < ROT_DIM // 2).astype(F32)
    hi = ((j >= ROT_DIM // 2) & (j < ROT_DIM)).astype(F32)
    return jnp.stack([freq, lo, hi] + [jnp.zeros((128,), F32)] * 5)


def _to_classes(x, scr, d):
    if d == 1:
        return [x]
    scr[...] = x
    return [scr[pl.ds(c, x.shape[0] // d, stride=d), :] for c in range(d)]


def _from_classes(parts, scr):
    d = len(parts)
    if d == 1:
        return parts[0]
    for c, part in enumerate(parts):
        scr[pl.ds(c, part.shape[0], stride=d), :] = part
    return scr[...]


DILATIONS = tuple(d for _, d in DIL_PAIRS)


def _rope_tables(posf):
    s = posf.shape[0]
    tm = _pick(s, (1024, 512))

    def body(p_ref, c_ref, *rest):
        o_refs, scr = rest[:-1], rest[-1]
        ang = p_ref[...] * c_ref[0:1, :]
        lo = c_ref[1:2, :]
        hi = c_ref[2:3, :]
        cs = jnp.cos(ang)
        sn = jnp.sin(ang)
        for q, t in enumerate((jnp.where(lo + hi > 0.0, cs, 1.0), -sn * lo, sn * hi)):
            for o_ref, d in zip(o_refs, DILATIONS):
                for c, part in enumerate(_to_classes(t, scr, d)):
                    o_ref[:, c * TABW + q * 128:c * TABW + (q + 1) * 128] = part

    return _pcall(body, name="rope_tables", out_shape=tuple(SDS((s // d, d * TABW), F32) for d in DILATIONS),
                  grid=(s // tm,),
                  in_specs=[pl.BlockSpec((tm, 1), lambda i: (i, 0)), pl.BlockSpec((8, 128), lambda i: (0, 0))],
                  out_specs=tuple(pl.BlockSpec((tm // d, d * TABW), lambda i: (i, 0)) for d in DILATIONS),
                  scratch_shapes=[pltpu.VMEM((tm, 128), F32)], dims=("parallel",))(posf, _rope_lane_consts())


def _rope_apply(x, tab, sign):
    w = x.shape[1]
    rep = w // 128
    c = jnp.tile(tab[:, 0:128], (1, rep)) if rep > 1 else tab[:, 0:128]
    a = jnp.tile(tab[:, 128:256], (1, rep)) if rep > 1 else tab[:, 128:256]
    b = jnp.tile(tab[:, 256:384], (1, rep)) if rep > 1 else tab[:, 256:384]
    up = pltpu.roll(x, w - 8, 1)
    dn = pltpu.roll(x, 8, 1)
    if sign > 0:
        return x * c + up * a + dn * b
    return x * c - up * a - dn * b


def _proj_rope(h0b, w_t, tab):
    s = h0b.shape[0]
    tm = _pick(s, (512,))
    tn = 256

    def body(a_ref, w_ref, t_ref, zw_ref, z0_ref, z1_ref, z2_ref, scr):
        z_refs = (z0_ref, z1_ref, z2_ref)
        a = a_ref[...]
        tabv = t_ref[...]
        for c0 in range(0, IN_WIDTH, tn):
            z = lax.dot_general(a, w_ref[c0:c0 + tn, :], _NT, preferred_element_type=F32)
            for g0 in range(c0, c0 + tn, 128):
                zg = z[:, g0 - c0:g0 - c0 + 128]
                if g0 < Z_VB or Z_QA <= g0 < Z_VA:
                    zg = _rope_apply(zg, tabv, 1)
                if g0 >= Z_QA:
                    zw_ref[:, g0 - Z_QA:g0 - Z_QA + 128] = zg.astype(BF16)
                    continue
                kind, within = divmod(g0, 1536)
                grp, off = divmod(within, 512)
                col = kind * 512 + off
                for c, part in enumerate(_to_classes(zg, scr, DILATIONS[grp])):
                    z_refs[grp][:, c * 1536 + col:c * 1536 + col + 128] = part.astype(BF16)

    return _pcall(body, name="proj_rope",
                  out_shape=(SDS((s, 768), BF16),) + tuple(SDS((s // d, d * 1536), BF16) for d in DILATIONS),
                  grid=(s // tm,),
                  in_specs=[pl.BlockSpec((tm, D_MODEL), lambda i: (i, 0)), pl.BlockSpec((IN_WIDTH, D_MODEL), lambda i: (0, 0)),
                            pl.BlockSpec((tm, TABW), lambda i: (i, 0))],
                  out_specs=(pl.BlockSpec((tm, 768), lambda i: (i, 0)),)
                  + tuple(pl.BlockSpec((tm // d, d * 1536), lambda i: (i, 0)) for d in DILATIONS),
                  scratch_shapes=[pltpu.VMEM((tm, 128), F32)], dims=("parallel",))(h0b, w_t, tab)


def _band_specs(sd, blk, tq, width, per_tok, cb):
    r = tq // blk
    nbk = sd // blk
    prev = pl.BlockSpec((blk, width), lambda c, j: (jnp.maximum(j * r - 1, 0), c * per_tok + cb))
    cur = pl.BlockSpec((tq, width), lambda c, j: (j, c * per_tok + cb))
    nxt = pl.BlockSpec((blk, width), lambda c, j: (jnp.minimum((j + 1) * r, nbk - 1), c * per_tok + cb))
    return [prev, cur, nxt]


def _band_bias(j, blk, tq, sd, rows_are_tile):
    w = tq + 2 * blk
    shape = (tq, w) if rows_are_tile else (w, tq)
    tile_pos = j * tq + lax.broadcasted_iota(jnp.int32, shape, 0 if rows_are_tile else 1)
    wide_pos = j * tq - blk + lax.broadcasted_iota(jnp.int32, shape, 1 if rows_are_tile else 0)
    ok = (jnp.abs(tile_pos - wide_pos) <= blk) & (wide_pos >= 0) & (wide_pos < sd)
    return jnp.where(ok, 0.0, NEG_INF)


_NT = (((1,), (1,)), ((), ()))
_NN = (((1,), (0,)), ((), ()))
_TN = (((0,), (0,)), ((), ()))


def _banded_fwd(zv, sink, *, d, blk, tq, ptw, qw, kw, qcb, kcb, vcb, heads, name):
    sd = zv.shape[0]
    tq = min(tq, sd)
    has_sink = sink is not None
    scale = HEAD_DIM ** -0.5

    def body(q_ref, kp, kc, kn, vp, vc, vn, *rest):
        if has_sink:
            sink_ref, o_ref, lse_ref = rest
        else:
            o_ref, lse_ref = rest
        j = pl.program_id(1)
        q = q_ref[...] * scale
        k = jnp.concatenate([kp[...], kc[...], kn[...]], axis=0)
        v = jnp.concatenate([vp[...], vc[...], vn[...]], axis=0)
        bias = _band_bias(j, blk, tq, sd, True)
        outs, lses = [], []
        for ql, kl, vl, si in heads:
            sc = lax.dot_general(q[:, ql:ql + HEAD_DIM], k[:, kl:kl + HEAD_DIM], _NT, preferred_element_type=F32) + bias
            m = jnp.max(sc, axis=-1, keepdims=True)
            if has_sink:
                m = jnp.maximum(m, sink_ref[0, si])
            p = jnp.exp(sc - m)
            den = jnp.sum(p, axis=-1, keepdims=True)
            if has_sink:
                den = den + jnp.exp(sink_ref[0, si] - m)
            o = lax.dot_general(p.astype(BF16), v[:, vl:vl + HEAD_DIM], _NN, preferred_element_type=F32) / den
            outs.append(o)
            lses.append(jnp.broadcast_to(m + jnp.log(den), (tq, HEAD_DIM)))
        o_ref[...] = jnp.concatenate(outs, axis=1)
        lse_ref[...] = jnp.concatenate(lses, axis=1)

    in_specs = ([pl.BlockSpec((tq, qw), lambda c, j: (j, c * (ptw // qw) + qcb))]
                + _band_specs(sd, blk, tq, kw, ptw // kw, kcb) + _band_specs(sd, blk, tq, kw, ptw // kw, vcb))
    args = [zv] * 7
    if has_sink:
        in_specs.append(pl.BlockSpec(memory_space=pltpu.SMEM))
        args.append(sink)
    o_spec = pl.BlockSpec((tq, qw), lambda c, j: (j, c))
    return _pcall(body, name=name, out_shape=(SDS((sd, d * qw), F32), SDS((sd, d * qw), F32)), grid=(d, sd // tq),
                  in_specs=in_specs, out_specs=(o_spec, o_spec), dims=("parallel", "parallel"))(*args)


def _banded_bwd(zv, ov, lv, dov, tv, sink, *, d, blk, tq, ptw, qw, kw, qcb, kcb, vcb, heads, kv_heads, name):
    sd = zv.shape[0]
    tq = min(tq, sd)
    has_sink = sink is not None
    scale = HEAD_DIM ** -0.5
    kvw = HEAD_DIM * len(kv_heads)
    b0, b1 = blk, blk + tq

    def body(*refs):
        q3 = jnp.concatenate([r[...] for r in refs[0:3]], axis=0) * scale
        k3 = jnp.concatenate([r[...] for r in refs[3:6]], axis=0)
        v3 = jnp.concatenate([r[...] for r in refs[6:9]], axis=0)
        o3 = jnp.concatenate([r[...] for r in refs[9:12]], axis=0)
        l3 = jnp.concatenate([r[...] for r in refs[12:15]], axis=0)
        do3 = jnp.concatenate([r[...] for r in refs[15:18]], axis=0)
        t_ref = refs[18]
        if has_sink:
            sink_ref, dq_ref, dkv_ref, dsink_ref = refs[19:]
        else:
            dq_ref, dkv_ref = refs[19:]
        j = pl.program_id(1)
        bias1 = _band_bias(j, blk, tq, sd, True)
        bias2 = _band_bias(j, blk, tq, sd, False)
        dqs = []
        dks = [None] * len(kv_heads)
        dvs = [None] * len(kv_heads)
        dsink_row = jnp.zeros((1, 128), F32)
        lane = lax.broadcasted_iota(jnp.int32, (1, 128), 1)
        for ql, kl, vl, si in heads:
            kvi = kv_heads.index((kl, vl))
            qh3 = q3[:, ql:ql + HEAD_DIM]
            kh3 = k3[:, kl:kl + HEAD_DIM]
            vh3 = v3[:, vl:vl + HEAD_DIM]
            doh3 = do3[:, ql:ql + HEAD_DIM]
            delta3 = jnp.sum(doh3 * o3[:, ql:ql + HEAD_DIM], axis=-1, keepdims=True)
            lse3 = l3[:, ql:ql + 1]
            dob3 = doh3.astype(BF16)
            sc = lax.dot_general(qh3[b0:b1], kh3, _NT, preferred_element_type=F32) + bias1
            p = jnp.exp(sc - lse3[b0:b1])
            dp = lax.dot_general(dob3[b0:b1], vh3, _NT, preferred_element_type=F32)
            ds = p * (dp - delta3[b0:b1])
            dqs.append(lax.dot_general(ds.astype(BF16), kh3, _NN, preferred_element_type=F32) * scale)
            if has_sink:
                psink = jnp.exp(sink_ref[0, si] - lse3[b0:b1])
                dsink_row = dsink_row + jnp.where(lane == si, -jnp.sum(psink * delta3[b0:b1]), 0.0)
            sc2 = lax.dot_general(qh3, kh3[b0:b1], _NT, preferred_element_type=F32) + bias2
            p2 = jnp.exp(sc2 - lse3)
            dv = lax.dot_general(p2.astype(BF16), dob3, _TN, preferred_element_type=F32)
            dp2 = lax.dot_general(dob3, vh3[b0:b1], _NT, preferred_element_type=F32)
            ds2 = p2 * (dp2 - delta3)
            dk = lax.dot_general(ds2.astype(BF16), qh3, _TN, preferred_element_type=F32)
            dks[kvi] = dk if dks[kvi] is None else dks[kvi] + dk
            dvs[kvi] = dv if dvs[kvi] is None else dvs[kvi] + dv
        tabv = t_ref[...]
        dq_ref[...] = _rope_apply(jnp.concatenate(dqs, axis=1), tabv, -1).astype(BF16)
        dk_all = jnp.concatenate(dks, axis=1) if len(dks) > 1 else dks[0]
        dv_all = jnp.concatenate(dvs, axis=1) if len(dvs) > 1 else dvs[0]
        dkv_ref[...] = jnp.concatenate([_rope_apply(dk_all, tabv, -1), dv_all], axis=1).astype(BF16)
        if has_sink:
            first = (pl.program_id(0) == 0) & (j == 0)

            @pl.when(first)
            def _():
                dsink_ref[...] = jnp.zeros_like(dsink_ref)

            dsink_ref[0:1, :] += dsink_row

    in_specs = (_band_specs(sd, blk, tq, qw, ptw // qw, qcb) + _band_specs(sd, blk, tq, kw, ptw // kw, kcb)
                + _band_specs(sd, blk, tq, kw, ptw // kw, vcb) + _band_specs(sd, blk, tq, qw, 1, 0) * 3
                + [pl.BlockSpec((tq, TABW), lambda c, j: (j, c))])
    args = [zv] * 9 + [ov] * 3 + [lv] * 3 + [dov] * 3 + [tv]
    out_shape = [SDS((sd, d * qw), BF16), SDS((sd, d * 2 * kvw), BF16)]
    out_specs = [pl.BlockSpec((tq, qw), lambda c, j: (j, c)), pl.BlockSpec((tq, 2 * kvw), lambda c, j: (j, c))]
    dims = ("parallel", "parallel")
    if has_sink:
        in_specs.append(pl.BlockSpec(memory_space=pltpu.SMEM))
        args.append(sink)
        out_shape.append(SDS((8, 128), F32))
        out_specs.append(pl.BlockSpec((8, 128), lambda c, j: (0, 0)))
        dims = ("arbitrary", "arbitrary")
    return _pcall(body, name=name, out_shape=tuple(out_shape), grid=(d, sd // tq), in_specs=in_specs,
                  out_specs=tuple(out_specs), dims=dims)(*args)


_WIN_HEADS = tuple((h * HEAD_DIM, (h // 4) * HEAD_DIM, 128 + (h // 4) * HEAD_DIM, h) for h in range(8))
_WIN_KV = ((0, 128), (64, 192))
_WIN_CFG = dict(d=1, blk=WIN_HALF, tq=ATTN_TQ, ptw=768, qw=512, kw=256, qcb=0, kcb=2, vcb=2, heads=_WIN_HEADS)
_DIL_HEADS = tuple((h * HEAD_DIM, h * HEAD_DIM, h * HEAD_DIM, h) for h in range(8))
_DIL_KV = tuple((h * HEAD_DIM, h * HEAD_DIM) for h in range(8))


def _dil_cfg(gi):
    return dict(d=DILATIONS[gi], blk=DIL_SIDE, tq=ATTN_TQ, ptw=1536, qw=512, kw=512, qcb=0, kcb=1, vcb=2,
                heads=_DIL_HEADS)


def _view_specs(tm, width):
    return tuple(pl.BlockSpec((tm // d, d * width), lambda i: (i, 0)) for d in DILATIONS)


def _mix_norm_fwd(oa, og_views, lg_views, g_win, g_dil):
    s = oa.shape[0]
    tm = _pick(s, (512,))

    def body(oa_ref, o0, o1, o2, l0, l1, l2, gw_ref, gd_ref, mixed_ref, ob0, ob1, ob2, lb0, lb1, lb2, scr, ob_s):
        o_refs, l_refs, ob_refs, lb_refs = (o0, o1, o2), (l0, l1, l2), (ob0, ob1, ob2), (lb0, lb1, lb2)
        ssq = jnp.zeros((tm, 1), F32)
        for q in range(4):
            os_, ls_ = [], []
            for g, d in enumerate(DILATIONS):
                cols = [slice(c * 512 + q * 128, c * 512 + (q + 1) * 128) for c in range(d)]
                os_.append(_from_classes([o_refs[g][:, cs] for cs in cols], scr))
                ls_.append(_from_classes([l_refs[g][:, cs] for cs in cols], scr))
            mx = jnp.maximum(jnp.maximum(ls_[0], ls_[1]), ls_[2])
            es = [jnp.exp(l - mx) for l in ls_]
            den = es[0] + es[1] + es[2]
            ob = (es[0] / den) * os_[0] + (es[1] / den) * os_[1] + (es[2] / den) * os_[2]
            lb = mx + jnp.log(den)
            ob_s[:, q * 128:(q + 1) * 128] = ob
            ssq = ssq + jnp.sum(ob * ob, axis=-1, keepdims=True)
            for g, d in enumerate(DILATIONS):
                for val, refs in ((ob, ob_refs), (lb, lb_refs)):
                    for c, part in enumerate(_to_classes(val, scr, d)):
                        refs[g][:, c * 512 + q * 128:c * 512 + (q + 1) * 128] = part
        a = oa_ref[...]
        ra = lax.rsqrt(jnp.mean(a * a, axis=-1, keepdims=True) + LN_EPS)
        rb = lax.rsqrt(ssq * (1.0 / 512) + LN_EPS)
        mixed_ref[...] = jnp.concatenate([a * ra * gw_ref[...], ob_s[...] * rb * gd_ref[...]], axis=1).astype(BF16)

    row = pl.BlockSpec((tm, 512), lambda i: (i, 0))
    vec = pl.BlockSpec((1, 512), lambda i: (0, 0))
    views = _view_specs(tm, 512)
    view_shapes = tuple(SDS((s // d, d * 512), F32) for d in DILATIONS)
    res = _pcall(body, name="mix_norm_fwd", out_shape=(SDS((s, 1024), BF16),) + view_shapes * 2, grid=(s // tm,),
                 in_specs=[row, *views, *views, vec, vec],
                 out_specs=(pl.BlockSpec((tm, 1024), lambda i: (i, 0)),) + views * 2,
                 scratch_shapes=[pltpu.VMEM((tm, 128), F32), pltpu.VMEM((tm, 512), F32)],
                 dims=("parallel",))(oa, *og_views, *lg_views, g_win, g_dil)
    return res[0], res[1:4], res[4:7]


def _mix_norm_bwd(oa, ob, dmixed, g_win, g_dil):
    s = oa.shape[0]
    tm = _pick(s, (512,))
    nt = s // tm

    def body(oa_ref, ob_ref, dm_ref, gw_ref, gd_ref, doa_ref, db0, db1, db2, dgw_ref, dgd_ref, acc_w, acc_d, scr):
        i = pl.program_id(0)

        @pl.when(i == 0)
        def _():
            acc_w[...] = jnp.zeros_like(acc_w)
            acc_d[...] = jnp.zeros_like(acc_d)

        dm = dm_ref[...]
        dxs = []
        for x_ref, g_ref, dy, acc in ((oa_ref, gw_ref, dm[:, :512], acc_w), (ob_ref, gd_ref, dm[:, 512:], acc_d)):
            x = x_ref[...]
            r = lax.rsqrt(jnp.mean(x * x, axis=-1, keepdims=True) + LN_EPS)
            dyg = dy * g_ref[...]
            dxs.append(r * dyg - x * (r * r * r) * jnp.mean(dyg * x, axis=-1, keepdims=True))
            acc[...] += jnp.sum((dy * x * r).reshape(tm // 8, 8, 512), axis=0)
        doa_ref[...] = dxs[0]
        for q in range(4):
            dq = dxs[1][:, q * 128:(q + 1) * 128]
            for db_ref, d in zip((db0, db1, db2), DILATIONS):
                for c, part in enumerate(_to_classes(dq, scr, d)):
                    db_ref[:, c * 512 + q * 128:c * 512 + (q + 1) * 128] = part

        @pl.when(i == nt - 1)
        def _():
            dgw_ref[...] = jnp.sum(acc_w[...], axis=0, keepdims=True)
            dgd_ref[...] = jnp.sum(acc_d[...], axis=0, keepdims=True)

    row = pl.BlockSpec((tm, 512), lambda i: (i, 0))
    vec = pl.BlockSpec((1, 512), lambda i: (0, 0))
    views = _view_specs(tm, 512)
    view_shapes = tuple(SDS((s // d, d * 512), F32) for d in DILATIONS)
    res = _pcall(body, name="mix_norm_bwd",
                 out_shape=(SDS((s, 512), F32),) + view_shapes + (SDS((1, 512), F32), SDS((1, 512), F32)),
                 grid=(nt,), in_specs=[row, row, pl.BlockSpec((tm, 1024), lambda i: (i, 0)), vec, vec],
                 out_specs=(row,) + views + (vec, vec),
                 scratch_shapes=[pltpu.VMEM((8, 512), F32), pltpu.VMEM((8, 512), F32), pltpu.VMEM((tm, 128), F32)],
                 dims=("arbitrary",))(oa, ob, dmixed, g_win, g_dil)
    return res[0], res[1:4], res[4], res[5]


def _dz_assemble(dq_views, dkv_views, dqa, dkva):
    s = dqa.shape[0]
    tm = _pick(s, (512,))

    def body(q0, q1, q2, kv0, kv1, kv2, qa_ref, kva_ref, o_ref, scr):
        for g, d in enumerate(DILATIONS):
            for kind, (ref, width, base) in enumerate((((q0, q1, q2)[g], 512, 0), ((kv0, kv1, kv2)[g], 1024, 0),
                                                       ((kv0, kv1, kv2)[g], 1024, 512))):
                for q in range(4):
                    src = base + q * 128
                    dst = kind * 1536 + g * 512 + q * 128
                    if d == 1:
                        o_ref[:, dst:dst + 128] = ref[:, src:src + 128]
                    else:
                        parts = [ref[:, c * width + src:c * width + src + 128].astype(F32) for c in range(d)]
                        o_ref[:, dst:dst + 128] = _from_classes(parts, scr).astype(BF16)
        o_ref[:, Z_QA:Z_QA + 512] = qa_ref[...]
        o_ref[:, Z_KA:Z_KA + 256] = kva_ref[...]

    return _pcall(body, name="dz_assemble", out_shape=SDS((s, IN_WIDTH), BF16), grid=(s // tm,),
                  in_specs=[*_view_specs(tm, 512), *_view_specs(tm, 1024), pl.BlockSpec((tm, 512), lambda i: (i, 0)),
                            pl.BlockSpec((tm, 256), lambda i: (i, 0))],
                  out_specs=pl.BlockSpec((tm, IN_WIDTH), lambda i: (i, 0)),
                  scratch_shapes=[pltpu.VMEM((tm, 128), F32)], dims=("parallel",))(*dq_views, *dkv_views, dqa, dkva)


def _ln_fwd(a, r, g, b, ca, name):
    s = a.shape[0]
    tm = _pick(s, (512, 256))
    has_r = r is not None

    def body(*refs):
        a_ref = refs[0]
        r_ref = refs[1] if has_r else None
        g_ref, b_ref, o_ref, ob_ref = refs[1 + has_r:]
        u = a_ref[...] if ca == 1.0 else ca * a_ref[...]
        if has_r:
            u = u + r_ref[...]
        mu = jnp.mean(u, axis=-1, keepdims=True)
        xc = u - mu
        var = jnp.mean(xc * xc, axis=-1, keepdims=True)
        y = xc * lax.rsqrt(var + LN_EPS) * g_ref[...] + b_ref[...]
        o_ref[...] = y
        ob_ref[...] = y.astype(BF16)

    row = pl.BlockSpec((tm, D_MODEL), lambda i: (i, 0))
    vec = pl.BlockSpec((1, D_MODEL), lambda i: (0, 0))
    args = [a] + ([r] if has_r else []) + [g, b]
    return _pcall(body, name=name, out_shape=(SDS((s, D_MODEL), F32), SDS((s, D_MODEL), BF16)), grid=(s // tm,),
                  in_specs=[row] * (1 + has_r) + [vec, vec], out_specs=(row, row), dims=("parallel",))(*args)


def _ln_bwd(a, r, dy, g, b, ca, name, loss_mode=False):
    s = a.shape[0]
    tm = _pick(s, (512, 256))
    nt = s // tm
    has_r = r is not None

    def body(*refs):
        a_ref = refs[0]
        r_ref = refs[1] if has_r else None
        dy_ref, g_ref, b_ref = refs[1 + has_r:4 + has_r]
        outs = refs[4 + has_r:]
        if loss_mode:
            du_ref, dub_ref, dg_ref, db_ref, loss_ref, acc_g, acc_b, acc_l = outs
        else:
            du_ref, dub_ref, dg_ref, db_ref, acc_g, acc_b = outs
        i = pl.program_id(0)

        @pl.when(i == 0)
        def _():
            acc_g[...] = jnp.zeros_like(acc_g)
            acc_b[...] = jnp.zeros_like(acc_b)
            if loss_mode:
                acc_l[...] = jnp.zeros_like(acc_l)

        u = a_ref[...] if ca == 1.0 else ca * a_ref[...]
        if has_r:
            u = u + r_ref[...]
        mu = jnp.mean(u, axis=-1, keepdims=True)
        xc = u - mu
        var = jnp.mean(xc * xc, axis=-1, keepdims=True)
        rstd = lax.rsqrt(var + LN_EPS)
        xhat = xc * rstd
        gv = g_ref[...]
        if loss_mode:
            err = (xhat * gv + b_ref[...]) - dy_ref[...]
            acc_l[...] += jnp.sum((err * err).reshape(tm // 8, 8, D_MODEL), axis=0)
            dyv = err * (1.0 / D_MODEL)
        else:
            dyv = dy_ref[...]
        dxh = dyv * gv
        du = rstd * (dxh - jnp.mean(dxh, axis=-1, keepdims=True) - xhat * jnp.mean(dxh * xhat, axis=-1, keepdims=True))
        du_ref[...] = du
        dub_ref[...] = du.astype(BF16)
        acc_g[...] += jnp.sum((dyv * xhat).reshape(tm // 8, 8, D_MODEL), axis=0)
        acc_b[...] += jnp.sum(dyv.reshape(tm // 8, 8, D_MODEL), axis=0)

        @pl.when(i == nt - 1)
        def _():
            dg_ref[...] = jnp.sum(acc_g[...], axis=0, keepdims=True)
            db_ref[...] = jnp.sum(acc_b[...], axis=0, keepdims=True)
            if loss_mode:
                tot = jnp.sum(jnp.sum(acc_l[...], axis=0, keepdims=True), axis=1, keepdims=True)
                loss_ref[...] = tot * (0.5 / D_MODEL)

    row = pl.BlockSpec((tm, D_MODEL), lambda i: (i, 0))
    vec = pl.BlockSpec((1, D_MODEL), lambda i: (0, 0))
    out_shape = [SDS((s, D_MODEL), F32), SDS((s, D_MODEL), BF16), SDS((1, D_MODEL), F32), SDS((1, D_MODEL), F32)]
    out_specs = [row, row, vec, vec]
    scratch = [pltpu.VMEM((8, D_MODEL), F32), pltpu.VMEM((8, D_MODEL), F32)]
    if loss_mode:
        out_shape.append(SDS((1, 1), F32))
        out_specs.append(pl.BlockSpec((1, 1), lambda i: (0, 0)))
        scratch.append(pltpu.VMEM((8, D_MODEL), F32))
    args = [a] + ([r] if has_r else []) + [dy, g, b]
    return _pcall(body, name=name, out_shape=tuple(out_shape), grid=(nt,), in_specs=[row] * (2 + has_r) + [vec, vec],
                  out_specs=tuple(out_specs), scratch_shapes=scratch, dims=("arbitrary",))(*args)


def _xattn_fwd(q, k, v):
    s = q.shape[0]
    tq = _pick(s, (512,))
    scale = X_HEAD_DIM ** -0.5

    def body(q_ref, k_ref, v_ref, o_ref, ob_ref):
        qv, kv, vv = q_ref[...], k_ref[...], v_ref[...]
        outs = []
        for h in range(X_HEADS):
            sl = slice(h * X_HEAD_DIM, (h + 1) * X_HEAD_DIM)
            sc = lax.dot_general(qv[:, sl], kv[:, sl], _NT, preferred_element_type=F32) * scale
            e = jnp.exp(sc - jnp.max(sc, axis=-1, keepdims=True))
            p = e / jnp.sum(e, axis=-1, keepdims=True)
            outs.append(lax.dot_general(p.astype(BF16), vv[:, sl], _NN, preferred_element_type=F32))
        o = jnp.concatenate(outs, axis=1)
        o_ref[...] = o
        ob_ref[...] = o.astype(BF16)

    row = pl.BlockSpec((tq, D_MODEL), lambda i: (i, 0))
    full = pl.BlockSpec((MEM_LEN, D_MODEL), lambda i: (0, 0))
    return _pcall(body, name="xattn_fwd", out_shape=(SDS((s, D_MODEL), F32), SDS((s, D_MODEL), BF16)), grid=(s // tq,),
                  in_specs=[row, full, full], out_specs=(row, row), dims=("parallel",))(q, k, v)


def _xattn_bwd(q, k, v, o, do):
    s = q.shape[0]
    tq = _pick(s, (512,))
    scale = X_HEAD_DIM ** -0.5

    def body(q_ref, k_ref, v_ref, o_ref, do_ref, dq_ref, dk_ref, dv_ref):
        i = pl.program_id(0)

        @pl.when(i == 0)
        def _():
            dk_ref[...] = jnp.zeros_like(dk_ref)
            dv_ref[...] = jnp.zeros_like(dv_ref)

        qv, kv, vv, ov, dov = q_ref[...], k_ref[...], v_ref[...], o_ref[...], do_ref[...]
        dqs, dks, dvs = [], [], []
        for h in range(X_HEADS):
            sl = slice(h * X_HEAD_DIM, (h + 1) * X_HEAD_DIM)
            sc = lax.dot_general(qv[:, sl], kv[:, sl], _NT, preferred_element_type=F32) * scale
            e = jnp.exp(sc - jnp.max(sc, axis=-1, keepdims=True))
            p = e / jnp.sum(e, axis=-1, keepdims=True)
            doh = dov[:, sl]
            dob = doh.astype(BF16)
            delta = jnp.sum(doh * ov[:, sl], axis=-1, keepdims=True)
            dvs.append(lax.dot_general(p.astype(BF16), dob, _TN, preferred_element_type=F32))
            dp = lax.dot_general(dob, vv[:, sl], _NT, preferred_element_type=F32)
            ds = (p * (dp - delta)).astype(BF16)
            dqs.append(lax.dot_general(ds, kv[:, sl], _NN, preferred_element_type=F32) * scale)
            dks.append(lax.dot_general(ds, qv[:, sl], _TN, preferred_element_type=F32) * scale)
        dq_ref[...] = jnp.concatenate(dqs, axis=1).astype(BF16)
        dk_ref[...] += jnp.concatenate(dks, axis=1)
        dv_ref[...] += jnp.concatenate(dvs, axis=1)

    row = pl.BlockSpec((tq, D_MODEL), lambda i: (i, 0))
    full = pl.BlockSpec((MEM_LEN, D_MODEL), lambda i: (0, 0))
    return _pcall(body, name="xattn_bwd",
                  out_shape=(SDS((s, D_MODEL), BF16), SDS((MEM_LEN, D_MODEL), F32), SDS((MEM_LEN, D_MODEL), F32)),
                  grid=(s // tq,), in_specs=[row, full, full, row, row], out_specs=(row, full, full),
                  dims=("arbitrary",))(q, k, v, o, do)


_SQRT_HALF = 0.7071067811865476
_INV_SQRT_2PI = 0.3989422804014327


def _halo_specs(s, tm, width):
    n8 = s // 8
    r8 = tm // 8
    prev = pl.BlockSpec((8, width), lambda i: (jnp.maximum(i * r8 - 1, 0), 0))
    nxt = pl.BlockSpec((8, width), lambda i: (jnp.minimum((i + 1) * r8, n8 - 1), 0))
    return prev, nxt


def _shifted(x, prev8, next8, i, nt):
    tm = x.shape[0]
    row = lax.broadcasted_iota(jnp.int32, x.shape, 0)
    first = jnp.where(i == 0, 0.0, 1.0) * prev8[7:8, :]
    last = jnp.where(i == nt - 1, 0.0, 1.0) * next8[0:1, :]
    xm1 = jnp.where(row == 0, first, pltpu.roll(x, 1, 0))
    xp1 = jnp.where(row == tm - 1, last, pltpu.roll(x, tm - 1, 0))
    return xm1, xp1


def _glu_fwd(g, up, cw, cb):
    s = g.shape[0]
    tm = _pick(s, (256,))
    nt = s // tm

    def body(g_ref, gp_ref, gn_ref, up_ref, cw_ref, cb_ref, act_ref):
        i = pl.program_id(0)
        gv = g_ref[...]
        gm1, gp1 = _shifted(gv, gp_ref[...], gn_ref[...], i, nt)
        gc = gm1 * cw_ref[0:1, :] + gv * cw_ref[1:2, :] + gp1 * cw_ref[2:3, :] + cb_ref[...]
        gelu = 0.5 * gc * (1.0 + lax.erf(gc * _SQRT_HALF))
        act_ref[...] = (gelu * up_ref[...]).astype(BF16)

    row = pl.BlockSpec((tm, D_FF), lambda i: (i, 0))
    prev, nxt = _halo_specs(s, tm, D_FF)
    return _pcall(body, name="glu_fwd", out_shape=SDS((s, D_FF), BF16), grid=(nt,),
                  in_specs=[row, prev, nxt, row, pl.BlockSpec((8, D_FF), lambda i: (0, 0)),
                            pl.BlockSpec((1, D_FF), lambda i: (0, 0))],
                  out_specs=row, dims=("parallel",))(g, g, g, up, cw, cb)


def _glu_bwd(g, up, dact, cw, cb):
    s = g.shape[0]
    tm = _pick(s, (256,))
    nt = s // tm

    def body(g_ref, gp_ref, gn_ref, up_ref, da_ref, cw_ref, cb_ref, dgc_ref, dup_ref, dcw_ref, dcb_ref, a0, a1, a2, a3):
        i = pl.program_id(0)

        @pl.when(i == 0)
        def _():
            for a in (a0, a1, a2, a3):
                a[...] = jnp.zeros_like(a)

        gv = g_ref[...]
        gm1, gp1 = _shifted(gv, gp_ref[...], gn_ref[...], i, nt)
        gc = gm1 * cw_ref[0:1, :] + gv * cw_ref[1:2, :] + gp1 * cw_ref[2:3, :] + cb_ref[...]
        cdf = 0.5 * (1.0 + lax.erf(gc * _SQRT_HALF))
        pdf = jnp.exp(-0.5 * gc * gc) * _INV_SQRT_2PI
        da = da_ref[...]
        dup_ref[...] = (da * (gc * cdf)).astype(BF16)
        dgc = da * up_ref[...] * (cdf + gc * pdf)
        dgc_ref[...] = dgc

        def fold(t):
            return jnp.sum(t.reshape(tm // 8, 8, D_FF), axis=0)

        a0[...] += fold(dgc * gm1)
        a1[...] += fold(dgc * gv)
        a2[...] += fold(dgc * gp1)
        a3[...] += fold(dgc)

        @pl.when(i == nt - 1)
        def _():
            dcw_ref[...] = jnp.concatenate(
                [jnp.sum(a[...], axis=0, keepdims=True) for a in (a0, a1, a2)] + [jnp.zeros((5, D_FF), F32)], axis=0)
            dcb_ref[...] = jnp.sum(a3[...], axis=0, keepdims=True)

    row = pl.BlockSpec((tm, D_FF), lambda i: (i, 0))
    prev, nxt = _halo_specs(s, tm, D_FF)
    cw_spec = pl.BlockSpec((8, D_FF), lambda i: (0, 0))
    cb_spec = pl.BlockSpec((1, D_FF), lambda i: (0, 0))
    return _pcall(body, name="glu_bwd",
                  out_shape=(SDS((s, D_FF), F32), SDS((s, D_FF), BF16), SDS((8, D_FF), F32), SDS((1, D_FF), F32)),
                  grid=(nt,), in_specs=[row, prev, nxt, row, row, cw_spec, cb_spec],
                  out_specs=(row, row, cw_spec, cb_spec), scratch_shapes=[pltpu.VMEM((8, D_FF), F32)] * 4,
                  dims=("arbitrary",))(g, g, g, up, dact, cw, cb)


def _conv_bwd_input(dgc, cw):
    s = dgc.shape[0]
    tm = _pick(s, (256,))
    nt = s // tm

    def body(x_ref, xp_ref, xn_ref, cw_ref, o_ref):
        i = pl.program_id(0)
        xv = x_ref[...]
        xm1, xp1 = _shifted(xv, xp_ref[...], xn_ref[...], i, nt)
        o_ref[...] = (xp1 * cw_ref[0:1, :] + xv * cw_ref[1:2, :] + xm1 * cw_ref[2:3, :]).astype(BF16)

    row = pl.BlockSpec((tm, D_FF), lambda i: (i, 0))
    prev, nxt = _halo_specs(s, tm, D_FF)
    return _pcall(body, name="conv_bwd_input", out_shape=SDS((s, D_FF), BF16), grid=(nt,),
                  in_specs=[row, prev, nxt, pl.BlockSpec((8, D_FF), lambda i: (0, 0))], out_specs=row,
                  dims=("parallel",))(dgc, dgc, dgc, cw)


def _adamw(w, g, m, v, name):
    rows, cols = w.shape
    tr = _pick(rows, (256, 128, 64, 32, 16, 8))
    c1 = 1.0 - ADAM_B1 ** ADAM_STEP
    c2 = 1.0 - ADAM_B2 ** ADAM_STEP

    def body(w_ref, g_ref, m_ref, v_ref, d_ref, nm_ref, nv_ref):
        gv = g_ref[...]
        nm = ADAM_B1 * m_ref[...] + (1.0 - ADAM_B1) * gv
        nv = ADAM_B2 * v_ref[...] + (1.0 - ADAM_B2) * (gv * gv)
        d_ref[...] = -ADAM_LR * ((nm / c1) / (jnp.sqrt(nv / c2) + ADAM_EPS) + ADAM_WD * w_ref[...])
        nm_ref[...] = nm
        nv_ref[...] = nv

    blk = pl.BlockSpec((tr, cols), lambda i: (i, 0))
    return _pcall(body, name=name, out_shape=(SDS(w.shape, F32),) * 3, grid=(rows // tr,), in_specs=[blk] * 4,
                  out_specs=(blk,) * 3, dims=("parallel",))(w, g, m, v)


def _all_gather_rows(x_shard, *, name, in_vmem, sum_rows=False):
    m_per, n = x_shard.shape

    def body(x_ref, out_ref, *rest):
        if sum_rows:
            sum_ref, send_sems, recv_sems, local_sem = rest
        else:
            send_sems, recv_sems, local_sem = rest
        x, y, c = lax.axis_index("x"), lax.axis_index("y"), lax.axis_index("c")
        me, sibling = (x, y, c), (x, y, 1 - c)
        chips = [(1 - x, y), (x, 1 - y), (1 - x, 1 - y)]

        def rows(px, py, pc):
            return out_ref.at[pl.ds((4 * px + 2 * py + pc) * m_per, m_per), :]

        def copy(k, block, to, src=None):
            return pltpu.make_async_remote_copy(
                src_ref=rows(*block) if src is None else src, dst_ref=rows(*block), send_sem=send_sems.at[k],
                recv_sem=recv_sems.at[k], device_id=to, device_id_type=pl.DeviceIdType.MESH)

        mine = pltpu.make_async_copy(x_ref, rows(*me), local_sem)
        mine.start()
        first = [copy(0, me, sibling, src=x_ref)]
        first += [copy(1 + j, me, (*chip, c), src=x_ref) for j, chip in enumerate(chips)]
        for cp in first:
            cp.start()
        passed = [copy(4 + j, (*chip, c), sibling) for j, chip in enumerate(chips)]
        for j, chip in enumerate(chips):
            copy(1 + j, (*chip, c), me).wait_recv()
            passed[j].start()
        copy(0, sibling, me).wait_recv()
        for j, chip in enumerate(chips):
            copy(4 + j, (*chip, 1 - c), me).wait_recv()
        for cp in first + passed:
            cp.wait_send()
        mine.wait()
        if sum_rows:
            acc = out_ref[0:m_per, :]
            for dev in range(1, N_DEV):
                acc = acc + out_ref[dev * m_per:(dev + 1) * m_per, :]
            sum_ref[...] = acc

    space = pltpu.VMEM if in_vmem else pl.ANY
    out_shape = [SDS((N_DEV * m_per, n), x_shard.dtype)]
    out_specs = [pl.BlockSpec(memory_space=space)]
    if sum_rows:
        out_shape.append(SDS((m_per, n), x_shard.dtype))
        out_specs.append(pl.BlockSpec(memory_space=pltpu.VMEM))
    res = _PALLAS_CALL(
        body, name=name, out_shape=tuple(out_shape), in_specs=[pl.BlockSpec(memory_space=space)],
        out_specs=tuple(out_specs),
        scratch_shapes=[pltpu.SemaphoreType.DMA((7,)), pltpu.SemaphoreType.DMA((7,)), pltpu.SemaphoreType.DMA],
        compiler_params=pltpu.CompilerParams(vmem_limit_bytes=VMEM_LIMIT_BYTES),
    )(x_shard)
    return res if sum_rows else res[0]


_HBM = pl.BlockSpec(memory_space=pltpu.HBM)
_SEM = pl.BlockSpec(memory_space=pltpu.SEMAPHORE)
_SPLIT_PARAMS = dict(has_side_effects=pltpu.SideEffectType.DATAFLOW_SIDE_EFFECTING)


def _split_copies(src_ref, land_ref, send_sems, recv_sems, gather):
    x, y, c = lax.axis_index("x"), lax.axis_index("y"), lax.axis_index("c")
    copies = []
    for k in range(1, N_DEV):
        px = 1 - x if k & 4 else x
        py = 1 - y if k & 2 else y
        pc = 1 - c if k & 1 else c
        if gather:
            rows = src_ref.shape[0]
            src, dst = src_ref, land_ref.at[pl.ds((4 * x + 2 * y + c) * rows, rows), :]
        else:
            src, dst = src_ref.at[4 * px + 2 * py + pc], land_ref.at[k - 1]
        copies.append(pltpu.make_async_remote_copy(
            src_ref=src, dst_ref=dst, send_sem=send_sems.at[k - 1], recv_sem=recv_sems.at[k - 1],
            device_id=(px, py, pc), device_id_type=pl.DeviceIdType.MESH))
    return copies


def _exchange_start(src, land_shape, *, gather, name):
    def body(src_ref, land_ref, send_sems, recv_sems, src_thru, land_thru, token):
        for cp in _split_copies(src_ref, land_ref, send_sems, recv_sems, gather):
            cp.start()
        token[...] = jnp.zeros_like(token)

    land = pltpu.with_memory_space_constraint(lax.empty(land_shape, src.dtype), pltpu.HBM)
    return _PALLAS_CALL(
        body, name=name,
        out_shape=(pltpu.SemaphoreType.DMA((N_DEV - 1,)), pltpu.SemaphoreType.DMA((N_DEV - 1,)),
                   pltpu.HBM(src.shape, src.dtype), pltpu.HBM(land_shape, src.dtype), SDS((8, 128), F32)),
        in_specs=(_HBM, _HBM), out_specs=(_SEM, _SEM, _HBM, _HBM, pl.BlockSpec(memory_space=pltpu.VMEM)),
        input_output_aliases={0: 2, 1: 3}, compiler_params=pltpu.CompilerParams(**_SPLIT_PARAMS),
    )(pltpu.with_memory_space_constraint(src, pltpu.HBM), land)


def _exchange_wait(started, after, *, gather, name):
    send_sems, recv_sems, src_thru, land_thru, _ = started

    def body(src_ref, land_ref, send_sems, recv_sems, after_ref, src_out, land_out):
        copies = _split_copies(src_ref, land_ref, send_sems, recv_sems, gather)
        for cp in copies:
            cp.wait_send()
        for cp in copies:
            cp.wait_recv()

    return _PALLAS_CALL(
        body, name=name,
        out_shape=(pltpu.HBM(src_thru.shape, src_thru.dtype), pltpu.HBM(land_thru.shape, land_thru.dtype)),
        in_specs=(_HBM, _HBM, _SEM, _SEM, pl.BlockSpec(memory_space=pl.ANY)), out_specs=(_HBM, _HBM),
        input_output_aliases={0: 0, 1: 1}, compiler_params=pltpu.CompilerParams(**_SPLIT_PARAMS),
    )(src_thru, land_thru, send_sems, recv_sems, after)


def _sum_parts(own, land, name):
    r, n = own.shape
    tr = _pick(r, (264, 320, 336, 128, 64, 32, 16, 8))

    def body(own_ref, x_ref, o_ref):
        acc = own_ref[...]
        for k in range(N_DEV - 1):
            acc = acc + x_ref[k]
        o_ref[...] = acc

    return _pcall(body, name=name, out_shape=SDS((r, n), F32), grid=(r // tr,),
                  in_specs=[pl.BlockSpec((tr, n), lambda i: (i, 0)), pl.BlockSpec((N_DEV - 1, tr, n), lambda i: (0, i, 0))],
                  out_specs=pl.BlockSpec((tr, n), lambda i: (i, 0)), dims=("parallel",))(own, land)


def _pad_rows(a, rows):
    return jnp.pad(a, ((0, rows - a.shape[0]), (0, 0)))


def kernel(x, mem, positions, ln_in_g, ln_in_b, w_in, attn_sink, g_win, g_dil, w_mix_out, ln1_g, ln1_b, mem_ln_g, mem_ln_b, w_xq, w_xk, w_xv, w_xo, ln2_g, ln2_b, w_gate, w_up, conv_w, conv_b, w_down, ln3_g, ln3_b, loss_target, m_ln_in_g, m_ln_in_b, m_w_in, m_attn_sink, m_g_win, m_g_dil, m_w_mix_out, m_ln1_g, m_ln1_b, m_mem_ln_g, m_mem_ln_b, m_w_xq, m_w_xk, m_w_xv, m_w_xo, m_ln2_g, m_ln2_b, m_w_gate, m_w_up, m_conv_w, m_conv_b, m_w_down, m_ln3_g, m_ln3_b, v_ln_in_g, v_ln_in_b, v_w_in, v_attn_sink, v_g_win, v_g_dil, v_w_mix_out, v_ln1_g, v_ln1_b, v_mem_ln_g, v_mem_ln_b, v_w_xq, v_w_xk, v_w_xv, v_w_xo, v_ln2_g, v_ln2_b, v_w_gate, v_w_up, v_conv_w, v_conv_b, v_w_down, v_ln3_g, v_ln3_b):
    weights = dict(ln_in_g=ln_in_g, ln_in_b=ln_in_b, w_in=w_in, attn_sink=attn_sink, g_win=g_win, g_dil=g_dil, w_mix_out=w_mix_out, ln1_g=ln1_g, ln1_b=ln1_b, mem_ln_g=mem_ln_g, mem_ln_b=mem_ln_b, w_xq=w_xq, w_xk=w_xk, w_xv=w_xv, w_xo=w_xo, ln2_g=ln2_g, ln2_b=ln2_b, w_gate=w_gate, w_up=w_up, conv_w=conv_w, conv_b=conv_b, w_down=w_down, ln3_g=ln3_g, ln3_b=ln3_b)
    mom_m = dict(ln_in_g=m_ln_in_g, ln_in_b=m_ln_in_b, w_in=m_w_in, attn_sink=m_attn_sink, g_win=m_g_win, g_dil=m_g_dil, w_mix_out=m_w_mix_out, ln1_g=m_ln1_g, ln1_b=m_ln1_b, mem_ln_g=m_mem_ln_g, mem_ln_b=m_mem_ln_b, w_xq=m_w_xq, w_xk=m_w_xk, w_xv=m_w_xv, w_xo=m_w_xo, ln2_g=m_ln2_g, ln2_b=m_ln2_b, w_gate=m_w_gate, w_up=m_w_up, conv_w=m_conv_w, conv_b=m_conv_b, w_down=m_w_down, ln3_g=m_ln3_g, ln3_b=m_ln3_b)
    mom_v = dict(ln_in_g=v_ln_in_g, ln_in_b=v_ln_in_b, w_in=v_w_in, attn_sink=v_attn_sink, g_win=v_g_win, g_dil=v_g_dil, w_mix_out=v_w_mix_out, ln1_g=v_ln1_g, ln1_b=v_ln1_b, mem_ln_g=v_mem_ln_g, mem_ln_b=v_mem_ln_b, w_xq=v_w_xq, w_xk=v_w_xk, w_xv=v_w_xv, w_xo=v_w_xo, ln2_g=v_ln2_g, ln2_b=v_ln2_b, w_gate=v_w_gate, w_up=v_w_up, conv_w=v_conv_w, conv_b=v_conv_b, w_down=v_w_down, ln3_g=v_ln3_g, ln3_b=v_ln3_b)
    order = list(weights)
    s = x.shape[1]
    xs = x[0]
    mems = mem[0]
    target = loss_target[0]
    row = lambda a: a.reshape(1, -1)

    shard_rows = dict(w_in=w_in[0].T, w_gate=w_gate[0].T, w_up=w_up[0].T, w_mix_out=w_mix_out[0], w_xq=w_xq[0],
                      w_xk=w_xk[0], w_xv=w_xv[0], w_xo=w_xo[0], w_down=w_down[0])
    me_lin = 4 * lax.axis_index("x") + 2 * lax.axis_index("y") + lax.axis_index("c")
    w_in_full = _all_gather_rows(shard_rows["w_in"].astype(BF16), name="w_in_all_gather", in_vmem=False)
    w_in_t = jnp.concatenate([w_in_full[768:], w_in_full[:768]], axis=0)
    late_rows = PACK_ROWS[1:]
    late_r = sum(r for _, r in late_rows)
    packed = jnp.concatenate([shard_rows[n].astype(BF16) for n, _ in late_rows], axis=0)
    w_started = _exchange_start(packed, (N_DEV * late_r, D_MODEL), gather=True, name="weight_gather_start")
    cw_pad = jnp.pad(conv_w[0], ((0, 5), (0, 32)))
    cw_all = _all_gather_rows(cw_pad, name="conv_w_all_gather", in_vmem=True).reshape(N_DEV, 8, 384)
    cw_full = jnp.transpose(cw_all[:, :3, :352], (1, 0, 2)).reshape(3, D_FF)
    cw8 = _pad_rows(cw_full, 8)

    tabs = _rope_tables(positions.astype(F32).reshape(s, 1) + w_started[4][0, 0])
    h0, h0b = _ln_fwd(xs, None, row(ln_in_g), row(ln_in_b), 1.0, "ln_in_fwd")
    zw, *zg = _proj_rope(h0b, w_in_t, tabs[0])
    oa, lse_a = _banded_fwd(zw, attn_sink, name="win_attn_fwd", **_WIN_CFG)
    og_views, lg_views = [], []
    for gi in range(3):
        o_g, l_g = _banded_fwd(zg[gi], None, name=f"dil_attn_fwd{gi}", **_dil_cfg(gi))
        og_views.append(o_g)
        lg_views.append(l_g)
    mixed, ob_views, lb_views = _mix_norm_fwd(oa, og_views, lg_views, g_win, g_dil)
    packed_thru, land = _exchange_wait(w_started, mixed, gather=True, name="weight_gather_wait")
    gathered = lax.dynamic_update_slice(land, packed_thru, (me_lin * late_r, 0)).reshape(N_DEV, late_r, D_MODEL)
    full = {}
    off = 0
    for n, r in late_rows:
        full[n] = gathered[:, off:off + r, :].reshape(N_DEV * r, D_MODEL)
        off += r
    mix = _mm(mixed, full["w_mix_out"], trans_b=False, out_dtype=F32, name="mm_mix_out")
    h1, h1b = _ln_fwd(h0, mix, ln1_g, ln1_b, ALPHA, "ln1_fwd")
    _, mem_nb = _ln_fwd(mems, None, mem_ln_g, mem_ln_b, 1.0, "mem_ln_fwd")
    kx = _mm(mem_nb, full["w_xk"], trans_b=False, out_dtype=BF16, name="mm_xk")
    vx = _mm(mem_nb, full["w_xv"], trans_b=False, out_dtype=BF16, name="mm_xv")
    qx = _mm(h1b, full["w_xq"], trans_b=False, out_dtype=BF16, name="mm_xq")
    ox, oxb = _xattn_fwd(qx, kx, vx)
    xa = _mm(oxb, full["w_xo"], trans_b=False, out_dtype=F32, name="mm_xo")
    h2, h2b = _ln_fwd(h1, xa, ln2_g, ln2_b, ALPHA, "ln2_fwd")
    gate = _mm(h2b, full["w_gate"], trans_b=True, out_dtype=F32, name="mm_gate")
    up = _mm(h2b, full["w_up"], trans_b=True, out_dtype=F32, name="mm_up")
    act = _glu_fwd(gate, up, cw8, conv_b)
    ff = _mm(act, full["w_down"], trans_b=False, out_dtype=F32, name="mm_down")

    du3, du3b, d_ln3_g, d_ln3_b, loss_local = _ln_bwd(h2, ff, target, ln3_g, ln3_b, ALPHA, "ln3_bwd_loss",
                                                      loss_mode=True)
    dact = _mm(du3b, full["w_down"], trans_b=True, out_dtype=F32, name="mm_d_act")
    dw_down = _mm_tn(act, du3b, name="mm_dw_down")
    dgc, dup, dcw8, d_conv_b = _glu_bwd(gate, up, dact, cw8, conv_b)
    dgate = _conv_bwd_input(dgc, cw8)
    dh2 = _mm(dgate, full["w_gate"], trans_b=False, out_dtype=F32, name="mm_dh2", more=((dup, full["w_up"], False),),
              addends=(du3,), coefs=(ALPHA,))
    dw_gate_t = _mm_tn(dgate, h2b, name="mm_dw_gate")
    dw_up_t = _mm_tn(dup, h2b, name="mm_dw_up")
    rows_of = dict(PACK_ROWS)

    def start_grad_exchange(parts, name):
        gp = jnp.concatenate([g.reshape(N_DEV, rows_of[n], D_MODEL) for n, g in parts], axis=1)
        return _exchange_start(gp, (N_DEV - 1,) + gp.shape[1:], gather=False, name=name)

    ffn_parts = (("w_gate", dw_gate_t), ("w_up", dw_up_t), ("w_down", dw_down))
    ffn_started = start_grad_exchange(ffn_parts, "grad_start_ffn")
    du2, du2b, d_ln2_g, d_ln2_b = _ln_bwd(h1, xa, dh2, ln2_g + ffn_started[4][0, 0], ln2_b, ALPHA, "ln2_bwd")
    dox = _mm(du2b, full["w_xo"], trans_b=True, out_dtype=F32, name="mm_d_ox")
    dw_xo = _mm_tn(oxb, du2b, name="mm_dw_xo")
    dqx, dkx, dvx = _xattn_bwd(qx, kx, vx, ox, dox)
    dh1 = _mm(dqx, full["w_xq"], trans_b=True, out_dtype=F32, name="mm_dh1", addends=(du2,), coefs=(ALPHA,))
    dw_xq = _mm_tn(h1b, dqx, name="mm_dw_xq")
    dw_xk = _mm_tn(mem_nb, dkx, name="mm_dw_xk")
    dw_xv = _mm_tn(mem_nb, dvx, name="mm_dw_xv")
    dmem_n = _mm(dkx, full["w_xk"], trans_b=True, out_dtype=F32, name="mm_dmem", more=((dvx, full["w_xv"], True),))
    _, _, d_mem_ln_g, d_mem_ln_b = _ln_bwd(mems, None, dmem_n, mem_ln_g, mem_ln_b, 1.0, "mem_ln_bwd")
    du1, du1b, d_ln1_g, d_ln1_b = _ln_bwd(h0, mix, dh1, ln1_g, ln1_b, ALPHA, "ln1_bwd")
    dmixed = _mm(du1b, full["w_mix_out"], trans_b=True, out_dtype=F32, name="mm_d_mixed")
    dw_mix_out = _mm_tn(mixed, du1b, name="mm_dw_mix_out")
    attn_parts = (("w_mix_out", dw_mix_out), ("w_xq", dw_xq), ("w_xk", dw_xk), ("w_xv", dw_xv), ("w_xo", dw_xo))
    attn_started = start_grad_exchange(attn_parts, "grad_start_attn")
    doa, dob_views, d_g_win, d_g_dil = _mix_norm_bwd(oa, ob_views[0], dmixed, g_win + attn_started[4][0, 0], g_dil)
    dqa, dkva, dsink8 = _banded_bwd(zw, oa, lse_a, doa, tabs[0], attn_sink, kv_heads=_WIN_KV, name="win_attn_bwd",
                                    **_WIN_CFG)
    dq_views, dkv_views = [], []
    for gi in range(3):
        dq_g, dkv_g = _banded_bwd(zg[gi], ob_views[gi], lb_views[gi], dob_views[gi], tabs[gi], None, kv_heads=_DIL_KV,
                                  name=f"dil_attn_bwd{gi}", **_dil_cfg(gi))
        dq_views.append(dq_g)
        dkv_views.append(dkv_g)
    dz = _dz_assemble(dq_views, dkv_views, dqa, dkva)
    dw_in_tz = _mm_tn(dz, h0b, name="mm_dw_in")
    dw_in_t = jnp.concatenate([dw_in_tz[4608:], dw_in_tz[:4608]], axis=0)
    in_parts = (("w_in", dw_in_t),)
    in_started = start_grad_exchange(in_parts, "grad_start_in")
    dh0 = _mm(dz, w_in_t, trans_b=False, out_dtype=F32, name="mm_dh0", addends=(du1,), coefs=(ALPHA,),
              after=in_started[4])
    dx, _, d_ln_in_g, d_ln_in_b = _ln_bwd(xs, None, dh0, row(ln_in_g), row(ln_in_b), 1.0, "ln_in_bwd")

    grads = {}
    small = jnp.concatenate([
        d_ln_in_g, d_ln_in_b, d_ln1_g, d_ln1_b, d_mem_ln_g, d_mem_ln_b, d_ln2_g, d_ln2_b, d_ln3_g, d_ln3_b,
        jnp.concatenate([d_g_win, d_g_dil], axis=1),
        jnp.pad(d_conv_b, ((0, 0), (0, 3072 - D_FF))).reshape(3, 1024),
        jnp.pad(dsink8[0:1, :], ((0, 0), (0, 1024 - 128))),
        jnp.pad(dcw8[0:3], ((0, 0), (0, 3072 - D_FF))).reshape(9, 1024),
    ], axis=0)
    _, ssum = _all_gather_rows(small, name="small_grad_all_reduce", in_vmem=True, sum_rows=True)
    names10 = ["ln_in_g", "ln_in_b", "ln1_g", "ln1_b", "mem_ln_g", "mem_ln_b", "ln2_g", "ln2_b", "ln3_g", "ln3_b"]
    for i, n in enumerate(names10):
        grads[n] = ssum[i].reshape(weights[n].shape)
    grads["g_win"] = ssum[10:11, :512]
    grads["g_dil"] = ssum[10:11, 512:]
    grads["conv_b"] = ssum[11:14].reshape(1, 3072)[:, :D_FF]
    grads["attn_sink"] = ssum[14:15, :8]
    dcw_full = ssum[15:24].reshape(3, 3072)[:, :D_FF]
    grads["conv_w"] = lax.dynamic_slice_in_dim(dcw_full, me_lin * 352, 352, axis=1)[None]

    delta, new_m, new_v = {}, {}, {}
    big = [n for n, _ in PACK_ROWS]
    small_names = [n for n in order if n not in big]

    def pack_small(src):
        rows_ = []
        for n in small_names:
            flat = src[n].reshape(1, -1)
            width = -(-flat.shape[1] // 1024) * 1024
            rows_.append(jnp.pad(flat, ((0, 0), (0, width - flat.shape[1]))).reshape(-1, 1024))
        packed_ = jnp.concatenate(rows_, axis=0)
        return _pad_rows(packed_, -(-packed_.shape[0] // 8) * 8)

    d_s, m_s, v_s = _adamw(pack_small(weights), pack_small(grads), pack_small(mom_m), pack_small(mom_v), "adamw_small")
    r0 = 0
    for n in small_names:
        size = weights[n].size
        nrow = -(-size // 1024)
        for dst, src in ((delta, d_s), (new_m, m_s), (new_v, v_s)):
            dst[n] = src[r0:r0 + nrow].reshape(-1)[:size].reshape(weights[n].shape)
        r0 += nrow

    after = d_s
    for parts, started, tag in ((ffn_parts, ffn_started, "ffn"), (attn_parts, attn_started, "attn"),
                                (in_parts, in_started, "in")):
        gp_thru, land = _exchange_wait(started, after, gather=False, name=f"grad_wait_{tag}")
        own = lax.dynamic_index_in_dim(gp_thru, me_lin, axis=0, keepdims=False)
        gsum = _sum_parts(own, land, f"grad_sum_{tag}")
        off = 0
        for n, _ in parts:
            blk = gsum[off:off + rows_of[n]]
            off += rows_of[n]
            grads[n] = (blk.T if n in ("w_in", "w_gate", "w_up") else blk)[None]
            shp = weights[n].shape
            d_, m_, v_ = _adamw(weights[n].reshape(shp[1:]), grads[n].reshape(shp[1:]), mom_m[n].reshape(shp[1:]),
                                mom_v[n].reshape(shp[1:]), f"adamw_{n}")
            delta[n], new_m[n], new_v[n] = d_.reshape(shp), m_.reshape(shp), v_.reshape(shp)
            after = d_

    loss = lax.psum(loss_local[0, 0], MESH_AXES)
    return (loss, dx[None], *[grads[n] for n in order], *[delta[n] for n in order], *[new_m[n] for n in order],
            *[new_v[n] for n in order])
```

```python
import functools
import math

import jax
import jax.numpy as jnp
from jax import lax
from jax.experimental import pallas as pl
from jax.experimental.pallas import tpu as pltpu

F32 = jnp.float32
BF16 = jnp.bfloat16
SDS = jax.ShapeDtypeStruct
_PALLAS_CALL = pl.pallas_call

D_MODEL = 1024
HEAD_DIM = 64
WIN_HALF = 128
DIL_PAIRS = ((128, 1), (512, 4), (2048, 16))
DIL_SIDE = 64
ROT_DIM = 16
ROPE_THETA = 500000.0
MEM_LEN = 256
X_HEADS = 4
X_HEAD_DIM = 256
D_FF = 2816
IN_WIDTH = 5376
Z_QB, Z_KB, Z_VB, Z_QA, Z_KA, Z_VA = 0, 1536, 3072, 4608, 5120, 5248
ALPHA = (2.0) ** 0.25
LN_EPS = 1e-5
NEG_INF = -1e30
ADAM_LR, ADAM_B1, ADAM_B2, ADAM_EPS, ADAM_WD, ADAM_STEP = 0.001, 0.9, 0.999, 1e-08, 0.01, 10
N_DEV = 8
MESH_AXES = ("x", "y", "c")
VMEM_LIMIT_BYTES = 52 * 1024 * 1024
ATTN_TQ = 256
TABW = 384

PACK_ROWS = (("w_in", 672), ("w_gate", 352), ("w_up", 352), ("w_mix_out", 128), ("w_xq", 128), ("w_xk", 128),
             ("w_xv", 128), ("w_xo", 128), ("w_down", 352))
SMALL_ROWS = 24


def _pick(n, cands):
    for c in cands:
        if n % c == 0:
            return c
    return n


def _pcall(body, *, name, out_shape, grid=None, in_specs=None, out_specs=None, scratch_shapes=(), dims=None,
           aliases=None):
    kw = {}
    if grid is not None:
        kw["grid"] = grid
    if in_specs is not None:
        kw["in_specs"] = in_specs
    if out_specs is not None:
        kw["out_specs"] = out_specs
    if aliases:
        kw["input_output_aliases"] = aliases
    return _PALLAS_CALL(
        body, name=name, out_shape=out_shape, scratch_shapes=list(scratch_shapes),
        compiler_params=pltpu.CompilerParams(dimension_semantics=dims, vmem_limit_bytes=VMEM_LIMIT_BYTES), **kw)


MM_VMEM_BUDGET = 40 * 1024 * 1024


def _mm(a, b, *, trans_b, out_dtype, name, addends=(), coefs=(), after=None, more=()):
    pairs = ((a, b, trans_b),) + tuple(more)
    m = a.shape[0]
    n = b.shape[0] if trans_b else b.shape[1]
    n_add = len(addends)
    extra = [] if after is None else [after]
    out_bytes = jnp.dtype(out_dtype).itemsize

    def vmem(tm, tn):
        tot = tm * tn * (out_bytes + 4 * n_add)
        for pa, pb, _ in pairs:
            tot += tm * pa.shape[1] * pa.dtype.itemsize + pa.shape[1] * tn * pb.dtype.itemsize
        return 2 * tot

    tm, tn = next(((cm, cn) for cn in (n, 1408, 1024, 512, 256, 128) if n % cn == 0
                   for cm in (1024, 512, 256, 128) if m % cm == 0 and vmem(cm, cn) <= MM_VMEM_BUDGET))
    n_pairs = len(pairs)

    def body(*refs):
        o_ref = refs[2 * n_pairs + n_add + len(extra)]
        acc = None
        for p, (_, _, tb) in enumerate(pairs):
            dn = _NT if tb else _NN
            part = lax.dot_general(refs[2 * p][...].astype(BF16), refs[2 * p + 1][...].astype(BF16), dn,
                                   preferred_element_type=F32)
            acc = part if acc is None else acc + part
        for r_ref, c in zip(refs[2 * n_pairs:2 * n_pairs + n_add], coefs):
            acc = acc + (r_ref[...] if c == 1.0 else c * r_ref[...])
        o_ref[...] = acc.astype(out_dtype)

    in_specs, args = [], []
    for pa, pb, tb in pairs:
        k = pa.shape[1]
        in_specs.append(pl.BlockSpec((tm, k), lambda j, i: (i, 0)))
        in_specs.append(pl.BlockSpec((tn, k), lambda j, i: (j, 0)) if tb else pl.BlockSpec((k, tn), lambda j, i: (0, j)))
        args += [pa, pb]
    in_specs += [pl.BlockSpec((tm, tn), lambda j, i: (i, j)) for _ in addends]
    in_specs += [pl.BlockSpec((8, 128), lambda j, i: (0, 0)) for _ in extra]
    return _pcall(body, name=name, out_shape=SDS((m, n), out_dtype), grid=(n // tn, m // tm), in_specs=in_specs,
                  out_specs=pl.BlockSpec((tm, tn), lambda j, i: (i, j)),
                  dims=("parallel", "parallel"))(*args, *addends, *extra)


def _mm_tn(a, b, *, name):
    s, m = a.shape
    n = b.shape[1]
    tm = _pick(m, (768, 1408, 1024, 512, 256, 128))
    tk = _pick(s, (1024, 512, 256))
    nk = s // tk

    def body(a_ref, b_ref, o_ref, acc_ref):
        kk = pl.program_id(1)

        @pl.when(kk == 0)
        def _():
            acc_ref[...] = jnp.zeros_like(acc_ref)

        acc_ref[...] += lax.dot_general(a_ref[...].astype(BF16), b_ref[...].astype(BF16), (((0,), (0,)), ((), ())),
                                        preferred_element_type=F32)

        @pl.when(kk == nk - 1)
        def _():
            o_ref[...] = acc_ref[...]

    return _pcall(body, name=name, out_shape=SDS((m, n), F32), grid=(m // tm, nk),
                  in_specs=[pl.BlockSpec((tk, tm), lambda i, kk: (kk, i)), pl.BlockSpec((tk, n), lambda i, kk: (kk, 0))],
                  out_specs=pl.BlockSpec((tm, n), lambda i, kk: (i, 0)), scratch_shapes=[pltpu.VMEM((tm, n), F32)],
                  dims=("parallel", "arbitrary"))(a, b)


def _rope_lane_consts():
    lane = jnp.arange(128)
    j = lane % HEAD_DIM
    inv_freq = ROPE_THETA ** (-jnp.arange(0, ROT_DIM, 2, dtype=F32) / ROT_DIM)
    freq = jnp.where(j < ROT_DIM, inv_freq[j % (ROT_DIM // 2)], 0.0).astype(F32)
    lo = (j < ROT_DIM // 2).astype(F32)
    hi = ((j >= ROT_DIM // 2) & (j < ROT_DIM)).astype(F32)
    return jnp.stack([freq, lo, hi] + [jnp.zeros((128,), F32)] * 5)


def _to_classes(x, scr, d):
    if d == 1:
        return [x]
    scr[...] = x
    return [scr[pl.ds(c, x.shape[0] // d, stride=d), :] for c in range(d)]


def _from_classes(parts, scr):
    d = len(parts)
    if d == 1:
        return parts[0]
    for c, part in enumerate(parts):
        scr[pl.ds(c, part.shape[0], stride=d), :] = part
    return scr[...]


DILATIONS = tuple(d for _, d in DIL_PAIRS)


def _rope_tables(posf):
    s = posf.shape[0]
    tm = _pick(s, (1024, 512))

    def body(p_ref, c_ref, *rest):
        o_refs, scr = rest[:-1], rest[-1]
        ang = p_ref[...] * c_ref[0:1, :]
        lo = c_ref[1:2, :]
        hi = c_ref[2:3, :]
        cs = jnp.cos(ang)
        sn = jnp.sin(ang)
        for q, t in enumerate((jnp.where(lo + hi > 0.0, cs, 1.0), -sn * lo, sn * hi)):
            for o_ref, d in zip(o_refs, DILATIONS):
                for c, part in enumerate(_to_classes(t, scr, d)):
                    o_ref[:, c * TABW + q * 128:c * TABW + (q + 1) * 128] = part

    return _pcall(body, name="rope_tables", out_shape=tuple(SDS((s // d, d * TABW), F32) for d in DILATIONS),
                  grid=(s // tm,),
                  in_specs=[pl.BlockSpec((tm, 1), lambda i: (i, 0)), pl.BlockSpec((8, 128), lambda i: (0, 0))],
                  out_specs=tuple(pl.BlockSpec((tm // d, d * TABW), lambda i: (i, 0)) for d in DILATIONS),
                  scratch_shapes=[pltpu.VMEM((tm, 128), F32)], dims=("parallel",))(posf, _rope_lane_consts())


def _rope_apply(x, tab, sign):
    w = x.shape[1]
    rep = w // 128
    c = jnp.tile(tab[:, 0:128], (1, rep)) if rep > 1 else tab[:, 0:128]
    a = jnp.tile(tab[:, 128:256], (1, rep)) if rep > 1 else tab[:, 128:256]
    b = jnp.tile(tab[:, 256:384], (1, rep)) if rep > 1 else tab[:, 256:384]
    up = pltpu.roll(x, w - 8, 1)
    dn = pltpu.roll(x, 8, 1)
    if sign > 0:
        return x * c + up * a + dn * b
    return x * c - up * a - dn * b


def _proj_rope(h0b, w_t, tab):
    s = h0b.shape[0]
    tm = _pick(s, (512,))
    tn = 256

    def body(a_ref, w_ref, t_ref, zw_ref, z0_ref, z1_ref, z2_ref, scr):
        z_refs = (z0_ref, z1_ref, z2_ref)
        a = a_ref[...]
        tabv = t_ref[...]
        for c0 in range(0, IN_WIDTH, tn):
            z = lax.dot_general(a, w_ref[c0:c0 + tn, :], _NT, preferred_element_type=F32)
            for g0 in range(c0, c0 + tn, 128):
                zg = z[:, g0 - c0:g0 - c0 + 128]
                if g0 < Z_VB or Z_QA <= g0 < Z_VA:
                    zg = _rope_apply(zg, tabv, 1)
                if g0 >= Z_QA:
                    zw_ref[:, g0 - Z_QA:g0 - Z_QA + 128] = zg.astype(BF16)
                    continue
                kind, within = divmod(g0, 1536)
                grp, off = divmod(within, 512)
                col = kind * 512 + off
                for c, part in enumerate(_to_classes(zg, scr, DILATIONS[grp])):
                    z_refs[grp][:, c * 1536 + col:c * 1536 + col + 128] = part.astype(BF16)

    return _pcall(body, name="proj_rope",
                  out_shape=(SDS((s, 768), BF16),) + tuple(SDS((s // d, d * 1536), BF16) for d in DILATIONS),
                  grid=(s // tm,),
                  in_specs=[pl.BlockSpec((tm, D_MODEL), lambda i: (i, 0)), pl.BlockSpec((IN_WIDTH, D_MODEL), lambda i: (0, 0)),
                            pl.BlockSpec((tm, TABW), lambda i: (i, 0))],
                  out_specs=(pl.BlockSpec((tm, 768), lambda i: (i, 0)),)
                  + tuple(pl.BlockSpec((tm // d, d * 1536), lambda i: (i, 0)) for d in DILATIONS),
                  scratch_shapes=[pltpu.VMEM((tm, 128), F32)], dims=("parallel",))(h0b, w_t, tab)


def _band_specs(sd, blk, tq, width, per_tok, cb):
    r = tq // blk
    nbk = sd // blk
    prev = pl.BlockSpec((blk, width), lambda c, j: (jnp.maximum(j * r - 1, 0), c * per_tok + cb))
    cur = pl.BlockSpec((tq, width), lambda c, j: (j, c * per_tok + cb))
    nxt = pl.BlockSpec((blk, width), lambda c, j: (jnp.minimum((j + 1) * r, nbk - 1), c * per_tok + cb))
    return [prev, cur, nxt]


def _band_bias(j, blk, tq, sd):
    shape = (tq, tq + 2 * blk)
    qpos = j * tq + lax.broadcasted_iota(jnp.int32, shape, 0)
    kpos = j * tq - blk + lax.broadcasted_iota(jnp.int32, shape, 1)
    ok = (jnp.abs(qpos - kpos) <= blk) & (kpos >= 0) & (kpos < sd)
    return jnp.where(ok, 0.0, NEG_INF)


_NT = (((1,), (1,)), ((), ()))
_NN = (((1,), (0,)), ((), ()))
_TN = (((0,), (0,)), ((), ()))


def _banded_fwd(zv, sink, *, d, blk, tq, ptw, qw, kw, qcb, kcb, vcb, heads, name):
    sd = zv.shape[0]
    tq = min(tq, sd)
    has_sink = sink is not None
    scale = HEAD_DIM ** -0.5

    def body(q_ref, kp, kc, kn, vp, vc, vn, *rest):
        if has_sink:
            sink_ref, o_ref, lse_ref = rest
        else:
            o_ref, lse_ref = rest
        j = pl.program_id(1)
        q = q_ref[...] * scale
        k = jnp.concatenate([kp[...], kc[...], kn[...]], axis=0)
        v = jnp.concatenate([vp[...], vc[...], vn[...]], axis=0)
        bias = _band_bias(j, blk, tq, sd)
        outs, lses = [], []
        for ql, kl, vl, si in heads:
            sc = lax.dot_general(q[:, ql:ql + HEAD_DIM], k[:, kl:kl + HEAD_DIM], _NT, preferred_element_type=F32) + bias
            m = jnp.max(sc, axis=-1, keepdims=True)
            if has_sink:
                m = jnp.maximum(m, sink_ref[0, si])
            p = jnp.exp(sc - m)
            den = jnp.sum(p, axis=-1, keepdims=True)
            if has_sink:
                den = den + jnp.exp(sink_ref[0, si] - m)
            o = lax.dot_general(p.astype(BF16), v[:, vl:vl + HEAD_DIM], _NN, preferred_element_type=F32) / den
            outs.append(o)
            lses.append(jnp.broadcast_to(m + jnp.log(den), (tq, HEAD_DIM)))
        o_ref[...] = jnp.concatenate(outs, axis=1)
        lse_ref[...] = jnp.concatenate(lses, axis=1)

    in_specs = ([pl.BlockSpec((tq, qw), lambda c, j: (j, c * (ptw // qw) + qcb))]
                + _band_specs(sd, blk, tq, kw, ptw // kw, kcb) + _band_specs(sd, blk, tq, kw, ptw // kw, vcb))
    args = [zv] * 7
    if has_sink:
        in_specs.append(pl.BlockSpec(memory_space=pltpu.SMEM))
        args.append(sink)
    o_spec = pl.BlockSpec((tq, qw), lambda c, j: (j, c))
    return _pcall(body, name=name, out_shape=(SDS((sd, d * qw), F32), SDS((sd, d * qw), F32)), grid=(d, sd // tq),
                  in_specs=in_specs, out_specs=(o_spec, o_spec), dims=("parallel", "parallel"))(*args)


def _banded_bwd(zv, ov, lv, dov, tv, sink, *, d, blk, tq, ptw, qw, kw, qcb, kcb, vcb, heads, kv_heads, name):
    sd = zv.shape[0]
    tq = min(tq, sd)
    nt = sd // tq
    r = tq // blk
    nbk = sd // blk
    has_sink = sink is not None
    scale = HEAD_DIM ** -0.5
    kvw = HEAD_DIM * len(kv_heads)

    def add_rows(x, y, last):
        if tq == blk:
            return x + y
        if last:
            return jnp.concatenate([x[:tq - blk], x[tq - blk:] + y], axis=0)
        return jnp.concatenate([x[:blk] + y, x[blk:]], axis=0)

    def body(q_ref, kp, kc, kn, vp, vc, vn, o_ref, l_ref, do_ref, t_ref, tlag_ref, *rest):
        if has_sink:
            sink_ref, dq_ref, dkv_ref, dsink_ref, acck, accv, nxtk, nxtv = rest
        else:
            dq_ref, dkv_ref, acck, accv, nxtk, nxtv = rest
        j = pl.program_id(1)

        @pl.when(j == 0)
        def _():
            nxtk[...] = jnp.zeros_like(nxtk)
            nxtv[...] = jnp.zeros_like(nxtv)

        if has_sink:
            @pl.when((pl.program_id(0) == 0) & (j == 0))
            def _():
                dsink_ref[...] = jnp.zeros_like(dsink_ref)

        def emit(dk_rows, dv_rows):
            dkv_ref[...] = jnp.concatenate([_rope_apply(dk_rows, tlag_ref[...], -1), dv_rows], axis=1).astype(BF16)

        @pl.when(j < nt)
        def _():
            q = q_ref[...] * scale
            k3 = jnp.concatenate([kp[...], kc[...], kn[...]], axis=0)
            v3 = jnp.concatenate([vp[...], vc[...], vn[...]], axis=0)
            o_t, l_t, do_t = o_ref[...], l_ref[...], do_ref[...]
            bias = _band_bias(j, blk, tq, sd)
            dqs = []
            dks = [None] * len(kv_heads)
            dvs = [None] * len(kv_heads)
            dsink_row = jnp.zeros((1, 128), F32)
            lane = lax.broadcasted_iota(jnp.int32, (1, 128), 1)
            for ql, kl, vl, si in heads:
                kvi = kv_heads.index((kl, vl))
                qh = q[:, ql:ql + HEAD_DIM]
                kh3 = k3[:, kl:kl + HEAD_DIM]
                vh3 = v3[:, vl:vl + HEAD_DIM]
                doh = do_t[:, ql:ql + HEAD_DIM]
                delta = jnp.sum(doh * o_t[:, ql:ql + HEAD_DIM], axis=-1, keepdims=True)
                lse = l_t[:, ql:ql + 1]
                dob = doh.astype(BF16)
                sc = lax.dot_general(qh, kh3, _NT, preferred_element_type=F32) + bias
                p = jnp.exp(sc - lse)
                dp = lax.dot_general(dob, vh3, _NT, preferred_element_type=F32)
                dsb = (p * (dp - delta)).astype(BF16)
                dqs.append(lax.dot_general(dsb, kh3, _NN, preferred_element_type=F32) * scale)
                dk = lax.dot_general(dsb, qh, _TN, preferred_element_type=F32)
                dv = lax.dot_general(p.astype(BF16), dob, _TN, preferred_element_type=F32)
                dks[kvi] = dk if dks[kvi] is None else dks[kvi] + dk
                dvs[kvi] = dv if dvs[kvi] is None else dvs[kvi] + dv
                if has_sink:
                    psink = jnp.exp(sink_ref[0, si] - lse)
                    dsink_row = dsink_row + jnp.where(lane == si, -jnp.sum(psink * delta), 0.0)
            dq_ref[...] = _rope_apply(jnp.concatenate(dqs, axis=1), t_ref[...], -1).astype(BF16)
            wk = jnp.concatenate(dks, axis=1) if len(dks) > 1 else dks[0]
            wv = jnp.concatenate(dvs, axis=1) if len(dvs) > 1 else dvs[0]
            if has_sink:
                dsink_ref[0:1, :] += dsink_row

            @pl.when(j > 0)
            def _():
                emit(add_rows(acck[...], wk[:blk], True), add_rows(accv[...], wv[:blk], True))

            acck[...] = add_rows(wk[blk:blk + tq], nxtk[...], False)
            accv[...] = add_rows(wv[blk:blk + tq], nxtv[...], False)
            nxtk[...] = wk[blk + tq:]
            nxtv[...] = wv[blk + tq:]

        @pl.when(j == nt)
        def _():
            emit(acck[...], accv[...])

    def tile(width, per_tok, cb):
        return pl.BlockSpec((tq, width), lambda c, j: (jnp.minimum(j, nt - 1), c * per_tok + cb))

    def halos(width, per_tok, cb):
        before = pl.BlockSpec((blk, width), lambda c, j: (jnp.maximum(jnp.minimum(j, nt - 1) * r - 1, 0), c * per_tok + cb))
        after = pl.BlockSpec((blk, width),
                             lambda c, j: (jnp.minimum((jnp.minimum(j, nt - 1) + 1) * r, nbk - 1), c * per_tok + cb))
        return [before, tile(width, per_tok, cb), after]

    def lagged(width):
        return pl.BlockSpec((tq, width), lambda c, j: (jnp.maximum(j - 1, 0), c))

    in_specs = ([tile(qw, ptw // qw, qcb)] + halos(kw, ptw // kw, kcb) + halos(kw, ptw // kw, vcb)
                + [tile(qw, 1, 0)] * 3 + [tile(TABW, 1, 0), lagged(TABW)])
    args = [zv] * 7 + [ov, lv, dov, tv, tv]
    out_shape = [SDS((sd, d * qw), BF16), SDS((sd, d * 2 * kvw), BF16)]
    out_specs = [tile(qw, 1, 0), lagged(2 * kvw)]
    if has_sink:
        in_specs.append(pl.BlockSpec(memory_space=pltpu.SMEM))
        args.append(sink)
        out_shape.append(SDS((8, 128), F32))
        out_specs.append(pl.BlockSpec((8, 128), lambda c, j: (0, 0)))
    scratch = [pltpu.VMEM((tq, kvw), F32), pltpu.VMEM((tq, kvw), F32), pltpu.VMEM((blk, kvw), F32),
               pltpu.VMEM((blk, kvw), F32)]
    return _pcall(body, name=name, out_shape=tuple(out_shape), grid=(d, nt + 1), in_specs=in_specs,
                  out_specs=tuple(out_specs), scratch_shapes=scratch, dims=("arbitrary", "arbitrary"))(*args)


_WIN_HEADS = tuple((h * HEAD_DIM, (h // 4) * HEAD_DIM, 128 + (h // 4) * HEAD_DIM, h) for h in range(8))
_WIN_KV = ((0, 128), (64, 192))
_WIN_CFG = dict(d=1, blk=WIN_HALF, tq=ATTN_TQ, ptw=768, qw=512, kw=256, qcb=0, kcb=2, vcb=2, heads=_WIN_HEADS)
_DIL_HEADS = tuple((h * HEAD_DIM, h * HEAD_DIM, h * HEAD_DIM, h) for h in range(8))
_DIL_KV = tuple((h * HEAD_DIM, h * HEAD_DIM) for h in range(8))


def _dil_cfg(gi):
    return dict(d=DILATIONS[gi], blk=DIL_SIDE, tq=ATTN_TQ, ptw=1536, qw=512, kw=512, qcb=0, kcb=1, vcb=2,
                heads=_DIL_HEADS)


def _view_specs(tm, width):
    return tuple(pl.BlockSpec((tm // d, d * width), lambda i: (i, 0)) for d in DILATIONS)


def _mix_norm_fwd(oa, og_views, lg_views, g_win, g_dil):
    s = oa.shape[0]
    tm = _pick(s, (512,))

    def body(oa_ref, o0, o1, o2, l0, l1, l2, gw_ref, gd_ref, mixed_ref, ob0, ob1, ob2, lb0, lb1, lb2, scr, ob_s):
        o_refs, l_refs, ob_refs, lb_refs = (o0, o1, o2), (l0, l1, l2), (ob0, ob1, ob2), (lb0, lb1, lb2)
        ssq = jnp.zeros((tm, 1), F32)
        for q in range(4):
            os_, ls_ = [], []
            for g, d in enumerate(DILATIONS):
                cols = [slice(c * 512 + q * 128, c * 512 + (q + 1) * 128) for c in range(d)]
                os_.append(_from_classes([o_refs[g][:, cs] for cs in cols], scr))
                ls_.append(_from_classes([l_refs[g][:, cs] for cs in cols], scr))
            mx = jnp.maximum(jnp.maximum(ls_[0], ls_[1]), ls_[2])
            es = [jnp.exp(l - mx) for l in ls_]
            den = es[0] + es[1] + es[2]
            ob = (es[0] / den) * os_[0] + (es[1] / den) * os_[1] + (es[2] / den) * os_[2]
            lb = mx + jnp.log(den)
            ob_s[:, q * 128:(q + 1) * 128] = ob
            ssq = ssq + jnp.sum(ob * ob, axis=-1, keepdims=True)
            for g, d in enumerate(DILATIONS):
                for val, refs in ((ob, ob_refs), (lb, lb_refs)):
                    for c, part in enumerate(_to_classes(val, scr, d)):
                        refs[g][:, c * 512 + q * 128:c * 512 + (q + 1) * 128] = part
        a = oa_ref[...]
        ra = lax.rsqrt(jnp.mean(a * a, axis=-1, keepdims=True) + LN_EPS)
        rb = lax.rsqrt(ssq * (1.0 / 512) + LN_EPS)
        mixed_ref[...] = jnp.concatenate([a * ra * gw_ref[...], ob_s[...] * rb * gd_ref[...]], axis=1).astype(BF16)

    row = pl.BlockSpec((tm, 512), lambda i: (i, 0))
    vec = pl.BlockSpec((1, 512), lambda i: (0, 0))
    views = _view_specs(tm, 512)
    view_shapes = tuple(SDS((s // d, d * 512), F32) for d in DILATIONS)
    res = _pcall(body, name="mix_norm_fwd", out_shape=(SDS((s, 1024), BF16),) + view_shapes * 2, grid=(s // tm,),
                 in_specs=[row, *views, *views, vec, vec],
                 out_specs=(pl.BlockSpec((tm, 1024), lambda i: (i, 0)),) + views * 2,
                 scratch_shapes=[pltpu.VMEM((tm, 128), F32), pltpu.VMEM((tm, 512), F32)],
                 dims=("parallel",))(oa, *og_views, *lg_views, g_win, g_dil)
    return res[0], res[1:4], res[4:7]


def _mix_norm_bwd(oa, ob, dmixed, g_win, g_dil):
    s = oa.shape[0]
    tm = _pick(s, (512,))
    nt = s // tm

    def body(oa_ref, ob_ref, dm_ref, gw_ref, gd_ref, doa_ref, db0, db1, db2, dgw_ref, dgd_ref, acc_w, acc_d, scr):
        i = pl.program_id(0)

        @pl.when(i == 0)
        def _():
            acc_w[...] = jnp.zeros_like(acc_w)
            acc_d[...] = jnp.zeros_like(acc_d)

        dm = dm_ref[...]
        dxs = []
        for x_ref, g_ref, dy, acc in ((oa_ref, gw_ref, dm[:, :512], acc_w), (ob_ref, gd_ref, dm[:, 512:], acc_d)):
            x = x_ref[...]
            r = lax.rsqrt(jnp.mean(x * x, axis=-1, keepdims=True) + LN_EPS)
            dyg = dy * g_ref[...]
            dxs.append(r * dyg - x * (r * r * r) * jnp.mean(dyg * x, axis=-1, keepdims=True))
            acc[...] += jnp.sum((dy * x * r).reshape(tm // 8, 8, 512), axis=0)
        doa_ref[...] = dxs[0]
        for q in range(4):
            dq = dxs[1][:, q * 128:(q + 1) * 128]
            for db_ref, d in zip((db0, db1, db2), DILATIONS):
                for c, part in enumerate(_to_classes(dq, scr, d)):
                    db_ref[:, c * 512 + q * 128:c * 512 + (q + 1) * 128] = part

        @pl.when(i == nt - 1)
        def _():
            dgw_ref[...] = jnp.sum(acc_w[...], axis=0, keepdims=True)
            dgd_ref[...] = jnp.sum(acc_d[...], axis=0, keepdims=True)

    row = pl.BlockSpec((tm, 512), lambda i: (i, 0))
    vec = pl.BlockSpec((1, 512), lambda i: (0, 0))
    views = _view_specs(tm, 512)
    view_shapes = tuple(SDS((s // d, d * 512), F32) for d in DILATIONS)
    res = _pcall(body, name="mix_norm_bwd",
                 out_shape=(SDS((s, 512), F32),) + view_shapes + (SDS((1, 512), F32), SDS((1, 512), F32)),
                 grid=(nt,), in_specs=[row, row, pl.BlockSpec((tm, 1024), lambda i: (i, 0)), vec, vec],
                 out_specs=(row,) + views + (vec, vec),
                 scratch_shapes=[pltpu.VMEM((8, 512), F32), pltpu.VMEM((8, 512), F32), pltpu.VMEM((tm, 128), F32)],
                 dims=("arbitrary",))(oa, ob, dmixed, g_win, g_dil)
    return res[0], res[1:4], res[4], res[5]


def _dz_assemble(dq_views, dkv_views, dqa, dkva):
    s = dqa.shape[0]
    tm = _pick(s, (512,))

    def body(q0, q1, q2, kv0, kv1, kv2, qa_ref, kva_ref, o_ref, scr):
        for g, d in enumerate(DILATIONS):
            for kind, (ref, width, base) in enumerate((((q0, q1, q2)[g], 512, 0), ((kv0, kv1, kv2)[g], 1024, 0),
                                                       ((kv0, kv1, kv2)[g], 1024, 512))):
                for q in range(4):
                    src = base + q * 128
                    dst = kind * 1536 + g * 512 + q * 128
                    if d == 1:
                        o_ref[:, dst:dst + 128] = ref[:, src:src + 128]
                    else:
                        parts = [ref[:, c * width + src:c * width + src + 128].astype(F32) for c in range(d)]
                        o_ref[:, dst:dst + 128] = _from_classes(parts, scr).astype(BF16)
        o_ref[:, Z_QA:Z_QA + 512] = qa_ref[...]
        o_ref[:, Z_KA:Z_KA + 256] = kva_ref[...]

    return _pcall(body, name="dz_assemble", out_shape=SDS((s, IN_WIDTH), BF16), grid=(s // tm,),
                  in_specs=[*_view_specs(tm, 512), *_view_specs(tm, 1024), pl.BlockSpec((tm, 512), lambda i: (i, 0)),
                            pl.BlockSpec((tm, 256), lambda i: (i, 0))],
                  out_specs=pl.BlockSpec((tm, IN_WIDTH), lambda i: (i, 0)),
                  scratch_shapes=[pltpu.VMEM((tm, 128), F32)], dims=("parallel",))(*dq_views, *dkv_views, dqa, dkva)


def _ln_fwd(a, r, g, b, ca, name):
    s = a.shape[0]
    tm = _pick(s, (512, 256))
    has_r = r is not None

    def body(*refs):
        a_ref = refs[0]
        r_ref = refs[1] if has_r else None
        g_ref, b_ref, o_ref, ob_ref = refs[1 + has_r:]
        u = a_ref[...] if ca == 1.0 else ca * a_ref[...]
        if has_r:
            u = u + r_ref[...]
        mu = jnp.mean(u, axis=-1, keepdims=True)
        xc = u - mu
        var = jnp.mean(xc * xc, axis=-1, keepdims=True)
        y = xc * lax.rsqrt(var + LN_EPS) * g_ref[...] + b_ref[...]
        o_ref[...] = y
        ob_ref[...] = y.astype(BF16)

    row = pl.BlockSpec((tm, D_MODEL), lambda i: (i, 0))
    vec = pl.BlockSpec((1, D_MODEL), lambda i: (0, 0))
    args = [a] + ([r] if has_r else []) + [g, b]
    return _pcall(body, name=name, out_shape=(SDS((s, D_MODEL), F32), SDS((s, D_MODEL), BF16)), grid=(s // tm,),
                  in_specs=[row] * (1 + has_r) + [vec, vec], out_specs=(row, row), dims=("parallel",))(*args)


def _ln_bwd(a, r, dy, g, b, ca, name, loss_mode=False):
    s = a.shape[0]
    tm = _pick(s, (512, 256))
    nt = s // tm
    has_r = r is not None

    def body(*refs):
        a_ref = refs[0]
        r_ref = refs[1] if has_r else None
        dy_ref, g_ref, b_ref = refs[1 + has_r:4 + has_r]
        outs = refs[4 + has_r:]
        if loss_mode:
            du_ref, dub_ref, dg_ref, db_ref, loss_ref, acc_g, acc_b, acc_l = outs
        else:
            du_ref, dub_ref, dg_ref, db_ref, acc_g, acc_b = outs
        i = pl.program_id(0)

        @pl.when(i == 0)
        def _():
            acc_g[...] = jnp.zeros_like(acc_g)
            acc_b[...] = jnp.zeros_like(acc_b)
            if loss_mode:
                acc_l[...] = jnp.zeros_like(acc_l)

        u = a_ref[...] if ca == 1.0 else ca * a_ref[...]
        if has_r:
            u = u + r_ref[...]
        mu = jnp.mean(u, axis=-1, keepdims=True)
        xc = u - mu
        var = jnp.mean(xc * xc, axis=-1, keepdims=True)
        rstd = lax.rsqrt(var + LN_EPS)
        xhat = xc * rstd
        gv = g_ref[...]
        if loss_mode:
            err = (xhat * gv + b_ref[...]) - dy_ref[...]
            acc_l[...] += jnp.sum((err * err).reshape(tm // 8, 8, D_MODEL), axis=0)
            dyv = err * (1.0 / D_MODEL)
        else:
            dyv = dy_ref[...]
        dxh = dyv * gv
        du = rstd * (dxh - jnp.mean(dxh, axis=-1, keepdims=True) - xhat * jnp.mean(dxh * xhat, axis=-1, keepdims=True))
        du_ref[...] = du
        dub_ref[...] = du.astype(BF16)
        acc_g[...] += jnp.sum((dyv * xhat).reshape(tm // 8, 8, D_MODEL), axis=0)
        acc_b[...] += jnp.sum(dyv.reshape(tm // 8, 8, D_MODEL), axis=0)

        @pl.when(i == nt - 1)
        def _():
            dg_ref[...] = jnp.sum(acc_g[...], axis=0, keepdims=True)
            db_ref[...] = jnp.sum(acc_b[...], axis=0, keepdims=True)
            if loss_mode:
                tot = jnp.sum(jnp.sum(acc_l[...], axis=0, keepdims=True), axis=1, keepdims=True)
                loss_ref[...] = tot * (0.5 / D_MODEL)

    row = pl.BlockSpec((tm, D_MODEL), lambda i: (i, 0))
    vec = pl.BlockSpec((1, D_MODEL), lambda i: (0, 0))
    out_shape = [SDS((s, D_MODEL), F32), SDS((s, D_MODEL), BF16), SDS((1, D_MODEL), F32), SDS((1, D_MODEL), F32)]
    out_specs = [row, row, vec, vec]
    scratch = [pltpu.VMEM((8, D_MODEL), F32), pltpu.VMEM((8, D_MODEL), F32)]
    if loss_mode:
        out_shape.append(SDS((1, 1), F32))
        out_specs.append(pl.BlockSpec((1, 1), lambda i: (0, 0)))
        scratch.append(pltpu.VMEM((8, D_MODEL), F32))
    args = [a] + ([r] if has_r else []) + [dy, g, b]
    return _pcall(body, name=name, out_shape=tuple(out_shape), grid=(nt,), in_specs=[row] * (2 + has_r) + [vec, vec],
                  out_specs=tuple(out_specs), scratch_shapes=scratch, dims=("arbitrary",))(*args)


def _xattn_fwd(q, k, v):
    s = q.shape[0]
    tq = _pick(s, (512,))
    scale = X_HEAD_DIM ** -0.5

    def body(q_ref, k_ref, v_ref, o_ref, ob_ref):
        qv, kv, vv = q_ref[...], k_ref[...], v_ref[...]
        outs = []
        for h in range(X_HEADS):
            sl = slice(h * X_HEAD_DIM, (h + 1) * X_HEAD_DIM)
            sc = lax.dot_general(qv[:, sl], kv[:, sl], _NT, preferred_element_type=F32) * scale
            e = jnp.exp(sc - jnp.max(sc, axis=-1, keepdims=True))
            p = e / jnp.sum(e, axis=-1, keepdims=True)
            outs.append(lax.dot_general(p.astype(BF16), vv[:, sl], _NN, preferred_element_type=F32))
        o = jnp.concatenate(outs, axis=1)
        o_ref[...] = o
        ob_ref[...] = o.astype(BF16)

    row = pl.BlockSpec((tq, D_MODEL), lambda i: (i, 0))
    full = pl.BlockSpec((MEM_LEN, D_MODEL), lambda i: (0, 0))
    return _pcall(body, name="xattn_fwd", out_shape=(SDS((s, D_MODEL), F32), SDS((s, D_MODEL), BF16)), grid=(s // tq,),
                  in_specs=[row, full, full], out_specs=(row, row), dims=("parallel",))(q, k, v)


def _xattn_bwd(q, k, v, o, do):
    s = q.shape[0]
    tq = _pick(s, (512,))
    scale = X_HEAD_DIM ** -0.5

    def body(q_ref, k_ref, v_ref, o_ref, do_ref, dq_ref, dk_ref, dv_ref):
        i = pl.program_id(0)

        @pl.when(i == 0)
        def _():
            dk_ref[...] = jnp.zeros_like(dk_ref)
            dv_ref[...] = jnp.zeros_like(dv_ref)

        qv, kv, vv, ov, dov = q_ref[...], k_ref[...], v_ref[...], o_ref[...], do_ref[...]
        dqs, dks, dvs = [], [], []
        for h in range(X_HEADS):
            sl = slice(h * X_HEAD_DIM, (h + 1) * X_HEAD_DIM)
            sc = lax.dot_general(qv[:, sl], kv[:, sl], _NT, preferred_element_type=F32) * scale
            e = jnp.exp(sc - jnp.max(sc, axis=-1, keepdims=True))
            p = e / jnp.sum(e, axis=-1, keepdims=True)
            doh = dov[:, sl]
            dob = doh.astype(BF16)
            delta = jnp.sum(doh * ov[:, sl], axis=-1, keepdims=True)
            dvs.append(lax.dot_general(p.astype(BF16), dob, _TN, preferred_element_type=F32))
            dp = lax.dot_general(dob, vv[:, sl], _NT, preferred_element_type=F32)
            ds = (p * (dp - delta)).astype(BF16)
            dqs.append(lax.dot_general(ds, kv[:, sl], _NN, preferred_element_type=F32) * scale)
            dks.append(lax.dot_general(ds, qv[:, sl], _TN, preferred_element_type=F32) * scale)
        dq_ref[...] = jnp.concatenate(dqs, axis=1).astype(BF16)
        dk_ref[...] += jnp.concatenate(dks, axis=1)
        dv_ref[...] += jnp.concatenate(dvs, axis=1)

    row = pl.BlockSpec((tq, D_MODEL), lambda i: (i, 0))
    full = pl.BlockSpec((MEM_LEN, D_MODEL), lambda i: (0, 0))
    return _pcall(body, name="xattn_bwd",
                  out_shape=(SDS((s, D_MODEL), BF16), SDS((MEM_LEN, D_MODEL), F32), SDS((MEM_LEN, D_MODEL), F32)),
                  grid=(s // tq,), in_specs=[row, full, full, row, row], out_specs=(row, full, full),
                  dims=("arbitrary",))(q, k, v, o, do)


_SQRT_HALF = 0.7071067811865476
_INV_SQRT_2PI = 0.3989422804014327


def _halo_specs(s, tm, width):
    n8 = s // 8
    r8 = tm // 8
    prev = pl.BlockSpec((8, width), lambda i: (jnp.maximum(i * r8 - 1, 0), 0))
    nxt = pl.BlockSpec((8, width), lambda i: (jnp.minimum((i + 1) * r8, n8 - 1), 0))
    return prev, nxt


def _shifted(x, prev8, next8, i, nt):
    tm = x.shape[0]
    row = lax.broadcasted_iota(jnp.int32, x.shape, 0)
    first = jnp.where(i == 0, 0.0, 1.0) * prev8[7:8, :]
    last = jnp.where(i == nt - 1, 0.0, 1.0) * next8[0:1, :]
    xm1 = jnp.where(row == 0, first, pltpu.roll(x, 1, 0))
    xp1 = jnp.where(row == tm - 1, last, pltpu.roll(x, tm - 1, 0))
    return xm1, xp1


def _glu_fwd(g, up, cw, cb):
    s = g.shape[0]
    tm = _pick(s, (256,))
    nt = s // tm

    def body(g_ref, gp_ref, gn_ref, up_ref, cw_ref, cb_ref, act_ref):
        i = pl.program_id(0)
        gv = g_ref[...]
        gm1, gp1 = _shifted(gv, gp_ref[...], gn_ref[...], i, nt)
        gc = gm1 * cw_ref[0:1, :] + gv * cw_ref[1:2, :] + gp1 * cw_ref[2:3, :] + cb_ref[...]
        gelu = 0.5 * gc * (1.0 + lax.erf(gc * _SQRT_HALF))
        act_ref[...] = (gelu * up_ref[...]).astype(BF16)

    row = pl.BlockSpec((tm, D_FF), lambda i: (i, 0))
    prev, nxt = _halo_specs(s, tm, D_FF)
    return _pcall(body, name="glu_fwd", out_shape=SDS((s, D_FF), BF16), grid=(nt,),
                  in_specs=[row, prev, nxt, row, pl.BlockSpec((8, D_FF), lambda i: (0, 0)),
                            pl.BlockSpec((1, D_FF), lambda i: (0, 0))],
                  out_specs=row, dims=("parallel",))(g, g, g, up, cw, cb)


def _glu_bwd(g, up, dact, cw, cb):
    s = g.shape[0]
    tm = _pick(s, (256,))
    nt = s // tm

    def body(g_ref, gp_ref, gn_ref, up_ref, da_ref, cw_ref, cb_ref, dgc_ref, dup_ref, dcw_ref, dcb_ref, a0, a1, a2, a3):
        i = pl.program_id(0)

        @pl.when(i == 0)
        def _():
            for a in (a0, a1, a2, a3):
                a[...] = jnp.zeros_like(a)

        gv = g_ref[...]
        gm1, gp1 = _shifted(gv, gp_ref[...], gn_ref[...], i, nt)
        gc = gm1 * cw_ref[0:1, :] + gv * cw_ref[1:2, :] + gp1 * cw_ref[2:3, :] + cb_ref[...]
        cdf = 0.5 * (1.0 + lax.erf(gc * _SQRT_HALF))
        pdf = jnp.exp(-0.5 * gc * gc) * _INV_SQRT_2PI
        da = da_ref[...]
        dup_ref[...] = (da * (gc * cdf)).astype(BF16)
        dgc = da * up_ref[...] * (cdf + gc * pdf)
        dgc_ref[...] = dgc

        def fold(t):
            return jnp.sum(t.reshape(tm // 8, 8, D_FF), axis=0)

        a0[...] += fold(dgc * gm1)
        a1[...] += fold(dgc * gv)
        a2[...] += fold(dgc * gp1)
        a3[...] += fold(dgc)

        @pl.when(i == nt - 1)
        def _():
            dcw_ref[...] = jnp.concatenate(
                [jnp.sum(a[...], axis=0, keepdims=True) for a in (a0, a1, a2)] + [jnp.zeros((5, D_FF), F32)], axis=0)
            dcb_ref[...] = jnp.sum(a3[...], axis=0, keepdims=True)

    row = pl.BlockSpec((tm, D_FF), lambda i: (i, 0))
    prev, nxt = _halo_specs(s, tm, D_FF)
    cw_spec = pl.BlockSpec((8, D_FF), lambda i: (0, 0))
    cb_spec = pl.BlockSpec((1, D_FF), lambda i: (0, 0))
    return _pcall(body, name="glu_bwd",
                  out_shape=(SDS((s, D_FF), F32), SDS((s, D_FF), BF16), SDS((8, D_FF), F32), SDS((1, D_FF), F32)),
                  grid=(nt,), in_specs=[row, prev, nxt, row, row, cw_spec, cb_spec],
                  out_specs=(row, row, cw_spec, cb_spec), scratch_shapes=[pltpu.VMEM((8, D_FF), F32)] * 4,
                  dims=("arbitrary",))(g, g, g, up, dact, cw, cb)


def _conv_bwd_input(dgc, cw):
    s = dgc.shape[0]
    tm = _pick(s, (256,))
    nt = s // tm

    def body(x_ref, xp_ref, xn_ref, cw_ref, o_ref):
        i = pl.program_id(0)
        xv = x_ref[...]
        xm1, xp1 = _shifted(xv, xp_ref[...], xn_ref[...], i, nt)
        o_ref[...] = (xp1 * cw_ref[0:1, :] + xv * cw_ref[1:2, :] + xm1 * cw_ref[2:3, :]).astype(BF16)

    row = pl.BlockSpec((tm, D_FF), lambda i: (i, 0))
    prev, nxt = _halo_specs(s, tm, D_FF)
    return _pcall(body, name="conv_bwd_input", out_shape=SDS((s, D_FF), BF16), grid=(nt,),
                  in_specs=[row, prev, nxt, pl.BlockSpec((8, D_FF), lambda i: (0, 0))], out_specs=row,
                  dims=("parallel",))(dgc, dgc, dgc, cw)


def _adamw(w, g, m, v, name):
    rows, cols = w.shape
    tr = _pick(rows, (256, 128, 64, 32, 16, 8))
    c1 = 1.0 - ADAM_B1 ** ADAM_STEP
    c2 = 1.0 - ADAM_B2 ** ADAM_STEP

    def body(w_ref, g_ref, m_ref, v_ref, d_ref, nm_ref, nv_ref):
        gv = g_ref[...]
        nm = ADAM_B1 * m_ref[...] + (1.0 - ADAM_B1) * gv
        nv = ADAM_B2 * v_ref[...] + (1.0 - ADAM_B2) * (gv * gv)
        d_ref[...] = -ADAM_LR * ((nm / c1) / (jnp.sqrt(nv / c2) + ADAM_EPS) + ADAM_WD * w_ref[...])
        nm_ref[...] = nm
        nv_ref[...] = nv

    blk = pl.BlockSpec((tr, cols), lambda i: (i, 0))
    return _pcall(body, name=name, out_shape=(SDS(w.shape, F32),) * 3, grid=(rows // tr,), in_specs=[blk] * 4,
                  out_specs=(blk,) * 3, dims=("parallel",))(w, g, m, v)


def _all_gather_rows(x_shard, *, name, in_vmem, sum_rows=False):
    m_per, n = x_shard.shape

    def body(x_ref, out_ref, *rest):
        if sum_rows:
            sum_ref, send_sems, recv_sems, local_sem = rest
        else:
            send_sems, recv_sems, local_sem = rest
        x, y, c = lax.axis_index("x"), lax.axis_index("y"), lax.axis_index("c")
        me, sibling = (x, y, c), (x, y, 1 - c)
        chips = [(1 - x, y), (x, 1 - y), (1 - x, 1 - y)]

        def rows(px, py, pc):
            return out_ref.at[pl.ds((4 * px + 2 * py + pc) * m_per, m_per), :]

        def copy(k, block, to, src=None):
            return pltpu.make_async_remote_copy(
                src_ref=rows(*block) if src is None else src, dst_ref=rows(*block), send_sem=send_sems.at[k],
                recv_sem=recv_sems.at[k], device_id=to, device_id_type=pl.DeviceIdType.MESH)

        mine = pltpu.make_async_copy(x_ref, rows(*me), local_sem)
        mine.start()
        first = [copy(0, me, sibling, src=x_ref)]
        first += [copy(1 + j, me, (*chip, c), src=x_ref) for j, chip in enumerate(chips)]
        for cp in first:
            cp.start()
        passed = [copy(4 + j, (*chip, c), sibling) for j, chip in enumerate(chips)]
        for j, chip in enumerate(chips):
            copy(1 + j, (*chip, c), me).wait_recv()
            passed[j].start()
        copy(0, sibling, me).wait_recv()
        for j, chip in enumerate(chips):
            copy(4 + j, (*chip, 1 - c), me).wait_recv()
        for cp in first + passed:
            cp.wait_send()
        mine.wait()
        if sum_rows:
            acc = out_ref[0:m_per, :]
            for dev in range(1, N_DEV):
                acc = acc + out_ref[dev * m_per:(dev + 1) * m_per, :]
            sum_ref[...] = acc

    space = pltpu.VMEM if in_vmem else pl.ANY
    out_shape = [SDS((N_DEV * m_per, n), x_shard.dtype)]
    out_specs = [pl.BlockSpec(memory_space=space)]
    if sum_rows:
        out_shape.append(SDS((m_per, n), x_shard.dtype))
        out_specs.append(pl.BlockSpec(memory_space=pltpu.VMEM))
    res = _PALLAS_CALL(
        body, name=name, out_shape=tuple(out_shape), in_specs=[pl.BlockSpec(memory_space=space)],
        out_specs=tuple(out_specs),
        scratch_shapes=[pltpu.SemaphoreType.DMA((7,)), pltpu.SemaphoreType.DMA((7,)), pltpu.SemaphoreType.DMA],
        compiler_params=pltpu.CompilerParams(vmem_limit_bytes=VMEM_LIMIT_BYTES),
    )(x_shard)
    return res if sum_rows else res[0]


_HBM = pl.BlockSpec(memory_space=pltpu.HBM)
_SEM = pl.BlockSpec(memory_space=pltpu.SEMAPHORE)
_SPLIT_PARAMS = dict(has_side_effects=pltpu.SideEffectType.DATAFLOW_SIDE_EFFECTING)


def _split_copies(src_ref, land_ref, send_sems, recv_sems, gather):
    x, y, c = lax.axis_index("x"), lax.axis_index("y"), lax.axis_index("c")
    copies = []
    for k in range(1, N_DEV):
        px = 1 - x if k & 4 else x
        py = 1 - y if k & 2 else y
        pc = 1 - c if k & 1 else c
        if gather:
            rows = src_ref.shape[0]
            src, dst = src_ref, land_ref.at[pl.ds((4 * x + 2 * y + c) * rows, rows), :]
        else:
            src, dst = src_ref.at[4 * px + 2 * py + pc], land_ref.at[k - 1]
        copies.append(pltpu.make_async_remote_copy(
            src_ref=src, dst_ref=dst, send_sem=send_sems.at[k - 1], recv_sem=recv_sems.at[k - 1],
            device_id=(px, py, pc), device_id_type=pl.DeviceIdType.MESH))
    return copies


def _exchange_start(src, land_shape, *, gather, name):
    def body(src_ref, land_ref, send_sems, recv_sems, src_thru, land_thru, token):
        for cp in _split_copies(src_ref, land_ref, send_sems, recv_sems, gather):
            cp.start()
        token[...] = jnp.zeros_like(token)

    land = pltpu.with_memory_space_constraint(lax.empty(land_shape, src.dtype), pltpu.HBM)
    return _PALLAS_CALL(
        body, name=name,
        out_shape=(pltpu.SemaphoreType.DMA((N_DEV - 1,)), pltpu.SemaphoreType.DMA((N_DEV - 1,)),
                   pltpu.HBM(src.shape, src.dtype), pltpu.HBM(land_shape, src.dtype), SDS((8, 128), F32)),
        in_specs=(_HBM, _HBM), out_specs=(_SEM, _SEM, _HBM, _HBM, pl.BlockSpec(memory_space=pltpu.VMEM)),
        input_output_aliases={0: 2, 1: 3}, compiler_params=pltpu.CompilerParams(**_SPLIT_PARAMS),
    )(pltpu.with_memory_space_constraint(src, pltpu.HBM), land)


def _exchange_wait(started, after, *, gather, name):
    send_sems, recv_sems, src_thru, land_thru, _ = started

    def body(src_ref, land_ref, send_sems, recv_sems, after_ref, src_out, land_out):
        copies = _split_copies(src_ref, land_ref, send_sems, recv_sems, gather)
        for cp in copies:
            cp.wait_send()
        for cp in copies:
            cp.wait_recv()

    return _PALLAS_CALL(
        body, name=name,
        out_shape=(pltpu.HBM(src_thru.shape, src_thru.dtype), pltpu.HBM(land_thru.shape, land_thru.dtype)),
        in_specs=(_HBM, _HBM, _SEM, _SEM, pl.BlockSpec(memory_space=pl.ANY)), out_specs=(_HBM, _HBM),
        input_output_aliases={0: 0, 1: 1}, compiler_params=pltpu.CompilerParams(**_SPLIT_PARAMS),
    )(src_thru, land_thru, send_sems, recv_sems, after)


def _sum_parts(own, land, name):
    r, n = own.shape
    tr = _pick(r, (264, 320, 336, 128, 64, 32, 16, 8))

    def body(own_ref, x_ref, o_ref):
        acc = own_ref[...]
        for k in range(N_DEV - 1):
            acc = acc + x_ref[k]
        o_ref[...] = acc

    return _pcall(body, name=name, out_shape=SDS((r, n), F32), grid=(r // tr,),
                  in_specs=[pl.BlockSpec((tr, n), lambda i: (i, 0)), pl.BlockSpec((N_DEV - 1, tr, n), lambda i: (0, i, 0))],
                  out_specs=pl.BlockSpec((tr, n), lambda i: (i, 0)), dims=("parallel",))(own, land)


def _pad_rows(a, rows):
    return jnp.pad(a, ((0, rows - a.shape[0]), (0, 0)))


def kernel(x, mem, positions, ln_in_g, ln_in_b, w_in, attn_sink, g_win, g_dil, w_mix_out, ln1_g, ln1_b, mem_ln_g, mem_ln_b, w_xq, w_xk, w_xv, w_xo, ln2_g, ln2_b, w_gate, w_up, conv_w, conv_b, w_down, ln3_g, ln3_b, loss_target, m_ln_in_g, m_ln_in_b, m_w_in, m_attn_sink, m_g_win, m_g_dil, m_w_mix_out, m_ln1_g, m_ln1_b, m_mem_ln_g, m_mem_ln_b, m_w_xq, m_w_xk, m_w_xv, m_w_xo, m_ln2_g, m_ln2_b, m_w_gate, m_w_up, m_conv_w, m_conv_b, m_w_down, m_ln3_g, m_ln3_b, v_ln_in_g, v_ln_in_b, v_w_in, v_attn_sink, v_g_win, v_g_dil, v_w_mix_out, v_ln1_g, v_ln1_b, v_mem_ln_g, v_mem_ln_b, v_w_xq, v_w_xk, v_w_xv, v_w_xo, v_ln2_g, v_ln2_b, v_w_gate, v_w_up, v_conv_w, v_conv_b, v_w_down, v_ln3_g, v_ln3_b):
    weights = dict(ln_in_g=ln_in_g, ln_in_b=ln_in_b, w_in=w_in, attn_sink=attn_sink, g_win=g_win, g_dil=g_dil, w_mix_out=w_mix_out, ln1_g=ln1_g, ln1_b=ln1_b, mem_ln_g=mem_ln_g, mem_ln_b=mem_ln_b, w_xq=w_xq, w_xk=w_xk, w_xv=w_xv, w_xo=w_xo, ln2_g=ln2_g, ln2_b=ln2_b, w_gate=w_gate, w_up=w_up, conv_w=conv_w, conv_b=conv_b, w_down=w_down, ln3_g=ln3_g, ln3_b=ln3_b)
    mom_m = dict(ln_in_g=m_ln_in_g, ln_in_b=m_ln_in_b, w_in=m_w_in, attn_sink=m_attn_sink, g_win=m_g_win, g_dil=m_g_dil, w_mix_out=m_w_mix_out, ln1_g=m_ln1_g, ln1_b=m_ln1_b, mem_ln_g=m_mem_ln_g, mem_ln_b=m_mem_ln_b, w_xq=m_w_xq, w_xk=m_w_xk, w_xv=m_w_xv, w_xo=m_w_xo, ln2_g=m_ln2_g, ln2_b=m_ln2_b, w_gate=m_w_gate, w_up=m_w_up, conv_w=m_conv_w, conv_b=m_conv_b, w_down=m_w_down, ln3_g=m_ln3_g, ln3_b=m_ln3_b)
    mom_v = dict(ln_in_g=v_ln_in_g, ln_in_b=v_ln_in_b, w_in=v_w_in, attn_sink=v_attn_sink, g_win=v_g_win, g_dil=v_g_dil, w_mix_out=v_w_mix_out, ln1_g=v_ln1_g, ln1_b=v_ln1_b, mem_ln_g=v_mem_ln_g, mem_ln_b=v_mem_ln_b, w_xq=v_w_xq, w_xk=v_w_xk, w_xv=v_w_xv, w_xo=v_w_xo, ln2_g=v_ln2_g, ln2_b=v_ln2_b, w_gate=v_w_gate, w_up=v_w_up, conv_w=v_conv_w, conv_b=v_conv_b, w_down=v_w_down, ln3_g=v_ln3_g, ln3_b=v_ln3_b)
    order = list(weights)
    s = x.shape[1]
    xs = x[0]
    mems = mem[0]
    target = loss_target[0]
    row = lambda a: a.reshape(1, -1)

    shard_rows = dict(w_in=w_in[0].T, w_gate=w_gate[0].T, w_up=w_up[0].T, w_mix_out=w_mix_out[0], w_xq=w_xq[0],
                      w_xk=w_xk[0], w_xv=w_xv[0], w_xo=w_xo[0], w_down=w_down[0])
    me_lin = 4 * lax.axis_index("x") + 2 * lax.axis_index("y") + lax.axis_index("c")
    w_in_full = _all_gather_rows(shard_rows["w_in"].astype(BF16), name="w_in_all_gather", in_vmem=False)
    w_in_t = jnp.concatenate([w_in_full[768:], w_in_full[:768]], axis=0)
    late_rows = PACK_ROWS[1:]
    late_r = sum(r for _, r in late_rows)
    packed = jnp.concatenate([shard_rows[n].astype(BF16) for n, _ in late_rows], axis=0)
    w_started = _exchange_start(packed, (N_DEV * late_r, D_MODEL), gather=True, name="weight_gather_start")
    cw_pad = jnp.pad(conv_w[0], ((0, 5), (0, 32)))
    cw_all = _all_gather_rows(cw_pad, name="conv_w_all_gather", in_vmem=True).reshape(N_DEV, 8, 384)
    cw_full = jnp.transpose(cw_all[:, :3, :352], (1, 0, 2)).reshape(3, D_FF)
    cw8 = _pad_rows(cw_full, 8)

    tabs = _rope_tables(positions.astype(F32).reshape(s, 1) + w_started[4][0, 0])
    h0, h0b = _ln_fwd(xs, None, row(ln_in_g), row(ln_in_b), 1.0, "ln_in_fwd")
    zw, *zg = _proj_rope(h0b, w_in_t, tabs[0])
    oa, lse_a = _banded_fwd(zw, attn_sink, name="win_attn_fwd", **_WIN_CFG)
    og_views, lg_views = [], []
    for gi in range(3):
        o_g, l_g = _banded_fwd(zg[gi], None, name=f"dil_attn_fwd{gi}", **_dil_cfg(gi))
        og_views.append(o_g)
        lg_views.append(l_g)
    mixed, ob_views, lb_views = _mix_norm_fwd(oa, og_views, lg_views, g_win, g_dil)
    packed_thru, land = _exchange_wait(w_started, mixed, gather=True, name="weight_gather_wait")
    gathered = lax.dynamic_update_slice(land, packed_thru, (me_lin * late_r, 0)).reshape(N_DEV, late_r, D_MODEL)
    full = {}
    off = 0
    for n, r in late_rows:
        full[n] = gathered[:, off:off + r, :].reshape(N_DEV * r, D_MODEL)
        off += r
    mix = _mm(mixed, full["w_mix_out"], trans_b=False, out_dtype=F32, name="mm_mix_out")
    h1, h1b = _ln_fwd(h0, mix, ln1_g, ln1_b, ALPHA, "ln1_fwd")
    _, mem_nb = _ln_fwd(mems, None, mem_ln_g, mem_ln_b, 1.0, "mem_ln_fwd")
    kx = _mm(mem_nb, full["w_xk"], trans_b=False, out_dtype=BF16, name="mm_xk")
    vx = _mm(mem_nb, full["w_xv"], trans_b=False, out_dtype=BF16, name="mm_xv")
    qx = _mm(h1b, full["w_xq"], trans_b=False, out_dtype=BF16, name="mm_xq")
    ox, oxb = _xattn_fwd(qx, kx, vx)
    xa = _mm(oxb, full["w_xo"], trans_b=False, out_dtype=F32, name="mm_xo")
    h2, h2b = _ln_fwd(h1, xa, ln2_g, ln2_b, ALPHA, "ln2_fwd")
    gate = _mm(h2b, full["w_gate"], trans_b=True, out_dtype=F32, name="mm_gate")
    up = _mm(h2b, full["w_up"], trans_b=True, out_dtype=F32, name="mm_up")
    act = _glu_fwd(gate, up, cw8, conv_b)
    ff = _mm(act, full["w_down"], trans_b=False, out_dtype=F32, name="mm_down")

    du3, du3b, d_ln3_g, d_ln3_b, loss_local = _ln_bwd(h2, ff, target, ln3_g, ln3_b, ALPHA, "ln3_bwd_loss",
                                                      loss_mode=True)
    dact = _mm(du3b, full["w_down"], trans_b=True, out_dtype=F32, name="mm_d_act")
    dw_down = _mm_tn(act, du3b, name="mm_dw_down")
    dgc, dup, dcw8, d_conv_b = _glu_bwd(gate, up, dact, cw8, conv_b)
    dgate = _conv_bwd_input(dgc, cw8)
    dh2 = _mm(dgate, full["w_gate"], trans_b=False, out_dtype=F32, name="mm_dh2", more=((dup, full["w_up"], False),),
              addends=(du3,), coefs=(ALPHA,))
    dw_gate_t = _mm_tn(dgate, h2b, name="mm_dw_gate")
    dw_up_t = _mm_tn(dup, h2b, name="mm_dw_up")
    rows_of = dict(PACK_ROWS)

    def start_grad_exchange(parts, name):
        gp = jnp.concatenate([g.reshape(N_DEV, rows_of[n], D_MODEL) for n, g in parts], axis=1)
        return _exchange_start(gp, (N_DEV - 1,) + gp.shape[1:], gather=False, name=name)

    ffn_parts = (("w_gate", dw_gate_t), ("w_up", dw_up_t), ("w_down", dw_down))
    ffn_started = start_grad_exchange(ffn_parts, "grad_start_ffn")
    du2, du2b, d_ln2_g, d_ln2_b = _ln_bwd(h1, xa, dh2, ln2_g + ffn_started[4][0, 0], ln2_b, ALPHA, "ln2_bwd")
    dox = _mm(du2b, full["w_xo"], trans_b=True, out_dtype=F32, name="mm_d_ox")
    dw_xo = _mm_tn(oxb, du2b, name="mm_dw_xo")
    dqx, dkx, dvx = _xattn_bwd(qx, kx, vx, ox, dox)
    dh1 = _mm(dqx, full["w_xq"], trans_b=True, out_dtype=F32, name="mm_dh1", addends=(du2,), coefs=(ALPHA,))
    dw_xq = _mm_tn(h1b, dqx, name="mm_dw_xq")
    dw_xk = _mm_tn(mem_nb, dkx, name="mm_dw_xk")
    dw_xv = _mm_tn(mem_nb, dvx, name="mm_dw_xv")
    dmem_n = _mm(dkx, full["w_xk"], trans_b=True, out_dtype=F32, name="mm_dmem", more=((dvx, full["w_xv"], True),))
    _, _, d_mem_ln_g, d_mem_ln_b = _ln_bwd(mems, None, dmem_n, mem_ln_g, mem_ln_b, 1.0, "mem_ln_bwd")
    du1, du1b, d_ln1_g, d_ln1_b = _ln_bwd(h0, mix, dh1, ln1_g, ln1_b, ALPHA, "ln1_bwd")
    dmixed = _mm(du1b, full["w_mix_out"], trans_b=True, out_dtype=F32, name="mm_d_mixed")
    dw_mix_out = _mm_tn(mixed, du1b, name="mm_dw_mix_out")
    attn_parts = (("w_mix_out", dw_mix_out), ("w_xq", dw_xq), ("w_xk", dw_xk), ("w_xv", dw_xv), ("w_xo", dw_xo))
    attn_started = start_grad_exchange(attn_parts, "grad_start_attn")
    doa, dob_views, d_g_win, d_g_dil = _mix_norm_bwd(oa, ob_views[0], dmixed, g_win + attn_started[4][0, 0], g_dil)
    dqa, dkva, dsink8 = _banded_bwd(zw, oa, lse_a, doa, tabs[0], attn_sink, kv_heads=_WIN_KV, name="win_attn_bwd",
                                    **_WIN_CFG)
    dq_views, dkv_views = [], []
    for gi in range(3):
        dq_g, dkv_g = _banded_bwd(zg[gi], ob_views[gi], lb_views[gi], dob_views[gi], tabs[gi], None, kv_heads=_DIL_KV,
                                  name=f"dil_attn_bwd{gi}", **_dil_cfg(gi))
        dq_views.append(dq_g)
        dkv_views.append(dkv_g)
    dz = _dz_assemble(dq_views, dkv_views, dqa, dkva)
    dw_in_tz = _mm_tn(dz, h0b, name="mm_dw_in")
    dw_in_t = jnp.concatenate([dw_in_tz[4608:], dw_in_tz[:4608]], axis=0)
    in_parts = (("w_in", dw_in_t),)
    in_started = start_grad_exchange(in_parts, "grad_start_in")
    dh0 = _mm(dz, w_in_t, trans_b=False, out_dtype=F32, name="mm_dh0", addends=(du1,), coefs=(ALPHA,),
              after=in_started[4])
    dx, _, d_ln_in_g, d_ln_in_b = _ln_bwd(xs, None, dh0, row(ln_in_g), row(ln_in_b), 1.0, "ln_in_bwd")

    grads = {}
    small = jnp.concatenate([
        d_ln_in_g, d_ln_in_b, d_ln1_g, d_ln1_b, d_mem_ln_g, d_mem_ln_b, d_ln2_g, d_ln2_b, d_ln3_g, d_ln3_b,
        jnp.concatenate([d_g_win, d_g_dil], axis=1),
        jnp.pad(d_conv_b, ((0, 0), (0, 3072 - D_FF))).reshape(3, 1024),
        jnp.pad(dsink8[0:1, :], ((0, 0), (0, 1024 - 128))),
        jnp.pad(dcw8[0:3], ((0, 0), (0, 3072 - D_FF))).reshape(9, 1024),
    ], axis=0)
    _, ssum = _all_gather_rows(small, name="small_grad_all_reduce", in_vmem=True, sum_rows=True)
    names10 = ["ln_in_g", "ln_in_b", "ln1_g", "ln1_b", "mem_ln_g", "mem_ln_b", "ln2_g", "ln2_b", "ln3_g", "ln3_b"]
    for i, n in enumerate(names10):
        grads[n] = ssum[i].reshape(weights[n].shape)
    grads["g_win"] = ssum[10:11, :512]
    grads["g_dil"] = ssum[10:11, 512:]
    grads["conv_b"] = ssum[11:14].reshape(1, 3072)[:, :D_FF]
    grads["attn_sink"] = ssum[14:15, :8]
    dcw_full = ssum[15:24].reshape(3, 3072)[:, :D_FF]
    grads["conv_w"] = lax.dynamic_slice_in_dim(dcw_full, me_lin * 352, 352, axis=1)[None]

    delta, new_m, new_v = {}, {}, {}
    big = [n for n, _ in PACK_ROWS]
    small_names = [n for n in order if n not in big]

    def pack_small(src):
        rows_ = []
        for n in small_names:
            flat = src[n].reshape(1, -1)
            width = -(-flat.shape[1] // 1024) * 1024
            rows_.append(jnp.pad(flat, ((0, 0), (0, width - flat.shape[1]))).reshape(-1, 1024))
        packed_ = jnp.concatenate(rows_, axis=0)
        return _pad_rows(packed_, -(-packed_.shape[0] // 8) * 8)

    d_s, m_s, v_s = _adamw(pack_small(weights), pack_small(grads), pack_small(mom_m), pack_small(mom_v), "adamw_small")
    r0 = 0
    for n in small_names:
        size = weights[n].size
        nrow = -(-size // 1024)
        for dst, src in ((delta, d_s), (new_m, m_s), (new_v, v_s)):
            dst[n] = src[r0:r0 + nrow].reshape(-1)[:size].reshape(weights[n].shape)
        r0 += nrow

    after = d_s
    for parts, started, tag in ((ffn_parts, ffn_started, "ffn"), (attn_parts, attn_started, "attn"),
                                (in_parts, in_started, "in")):
        gp_thru, land = _exchange_wait(started, after, gather=False, name=f"grad_wait_{tag}")
        own = lax.dynamic_index_in_dim(gp_thru, me_lin, axis=0, keepdims=False)
        gsum = _sum_parts(own, land, f"grad_sum_{tag}")
        off = 0
        for n, _ in parts:
            blk = gsum[off:off + rows_of[n]]
            off += rows_of[n]
            grads[n] = (blk.T if n in ("w_in", "w_gate", "w_up") else blk)[None]
            shp = weights[n].shape
            d_, m_, v_ = _adamw(weights[n].reshape(shp[1:]), grads[n].reshape(shp[1:]), mom_m[n].reshape(shp[1:]),
                                mom_v[n].reshape(shp[1:]), f"adamw_{n}")
            delta[n], new_m[n], new_v[n] = d_.reshape(shp), m_.reshape(shp), v_.reshape(shp)
            after = d_

    loss = lax.psum(loss_local[0, 0], MESH_AXES)
    return (loss, dx[None], *[grads[n] for n in order], *[delta[n] for n in order], *[new_m[n] for n in order],
            *[new_v[n] for n in order])
```

```python
import functools
import math

import jax
import jax.numpy as jnp
from jax import lax
from jax.experimental import pallas as pl
from jax.experimental.pallas import tpu as pltpu

F32 = jnp.float32
BF16 = jnp.bfloat16
SDS = jax.ShapeDtypeStruct
_PALLAS_CALL = pl.pallas_call

D_MODEL = 1024
HEAD_DIM = 64
WIN_HALF = 128
DIL_PAIRS = ((128, 1), (512, 4), (2048, 16))
DIL_SIDE = 64
ROT_DIM = 16
ROPE_THETA = 500000.0
MEM_LEN = 256
X_HEADS = 4
X_HEAD_DIM = 256
D_FF = 2816
IN_WIDTH = 5376
Z_QB, Z_KB, Z_VB, Z_QA, Z_KA, Z_VA = 0, 1536, 3072, 4608, 5120, 5248
ALPHA = (2.0) ** 0.25
LN_EPS = 1e-5
NEG_INF = -1e30
ADAM_LR, ADAM_B1, ADAM_B2, ADAM_EPS, ADAM_WD, ADAM_STEP = 0.001, 0.9, 0.999, 1e-08, 0.01, 10
N_DEV = 8
MESH_AXES = ("x", "y", "c")
VMEM_LIMIT_BYTES = 52 * 1024 * 1024
ATTN_TQ = 256
TABW = 384

PACK_ROWS = (("w_in", 672), ("w_gate", 352), ("w_up", 352), ("w_mix_out", 128), ("w_xq", 128), ("w_xk", 128),
             ("w_xv", 128), ("w_xo", 128), ("w_down", 352))
SMALL_ROWS = 24


def _pick(n, cands):
    for c in cands:
        if n % c == 0:
            return c
    return n


def _pcall(body, *, name, out_shape, grid=None, in_specs=None, out_specs=None, scratch_shapes=(), dims=None,
           aliases=None):
    kw = {}
    if grid is not None:
        kw["grid"] = grid
    if in_specs is not None:
        kw["in_specs"] = in_specs
    if out_specs is not None:
        kw["out_specs"] = out_specs
    if aliases:
        kw["input_output_aliases"] = aliases
    return _PALLAS_CALL(
        body, name=name, out_shape=out_shape, scratch_shapes=list(scratch_shapes),
        compiler_params=pltpu.CompilerParams(dimension_semantics=dims, vmem_limit_bytes=VMEM_LIMIT_BYTES), **kw)


MM_VMEM_BUDGET = 40 * 1024 * 1024


def _mm(a, b, *, trans_b, out_dtype, name, addends=(), coefs=(), after=None, more=()):
    pairs = ((a, b, trans_b),) + tuple(more)
    m = a.shape[0]
    n = b.shape[0] if trans_b else b.shape[1]
    n_add = len(addends)
    extra = [] if after is None else [after]
    out_bytes = jnp.dtype(out_dtype).itemsize

    def vmem(tm, tn):
        tot = tm * tn * (out_bytes + 4 * n_add)
        for pa, pb, _ in pairs:
            tot += tm * pa.shape[1] * pa.dtype.itemsize + pa.shape[1] * tn * pb.dtype.itemsize
        return 2 * tot

    tm, tn = next(((cm, cn) for cn in (n, 1408, 1024, 512, 256, 128) if n % cn == 0
                   for cm in (1024, 512, 256, 128) if m % cm == 0 and vmem(cm, cn) <= MM_VMEM_BUDGET))
    n_pairs = len(pairs)

    def body(*refs):
        o_ref = refs[2 * n_pairs + n_add + len(extra)]
        acc = None
        for p, (_, _, tb) in enumerate(pairs):
            dn = _NT if tb else _NN
            part = lax.dot_general(refs[2 * p][...].astype(BF16), refs[2 * p + 1][...].astype(BF16), dn,
                                   preferred_element_type=F32)
            acc = part if acc is None else acc + part
        for r_ref, c in zip(refs[2 * n_pairs:2 * n_pairs + n_add], coefs):
            acc = acc + (r_ref[...] if c == 1.0 else c * r_ref[...])
        o_ref[...] = acc.astype(out_dtype)

    in_specs, args = [], []
    for pa, pb, tb in pairs:
        k = pa.shape[1]
        in_specs.append(pl.BlockSpec((tm, k), lambda j, i: (i, 0)))
        in_specs.append(pl.BlockSpec((tn, k), lambda j, i: (j, 0)) if tb else pl.BlockSpec((k, tn), lambda j, i: (0, j)))
        args += [pa, pb]
    in_specs += [pl.BlockSpec((tm, tn), lambda j, i: (i, j)) for _ in addends]
    in_specs += [pl.BlockSpec((8, 128), lambda j, i: (0, 0)) for _ in extra]
    return _pcall(body, name=name, out_shape=SDS((m, n), out_dtype), grid=(n // tn, m // tm), in_specs=in_specs,
                  out_specs=pl.BlockSpec((tm, tn), lambda j, i: (i, j)),
                  dims=("parallel", "parallel"))(*args, *addends, *extra)


def _mm_tn(a, b, *, name):
    s, m = a.shape
    n = b.shape[1]
    tm = _pick(m, (768, 1408, 1024, 512, 256, 128))
    tk = _pick(s, (1024, 512, 256))
    nk = s // tk

    def body(a_ref, b_ref, o_ref, acc_ref):
        kk = pl.program_id(1)

        @pl.when(kk == 0)
        def _():
            acc_ref[...] = jnp.zeros_like(acc_ref)

        acc_ref[...] += lax.dot_general(a_ref[...].astype(BF16), b_ref[...].astype(BF16), (((0,), (0,)), ((), ())),
                                        preferred_element_type=F32)

        @pl.when(kk == nk - 1)
        def _():
            o_ref[...] = acc_ref[...]

    return _pcall(body, name=name, out_shape=SDS((m, n), F32), grid=(m // tm, nk),
                  in_specs=[pl.BlockSpec((tk, tm), lambda i, kk: (kk, i)), pl.BlockSpec((tk, n), lambda i, kk: (kk, 0))],
                  out_specs=pl.BlockSpec((tm, n), lambda i, kk: (i, 0)), scratch_shapes=[pltpu.VMEM((tm, n), F32)],
                  dims=("parallel", "arbitrary"))(a, b)


def _rope_lane_consts():
    lane = jnp.arange(128)
    j = lane % HEAD_DIM
    inv_freq = ROPE_THETA ** (-jnp.arange(0, ROT_DIM, 2, dtype=F32) / ROT_DIM)
    freq = jnp.where(j < ROT_DIM, inv_freq[j % (ROT_DIM // 2)], 0.0).astype(F32)
    lo = (j < ROT_DIM // 2).astype(F32)
    hi = ((j >= ROT_DIM // 2) & (j < ROT_DIM)).astype(F32)
    return jnp.stack([freq, lo, hi] + [jnp.zeros((128,), F32)] * 5)


def _to_classes(x, scr, d):
    if d == 1:
        return [x]
    scr[...] = x
    return [scr[pl.ds(c, x.shape[0] // d, stride=d), :] for c in range(d)]


def _from_classes(parts, scr):
    d = len(parts)
    if d == 1:
        return parts[0]
    for c, part in enumerate(parts):
        scr[pl.ds(c, part.shape[0], stride=d), :] = part
    return scr[...]


DILATIONS = tuple(d for _, d in DIL_PAIRS)


def _rope_tables(posf):
    s = posf.shape[0]
    tm = _pick(s, (1024, 512))

    def body(p_ref, c_ref, *rest):
        o_refs, scr = rest[:-1], rest[-1]
        ang = p_ref[...] * c_ref[0:1, :]
        lo = c_ref[1:2, :]
        hi = c_ref[2:3, :]
        cs = jnp.cos(ang)
        sn = jnp.sin(ang)
        for q, t in enumerate((jnp.where(lo + hi > 0.0, cs, 1.0), -sn * lo, sn * hi)):
            for o_ref, d in zip(o_refs, DILATIONS):
                for c, part in enumerate(_to_classes(t, scr, d)):
                    o_ref[:, c * TABW + q * 128:c * TABW + (q + 1) * 128] = part

    return _pcall(body, name="rope_tables", out_shape=tuple(SDS((s // d, d * TABW), F32) for d in DILATIONS),
                  grid=(s // tm,),
                  in_specs=[pl.BlockSpec((tm, 1), lambda i: (i, 0)), pl.BlockSpec((8, 128), lambda i: (0, 0))],
                  out_specs=tuple(pl.BlockSpec((tm // d, d * TABW), lambda i: (i, 0)) for d in DILATIONS),
                  scratch_shapes=[pltpu.VMEM((tm, 128), F32)], dims=("parallel",))(posf, _rope_lane_consts())


def _rope_apply(x, tab, sign):
    w = x.shape[1]
    rep = w // 128
    c = jnp.tile(tab[:, 0:128], (1, rep)) if rep > 1 else tab[:, 0:128]
    a = jnp.tile(tab[:, 128:256], (1, rep)) if rep > 1 else tab[:, 128:256]
    b = jnp.tile(tab[:, 256:384], (1, rep)) if rep > 1 else tab[:, 256:384]
    up = pltpu.roll(x, w - 8, 1)
    dn = pltpu.roll(x, 8, 1)
    if sign > 0:
        return x * c + up * a + dn * b
    return x * c - up * a - dn * b


def _proj_rope(h0b, w_t, tab):
    s = h0b.shape[0]
    tm = _pick(s, (512,))
    tn = 256

    def body(a_ref, w_ref, t_ref, zw_ref, z0_ref, z1_ref, z2_ref, scr):
        z_refs = (z0_ref, z1_ref, z2_ref)
        a = a_ref[...]
        tabv = t_ref[...]
        for c0 in range(0, IN_WIDTH, tn):
            z = lax.dot_general(a, w_ref[c0:c0 + tn, :], _NT, preferred_element_type=F32)
            for g0 in range(c0, c0 + tn, 128):
                zg = z[:, g0 - c0:g0 - c0 + 128]
                if g0 < Z_VB or Z_QA <= g0 < Z_VA:
                    zg = _rope_apply(zg, tabv, 1)
                if g0 >= Z_QA:
                    zw_ref[:, g0 - Z_QA:g0 - Z_QA + 128] = zg.astype(BF16)
                    continue
                kind, within = divmod(g0, 1536)
                grp, off = divmod(within, 512)
                col = kind * 512 + off
                for c, part in enumerate(_to_classes(zg, scr, DILATIONS[grp])):
                    z_refs[grp][:, c * 1536 + col:c * 1536 + col + 128] = part.astype(BF16)

    return _pcall(body, name="proj_rope",
                  out_shape=(SDS((s, 768), BF16),) + tuple(SDS((s // d, d * 1536), BF16) for d in DILATIONS),
                  grid=(s // tm,),
                  in_specs=[pl.BlockSpec((tm, D_MODEL), lambda i: (i, 0)), pl.BlockSpec((IN_WIDTH, D_MODEL), lambda i: (0, 0)),
                            pl.BlockSpec((tm, TABW), lambda i: (i, 0))],
                  out_specs=(pl.BlockSpec((tm, 768), lambda i: (i, 0)),)
                  + tuple(pl.BlockSpec((tm // d, d * 1536), lambda i: (i, 0)) for d in DILATIONS),
                  scratch_shapes=[pltpu.VMEM((tm, 128), F32)], dims=("parallel",))(h0b, w_t, tab)


def _band_specs(sd, blk, tq, width, per_tok, cb):
    r = tq // blk
    nbk = sd // blk
    prev = pl.BlockSpec((blk, width), lambda c, j: (jnp.maximum(j * r - 1, 0), c * per_tok + cb))
    cur = pl.BlockSpec((tq, width), lambda c, j: (j, c * per_tok + cb))
    nxt = pl.BlockSpec((blk, width), lambda c, j: (jnp.minimum((j + 1) * r, nbk - 1), c * per_tok + cb))
    return [prev, cur, nxt]


def _band_bias(j, blk, tq, sd):
    shape = (tq, tq + 2 * blk)
    qpos = j * tq + lax.broadcasted_iota(jnp.int32, shape, 0)
    kpos = j * tq - blk + lax.broadcasted_iota(jnp.int32, shape, 1)
    ok = (jnp.abs(qpos - kpos) <= blk) & (kpos >= 0) & (kpos < sd)
    return jnp.where(ok, 0.0, NEG_INF)


_NT = (((1,), (1,)), ((), ()))
_NN = (((1,), (0,)), ((), ()))
_TN = (((0,), (0,)), ((), ()))


def _banded_fwd(zv, sink, *, d, blk, tq, ptw, qw, kw, qcb, kcb, vcb, heads, name):
    sd = zv.shape[0]
    tq = min(tq, sd)
    has_sink = sink is not None
    scale = HEAD_DIM ** -0.5

    def body(q_ref, kp, kc, kn, vp, vc, vn, *rest):
        if has_sink:
            sink_ref, o_ref, lse_ref = rest
        else:
            o_ref, lse_ref = rest
        j = pl.program_id(1)
        q = q_ref[...] * scale
        k = jnp.concatenate([kp[...], kc[...], kn[...]], axis=0)
        v = jnp.concatenate([vp[...], vc[...], vn[...]], axis=0)
        bias = _band_bias(j, blk, tq, sd)
        outs, lses = [], []
        for ql, kl, vl, si in heads:
            sc = lax.dot_general(q[:, ql:ql + HEAD_DIM], k[:, kl:kl + HEAD_DIM], _NT, preferred_element_type=F32) + bias
            m = jnp.max(sc, axis=-1, keepdims=True)
            if has_sink:
                m = jnp.maximum(m, sink_ref[0, si])
            p = jnp.exp(sc - m)
            den = jnp.sum(p, axis=-1, keepdims=True)
            if has_sink:
                den = den + jnp.exp(sink_ref[0, si] - m)
            o = lax.dot_general(p.astype(BF16), v[:, vl:vl + HEAD_DIM], _NN, preferred_element_type=F32) / den
            outs.append(o)
            lses.append(jnp.broadcast_to(m + jnp.log(den), (tq, HEAD_DIM)))
        o_ref[...] = jnp.concatenate(outs, axis=1)
        lse_ref[...] = jnp.concatenate(lses, axis=1)

    in_specs = ([pl.BlockSpec((tq, qw), lambda c, j: (j, c * (ptw // qw) + qcb))]
                + _band_specs(sd, blk, tq, kw, ptw // kw, kcb) + _band_specs(sd, blk, tq, kw, ptw // kw, vcb))
    args = [zv] * 7
    if has_sink:
        in_specs.append(pl.BlockSpec(memory_space=pltpu.SMEM))
        args.append(sink)
    o_spec = pl.BlockSpec((tq, qw), lambda c, j: (j, c))
    return _pcall(body, name=name, out_shape=(SDS((sd, d * qw), F32), SDS((sd, d * qw), F32)), grid=(d, sd // tq),
                  in_specs=in_specs, out_specs=(o_spec, o_spec), dims=("parallel", "parallel"))(*args)


def _banded_bwd(zv, ov, lv, dov, tv, sink, *, d, blk, tq, ptw, qw, kw, qcb, kcb, vcb, heads, kv_heads, name):
    sd = zv.shape[0]
    tq = min(tq, sd)
    nt = sd // tq
    r = tq // blk
    nbk = sd // blk
    has_sink = sink is not None
    scale = HEAD_DIM ** -0.5
    kvw = HEAD_DIM * len(kv_heads)

    def add_rows(x, y, last):
        if tq == blk:
            return x + y
        if last:
            return jnp.concatenate([x[:tq - blk], x[tq - blk:] + y], axis=0)
        return jnp.concatenate([x[:blk] + y, x[blk:]], axis=0)

    def body(q_ref, kp, kc, kn, vp, vc, vn, o_ref, l_ref, do_ref, t_ref, tlag_ref, *rest):
        if has_sink:
            sink_ref, dq_ref, dkv_ref, dsink_ref, acck, accv, nxtk, nxtv = rest
        else:
            dq_ref, dkv_ref, acck, accv, nxtk, nxtv = rest
        j = pl.program_id(1)

        @pl.when(j == 0)
        def _():
            nxtk[...] = jnp.zeros_like(nxtk)
            nxtv[...] = jnp.zeros_like(nxtv)

        if has_sink:
            @pl.when((pl.program_id(0) == 0) & (j == 0))
            def _():
                dsink_ref[...] = jnp.zeros_like(dsink_ref)

        def emit(dk_rows, dv_rows):
            dkv_ref[...] = jnp.concatenate([_rope_apply(dk_rows, tlag_ref[...], -1), dv_rows], axis=1).astype(BF16)

        @pl.when(j < nt)
        def _():
            q = q_ref[...] * scale
            k3 = jnp.concatenate([kp[...], kc[...], kn[...]], axis=0)
            v3 = jnp.concatenate([vp[...], vc[...], vn[...]], axis=0)
            o_t, l_t, do_t = o_ref[...], l_ref[...], do_ref[...]
            bias = _band_bias(j, blk, tq, sd)
            dqs = []
            dks = [None] * len(kv_heads)
            dvs = [None] * len(kv_heads)
            dsink_row = jnp.zeros((1, 128), F32)
            lane = lax.broadcasted_iota(jnp.int32, (1, 128), 1)
            for ql, kl, vl, si in heads:
                kvi = kv_heads.index((kl, vl))
                qh = q[:, ql:ql + HEAD_DIM]
                kh3 = k3[:, kl:kl + HEAD_DIM]
                vh3 = v3[:, vl:vl + HEAD_DIM]
                doh = do_t[:, ql:ql + HEAD_DIM]
                delta = jnp.sum(doh * o_t[:, ql:ql + HEAD_DIM], axis=-1, keepdims=True)
                lse = l_t[:, ql:ql + 1]
                dob = doh.astype(BF16)
                sc = lax.dot_general(qh, kh3, _NT, preferred_element_type=F32) + bias
                p = jnp.exp(sc - lse)
                dp = lax.dot_general(dob, vh3, _NT, preferred_element_type=F32)
                dsb = (p * (dp - delta)).astype(BF16)
                dqs.append(lax.dot_general(dsb, kh3, _NN, preferred_element_type=F32) * scale)
                dk = lax.dot_general(dsb, qh, _TN, preferred_element_type=F32)
                dv = lax.dot_general(p.astype(BF16), dob, _TN, preferred_element_type=F32)
                dks[kvi] = dk if dks[kvi] is None else dks[kvi] + dk
                dvs[kvi] = dv if dvs[kvi] is None else dvs[kvi] + dv
                if has_sink:
                    psink = jnp.exp(sink_ref[0, si] - lse)
                    dsink_row = dsink_row + jnp.where(lane == si, -jnp.sum(psink * delta), 0.0)
            dq_ref[...] = _rope_apply(jnp.concatenate(dqs, axis=1), t_ref[...], -1).astype(BF16)
            wk = jnp.concatenate(dks, axis=1) if len(dks) > 1 else dks[0]
            wv = jnp.concatenate(dvs, axis=1) if len(dvs) > 1 else dvs[0]
            if has_sink:
                dsink_ref[0:1, :] += dsink_row

            @pl.when(j > 0)
            def _():
                emit(add_rows(acck[...], wk[:blk], True), add_rows(accv[...], wv[:blk], True))

            acck[...] = add_rows(wk[blk:blk + tq], nxtk[...], False)
            accv[...] = add_rows(wv[blk:blk + tq], nxtv[...], False)
            nxtk[...] = wk[blk + tq:]
            nxtv[...] = wv[blk + tq:]

        @pl.when(j == nt)
        def _():
            emit(acck[...], accv[...])

    def tile(width, per_tok, cb):
        return pl.BlockSpec((tq, width), lambda c, j: (jnp.minimum(j, nt - 1), c * per_tok + cb))

    def halos(width, per_tok, cb):
        before = pl.BlockSpec((blk, width), lambda c, j: (jnp.maximum(jnp.minimum(j, nt - 1) * r - 1, 0), c * per_tok + cb))
        after = pl.BlockSpec((blk, width),
                             lambda c, j: (jnp.minimum((jnp.minimum(j, nt - 1) + 1) * r, nbk - 1), c * per_tok + cb))
        return [before, tile(width, per_tok, cb), after]

    def lagged(width):
        return pl.BlockSpec((tq, width), lambda c, j: (jnp.maximum(j - 1, 0), c))

    in_specs = ([tile(qw, ptw // qw, qcb)] + halos(kw, ptw // kw, kcb) + halos(kw, ptw // kw, vcb)
                + [tile(qw, 1, 0)] * 3 + [tile(TABW, 1, 0), lagged(TABW)])
    args = [zv] * 7 + [ov, lv, dov, tv, tv]
    out_shape = [SDS((sd, d * qw), BF16), SDS((sd, d * 2 * kvw), BF16)]
    out_specs = [tile(qw, 1, 0), lagged(2 * kvw)]
    if has_sink:
        in_specs.append(pl.BlockSpec(memory_space=pltpu.SMEM))
        args.append(sink)
        out_shape.append(SDS((8, 128), F32))
        out_specs.append(pl.BlockSpec((8, 128), lambda c, j: (0, 0)))
    scratch = [pltpu.VMEM((tq, kvw), F32), pltpu.VMEM((tq, kvw), F32), pltpu.VMEM((blk, kvw), F32),
               pltpu.VMEM((blk, kvw), F32)]
    return _pcall(body, name=name, out_shape=tuple(out_shape), grid=(d, nt + 1), in_specs=in_specs,
                  out_specs=tuple(out_specs), scratch_shapes=scratch, dims=("arbitrary", "arbitrary"))(*args)


_WIN_HEADS = tuple((h * HEAD_DIM, (h // 4) * HEAD_DIM, 128 + (h // 4) * HEAD_DIM, h) for h in range(8))
_WIN_KV = ((0, 128), (64, 192))
_WIN_CFG = dict(d=1, blk=WIN_HALF, tq=ATTN_TQ, ptw=768, qw=512, kw=256, qcb=0, kcb=2, vcb=2, heads=_WIN_HEADS)
_DIL_HEADS = tuple((h * HEAD_DIM, h * HEAD_DIM, h * HEAD_DIM, h) for h in range(8))
_DIL_KV = tuple((h * HEAD_DIM, h * HEAD_DIM) for h in range(8))


def _dil_cfg(gi):
    return dict(d=DILATIONS[gi], blk=DIL_SIDE, tq=ATTN_TQ, ptw=1536, qw=512, kw=512, qcb=0, kcb=1, vcb=2,
                heads=_DIL_HEADS)


def _view_specs(tm, width):
    return tuple(pl.BlockSpec((tm // d, d * width), lambda i: (i, 0)) for d in DILATIONS)


def _mix_norm_fwd(oa, og_views, lg_views, g_win, g_dil):
    s = oa.shape[0]
    tm = _pick(s, (512,))

    def body(oa_ref, o0, o1, o2, l0, l1, l2, gw_ref, gd_ref, mixed_ref, ob0, ob1, ob2, lb0, lb1, lb2, scr, ob_s):
        o_refs, l_refs, ob_refs, lb_refs = (o0, o1, o2), (l0, l1, l2), (ob0, ob1, ob2), (lb0, lb1, lb2)
        ssq = jnp.zeros((tm, 1), F32)
        for q in range(4):
            os_, ls_ = [], []
            for g, d in enumerate(DILATIONS):
                cols = [slice(c * 512 + q * 128, c * 512 + (q + 1) * 128) for c in range(d)]
                os_.append(_from_classes([o_refs[g][:, cs] for cs in cols], scr))
                ls_.append(_from_classes([l_refs[g][:, cs] for cs in cols], scr))
            mx = jnp.maximum(jnp.maximum(ls_[0], ls_[1]), ls_[2])
            es = [jnp.exp(l - mx) for l in ls_]
            den = es[0] + es[1] + es[2]
            ob = (es[0] / den) * os_[0] + (es[1] / den) * os_[1] + (es[2] / den) * os_[2]
            lb = mx + jnp.log(den)
            ob_s[:, q * 128:(q + 1) * 128] = ob
            ssq = ssq + jnp.sum(ob * ob, axis=-1, keepdims=True)
            for g, d in enumerate(DILATIONS):
                for val, refs in ((ob, ob_refs), (lb, lb_refs)):
                    for c, part in enumerate(_to_classes(val, scr, d)):
                        refs[g][:, c * 512 + q * 128:c * 512 + (q + 1) * 128] = part
        a = oa_ref[...]
        ra = lax.rsqrt(jnp.mean(a * a, axis=-1, keepdims=True) + LN_EPS)
        rb = lax.rsqrt(ssq * (1.0 / 512) + LN_EPS)
        mixed_ref[...] = jnp.concatenate([a * ra * gw_ref[...], ob_s[...] * rb * gd_ref[...]], axis=1).astype(BF16)

    row = pl.BlockSpec((tm, 512), lambda i: (i, 0))
    vec = pl.BlockSpec((1, 512), lambda i: (0, 0))
    views = _view_specs(tm, 512)
    view_shapes = tuple(SDS((s // d, d * 512), F32) for d in DILATIONS)
    res = _pcall(body, name="mix_norm_fwd", out_shape=(SDS((s, 1024), BF16),) + view_shapes * 2, grid=(s // tm,),
                 in_specs=[row, *views, *views, vec, vec],
                 out_specs=(pl.BlockSpec((tm, 1024), lambda i: (i, 0)),) + views * 2,
                 scratch_shapes=[pltpu.VMEM((tm, 128), F32), pltpu.VMEM((tm, 512), F32)],
                 dims=("parallel",))(oa, *og_views, *lg_views, g_win, g_dil)
    return res[0], res[1:4], res[4:7]


def _mix_norm_bwd(oa, ob, dmixed, g_win, g_dil):
    s = oa.shape[0]
    tm = _pick(s, (512,))
    nt = s // tm

    def body(oa_ref, ob_ref, dm_ref, gw_ref, gd_ref, doa_ref, db0, db1, db2, dgw_ref, dgd_ref, acc_w, acc_d, scr):
        i = pl.program_id(0)

        @pl.when(i == 0)
        def _():
            acc_w[...] = jnp.zeros_like(acc_w)
            acc_d[...] = jnp.zeros_like(acc_d)

        dm = dm_ref[...]
        dxs = []
        for x_ref, g_ref, dy, acc in ((oa_ref, gw_ref, dm[:, :512], acc_w), (ob_ref, gd_ref, dm[:, 512:], acc_d)):
            x = x_ref[...]
            r = lax.rsqrt(jnp.mean(x * x, axis=-1, keepdims=True) + LN_EPS)
            dyg = dy * g_ref[...]
            dxs.append(r * dyg - x * (r * r * r) * jnp.mean(dyg * x, axis=-1, keepdims=True))
            acc[...] += jnp.sum((dy * x * r).reshape(tm // 8, 8, 512), axis=0)
        doa_ref[...] = dxs[0]
        for q in range(4):
            dq = dxs[1][:, q * 128:(q + 1) * 128]
            for db_ref, d in zip((db0, db1, db2), DILATIONS):
                for c, part in enumerate(_to_classes(dq, scr, d)):
                    db_ref[:, c * 512 + q * 128:c * 512 + (q + 1) * 128] = part

        @pl.when(i == nt - 1)
        def _():
            dgw_ref[...] = jnp.sum(acc_w[...], axis=0, keepdims=True)
            dgd_ref[...] = jnp.sum(acc_d[...], axis=0, keepdims=True)

    row = pl.BlockSpec((tm, 512), lambda i: (i, 0))
    vec = pl.BlockSpec((1, 512), lambda i: (0, 0))
    views = _view_specs(tm, 512)
    view_shapes = tuple(SDS((s // d, d * 512), F32) for d in DILATIONS)
    res = _pcall(body, name="mix_norm_bwd",
                 out_shape=(SDS((s, 512), F32),) + view_shapes + (SDS((1, 512), F32), SDS((1, 512), F32)),
                 grid=(nt,), in_specs=[row, row, pl.BlockSpec((tm, 1024), lambda i: (i, 0)), vec, vec],
                 out_specs=(row,) + views + (vec, vec),
                 scratch_shapes=[pltpu.VMEM((8, 512), F32), pltpu.VMEM((8, 512), F32), pltpu.VMEM((tm, 128), F32)],
                 dims=("arbitrary",))(oa, ob, dmixed, g_win, g_dil)
    return res[0], res[1:4], res[4], res[5]


def _dz_assemble(dq_views, dkv_views, dqa, dkva):
    s = dqa.shape[0]
    tm = _pick(s, (512,))

    def body(q0, q1, q2, kv0, kv1, kv2, qa_ref, kva_ref, o_ref, scr):
        for g, d in enumerate(DILATIONS):
            for kind, (ref, width, base) in enumerate((((q0, q1, q2)[g], 512, 0), ((kv0, kv1, kv2)[g], 1024, 0),
                                                       ((kv0, kv1, kv2)[g], 1024, 512))):
                for q in range(4):
                    src = base + q * 128
                    dst = kind * 1536 + g * 512 + q * 128
                    if d == 1:
                        o_ref[:, dst:dst + 128] = ref[:, src:src + 128]
                    else:
                        parts = [ref[:, c * width + src:c * width + src + 128].astype(F32) for c in range(d)]
                        o_ref[:, dst:dst + 128] = _from_classes(parts, scr).astype(BF16)
        o_ref[:, Z_QA:Z_QA + 512] = qa_ref[...]
        o_ref[:, Z_KA:Z_KA + 256] = kva_ref[...]

    return _pcall(body, name="dz_assemble", out_shape=SDS((s, IN_WIDTH), BF16), grid=(s // tm,),
                  in_specs=[*_view_specs(tm, 512), *_view_specs(tm, 1024), pl.BlockSpec((tm, 512), lambda i: (i, 0)),
                            pl.BlockSpec((tm, 256), lambda i: (i, 0))],
                  out_specs=pl.BlockSpec((tm, IN_WIDTH), lambda i: (i, 0)),
                  scratch_shapes=[pltpu.VMEM((tm, 128), F32)], dims=("parallel",))(*dq_views, *dkv_views, dqa, dkva)


def _ln_fwd(a, r, g, b, ca, name):
    s = a.shape[0]
    tm = _pick(s, (512, 256))
    has_r = r is not None

    def body(*refs):
        a_ref = refs[0]
        r_ref = refs[1] if has_r else None
        g_ref, b_ref, o_ref, ob_ref = refs[1 + has_r:]
        u = a_ref[...] if ca == 1.0 else ca * a_ref[...]
        if has_r:
            u = u + r_ref[...]
        mu = jnp.mean(u, axis=-1, keepdims=True)
        xc = u - mu
        var = jnp.mean(xc * xc, axis=-1, keepdims=True)
        y = xc * lax.rsqrt(var + LN_EPS) * g_ref[...] + b_ref[...]
        o_ref[...] = y
        ob_ref[...] = y.astype(BF16)

    row = pl.BlockSpec((tm, D_MODEL), lambda i: (i, 0))
    vec = pl.BlockSpec((1, D_MODEL), lambda i: (0, 0))
    args = [a] + ([r] if has_r else []) + [g, b]
    return _pcall(body, name=name, out_shape=(SDS((s, D_MODEL), F32), SDS((s, D_MODEL), BF16)), grid=(s // tm,),
                  in_specs=[row] * (1 + has_r) + [vec, vec], out_specs=(row, row), dims=("parallel",))(*args)


def _ln_bwd(a, r, dy, g, b, ca, name, loss_mode=False):
    s = a.shape[0]
    tm = _pick(s, (512, 256))
    nt = s // tm
    has_r = r is not None

    def body(*refs):
        a_ref = refs[0]
        r_ref = refs[1] if has_r else None
        dy_ref, g_ref, b_ref = refs[1 + has_r:4 + has_r]
        outs = refs[4 + has_r:]
        if loss_mode:
            du_ref, dub_ref, dg_ref, db_ref, loss_ref, acc_g, acc_b, acc_l = outs
        else:
            du_ref, dub_ref, dg_ref, db_ref, acc_g, acc_b = outs
        i = pl.program_id(0)

        @pl.when(i == 0)
        def _():
            acc_g[...] = jnp.zeros_like(acc_g)
            acc_b[...] = jnp.zeros_like(acc_b)
            if loss_mode:
                acc_l[...] = jnp.zeros_like(acc_l)

        u = a_ref[...] if ca == 1.0 else ca * a_ref[...]
        if has_r:
            u = u + r_ref[...]
        mu = jnp.mean(u, axis=-1, keepdims=True)
        xc = u - mu
        var = jnp.mean(xc * xc, axis=-1, keepdims=True)
        rstd = lax.rsqrt(var + LN_EPS)
        xhat = xc * rstd
        gv = g_ref[...]
        if loss_mode:
            err = (xhat * gv + b_ref[...]) - dy_ref[...]
            acc_l[...] += jnp.sum((err * err).reshape(tm // 8, 8, D_MODEL), axis=0)
            dyv = err * (1.0 / D_MODEL)
        else:
            dyv = dy_ref[...]
        dxh = dyv * gv
        du = rstd * (dxh - jnp.mean(dxh, axis=-1, keepdims=True) - xhat * jnp.mean(dxh * xhat, axis=-1, keepdims=True))
        du_ref[...] = du
        dub_ref[...] = du.astype(BF16)
        acc_g[...] += jnp.sum((dyv * xhat).reshape(tm // 8, 8, D_MODEL), axis=0)
        acc_b[...] += jnp.sum(dyv.reshape(tm // 8, 8, D_MODEL), axis=0)

        @pl.when(i == nt - 1)
        def _():
            dg_ref[...] = jnp.sum(acc_g[...], axis=0, keepdims=True)
            db_ref[...] = jnp.sum(acc_b[...], axis=0, keepdims=True)
            if loss_mode:
                tot = jnp.sum(jnp.sum(acc_l[...], axis=0, keepdims=True), axis=1, keepdims=True)
                loss_ref[...] = tot * (0.5 / D_MODEL)

    row = pl.BlockSpec((tm, D_MODEL), lambda i: (i, 0))
    vec = pl.BlockSpec((1, D_MODEL), lambda i: (0, 0))
    out_shape = [SDS((s, D_MODEL), F32), SDS((s, D_MODEL), BF16), SDS((1, D_MODEL), F32), SDS((1, D_MODEL), F32)]
    out_specs = [row, row, vec, vec]
    scratch = [pltpu.VMEM((8, D_MODEL), F32), pltpu.VMEM((8, D_MODEL), F32)]
    if loss_mode:
        out_shape.append(SDS((1, 1), F32))
        out_specs.append(pl.BlockSpec((1, 1), lambda i: (0, 0)))
        scratch.append(pltpu.VMEM((8, D_MODEL), F32))
    args = [a] + ([r] if has_r else []) + [dy, g, b]
    return _pcall(body, name=name, out_shape=tuple(out_shape), grid=(nt,), in_specs=[row] * (2 + has_r) + [vec, vec],
                  out_specs=tuple(out_specs), scratch_shapes=scratch, dims=("arbitrary",))(*args)


def _xattn_fwd(q, k, v):
    s = q.shape[0]
    tq = _pick(s, (512,))
    scale = X_HEAD_DIM ** -0.5

    def body(q_ref, k_ref, v_ref, o_ref, ob_ref):
        qv, kv, vv = q_ref[...], k_ref[...], v_ref[...]
        outs = []
        for h in range(X_HEADS):
            sl = slice(h * X_HEAD_DIM, (h + 1) * X_HEAD_DIM)
            sc = lax.dot_general(qv[:, sl], kv[:, sl], _NT, preferred_element_type=F32) * scale
            e = jnp.exp(sc - jnp.max(sc, axis=-1, keepdims=True))
            p = e / jnp.sum(e, axis=-1, keepdims=True)
            outs.append(lax.dot_general(p.astype(BF16), vv[:, sl], _NN, preferred_element_type=F32))
        o = jnp.concatenate(outs, axis=1)
        o_ref[...] = o
        ob_ref[...] = o.astype(BF16)

    row = pl.BlockSpec((tq, D_MODEL), lambda i: (i, 0))
    full = pl.BlockSpec((MEM_LEN, D_MODEL), lambda i: (0, 0))
    return _pcall(body, name="xattn_fwd", out_shape=(SDS((s, D_MODEL), F32), SDS((s, D_MODEL), BF16)), grid=(s // tq,),
                  in_specs=[row, full, full], out_specs=(row, row), dims=("parallel",))(q, k, v)


def _xattn_bwd(q, k, v, o, do):
    s = q.shape[0]
    tq = _pick(s, (512,))
    scale = X_HEAD_DIM ** -0.5

    def body(q_ref, k_ref, v_ref, o_ref, do_ref, dq_ref, dk_ref, dv_ref):
        i = pl.program_id(0)

        @pl.when(i == 0)
        def _():
            dk_ref[...] = jnp.zeros_like(dk_ref)
            dv_ref[...] = jnp.zeros_like(dv_ref)

        qv, kv, vv, ov, dov = q_ref[...], k_ref[...], v_ref[...], o_ref[...], do_ref[...]
        dqs, dks, dvs = [], [], []
        for h in range(X_HEADS):
            sl = slice(h * X_HEAD_DIM, (h + 1) * X_HEAD_DIM)
            sc = lax.dot_general(qv[:, sl], kv[:, sl], _NT, preferred_element_type=F32) * scale
            e = jnp.exp(sc - jnp.max(sc, axis=-1, keepdims=True))
            p = e / jnp.sum(e, axis=-1, keepdims=True)
            doh = dov[:, sl]
            dob = doh.astype(BF16)
            delta = jnp.sum(doh * ov[:, sl], axis=-1, keepdims=True)
            dvs.append(lax.dot_general(p.astype(BF16), dob, _TN, preferred_element_type=F32))
            dp = lax.dot_general(dob, vv[:, sl], _NT, preferred_element_type=F32)
            ds = (p * (dp - delta)).astype(BF16)
            dqs.append(lax.dot_general(ds, kv[:, sl], _NN, preferred_element_type=F32) * scale)
            dks.append(lax.dot_general(ds, qv[:, sl], _TN, preferred_element_type=F32) * scale)
        dq_ref[...] = jnp.concatenate(dqs, axis=1).astype(BF16)
        dk_ref[...] += jnp.concatenate(dks, axis=1)
        dv_ref[...] += jnp.concatenate(dvs, axis=1)

    row = pl.BlockSpec((tq, D_MODEL), lambda i: (i, 0))
    full = pl.BlockSpec((MEM_LEN, D_MODEL), lambda i: (0, 0))
    return _pcall(body, name="xattn_bwd",
                  out_shape=(SDS((s, D_MODEL), BF16), SDS((MEM_LEN, D_MODEL), F32), SDS((MEM_LEN, D_MODEL), F32)),
                  grid=(s // tq,), in_specs=[row, full, full, row, row], out_specs=(row, full, full),
                  dims=("arbitrary",))(q, k, v, o, do)


_SQRT_HALF = 0.7071067811865476
_INV_SQRT_2PI = 0.3989422804014327


def _halo_specs(s, tm, width):
    n8 = s // 8
    r8 = tm // 8
    prev = pl.BlockSpec((8, width), lambda i: (jnp.maximum(i * r8 - 1, 0), 0))
    nxt = pl.BlockSpec((8, width), lambda i: (jnp.minimum((i + 1) * r8, n8 - 1), 0))
    return prev, nxt


def _shifted(x, prev8, next8, i, nt):
    tm = x.shape[0]
    row = lax.broadcasted_iota(jnp.int32, x.shape, 0)
    first = jnp.where(i == 0, 0.0, 1.0) * prev8[7:8, :]
    last = jnp.where(i == nt - 1, 0.0, 1.0) * next8[0:1, :]
    xm1 = jnp.where(row == 0, first, pltpu.roll(x, 1, 0))
    xp1 = jnp.where(row == tm - 1, last, pltpu.roll(x, tm - 1, 0))
    return xm1, xp1


def _glu_fwd(g, up, cw, cb):
    s = g.shape[0]
    tm = _pick(s, (256,))
    nt = s // tm

    def body(g_ref, gp_ref, gn_ref, up_ref, cw_ref, cb_ref, act_ref):
        i = pl.program_id(0)
        gv = g_ref[...]
        gm1, gp1 = _shifted(gv, gp_ref[...], gn_ref[...], i, nt)
        gc = gm1 * cw_ref[0:1, :] + gv * cw_ref[1:2, :] + gp1 * cw_ref[2:3, :] + cb_ref[...]
        gelu = 0.5 * gc * (1.0 + lax.erf(gc * _SQRT_HALF))
        act_ref[...] = (gelu * up_ref[...]).astype(BF16)

    row = pl.BlockSpec((tm, D_FF), lambda i: (i, 0))
    prev, nxt = _halo_specs(s, tm, D_FF)
    return _pcall(body, name="glu_fwd", out_shape=SDS((s, D_FF), BF16), grid=(nt,),
                  in_specs=[row, prev, nxt, row, pl.BlockSpec((8, D_FF), lambda i: (0, 0)),
                            pl.BlockSpec((1, D_FF), lambda i: (0, 0))],
                  out_specs=row, dims=("parallel",))(g, g, g, up, cw, cb)


def _glu_bwd(g, up, dact, cw, cb):
    s = g.shape[0]
    tm = _pick(s, (256,))
    nt = s // tm

    def body(g_ref, gp_ref, gn_ref, up_ref, upp_ref, upn_ref, da_ref, dap_ref, dan_ref, cw_ref, cb_ref,
             dg_ref, dup_ref, dcw_ref, dcb_ref, a0, a1, a2, a3):
        i = pl.program_id(0)

        @pl.when(i == 0)
        def _():
            for a in (a0, a1, a2, a3):
                a[...] = jnp.zeros_like(a)

        cw0, cw1, cw2, cbv = cw_ref[0:1, :], cw_ref[1:2, :], cw_ref[2:3, :], cb_ref[...]

        def d_conv_out(gc_, up_, da_):
            cdf_ = 0.5 * (1.0 + lax.erf(gc_ * _SQRT_HALF))
            pdf_ = jnp.exp(-0.5 * gc_ * gc_) * _INV_SQRT_2PI
            return da_ * up_ * (cdf_ + gc_ * pdf_), cdf_

        gv = g_ref[...]
        g_before, g_after = gp_ref[...], gn_ref[...]
        gm1, gp1 = _shifted(gv, g_before, g_after, i, nt)
        gc = gm1 * cw0 + gv * cw1 + gp1 * cw2 + cbv
        da = da_ref[...]
        dgc, cdf = d_conv_out(gc, up_ref[...], da)
        dup_ref[...] = (da * (gc * cdf)).astype(BF16)
        gc_b = g_before[6:7, :] * cw0 + g_before[7:8, :] * cw1 + gv[0:1, :] * cw2 + cbv
        gc_a = gv[tm - 1:tm, :] * cw0 + g_after[0:1, :] * cw1 + g_after[1:2, :] * cw2 + cbv
        dgc_b = jnp.where(i == 0, 0.0, 1.0) * d_conv_out(gc_b, upp_ref[7:8, :], dap_ref[7:8, :])[0]
        dgc_a = jnp.where(i == nt - 1, 0.0, 1.0) * d_conv_out(gc_a, upn_ref[0:1, :], dan_ref[0:1, :])[0]
        row = lax.broadcasted_iota(jnp.int32, dgc.shape, 0)
        dgc_m1 = jnp.where(row == 0, dgc_b, pltpu.roll(dgc, 1, 0))
        dgc_p1 = jnp.where(row == tm - 1, dgc_a, pltpu.roll(dgc, tm - 1, 0))
        dg_ref[...] = (dgc_p1 * cw0 + dgc * cw1 + dgc_m1 * cw2).astype(BF16)

        def fold(t):
            return jnp.sum(t.reshape(tm // 8, 8, D_FF), axis=0)

        a0[...] += fold(dgc * gm1)
        a1[...] += fold(dgc * gv)
        a2[...] += fold(dgc * gp1)
        a3[...] += fold(dgc)

        @pl.when(i == nt - 1)
        def _():
            dcw_ref[...] = jnp.concatenate(
                [jnp.sum(a[...], axis=0, keepdims=True) for a in (a0, a1, a2)] + [jnp.zeros((5, D_FF), F32)], axis=0)
            dcb_ref[...] = jnp.sum(a3[...], axis=0, keepdims=True)

    row = pl.BlockSpec((tm, D_FF), lambda i: (i, 0))
    prev, nxt = _halo_specs(s, tm, D_FF)
    cw_spec = pl.BlockSpec((8, D_FF), lambda i: (0, 0))
    cb_spec = pl.BlockSpec((1, D_FF), lambda i: (0, 0))
    return _pcall(body, name="glu_bwd",
                  out_shape=(SDS((s, D_FF), BF16), SDS((s, D_FF), BF16), SDS((8, D_FF), F32), SDS((1, D_FF), F32)),
                  grid=(nt,), in_specs=[row, prev, nxt] * 3 + [cw_spec, cb_spec],
                  out_specs=(row, row, cw_spec, cb_spec), scratch_shapes=[pltpu.VMEM((8, D_FF), F32)] * 4,
                  dims=("arbitrary",))(g, g, g, up, up, up, dact, dact, dact, cw, cb)


def _adamw(w, g, m, v, name):
    rows, cols = w.shape
    tr = _pick(rows, (256, 128, 64, 32, 16, 8))
    c1 = 1.0 - ADAM_B1 ** ADAM_STEP
    c2 = 1.0 - ADAM_B2 ** ADAM_STEP

    def body(w_ref, g_ref, m_ref, v_ref, d_ref, nm_ref, nv_ref):
        gv = g_ref[...]
        nm = ADAM_B1 * m_ref[...] + (1.0 - ADAM_B1) * gv
        nv = ADAM_B2 * v_ref[...] + (1.0 - ADAM_B2) * (gv * gv)
        d_ref[...] = -ADAM_LR * ((nm / c1) / (jnp.sqrt(nv / c2) + ADAM_EPS) + ADAM_WD * w_ref[...])
        nm_ref[...] = nm
        nv_ref[...] = nv

    blk = pl.BlockSpec((tr, cols), lambda i: (i, 0))
    return _pcall(body, name=name, out_shape=(SDS(w.shape, F32),) * 3, grid=(rows // tr,), in_specs=[blk] * 4,
                  out_specs=(blk,) * 3, dims=("parallel",))(w, g, m, v)


def _adamw_many(ws, gs, ms, vs, name):
    n = len(ws)
    c1 = 1.0 - ADAM_B1 ** ADAM_STEP
    c2 = 1.0 - ADAM_B2 ** ADAM_STEP

    def body(*refs):
        outs = refs[4 * n:]
        for k in range(n):
            gv = refs[n + k][...]
            nm = ADAM_B1 * refs[2 * n + k][...] + (1.0 - ADAM_B1) * gv
            nv = ADAM_B2 * refs[3 * n + k][...] + (1.0 - ADAM_B2) * (gv * gv)
            outs[k][...] = -ADAM_LR * ((nm / c1) / (jnp.sqrt(nv / c2) + ADAM_EPS) + ADAM_WD * refs[k][...])
            outs[n + k][...] = nm
            outs[2 * n + k][...] = nv

    shapes = tuple(SDS(w.shape, F32) for w in ws)
    res = _pcall(body, name=name, out_shape=shapes * 3)(*ws, *gs, *ms, *vs)
    return res[:n], res[n:2 * n], res[2 * n:]


def _all_gather_rows(x_shard, *, name, in_vmem, sum_rows=False, after=None):
    m_per, n = x_shard.shape
    extra = [] if after is None else [after]

    def body(x_ref, *rest):
        out_ref, rest = rest[len(extra)], rest[len(extra) + 1:]
        if sum_rows:
            sum_ref, send_sems, recv_sems, local_sem = rest
        else:
            send_sems, recv_sems, local_sem = rest
        x, y, c = lax.axis_index("x"), lax.axis_index("y"), lax.axis_index("c")
        me, sibling = (x, y, c), (x, y, 1 - c)
        chips = [(1 - x, y), (x, 1 - y), (1 - x, 1 - y)]

        def rows(px, py, pc):
            return out_ref.at[pl.ds((4 * px + 2 * py + pc) * m_per, m_per), :]

        def copy(k, block, to, src=None):
            return pltpu.make_async_remote_copy(
                src_ref=rows(*block) if src is None else src, dst_ref=rows(*block), send_sem=send_sems.at[k],
                recv_sem=recv_sems.at[k], device_id=to, device_id_type=pl.DeviceIdType.MESH)

        mine = pltpu.make_async_copy(x_ref, rows(*me), local_sem)
        mine.start()
        first = [copy(0, me, sibling, src=x_ref)]
        first += [copy(1 + j, me, (*chip, c), src=x_ref) for j, chip in enumerate(chips)]
        for cp in first:
            cp.start()
        passed = [copy(4 + j, (*chip, c), sibling) for j, chip in enumerate(chips)]
        for j, chip in enumerate(chips):
            copy(1 + j, (*chip, c), me).wait_recv()
            passed[j].start()
        copy(0, sibling, me).wait_recv()
        for j, chip in enumerate(chips):
            copy(4 + j, (*chip, 1 - c), me).wait_recv()
        for cp in first + passed:
            cp.wait_send()
        mine.wait()
        if sum_rows:
            acc = out_ref[0:m_per, :]
            for dev in range(1, N_DEV):
                acc = acc + out_ref[dev * m_per:(dev + 1) * m_per, :]
            sum_ref[...] = acc

    space = pltpu.VMEM if in_vmem else pl.ANY
    out_shape = [SDS((N_DEV * m_per, n), x_shard.dtype)]
    out_specs = [pl.BlockSpec(memory_space=space)]
    if sum_rows:
        out_shape.append(SDS((m_per, n), x_shard.dtype))
        out_specs.append(pl.BlockSpec(memory_space=pltpu.VMEM))
    res = _PALLAS_CALL(
        body, name=name, out_shape=tuple(out_shape),
        in_specs=[pl.BlockSpec(memory_space=space)] + [pl.BlockSpec(memory_space=pl.ANY)] * len(extra),
        out_specs=tuple(out_specs),
        scratch_shapes=[pltpu.SemaphoreType.DMA((7,)), pltpu.SemaphoreType.DMA((7,)), pltpu.SemaphoreType.DMA],
        compiler_params=pltpu.CompilerParams(vmem_limit_bytes=VMEM_LIMIT_BYTES),
    )(x_shard, *extra)
    return res if sum_rows else res[0]


_HBM = pl.BlockSpec(memory_space=pltpu.HBM)
_SEM = pl.BlockSpec(memory_space=pltpu.SEMAPHORE)
_SPLIT_PARAMS = dict(has_side_effects=pltpu.SideEffectType.DATAFLOW_SIDE_EFFECTING)


def _split_copies(src_ref, land_ref, send_sems, recv_sems, gather):
    x, y, c = lax.axis_index("x"), lax.axis_index("y"), lax.axis_index("c")
    copies = []
    for k in range(1, N_DEV):
        px = 1 - x if k & 4 else x
        py = 1 - y if k & 2 else y
        pc = 1 - c if k & 1 else c
        if gather:
            rows = src_ref.shape[0]
            src, dst = src_ref, land_ref.at[pl.ds((4 * x + 2 * y + c) * rows, rows), :]
        else:
            src, dst = src_ref.at[4 * px + 2 * py + pc], land_ref.at[k - 1]
        copies.append(pltpu.make_async_remote_copy(
            src_ref=src, dst_ref=dst, send_sem=send_sems.at[k - 1], recv_sem=recv_sems.at[k - 1],
            device_id=(px, py, pc), device_id_type=pl.DeviceIdType.MESH))
    return copies


def _exchange_start(src, land_shape, *, gather, name):
    def body(src_ref, land_ref, send_sems, recv_sems, src_thru, land_thru, token):
        for cp in _split_copies(src_ref, land_ref, send_sems, recv_sems, gather):
            cp.start()
        token[...] = jnp.zeros_like(token)

    land = pltpu.with_memory_space_constraint(lax.empty(land_shape, src.dtype), pltpu.HBM)
    return _PALLAS_CALL(
        body, name=name,
        out_shape=(pltpu.SemaphoreType.DMA((N_DEV - 1,)), pltpu.SemaphoreType.DMA((N_DEV - 1,)),
                   pltpu.HBM(src.shape, src.dtype), pltpu.HBM(land_shape, src.dtype), SDS((8, 128), F32)),
        in_specs=(_HBM, _HBM), out_specs=(_SEM, _SEM, _HBM, _HBM, pl.BlockSpec(memory_space=pltpu.VMEM)),
        input_output_aliases={0: 2, 1: 3}, compiler_params=pltpu.CompilerParams(**_SPLIT_PARAMS),
    )(pltpu.with_memory_space_constraint(src, pltpu.HBM), land)


def _exchange_wait(started, after, *, gather, name):
    send_sems, recv_sems, src_thru, land_thru, _ = started

    def body(src_ref, land_ref, send_sems, recv_sems, after_ref, src_out, land_out):
        copies = _split_copies(src_ref, land_ref, send_sems, recv_sems, gather)
        for cp in copies:
            cp.wait_send()
        for cp in copies:
            cp.wait_recv()

    return _PALLAS_CALL(
        body, name=name,
        out_shape=(pltpu.HBM(src_thru.shape, src_thru.dtype), pltpu.HBM(land_thru.shape, land_thru.dtype)),
        in_specs=(_HBM, _HBM, _SEM, _SEM, pl.BlockSpec(memory_space=pl.ANY)), out_specs=(_HBM, _HBM),
        input_output_aliases={0: 0, 1: 1}, compiler_params=pltpu.CompilerParams(**_SPLIT_PARAMS),
    )(src_thru, land_thru, send_sems, recv_sems, after)


def _sum_parts(own, land, name):
    r, n = own.shape
    tr = _pick(r, (264, 320, 336, 128, 64, 32, 16, 8))

    def body(own_ref, x_ref, o_ref):
        acc = own_ref[...]
        for k in range(N_DEV - 1):
            acc = acc + x_ref[k]
        o_ref[...] = acc

    return _pcall(body, name=name, out_shape=SDS((r, n), F32), grid=(r // tr,),
                  in_specs=[pl.BlockSpec((tr, n), lambda i: (i, 0)), pl.BlockSpec((N_DEV - 1, tr, n), lambda i: (0, i, 0))],
                  out_specs=pl.BlockSpec((tr, n), lambda i: (i, 0)), dims=("parallel",))(own, land)


def _pad_rows(a, rows):
    return jnp.pad(a, ((0, rows - a.shape[0]), (0, 0)))


def kernel(x, mem, positions, ln_in_g, ln_in_b, w_in, attn_sink, g_win, g_dil, w_mix_out, ln1_g, ln1_b, mem_ln_g, mem_ln_b, w_xq, w_xk, w_xv, w_xo, ln2_g, ln2_b, w_gate, w_up, conv_w, conv_b, w_down, ln3_g, ln3_b, loss_target, m_ln_in_g, m_ln_in_b, m_w_in, m_attn_sink, m_g_win, m_g_dil, m_w_mix_out, m_ln1_g, m_ln1_b, m_mem_ln_g, m_mem_ln_b, m_w_xq, m_w_xk, m_w_xv, m_w_xo, m_ln2_g, m_ln2_b, m_w_gate, m_w_up, m_conv_w, m_conv_b, m_w_down, m_ln3_g, m_ln3_b, v_ln_in_g, v_ln_in_b, v_w_in, v_attn_sink, v_g_win, v_g_dil, v_w_mix_out, v_ln1_g, v_ln1_b, v_mem_ln_g, v_mem_ln_b, v_w_xq, v_w_xk, v_w_xv, v_w_xo, v_ln2_g, v_ln2_b, v_w_gate, v_w_up, v_conv_w, v_conv_b, v_w_down, v_ln3_g, v_ln3_b):
    weights = dict(ln_in_g=ln_in_g, ln_in_b=ln_in_b, w_in=w_in, attn_sink=attn_sink, g_win=g_win, g_dil=g_dil, w_mix_out=w_mix_out, ln1_g=ln1_g, ln1_b=ln1_b, mem_ln_g=mem_ln_g, mem_ln_b=mem_ln_b, w_xq=w_xq, w_xk=w_xk, w_xv=w_xv, w_xo=w_xo, ln2_g=ln2_g, ln2_b=ln2_b, w_gate=w_gate, w_up=w_up, conv_w=conv_w, conv_b=conv_b, w_down=w_down, ln3_g=ln3_g, ln3_b=ln3_b)
    mom_m = dict(ln_in_g=m_ln_in_g, ln_in_b=m_ln_in_b, w_in=m_w_in, attn_sink=m_attn_sink, g_win=m_g_win, g_dil=m_g_dil, w_mix_out=m_w_mix_out, ln1_g=m_ln1_g, ln1_b=m_ln1_b, mem_ln_g=m_mem_ln_g, mem_ln_b=m_mem_ln_b, w_xq=m_w_xq, w_xk=m_w_xk, w_xv=m_w_xv, w_xo=m_w_xo, ln2_g=m_ln2_g, ln2_b=m_ln2_b, w_gate=m_w_gate, w_up=m_w_up, conv_w=m_conv_w, conv_b=m_conv_b, w_down=m_w_down, ln3_g=m_ln3_g, ln3_b=m_ln3_b)
    mom_v = dict(ln_in_g=v_ln_in_g, ln_in_b=v_ln_in_b, w_in=v_w_in, attn_sink=v_attn_sink, g_win=v_g_win, g_dil=v_g_dil, w_mix_out=v_w_mix_out, ln1_g=v_ln1_g, ln1_b=v_ln1_b, mem_ln_g=v_mem_ln_g, mem_ln_b=v_mem_ln_b, w_xq=v_w_xq, w_xk=v_w_xk, w_xv=v_w_xv, w_xo=v_w_xo, ln2_g=v_ln2_g, ln2_b=v_ln2_b, w_gate=v_w_gate, w_up=v_w_up, conv_w=v_conv_w, conv_b=v_conv_b, w_down=v_w_down, ln3_g=v_ln3_g, ln3_b=v_ln3_b)
    order = list(weights)
    s = x.shape[1]
    xs = x[0]
    mems = mem[0]
    target = loss_target[0]
    row = lambda a: a.reshape(1, -1)

    shard_rows = dict(w_in=w_in[0].T, w_gate=w_gate[0].T, w_up=w_up[0].T, w_mix_out=w_mix_out[0], w_xq=w_xq[0],
                      w_xk=w_xk[0], w_xv=w_xv[0], w_xo=w_xo[0], w_down=w_down[0])
    me_lin = 4 * lax.axis_index("x") + 2 * lax.axis_index("y") + lax.axis_index("c")
    w_in_full = _all_gather_rows(shard_rows["w_in"].astype(BF16), name="w_in_all_gather", in_vmem=False)
    w_in_t = jnp.concatenate([w_in_full[768:], w_in_full[:768]], axis=0)
    late_rows = PACK_ROWS[1:]
    late_r = sum(r for _, r in late_rows)
    packed = jnp.concatenate([shard_rows[n].astype(BF16) for n, _ in late_rows], axis=0)
    w_started = _exchange_start(packed, (N_DEV * late_r, D_MODEL), gather=True, name="weight_gather_start")
    cw_pad = jnp.pad(conv_w[0], ((0, 5), (0, 32)))
    cw_all = _all_gather_rows(cw_pad, name="conv_w_all_gather", in_vmem=True).reshape(N_DEV, 8, 384)
    cw_full = jnp.transpose(cw_all[:, :3, :352], (1, 0, 2)).reshape(3, D_FF)
    cw8 = _pad_rows(cw_full, 8)

    tabs = _rope_tables(positions.astype(F32).reshape(s, 1) + w_started[4][0, 0])
    h0, h0b = _ln_fwd(xs, None, row(ln_in_g), row(ln_in_b), 1.0, "ln_in_fwd")
    zw, *zg = _proj_rope(h0b, w_in_t, tabs[0])
    oa, lse_a = _banded_fwd(zw, attn_sink, name="win_attn_fwd", **_WIN_CFG)
    og_views, lg_views = [], []
    for gi in range(3):
        o_g, l_g = _banded_fwd(zg[gi], None, name=f"dil_attn_fwd{gi}", **_dil_cfg(gi))
        og_views.append(o_g)
        lg_views.append(l_g)
    mixed, ob_views, lb_views = _mix_norm_fwd(oa, og_views, lg_views, g_win, g_dil)
    packed_thru, land = _exchange_wait(w_started, mixed, gather=True, name="weight_gather_wait")
    gathered = lax.dynamic_update_slice(land, packed_thru, (me_lin * late_r, 0)).reshape(N_DEV, late_r, D_MODEL)
    full = {}
    off = 0
    for n, r in late_rows:
        full[n] = gathered[:, off:off + r, :].reshape(N_DEV * r, D_MODEL)
        off += r
    mix = _mm(mixed, full["w_mix_out"], trans_b=False, out_dtype=F32, name="mm_mix_out")
    h1, h1b = _ln_fwd(h0, mix, ln1_g, ln1_b, ALPHA, "ln1_fwd")
    _, mem_nb = _ln_fwd(mems, None, mem_ln_g, mem_ln_b, 1.0, "mem_ln_fwd")
    kx = _mm(mem_nb, full["w_xk"], trans_b=False, out_dtype=BF16, name="mm_xk")
    vx = _mm(mem_nb, full["w_xv"], trans_b=False, out_dtype=BF16, name="mm_xv")
    qx = _mm(h1b, full["w_xq"], trans_b=False, out_dtype=BF16, name="mm_xq")
    ox, oxb = _xattn_fwd(qx, kx, vx)
    xa = _mm(oxb, full["w_xo"], trans_b=False, out_dtype=F32, name="mm_xo")
    h2, h2b = _ln_fwd(h1, xa, ln2_g, ln2_b, ALPHA, "ln2_fwd")
    gate = _mm(h2b, full["w_gate"], trans_b=True, out_dtype=F32, name="mm_gate")
    up = _mm(h2b, full["w_up"], trans_b=True, out_dtype=F32, name="mm_up")
    act = _glu_fwd(gate, up, cw8, conv_b)
    ff = _mm(act, full["w_down"], trans_b=False, out_dtype=F32, name="mm_down")

    du3, du3b, d_ln3_g, d_ln3_b, loss_local = _ln_bwd(h2, ff, target, ln3_g, ln3_b, ALPHA, "ln3_bwd_loss",
                                                      loss_mode=True)
    dact = _mm(du3b, full["w_down"], trans_b=True, out_dtype=F32, name="mm_d_act")
    dw_down = _mm_tn(act, du3b, name="mm_dw_down")
    dgate, dup, dcw8, d_conv_b = _glu_bwd(gate, up, dact, cw8, conv_b)
    dh2 = _mm(dgate, full["w_gate"], trans_b=False, out_dtype=F32, name="mm_dh2", more=((dup, full["w_up"], False),),
              addends=(du3,), coefs=(ALPHA,))
    dw_gate_t = _mm_tn(dgate, h2b, name="mm_dw_gate")
    dw_up_t = _mm_tn(dup, h2b, name="mm_dw_up")
    rows_of = dict(PACK_ROWS)

    def start_grad_exchange(parts, name):
        gp = jnp.concatenate([g.reshape(N_DEV, rows_of[n], D_MODEL) for n, g in parts], axis=1)
        return _exchange_start(gp, (N_DEV - 1,) + gp.shape[1:], gather=False, name=name)

    ffn_parts = (("w_gate", dw_gate_t), ("w_up", dw_up_t), ("w_down", dw_down))
    ffn_started = start_grad_exchange(ffn_parts, "grad_start_ffn")
    du2, du2b, d_ln2_g, d_ln2_b = _ln_bwd(h1, xa, dh2, ln2_g + ffn_started[4][0, 0], ln2_b, ALPHA, "ln2_bwd")
    dox = _mm(du2b, full["w_xo"], trans_b=True, out_dtype=F32, name="mm_d_ox")
    dw_xo = _mm_tn(oxb, du2b, name="mm_dw_xo")
    dqx, dkx, dvx = _xattn_bwd(qx, kx, vx, ox, dox)
    dh1 = _mm(dqx, full["w_xq"], trans_b=True, out_dtype=F32, name="mm_dh1", addends=(du2,), coefs=(ALPHA,))
    dw_xq = _mm_tn(h1b, dqx, name="mm_dw_xq")
    dw_xk = _mm_tn(mem_nb, dkx, name="mm_dw_xk")
    dw_xv = _mm_tn(mem_nb, dvx, name="mm_dw_xv")
    dmem_n = _mm(dkx, full["w_xk"], trans_b=True, out_dtype=F32, name="mm_dmem", more=((dvx, full["w_xv"], True),))
    _, _, d_mem_ln_g, d_mem_ln_b = _ln_bwd(mems, None, dmem_n, mem_ln_g, mem_ln_b, 1.0, "mem_ln_bwd")
    du1, du1b, d_ln1_g, d_ln1_b = _ln_bwd(h0, mix, dh1, ln1_g, ln1_b, ALPHA, "ln1_bwd")
    dmixed = _mm(du1b, full["w_mix_out"], trans_b=True, out_dtype=F32, name="mm_d_mixed")
    dw_mix_out = _mm_tn(mixed, du1b, name="mm_dw_mix_out")
    attn_parts = (("w_mix_out", dw_mix_out), ("w_xq", dw_xq), ("w_xk", dw_xk), ("w_xv", dw_xv), ("w_xo", dw_xo))
    attn_started = start_grad_exchange(attn_parts, "grad_start_attn")
    doa, dob_views, d_g_win, d_g_dil = _mix_norm_bwd(oa, ob_views[0], dmixed, g_win + attn_started[4][0, 0], g_dil)
    dqa, dkva, dsink8 = _banded_bwd(zw, oa, lse_a, doa, tabs[0], attn_sink, kv_heads=_WIN_KV, name="win_attn_bwd",
                                    **_WIN_CFG)
    dq_views, dkv_views = [], []
    for gi in range(3):
        dq_g, dkv_g = _banded_bwd(zg[gi], ob_views[gi], lb_views[gi], dob_views[gi], tabs[gi], None, kv_heads=_DIL_KV,
                                  name=f"dil_attn_bwd{gi}", **_dil_cfg(gi))
        dq_views.append(dq_g)
        dkv_views.append(dkv_g)
    dz = _dz_assemble(dq_views, dkv_views, dqa, dkva)
    dw_in_tz = _mm_tn(dz, h0b, name="mm_dw_in")
    dw_in_t = jnp.concatenate([dw_in_tz[4608:], dw_in_tz[:4608]], axis=0)
    in_parts = (("w_in", dw_in_t),)
    in_started = start_grad_exchange(in_parts, "grad_start_in")
    dh0 = _mm(dz, w_in_t, trans_b=False, out_dtype=F32, name="mm_dh0", addends=(du1,), coefs=(ALPHA,),
              after=in_started[4])
    dx, _, d_ln_in_g, d_ln_in_b = _ln_bwd(xs, None, dh0, row(ln_in_g), row(ln_in_b), 1.0, "ln_in_bwd")

    grads, delta, new_m, new_v = {}, {}, {}, {}
    after = dx
    for parts, started, tag in ((ffn_parts, ffn_started, "ffn"), (attn_parts, attn_started, "attn"),
                                (in_parts, in_started, "in")):
        gp_thru, land = _exchange_wait(started, after, gather=False, name=f"grad_wait_{tag}")
        own = lax.dynamic_index_in_dim(gp_thru, me_lin, axis=0, keepdims=False)
        gsum = _sum_parts(own, land, f"grad_sum_{tag}")
        off = 0
        for n, _ in parts:
            blk = gsum[off:off + rows_of[n]]
            off += rows_of[n]
            grads[n] = (blk.T if n in ("w_in", "w_gate", "w_up") else blk)[None]
            shp = weights[n].shape
            d_, m_, v_ = _adamw(weights[n].reshape(shp[1:]), grads[n].reshape(shp[1:]), mom_m[n].reshape(shp[1:]),
                                mom_v[n].reshape(shp[1:]), f"adamw_{n}")
            delta[n], new_m[n], new_v[n] = d_.reshape(shp), m_.reshape(shp), v_.reshape(shp)
            after = d_

    small = jnp.concatenate([
        d_ln_in_g, d_ln_in_b, d_ln1_g, d_ln1_b, d_mem_ln_g, d_mem_ln_b, d_ln2_g, d_ln2_b, d_ln3_g, d_ln3_b,
        jnp.concatenate([d_g_win, d_g_dil], axis=1),
        jnp.pad(d_conv_b, ((0, 0), (0, 3072 - D_FF))).reshape(3, 1024),
        jnp.pad(dsink8[0:1, :], ((0, 0), (0, 1024 - 128))),
        jnp.pad(dcw8[0:3], ((0, 0), (0, 3072 - D_FF))).reshape(9, 1024),
    ], axis=0)
    _, ssum = _all_gather_rows(small, name="small_grad_all_reduce", in_vmem=True, sum_rows=True, after=after)
    names10 = ["ln_in_g", "ln_in_b", "ln1_g", "ln1_b", "mem_ln_g", "mem_ln_b", "ln2_g", "ln2_b", "ln3_g", "ln3_b"]
    small_g = {n: ssum[i:i + 1] for i, n in enumerate(names10)}
    small_g["g_win"] = ssum[10:11, :512]
    small_g["g_dil"] = ssum[10:11, 512:]
    small_g["conv_b"] = ssum[11:14].reshape(1, 3072)[:, :D_FF]
    small_g["attn_sink"] = ssum[14:15, :8]
    small_g["conv_w"] = lax.dynamic_slice_in_dim(ssum[15:24].reshape(3, 3072)[:, :D_FF], me_lin * 352, 352, axis=1)

    small_names = [n for n in order if n not in rows_of]
    two_d = lambda a: a.reshape(-1, a.shape[-1])
    d_s, m_s, v_s = _adamw_many([two_d(weights[n]) for n in small_names], [small_g[n] for n in small_names],
                                [two_d(mom_m[n]) for n in small_names], [two_d(mom_v[n]) for n in small_names],
                                "adamw_small")
    for k, n in enumerate(small_names):
        shp = weights[n].shape
        grads[n], delta[n], new_m[n], new_v[n] = (t.reshape(shp) for t in (small_g[n], d_s[k], m_s[k], v_s[k]))

    loss = lax.psum(loss_local[0, 0], MESH_AXES)
    return (loss, dx[None], *[grads[n] for n in order], *[delta[n] for n in order], *[new_m[n] for n in order],
            *[new_v[n] for n in order])
```

```python
import functools
import math

import jax
import jax.numpy as jnp
from jax import lax
from jax.experimental import pallas as pl
from jax.experimental.pallas import tpu as pltpu

F32 = jnp.float32
BF16 = jnp.bfloat16
SDS = jax.ShapeDtypeStruct
_PALLAS_CALL = pl.pallas_call

D_MODEL = 1024
HEAD_DIM = 64
WIN_HALF = 128
DIL_PAIRS = ((128, 1), (512, 4), (2048, 16))
DIL_SIDE = 64
ROT_DIM = 16
ROPE_THETA = 500000.0
MEM_LEN = 256
X_HEADS = 4
X_HEAD_DIM = 256
D_FF = 2816
IN_WIDTH = 5376
Z_QB, Z_KB, Z_VB, Z_QA, Z_KA, Z_VA = 0, 1536, 3072, 4608, 5120, 5248
ALPHA = (2.0) ** 0.25
LN_EPS = 1e-5
NEG_INF = -1e30
ADAM_LR, ADAM_B1, ADAM_B2, ADAM_EPS, ADAM_WD, ADAM_STEP = 0.001, 0.9, 0.999, 1e-08, 0.01, 10
N_DEV = 8
MESH_AXES = ("x", "y", "c")
VMEM_LIMIT_BYTES = 52 * 1024 * 1024
ATTN_TQ = 256
TABW = 384

PACK_ROWS = (("w_in", 672), ("w_gate", 352), ("w_up", 352), ("w_mix_out", 128), ("w_xq", 128), ("w_xk", 128),
             ("w_xv", 128), ("w_xo", 128), ("w_down", 352))
SMALL_ROWS = 24


def _pick(n, cands):
    for c in cands:
        if n % c == 0:
            return c
    return n


def _pcall(body, *, name, out_shape, grid=None, in_specs=None, out_specs=None, scratch_shapes=(), dims=None,
           aliases=None):
    kw = {}
    if grid is not None:
        kw["grid"] = grid
    if in_specs is not None:
        kw["in_specs"] = in_specs
    if out_specs is not None:
        kw["out_specs"] = out_specs
    if aliases:
        kw["input_output_aliases"] = aliases
    return _PALLAS_CALL(
        body, name=name, out_shape=out_shape, scratch_shapes=list(scratch_shapes),
        compiler_params=pltpu.CompilerParams(dimension_semantics=dims, vmem_limit_bytes=VMEM_LIMIT_BYTES), **kw)


MM_VMEM_BUDGET = 40 * 1024 * 1024


def _mm(a, b, *, trans_b, out_dtype, name, addends=(), coefs=(), after=None, more=()):
    pairs = ((a, b, trans_b),) + tuple(more)
    m = a.shape[0]
    n = b.shape[0] if trans_b else b.shape[1]
    n_add = len(addends)
    extra = [] if after is None else [after]
    out_bytes = jnp.dtype(out_dtype).itemsize

    def vmem(tm, tn):
        tot = tm * tn * (out_bytes + 4 * n_add)
        for pa, pb, _ in pairs:
            tot += tm * pa.shape[1] * pa.dtype.itemsize + pa.shape[1] * tn * pb.dtype.itemsize
        return 2 * tot

    tm, tn = next(((cm, cn) for cn in (n, 1408, 1024, 512, 256, 128) if n % cn == 0
                   for cm in (1024, 512, 256, 128) if m % cm == 0 and vmem(cm, cn) <= MM_VMEM_BUDGET))
    n_pairs = len(pairs)

    def body(*refs):
        o_ref = refs[2 * n_pairs + n_add + len(extra)]
        acc = None
        for p, (_, _, tb) in enumerate(pairs):
            dn = _NT if tb else _NN
            part = lax.dot_general(refs[2 * p][...].astype(BF16), refs[2 * p + 1][...].astype(BF16), dn,
                                   preferred_element_type=F32)
            acc = part if acc is None else acc + part
        for r_ref, c in zip(refs[2 * n_pairs:2 * n_pairs + n_add], coefs):
            acc = acc + (r_ref[...] if c == 1.0 else c * r_ref[...])
        o_ref[...] = acc.astype(out_dtype)

    in_specs, args = [], []
    for pa, pb, tb in pairs:
        k = pa.shape[1]
        in_specs.append(pl.BlockSpec((tm, k), lambda j, i: (i, 0)))
        in_specs.append(pl.BlockSpec((tn, k), lambda j, i: (j, 0)) if tb else pl.BlockSpec((k, tn), lambda j, i: (0, j)))
        args += [pa, pb]
    in_specs += [pl.BlockSpec((tm, tn), lambda j, i: (i, j)) for _ in addends]
    in_specs += [pl.BlockSpec((8, 128), lambda j, i: (0, 0)) for _ in extra]
    return _pcall(body, name=name, out_shape=SDS((m, n), out_dtype), grid=(n // tn, m // tm), in_specs=in_specs,
                  out_specs=pl.BlockSpec((tm, tn), lambda j, i: (i, j)),
                  dims=("parallel", "parallel"))(*args, *addends, *extra)


def _mm_tn(a, b, *, name):
    s, m = a.shape
    n = b.shape[1]
    tm = _pick(m, (768, 1408, 1024, 512, 256, 128))
    tk = _pick(s, (1024, 512, 256))
    nk = s // tk

    def body(a_ref, b_ref, o_ref, acc_ref):
        kk = pl.program_id(1)

        @pl.when(kk == 0)
        def _():
            acc_ref[...] = jnp.zeros_like(acc_ref)

        acc_ref[...] += lax.dot_general(a_ref[...].astype(BF16), b_ref[...].astype(BF16), (((0,), (0,)), ((), ())),
                                        preferred_element_type=F32)

        @pl.when(kk == nk - 1)
        def _():
            o_ref[...] = acc_ref[...]

    return _pcall(body, name=name, out_shape=SDS((m, n), F32), grid=(m // tm, nk),
                  in_specs=[pl.BlockSpec((tk, tm), lambda i, kk: (kk, i)), pl.BlockSpec((tk, n), lambda i, kk: (kk, 0))],
                  out_specs=pl.BlockSpec((tm, n), lambda i, kk: (i, 0)), scratch_shapes=[pltpu.VMEM((tm, n), F32)],
                  dims=("parallel", "arbitrary"))(a, b)


def _rope_lane_consts():
    lane = jnp.arange(128)
    j = lane % HEAD_DIM
    inv_freq = ROPE_THETA ** (-jnp.arange(0, ROT_DIM, 2, dtype=F32) / ROT_DIM)
    freq = jnp.where(j < ROT_DIM, inv_freq[j % (ROT_DIM // 2)], 0.0).astype(F32)
    lo = (j < ROT_DIM // 2).astype(F32)
    hi = ((j >= ROT_DIM // 2) & (j < ROT_DIM)).astype(F32)
    return jnp.stack([freq, lo, hi] + [jnp.zeros((128,), F32)] * 5)


def _to_classes(x, scr, d):
    if d == 1:
        return [x]
    scr[...] = x
    return [scr[pl.ds(c, x.shape[0] // d, stride=d), :] for c in range(d)]


def _from_classes(parts, scr):
    d = len(parts)
    if d == 1:
        return parts[0]
    for c, part in enumerate(parts):
        scr[pl.ds(c, part.shape[0], stride=d), :] = part
    return scr[...]


DILATIONS = tuple(d for _, d in DIL_PAIRS)


def _rope_tables(posf):
    s = posf.shape[0]
    tm = _pick(s, (1024, 512))

    def body(p_ref, c_ref, *rest):
        o_refs, scr = rest[:-1], rest[-1]
        ang = p_ref[...] * c_ref[0:1, :]
        lo = c_ref[1:2, :]
        hi = c_ref[2:3, :]
        cs = jnp.cos(ang)
        sn = jnp.sin(ang)
        for q, t in enumerate((jnp.where(lo + hi > 0.0, cs, 1.0), -sn * lo, sn * hi)):
            for o_ref, d in zip(o_refs, DILATIONS):
                for c, part in enumerate(_to_classes(t, scr, d)):
                    o_ref[:, c * TABW + q * 128:c * TABW + (q + 1) * 128] = part

    return _pcall(body, name="rope_tables", out_shape=tuple(SDS((s // d, d * TABW), F32) for d in DILATIONS),
                  grid=(s // tm,),
                  in_specs=[pl.BlockSpec((tm, 1), lambda i: (i, 0)), pl.BlockSpec((8, 128), lambda i: (0, 0))],
                  out_specs=tuple(pl.BlockSpec((tm // d, d * TABW), lambda i: (i, 0)) for d in DILATIONS),
                  scratch_shapes=[pltpu.VMEM((tm, 128), F32)], dims=("parallel",))(posf, _rope_lane_consts())


def _rope_apply(x, tab, sign):
    w = x.shape[1]
    rep = w // 128
    c = jnp.tile(tab[:, 0:128], (1, rep)) if rep > 1 else tab[:, 0:128]
    a = jnp.tile(tab[:, 128:256], (1, rep)) if rep > 1 else tab[:, 128:256]
    b = jnp.tile(tab[:, 256:384], (1, rep)) if rep > 1 else tab[:, 256:384]
    up = pltpu.roll(x, w - 8, 1)
    dn = pltpu.roll(x, 8, 1)
    if sign > 0:
        return x * c + up * a + dn * b
    return x * c - up * a - dn * b


def _proj_rope(h0b, w_t, tab):
    s = h0b.shape[0]
    tm = _pick(s, (512,))
    tn = 256

    def body(a_ref, w_ref, t_ref, zw_ref, z0_ref, z1_ref, z2_ref, scr):
        z_refs = (z0_ref, z1_ref, z2_ref)
        a = a_ref[...]
        tabv = t_ref[...]
        for c0 in range(0, IN_WIDTH, tn):
            z = lax.dot_general(a, w_ref[c0:c0 + tn, :], _NT, preferred_element_type=F32)
            for g0 in range(c0, c0 + tn, 128):
                zg = z[:, g0 - c0:g0 - c0 + 128]
                if g0 < Z_VB or Z_QA <= g0 < Z_VA:
                    zg = _rope_apply(zg, tabv, 1)
                if g0 >= Z_QA:
                    zw_ref[:, g0 - Z_QA:g0 - Z_QA + 128] = zg.astype(BF16)
                    continue
                kind, within = divmod(g0, 1536)
                grp, off = divmod(within, 512)
                col = kind * 512 + off
                for c, part in enumerate(_to_classes(zg, scr, DILATIONS[grp])):
                    z_refs[grp][:, c * 1536 + col:c * 1536 + col + 128] = part.astype(BF16)

    return _pcall(body, name="proj_rope",
                  out_shape=(SDS((s, 768), BF16),) + tuple(SDS((s // d, d * 1536), BF16) for d in DILATIONS),
                  grid=(s // tm,),
                  in_specs=[pl.BlockSpec((tm, D_MODEL), lambda i: (i, 0)), pl.BlockSpec((IN_WIDTH, D_MODEL), lambda i: (0, 0)),
                            pl.BlockSpec((tm, TABW), lambda i: (i, 0))],
                  out_specs=(pl.BlockSpec((tm, 768), lambda i: (i, 0)),)
                  + tuple(pl.BlockSpec((tm // d, d * 1536), lambda i: (i, 0)) for d in DILATIONS),
                  scratch_shapes=[pltpu.VMEM((tm, 128), F32)], dims=("parallel",))(h0b, w_t, tab)


def _band_specs(sd, blk, tq, width, per_tok, cb):
    r = tq // blk
    nbk = sd // blk
    prev = pl.BlockSpec((blk, width), lambda c, j: (jnp.maximum(j * r - 1, 0), c * per_tok + cb))
    cur = pl.BlockSpec((tq, width), lambda c, j: (j, c * per_tok + cb))
    nxt = pl.BlockSpec((blk, width), lambda c, j: (jnp.minimum((j + 1) * r, nbk - 1), c * per_tok + cb))
    return [prev, cur, nxt]


def _band_bias(j, blk, tq, sd):
    shape = (tq, tq + 2 * blk)
    qpos = j * tq + lax.broadcasted_iota(jnp.int32, shape, 0)
    kpos = j * tq - blk + lax.broadcasted_iota(jnp.int32, shape, 1)
    ok = (jnp.abs(qpos - kpos) <= blk) & (kpos >= 0) & (kpos < sd)
    return jnp.where(ok, 0.0, NEG_INF)


_NT = (((1,), (1,)), ((), ()))
_NN = (((1,), (0,)), ((), ()))
_TN = (((0,), (0,)), ((), ()))


def _banded_fwd(zv, sink, *, d, blk, tq, ptw, qw, kw, qcb, kcb, vcb, heads, name):
    sd = zv.shape[0]
    tq = min(tq, sd)
    has_sink = sink is not None
    scale = HEAD_DIM ** -0.5

    def body(q_ref, kp, kc, kn, vp, vc, vn, *rest):
        if has_sink:
            sink_ref, o_ref, lse_ref = rest
        else:
            o_ref, lse_ref = rest
        j = pl.program_id(1)
        q = q_ref[...] * scale
        k = jnp.concatenate([kp[...], kc[...], kn[...]], axis=0)
        v = jnp.concatenate([vp[...], vc[...], vn[...]], axis=0)
        bias = _band_bias(j, blk, tq, sd)
        outs, lses = [], []
        for ql, kl, vl, si in heads:
            sc = lax.dot_general(q[:, ql:ql + HEAD_DIM], k[:, kl:kl + HEAD_DIM], _NT, preferred_element_type=F32) + bias
            m = jnp.max(sc, axis=-1, keepdims=True)
            if has_sink:
                m = jnp.maximum(m, sink_ref[0, si])
            p = jnp.exp(sc - m)
            den = jnp.sum(p, axis=-1, keepdims=True)
            if has_sink:
                den = den + jnp.exp(sink_ref[0, si] - m)
            o = lax.dot_general(p.astype(BF16), v[:, vl:vl + HEAD_DIM], _NN, preferred_element_type=F32) / den
            outs.append(o)
            lses.append(jnp.broadcast_to(m + jnp.log(den), (tq, HEAD_DIM)))
        o_ref[...] = jnp.concatenate(outs, axis=1)
        lse_ref[...] = jnp.concatenate(lses, axis=1)

    in_specs = ([pl.BlockSpec((tq, qw), lambda c, j: (j, c * (ptw // qw) + qcb))]
                + _band_specs(sd, blk, tq, kw, ptw // kw, kcb) + _band_specs(sd, blk, tq, kw, ptw // kw, vcb))
    args = [zv] * 7
    if has_sink:
        in_specs.append(pl.BlockSpec(memory_space=pltpu.SMEM))
        args.append(sink)
    o_spec = pl.BlockSpec((tq, qw), lambda c, j: (j, c))
    return _pcall(body, name=name, out_shape=(SDS((sd, d * qw), F32), SDS((sd, d * qw), F32)), grid=(d, sd // tq),
                  in_specs=in_specs, out_specs=(o_spec, o_spec), dims=("parallel", "parallel"))(*args)


def _banded_bwd(zv, ov, lv, dov, tv, sink, *, d, blk, tq, ptw, qw, kw, qcb, kcb, vcb, heads, kv_heads, name):
    sd = zv.shape[0]
    tq = min(tq, sd)
    nt = sd // tq
    r = tq // blk
    nbk = sd // blk
    has_sink = sink is not None
    scale = HEAD_DIM ** -0.5
    kvw = HEAD_DIM * len(kv_heads)

    def add_rows(x, y, last):
        if tq == blk:
            return x + y
        if last:
            return jnp.concatenate([x[:tq - blk], x[tq - blk:] + y], axis=0)
        return jnp.concatenate([x[:blk] + y, x[blk:]], axis=0)

    def body(q_ref, kp, kc, kn, vp, vc, vn, o_ref, l_ref, do_ref, t_ref, tlag_ref, *rest):
        if has_sink:
            sink_ref, dq_ref, dkv_ref, dsink_ref, acck, accv, nxtk, nxtv = rest
        else:
            dq_ref, dkv_ref, acck, accv, nxtk, nxtv = rest
        j = pl.program_id(1)

        @pl.when(j == 0)
        def _():
            nxtk[...] = jnp.zeros_like(nxtk)
            nxtv[...] = jnp.zeros_like(nxtv)

        if has_sink:
            @pl.when((pl.program_id(0) == 0) & (j == 0))
            def _():
                dsink_ref[...] = jnp.zeros_like(dsink_ref)

        def emit(dk_rows, dv_rows):
            dkv_ref[...] = jnp.concatenate([_rope_apply(dk_rows, tlag_ref[...], -1), dv_rows], axis=1).astype(BF16)

        @pl.when(j < nt)
        def _():
            q = q_ref[...] * scale
            k3 = jnp.concatenate([kp[...], kc[...], kn[...]], axis=0)
            v3 = jnp.concatenate([vp[...], vc[...], vn[...]], axis=0)
            o_t, l_t, do_t = o_ref[...], l_ref[...], do_ref[...]
            bias = _band_bias(j, blk, tq, sd)
            dqs = []
            dks = [None] * len(kv_heads)
            dvs = [None] * len(kv_heads)
            dsink_row = jnp.zeros((1, 128), F32)
            lane = lax.broadcasted_iota(jnp.int32, (1, 128), 1)
            for ql, kl, vl, si in heads:
                kvi = kv_heads.index((kl, vl))
                qh = q[:, ql:ql + HEAD_DIM]
                kh3 = k3[:, kl:kl + HEAD_DIM]
                vh3 = v3[:, vl:vl + HEAD_DIM]
                doh = do_t[:, ql:ql + HEAD_DIM]
                delta = jnp.sum(doh * o_t[:, ql:ql + HEAD_DIM], axis=-1, keepdims=True)
                lse = l_t[:, ql:ql + 1]
                dob = doh.astype(BF16)
                sc = lax.dot_general(qh, kh3, _NT, preferred_element_type=F32) + bias
                p = jnp.exp(sc - lse)
                dp = lax.dot_general(dob, vh3, _NT, preferred_element_type=F32)
                dsb = (p * (dp - delta)).astype(BF16)
                dqs.append(lax.dot_general(dsb, kh3, _NN, preferred_element_type=F32) * scale)
                dk = lax.dot_general(dsb, qh, _TN, preferred_element_type=F32)
                dv = lax.dot_general(p.astype(BF16), dob, _TN, preferred_element_type=F32)
                dks[kvi] = dk if dks[kvi] is None else dks[kvi] + dk
                dvs[kvi] = dv if dvs[kvi] is None else dvs[kvi] + dv
                if has_sink:
                    psink = jnp.exp(sink_ref[0, si] - lse)
                    dsink_row = dsink_row + jnp.where(lane == si, -jnp.sum(psink * delta), 0.0)
            dq_ref[...] = _rope_apply(jnp.concatenate(dqs, axis=1), t_ref[...], -1).astype(BF16)
            wk = jnp.concatenate(dks, axis=1) if len(dks) > 1 else dks[0]
            wv = jnp.concatenate(dvs, axis=1) if len(dvs) > 1 else dvs[0]
            if has_sink:
                dsink_ref[0:1, :] += dsink_row

            @pl.when(j > 0)
            def _():
                emit(add_rows(acck[...], wk[:blk], True), add_rows(accv[...], wv[:blk], True))

            acck[...] = add_rows(wk[blk:blk + tq], nxtk[...], False)
            accv[...] = add_rows(wv[blk:blk + tq], nxtv[...], False)
            nxtk[...] = wk[blk + tq:]
            nxtv[...] = wv[blk + tq:]

        @pl.when(j == nt)
        def _():
            emit(acck[...], accv[...])

    def tile(width, per_tok, cb):
        return pl.BlockSpec((tq, width), lambda c, j: (jnp.minimum(j, nt - 1), c * per_tok + cb))

    def halos(width, per_tok, cb):
        before = pl.BlockSpec((blk, width), lambda c, j: (jnp.maximum(jnp.minimum(j, nt - 1) * r - 1, 0), c * per_tok + cb))
        after = pl.BlockSpec((blk, width),
                             lambda c, j: (jnp.minimum((jnp.minimum(j, nt - 1) + 1) * r, nbk - 1), c * per_tok + cb))
        return [before, tile(width, per_tok, cb), after]

    def lagged(width):
        return pl.BlockSpec((tq, width), lambda c, j: (jnp.maximum(j - 1, 0), c))

    in_specs = ([tile(qw, ptw // qw, qcb)] + halos(kw, ptw // kw, kcb) + halos(kw, ptw // kw, vcb)
                + [tile(qw, 1, 0)] * 3 + [tile(TABW, 1, 0), lagged(TABW)])
    args = [zv] * 7 + [ov, lv, dov, tv, tv]
    out_shape = [SDS((sd, d * qw), BF16), SDS((sd, d * 2 * kvw), BF16)]
    out_specs = [tile(qw, 1, 0), lagged(2 * kvw)]
    if has_sink:
        in_specs.append(pl.BlockSpec(memory_space=pltpu.SMEM))
        args.append(sink)
        out_shape.append(SDS((8, 128), F32))
        out_specs.append(pl.BlockSpec((8, 128), lambda c, j: (0, 0)))
    scratch = [pltpu.VMEM((tq, kvw), F32), pltpu.VMEM((tq, kvw), F32), pltpu.VMEM((blk, kvw), F32),
               pltpu.VMEM((blk, kvw), F32)]
    return _pcall(body, name=name, out_shape=tuple(out_shape), grid=(d, nt + 1), in_specs=in_specs,
                  out_specs=tuple(out_specs), scratch_shapes=scratch, dims=("arbitrary", "arbitrary"))(*args)


_WIN_HEADS = tuple((h * HEAD_DIM, (h // 4) * HEAD_DIM, 128 + (h // 4) * HEAD_DIM, h) for h in range(8))
_WIN_KV = ((0, 128), (64, 192))
_WIN_CFG = dict(d=1, blk=WIN_HALF, tq=ATTN_TQ, ptw=768, qw=512, kw=256, qcb=0, kcb=2, vcb=2, heads=_WIN_HEADS)
_DIL_HEADS = tuple((h * HEAD_DIM, h * HEAD_DIM, h * HEAD_DIM, h) for h in range(8))
_DIL_KV = tuple((h * HEAD_DIM, h * HEAD_DIM) for h in range(8))


def _dil_cfg(gi):
    return dict(d=DILATIONS[gi], blk=DIL_SIDE, tq=ATTN_TQ, ptw=1536, qw=512, kw=512, qcb=0, kcb=1, vcb=2,
                heads=_DIL_HEADS)


def _view_specs(tm, width):
    return tuple(pl.BlockSpec((tm // d, d * width), lambda i: (i, 0)) for d in DILATIONS)


def _mix_norm_fwd(oa, og_views, lg_views, g_win, g_dil):
    s = oa.shape[0]
    tm = _pick(s, (512,))

    def body(oa_ref, o0, o1, o2, l0, l1, l2, gw_ref, gd_ref, mixed_ref, ob0, ob1, ob2, lb0, lb1, lb2, scr, ob_s):
        o_refs, l_refs, ob_refs, lb_refs = (o0, o1, o2), (l0, l1, l2), (ob0, ob1, ob2), (lb0, lb1, lb2)
        ssq = jnp.zeros((tm, 1), F32)
        for q in range(4):
            os_, ls_ = [], []
            for g, d in enumerate(DILATIONS):
                cols = [slice(c * 512 + q * 128, c * 512 + (q + 1) * 128) for c in range(d)]
                os_.append(_from_classes([o_refs[g][:, cs] for cs in cols], scr))
                ls_.append(_from_classes([l_refs[g][:, cs] for cs in cols], scr))
            mx = jnp.maximum(jnp.maximum(ls_[0], ls_[1]), ls_[2])
            es = [jnp.exp(l - mx) for l in ls_]
            den = es[0] + es[1] + es[2]
            ob = (es[0] / den) * os_[0] + (es[1] / den) * os_[1] + (es[2] / den) * os_[2]
            lb = mx + jnp.log(den)
            ob_s[:, q * 128:(q + 1) * 128] = ob
            ssq = ssq + jnp.sum(ob * ob, axis=-1, keepdims=True)
            for g, d in enumerate(DILATIONS):
                for val, refs in ((ob, ob_refs), (lb, lb_refs)):
                    for c, part in enumerate(_to_classes(val, scr, d)):
                        refs[g][:, c * 512 + q * 128:c * 512 + (q + 1) * 128] = part
        a = oa_ref[...]
        ra = lax.rsqrt(jnp.mean(a * a, axis=-1, keepdims=True) + LN_EPS)
        rb = lax.rsqrt(ssq * (1.0 / 512) + LN_EPS)
        mixed_ref[...] = jnp.concatenate([a * ra * gw_ref[...], ob_s[...] * rb * gd_ref[...]], axis=1).astype(BF16)

    row = pl.BlockSpec((tm, 512), lambda i: (i, 0))
    vec = pl.BlockSpec((1, 512), lambda i: (0, 0))
    views = _view_specs(tm, 512)
    view_shapes = tuple(SDS((s // d, d * 512), F32) for d in DILATIONS)
    res = _pcall(body, name="mix_norm_fwd", out_shape=(SDS((s, 1024), BF16),) + view_shapes * 2, grid=(s // tm,),
                 in_specs=[row, *views, *views, vec, vec],
                 out_specs=(pl.BlockSpec((tm, 1024), lambda i: (i, 0)),) + views * 2,
                 scratch_shapes=[pltpu.VMEM((tm, 128), F32), pltpu.VMEM((tm, 512), F32)],
                 dims=("parallel",))(oa, *og_views, *lg_views, g_win, g_dil)
    return res[0], res[1:4], res[4:7]


def _mix_norm_bwd(oa, ob, dmixed, g_win, g_dil):
    s = oa.shape[0]
    tm = _pick(s, (512,))
    nt = s // tm

    def body(oa_ref, ob_ref, dm_ref, gw_ref, gd_ref, doa_ref, db0, db1, db2, dgw_ref, dgd_ref, acc_w, acc_d, scr):
        i = pl.program_id(0)

        @pl.when(i == 0)
        def _():
            acc_w[...] = jnp.zeros_like(acc_w)
            acc_d[...] = jnp.zeros_like(acc_d)

        dm = dm_ref[...]
        dxs = []
        for x_ref, g_ref, dy, acc in ((oa_ref, gw_ref, dm[:, :512], acc_w), (ob_ref, gd_ref, dm[:, 512:], acc_d)):
            x = x_ref[...]
            r = lax.rsqrt(jnp.mean(x * x, axis=-1, keepdims=True) + LN_EPS)
            dyg = dy * g_ref[...]
            dxs.append(r * dyg - x * (r * r * r) * jnp.mean(dyg * x, axis=-1, keepdims=True))
            acc[...] += jnp.sum((dy * x * r).reshape(tm // 8, 8, 512), axis=0)
        doa_ref[...] = dxs[0]
        for q in range(4):
            dq = dxs[1][:, q * 128:(q + 1) * 128]
            for db_ref, d in zip((db0, db1, db2), DILATIONS):
                for c, part in enumerate(_to_classes(dq, scr, d)):
                    db_ref[:, c * 512 + q * 128:c * 512 + (q + 1) * 128] = part

        @pl.when(i == nt - 1)
        def _():
            dgw_ref[...] = jnp.sum(acc_w[...], axis=0, keepdims=True)
            dgd_ref[...] = jnp.sum(acc_d[...], axis=0, keepdims=True)

    row = pl.BlockSpec((tm, 512), lambda i: (i, 0))
    vec = pl.BlockSpec((1, 512), lambda i: (0, 0))
    views = _view_specs(tm, 512)
    view_shapes = tuple(SDS((s // d, d * 512), F32) for d in DILATIONS)
    res = _pcall(body, name="mix_norm_bwd",
                 out_shape=(SDS((s, 512), F32),) + view_shapes + (SDS((1, 512), F32), SDS((1, 512), F32)),
                 grid=(nt,), in_specs=[row, row, pl.BlockSpec((tm, 1024), lambda i: (i, 0)), vec, vec],
                 out_specs=(row,) + views + (vec, vec),
                 scratch_shapes=[pltpu.VMEM((8, 512), F32), pltpu.VMEM((8, 512), F32), pltpu.VMEM((tm, 128), F32)],
                 dims=("arbitrary",))(oa, ob, dmixed, g_win, g_dil)
    return res[0], res[1:4], res[4], res[5]


def _dz_assemble(dq_views, dkv_views, dqa, dkva):
    s = dqa.shape[0]
    tm = _pick(s, (512,))

    def body(q0, q1, q2, kv0, kv1, kv2, qa_ref, kva_ref, o_ref, scr):
        for g, d in enumerate(DILATIONS):
            for kind, (ref, width, base) in enumerate((((q0, q1, q2)[g], 512, 0), ((kv0, kv1, kv2)[g], 1024, 0),
                                                       ((kv0, kv1, kv2)[g], 1024, 512))):
                for q in range(4):
                    src = base + q * 128
                    dst = kind * 1536 + g * 512 + q * 128
                    if d == 1:
                        o_ref[:, dst:dst + 128] = ref[:, src:src + 128]
                    else:
                        parts = [ref[:, c * width + src:c * width + src + 128].astype(F32) for c in range(d)]
                        o_ref[:, dst:dst + 128] = _from_classes(parts, scr).astype(BF16)
        o_ref[:, Z_QA:Z_QA + 512] = qa_ref[...]
        o_ref[:, Z_KA:Z_KA + 256] = kva_ref[...]

    return _pcall(body, name="dz_assemble", out_shape=SDS((s, IN_WIDTH), BF16), grid=(s // tm,),
                  in_specs=[*_view_specs(tm, 512), *_view_specs(tm, 1024), pl.BlockSpec((tm, 512), lambda i: (i, 0)),
                            pl.BlockSpec((tm, 256), lambda i: (i, 0))],
                  out_specs=pl.BlockSpec((tm, IN_WIDTH), lambda i: (i, 0)),
                  scratch_shapes=[pltpu.VMEM((tm, 128), F32)], dims=("parallel",))(*dq_views, *dkv_views, dqa, dkva)


def _ln_fwd(a, r, g, b, ca, name):
    s = a.shape[0]
    tm = _pick(s, (512, 256))
    has_r = r is not None

    def body(*refs):
        a_ref = refs[0]
        r_ref = refs[1] if has_r else None
        g_ref, b_ref, o_ref, ob_ref = refs[1 + has_r:]
        u = a_ref[...] if ca == 1.0 else ca * a_ref[...]
        if has_r:
            u = u + r_ref[...]
        mu = jnp.mean(u, axis=-1, keepdims=True)
        xc = u - mu
        var = jnp.mean(xc * xc, axis=-1, keepdims=True)
        y = xc * lax.rsqrt(var + LN_EPS) * g_ref[...] + b_ref[...]
        o_ref[...] = y
        ob_ref[...] = y.astype(BF16)

    row = pl.BlockSpec((tm, D_MODEL), lambda i: (i, 0))
    vec = pl.BlockSpec((1, D_MODEL), lambda i: (0, 0))
    args = [a] + ([r] if has_r else []) + [g, b]
    return _pcall(body, name=name, out_shape=(SDS((s, D_MODEL), F32), SDS((s, D_MODEL), BF16)), grid=(s // tm,),
                  in_specs=[row] * (1 + has_r) + [vec, vec], out_specs=(row, row), dims=("parallel",))(*args)


def _ln_bwd(a, r, dy, g, b, ca, name, loss_mode=False):
    s = a.shape[0]
    tm = _pick(s, (512, 256))
    nt = s // tm
    has_r = r is not None

    def body(*refs):
        a_ref = refs[0]
        r_ref = refs[1] if has_r else None
        dy_ref, g_ref, b_ref = refs[1 + has_r:4 + has_r]
        outs = refs[4 + has_r:]
        if loss_mode:
            du_ref, dub_ref, dg_ref, db_ref, loss_ref, acc_g, acc_b, acc_l = outs
        else:
            du_ref, dub_ref, dg_ref, db_ref, acc_g, acc_b = outs
        i = pl.program_id(0)

        @pl.when(i == 0)
        def _():
            acc_g[...] = jnp.zeros_like(acc_g)
            acc_b[...] = jnp.zeros_like(acc_b)
            if loss_mode:
                acc_l[...] = jnp.zeros_like(acc_l)

        u = a_ref[...] if ca == 1.0 else ca * a_ref[...]
        if has_r:
            u = u + r_ref[...]
        mu = jnp.mean(u, axis=-1, keepdims=True)
        xc = u - mu
        var = jnp.mean(xc * xc, axis=-1, keepdims=True)
        rstd = lax.rsqrt(var + LN_EPS)
        xhat = xc * rstd
        gv = g_ref[...]
        if loss_mode:
            err = (xhat * gv + b_ref[...]) - dy_ref[...]
            acc_l[...] += jnp.sum((err * err).reshape(tm // 8, 8, D_MODEL), axis=0)
            dyv = err * (1.0 / D_MODEL)
        else:
            dyv = dy_ref[...]
        dxh = dyv * gv
        du = rstd * (dxh - jnp.mean(dxh, axis=-1, keepdims=True) - xhat * jnp.mean(dxh * xhat, axis=-1, keepdims=True))
        du_ref[...] = du
        dub_ref[...] = du.astype(BF16)
        acc_g[...] += jnp.sum((dyv * xhat).reshape(tm // 8, 8, D_MODEL), axis=0)
        acc_b[...] += jnp.sum(dyv.reshape(tm // 8, 8, D_MODEL), axis=0)

        @pl.when(i == nt - 1)
        def _():
            dg_ref[...] = jnp.sum(acc_g[...], axis=0, keepdims=True)
            db_ref[...] = jnp.sum(acc_b[...], axis=0, keepdims=True)
            if loss_mode:
                tot = jnp.sum(jnp.sum(acc_l[...], axis=0, keepdims=True), axis=1, keepdims=True)
                loss_ref[...] = tot * (0.5 / D_MODEL)

    row = pl.BlockSpec((tm, D_MODEL), lambda i: (i, 0))
    vec = pl.BlockSpec((1, D_MODEL), lambda i: (0, 0))
    out_shape = [SDS((s, D_MODEL), F32), SDS((s, D_MODEL), BF16), SDS((1, D_MODEL), F32), SDS((1, D_MODEL), F32)]
    out_specs = [row, row, vec, vec]
    scratch = [pltpu.VMEM((8, D_MODEL), F32), pltpu.VMEM((8, D_MODEL), F32)]
    if loss_mode:
        out_shape.append(SDS((1, 1), F32))
        out_specs.append(pl.BlockSpec((1, 1), lambda i: (0, 0)))
        scratch.append(pltpu.VMEM((8, D_MODEL), F32))
    args = [a] + ([r] if has_r else []) + [dy, g, b]
    return _pcall(body, name=name, out_shape=tuple(out_shape), grid=(nt,), in_specs=[row] * (2 + has_r) + [vec, vec],
                  out_specs=tuple(out_specs), scratch_shapes=scratch, dims=("arbitrary",))(*args)


def _xattn_fwd(q, k, v):
    s = q.shape[0]
    tq = _pick(s, (512,))
    scale = X_HEAD_DIM ** -0.5

    def body(q_ref, k_ref, v_ref, o_ref, ob_ref):
        qv, kv, vv = q_ref[...], k_ref[...], v_ref[...]
        outs = []
        for h in range(X_HEADS):
            sl = slice(h * X_HEAD_DIM, (h + 1) * X_HEAD_DIM)
            sc = lax.dot_general(qv[:, sl], kv[:, sl], _NT, preferred_element_type=F32) * scale
            e = jnp.exp(sc - jnp.max(sc, axis=-1, keepdims=True))
            p = e / jnp.sum(e, axis=-1, keepdims=True)
            outs.append(lax.dot_general(p.astype(BF16), vv[:, sl], _NN, preferred_element_type=F32))
        o = jnp.concatenate(outs, axis=1)
        o_ref[...] = o
        ob_ref[...] = o.astype(BF16)

    row = pl.BlockSpec((tq, D_MODEL), lambda i: (i, 0))
    full = pl.BlockSpec((MEM_LEN, D_MODEL), lambda i: (0, 0))
    return _pcall(body, name="xattn_fwd", out_shape=(SDS((s, D_MODEL), F32), SDS((s, D_MODEL), BF16)), grid=(s // tq,),
                  in_specs=[row, full, full], out_specs=(row, row), dims=("parallel",))(q, k, v)


def _xattn_bwd(q, k, v, o, do):
    s = q.shape[0]
    tq = _pick(s, (512,))
    scale = X_HEAD_DIM ** -0.5

    def body(q_ref, k_ref, v_ref, o_ref, do_ref, dq_ref, dk_ref, dv_ref):
        i = pl.program_id(0)

        @pl.when(i == 0)
        def _():
            dk_ref[...] = jnp.zeros_like(dk_ref)
            dv_ref[...] = jnp.zeros_like(dv_ref)

        qv, kv, vv, ov, dov = q_ref[...], k_ref[...], v_ref[...], o_ref[...], do_ref[...]
        dqs, dks, dvs = [], [], []
        for h in range(X_HEADS):
            sl = slice(h * X_HEAD_DIM, (h + 1) * X_HEAD_DIM)
            sc = lax.dot_general(qv[:, sl], kv[:, sl], _NT, preferred_element_type=F32) * scale
            e = jnp.exp(sc - jnp.max(sc, axis=-1, keepdims=True))
            p = e / jnp.sum(e, axis=-1, keepdims=True)
            doh = dov[:, sl]
            dob = doh.astype(BF16)
            delta = jnp.sum(doh * ov[:, sl], axis=-1, keepdims=True)
            dvs.append(lax.dot_general(p.astype(BF16), dob, _TN, preferred_element_type=F32))
            dp = lax.dot_general(dob, vv[:, sl], _NT, preferred_element_type=F32)
            ds = (p * (dp - delta)).astype(BF16)
            dqs.append(lax.dot_general(ds, kv[:, sl], _NN, preferred_element_type=F32) * scale)
            dks.append(lax.dot_general(ds, qv[:, sl], _TN, preferred_element_type=F32) * scale)
        dq_ref[...] = jnp.concatenate(dqs, axis=1).astype(BF16)
        dk_ref[...] += jnp.concatenate(dks, axis=1)
        dv_ref[...] += jnp.concatenate(dvs, axis=1)

    row = pl.BlockSpec((tq, D_MODEL), lambda i: (i, 0))
    full = pl.BlockSpec((MEM_LEN, D_MODEL), lambda i: (0, 0))
    return _pcall(body, name="xattn_bwd",
                  out_shape=(SDS((s, D_MODEL), BF16), SDS((MEM_LEN, D_MODEL), F32), SDS((MEM_LEN, D_MODEL), F32)),
                  grid=(s // tq,), in_specs=[row, full, full, row, row], out_specs=(row, full, full),
                  dims=("arbitrary",))(q, k, v, o, do)


_SQRT_HALF = 0.7071067811865476
_INV_SQRT_2PI = 0.3989422804014327


def _halo_specs(s, tm, width, rows=8):
    nb = s // rows
    r = tm // rows
    prev = pl.BlockSpec((rows, width), lambda i: (jnp.maximum(i * r - 1, 0), 0))
    nxt = pl.BlockSpec((rows, width), lambda i: (jnp.minimum((i + 1) * r, nb - 1), 0))
    return prev, nxt


def _shifted(x, before_row, after_row, i, nt):
    tm = x.shape[0]
    row = lax.broadcasted_iota(jnp.int32, x.shape, 0)
    first = jnp.where(i == 0, 0.0, 1.0) * before_row
    last = jnp.where(i == nt - 1, 0.0, 1.0) * after_row
    xm1 = jnp.where(row == 0, first, pltpu.roll(x, 1, 0))
    xp1 = jnp.where(row == tm - 1, last, pltpu.roll(x, tm - 1, 0))
    return xm1, xp1


BF16_ROWS = 16


def _ffn_fwd(hb, wg_t, wu_t, cw, cb):
    s = hb.shape[0]
    tm = _pick(s, (256,))
    nt = s // tm
    hr = BF16_ROWS

    def body(h_ref, hp_ref, hn_ref, wg_ref, wu_ref, cw_ref, cb_ref, g_ref, up_ref, act_ref):
        i = pl.program_id(0)
        hv = h_ref[...]
        g_ext = lax.dot_general(jnp.concatenate([hp_ref[...], hv, hn_ref[...]], axis=0), wg_ref[...], _NT,
                                preferred_element_type=F32)
        gv = g_ext[hr:hr + tm]
        upv = lax.dot_general(hv, wu_ref[...], _NT, preferred_element_type=F32)
        gm1, gp1 = _shifted(gv, g_ext[hr - 1:hr], g_ext[hr + tm:hr + tm + 1], i, nt)
        gc = gm1 * cw_ref[0:1, :] + gv * cw_ref[1:2, :] + gp1 * cw_ref[2:3, :] + cb_ref[...]
        gelu = 0.5 * gc * (1.0 + lax.erf(gc * _SQRT_HALF))
        g_ref[...] = gv
        up_ref[...] = upv
        act_ref[...] = (gelu * upv).astype(BF16)

    hrow = pl.BlockSpec((tm, D_MODEL), lambda i: (i, 0))
    prev, nxt = _halo_specs(s, tm, D_MODEL, hr)
    wfull = pl.BlockSpec((D_FF, D_MODEL), lambda i: (0, 0))
    row = pl.BlockSpec((tm, D_FF), lambda i: (i, 0))
    return _pcall(body, name="ffn_fwd", out_shape=(SDS((s, D_FF), F32), SDS((s, D_FF), F32), SDS((s, D_FF), BF16)),
                  grid=(nt,), in_specs=[hrow, prev, nxt, wfull, wfull, pl.BlockSpec((8, D_FF), lambda i: (0, 0)),
                                        pl.BlockSpec((1, D_FF), lambda i: (0, 0))],
                  out_specs=(row, row, row), dims=("parallel",))(hb, hb, hb, wg_t, wu_t, cw, cb)


def _ffn_bwd(dffb, w_down, g, up, cw, cb):
    s = g.shape[0]
    tm = _pick(s, (256,))
    nt = s // tm
    hr = BF16_ROWS

    def body(df_ref, dfp_ref, dfn_ref, wd_ref, g_ref, gp_ref, gn_ref, up_ref, upp_ref, upn_ref, cw_ref, cb_ref,
             dg_ref, dup_ref, dcw_ref, dcb_ref, a0, a1, a2, a3):
        i = pl.program_id(0)
        da_ext = lax.dot_general(jnp.concatenate([dfp_ref[...], df_ref[...], dfn_ref[...]], axis=0), wd_ref[...], _NT,
                                 preferred_element_type=F32)

        @pl.when(i == 0)
        def _():
            for a in (a0, a1, a2, a3):
                a[...] = jnp.zeros_like(a)

        cw0, cw1, cw2, cbv = cw_ref[0:1, :], cw_ref[1:2, :], cw_ref[2:3, :], cb_ref[...]

        def d_conv_out(gc_, up_, da_):
            cdf_ = 0.5 * (1.0 + lax.erf(gc_ * _SQRT_HALF))
            pdf_ = jnp.exp(-0.5 * gc_ * gc_) * _INV_SQRT_2PI
            return da_ * up_ * (cdf_ + gc_ * pdf_), cdf_

        gv = g_ref[...]
        g_before, g_after = gp_ref[...], gn_ref[...]
        gm1, gp1 = _shifted(gv, g_before[7:8, :], g_after[0:1, :], i, nt)
        gc = gm1 * cw0 + gv * cw1 + gp1 * cw2 + cbv
        da = da_ext[hr:hr + tm]
        dgc, cdf = d_conv_out(gc, up_ref[...], da)
        dup_ref[...] = (da * (gc * cdf)).astype(BF16)
        gc_b = g_before[6:7, :] * cw0 + g_before[7:8, :] * cw1 + gv[0:1, :] * cw2 + cbv
        gc_a = gv[tm - 1:tm, :] * cw0 + g_after[0:1, :] * cw1 + g_after[1:2, :] * cw2 + cbv
        dgc_b = jnp.where(i == 0, 0.0, 1.0) * d_conv_out(gc_b, upp_ref[7:8, :], da_ext[hr - 1:hr])[0]
        dgc_a = jnp.where(i == nt - 1, 0.0, 1.0) * d_conv_out(gc_a, upn_ref[0:1, :], da_ext[hr + tm:hr + tm + 1])[0]
        row = lax.broadcasted_iota(jnp.int32, dgc.shape, 0)
        dgc_m1 = jnp.where(row == 0, dgc_b, pltpu.roll(dgc, 1, 0))
        dgc_p1 = jnp.where(row == tm - 1, dgc_a, pltpu.roll(dgc, tm - 1, 0))
        dg_ref[...] = (dgc_p1 * cw0 + dgc * cw1 + dgc_m1 * cw2).astype(BF16)

        def fold(t):
            return jnp.sum(t.reshape(tm // 8, 8, D_FF), axis=0)

        a0[...] += fold(dgc * gm1)
        a1[...] += fold(dgc * gv)
        a2[...] += fold(dgc * gp1)
        a3[...] += fold(dgc)

        @pl.when(i == nt - 1)
        def _():
            dcw_ref[...] = jnp.concatenate(
                [jnp.sum(a[...], axis=0, keepdims=True) for a in (a0, a1, a2)] + [jnp.zeros((5, D_FF), F32)], axis=0)
            dcb_ref[...] = jnp.sum(a3[...], axis=0, keepdims=True)

    row = pl.BlockSpec((tm, D_FF), lambda i: (i, 0))
    prev, nxt = _halo_specs(s, tm, D_FF)
    cw_spec = pl.BlockSpec((8, D_FF), lambda i: (0, 0))
    cb_spec = pl.BlockSpec((1, D_FF), lambda i: (0, 0))
    dprev, dnxt = _halo_specs(s, tm, D_MODEL, hr)
    return _pcall(body, name="ffn_bwd",
                  out_shape=(SDS((s, D_FF), BF16), SDS((s, D_FF), BF16), SDS((8, D_FF), F32), SDS((1, D_FF), F32)),
                  grid=(nt,),
                  in_specs=[pl.BlockSpec((tm, D_MODEL), lambda i: (i, 0)), dprev, dnxt,
                            pl.BlockSpec((D_FF, D_MODEL), lambda i: (0, 0))] + [row, prev, nxt] * 2 + [cw_spec, cb_spec],
                  out_specs=(row, row, cw_spec, cb_spec), scratch_shapes=[pltpu.VMEM((8, D_FF), F32)] * 4,
                  dims=("arbitrary",))(dffb, dffb, dffb, w_down, g, g, g, up, up, up, cw, cb)


def _adamw(w, g, m, v, name):
    rows, cols = w.shape
    tr = _pick(rows, (256, 128, 64, 32, 16, 8))
    c1 = 1.0 - ADAM_B1 ** ADAM_STEP
    c2 = 1.0 - ADAM_B2 ** ADAM_STEP

    def body(w_ref, g_ref, m_ref, v_ref, d_ref, nm_ref, nv_ref):
        gv = g_ref[...]
        nm = ADAM_B1 * m_ref[...] + (1.0 - ADAM_B1) * gv
        nv = ADAM_B2 * v_ref[...] + (1.0 - ADAM_B2) * (gv * gv)
        d_ref[...] = -ADAM_LR * ((nm / c1) / (jnp.sqrt(nv / c2) + ADAM_EPS) + ADAM_WD * w_ref[...])
        nm_ref[...] = nm
        nv_ref[...] = nv

    blk = pl.BlockSpec((tr, cols), lambda i: (i, 0))
    return _pcall(body, name=name, out_shape=(SDS(w.shape, F32),) * 3, grid=(rows // tr,), in_specs=[blk] * 4,
                  out_specs=(blk,) * 3, dims=("parallel",))(w, g, m, v)


def _adamw_many(ws, gs, ms, vs, name):
    n = len(ws)
    c1 = 1.0 - ADAM_B1 ** ADAM_STEP
    c2 = 1.0 - ADAM_B2 ** ADAM_STEP

    def body(*refs):
        outs = refs[4 * n:]
        for k in range(n):
            gv = refs[n + k][...]
            nm = ADAM_B1 * refs[2 * n + k][...] + (1.0 - ADAM_B1) * gv
            nv = ADAM_B2 * refs[3 * n + k][...] + (1.0 - ADAM_B2) * (gv * gv)
            outs[k][...] = -ADAM_LR * ((nm / c1) / (jnp.sqrt(nv / c2) + ADAM_EPS) + ADAM_WD * refs[k][...])
            outs[n + k][...] = nm
            outs[2 * n + k][...] = nv

    shapes = tuple(SDS(w.shape, F32) for w in ws)
    res = _pcall(body, name=name, out_shape=shapes * 3)(*ws, *gs, *ms, *vs)
    return res[:n], res[n:2 * n], res[2 * n:]


def _all_gather_rows(x_shard, *, name, in_vmem, sum_rows=False, after=None):
    m_per, n = x_shard.shape
    extra = [] if after is None else [after]

    def body(x_ref, *rest):
        out_ref, rest = rest[len(extra)], rest[len(extra) + 1:]
        if sum_rows:
            sum_ref, send_sems, recv_sems, local_sem = rest
        else:
            send_sems, recv_sems, local_sem = rest
        x, y, c = lax.axis_index("x"), lax.axis_index("y"), lax.axis_index("c")
        me, sibling = (x, y, c), (x, y, 1 - c)
        chips = [(1 - x, y), (x, 1 - y), (1 - x, 1 - y)]

        def rows(px, py, pc):
            return out_ref.at[pl.ds((4 * px + 2 * py + pc) * m_per, m_per), :]

        def copy(k, block, to, src=None):
            return pltpu.make_async_remote_copy(
                src_ref=rows(*block) if src is None else src, dst_ref=rows(*block), send_sem=send_sems.at[k],
                recv_sem=recv_sems.at[k], device_id=to, device_id_type=pl.DeviceIdType.MESH)

        mine = pltpu.make_async_copy(x_ref, rows(*me), local_sem)
        mine.start()
        first = [copy(0, me, sibling, src=x_ref)]
        first += [copy(1 + j, me, (*chip, c), src=x_ref) for j, chip in enumerate(chips)]
        for cp in first:
            cp.start()
        passed = [copy(4 + j, (*chip, c), sibling) for j, chip in enumerate(chips)]
        for j, chip in enumerate(chips):
            copy(1 + j, (*chip, c), me).wait_recv()
            passed[j].start()
        copy(0, sibling, me).wait_recv()
        for j, chip in enumerate(chips):
            copy(4 + j, (*chip, 1 - c), me).wait_recv()
        for cp in first + passed:
            cp.wait_send()
        mine.wait()
        if sum_rows:
            acc = out_ref[0:m_per, :]
            for dev in range(1, N_DEV):
                acc = acc + out_ref[dev * m_per:(dev + 1) * m_per, :]
            sum_ref[...] = acc

    space = pltpu.VMEM if in_vmem else pl.ANY
    out_shape = [SDS((N_DEV * m_per, n), x_shard.dtype)]
    out_specs = [pl.BlockSpec(memory_space=space)]
    if sum_rows:
        out_shape.append(SDS((m_per, n), x_shard.dtype))
        out_specs.append(pl.BlockSpec(memory_space=pltpu.VMEM))
    res = _PALLAS_CALL(
        body, name=name, out_shape=tuple(out_shape),
        in_specs=[pl.BlockSpec(memory_space=space)] + [pl.BlockSpec(memory_space=pl.ANY)] * len(extra),
        out_specs=tuple(out_specs),
        scratch_shapes=[pltpu.SemaphoreType.DMA((7,)), pltpu.SemaphoreType.DMA((7,)), pltpu.SemaphoreType.DMA],
        compiler_params=pltpu.CompilerParams(vmem_limit_bytes=VMEM_LIMIT_BYTES),
    )(x_shard, *extra)
    return res if sum_rows else res[0]


_HBM = pl.BlockSpec(memory_space=pltpu.HBM)
_SEM = pl.BlockSpec(memory_space=pltpu.SEMAPHORE)
_SPLIT_PARAMS = dict(has_side_effects=pltpu.SideEffectType.DATAFLOW_SIDE_EFFECTING)


def _split_copies(src_ref, land_ref, send_sems, recv_sems, gather):
    x, y, c = lax.axis_index("x"), lax.axis_index("y"), lax.axis_index("c")
    copies = []
    for k in range(1, N_DEV):
        px = 1 - x if k & 4 else x
        py = 1 - y if k & 2 else y
        pc = 1 - c if k & 1 else c
        if gather:
            rows = src_ref.shape[0]
            src, dst = src_ref, land_ref.at[pl.ds((4 * x + 2 * y + c) * rows, rows), :]
        else:
            src, dst = src_ref.at[4 * px + 2 * py + pc], land_ref.at[k - 1]
        copies.append(pltpu.make_async_remote_copy(
            src_ref=src, dst_ref=dst, send_sem=send_sems.at[k - 1], recv_sem=recv_sems.at[k - 1],
            device_id=(px, py, pc), device_id_type=pl.DeviceIdType.MESH))
    return copies


def _exchange_start(src, land_shape, *, gather, name):
    def body(src_ref, land_ref, send_sems, recv_sems, src_thru, land_thru, token):
        for cp in _split_copies(src_ref, land_ref, send_sems, recv_sems, gather):
            cp.start()
        token[...] = jnp.zeros_like(token)

    land = pltpu.with_memory_space_constraint(lax.empty(land_shape, src.dtype), pltpu.HBM)
    return _PALLAS_CALL(
        body, name=name,
        out_shape=(pltpu.SemaphoreType.DMA((N_DEV - 1,)), pltpu.SemaphoreType.DMA((N_DEV - 1,)),
                   pltpu.HBM(src.shape, src.dtype), pltpu.HBM(land_shape, src.dtype), SDS((8, 128), F32)),
        in_specs=(_HBM, _HBM), out_specs=(_SEM, _SEM, _HBM, _HBM, pl.BlockSpec(memory_space=pltpu.VMEM)),
        input_output_aliases={0: 2, 1: 3}, compiler_params=pltpu.CompilerParams(**_SPLIT_PARAMS),
    )(pltpu.with_memory_space_constraint(src, pltpu.HBM), land)


def _exchange_wait(started, after, *, gather, name):
    send_sems, recv_sems, src_thru, land_thru, _ = started

    def body(src_ref, land_ref, send_sems, recv_sems, after_ref, src_out, land_out):
        copies = _split_copies(src_ref, land_ref, send_sems, recv_sems, gather)
        for cp in copies:
            cp.wait_send()
        for cp in copies:
            cp.wait_recv()

    return _PALLAS_CALL(
        body, name=name,
        out_shape=(pltpu.HBM(src_thru.shape, src_thru.dtype), pltpu.HBM(land_thru.shape, land_thru.dtype)),
        in_specs=(_HBM, _HBM, _SEM, _SEM, pl.BlockSpec(memory_space=pl.ANY)), out_specs=(_HBM, _HBM),
        input_output_aliases={0: 0, 1: 1}, compiler_params=pltpu.CompilerParams(**_SPLIT_PARAMS),
    )(src_thru, land_thru, send_sems, recv_sems, after)


def _sum_parts(own, land, name):
    r, n = own.shape
    tr = _pick(r, (264, 320, 336, 128, 64, 32, 16, 8))

    def body(own_ref, x_ref, o_ref):
        acc = own_ref[...]
        for k in range(N_DEV - 1):
            acc = acc + x_ref[k]
        o_ref[...] = acc

    return _pcall(body, name=name, out_shape=SDS((r, n), F32), grid=(r // tr,),
                  in_specs=[pl.BlockSpec((tr, n), lambda i: (i, 0)), pl.BlockSpec((N_DEV - 1, tr, n), lambda i: (0, i, 0))],
                  out_specs=pl.BlockSpec((tr, n), lambda i: (i, 0)), dims=("parallel",))(own, land)


def _pad_rows(a, rows):
    return jnp.pad(a, ((0, rows - a.shape[0]), (0, 0)))


def kernel(x, mem, positions, ln_in_g, ln_in_b, w_in, attn_sink, g_win, g_dil, w_mix_out, ln1_g, ln1_b, mem_ln_g, mem_ln_b, w_xq, w_xk, w_xv, w_xo, ln2_g, ln2_b, w_gate, w_up, conv_w, conv_b, w_down, ln3_g, ln3_b, loss_target, m_ln_in_g, m_ln_in_b, m_w_in, m_attn_sink, m_g_win, m_g_dil, m_w_mix_out, m_ln1_g, m_ln1_b, m_mem_ln_g, m_mem_ln_b, m_w_xq, m_w_xk, m_w_xv, m_w_xo, m_ln2_g, m_ln2_b, m_w_gate, m_w_up, m_conv_w, m_conv_b, m_w_down, m_ln3_g, m_ln3_b, v_ln_in_g, v_ln_in_b, v_w_in, v_attn_sink, v_g_win, v_g_dil, v_w_mix_out, v_ln1_g, v_ln1_b, v_mem_ln_g, v_mem_ln_b, v_w_xq, v_w_xk, v_w_xv, v_w_xo, v_ln2_g, v_ln2_b, v_w_gate, v_w_up, v_conv_w, v_conv_b, v_w_down, v_ln3_g, v_ln3_b):
    weights = dict(ln_in_g=ln_in_g, ln_in_b=ln_in_b, w_in=w_in, attn_sink=attn_sink, g_win=g_win, g_dil=g_dil, w_mix_out=w_mix_out, ln1_g=ln1_g, ln1_b=ln1_b, mem_ln_g=mem_ln_g, mem_ln_b=mem_ln_b, w_xq=w_xq, w_xk=w_xk, w_xv=w_xv, w_xo=w_xo, ln2_g=ln2_g, ln2_b=ln2_b, w_gate=w_gate, w_up=w_up, conv_w=conv_w, conv_b=conv_b, w_down=w_down, ln3_g=ln3_g, ln3_b=ln3_b)
    mom_m = dict(ln_in_g=m_ln_in_g, ln_in_b=m_ln_in_b, w_in=m_w_in, attn_sink=m_attn_sink, g_win=m_g_win, g_dil=m_g_dil, w_mix_out=m_w_mix_out, ln1_g=m_ln1_g, ln1_b=m_ln1_b, mem_ln_g=m_mem_ln_g, mem_ln_b=m_mem_ln_b, w_xq=m_w_xq, w_xk=m_w_xk, w_xv=m_w_xv, w_xo=m_w_xo, ln2_g=m_ln2_g, ln2_b=m_ln2_b, w_gate=m_w_gate, w_up=m_w_up, conv_w=m_conv_w, conv_b=m_conv_b, w_down=m_w_down, ln3_g=m_ln3_g, ln3_b=m_ln3_b)
    mom_v = dict(ln_in_g=v_ln_in_g, ln_in_b=v_ln_in_b, w_in=v_w_in, attn_sink=v_attn_sink, g_win=v_g_win, g_dil=v_g_dil, w_mix_out=v_w_mix_out, ln1_g=v_ln1_g, ln1_b=v_ln1_b, mem_ln_g=v_mem_ln_g, mem_ln_b=v_mem_ln_b, w_xq=v_w_xq, w_xk=v_w_xk, w_xv=v_w_xv, w_xo=v_w_xo, ln2_g=v_ln2_g, ln2_b=v_ln2_b, w_gate=v_w_gate, w_up=v_w_up, conv_w=v_conv_w, conv_b=v_conv_b, w_down=v_w_down, ln3_g=v_ln3_g, ln3_b=v_ln3_b)
    order = list(weights)
    s = x.shape[1]
    xs = x[0]
    mems = mem[0]
    target = loss_target[0]
    row = lambda a: a.reshape(1, -1)

    shard_rows = dict(w_in=w_in[0].T, w_gate=w_gate[0].T, w_up=w_up[0].T, w_mix_out=w_mix_out[0], w_xq=w_xq[0],
                      w_xk=w_xk[0], w_xv=w_xv[0], w_xo=w_xo[0], w_down=w_down[0])
    me_lin = 4 * lax.axis_index("x") + 2 * lax.axis_index("y") + lax.axis_index("c")
    w_in_full = _all_gather_rows(shard_rows["w_in"].astype(BF16), name="w_in_all_gather", in_vmem=False)
    w_in_t = jnp.concatenate([w_in_full[768:], w_in_full[:768]], axis=0)
    late_rows = PACK_ROWS[1:]
    late_r = sum(r for _, r in late_rows)
    packed = jnp.concatenate([shard_rows[n].astype(BF16) for n, _ in late_rows], axis=0)
    w_started = _exchange_start(packed, (N_DEV * late_r, D_MODEL), gather=True, name="weight_gather_start")
    cw_pad = jnp.pad(conv_w[0], ((0, 5), (0, 32)))
    cw_all = _all_gather_rows(cw_pad, name="conv_w_all_gather", in_vmem=True).reshape(N_DEV, 8, 384)
    cw_full = jnp.transpose(cw_all[:, :3, :352], (1, 0, 2)).reshape(3, D_FF)
    cw8 = _pad_rows(cw_full, 8)

    tabs = _rope_tables(positions.astype(F32).reshape(s, 1) + w_started[4][0, 0])
    h0, h0b = _ln_fwd(xs, None, row(ln_in_g), row(ln_in_b), 1.0, "ln_in_fwd")
    zw, *zg = _proj_rope(h0b, w_in_t, tabs[0])
    oa, lse_a = _banded_fwd(zw, attn_sink, name="win_attn_fwd", **_WIN_CFG)
    og_views, lg_views = [], []
    for gi in range(3):
        o_g, l_g = _banded_fwd(zg[gi], None, name=f"dil_attn_fwd{gi}", **_dil_cfg(gi))
        og_views.append(o_g)
        lg_views.append(l_g)
    mixed, ob_views, lb_views = _mix_norm_fwd(oa, og_views, lg_views, g_win, g_dil)
    packed_thru, land = _exchange_wait(w_started, mixed, gather=True, name="weight_gather_wait")
    gathered = lax.dynamic_update_slice(land, packed_thru, (me_lin * late_r, 0)).reshape(N_DEV, late_r, D_MODEL)
    full = {}
    off = 0
    for n, r in late_rows:
        full[n] = gathered[:, off:off + r, :].reshape(N_DEV * r, D_MODEL)
        off += r
    mix = _mm(mixed, full["w_mix_out"], trans_b=False, out_dtype=F32, name="mm_mix_out")
    h1, h1b = _ln_fwd(h0, mix, ln1_g, ln1_b, ALPHA, "ln1_fwd")
    _, mem_nb = _ln_fwd(mems, None, mem_ln_g, mem_ln_b, 1.0, "mem_ln_fwd")
    kx = _mm(mem_nb, full["w_xk"], trans_b=False, out_dtype=BF16, name="mm_xk")
    vx = _mm(mem_nb, full["w_xv"], trans_b=False, out_dtype=BF16, name="mm_xv")
    qx = _mm(h1b, full["w_xq"], trans_b=False, out_dtype=BF16, name="mm_xq")
    ox, oxb = _xattn_fwd(qx, kx, vx)
    xa = _mm(oxb, full["w_xo"], trans_b=False, out_dtype=F32, name="mm_xo")
    h2, h2b = _ln_fwd(h1, xa, ln2_g, ln2_b, ALPHA, "ln2_fwd")
    gate, up, act = _ffn_fwd(h2b, full["w_gate"], full["w_up"], cw8, conv_b)
    ff = _mm(act, full["w_down"], trans_b=False, out_dtype=F32, name="mm_down")

    du3, du3b, d_ln3_g, d_ln3_b, loss_local = _ln_bwd(h2, ff, target, ln3_g, ln3_b, ALPHA, "ln3_bwd_loss",
                                                      loss_mode=True)
    dw_down = _mm_tn(act, du3b, name="mm_dw_down")
    dgate, dup, dcw8, d_conv_b = _ffn_bwd(du3b, full["w_down"], gate, up, cw8, conv_b)
    dh2 = _mm(dgate, full["w_gate"], trans_b=False, out_dtype=F32, name="mm_dh2", more=((dup, full["w_up"], False),),
              addends=(du3,), coefs=(ALPHA,))
    dw_gate_t = _mm_tn(dgate, h2b, name="mm_dw_gate")
    dw_up_t = _mm_tn(dup, h2b, name="mm_dw_up")
    rows_of = dict(PACK_ROWS)

    def start_grad_exchange(parts, name):
        gp = jnp.concatenate([g.reshape(N_DEV, rows_of[n], D_MODEL) for n, g in parts], axis=1)
        return _exchange_start(gp, (N_DEV - 1,) + gp.shape[1:], gather=False, name=name)

    ffn_parts = (("w_gate", dw_gate_t), ("w_up", dw_up_t), ("w_down", dw_down))
    ffn_started = start_grad_exchange(ffn_parts, "grad_start_ffn")
    du2, du2b, d_ln2_g, d_ln2_b = _ln_bwd(h1, xa, dh2, ln2_g + ffn_started[4][0, 0], ln2_b, ALPHA, "ln2_bwd")
    dox = _mm(du2b, full["w_xo"], trans_b=True, out_dtype=F32, name="mm_d_ox")
    dw_xo = _mm_tn(oxb, du2b, name="mm_dw_xo")
    dqx, dkx, dvx = _xattn_bwd(qx, kx, vx, ox, dox)
    dh1 = _mm(dqx, full["w_xq"], trans_b=True, out_dtype=F32, name="mm_dh1", addends=(du2,), coefs=(ALPHA,))
    dw_xq = _mm_tn(h1b, dqx, name="mm_dw_xq")
    dw_xk = _mm_tn(mem_nb, dkx, name="mm_dw_xk")
    dw_xv = _mm_tn(mem_nb, dvx, name="mm_dw_xv")
    dmem_n = _mm(dkx, full["w_xk"], trans_b=True, out_dtype=F32, name="mm_dmem", more=((dvx, full["w_xv"], True),))
    _, _, d_mem_ln_g, d_mem_ln_b = _ln_bwd(mems, None, dmem_n, mem_ln_g, mem_ln_b, 1.0, "mem_ln_bwd")
    du1, du1b, d_ln1_g, d_ln1_b = _ln_bwd(h0, mix, dh1, ln1_g, ln1_b, ALPHA, "ln1_bwd")
    dmixed = _mm(du1b, full["w_mix_out"], trans_b=True, out_dtype=F32, name="mm_d_mixed")
    dw_mix_out = _mm_tn(mixed, du1b, name="mm_dw_mix_out")
    attn_parts = (("w_mix_out", dw_mix_out), ("w_xq", dw_xq), ("w_xk", dw_xk), ("w_xv", dw_xv), ("w_xo", dw_xo))
    attn_started = start_grad_exchange(attn_parts, "grad_start_attn")
    doa, dob_views, d_g_win, d_g_dil = _mix_norm_bwd(oa, ob_views[0], dmixed, g_win + attn_started[4][0, 0], g_dil)
    dqa, dkva, dsink8 = _banded_bwd(zw, oa, lse_a, doa, tabs[0], attn_sink, kv_heads=_WIN_KV, name="win_attn_bwd",
                                    **_WIN_CFG)
    dq_views, dkv_views = [], []
    for gi in range(3):
        dq_g, dkv_g = _banded_bwd(zg[gi], ob_views[gi], lb_views[gi], dob_views[gi], tabs[gi], None, kv_heads=_DIL_KV,
                                  name=f"dil_attn_bwd{gi}", **_dil_cfg(gi))
        dq_views.append(dq_g)
        dkv_views.append(dkv_g)
    dz = _dz_assemble(dq_views, dkv_views, dqa, dkva)
    dw_in_tz = _mm_tn(dz, h0b, name="mm_dw_in")
    dw_in_t = jnp.concatenate([dw_in_tz[4608:], dw_in_tz[:4608]], axis=0)
    in_parts = (("w_in", dw_in_t),)
    in_started = start_grad_exchange(in_parts, "grad_start_in")
    dh0 = _mm(dz, w_in_t, trans_b=False, out_dtype=F32, name="mm_dh0", addends=(du1,), coefs=(ALPHA,),
              after=in_started[4])
    dx, _, d_ln_in_g, d_ln_in_b = _ln_bwd(xs, None, dh0, row(ln_in_g), row(ln_in_b), 1.0, "ln_in_bwd")

    grads, delta, new_m, new_v = {}, {}, {}, {}
    after = dx
    for parts, started, tag in ((ffn_parts, ffn_started, "ffn"), (attn_parts, attn_started, "attn"),
                                (in_parts, in_started, "in")):
        gp_thru, land = _exchange_wait(started, after, gather=False, name=f"grad_wait_{tag}")
        own = lax.dynamic_index_in_dim(gp_thru, me_lin, axis=0, keepdims=False)
        gsum = _sum_parts(own, land, f"grad_sum_{tag}")
        off = 0
        for n, _ in parts:
            blk = gsum[off:off + rows_of[n]]
            off += rows_of[n]
            grads[n] = (blk.T if n in ("w_in", "w_gate", "w_up") else blk)[None]
            shp = weights[n].shape
            d_, m_, v_ = _adamw(weights[n].reshape(shp[1:]), grads[n].reshape(shp[1:]), mom_m[n].reshape(shp[1:]),
                                mom_v[n].reshape(shp[1:]), f"adamw_{n}")
            delta[n], new_m[n], new_v[n] = d_.reshape(shp), m_.reshape(shp), v_.reshape(shp)
            after = d_

    small = jnp.concatenate([
        d_ln_in_g, d_ln_in_b, d_ln1_g, d_ln1_b, d_mem_ln_g, d_mem_ln_b, d_ln2_g, d_ln2_b, d_ln3_g, d_ln3_b,
        jnp.concatenate([d_g_win, d_g_dil], axis=1),
        jnp.pad(d_conv_b, ((0, 0), (0, 3072 - D_FF))).reshape(3, 1024),
        jnp.pad(dsink8[0:1, :], ((0, 0), (0, 1024 - 128))),
        jnp.pad(dcw8[0:3], ((0, 0), (0, 3072 - D_FF))).reshape(9, 1024),
    ], axis=0)
    _, ssum = _all_gather_rows(small, name="small_grad_all_reduce", in_vmem=True, sum_rows=True, after=after)
    names10 = ["ln_in_g", "ln_in_b", "ln1_g", "ln1_b", "mem_ln_g", "mem_ln_b", "ln2_g", "ln2_b", "ln3_g", "ln3_b"]
    small_g = {n: ssum[i:i + 1] for i, n in enumerate(names10)}
    small_g["g_win"] = ssum[10:11, :512]
    small_g["g_dil"] = ssum[10:11, 512:]
    small_g["conv_b"] = ssum[11:14].reshape(1, 3072)[:, :D_FF]
    small_g["attn_sink"] = ssum[14:15, :8]
    small_g["conv_w"] = lax.dynamic_slice_in_dim(ssum[15:24].reshape(3, 3072)[:, :D_FF], me_lin * 352, 352, axis=1)

    small_names = [n for n in order if n not in rows_of]
    two_d = lambda a: a.reshape(-1, a.shape[-1])
    d_s, m_s, v_s = _adamw_many([two_d(weights[n]) for n in small_names], [small_g[n] for n in small_names],
                                [two_d(mom_m[n]) for n in small_names], [two_d(mom_v[n]) for n in small_names],
                                "adamw_small")
    for k, n in enumerate(small_names):
        shp = weights[n].shape
        grads[n], delta[n], new_m[n], new_v[n] = (t.reshape(shp) for t in (small_g[n], d_s[k], m_s[k], v_s[k]))

    loss = lax.psum(loss_local[0, 0], MESH_AXES)
    return (loss, dx[None], *[grads[n] for n in order], *[delta[n] for n in order], *[new_m[n] for n in order],
            *[new_v[n] for n in order])
```

```python
import functools
import math

import jax
import jax.numpy as jnp
from jax import lax
from jax.experimental import pallas as pl
from jax.experimental.pallas import tpu as pltpu

F32 = jnp.float32
BF16 = jnp.bfloat16
SDS = jax.ShapeDtypeStruct
_PALLAS_CALL = pl.pallas_call

D_MODEL = 1024
HEAD_DIM = 64
WIN_HALF = 128
DIL_PAIRS = ((128, 1), (512, 4), (2048, 16))
DIL_SIDE = 64
ROT_DIM = 16
ROPE_THETA = 500000.0
MEM_LEN = 256
X_HEADS = 4
X_HEAD_DIM = 256
D_FF = 2816
IN_WIDTH = 5376
Z_QB, Z_KB, Z_VB, Z_QA, Z_KA, Z_VA = 0, 1536, 3072, 4608, 5120, 5248
ALPHA = (2.0) ** 0.25
LN_EPS = 1e-5
NEG_INF = -1e30
ADAM_LR, ADAM_B1, ADAM_B2, ADAM_EPS, ADAM_WD, ADAM_STEP = 0.001, 0.9, 0.999, 1e-08, 0.01, 10
N_DEV = 8
MESH_AXES = ("x", "y", "c")
VMEM_LIMIT_BYTES = 52 * 1024 * 1024
ATTN_TQ = 256
TABW = 384

PACK_ROWS = (("w_in", 672), ("w_gate", 352), ("w_up", 352), ("w_mix_out", 128), ("w_xq", 128), ("w_xk", 128),
             ("w_xv", 128), ("w_xo", 128), ("w_down", 352))
SMALL_ROWS = 24


def _pick(n, cands):
    for c in cands:
        if n % c == 0:
            return c
    return n


def _pcall(body, *, name, out_shape, grid=None, in_specs=None, out_specs=None, scratch_shapes=(), dims=None,
           aliases=None):
    kw = {}
    if grid is not None:
        kw["grid"] = grid
    if in_specs is not None:
        kw["in_specs"] = in_specs
    if out_specs is not None:
        kw["out_specs"] = out_specs
    if aliases:
        kw["input_output_aliases"] = aliases
    return _PALLAS_CALL(
        body, name=name, out_shape=out_shape, scratch_shapes=list(scratch_shapes),
        compiler_params=pltpu.CompilerParams(dimension_semantics=dims, vmem_limit_bytes=VMEM_LIMIT_BYTES), **kw)


MM_VMEM_BUDGET = 40 * 1024 * 1024


def _mm(a, b, *, trans_b, out_dtype, name, addends=(), coefs=(), after=None, more=()):
    pairs = ((a, b, trans_b),) + tuple(more)
    m = a.shape[0]
    n = b.shape[0] if trans_b else b.shape[1]
    n_add = len(addends)
    extra = [] if after is None else [after]
    out_bytes = jnp.dtype(out_dtype).itemsize

    def vmem(tm, tn):
        tot = tm * tn * (out_bytes + 4 * n_add)
        for pa, pb, _ in pairs:
            tot += tm * pa.shape[1] * pa.dtype.itemsize + pa.shape[1] * tn * pb.dtype.itemsize
        return 2 * tot

    tm, tn = next(((cm, cn) for cn in (n, 1408, 1024, 512, 256, 128) if n % cn == 0
                   for cm in (1024, 512, 256, 128) if m % cm == 0 and vmem(cm, cn) <= MM_VMEM_BUDGET))
    n_pairs = len(pairs)

    def body(*refs):
        o_ref = refs[2 * n_pairs + n_add + len(extra)]
        acc = None
        for p, (_, _, tb) in enumerate(pairs):
            dn = _NT if tb else _NN
            part = lax.dot_general(refs[2 * p][...].astype(BF16), refs[2 * p + 1][...].astype(BF16), dn,
                                   preferred_element_type=F32)
            acc = part if acc is None else acc + part
        for r_ref, c in zip(refs[2 * n_pairs:2 * n_pairs + n_add], coefs):
            acc = acc + (r_ref[...] if c == 1.0 else c * r_ref[...])
        o_ref[...] = acc.astype(out_dtype)

    in_specs, args = [], []
    for pa, pb, tb in pairs:
        k = pa.shape[1]
        in_specs.append(pl.BlockSpec((tm, k), lambda j, i: (i, 0)))
        in_specs.append(pl.BlockSpec((tn, k), lambda j, i: (j, 0)) if tb else pl.BlockSpec((k, tn), lambda j, i: (0, j)))
        args += [pa, pb]
    in_specs += [pl.BlockSpec((tm, tn), lambda j, i: (i, j)) for _ in addends]
    in_specs += [pl.BlockSpec((8, 128), lambda j, i: (0, 0)) for _ in extra]
    return _pcall(body, name=name, out_shape=SDS((m, n), out_dtype), grid=(n // tn, m // tm), in_specs=in_specs,
                  out_specs=pl.BlockSpec((tm, tn), lambda j, i: (i, j)),
                  dims=("parallel", "parallel"))(*args, *addends, *extra)


def _mm_tn(a, b, *, name):
    s, m = a.shape
    n = b.shape[1]
    tm = _pick(m, (768, 1408, 1024, 512, 256, 128))
    tk = _pick(s, (1024, 512, 256))
    nk = s // tk

    def body(a_ref, b_ref, o_ref, acc_ref):
        kk = pl.program_id(1)

        @pl.when(kk == 0)
        def _():
            acc_ref[...] = jnp.zeros_like(acc_ref)

        acc_ref[...] += lax.dot_general(a_ref[...].astype(BF16), b_ref[...].astype(BF16), (((0,), (0,)), ((), ())),
                                        preferred_element_type=F32)

        @pl.when(kk == nk - 1)
        def _():
            o_ref[...] = acc_ref[...]

    return _pcall(body, name=name, out_shape=SDS((m, n), F32), grid=(m // tm, nk),
                  in_specs=[pl.BlockSpec((tk, tm), lambda i, kk: (kk, i)), pl.BlockSpec((tk, n), lambda i, kk: (kk, 0))],
                  out_specs=pl.BlockSpec((tm, n), lambda i, kk: (i, 0)), scratch_shapes=[pltpu.VMEM((tm, n), F32)],
                  dims=("parallel", "arbitrary"))(a, b)


def _rope_lane_consts():
    lane = jnp.arange(128)
    j = lane % HEAD_DIM
    inv_freq = ROPE_THETA ** (-jnp.arange(0, ROT_DIM, 2, dtype=F32) / ROT_DIM)
    freq = jnp.where(j < ROT_DIM, inv_freq[j % (ROT_DIM // 2)], 0.0).astype(F32)
    lo = (j < ROT_DIM // 2).astype(F32)
    hi = ((j >= ROT_DIM // 2) & (j < ROT_DIM)).astype(F32)
    return jnp.stack([freq, lo, hi] + [jnp.zeros((128,), F32)] * 5)


def _to_classes(x, scr, d):
    if d == 1:
        return [x]
    scr[...] = x
    return [scr[pl.ds(c, x.shape[0] // d, stride=d), :] for c in range(d)]


def _from_classes(parts, scr):
    d = len(parts)
    if d == 1:
        return parts[0]
    for c, part in enumerate(parts):
        scr[pl.ds(c, part.shape[0], stride=d), :] = part
    return scr[...]


DILATIONS = tuple(d for _, d in DIL_PAIRS)


def _rope_tables(posf):
    s = posf.shape[0]
    tm = _pick(s, (1024, 512))

    def body(p_ref, c_ref, *rest):
        o_refs, scr = rest[:-1], rest[-1]
        ang = p_ref[...] * c_ref[0:1, :]
        lo = c_ref[1:2, :]
        hi = c_ref[2:3, :]
        cs = jnp.cos(ang)
        sn = jnp.sin(ang)
        for q, t in enumerate((jnp.where(lo + hi > 0.0, cs, 1.0), -sn * lo, sn * hi)):
            for o_ref, d in zip(o_refs, DILATIONS):
                for c, part in enumerate(_to_classes(t, scr, d)):
                    o_ref[:, c * TABW + q * 128:c * TABW + (q + 1) * 128] = part

    return _pcall(body, name="rope_tables", out_shape=tuple(SDS((s // d, d * TABW), F32) for d in DILATIONS),
                  grid=(s // tm,),
                  in_specs=[pl.BlockSpec((tm, 1), lambda i: (i, 0)), pl.BlockSpec((8, 128), lambda i: (0, 0))],
                  out_specs=tuple(pl.BlockSpec((tm // d, d * TABW), lambda i: (i, 0)) for d in DILATIONS),
                  scratch_shapes=[pltpu.VMEM((tm, 128), F32)], dims=("parallel",))(posf, _rope_lane_consts())


def _rope_apply(x, tab, sign):
    w = x.shape[1]
    rep = w // 128
    c = jnp.tile(tab[:, 0:128], (1, rep)) if rep > 1 else tab[:, 0:128]
    a = jnp.tile(tab[:, 128:256], (1, rep)) if rep > 1 else tab[:, 128:256]
    b = jnp.tile(tab[:, 256:384], (1, rep)) if rep > 1 else tab[:, 256:384]
    up = pltpu.roll(x, w - 8, 1)
    dn = pltpu.roll(x, 8, 1)
    if sign > 0:
        return x * c + up * a + dn * b
    return x * c - up * a - dn * b


def _proj_rope(h0b, w_t, tab):
    s = h0b.shape[0]
    tm = _pick(s, (512,))
    tn = 256

    def body(a_ref, w_ref, t_ref, zw_ref, z0_ref, z1_ref, z2_ref, scr):
        z_refs = (z0_ref, z1_ref, z2_ref)
        a = a_ref[...]
        tabv = t_ref[...]
        for c0 in range(0, IN_WIDTH, tn):
            z = lax.dot_general(a, w_ref[c0:c0 + tn, :], _NT, preferred_element_type=F32)
            for g0 in range(c0, c0 + tn, 128):
                zg = z[:, g0 - c0:g0 - c0 + 128]
                if g0 < Z_VB or Z_QA <= g0 < Z_VA:
                    zg = _rope_apply(zg, tabv, 1)
                if g0 >= Z_QA:
                    zw_ref[:, g0 - Z_QA:g0 - Z_QA + 128] = zg.astype(BF16)
                    continue
                kind, within = divmod(g0, 1536)
                grp, off = divmod(within, 512)
                col = kind * 512 + off
                for c, part in enumerate(_to_classes(zg, scr, DILATIONS[grp])):
                    z_refs[grp][:, c * 1536 + col:c * 1536 + col + 128] = part.astype(BF16)

    return _pcall(body, name="proj_rope",
                  out_shape=(SDS((s, 768), BF16),) + tuple(SDS((s // d, d * 1536), BF16) for d in DILATIONS),
                  grid=(s // tm,),
                  in_specs=[pl.BlockSpec((tm, D_MODEL), lambda i: (i, 0)), pl.BlockSpec((IN_WIDTH, D_MODEL), lambda i: (0, 0)),
                            pl.BlockSpec((tm, TABW), lambda i: (i, 0))],
                  out_specs=(pl.BlockSpec((tm, 768), lambda i: (i, 0)),)
                  + tuple(pl.BlockSpec((tm // d, d * 1536), lambda i: (i, 0)) for d in DILATIONS),
                  scratch_shapes=[pltpu.VMEM((tm, 128), F32)], dims=("parallel",))(h0b, w_t, tab)


def _band_specs(sd, blk, tq, width, per_tok, cb):
    r = tq // blk
    nbk = sd // blk
    prev = pl.BlockSpec((blk, width), lambda c, j: (jnp.maximum(j * r - 1, 0), c * per_tok + cb))
    cur = pl.BlockSpec((tq, width), lambda c, j: (j, c * per_tok + cb))
    nxt = pl.BlockSpec((blk, width), lambda c, j: (jnp.minimum((j + 1) * r, nbk - 1), c * per_tok + cb))
    return [prev, cur, nxt]


def _band_bias(j, blk, tq, sd):
    shape = (tq, tq + 2 * blk)
    qpos = j * tq + lax.broadcasted_iota(jnp.int32, shape, 0)
    kpos = j * tq - blk + lax.broadcasted_iota(jnp.int32, shape, 1)
    ok = (jnp.abs(qpos - kpos) <= blk) & (kpos >= 0) & (kpos < sd)
    return jnp.where(ok, 0.0, NEG_INF)


_NT = (((1,), (1,)), ((), ()))
_NN = (((1,), (0,)), ((), ()))
_TN = (((0,), (0,)), ((), ()))


def _banded_fwd(zv, sink, *, d, blk, tq, ptw, qw, kw, qcb, kcb, vcb, heads, name):
    sd = zv.shape[0]
    tq = min(tq, sd)
    has_sink = sink is not None
    scale = HEAD_DIM ** -0.5

    def body(q_ref, kp, kc, kn, vp, vc, vn, *rest):
        if has_sink:
            sink_ref, o_ref, lse_ref = rest
        else:
            o_ref, lse_ref = rest
        j = pl.program_id(1)
        q = q_ref[...] * scale
        k = jnp.concatenate([kp[...], kc[...], kn[...]], axis=0)
        v = jnp.concatenate([vp[...], vc[...], vn[...]], axis=0)
        bias = _band_bias(j, blk, tq, sd)
        outs, lses = [], []
        for ql, kl, vl, si in heads:
            sc = lax.dot_general(q[:, ql:ql + HEAD_DIM], k[:, kl:kl + HEAD_DIM], _NT, preferred_element_type=F32) + bias
            m = jnp.max(sc, axis=-1, keepdims=True)
            if has_sink:
                m = jnp.maximum(m, sink_ref[0, si])
            p = jnp.exp(sc - m)
            den = jnp.sum(p, axis=-1, keepdims=True)
            if has_sink:
                den = den + jnp.exp(sink_ref[0, si] - m)
            o = lax.dot_general(p.astype(BF16), v[:, vl:vl + HEAD_DIM], _NN, preferred_element_type=F32) / den
            outs.append(o)
            lses.append(jnp.broadcast_to(m + jnp.log(den), (tq, HEAD_DIM)))
        o_ref[...] = jnp.concatenate(outs, axis=1)
        lse_ref[...] = jnp.concatenate(lses, axis=1)

    in_specs = ([pl.BlockSpec((tq, qw), lambda c, j: (j, c * (ptw // qw) + qcb))]
                + _band_specs(sd, blk, tq, kw, ptw // kw, kcb) + _band_specs(sd, blk, tq, kw, ptw // kw, vcb))
    args = [zv] * 7
    if has_sink:
        in_specs.append(pl.BlockSpec(memory_space=pltpu.SMEM))
        args.append(sink)
    o_spec = pl.BlockSpec((tq, qw), lambda c, j: (j, c))
    return _pcall(body, name=name, out_shape=(SDS((sd, d * qw), F32), SDS((sd, d * qw), F32)), grid=(d, sd // tq),
                  in_specs=in_specs, out_specs=(o_spec, o_spec), dims=("parallel", "parallel"))(*args)


def _banded_bwd(zv, ov, lv, dov, tv, sink, *, d, blk, tq, ptw, qw, kw, qcb, kcb, vcb, heads, kv_heads, name):
    sd = zv.shape[0]
    tq = min(tq, sd)
    nt = sd // tq
    r = tq // blk
    nbk = sd // blk
    has_sink = sink is not None
    scale = HEAD_DIM ** -0.5
    kvw = HEAD_DIM * len(kv_heads)

    def add_rows(x, y, last):
        if tq == blk:
            return x + y
        if last:
            return jnp.concatenate([x[:tq - blk], x[tq - blk:] + y], axis=0)
        return jnp.concatenate([x[:blk] + y, x[blk:]], axis=0)

    def body(q_ref, kp, kc, kn, vp, vc, vn, o_ref, l_ref, do_ref, t_ref, tlag_ref, *rest):
        if has_sink:
            sink_ref, dq_ref, dkv_ref, dsink_ref, acck, accv, nxtk, nxtv = rest
        else:
            dq_ref, dkv_ref, acck, accv, nxtk, nxtv = rest
        j = pl.program_id(1)

        @pl.when(j == 0)
        def _():
            nxtk[...] = jnp.zeros_like(nxtk)
            nxtv[...] = jnp.zeros_like(nxtv)

        if has_sink:
            @pl.when((pl.program_id(0) == 0) & (j == 0))
            def _():
                dsink_ref[...] = jnp.zeros_like(dsink_ref)

        def emit(dk_rows, dv_rows):
            dkv_ref[...] = jnp.concatenate([_rope_apply(dk_rows, tlag_ref[...], -1), dv_rows], axis=1).astype(BF16)

        @pl.when(j < nt)
        def _():
            q = q_ref[...] * scale
            k3 = jnp.concatenate([kp[...], kc[...], kn[...]], axis=0)
            v3 = jnp.concatenate([vp[...], vc[...], vn[...]], axis=0)
            o_t, l_t, do_t = o_ref[...], l_ref[...], do_ref[...]
            bias = _band_bias(j, blk, tq, sd)
            dqs = []
            dks = [None] * len(kv_heads)
            dvs = [None] * len(kv_heads)
            dsink_row = jnp.zeros((1, 128), F32)
            lane = lax.broadcasted_iota(jnp.int32, (1, 128), 1)
            for ql, kl, vl, si in heads:
                kvi = kv_heads.index((kl, vl))
                qh = q[:, ql:ql + HEAD_DIM]
                kh3 = k3[:, kl:kl + HEAD_DIM]
                vh3 = v3[:, vl:vl + HEAD_DIM]
                doh = do_t[:, ql:ql + HEAD_DIM]
                delta = jnp.sum(doh * o_t[:, ql:ql + HEAD_DIM], axis=-1, keepdims=True)
                lse = l_t[:, ql:ql + 1]
                dob = doh.astype(BF16)
                sc = lax.dot_general(qh, kh3, _NT, preferred_element_type=F32) + bias
                p = jnp.exp(sc - lse)
                dp = lax.dot_general(dob, vh3, _NT, preferred_element_type=F32)
                dsb = (p * (dp - delta)).astype(BF16)
                dqs.append(lax.dot_general(dsb, kh3, _NN, preferred_element_type=F32) * scale)
                dk = lax.dot_general(dsb, qh, _TN, preferred_element_type=F32)
                dv = lax.dot_general(p.astype(BF16), dob, _TN, preferred_element_type=F32)
                dks[kvi] = dk if dks[kvi] is None else dks[kvi] + dk
                dvs[kvi] = dv if dvs[kvi] is None else dvs[kvi] + dv
                if has_sink:
                    psink = jnp.exp(sink_ref[0, si] - lse)
                    dsink_row = dsink_row + jnp.where(lane == si, -jnp.sum(psink * delta), 0.0)
            dq_ref[...] = _rope_apply(jnp.concatenate(dqs, axis=1), t_ref[...], -1).astype(BF16)
            wk = jnp.concatenate(dks, axis=1) if len(dks) > 1 else dks[0]
            wv = jnp.concatenate(dvs, axis=1) if len(dvs) > 1 else dvs[0]
            if has_sink:
                dsink_ref[0:1, :] += dsink_row

            @pl.when(j > 0)
            def _():
                emit(add_rows(acck[...], wk[:blk], True), add_rows(accv[...], wv[:blk], True))

            acck[...] = add_rows(wk[blk:blk + tq], nxtk[...], False)
            accv[...] = add_rows(wv[blk:blk + tq], nxtv[...], False)
            nxtk[...] = wk[blk + tq:]
            nxtv[...] = wv[blk + tq:]

        @pl.when(j == nt)
        def _():
            emit(acck[...], accv[...])

    def tile(width, per_tok, cb):
        return pl.BlockSpec((tq, width), lambda c, j: (jnp.minimum(j, nt - 1), c * per_tok + cb))

    def halos(width, per_tok, cb):
        before = pl.BlockSpec((blk, width), lambda c, j: (jnp.maximum(jnp.minimum(j, nt - 1) * r - 1, 0), c * per_tok + cb))
        after = pl.BlockSpec((blk, width),
                             lambda c, j: (jnp.minimum((jnp.minimum(j, nt - 1) + 1) * r, nbk - 1), c * per_tok + cb))
        return [before, tile(width, per_tok, cb), after]

    def lagged(width):
        return pl.BlockSpec((tq, width), lambda c, j: (jnp.maximum(j - 1, 0), c))

    in_specs = ([tile(qw, ptw // qw, qcb)] + halos(kw, ptw // kw, kcb) + halos(kw, ptw // kw, vcb)
                + [tile(qw, 1, 0)] * 3 + [tile(TABW, 1, 0), lagged(TABW)])
    args = [zv] * 7 + [ov, lv, dov, tv, tv]
    out_shape = [SDS((sd, d * qw), BF16), SDS((sd, d * 2 * kvw), BF16)]
    out_specs = [tile(qw, 1, 0), lagged(2 * kvw)]
    if has_sink:
        in_specs.append(pl.BlockSpec(memory_space=pltpu.SMEM))
        args.append(sink)
        out_shape.append(SDS((8, 128), F32))
        out_specs.append(pl.BlockSpec((8, 128), lambda c, j: (0, 0)))
    scratch = [pltpu.VMEM((tq, kvw), F32), pltpu.VMEM((tq, kvw), F32), pltpu.VMEM((blk, kvw), F32),
               pltpu.VMEM((blk, kvw), F32)]
    return _pcall(body, name=name, out_shape=tuple(out_shape), grid=(d, nt + 1), in_specs=in_specs,
                  out_specs=tuple(out_specs), scratch_shapes=scratch, dims=("arbitrary", "arbitrary"))(*args)


_WIN_HEADS = tuple((h * HEAD_DIM, (h // 4) * HEAD_DIM, 128 + (h // 4) * HEAD_DIM, h) for h in range(8))
_WIN_KV = ((0, 128), (64, 192))
_WIN_CFG = dict(d=1, blk=WIN_HALF, tq=ATTN_TQ, ptw=768, qw=512, kw=256, qcb=0, kcb=2, vcb=2, heads=_WIN_HEADS)
_DIL_HEADS = tuple((h * HEAD_DIM, h * HEAD_DIM, h * HEAD_DIM, h) for h in range(8))
_DIL_KV = tuple((h * HEAD_DIM, h * HEAD_DIM) for h in range(8))


def _dil_cfg(gi):
    return dict(d=DILATIONS[gi], blk=DIL_SIDE, tq=ATTN_TQ, ptw=1536, qw=512, kw=512, qcb=0, kcb=1, vcb=2,
                heads=_DIL_HEADS)


def _view_specs(tm, width):
    return tuple(pl.BlockSpec((tm // d, d * width), lambda i: (i, 0)) for d in DILATIONS)


def _mix_norm_fwd(oa, og_views, lg_views, g_win, g_dil):
    s = oa.shape[0]
    tm = _pick(s, (512,))

    def body(oa_ref, o0, o1, o2, l0, l1, l2, gw_ref, gd_ref, mixed_ref, ob0, ob1, ob2, lb0, lb1, lb2, scr, ob_s):
        o_refs, l_refs, ob_refs, lb_refs = (o0, o1, o2), (l0, l1, l2), (ob0, ob1, ob2), (lb0, lb1, lb2)
        ssq = jnp.zeros((tm, 1), F32)
        for q in range(4):
            os_, ls_ = [], []
            for g, d in enumerate(DILATIONS):
                cols = [slice(c * 512 + q * 128, c * 512 + (q + 1) * 128) for c in range(d)]
                os_.append(_from_classes([o_refs[g][:, cs] for cs in cols], scr))
                ls_.append(_from_classes([l_refs[g][:, cs] for cs in cols], scr))
            mx = jnp.maximum(jnp.maximum(ls_[0], ls_[1]), ls_[2])
            es = [jnp.exp(l - mx) for l in ls_]
            den = es[0] + es[1] + es[2]
            ob = (es[0] / den) * os_[0] + (es[1] / den) * os_[1] + (es[2] / den) * os_[2]
            lb = mx + jnp.log(den)
            ob_s[:, q * 128:(q + 1) * 128] = ob
            ssq = ssq + jnp.sum(ob * ob, axis=-1, keepdims=True)
            for g, d in enumerate(DILATIONS):
                for val, refs in ((ob, ob_refs), (lb, lb_refs)):
                    for c, part in enumerate(_to_classes(val, scr, d)):
                        refs[g][:, c * 512 + q * 128:c * 512 + (q + 1) * 128] = part
        a = oa_ref[...]
        ra = lax.rsqrt(jnp.mean(a * a, axis=-1, keepdims=True) + LN_EPS)
        rb = lax.rsqrt(ssq * (1.0 / 512) + LN_EPS)
        mixed_ref[...] = jnp.concatenate([a * ra * gw_ref[...], ob_s[...] * rb * gd_ref[...]], axis=1).astype(BF16)

    row = pl.BlockSpec((tm, 512), lambda i: (i, 0))
    vec = pl.BlockSpec((1, 512), lambda i: (0, 0))
    views = _view_specs(tm, 512)
    view_shapes = tuple(SDS((s // d, d * 512), F32) for d in DILATIONS)
    res = _pcall(body, name="mix_norm_fwd", out_shape=(SDS((s, 1024), BF16),) + view_shapes * 2, grid=(s // tm,),
                 in_specs=[row, *views, *views, vec, vec],
                 out_specs=(pl.BlockSpec((tm, 1024), lambda i: (i, 0)),) + views * 2,
                 scratch_shapes=[pltpu.VMEM((tm, 128), F32), pltpu.VMEM((tm, 512), F32)],
                 dims=("parallel",))(oa, *og_views, *lg_views, g_win, g_dil)
    return res[0], res[1:4], res[4:7]


def _mix_norm_bwd(oa, ob, dmixed, g_win, g_dil):
    s = oa.shape[0]
    tm = _pick(s, (512,))
    nt = s // tm

    def body(oa_ref, ob_ref, dm_ref, gw_ref, gd_ref, doa_ref, db0, db1, db2, dgw_ref, dgd_ref, acc_w, acc_d, scr):
        i = pl.program_id(0)

        @pl.when(i == 0)
        def _():
            acc_w[...] = jnp.zeros_like(acc_w)
            acc_d[...] = jnp.zeros_like(acc_d)

        dm = dm_ref[...]
        dxs = []
        for x_ref, g_ref, dy, acc in ((oa_ref, gw_ref, dm[:, :512], acc_w), (ob_ref, gd_ref, dm[:, 512:], acc_d)):
            x = x_ref[...]
            r = lax.rsqrt(jnp.mean(x * x, axis=-1, keepdims=True) + LN_EPS)
            dyg = dy * g_ref[...]
            dxs.append(r * dyg - x * (r * r * r) * jnp.mean(dyg * x, axis=-1, keepdims=True))
            acc[...] += jnp.sum((dy * x * r).reshape(tm // 8, 8, 512), axis=0)
        doa_ref[...] = dxs[0]
        for q in range(4):
            dq = dxs[1][:, q * 128:(q + 1) * 128]
            for db_ref, d in zip((db0, db1, db2), DILATIONS):
                for c, part in enumerate(_to_classes(dq, scr, d)):
                    db_ref[:, c * 512 + q * 128:c * 512 + (q + 1) * 128] = part

        @pl.when(i == nt - 1)
        def _():
            dgw_ref[...] = jnp.sum(acc_w[...], axis=0, keepdims=True)
            dgd_ref[...] = jnp.sum(acc_d[...], axis=0, keepdims=True)

    row = pl.BlockSpec((tm, 512), lambda i: (i, 0))
    vec = pl.BlockSpec((1, 512), lambda i: (0, 0))
    views = _view_specs(tm, 512)
    view_shapes = tuple(SDS((s // d, d * 512), F32) for d in DILATIONS)
    res = _pcall(body, name="mix_norm_bwd",
                 out_shape=(SDS((s, 512), F32),) + view_shapes + (SDS((1, 512), F32), SDS((1, 512), F32)),
                 grid=(nt,), in_specs=[row, row, pl.BlockSpec((tm, 1024), lambda i: (i, 0)), vec, vec],
                 out_specs=(row,) + views + (vec, vec),
                 scratch_shapes=[pltpu.VMEM((8, 512), F32), pltpu.VMEM((8, 512), F32), pltpu.VMEM((tm, 128), F32)],
                 dims=("arbitrary",))(oa, ob, dmixed, g_win, g_dil)
    return res[0], res[1:4], res[4], res[5]


def _dz_assemble(dq_views, dkv_views, dqa, dkva):
    s = dqa.shape[0]
    tm = _pick(s, (512,))

    def body(q0, q1, q2, kv0, kv1, kv2, qa_ref, kva_ref, o_ref, scr):
        for g, d in enumerate(DILATIONS):
            for kind, (ref, width, base) in enumerate((((q0, q1, q2)[g], 512, 0), ((kv0, kv1, kv2)[g], 1024, 0),
                                                       ((kv0, kv1, kv2)[g], 1024, 512))):
                for q in range(4):
                    src = base + q * 128
                    dst = kind * 1536 + g * 512 + q * 128
                    if d == 1:
                        o_ref[:, dst:dst + 128] = ref[:, src:src + 128]
                    else:
                        parts = [ref[:, c * width + src:c * width + src + 128].astype(F32) for c in range(d)]
                        o_ref[:, dst:dst + 128] = _from_classes(parts, scr).astype(BF16)
        o_ref[:, Z_QA:Z_QA + 512] = qa_ref[...]
        o_ref[:, Z_KA:Z_KA + 256] = kva_ref[...]

    return _pcall(body, name="dz_assemble", out_shape=SDS((s, IN_WIDTH), BF16), grid=(s // tm,),
                  in_specs=[*_view_specs(tm, 512), *_view_specs(tm, 1024), pl.BlockSpec((tm, 512), lambda i: (i, 0)),
                            pl.BlockSpec((tm, 256), lambda i: (i, 0))],
                  out_specs=pl.BlockSpec((tm, IN_WIDTH), lambda i: (i, 0)),
                  scratch_shapes=[pltpu.VMEM((tm, 128), F32)], dims=("parallel",))(*dq_views, *dkv_views, dqa, dkva)


def _ln_fwd(a, r, g, b, ca, name):
    s = a.shape[0]
    tm = _pick(s, (512, 256))
    has_r = r is not None

    def body(*refs):
        a_ref = refs[0]
        r_ref = refs[1] if has_r else None
        g_ref, b_ref, o_ref, ob_ref = refs[1 + has_r:]
        u = a_ref[...] if ca == 1.0 else ca * a_ref[...]
        if has_r:
            u = u + r_ref[...]
        mu = jnp.mean(u, axis=-1, keepdims=True)
        xc = u - mu
        var = jnp.mean(xc * xc, axis=-1, keepdims=True)
        y = xc * lax.rsqrt(var + LN_EPS) * g_ref[...] + b_ref[...]
        o_ref[...] = y
        ob_ref[...] = y.astype(BF16)

    row = pl.BlockSpec((tm, D_MODEL), lambda i: (i, 0))
    vec = pl.BlockSpec((1, D_MODEL), lambda i: (0, 0))
    args = [a] + ([r] if has_r else []) + [g, b]
    return _pcall(body, name=name, out_shape=(SDS((s, D_MODEL), F32), SDS((s, D_MODEL), BF16)), grid=(s // tm,),
                  in_specs=[row] * (1 + has_r) + [vec, vec], out_specs=(row, row), dims=("parallel",))(*args)


def _ln_bwd(a, r, dy, g, b, ca, name, loss_mode=False):
    s = a.shape[0]
    tm = _pick(s, (512, 256))
    nt = s // tm
    has_r = r is not None

    def body(*refs):
        a_ref = refs[0]
        r_ref = refs[1] if has_r else None
        dy_ref, g_ref, b_ref = refs[1 + has_r:4 + has_r]
        outs = refs[4 + has_r:]
        if loss_mode:
            du_ref, dub_ref, dg_ref, db_ref, loss_ref, acc_g, acc_b, acc_l = outs
        else:
            du_ref, dub_ref, dg_ref, db_ref, acc_g, acc_b = outs
        i = pl.program_id(0)

        @pl.when(i == 0)
        def _():
            acc_g[...] = jnp.zeros_like(acc_g)
            acc_b[...] = jnp.zeros_like(acc_b)
            if loss_mode:
                acc_l[...] = jnp.zeros_like(acc_l)

        u = a_ref[...] if ca == 1.0 else ca * a_ref[...]
        if has_r:
            u = u + r_ref[...]
        mu = jnp.mean(u, axis=-1, keepdims=True)
        xc = u - mu
        var = jnp.mean(xc * xc, axis=-1, keepdims=True)
        rstd = lax.rsqrt(var + LN_EPS)
        xhat = xc * rstd
        gv = g_ref[...]
        if loss_mode:
            err = (xhat * gv + b_ref[...]) - dy_ref[...]
            acc_l[...] += jnp.sum((err * err).reshape(tm // 8, 8, D_MODEL), axis=0)
            dyv = err * (1.0 / D_MODEL)
        else:
            dyv = dy_ref[...]
        dxh = dyv * gv
        du = rstd * (dxh - jnp.mean(dxh, axis=-1, keepdims=True) - xhat * jnp.mean(dxh * xhat, axis=-1, keepdims=True))
        du_ref[...] = du
        dub_ref[...] = du.astype(BF16)
        acc_g[...] += jnp.sum((dyv * xhat).reshape(tm // 8, 8, D_MODEL), axis=0)
        acc_b[...] += jnp.sum(dyv.reshape(tm // 8, 8, D_MODEL), axis=0)

        @pl.when(i == nt - 1)
        def _():
            dg_ref[...] = jnp.sum(acc_g[...], axis=0, keepdims=True)
            db_ref[...] = jnp.sum(acc_b[...], axis=0, keepdims=True)
            if loss_mode:
                tot = jnp.sum(jnp.sum(acc_l[...], axis=0, keepdims=True), axis=1, keepdims=True)
                loss_ref[...] = tot * (0.5 / D_MODEL)

    row = pl.BlockSpec((tm, D_MODEL), lambda i: (i, 0))
    vec = pl.BlockSpec((1, D_MODEL), lambda i: (0, 0))
    out_shape = [SDS((s, D_MODEL), F32), SDS((s, D_MODEL), BF16), SDS((1, D_MODEL), F32), SDS((1, D_MODEL), F32)]
    out_specs = [row, row, vec, vec]
    scratch = [pltpu.VMEM((8, D_MODEL), F32), pltpu.VMEM((8, D_MODEL), F32)]
    if loss_mode:
        out_shape.append(SDS((1, 1), F32))
        out_specs.append(pl.BlockSpec((1, 1), lambda i: (0, 0)))
        scratch.append(pltpu.VMEM((8, D_MODEL), F32))
    args = [a] + ([r] if has_r else []) + [dy, g, b]
    return _pcall(body, name=name, out_shape=tuple(out_shape), grid=(nt,), in_specs=[row] * (2 + has_r) + [vec, vec],
                  out_specs=tuple(out_specs), scratch_shapes=scratch, dims=("arbitrary",))(*args)


def _mm_ln_fwd(a, w, resid, g, b, ca, name):
    s, k = a.shape
    tm = _pick(s, (512, 256))

    def body(a_ref, w_ref, res_ref, g_ref, b_ref, r_ref, o_ref, ob_ref):
        rv = lax.dot_general(a_ref[...], w_ref[...], _NN, preferred_element_type=F32)
        r_ref[...] = rv
        u = ca * res_ref[...] + rv
        mu = jnp.mean(u, axis=-1, keepdims=True)
        xc = u - mu
        var = jnp.mean(xc * xc, axis=-1, keepdims=True)
        y = xc * lax.rsqrt(var + LN_EPS) * g_ref[...] + b_ref[...]
        o_ref[...] = y
        ob_ref[...] = y.astype(BF16)

    row = pl.BlockSpec((tm, D_MODEL), lambda i: (i, 0))
    vec = pl.BlockSpec((1, D_MODEL), lambda i: (0, 0))
    return _pcall(body, name=name, out_shape=(SDS((s, D_MODEL), F32), SDS((s, D_MODEL), F32), SDS((s, D_MODEL), BF16)),
                  grid=(s // tm,),
                  in_specs=[pl.BlockSpec((tm, k), lambda i: (i, 0)), pl.BlockSpec((k, D_MODEL), lambda i: (0, 0)), row, vec, vec],
                  out_specs=(row, row, row), dims=("parallel",))(a, w, resid, g, b)


def _mm_ln_bwd(pairs, addend, coef, a, r, g, ca, name, after=None):
    s = a.shape[0]
    has_r = r is not None
    extra = [] if after is None else [after]
    n_pairs = len(pairs)

    def vmem(tm):
        tot = tm * D_MODEL * (4 * (2 + has_r) + 6)
        for pa, pb, _ in pairs:
            tot += tm * pa.shape[1] * pa.dtype.itemsize + pb.size * pb.dtype.itemsize
        return 2 * tot

    tm = next(c for c in (512, 256, 128) if s % c == 0 and vmem(c) <= MM_VMEM_BUDGET)
    nt = s // tm

    def body(*refs):
        ins = refs[2 * n_pairs:]
        add_ref, a_ref = ins[0], ins[1]
        r_ref = ins[2] if has_r else None
        g_ref = ins[2 + has_r]
        du_ref, dub_ref, dg_ref, db_ref, acc_g, acc_b = ins[3 + has_r + len(extra):]
        i = pl.program_id(0)

        @pl.when(i == 0)
        def _():
            acc_g[...] = jnp.zeros_like(acc_g)
            acc_b[...] = jnp.zeros_like(acc_b)

        dyv = coef * add_ref[...]
        for p, (_, _, tb) in enumerate(pairs):
            dyv = dyv + lax.dot_general(refs[2 * p][...].astype(BF16), refs[2 * p + 1][...], _NT if tb else _NN,
                                        preferred_element_type=F32)
        u = a_ref[...] if ca == 1.0 else ca * a_ref[...]
        if has_r:
            u = u + r_ref[...]
        mu = jnp.mean(u, axis=-1, keepdims=True)
        xc = u - mu
        var = jnp.mean(xc * xc, axis=-1, keepdims=True)
        rstd = lax.rsqrt(var + LN_EPS)
        xhat = xc * rstd
        dxh = dyv * g_ref[...]
        du = rstd * (dxh - jnp.mean(dxh, axis=-1, keepdims=True) - xhat * jnp.mean(dxh * xhat, axis=-1, keepdims=True))
        du_ref[...] = du
        dub_ref[...] = du.astype(BF16)
        acc_g[...] += jnp.sum((dyv * xhat).reshape(tm // 8, 8, D_MODEL), axis=0)
        acc_b[...] += jnp.sum(dyv.reshape(tm // 8, 8, D_MODEL), axis=0)

        @pl.when(i == nt - 1)
        def _():
            dg_ref[...] = jnp.sum(acc_g[...], axis=0, keepdims=True)
            db_ref[...] = jnp.sum(acc_b[...], axis=0, keepdims=True)

    row = pl.BlockSpec((tm, D_MODEL), lambda i: (i, 0))
    vec = pl.BlockSpec((1, D_MODEL), lambda i: (0, 0))
    in_specs, args = [], []
    for pa, pb, _ in pairs:
        in_specs += [pl.BlockSpec((tm, pa.shape[1]), lambda i: (i, 0)), pl.BlockSpec(pb.shape, lambda i: (0, 0))]
        args += [pa, pb]
    in_specs += [row] * (2 + has_r) + [vec] + [pl.BlockSpec((8, 128), lambda i: (0, 0))] * len(extra)
    args += [addend, a] + ([r] if has_r else []) + [g] + extra
    return _pcall(body, name=name,
                  out_shape=(SDS((s, D_MODEL), F32), SDS((s, D_MODEL), BF16), SDS((1, D_MODEL), F32), SDS((1, D_MODEL), F32)),
                  grid=(nt,), in_specs=in_specs, out_specs=(row, row, vec, vec),
                  scratch_shapes=[pltpu.VMEM((8, D_MODEL), F32), pltpu.VMEM((8, D_MODEL), F32)],
                  dims=("arbitrary",))(*args)


def _xattn_fwd(q, k, v):
    s = q.shape[0]
    tq = _pick(s, (512,))
    scale = X_HEAD_DIM ** -0.5

    def body(q_ref, k_ref, v_ref, o_ref, ob_ref):
        qv, kv, vv = q_ref[...], k_ref[...], v_ref[...]
        outs = []
        for h in range(X_HEADS):
            sl = slice(h * X_HEAD_DIM, (h + 1) * X_HEAD_DIM)
            sc = lax.dot_general(qv[:, sl], kv[:, sl], _NT, preferred_element_type=F32) * scale
            e = jnp.exp(sc - jnp.max(sc, axis=-1, keepdims=True))
            p = e / jnp.sum(e, axis=-1, keepdims=True)
            outs.append(lax.dot_general(p.astype(BF16), vv[:, sl], _NN, preferred_element_type=F32))
        o = jnp.concatenate(outs, axis=1)
        o_ref[...] = o
        ob_ref[...] = o.astype(BF16)

    row = pl.BlockSpec((tq, D_MODEL), lambda i: (i, 0))
    full = pl.BlockSpec((MEM_LEN, D_MODEL), lambda i: (0, 0))
    return _pcall(body, name="xattn_fwd", out_shape=(SDS((s, D_MODEL), F32), SDS((s, D_MODEL), BF16)), grid=(s // tq,),
                  in_specs=[row, full, full], out_specs=(row, row), dims=("parallel",))(q, k, v)


def _xattn_bwd(q, k, v, o, do):
    s = q.shape[0]
    tq = _pick(s, (512,))
    scale = X_HEAD_DIM ** -0.5

    def body(q_ref, k_ref, v_ref, o_ref, do_ref, dq_ref, dk_ref, dv_ref):
        i = pl.program_id(0)

        @pl.when(i == 0)
        def _():
            dk_ref[...] = jnp.zeros_like(dk_ref)
            dv_ref[...] = jnp.zeros_like(dv_ref)

        qv, kv, vv, ov, dov = q_ref[...], k_ref[...], v_ref[...], o_ref[...], do_ref[...]
        dqs, dks, dvs = [], [], []
        for h in range(X_HEADS):
            sl = slice(h * X_HEAD_DIM, (h + 1) * X_HEAD_DIM)
            sc = lax.dot_general(qv[:, sl], kv[:, sl], _NT, preferred_element_type=F32) * scale
            e = jnp.exp(sc - jnp.max(sc, axis=-1, keepdims=True))
            p = e / jnp.sum(e, axis=-1, keepdims=True)
            doh = dov[:, sl]
            dob = doh.astype(BF16)
            delta = jnp.sum(doh * ov[:, sl], axis=-1, keepdims=True)
            dvs.append(lax.dot_general(p.astype(BF16), dob, _TN, preferred_element_type=F32))
            dp = lax.dot_general(dob, vv[:, sl], _NT, preferred_element_type=F32)
            ds = (p * (dp - delta)).astype(BF16)
            dqs.append(lax.dot_general(ds, kv[:, sl], _NN, preferred_element_type=F32) * scale)
            dks.append(lax.dot_general(ds, qv[:, sl], _TN, preferred_element_type=F32) * scale)
        dq_ref[...] = jnp.concatenate(dqs, axis=1).astype(BF16)
        dk_ref[...] += jnp.concatenate(dks, axis=1)
        dv_ref[...] += jnp.concatenate(dvs, axis=1)

    row = pl.BlockSpec((tq, D_MODEL), lambda i: (i, 0))
    full = pl.BlockSpec((MEM_LEN, D_MODEL), lambda i: (0, 0))
    return _pcall(body, name="xattn_bwd",
                  out_shape=(SDS((s, D_MODEL), BF16), SDS((MEM_LEN, D_MODEL), F32), SDS((MEM_LEN, D_MODEL), F32)),
                  grid=(s // tq,), in_specs=[row, full, full, row, row], out_specs=(row, full, full),
                  dims=("arbitrary",))(q, k, v, o, do)


_SQRT_HALF = 0.7071067811865476
_INV_SQRT_2PI = 0.3989422804014327


def _halo_specs(s, tm, width, rows=8):
    nb = s // rows
    r = tm // rows
    prev = pl.BlockSpec((rows, width), lambda i: (jnp.maximum(i * r - 1, 0), 0))
    nxt = pl.BlockSpec((rows, width), lambda i: (jnp.minimum((i + 1) * r, nb - 1), 0))
    return prev, nxt


def _shifted(x, before_row, after_row, i, nt):
    tm = x.shape[0]
    row = lax.broadcasted_iota(jnp.int32, x.shape, 0)
    first = jnp.where(i == 0, 0.0, 1.0) * before_row
    last = jnp.where(i == nt - 1, 0.0, 1.0) * after_row
    xm1 = jnp.where(row == 0, first, pltpu.roll(x, 1, 0))
    xp1 = jnp.where(row == tm - 1, last, pltpu.roll(x, tm - 1, 0))
    return xm1, xp1


BF16_ROWS = 16


def _ffn_fwd(hb, wg_t, wu_t, cw, cb):
    s = hb.shape[0]
    tm = _pick(s, (256,))
    nt = s // tm
    hr = BF16_ROWS

    def body(h_ref, hp_ref, hn_ref, wg_ref, wu_ref, cw_ref, cb_ref, g_ref, up_ref, act_ref):
        i = pl.program_id(0)
        hv = h_ref[...]
        g_ext = lax.dot_general(jnp.concatenate([hp_ref[...], hv, hn_ref[...]], axis=0), wg_ref[...], _NT,
                                preferred_element_type=F32)
        gv = g_ext[hr:hr + tm]
        upv = lax.dot_general(hv, wu_ref[...], _NT, preferred_element_type=F32)
        gm1, gp1 = _shifted(gv, g_ext[hr - 1:hr], g_ext[hr + tm:hr + tm + 1], i, nt)
        gc = gm1 * cw_ref[0:1, :] + gv * cw_ref[1:2, :] + gp1 * cw_ref[2:3, :] + cb_ref[...]
        gelu = 0.5 * gc * (1.0 + lax.erf(gc * _SQRT_HALF))
        g_ref[...] = gv
        up_ref[...] = upv
        act_ref[...] = (gelu * upv).astype(BF16)

    hrow = pl.BlockSpec((tm, D_MODEL), lambda i: (i, 0))
    prev, nxt = _halo_specs(s, tm, D_MODEL, hr)
    wfull = pl.BlockSpec((D_FF, D_MODEL), lambda i: (0, 0))
    row = pl.BlockSpec((tm, D_FF), lambda i: (i, 0))
    return _pcall(body, name="ffn_fwd", out_shape=(SDS((s, D_FF), F32), SDS((s, D_FF), F32), SDS((s, D_FF), BF16)),
                  grid=(nt,), in_specs=[hrow, prev, nxt, wfull, wfull, pl.BlockSpec((8, D_FF), lambda i: (0, 0)),
                                        pl.BlockSpec((1, D_FF), lambda i: (0, 0))],
                  out_specs=(row, row, row), dims=("parallel",))(hb, hb, hb, wg_t, wu_t, cw, cb)


def _ffn_bwd(dffb, w_down, g, up, cw, cb):
    s = g.shape[0]
    tm = _pick(s, (256,))
    nt = s // tm
    hr = BF16_ROWS

    def body(df_ref, dfp_ref, dfn_ref, wd_ref, g_ref, gp_ref, gn_ref, up_ref, upp_ref, upn_ref, cw_ref, cb_ref,
             dg_ref, dup_ref, dcw_ref, dcb_ref, a0, a1, a2, a3):
        i = pl.program_id(0)
        da_ext = lax.dot_general(jnp.concatenate([dfp_ref[...], df_ref[...], dfn_ref[...]], axis=0), wd_ref[...], _NT,
                                 preferred_element_type=F32)

        @pl.when(i == 0)
        def _():
            for a in (a0, a1, a2, a3):
                a[...] = jnp.zeros_like(a)

        cw0, cw1, cw2, cbv = cw_ref[0:1, :], cw_ref[1:2, :], cw_ref[2:3, :], cb_ref[...]

        def d_conv_out(gc_, up_, da_):
            cdf_ = 0.5 * (1.0 + lax.erf(gc_ * _SQRT_HALF))
            pdf_ = jnp.exp(-0.5 * gc_ * gc_) * _INV_SQRT_2PI
            return da_ * up_ * (cdf_ + gc_ * pdf_), cdf_

        gv = g_ref[...]
        g_before, g_after = gp_ref[...], gn_ref[...]
        gm1, gp1 = _shifted(gv, g_before[7:8, :], g_after[0:1, :], i, nt)
        gc = gm1 * cw0 + gv * cw1 + gp1 * cw2 + cbv
        da = da_ext[hr:hr + tm]
        dgc, cdf = d_conv_out(gc, up_ref[...], da)
        dup_ref[...] = (da * (gc * cdf)).astype(BF16)
        gc_b = g_before[6:7, :] * cw0 + g_before[7:8, :] * cw1 + gv[0:1, :] * cw2 + cbv
        gc_a = gv[tm - 1:tm, :] * cw0 + g_after[0:1, :] * cw1 + g_after[1:2, :] * cw2 + cbv
        dgc_b = jnp.where(i == 0, 0.0, 1.0) * d_conv_out(gc_b, upp_ref[7:8, :], da_ext[hr - 1:hr])[0]
        dgc_a = jnp.where(i == nt - 1, 0.0, 1.0) * d_conv_out(gc_a, upn_ref[0:1, :], da_ext[hr + tm:hr + tm + 1])[0]
        row = lax.broadcasted_iota(jnp.int32, dgc.shape, 0)
        dgc_m1 = jnp.where(row == 0, dgc_b, pltpu.roll(dgc, 1, 0))
        dgc_p1 = jnp.where(row == tm - 1, dgc_a, pltpu.roll(dgc, tm - 1, 0))
        dg_ref[...] = (dgc_p1 * cw0 + dgc * cw1 + dgc_m1 * cw2).astype(BF16)

        def fold(t):
            return jnp.sum(t.reshape(tm // 8, 8, D_FF), axis=0)

        a0[...] += fold(dgc * gm1)
        a1[...] += fold(dgc * gv)
        a2[...] += fold(dgc * gp1)
        a3[...] += fold(dgc)

        @pl.when(i == nt - 1)
        def _():
            dcw_ref[...] = jnp.concatenate(
                [jnp.sum(a[...], axis=0, keepdims=True) for a in (a0, a1, a2)] + [jnp.zeros((5, D_FF), F32)], axis=0)
            dcb_ref[...] = jnp.sum(a3[...], axis=0, keepdims=True)

    row = pl.BlockSpec((tm, D_FF), lambda i: (i, 0))
    prev, nxt = _halo_specs(s, tm, D_FF)
    cw_spec = pl.BlockSpec((8, D_FF), lambda i: (0, 0))
    cb_spec = pl.BlockSpec((1, D_FF), lambda i: (0, 0))
    dprev, dnxt = _halo_specs(s, tm, D_MODEL, hr)
    return _pcall(body, name="ffn_bwd",
                  out_shape=(SDS((s, D_FF), BF16), SDS((s, D_FF), BF16), SDS((8, D_FF), F32), SDS((1, D_FF), F32)),
                  grid=(nt,),
                  in_specs=[pl.BlockSpec((tm, D_MODEL), lambda i: (i, 0)), dprev, dnxt,
                            pl.BlockSpec((D_FF, D_MODEL), lambda i: (0, 0))] + [row, prev, nxt] * 2 + [cw_spec, cb_spec],
                  out_specs=(row, row, cw_spec, cb_spec), scratch_shapes=[pltpu.VMEM((8, D_FF), F32)] * 4,
                  dims=("arbitrary",))(dffb, dffb, dffb, w_down, g, g, g, up, up, up, cw, cb)


def _adamw(w, g, m, v, name):
    rows, cols = w.shape
    tr = _pick(rows, (256, 128, 64, 32, 16, 8))
    c1 = 1.0 - ADAM_B1 ** ADAM_STEP
    c2 = 1.0 - ADAM_B2 ** ADAM_STEP

    def body(w_ref, g_ref, m_ref, v_ref, d_ref, nm_ref, nv_ref):
        gv = g_ref[...]
        nm = ADAM_B1 * m_ref[...] + (1.0 - ADAM_B1) * gv
        nv = ADAM_B2 * v_ref[...] + (1.0 - ADAM_B2) * (gv * gv)
        d_ref[...] = -ADAM_LR * ((nm / c1) / (jnp.sqrt(nv / c2) + ADAM_EPS) + ADAM_WD * w_ref[...])
        nm_ref[...] = nm
        nv_ref[...] = nv

    blk = pl.BlockSpec((tr, cols), lambda i: (i, 0))
    return _pcall(body, name=name, out_shape=(SDS(w.shape, F32),) * 3, grid=(rows // tr,), in_specs=[blk] * 4,
                  out_specs=(blk,) * 3, dims=("parallel",))(w, g, m, v)


def _adamw_many(ws, gs, ms, vs, name):
    n = len(ws)
    c1 = 1.0 - ADAM_B1 ** ADAM_STEP
    c2 = 1.0 - ADAM_B2 ** ADAM_STEP

    def body(*refs):
        outs = refs[4 * n:]
        for k in range(n):
            gv = refs[n + k][...]
            nm = ADAM_B1 * refs[2 * n + k][...] + (1.0 - ADAM_B1) * gv
            nv = ADAM_B2 * refs[3 * n + k][...] + (1.0 - ADAM_B2) * (gv * gv)
            outs[k][...] = -ADAM_LR * ((nm / c1) / (jnp.sqrt(nv / c2) + ADAM_EPS) + ADAM_WD * refs[k][...])
            outs[n + k][...] = nm
            outs[2 * n + k][...] = nv

    shapes = tuple(SDS(w.shape, F32) for w in ws)
    res = _pcall(body, name=name, out_shape=shapes * 3)(*ws, *gs, *ms, *vs)
    return res[:n], res[n:2 * n], res[2 * n:]


def _all_gather_rows(x_shard, *, name, in_vmem, sum_rows=False, after=None):
    m_per, n = x_shard.shape
    extra = [] if after is None else [after]

    def body(x_ref, *rest):
        out_ref, rest = rest[len(extra)], rest[len(extra) + 1:]
        if sum_rows:
            sum_ref, send_sems, recv_sems, local_sem = rest
        else:
            send_sems, recv_sems, local_sem = rest
        x, y, c = lax.axis_index("x"), lax.axis_index("y"), lax.axis_index("c")
        me, sibling = (x, y, c), (x, y, 1 - c)
        chips = [(1 - x, y), (x, 1 - y), (1 - x, 1 - y)]

        def rows(px, py, pc):
            return out_ref.at[pl.ds((4 * px + 2 * py + pc) * m_per, m_per), :]

        def copy(k, block, to, src=None):
            return pltpu.make_async_remote_copy(
                src_ref=rows(*block) if src is None else src, dst_ref=rows(*block), send_sem=send_sems.at[k],
                recv_sem=recv_sems.at[k], device_id=to, device_id_type=pl.DeviceIdType.MESH)

        mine = pltpu.make_async_copy(x_ref, rows(*me), local_sem)
        mine.start()
        first = [copy(0, me, sibling, src=x_ref)]
        first += [copy(1 + j, me, (*chip, c), src=x_ref) for j, chip in enumerate(chips)]
        for cp in first:
            cp.start()
        passed = [copy(4 + j, (*chip, c), sibling) for j, chip in enumerate(chips)]
        for j, chip in enumerate(chips):
            copy(1 + j, (*chip, c), me).wait_recv()
            passed[j].start()
        copy(0, sibling, me).wait_recv()
        for j, chip in enumerate(chips):
            copy(4 + j, (*chip, 1 - c), me).wait_recv()
        for cp in first + passed:
            cp.wait_send()
        mine.wait()
        if sum_rows:
            acc = out_ref[0:m_per, :]
            for dev in range(1, N_DEV):
                acc = acc + out_ref[dev * m_per:(dev + 1) * m_per, :]
            sum_ref[...] = acc

    space = pltpu.VMEM if in_vmem else pl.ANY
    out_shape = [SDS((N_DEV * m_per, n), x_shard.dtype)]
    out_specs = [pl.BlockSpec(memory_space=space)]
    if sum_rows:
        out_shape.append(SDS((m_per, n), x_shard.dtype))
        out_specs.append(pl.BlockSpec(memory_space=pltpu.VMEM))
    res = _PALLAS_CALL(
        body, name=name, out_shape=tuple(out_shape),
        in_specs=[pl.BlockSpec(memory_space=space)] + [pl.BlockSpec(memory_space=pl.ANY)] * len(extra),
        out_specs=tuple(out_specs),
        scratch_shapes=[pltpu.SemaphoreType.DMA((7,)), pltpu.SemaphoreType.DMA((7,)), pltpu.SemaphoreType.DMA],
        compiler_params=pltpu.CompilerParams(vmem_limit_bytes=VMEM_LIMIT_BYTES),
    )(x_shard, *extra)
    return res if sum_rows else res[0]


_HBM = pl.BlockSpec(memory_space=pltpu.HBM)
_SEM = pl.BlockSpec(memory_space=pltpu.SEMAPHORE)
_SPLIT_PARAMS = dict(has_side_effects=pltpu.SideEffectType.DATAFLOW_SIDE_EFFECTING)


def _split_copies(src_ref, land_ref, send_sems, recv_sems, gather):
    x, y, c = lax.axis_index("x"), lax.axis_index("y"), lax.axis_index("c")
    copies = []
    for k in range(1, N_DEV):
        px = 1 - x if k & 4 else x
        py = 1 - y if k & 2 else y
        pc = 1 - c if k & 1 else c
        if gather:
            rows = src_ref.shape[0]
            src, dst = src_ref, land_ref.at[pl.ds((4 * x + 2 * y + c) * rows, rows), :]
        else:
            src, dst = src_ref.at[4 * px + 2 * py + pc], land_ref.at[k - 1]
        copies.append(pltpu.make_async_remote_copy(
            src_ref=src, dst_ref=dst, send_sem=send_sems.at[k - 1], recv_sem=recv_sems.at[k - 1],
            device_id=(px, py, pc), device_id_type=pl.DeviceIdType.MESH))
    return copies


def _exchange_start(src, land_shape, *, gather, name):
    def body(src_ref, land_ref, send_sems, recv_sems, src_thru, land_thru, token):
        for cp in _split_copies(src_ref, land_ref, send_sems, recv_sems, gather):
            cp.start()
        token[...] = jnp.zeros_like(token)

    land = pltpu.with_memory_space_constraint(lax.empty(land_shape, src.dtype), pltpu.HBM)
    return _PALLAS_CALL(
        body, name=name,
        out_shape=(pltpu.SemaphoreType.DMA((N_DEV - 1,)), pltpu.SemaphoreType.DMA((N_DEV - 1,)),
                   pltpu.HBM(src.shape, src.dtype), pltpu.HBM(land_shape, src.dtype), SDS((8, 128), F32)),
        in_specs=(_HBM, _HBM), out_specs=(_SEM, _SEM, _HBM, _HBM, pl.BlockSpec(memory_space=pltpu.VMEM)),
        input_output_aliases={0: 2, 1: 3}, compiler_params=pltpu.CompilerParams(**_SPLIT_PARAMS),
    )(pltpu.with_memory_space_constraint(src, pltpu.HBM), land)


def _exchange_wait(started, after, *, gather, name):
    send_sems, recv_sems, src_thru, land_thru, _ = started

    def body(src_ref, land_ref, send_sems, recv_sems, after_ref, src_out, land_out):
        copies = _split_copies(src_ref, land_ref, send_sems, recv_sems, gather)
        for cp in copies:
            cp.wait_send()
        for cp in copies:
            cp.wait_recv()

    return _PALLAS_CALL(
        body, name=name,
        out_shape=(pltpu.HBM(src_thru.shape, src_thru.dtype), pltpu.HBM(land_thru.shape, land_thru.dtype)),
        in_specs=(_HBM, _HBM, _SEM, _SEM, pl.BlockSpec(memory_space=pl.ANY)), out_specs=(_HBM, _HBM),
        input_output_aliases={0: 0, 1: 1}, compiler_params=pltpu.CompilerParams(**_SPLIT_PARAMS),
    )(src_thru, land_thru, send_sems, recv_sems, after)


def _sum_parts(own, land, name):
    r, n = own.shape
    tr = _pick(r, (264, 320, 336, 128, 64, 32, 16, 8))

    def body(own_ref, x_ref, o_ref):
        acc = own_ref[...]
        for k in range(N_DEV - 1):
            acc = acc + x_ref[k]
        o_ref[...] = acc

    return _pcall(body, name=name, out_shape=SDS((r, n), F32), grid=(r // tr,),
                  in_specs=[pl.BlockSpec((tr, n), lambda i: (i, 0)), pl.BlockSpec((N_DEV - 1, tr, n), lambda i: (0, i, 0))],
                  out_specs=pl.BlockSpec((tr, n), lambda i: (i, 0)), dims=("parallel",))(own, land)


def _pad_rows(a, rows):
    return jnp.pad(a, ((0, rows - a.shape[0]), (0, 0)))


def kernel(x, mem, positions, ln_in_g, ln_in_b, w_in, attn_sink, g_win, g_dil, w_mix_out, ln1_g, ln1_b, mem_ln_g, mem_ln_b, w_xq, w_xk, w_xv, w_xo, ln2_g, ln2_b, w_gate, w_up, conv_w, conv_b, w_down, ln3_g, ln3_b, loss_target, m_ln_in_g, m_ln_in_b, m_w_in, m_attn_sink, m_g_win, m_g_dil, m_w_mix_out, m_ln1_g, m_ln1_b, m_mem_ln_g, m_mem_ln_b, m_w_xq, m_w_xk, m_w_xv, m_w_xo, m_ln2_g, m_ln2_b, m_w_gate, m_w_up, m_conv_w, m_conv_b, m_w_down, m_ln3_g, m_ln3_b, v_ln_in_g, v_ln_in_b, v_w_in, v_attn_sink, v_g_win, v_g_dil, v_w_mix_out, v_ln1_g, v_ln1_b, v_mem_ln_g, v_mem_ln_b, v_w_xq, v_w_xk, v_w_xv, v_w_xo, v_ln2_g, v_ln2_b, v_w_gate, v_w_up, v_conv_w, v_conv_b, v_w_down, v_ln3_g, v_ln3_b):
    weights = dict(ln_in_g=ln_in_g, ln_in_b=ln_in_b, w_in=w_in, attn_sink=attn_sink, g_win=g_win, g_dil=g_dil, w_mix_out=w_mix_out, ln1_g=ln1_g, ln1_b=ln1_b, mem_ln_g=mem_ln_g, mem_ln_b=mem_ln_b, w_xq=w_xq, w_xk=w_xk, w_xv=w_xv, w_xo=w_xo, ln2_g=ln2_g, ln2_b=ln2_b, w_gate=w_gate, w_up=w_up, conv_w=conv_w, conv_b=conv_b, w_down=w_down, ln3_g=ln3_g, ln3_b=ln3_b)
    mom_m = dict(ln_in_g=m_ln_in_g, ln_in_b=m_ln_in_b, w_in=m_w_in, attn_sink=m_attn_sink, g_win=m_g_win, g_dil=m_g_dil, w_mix_out=m_w_mix_out, ln1_g=m_ln1_g, ln1_b=m_ln1_b, mem_ln_g=m_mem_ln_g, mem_ln_b=m_mem_ln_b, w_xq=m_w_xq, w_xk=m_w_xk, w_xv=m_w_xv, w_xo=m_w_xo, ln2_g=m_ln2_g, ln2_b=m_ln2_b, w_gate=m_w_gate, w_up=m_w_up, conv_w=m_conv_w, conv_b=m_conv_b, w_down=m_w_down, ln3_g=m_ln3_g, ln3_b=m_ln3_b)
    mom_v = dict(ln_in_g=v_ln_in_g, ln_in_b=v_ln_in_b, w_in=v_w_in, attn_sink=v_attn_sink, g_win=v_g_win, g_dil=v_g_dil, w_mix_out=v_w_mix_out, ln1_g=v_ln1_g, ln1_b=v_ln1_b, mem_ln_g=v_mem_ln_g, mem_ln_b=v_mem_ln_b, w_xq=v_w_xq, w_xk=v_w_xk, w_xv=v_w_xv, w_xo=v_w_xo, ln2_g=v_ln2_g, ln2_b=v_ln2_b, w_gate=v_w_gate, w_up=v_w_up, conv_w=v_conv_w, conv_b=v_conv_b, w_down=v_w_down, ln3_g=v_ln3_g, ln3_b=v_ln3_b)
    order = list(weights)
    s = x.shape[1]
    xs = x[0]
    mems = mem[0]
    target = loss_target[0]
    row = lambda a: a.reshape(1, -1)

    shard_rows = dict(w_in=w_in[0].T, w_gate=w_gate[0].T, w_up=w_up[0].T, w_mix_out=w_mix_out[0], w_xq=w_xq[0],
                      w_xk=w_xk[0], w_xv=w_xv[0], w_xo=w_xo[0], w_down=w_down[0])
    me_lin = 4 * lax.axis_index("x") + 2 * lax.axis_index("y") + lax.axis_index("c")
    w_in_full = _all_gather_rows(shard_rows["w_in"].astype(BF16), name="w_in_all_gather", in_vmem=False)
    w_in_t = jnp.concatenate([w_in_full[768:], w_in_full[:768]], axis=0)
    late_rows = PACK_ROWS[1:]
    late_r = sum(r for _, r in late_rows)
    packed = jnp.concatenate([shard_rows[n].astype(BF16) for n, _ in late_rows], axis=0)
    w_started = _exchange_start(packed, (N_DEV * late_r, D_MODEL), gather=True, name="weight_gather_start")
    cw_pad = jnp.pad(conv_w[0], ((0, 5), (0, 32)))
    cw_all = _all_gather_rows(cw_pad, name="conv_w_all_gather", in_vmem=True).reshape(N_DEV, 8, 384)
    cw_full = jnp.transpose(cw_all[:, :3, :352], (1, 0, 2)).reshape(3, D_FF)
    cw8 = _pad_rows(cw_full, 8)

    tabs = _rope_tables(positions.astype(F32).reshape(s, 1) + w_started[4][0, 0])
    h0, h0b = _ln_fwd(xs, None, row(ln_in_g), row(ln_in_b), 1.0, "ln_in_fwd")
    zw, *zg = _proj_rope(h0b, w_in_t, tabs[0])
    oa, lse_a = _banded_fwd(zw, attn_sink, name="win_attn_fwd", **_WIN_CFG)
    og_views, lg_views = [], []
    for gi in range(3):
        o_g, l_g = _banded_fwd(zg[gi], None, name=f"dil_attn_fwd{gi}", **_dil_cfg(gi))
        og_views.append(o_g)
        lg_views.append(l_g)
    mixed, ob_views, lb_views = _mix_norm_fwd(oa, og_views, lg_views, g_win, g_dil)
    packed_thru, land = _exchange_wait(w_started, mixed, gather=True, name="weight_gather_wait")
    gathered = lax.dynamic_update_slice(land, packed_thru, (me_lin * late_r, 0)).reshape(N_DEV, late_r, D_MODEL)
    full = {}
    off = 0
    for n, r in late_rows:
        full[n] = gathered[:, off:off + r, :].reshape(N_DEV * r, D_MODEL)
        off += r
    mix, h1, h1b = _mm_ln_fwd(mixed, full["w_mix_out"], h0, ln1_g, ln1_b, ALPHA, "mm_mix_out_ln1")
    _, mem_nb = _ln_fwd(mems, None, mem_ln_g, mem_ln_b, 1.0, "mem_ln_fwd")
    kx = _mm(mem_nb, full["w_xk"], trans_b=False, out_dtype=BF16, name="mm_xk")
    vx = _mm(mem_nb, full["w_xv"], trans_b=False, out_dtype=BF16, name="mm_xv")
    qx = _mm(h1b, full["w_xq"], trans_b=False, out_dtype=BF16, name="mm_xq")
    ox, oxb = _xattn_fwd(qx, kx, vx)
    xa, h2, h2b = _mm_ln_fwd(oxb, full["w_xo"], h1, ln2_g, ln2_b, ALPHA, "mm_xo_ln2")
    gate, up, act = _ffn_fwd(h2b, full["w_gate"], full["w_up"], cw8, conv_b)
    ff = _mm(act, full["w_down"], trans_b=False, out_dtype=F32, name="mm_down")

    du3, du3b, d_ln3_g, d_ln3_b, loss_local = _ln_bwd(h2, ff, target, ln3_g, ln3_b, ALPHA, "ln3_bwd_loss",
                                                      loss_mode=True)
    dw_down = _mm_tn(act, du3b, name="mm_dw_down")
    dgate, dup, dcw8, d_conv_b = _ffn_bwd(du3b, full["w_down"], gate, up, cw8, conv_b)
    dw_gate_t = _mm_tn(dgate, h2b, name="mm_dw_gate")
    dw_up_t = _mm_tn(dup, h2b, name="mm_dw_up")
    rows_of = dict(PACK_ROWS)

    def start_grad_exchange(parts, name):
        gp = jnp.concatenate([g.reshape(N_DEV, rows_of[n], D_MODEL) for n, g in parts], axis=1)
        return _exchange_start(gp, (N_DEV - 1,) + gp.shape[1:], gather=False, name=name)

    ffn_parts = (("w_gate", dw_gate_t), ("w_up", dw_up_t), ("w_down", dw_down))
    ffn_started = start_grad_exchange(ffn_parts, "grad_start_ffn")
    du2, du2b, d_ln2_g, d_ln2_b = _mm_ln_bwd(((dgate, full["w_gate"], False), (dup, full["w_up"], False)), du3, ALPHA,
                                             h1, xa, ln2_g + ffn_started[4][0, 0], ALPHA, "mm_dh2_ln2_bwd")
    dox = _mm(du2b, full["w_xo"], trans_b=True, out_dtype=F32, name="mm_d_ox")
    dw_xo = _mm_tn(oxb, du2b, name="mm_dw_xo")
    dqx, dkx, dvx = _xattn_bwd(qx, kx, vx, ox, dox)
    dw_xq = _mm_tn(h1b, dqx, name="mm_dw_xq")
    dw_xk = _mm_tn(mem_nb, dkx, name="mm_dw_xk")
    dw_xv = _mm_tn(mem_nb, dvx, name="mm_dw_xv")
    dmem_n = _mm(dkx, full["w_xk"], trans_b=True, out_dtype=F32, name="mm_dmem", more=((dvx, full["w_xv"], True),))
    _, _, d_mem_ln_g, d_mem_ln_b = _ln_bwd(mems, None, dmem_n, mem_ln_g, mem_ln_b, 1.0, "mem_ln_bwd")
    du1, du1b, d_ln1_g, d_ln1_b = _mm_ln_bwd(((dqx, full["w_xq"], True),), du2, ALPHA, h0, mix, ln1_g, ALPHA,
                                             "mm_dh1_ln1_bwd")
    dmixed = _mm(du1b, full["w_mix_out"], trans_b=True, out_dtype=F32, name="mm_d_mixed")
    dw_mix_out = _mm_tn(mixed, du1b, name="mm_dw_mix_out")
    attn_parts = (("w_mix_out", dw_mix_out), ("w_xq", dw_xq), ("w_xk", dw_xk), ("w_xv", dw_xv), ("w_xo", dw_xo))
    attn_started = start_grad_exchange(attn_parts, "grad_start_attn")
    doa, dob_views, d_g_win, d_g_dil = _mix_norm_bwd(oa, ob_views[0], dmixed, g_win + attn_started[4][0, 0], g_dil)
    dqa, dkva, dsink8 = _banded_bwd(zw, oa, lse_a, doa, tabs[0], attn_sink, kv_heads=_WIN_KV, name="win_attn_bwd",
                                    **_WIN_CFG)
    dq_views, dkv_views = [], []
    for gi in range(3):
        dq_g, dkv_g = _banded_bwd(zg[gi], ob_views[gi], lb_views[gi], dob_views[gi], tabs[gi], None, kv_heads=_DIL_KV,
                                  name=f"dil_attn_bwd{gi}", **_dil_cfg(gi))
        dq_views.append(dq_g)
        dkv_views.append(dkv_g)
    dz = _dz_assemble(dq_views, dkv_views, dqa, dkva)
    dw_in_tz = _mm_tn(dz, h0b, name="mm_dw_in")
    dw_in_t = jnp.concatenate([dw_in_tz[4608:], dw_in_tz[:4608]], axis=0)
    in_parts = (("w_in", dw_in_t),)
    in_started = start_grad_exchange(in_parts, "grad_start_in")
    dx, _, d_ln_in_g, d_ln_in_b = _mm_ln_bwd(((dz, w_in_t, False),), du1, ALPHA, xs, None, row(ln_in_g), 1.0,
                                             "mm_dh0_ln_in_bwd", after=in_started[4])

    grads, delta, new_m, new_v = {}, {}, {}, {}
    after = dx
    for parts, started, tag in ((ffn_parts, ffn_started, "ffn"), (attn_parts, attn_started, "attn"),
                                (in_parts, in_started, "in")):
        gp_thru, land = _exchange_wait(started, after, gather=False, name=f"grad_wait_{tag}")
        own = lax.dynamic_index_in_dim(gp_thru, me_lin, axis=0, keepdims=False)
        gsum = _sum_parts(own, land, f"grad_sum_{tag}")
        off = 0
        for n, _ in parts:
            blk = gsum[off:off + rows_of[n]]
            off += rows_of[n]
            grads[n] = (blk.T if n in ("w_in", "w_gate", "w_up") else blk)[None]
            shp = weights[n].shape
            d_, m_, v_ = _adamw(weights[n].reshape(shp[1:]), grads[n].reshape(shp[1:]), mom_m[n].reshape(shp[1:]),
                                mom_v[n].reshape(shp[1:]), f"adamw_{n}")
            delta[n], new_m[n], new_v[n] = d_.reshape(shp), m_.reshape(shp), v_.reshape(shp)
            after = d_

    small = jnp.concatenate([
        d_ln_in_g, d_ln_in_b, d_ln1_g, d_ln1_b, d_mem_ln_g, d_mem_ln_b, d_ln2_g, d_ln2_b, d_ln3_g, d_ln3_b,
        jnp.concatenate([d_g_win, d_g_dil], axis=1),
        jnp.pad(d_conv_b, ((0, 0), (0, 3072 - D_FF))).reshape(3, 1024),
        jnp.pad(dsink8[0:1, :], ((0, 0), (0, 1024 - 128))),
        jnp.pad(dcw8[0:3], ((0, 0), (0, 3072 - D_FF))).reshape(9, 1024),
    ], axis=0)
    _, ssum = _all_gather_rows(small, name="small_grad_all_reduce", in_vmem=True, sum_rows=True, after=after)
    names10 = ["ln_in_g", "ln_in_b", "ln1_g", "ln1_b", "mem_ln_g", "mem_ln_b", "ln2_g", "ln2_b", "ln3_g", "ln3_b"]
    small_g = {n: ssum[i:i + 1] for i, n in enumerate(names10)}
    small_g["g_win"] = ssum[10:11, :512]
    small_g["g_dil"] = ssum[10:11, 512:]
    small_g["conv_b"] = ssum[11:14].reshape(1, 3072)[:, :D_FF]
    small_g["attn_sink"] = ssum[14:15, :8]
    small_g["conv_w"] = lax.dynamic_slice_in_dim(ssum[15:24].reshape(3, 3072)[:, :D_FF], me_lin * 352, 352, axis=1)

    small_names = [n for n in order if n not in rows_of]
    two_d = lambda a: a.reshape(-1, a.shape[-1])
    d_s, m_s, v_s = _adamw_many([two_d(weights[n]) for n in small_names], [small_g[n] for n in small_names],
                                [two_d(mom_m[n]) for n in small_names], [two_d(mom_v[n]) for n in small_names],
                                "adamw_small")
    for k, n in enumerate(small_names):
        shp = weights[n].shape
        grads[n], delta[n], new_m[n], new_v[n] = (t.reshape(shp) for t in (small_g[n], d_s[k], m_s[k], v_s[k]))

    loss = lax.psum(loss_local[0, 0], MESH_AXES)
    return (loss, dx[None], *[grads[n] for n in order], *[delta[n] for n in order], *[new_m[n] for n in order],
            *[new_v[n] for n in order])
```

```python
import functools
import math

import jax
import jax.numpy as jnp
from jax import lax
from jax.experimental import pallas as pl
from jax.experimental.pallas import tpu as pltpu

F32 = jnp.float32
BF16 = jnp.bfloat16
SDS = jax.ShapeDtypeStruct
_PALLAS_CALL = pl.pallas_call

D_MODEL = 1024
HEAD_DIM = 64
WIN_HALF = 128
DIL_PAIRS = ((128, 1), (512, 4), (2048, 16))
DIL_SIDE = 64
ROT_DIM = 16
ROPE_THETA = 500000.0
MEM_LEN = 256
X_HEADS = 4
X_HEAD_DIM = 256
D_FF = 2816
IN_WIDTH = 5376
Z_QB, Z_KB, Z_VB, Z_QA, Z_KA, Z_VA = 0, 1536, 3072, 4608, 5120, 5248
ALPHA = (2.0) ** 0.25
LN_EPS = 1e-5
NEG_INF = -1e30
ADAM_LR, ADAM_B1, ADAM_B2, ADAM_EPS, ADAM_WD, ADAM_STEP = 0.001, 0.9, 0.999, 1e-08, 0.01, 10
N_DEV = 8
MESH_AXES = ("x", "y", "c")
VMEM_LIMIT_BYTES = 52 * 1024 * 1024
ATTN_TQ = 256
TABW = 384

PACK_ROWS = (("w_in", 672), ("w_gate", 352), ("w_up", 352), ("w_mix_out", 128), ("w_xq", 128), ("w_xk", 128),
             ("w_xv", 128), ("w_xo", 128), ("w_down", 352))
SMALL_ROWS = 24


def _pick(n, cands):
    for c in cands:
        if n % c == 0:
            return c
    return n


def _pcall(body, *, name, out_shape, grid=None, in_specs=None, out_specs=None, scratch_shapes=(), dims=None,
           aliases=None):
    kw = {}
    if grid is not None:
        kw["grid"] = grid
    if in_specs is not None:
        kw["in_specs"] = in_specs
    if out_specs is not None:
        kw["out_specs"] = out_specs
    if aliases:
        kw["input_output_aliases"] = aliases
    return _PALLAS_CALL(
        body, name=name, out_shape=out_shape, scratch_shapes=list(scratch_shapes),
        compiler_params=pltpu.CompilerParams(dimension_semantics=dims, vmem_limit_bytes=VMEM_LIMIT_BYTES), **kw)


MM_VMEM_BUDGET = 40 * 1024 * 1024


def _mm(a, b, *, trans_b, out_dtype, name, addends=(), coefs=(), after=None, more=()):
    pairs = ((a, b, trans_b),) + tuple(more)
    m = a.shape[0]
    n = b.shape[0] if trans_b else b.shape[1]
    n_add = len(addends)
    extra = [] if after is None else [after]
    out_bytes = jnp.dtype(out_dtype).itemsize

    def vmem(tm, tn):
        tot = tm * tn * (out_bytes + 4 * n_add)
        for pa, pb, _ in pairs:
            tot += tm * pa.shape[1] * pa.dtype.itemsize + pa.shape[1] * tn * pb.dtype.itemsize
        return 2 * tot

    tm, tn = next(((cm, cn) for cn in (n, 1408, 1024, 512, 256, 128) if n % cn == 0
                   for cm in (1024, 512, 256, 128) if m % cm == 0 and vmem(cm, cn) <= MM_VMEM_BUDGET))
    n_pairs = len(pairs)

    def body(*refs):
        o_ref = refs[2 * n_pairs + n_add + len(extra)]
        acc = None
        for p, (_, _, tb) in enumerate(pairs):
            dn = _NT if tb else _NN
            part = lax.dot_general(refs[2 * p][...].astype(BF16), refs[2 * p + 1][...].astype(BF16), dn,
                                   preferred_element_type=F32)
            acc = part if acc is None else acc + part
        for r_ref, c in zip(refs[2 * n_pairs:2 * n_pairs + n_add], coefs):
            acc = acc + (r_ref[...] if c == 1.0 else c * r_ref[...])
        o_ref[...] = acc.astype(out_dtype)

    in_specs, args = [], []
    for pa, pb, tb in pairs:
        k = pa.shape[1]
        in_specs.append(pl.BlockSpec((tm, k), lambda j, i: (i, 0)))
        in_specs.append(pl.BlockSpec((tn, k), lambda j, i: (j, 0)) if tb else pl.BlockSpec((k, tn), lambda j, i: (0, j)))
        args += [pa, pb]
    in_specs += [pl.BlockSpec((tm, tn), lambda j, i: (i, j)) for _ in addends]
    in_specs += [pl.BlockSpec((8, 128), lambda j, i: (0, 0)) for _ in extra]
    return _pcall(body, name=name, out_shape=SDS((m, n), out_dtype), grid=(n // tn, m // tm), in_specs=in_specs,
                  out_specs=pl.BlockSpec((tm, tn), lambda j, i: (i, j)),
                  dims=("parallel", "parallel"))(*args, *addends, *extra)


def _mm_tn(a, b, *, name):
    s, m = a.shape
    n = b.shape[1]
    tm = _pick(m, (768, 1408, 1024, 512, 256, 128))
    tk = _pick(s, (1024, 512, 256))
    nk = s // tk

    def body(a_ref, b_ref, o_ref, acc_ref):
        kk = pl.program_id(1)

        @pl.when(kk == 0)
        def _():
            acc_ref[...] = jnp.zeros_like(acc_ref)

        acc_ref[...] += lax.dot_general(a_ref[...].astype(BF16), b_ref[...].astype(BF16), (((0,), (0,)), ((), ())),
                                        preferred_element_type=F32)

        @pl.when(kk == nk - 1)
        def _():
            o_ref[...] = acc_ref[...]

    return _pcall(body, name=name, out_shape=SDS((m, n), F32), grid=(m // tm, nk),
                  in_specs=[pl.BlockSpec((tk, tm), lambda i, kk: (kk, i)), pl.BlockSpec((tk, n), lambda i, kk: (kk, 0))],
                  out_specs=pl.BlockSpec((tm, n), lambda i, kk: (i, 0)), scratch_shapes=[pltpu.VMEM((tm, n), F32)],
                  dims=("parallel", "arbitrary"))(a, b)


def _rope_lane_consts():
    lane = jnp.arange(128)
    j = lane % HEAD_DIM
    inv_freq = ROPE_THETA ** (-jnp.arange(0, ROT_DIM, 2, dtype=F32) / ROT_DIM)
    freq = jnp.where(j < ROT_DIM, inv_freq[j % (ROT_DIM // 2)], 0.0).astype(F32)
    lo = (j < ROT_DIM // 2).astype(F32)
    hi = ((j >= ROT_DIM // 2) & (j < ROT_DIM)).astype(F32)
    return jnp.stack([freq, lo, hi] + [jnp.zeros((128,), F32)] * 5)


def _to_classes(x, scr, d):
    if d == 1:
        return [x]
    scr[...] = x
    return [scr[pl.ds(c, x.shape[0] // d, stride=d), :] for c in range(d)]


def _from_classes(parts, scr):
    d = len(parts)
    if d == 1:
        return parts[0]
    for c, part in enumerate(parts):
        scr[pl.ds(c, part.shape[0], stride=d), :] = part
    return scr[...]


DILATIONS = tuple(d for _, d in DIL_PAIRS)


def _rope_tables(posf):
    s = posf.shape[0]
    tm = _pick(s, (1024, 512))

    def body(p_ref, c_ref, *rest):
        o_refs, scr = rest[:-1], rest[-1]
        ang = p_ref[...] * c_ref[0:1, :]
        lo = c_ref[1:2, :]
        hi = c_ref[2:3, :]
        cs = jnp.cos(ang)
        sn = jnp.sin(ang)
        for q, t in enumerate((jnp.where(lo + hi > 0.0, cs, 1.0), -sn * lo, sn * hi)):
            for o_ref, d in zip(o_refs, DILATIONS):
                for c, part in enumerate(_to_classes(t, scr, d)):
                    o_ref[:, c * TABW + q * 128:c * TABW + (q + 1) * 128] = part

    return _pcall(body, name="rope_tables", out_shape=tuple(SDS((s // d, d * TABW), F32) for d in DILATIONS),
                  grid=(s // tm,),
                  in_specs=[pl.BlockSpec((tm, 1), lambda i: (i, 0)), pl.BlockSpec((8, 128), lambda i: (0, 0))],
                  out_specs=tuple(pl.BlockSpec((tm // d, d * TABW), lambda i: (i, 0)) for d in DILATIONS),
                  scratch_shapes=[pltpu.VMEM((tm, 128), F32)], dims=("parallel",))(posf, _rope_lane_consts())


def _rope_apply(x, tab, sign):
    w = x.shape[1]
    rep = w // 128
    c = jnp.tile(tab[:, 0:128], (1, rep)) if rep > 1 else tab[:, 0:128]
    a = jnp.tile(tab[:, 128:256], (1, rep)) if rep > 1 else tab[:, 128:256]
    b = jnp.tile(tab[:, 256:384], (1, rep)) if rep > 1 else tab[:, 256:384]
    up = pltpu.roll(x, w - 8, 1)
    dn = pltpu.roll(x, 8, 1)
    if sign > 0:
        return x * c + up * a + dn * b
    return x * c - up * a - dn * b


def _proj_rope(h0b, w_t, tab):
    s = h0b.shape[0]
    tm = _pick(s, (512,))
    tn = 256

    def body(a_ref, w_ref, t_ref, zw_ref, z0_ref, z1_ref, z2_ref, scr):
        z_refs = (z0_ref, z1_ref, z2_ref)
        a = a_ref[...]
        tabv = t_ref[...]
        for c0 in range(0, IN_WIDTH, tn):
            z = lax.dot_general(a, w_ref[c0:c0 + tn, :], _NT, preferred_element_type=F32)
            for g0 in range(c0, c0 + tn, 128):
                zg = z[:, g0 - c0:g0 - c0 + 128]
                if g0 < Z_VB or Z_QA <= g0 < Z_VA:
                    zg = _rope_apply(zg, tabv, 1)
                if g0 >= Z_QA:
                    zw_ref[:, g0 - Z_QA:g0 - Z_QA + 128] = zg.astype(BF16)
                    continue
                kind, within = divmod(g0, 1536)
                grp, off = divmod(within, 512)
                col = kind * 512 + off
                for c, part in enumerate(_to_classes(zg, scr, DILATIONS[grp])):
                    z_refs[grp][:, c * 1536 + col:c * 1536 + col + 128] = part.astype(BF16)

    return _pcall(body, name="proj_rope",
                  out_shape=(SDS((s, 768), BF16),) + tuple(SDS((s // d, d * 1536), BF16) for d in DILATIONS),
                  grid=(s // tm,),
                  in_specs=[pl.BlockSpec((tm, D_MODEL), lambda i: (i, 0)), pl.BlockSpec((IN_WIDTH, D_MODEL), lambda i: (0, 0)),
                            pl.BlockSpec((tm, TABW), lambda i: (i, 0))],
                  out_specs=(pl.BlockSpec((tm, 768), lambda i: (i, 0)),)
                  + tuple(pl.BlockSpec((tm // d, d * 1536), lambda i: (i, 0)) for d in DILATIONS),
                  scratch_shapes=[pltpu.VMEM((tm, 128), F32)], dims=("parallel",))(h0b, w_t, tab)


def _band_specs(sd, blk, tq, width, per_tok, cb):
    r = tq // blk
    nbk = sd // blk
    prev = pl.BlockSpec((blk, width), lambda c, j: (jnp.maximum(j * r - 1, 0), c * per_tok + cb))
    cur = pl.BlockSpec((tq, width), lambda c, j: (j, c * per_tok + cb))
    nxt = pl.BlockSpec((blk, width), lambda c, j: (jnp.minimum((j + 1) * r, nbk - 1), c * per_tok + cb))
    return [prev, cur, nxt]


def _band_bias(j, blk, tq, sd):
    shape = (tq, tq + 2 * blk)
    qpos = j * tq + lax.broadcasted_iota(jnp.int32, shape, 0)
    kpos = j * tq - blk + lax.broadcasted_iota(jnp.int32, shape, 1)
    ok = (jnp.abs(qpos - kpos) <= blk) & (kpos >= 0) & (kpos < sd)
    return jnp.where(ok, 0.0, NEG_INF)


_NT = (((1,), (1,)), ((), ()))
_NN = (((1,), (0,)), ((), ()))
_TN = (((0,), (0,)), ((), ()))


def _banded_fwd(zv, sink, *, d, blk, tq, ptw, qw, kw, qcb, kcb, vcb, heads, name):
    sd = zv.shape[0]
    tq = min(tq, sd)
    has_sink = sink is not None
    scale = HEAD_DIM ** -0.5

    def body(q_ref, kp, kc, kn, vp, vc, vn, *rest):
        if has_sink:
            sink_ref, o_ref, lse_ref = rest
        else:
            o_ref, lse_ref = rest
        j = pl.program_id(1)
        q = q_ref[...] * scale
        k = jnp.concatenate([kp[...], kc[...], kn[...]], axis=0)
        v = jnp.concatenate([vp[...], vc[...], vn[...]], axis=0)
        bias = _band_bias(j, blk, tq, sd)
        outs, lses = [], []
        for ql, kl, vl, si in heads:
            sc = lax.dot_general(q[:, ql:ql + HEAD_DIM], k[:, kl:kl + HEAD_DIM], _NT, preferred_element_type=F32) + bias
            m = jnp.max(sc, axis=-1, keepdims=True)
            if has_sink:
                m = jnp.maximum(m, sink_ref[0, si])
            p = jnp.exp(sc - m)
            den = jnp.sum(p, axis=-1, keepdims=True)
            if has_sink:
                den = den + jnp.exp(sink_ref[0, si] - m)
            o = lax.dot_general(p.astype(BF16), v[:, vl:vl + HEAD_DIM], _NN, preferred_element_type=F32) / den
            outs.append(o)
            lses.append(jnp.broadcast_to(m + jnp.log(den), (tq, HEAD_DIM)))
        o_ref[...] = jnp.concatenate(outs, axis=1)
        lse_ref[...] = jnp.concatenate(lses, axis=1)

    in_specs = ([pl.BlockSpec((tq, qw), lambda c, j: (j, c * (ptw // qw) + qcb))]
                + _band_specs(sd, blk, tq, kw, ptw // kw, kcb) + _band_specs(sd, blk, tq, kw, ptw // kw, vcb))
    args = [zv] * 7
    if has_sink:
        in_specs.append(pl.BlockSpec(memory_space=pltpu.SMEM))
        args.append(sink)
    o_spec = pl.BlockSpec((tq, qw), lambda c, j: (j, c))
    return _pcall(body, name=name, out_shape=(SDS((sd, d * qw), F32), SDS((sd, d * qw), F32)), grid=(d, sd // tq),
                  in_specs=in_specs, out_specs=(o_spec, o_spec), dims=("parallel", "parallel"))(*args)


def _banded_bwd(zv, ov, lv, dov, tv, sink, *, d, blk, tq, ptw, qw, kw, qcb, kcb, vcb, heads, kv_heads, name):
    sd = zv.shape[0]
    tq = min(tq, sd)
    nt = sd // tq
    r = tq // blk
    nbk = sd // blk
    has_sink = sink is not None
    scale = HEAD_DIM ** -0.5
    kvw = HEAD_DIM * len(kv_heads)

    def add_rows(x, y, last):
        if tq == blk:
            return x + y
        if last:
            return jnp.concatenate([x[:tq - blk], x[tq - blk:] + y], axis=0)
        return jnp.concatenate([x[:blk] + y, x[blk:]], axis=0)

    def body(q_ref, kp, kc, kn, vp, vc, vn, o_ref, l_ref, do_ref, t_ref, tlag_ref, *rest):
        if has_sink:
            sink_ref, dq_ref, dkv_ref, dsink_ref, acck, accv, nxtk, nxtv = rest
        else:
            dq_ref, dkv_ref, acck, accv, nxtk, nxtv = rest
        j = pl.program_id(1)

        @pl.when(j == 0)
        def _():
            nxtk[...] = jnp.zeros_like(nxtk)
            nxtv[...] = jnp.zeros_like(nxtv)

        if has_sink:
            @pl.when((pl.program_id(0) == 0) & (j == 0))
            def _():
                dsink_ref[...] = jnp.zeros_like(dsink_ref)

        def emit(dk_rows, dv_rows):
            dkv_ref[...] = jnp.concatenate([_rope_apply(dk_rows, tlag_ref[...], -1), dv_rows], axis=1).astype(BF16)

        @pl.when(j < nt)
        def _():
            q = q_ref[...] * scale
            k3 = jnp.concatenate([kp[...], kc[...], kn[...]], axis=0)
            v3 = jnp.concatenate([vp[...], vc[...], vn[...]], axis=0)
            o_t, l_t, do_t = o_ref[...], l_ref[...], do_ref[...]
            bias = _band_bias(j, blk, tq, sd)
            dqs = []
            dks = [None] * len(kv_heads)
            dvs = [None] * len(kv_heads)
            dsink_row = jnp.zeros((1, 128), F32)
            lane = lax.broadcasted_iota(jnp.int32, (1, 128), 1)
            for ql, kl, vl, si in heads:
                kvi = kv_heads.index((kl, vl))
                qh = q[:, ql:ql + HEAD_DIM]
                kh3 = k3[:, kl:kl + HEAD_DIM]
                vh3 = v3[:, vl:vl + HEAD_DIM]
                doh = do_t[:, ql:ql + HEAD_DIM]
                delta = jnp.sum(doh * o_t[:, ql:ql + HEAD_DIM], axis=-1, keepdims=True)
                lse = l_t[:, ql:ql + 1]
                dob = doh.astype(BF16)
                sc = lax.dot_general(qh, kh3, _NT, preferred_element_type=F32) + bias
                p = jnp.exp(sc - lse)
                dp = lax.dot_general(dob, vh3, _NT, preferred_element_type=F32)
                dsb = (p * (dp - delta)).astype(BF16)
                dqs.append(lax.dot_general(dsb, kh3, _NN, preferred_element_type=F32) * scale)
                dk = lax.dot_general(dsb, qh, _TN, preferred_element_type=F32)
                dv = lax.dot_general(p.astype(BF16), dob, _TN, preferred_element_type=F32)
                dks[kvi] = dk if dks[kvi] is None else dks[kvi] + dk
                dvs[kvi] = dv if dvs[kvi] is None else dvs[kvi] + dv
                if has_sink:
                    psink = jnp.exp(sink_ref[0, si] - lse)
                    dsink_row = dsink_row + jnp.where(lane == si, -jnp.sum(psink * delta), 0.0)
            dq_ref[...] = _rope_apply(jnp.concatenate(dqs, axis=1), t_ref[...], -1).astype(BF16)
            wk = jnp.concatenate(dks, axis=1) if len(dks) > 1 else dks[0]
            wv = jnp.concatenate(dvs, axis=1) if len(dvs) > 1 else dvs[0]
            if has_sink:
                dsink_ref[0:1, :] += dsink_row

            @pl.when(j > 0)
            def _():
                emit(add_rows(acck[...], wk[:blk], True), add_rows(accv[...], wv[:blk], True))

            acck[...] = add_rows(wk[blk:blk + tq], nxtk[...], False)
            accv[...] = add_rows(wv[blk:blk + tq], nxtv[...], False)
            nxtk[...] = wk[blk + tq:]
            nxtv[...] = wv[blk + tq:]

        @pl.when(j == nt)
        def _():
            emit(acck[...], accv[...])

    def tile(width, per_tok, cb):
        return pl.BlockSpec((tq, width), lambda c, j: (jnp.minimum(j, nt - 1), c * per_tok + cb))

    def halos(width, per_tok, cb):
        before = pl.BlockSpec((blk, width), lambda c, j: (jnp.maximum(jnp.minimum(j, nt - 1) * r - 1, 0), c * per_tok + cb))
        after = pl.BlockSpec((blk, width),
                             lambda c, j: (jnp.minimum((jnp.minimum(j, nt - 1) + 1) * r, nbk - 1), c * per_tok + cb))
        return [before, tile(width, per_tok, cb), after]

    def lagged(width):
        return pl.BlockSpec((tq, width), lambda c, j: (jnp.maximum(j - 1, 0), c))

    in_specs = ([tile(qw, ptw // qw, qcb)] + halos(kw, ptw // kw, kcb) + halos(kw, ptw // kw, vcb)
                + [tile(qw, 1, 0)] * 3 + [tile(TABW, 1, 0), lagged(TABW)])
    args = [zv] * 7 + [ov, lv, dov, tv, tv]
    out_shape = [SDS((sd, d * qw), BF16), SDS((sd, d * 2 * kvw), BF16)]
    out_specs = [tile(qw, 1, 0), lagged(2 * kvw)]
    if has_sink:
        in_specs.append(pl.BlockSpec(memory_space=pltpu.SMEM))
        args.append(sink)
        out_shape.append(SDS((8, 128), F32))
        out_specs.append(pl.BlockSpec((8, 128), lambda c, j: (0, 0)))
    scratch = [pltpu.VMEM((tq, kvw), F32), pltpu.VMEM((tq, kvw), F32), pltpu.VMEM((blk, kvw), F32),
               pltpu.VMEM((blk, kvw), F32)]
    return _pcall(body, name=name, out_shape=tuple(out_shape), grid=(d, nt + 1), in_specs=in_specs,
                  out_specs=tuple(out_specs), scratch_shapes=scratch, dims=("arbitrary", "arbitrary"))(*args)


_WIN_HEADS = tuple((h * HEAD_DIM, (h // 4) * HEAD_DIM, 128 + (h // 4) * HEAD_DIM, h) for h in range(8))
_WIN_KV = ((0, 128), (64, 192))
_WIN_CFG = dict(d=1, blk=WIN_HALF, tq=ATTN_TQ, ptw=768, qw=512, kw=256, qcb=0, kcb=2, vcb=2, heads=_WIN_HEADS)
_DIL_HEADS = tuple((h * HEAD_DIM, h * HEAD_DIM, h * HEAD_DIM, h) for h in range(8))
_DIL_KV = tuple((h * HEAD_DIM, h * HEAD_DIM) for h in range(8))


def _dil_cfg(gi):
    return dict(d=DILATIONS[gi], blk=DIL_SIDE, tq=ATTN_TQ, ptw=1536, qw=512, kw=512, qcb=0, kcb=1, vcb=2,
                heads=_DIL_HEADS)


def _view_specs(tm, width):
    return tuple(pl.BlockSpec((tm // d, d * width), lambda i: (i, 0)) for d in DILATIONS)


def _mix_norm_fwd(oa, og_views, lg_views, g_win, g_dil):
    s = oa.shape[0]
    tm = _pick(s, (512,))

    def body(oa_ref, o0, o1, o2, l0, l1, l2, gw_ref, gd_ref, mixed_ref, ob0, ob1, ob2, lb0, lb1, lb2, scr, ob_s):
        o_refs, l_refs, ob_refs, lb_refs = (o0, o1, o2), (l0, l1, l2), (ob0, ob1, ob2), (lb0, lb1, lb2)
        ssq = jnp.zeros((tm, 1), F32)
        for q in range(4):
            os_, ls_ = [], []
            for g, d in enumerate(DILATIONS):
                cols = [slice(c * 512 + q * 128, c * 512 + (q + 1) * 128) for c in range(d)]
                os_.append(_from_classes([o_refs[g][:, cs] for cs in cols], scr))
                ls_.append(_from_classes([l_refs[g][:, cs] for cs in cols], scr))
            mx = jnp.maximum(jnp.maximum(ls_[0], ls_[1]), ls_[2])
            es = [jnp.exp(l - mx) for l in ls_]
            den = es[0] + es[1] + es[2]
            ob = (es[0] / den) * os_[0] + (es[1] / den) * os_[1] + (es[2] / den) * os_[2]
            lb = mx + jnp.log(den)
            ob_s[:, q * 128:(q + 1) * 128] = ob
            ssq = ssq + jnp.sum(ob * ob, axis=-1, keepdims=True)
            for g, d in enumerate(DILATIONS):
                for val, refs in ((ob, ob_refs), (lb, lb_refs)):
                    for c, part in enumerate(_to_classes(val, scr, d)):
                        refs[g][:, c * 512 + q * 128:c * 512 + (q + 1) * 128] = part
        a = oa_ref[...]
        ra = lax.rsqrt(jnp.mean(a * a, axis=-1, keepdims=True) + LN_EPS)
        rb = lax.rsqrt(ssq * (1.0 / 512) + LN_EPS)
        mixed_ref[...] = jnp.concatenate([a * ra * gw_ref[...], ob_s[...] * rb * gd_ref[...]], axis=1).astype(BF16)

    row = pl.BlockSpec((tm, 512), lambda i: (i, 0))
    vec = pl.BlockSpec((1, 512), lambda i: (0, 0))
    views = _view_specs(tm, 512)
    view_shapes = tuple(SDS((s // d, d * 512), F32) for d in DILATIONS)
    res = _pcall(body, name="mix_norm_fwd", out_shape=(SDS((s, 1024), BF16),) + view_shapes * 2, grid=(s // tm,),
                 in_specs=[row, *views, *views, vec, vec],
                 out_specs=(pl.BlockSpec((tm, 1024), lambda i: (i, 0)),) + views * 2,
                 scratch_shapes=[pltpu.VMEM((tm, 128), F32), pltpu.VMEM((tm, 512), F32)],
                 dims=("parallel",))(oa, *og_views, *lg_views, g_win, g_dil)
    return res[0], res[1:4], res[4:7]


def _mix_norm_bwd(oa, ob, dmixed, g_win, g_dil):
    s = oa.shape[0]
    tm = _pick(s, (512,))
    nt = s // tm

    def body(oa_ref, ob_ref, dm_ref, gw_ref, gd_ref, doa_ref, db0, db1, db2, dgw_ref, dgd_ref, acc_w, acc_d, scr):
        i = pl.program_id(0)

        @pl.when(i == 0)
        def _():
            acc_w[...] = jnp.zeros_like(acc_w)
            acc_d[...] = jnp.zeros_like(acc_d)

        dm = dm_ref[...]
        dxs = []
        for x_ref, g_ref, dy, acc in ((oa_ref, gw_ref, dm[:, :512], acc_w), (ob_ref, gd_ref, dm[:, 512:], acc_d)):
            x = x_ref[...]
            r = lax.rsqrt(jnp.mean(x * x, axis=-1, keepdims=True) + LN_EPS)
            dyg = dy * g_ref[...]
            dxs.append(r * dyg - x * (r * r * r) * jnp.mean(dyg * x, axis=-1, keepdims=True))
            acc[...] += jnp.sum((dy * x * r).reshape(tm // 8, 8, 512), axis=0)
        doa_ref[...] = dxs[0]
        for q in range(4):
            dq = dxs[1][:, q * 128:(q + 1) * 128]
            for db_ref, d in zip((db0, db1, db2), DILATIONS):
                for c, part in enumerate(_to_classes(dq, scr, d)):
                    db_ref[:, c * 512 + q * 128:c * 512 + (q + 1) * 128] = part

        @pl.when(i == nt - 1)
        def _():
            dgw_ref[...] = jnp.sum(acc_w[...], axis=0, keepdims=True)
            dgd_ref[...] = jnp.sum(acc_d[...], axis=0, keepdims=True)

    row = pl.BlockSpec((tm, 512), lambda i: (i, 0))
    vec = pl.BlockSpec((1, 512), lambda i: (0, 0))
    views = _view_specs(tm, 512)
    view_shapes = tuple(SDS((s // d, d * 512), F32) for d in DILATIONS)
    res = _pcall(body, name="mix_norm_bwd",
                 out_shape=(SDS((s, 512), F32),) + view_shapes + (SDS((1, 512), F32), SDS((1, 512), F32)),
                 grid=(nt,), in_specs=[row, row, pl.BlockSpec((tm, 1024), lambda i: (i, 0)), vec, vec],
                 out_specs=(row,) + views + (vec, vec),
                 scratch_shapes=[pltpu.VMEM((8, 512), F32), pltpu.VMEM((8, 512), F32), pltpu.VMEM((tm, 128), F32)],
                 dims=("arbitrary",))(oa, ob, dmixed, g_win, g_dil)
    return res[0], res[1:4], res[4], res[5]


def _dz_assemble(dq_views, dkv_views, dqa, dkva):
    s = dqa.shape[0]
    tm = _pick(s, (512,))

    def body(q0, q1, q2, kv0, kv1, kv2, qa_ref, kva_ref, o_ref, scr):
        for g, d in enumerate(DILATIONS):
            for kind, (ref, width, base) in enumerate((((q0, q1, q2)[g], 512, 0), ((kv0, kv1, kv2)[g], 1024, 0),
                                                       ((kv0, kv1, kv2)[g], 1024, 512))):
                for q in range(4):
                    src = base + q * 128
                    dst = kind * 1536 + g * 512 + q * 128
                    if d == 1:
                        o_ref[:, dst:dst + 128] = ref[:, src:src + 128]
                    else:
                        parts = [ref[:, c * width + src:c * width + src + 128].astype(F32) for c in range(d)]
                        o_ref[:, dst:dst + 128] = _from_classes(parts, scr).astype(BF16)
        o_ref[:, Z_QA:Z_QA + 512] = qa_ref[...]
        o_ref[:, Z_KA:Z_KA + 256] = kva_ref[...]

    return _pcall(body, name="dz_assemble", out_shape=SDS((s, IN_WIDTH), BF16), grid=(s // tm,),
                  in_specs=[*_view_specs(tm, 512), *_view_specs(tm, 1024), pl.BlockSpec((tm, 512), lambda i: (i, 0)),
                            pl.BlockSpec((tm, 256), lambda i: (i, 0))],
                  out_specs=pl.BlockSpec((tm, IN_WIDTH), lambda i: (i, 0)),
                  scratch_shapes=[pltpu.VMEM((tm, 128), F32)], dims=("parallel",))(*dq_views, *dkv_views, dqa, dkva)


def _ln_fwd(a, r, g, b, ca, name):
    s = a.shape[0]
    tm = _pick(s, (512, 256))
    has_r = r is not None

    def body(*refs):
        a_ref = refs[0]
        r_ref = refs[1] if has_r else None
        g_ref, b_ref, o_ref, ob_ref = refs[1 + has_r:]
        u = a_ref[...] if ca == 1.0 else ca * a_ref[...]
        if has_r:
            u = u + r_ref[...]
        mu = jnp.mean(u, axis=-1, keepdims=True)
        xc = u - mu
        var = jnp.mean(xc * xc, axis=-1, keepdims=True)
        y = xc * lax.rsqrt(var + LN_EPS) * g_ref[...] + b_ref[...]
        o_ref[...] = y
        ob_ref[...] = y.astype(BF16)

    row = pl.BlockSpec((tm, D_MODEL), lambda i: (i, 0))
    vec = pl.BlockSpec((1, D_MODEL), lambda i: (0, 0))
    args = [a] + ([r] if has_r else []) + [g, b]
    return _pcall(body, name=name, out_shape=(SDS((s, D_MODEL), F32), SDS((s, D_MODEL), BF16)), grid=(s // tm,),
                  in_specs=[row] * (1 + has_r) + [vec, vec], out_specs=(row, row), dims=("parallel",))(*args)


def _ln_bwd(a, r, dy, g, b, ca, name, loss_mode=False):
    s = a.shape[0]
    tm = _pick(s, (512, 256))
    nt = s // tm
    has_r = r is not None

    def body(*refs):
        a_ref = refs[0]
        r_ref = refs[1] if has_r else None
        dy_ref, g_ref, b_ref = refs[1 + has_r:4 + has_r]
        outs = refs[4 + has_r:]
        if loss_mode:
            du_ref, dub_ref, dg_ref, db_ref, loss_ref, acc_g, acc_b, acc_l = outs
        else:
            du_ref, dub_ref, dg_ref, db_ref, acc_g, acc_b = outs
        i = pl.program_id(0)

        @pl.when(i == 0)
        def _():
            acc_g[...] = jnp.zeros_like(acc_g)
            acc_b[...] = jnp.zeros_like(acc_b)
            if loss_mode:
                acc_l[...] = jnp.zeros_like(acc_l)

        u = a_ref[...] if ca == 1.0 else ca * a_ref[...]
        if has_r:
            u = u + r_ref[...]
        mu = jnp.mean(u, axis=-1, keepdims=True)
        xc = u - mu
        var = jnp.mean(xc * xc, axis=-1, keepdims=True)
        rstd = lax.rsqrt(var + LN_EPS)
        xhat = xc * rstd
        gv = g_ref[...]
        if loss_mode:
            err = (xhat * gv + b_ref[...]) - dy_ref[...]
            acc_l[...] += jnp.sum((err * err).reshape(tm // 8, 8, D_MODEL), axis=0)
            dyv = err * (1.0 / D_MODEL)
        else:
            dyv = dy_ref[...]
        dxh = dyv * gv
        du = rstd * (dxh - jnp.mean(dxh, axis=-1, keepdims=True) - xhat * jnp.mean(dxh * xhat, axis=-1, keepdims=True))
        du_ref[...] = du
        dub_ref[...] = du.astype(BF16)
        acc_g[...] += jnp.sum((dyv * xhat).reshape(tm // 8, 8, D_MODEL), axis=0)
        acc_b[...] += jnp.sum(dyv.reshape(tm // 8, 8, D_MODEL), axis=0)

        @pl.when(i == nt - 1)
        def _():
            dg_ref[...] = jnp.sum(acc_g[...], axis=0, keepdims=True)
            db_ref[...] = jnp.sum(acc_b[...], axis=0, keepdims=True)
            if loss_mode:
                tot = jnp.sum(jnp.sum(acc_l[...], axis=0, keepdims=True), axis=1, keepdims=True)
                loss_ref[...] = tot * (0.5 / D_MODEL)

    row = pl.BlockSpec((tm, D_MODEL), lambda i: (i, 0))
    vec = pl.BlockSpec((1, D_MODEL), lambda i: (0, 0))
    out_shape = [SDS((s, D_MODEL), F32), SDS((s, D_MODEL), BF16), SDS((1, D_MODEL), F32), SDS((1, D_MODEL), F32)]
    out_specs = [row, row, vec, vec]
    scratch = [pltpu.VMEM((8, D_MODEL), F32), pltpu.VMEM((8, D_MODEL), F32)]
    if loss_mode:
        out_shape.append(SDS((1, 1), F32))
        out_specs.append(pl.BlockSpec((1, 1), lambda i: (0, 0)))
        scratch.append(pltpu.VMEM((8, D_MODEL), F32))
    args = [a] + ([r] if has_r else []) + [dy, g, b]
    return _pcall(body, name=name, out_shape=tuple(out_shape), grid=(nt,), in_specs=[row] * (2 + has_r) + [vec, vec],
                  out_specs=tuple(out_specs), scratch_shapes=scratch, dims=("arbitrary",))(*args)


def _mm_ln_fwd(a, w, resid, g, b, ca, name):
    s, k = a.shape
    tm = _pick(s, (512, 256))

    def body(a_ref, w_ref, res_ref, g_ref, b_ref, r_ref, o_ref, ob_ref):
        rv = lax.dot_general(a_ref[...], w_ref[...], _NN, preferred_element_type=F32)
        r_ref[...] = rv
        u = ca * res_ref[...] + rv
        mu = jnp.mean(u, axis=-1, keepdims=True)
        xc = u - mu
        var = jnp.mean(xc * xc, axis=-1, keepdims=True)
        y = xc * lax.rsqrt(var + LN_EPS) * g_ref[...] + b_ref[...]
        o_ref[...] = y
        ob_ref[...] = y.astype(BF16)

    row = pl.BlockSpec((tm, D_MODEL), lambda i: (i, 0))
    vec = pl.BlockSpec((1, D_MODEL), lambda i: (0, 0))
    return _pcall(body, name=name, out_shape=(SDS((s, D_MODEL), F32), SDS((s, D_MODEL), F32), SDS((s, D_MODEL), BF16)),
                  grid=(s // tm,),
                  in_specs=[pl.BlockSpec((tm, k), lambda i: (i, 0)), pl.BlockSpec((k, D_MODEL), lambda i: (0, 0)), row, vec, vec],
                  out_specs=(row, row, row), dims=("parallel",))(a, w, resid, g, b)


def _mm_ln_bwd(pairs, addend, coef, a, r, g, ca, name, after=None):
    s = a.shape[0]
    has_r = r is not None
    extra = [] if after is None else [after]
    n_pairs = len(pairs)

    def vmem(tm):
        tot = tm * D_MODEL * (4 * (2 + has_r) + 6)
        for pa, pb, _ in pairs:
            tot += tm * pa.shape[1] * pa.dtype.itemsize + pb.size * pb.dtype.itemsize
        return 2 * tot

    tm = next(c for c in (512, 256, 128) if s % c == 0 and vmem(c) <= MM_VMEM_BUDGET)
    nt = s // tm

    def body(*refs):
        ins = refs[2 * n_pairs:]
        add_ref, a_ref = ins[0], ins[1]
        r_ref = ins[2] if has_r else None
        g_ref = ins[2 + has_r]
        du_ref, dub_ref, dg_ref, db_ref, acc_g, acc_b = ins[3 + has_r + len(extra):]
        i = pl.program_id(0)

        @pl.when(i == 0)
        def _():
            acc_g[...] = jnp.zeros_like(acc_g)
            acc_b[...] = jnp.zeros_like(acc_b)

        dyv = coef * add_ref[...]
        for p, (_, _, tb) in enumerate(pairs):
            dyv = dyv + lax.dot_general(refs[2 * p][...].astype(BF16), refs[2 * p + 1][...], _NT if tb else _NN,
                                        preferred_element_type=F32)
        u = a_ref[...] if ca == 1.0 else ca * a_ref[...]
        if has_r:
            u = u + r_ref[...]
        mu = jnp.mean(u, axis=-1, keepdims=True)
        xc = u - mu
        var = jnp.mean(xc * xc, axis=-1, keepdims=True)
        rstd = lax.rsqrt(var + LN_EPS)
        xhat = xc * rstd
        dxh = dyv * g_ref[...]
        du = rstd * (dxh - jnp.mean(dxh, axis=-1, keepdims=True) - xhat * jnp.mean(dxh * xhat, axis=-1, keepdims=True))
        du_ref[...] = du
        dub_ref[...] = du.astype(BF16)
        acc_g[...] += jnp.sum((dyv * xhat).reshape(tm // 8, 8, D_MODEL), axis=0)
        acc_b[...] += jnp.sum(dyv.reshape(tm // 8, 8, D_MODEL), axis=0)

        @pl.when(i == nt - 1)
        def _():
            dg_ref[...] = jnp.sum(acc_g[...], axis=0, keepdims=True)
            db_ref[...] = jnp.sum(acc_b[...], axis=0, keepdims=True)

    row = pl.BlockSpec((tm, D_MODEL), lambda i: (i, 0))
    vec = pl.BlockSpec((1, D_MODEL), lambda i: (0, 0))
    in_specs, args = [], []
    for pa, pb, _ in pairs:
        in_specs += [pl.BlockSpec((tm, pa.shape[1]), lambda i: (i, 0)), pl.BlockSpec(pb.shape, lambda i: (0, 0))]
        args += [pa, pb]
    in_specs += [row] * (2 + has_r) + [vec] + [pl.BlockSpec((8, 128), lambda i: (0, 0))] * len(extra)
    args += [addend, a] + ([r] if has_r else []) + [g] + extra
    return _pcall(body, name=name,
                  out_shape=(SDS((s, D_MODEL), F32), SDS((s, D_MODEL), BF16), SDS((1, D_MODEL), F32), SDS((1, D_MODEL), F32)),
                  grid=(nt,), in_specs=in_specs, out_specs=(row, row, vec, vec),
                  scratch_shapes=[pltpu.VMEM((8, D_MODEL), F32), pltpu.VMEM((8, D_MODEL), F32)],
                  dims=("arbitrary",))(*args)


def _xattn_fwd(q, k, v):
    s = q.shape[0]
    tq = _pick(s, (512,))
    scale = X_HEAD_DIM ** -0.5

    def body(q_ref, k_ref, v_ref, o_ref, ob_ref):
        qv, kv, vv = q_ref[...], k_ref[...], v_ref[...]
        outs = []
        for h in range(X_HEADS):
            sl = slice(h * X_HEAD_DIM, (h + 1) * X_HEAD_DIM)
            sc = lax.dot_general(qv[:, sl], kv[:, sl], _NT, preferred_element_type=F32) * scale
            e = jnp.exp(sc - jnp.max(sc, axis=-1, keepdims=True))
            p = e / jnp.sum(e, axis=-1, keepdims=True)
            outs.append(lax.dot_general(p.astype(BF16), vv[:, sl], _NN, preferred_element_type=F32))
        o = jnp.concatenate(outs, axis=1)
        o_ref[...] = o
        ob_ref[...] = o.astype(BF16)

    row = pl.BlockSpec((tq, D_MODEL), lambda i: (i, 0))
    full = pl.BlockSpec((MEM_LEN, D_MODEL), lambda i: (0, 0))
    return _pcall(body, name="xattn_fwd", out_shape=(SDS((s, D_MODEL), F32), SDS((s, D_MODEL), BF16)), grid=(s // tq,),
                  in_specs=[row, full, full], out_specs=(row, row), dims=("parallel",))(q, k, v)


def _xattn_bwd(q, k, v, o, do):
    s = q.shape[0]
    tq = _pick(s, (512,))
    scale = X_HEAD_DIM ** -0.5

    def body(q_ref, k_ref, v_ref, o_ref, do_ref, dq_ref, dk_ref, dv_ref):
        i = pl.program_id(0)

        @pl.when(i == 0)
        def _():
            dk_ref[...] = jnp.zeros_like(dk_ref)
            dv_ref[...] = jnp.zeros_like(dv_ref)

        qv, kv, vv, ov, dov = q_ref[...], k_ref[...], v_ref[...], o_ref[...], do_ref[...]
        dqs, dks, dvs = [], [], []
        for h in range(X_HEADS):
            sl = slice(h * X_HEAD_DIM, (h + 1) * X_HEAD_DIM)
            sc = lax.dot_general(qv[:, sl], kv[:, sl], _NT, preferred_element_type=F32) * scale
            e = jnp.exp(sc - jnp.max(sc, axis=-1, keepdims=True))
            p = e / jnp.sum(e, axis=-1, keepdims=True)
            doh = dov[:, sl]
            dob = doh.astype(BF16)
            delta = jnp.sum(doh * ov[:, sl], axis=-1, keepdims=True)
            dvs.append(lax.dot_general(p.astype(BF16), dob, _TN, preferred_element_type=F32))
            dp = lax.dot_general(dob, vv[:, sl], _NT, preferred_element_type=F32)
            ds = (p * (dp - delta)).astype(BF16)
            dqs.append(lax.dot_general(ds, kv[:, sl], _NN, preferred_element_type=F32) * scale)
            dks.append(lax.dot_general(ds, qv[:, sl], _TN, preferred_element_type=F32) * scale)
        dq_ref[...] = jnp.concatenate(dqs, axis=1).astype(BF16)
        dk_ref[...] += jnp.concatenate(dks, axis=1)
        dv_ref[...] += jnp.concatenate(dvs, axis=1)

    row = pl.BlockSpec((tq, D_MODEL), lambda i: (i, 0))
    full = pl.BlockSpec((MEM_LEN, D_MODEL), lambda i: (0, 0))
    return _pcall(body, name="xattn_bwd",
                  out_shape=(SDS((s, D_MODEL), BF16), SDS((MEM_LEN, D_MODEL), F32), SDS((MEM_LEN, D_MODEL), F32)),
                  grid=(s // tq,), in_specs=[row, full, full, row, row], out_specs=(row, full, full),
                  dims=("arbitrary",))(q, k, v, o, do)


_SQRT_HALF = 0.7071067811865476
_INV_SQRT_2PI = 0.3989422804014327


def _halo_specs(s, tm, width, rows=8):
    nb = s // rows
    r = tm // rows
    prev = pl.BlockSpec((rows, width), lambda i: (jnp.maximum(i * r - 1, 0), 0))
    nxt = pl.BlockSpec((rows, width), lambda i: (jnp.minimum((i + 1) * r, nb - 1), 0))
    return prev, nxt


def _shifted(x, before_row, after_row, i, nt):
    tm = x.shape[0]
    row = lax.broadcasted_iota(jnp.int32, x.shape, 0)
    first = jnp.where(i == 0, 0.0, 1.0) * before_row
    last = jnp.where(i == nt - 1, 0.0, 1.0) * after_row
    xm1 = jnp.where(row == 0, first, pltpu.roll(x, 1, 0))
    xp1 = jnp.where(row == tm - 1, last, pltpu.roll(x, tm - 1, 0))
    return xm1, xp1


BF16_ROWS = 16


def _ffn_fwd(hb, wg_t, wu_t, cw, cb):
    s = hb.shape[0]
    tm = _pick(s, (256,))
    nt = s // tm
    hr = BF16_ROWS

    def body(h_ref, hp_ref, hn_ref, wg_ref, wu_ref, cw_ref, cb_ref, g_ref, up_ref, act_ref):
        i = pl.program_id(0)
        hv = h_ref[...]
        h_ext = jnp.concatenate([hp_ref[...], hv, hn_ref[...]], axis=0)
        tn = 256
        for c0 in range(0, D_FF, tn):
            cs = slice(c0, c0 + tn)
            g_ext = lax.dot_general(h_ext, wg_ref[cs, :], _NT, preferred_element_type=F32)
            gv = g_ext[hr:hr + tm]
            upv = lax.dot_general(hv, wu_ref[cs, :], _NT, preferred_element_type=F32)
            gm1, gp1 = _shifted(gv, g_ext[hr - 1:hr], g_ext[hr + tm:hr + tm + 1], i, nt)
            gc = gm1 * cw_ref[0:1, cs] + gv * cw_ref[1:2, cs] + gp1 * cw_ref[2:3, cs] + cb_ref[:, cs]
            gelu = 0.5 * gc * (1.0 + lax.erf(gc * _SQRT_HALF))
            g_ref[:, cs] = gv
            up_ref[:, cs] = upv
            act_ref[:, cs] = (gelu * upv).astype(BF16)

    hrow = pl.BlockSpec((tm, D_MODEL), lambda i: (i, 0))
    prev, nxt = _halo_specs(s, tm, D_MODEL, hr)
    wfull = pl.BlockSpec((D_FF, D_MODEL), lambda i: (0, 0))
    row = pl.BlockSpec((tm, D_FF), lambda i: (i, 0))
    return _pcall(body, name="ffn_fwd", out_shape=(SDS((s, D_FF), F32), SDS((s, D_FF), F32), SDS((s, D_FF), BF16)),
                  grid=(nt,), in_specs=[hrow, prev, nxt, wfull, wfull, pl.BlockSpec((8, D_FF), lambda i: (0, 0)),
                                        pl.BlockSpec((1, D_FF), lambda i: (0, 0))],
                  out_specs=(row, row, row), dims=("parallel",))(hb, hb, hb, wg_t, wu_t, cw, cb)


def _ffn_bwd(dffb, w_down, g, up, cw, cb):
    s = g.shape[0]
    tm = _pick(s, (256,))
    nt = s // tm
    hr = BF16_ROWS

    def body(df_ref, dfp_ref, dfn_ref, wd_ref, g_ref, gp_ref, gn_ref, up_ref, upp_ref, upn_ref, cw_ref, cb_ref,
             dg_ref, dup_ref, dcw_ref, dcb_ref, a0, a1, a2, a3):
        i = pl.program_id(0)

        @pl.when(i == 0)
        def _():
            for a in (a0, a1, a2, a3):
                a[...] = jnp.zeros_like(a)

        def d_conv_out(gc_, up_, da_):
            cdf_ = 0.5 * (1.0 + lax.erf(gc_ * _SQRT_HALF))
            pdf_ = jnp.exp(-0.5 * gc_ * gc_) * _INV_SQRT_2PI
            return da_ * up_ * (cdf_ + gc_ * pdf_), cdf_

        df_ext = jnp.concatenate([dfp_ref[...], df_ref[...], dfn_ref[...]], axis=0)
        tn = 256
        for c0 in range(0, D_FF, tn):
            cs = slice(c0, c0 + tn)
            da_ext = lax.dot_general(df_ext, wd_ref[cs, :], _NT, preferred_element_type=F32)
            cw0, cw1, cw2, cbv = cw_ref[0:1, cs], cw_ref[1:2, cs], cw_ref[2:3, cs], cb_ref[:, cs]
            gv = g_ref[:, cs]
            g_before, g_after = gp_ref[:, cs], gn_ref[:, cs]
            gm1, gp1 = _shifted(gv, g_before[7:8, :], g_after[0:1, :], i, nt)
            gc = gm1 * cw0 + gv * cw1 + gp1 * cw2 + cbv
            da = da_ext[hr:hr + tm]
            dgc, cdf = d_conv_out(gc, up_ref[:, cs], da)
            dup_ref[:, cs] = (da * (gc * cdf)).astype(BF16)
            gc_b = g_before[6:7, :] * cw0 + g_before[7:8, :] * cw1 + gv[0:1, :] * cw2 + cbv
            gc_a = gv[tm - 1:tm, :] * cw0 + g_after[0:1, :] * cw1 + g_after[1:2, :] * cw2 + cbv
            dgc_b = jnp.where(i == 0, 0.0, 1.0) * d_conv_out(gc_b, upp_ref[7:8, cs], da_ext[hr - 1:hr])[0]
            dgc_a = jnp.where(i == nt - 1, 0.0, 1.0) * d_conv_out(gc_a, upn_ref[0:1, cs], da_ext[hr + tm:hr + tm + 1])[0]
            row = lax.broadcasted_iota(jnp.int32, dgc.shape, 0)
            dgc_m1 = jnp.where(row == 0, dgc_b, pltpu.roll(dgc, 1, 0))
            dgc_p1 = jnp.where(row == tm - 1, dgc_a, pltpu.roll(dgc, tm - 1, 0))
            dg_ref[:, cs] = (dgc_p1 * cw0 + dgc * cw1 + dgc_m1 * cw2).astype(BF16)

            def fold(t):
                return jnp.sum(t.reshape(tm // 8, 8, tn), axis=0)

            a0[:, cs] += fold(dgc * gm1)
            a1[:, cs] += fold(dgc * gv)
            a2[:, cs] += fold(dgc * gp1)
            a3[:, cs] += fold(dgc)

        @pl.when(i == nt - 1)
        def _():
            dcw_ref[...] = jnp.concatenate(
                [jnp.sum(a[...], axis=0, keepdims=True) for a in (a0, a1, a2)] + [jnp.zeros((5, D_FF), F32)], axis=0)
            dcb_ref[...] = jnp.sum(a3[...], axis=0, keepdims=True)

    row = pl.BlockSpec((tm, D_FF), lambda i: (i, 0))
    prev, nxt = _halo_specs(s, tm, D_FF)
    cw_spec = pl.BlockSpec((8, D_FF), lambda i: (0, 0))
    cb_spec = pl.BlockSpec((1, D_FF), lambda i: (0, 0))
    dprev, dnxt = _halo_specs(s, tm, D_MODEL, hr)
    return _pcall(body, name="ffn_bwd",
                  out_shape=(SDS((s, D_FF), BF16), SDS((s, D_FF), BF16), SDS((8, D_FF), F32), SDS((1, D_FF), F32)),
                  grid=(nt,),
                  in_specs=[pl.BlockSpec((tm, D_MODEL), lambda i: (i, 0)), dprev, dnxt,
                            pl.BlockSpec((D_FF, D_MODEL), lambda i: (0, 0))] + [row, prev, nxt] * 2 + [cw_spec, cb_spec],
                  out_specs=(row, row, cw_spec, cb_spec), scratch_shapes=[pltpu.VMEM((8, D_FF), F32)] * 4,
                  dims=("arbitrary",))(dffb, dffb, dffb, w_down, g, g, g, up, up, up, cw, cb)


def _adamw(w, g, m, v, name):
    rows, cols = w.shape
    tr = _pick(rows, (256, 128, 64, 32, 16, 8))
    c1 = 1.0 - ADAM_B1 ** ADAM_STEP
    c2 = 1.0 - ADAM_B2 ** ADAM_STEP

    def body(w_ref, g_ref, m_ref, v_ref, d_ref, nm_ref, nv_ref):
        gv = g_ref[...]
        nm = ADAM_B1 * m_ref[...] + (1.0 - ADAM_B1) * gv
        nv = ADAM_B2 * v_ref[...] + (1.0 - ADAM_B2) * (gv * gv)
        d_ref[...] = -ADAM_LR * ((nm / c1) / (jnp.sqrt(nv / c2) + ADAM_EPS) + ADAM_WD * w_ref[...])
        nm_ref[...] = nm
        nv_ref[...] = nv

    blk = pl.BlockSpec((tr, cols), lambda i: (i, 0))
    return _pcall(body, name=name, out_shape=(SDS(w.shape, F32),) * 3, grid=(rows // tr,), in_specs=[blk] * 4,
                  out_specs=(blk,) * 3, dims=("parallel",))(w, g, m, v)


def _adamw_many(ws, gs, ms, vs, name):
    n = len(ws)
    c1 = 1.0 - ADAM_B1 ** ADAM_STEP
    c2 = 1.0 - ADAM_B2 ** ADAM_STEP

    def body(*refs):
        outs = refs[4 * n:]
        for k in range(n):
            gv = refs[n + k][...]
            nm = ADAM_B1 * refs[2 * n + k][...] + (1.0 - ADAM_B1) * gv
            nv = ADAM_B2 * refs[3 * n + k][...] + (1.0 - ADAM_B2) * (gv * gv)
            outs[k][...] = -ADAM_LR * ((nm / c1) / (jnp.sqrt(nv / c2) + ADAM_EPS) + ADAM_WD * refs[k][...])
            outs[n + k][...] = nm
            outs[2 * n + k][...] = nv

    shapes = tuple(SDS(w.shape, F32) for w in ws)
    res = _pcall(body, name=name, out_shape=shapes * 3)(*ws, *gs, *ms, *vs)
    return res[:n], res[n:2 * n], res[2 * n:]


def _all_gather_rows(x_shard, *, name, in_vmem, sum_rows=False, after=None):
    m_per, n = x_shard.shape
    extra = [] if after is None else [after]

    def body(x_ref, *rest):
        out_ref, rest = rest[len(extra)], rest[len(extra) + 1:]
        if sum_rows:
            sum_ref, send_sems, recv_sems, local_sem = rest
        else:
            send_sems, recv_sems, local_sem = rest
        x, y, c = lax.axis_index("x"), lax.axis_index("y"), lax.axis_index("c")
        me, sibling = (x, y, c), (x, y, 1 - c)
        chips = [(1 - x, y), (x, 1 - y), (1 - x, 1 - y)]

        def rows(px, py, pc):
            return out_ref.at[pl.ds((4 * px + 2 * py + pc) * m_per, m_per), :]

        def copy(k, block, to, src=None):
            return pltpu.make_async_remote_copy(
                src_ref=rows(*block) if src is None else src, dst_ref=rows(*block), send_sem=send_sems.at[k],
                recv_sem=recv_sems.at[k], device_id=to, device_id_type=pl.DeviceIdType.MESH)

        mine = pltpu.make_async_copy(x_ref, rows(*me), local_sem)
        mine.start()
        first = [copy(0, me, sibling, src=x_ref)]
        first += [copy(1 + j, me, (*chip, c), src=x_ref) for j, chip in enumerate(chips)]
        for cp in first:
            cp.start()
        passed = [copy(4 + j, (*chip, c), sibling) for j, chip in enumerate(chips)]
        for j, chip in enumerate(chips):
            copy(1 + j, (*chip, c), me).wait_recv()
            passed[j].start()
        copy(0, sibling, me).wait_recv()
        for j, chip in enumerate(chips):
            copy(4 + j, (*chip, 1 - c), me).wait_recv()
        for cp in first + passed:
            cp.wait_send()
        mine.wait()
        if sum_rows:
            acc = out_ref[0:m_per, :]
            for dev in range(1, N_DEV):
                acc = acc + out_ref[dev * m_per:(dev + 1) * m_per, :]
            sum_ref[...] = acc

    space = pltpu.VMEM if in_vmem else pl.ANY
    out_shape = [SDS((N_DEV * m_per, n), x_shard.dtype)]
    out_specs = [pl.BlockSpec(memory_space=space)]
    if sum_rows:
        out_shape.append(SDS((m_per, n), x_shard.dtype))
        out_specs.append(pl.BlockSpec(memory_space=pltpu.VMEM))
    res = _PALLAS_CALL(
        body, name=name, out_shape=tuple(out_shape),
        in_specs=[pl.BlockSpec(memory_space=space)] + [pl.BlockSpec(memory_space=pl.ANY)] * len(extra),
        out_specs=tuple(out_specs),
        scratch_shapes=[pltpu.SemaphoreType.DMA((7,)), pltpu.SemaphoreType.DMA((7,)), pltpu.SemaphoreType.DMA],
        compiler_params=pltpu.CompilerParams(vmem_limit_bytes=VMEM_LIMIT_BYTES),
    )(x_shard, *extra)
    return res if sum_rows else res[0]


_HBM = pl.BlockSpec(memory_space=pltpu.HBM)
_SEM = pl.BlockSpec(memory_space=pltpu.SEMAPHORE)
_SPLIT_PARAMS = dict(has_side_effects=pltpu.SideEffectType.DATAFLOW_SIDE_EFFECTING)


def _split_copies(src_ref, land_ref, send_sems, recv_sems, gather):
    x, y, c = lax.axis_index("x"), lax.axis_index("y"), lax.axis_index("c")
    copies = []
    for k in range(1, N_DEV):
        px = 1 - x if k & 4 else x
        py = 1 - y if k & 2 else y
        pc = 1 - c if k & 1 else c
        if gather:
            rows = src_ref.shape[0]
            src, dst = src_ref, land_ref.at[pl.ds((4 * x + 2 * y + c) * rows, rows), :]
        else:
            src, dst = src_ref.at[4 * px + 2 * py + pc], land_ref.at[k - 1]
        copies.append(pltpu.make_async_remote_copy(
            src_ref=src, dst_ref=dst, send_sem=send_sems.at[k - 1], recv_sem=recv_sems.at[k - 1],
            device_id=(px, py, pc), device_id_type=pl.DeviceIdType.MESH))
    return copies


def _exchange_start(src, land_shape, *, gather, name):
    def body(src_ref, land_ref, send_sems, recv_sems, src_thru, land_thru, token):
        for cp in _split_copies(src_ref, land_ref, send_sems, recv_sems, gather):
            cp.start()
        token[...] = jnp.zeros_like(token)

    land = pltpu.with_memory_space_constraint(lax.empty(land_shape, src.dtype), pltpu.HBM)
    return _PALLAS_CALL(
        body, name=name,
        out_shape=(pltpu.SemaphoreType.DMA((N_DEV - 1,)), pltpu.SemaphoreType.DMA((N_DEV - 1,)),
                   pltpu.HBM(src.shape, src.dtype), pltpu.HBM(land_shape, src.dtype), SDS((8, 128), F32)),
        in_specs=(_HBM, _HBM), out_specs=(_SEM, _SEM, _HBM, _HBM, pl.BlockSpec(memory_space=pltpu.VMEM)),
        input_output_aliases={0: 2, 1: 3}, compiler_params=pltpu.CompilerParams(**_SPLIT_PARAMS),
    )(pltpu.with_memory_space_constraint(src, pltpu.HBM), land)


def _exchange_wait(started, after, *, gather, name):
    send_sems, recv_sems, src_thru, land_thru, _ = started

    def body(src_ref, land_ref, send_sems, recv_sems, after_ref, src_out, land_out):
        copies = _split_copies(src_ref, land_ref, send_sems, recv_sems, gather)
        for cp in copies:
            cp.wait_send()
        for cp in copies:
            cp.wait_recv()

    return _PALLAS_CALL(
        body, name=name,
        out_shape=(pltpu.HBM(src_thru.shape, src_thru.dtype), pltpu.HBM(land_thru.shape, land_thru.dtype)),
        in_specs=(_HBM, _HBM, _SEM, _SEM, pl.BlockSpec(memory_space=pl.ANY)), out_specs=(_HBM, _HBM),
        input_output_aliases={0: 0, 1: 1}, compiler_params=pltpu.CompilerParams(**_SPLIT_PARAMS),
    )(src_thru, land_thru, send_sems, recv_sems, after)


def _sum_parts(own, land, name):
    r, n = own.shape
    tr = _pick(r, (264, 320, 336, 128, 64, 32, 16, 8))

    def body(own_ref, x_ref, o_ref):
        acc = own_ref[...]
        for k in range(N_DEV - 1):
            acc = acc + x_ref[k].astype(F32)
        o_ref[...] = acc

    return _pcall(body, name=name, out_shape=SDS((r, n), F32), grid=(r // tr,),
                  in_specs=[pl.BlockSpec((tr, n), lambda i: (i, 0)), pl.BlockSpec((N_DEV - 1, tr, n), lambda i: (0, i, 0))],
                  out_specs=pl.BlockSpec((tr, n), lambda i: (i, 0)), dims=("parallel",))(own, land)


def _pad_rows(a, rows):
    return jnp.pad(a, ((0, rows - a.shape[0]), (0, 0)))


def kernel(x, mem, positions, ln_in_g, ln_in_b, w_in, attn_sink, g_win, g_dil, w_mix_out, ln1_g, ln1_b, mem_ln_g, mem_ln_b, w_xq, w_xk, w_xv, w_xo, ln2_g, ln2_b, w_gate, w_up, conv_w, conv_b, w_down, ln3_g, ln3_b, loss_target, m_ln_in_g, m_ln_in_b, m_w_in, m_attn_sink, m_g_win, m_g_dil, m_w_mix_out, m_ln1_g, m_ln1_b, m_mem_ln_g, m_mem_ln_b, m_w_xq, m_w_xk, m_w_xv, m_w_xo, m_ln2_g, m_ln2_b, m_w_gate, m_w_up, m_conv_w, m_conv_b, m_w_down, m_ln3_g, m_ln3_b, v_ln_in_g, v_ln_in_b, v_w_in, v_attn_sink, v_g_win, v_g_dil, v_w_mix_out, v_ln1_g, v_ln1_b, v_mem_ln_g, v_mem_ln_b, v_w_xq, v_w_xk, v_w_xv, v_w_xo, v_ln2_g, v_ln2_b, v_w_gate, v_w_up, v_conv_w, v_conv_b, v_w_down, v_ln3_g, v_ln3_b):
    weights = dict(ln_in_g=ln_in_g, ln_in_b=ln_in_b, w_in=w_in, attn_sink=attn_sink, g_win=g_win, g_dil=g_dil, w_mix_out=w_mix_out, ln1_g=ln1_g, ln1_b=ln1_b, mem_ln_g=mem_ln_g, mem_ln_b=mem_ln_b, w_xq=w_xq, w_xk=w_xk, w_xv=w_xv, w_xo=w_xo, ln2_g=ln2_g, ln2_b=ln2_b, w_gate=w_gate, w_up=w_up, conv_w=conv_w, conv_b=conv_b, w_down=w_down, ln3_g=ln3_g, ln3_b=ln3_b)
    mom_m = dict(ln_in_g=m_ln_in_g, ln_in_b=m_ln_in_b, w_in=m_w_in, attn_sink=m_attn_sink, g_win=m_g_win, g_dil=m_g_dil, w_mix_out=m_w_mix_out, ln1_g=m_ln1_g, ln1_b=m_ln1_b, mem_ln_g=m_mem_ln_g, mem_ln_b=m_mem_ln_b, w_xq=m_w_xq, w_xk=m_w_xk, w_xv=m_w_xv, w_xo=m_w_xo, ln2_g=m_ln2_g, ln2_b=m_ln2_b, w_gate=m_w_gate, w_up=m_w_up, conv_w=m_conv_w, conv_b=m_conv_b, w_down=m_w_down, ln3_g=m_ln3_g, ln3_b=m_ln3_b)
    mom_v = dict(ln_in_g=v_ln_in_g, ln_in_b=v_ln_in_b, w_in=v_w_in, attn_sink=v_attn_sink, g_win=v_g_win, g_dil=v_g_dil, w_mix_out=v_w_mix_out, ln1_g=v_ln1_g, ln1_b=v_ln1_b, mem_ln_g=v_mem_ln_g, mem_ln_b=v_mem_ln_b, w_xq=v_w_xq, w_xk=v_w_xk, w_xv=v_w_xv, w_xo=v_w_xo, ln2_g=v_ln2_g, ln2_b=v_ln2_b, w_gate=v_w_gate, w_up=v_w_up, conv_w=v_conv_w, conv_b=v_conv_b, w_down=v_w_down, ln3_g=v_ln3_g, ln3_b=v_ln3_b)
    order = list(weights)
    s = x.shape[1]
    xs = x[0]
    mems = mem[0]
    target = loss_target[0]
    row = lambda a: a.reshape(1, -1)

    shard_rows = dict(w_in=w_in[0].T, w_gate=w_gate[0].T, w_up=w_up[0].T, w_mix_out=w_mix_out[0], w_xq=w_xq[0],
                      w_xk=w_xk[0], w_xv=w_xv[0], w_xo=w_xo[0], w_down=w_down[0])
    me_lin = 4 * lax.axis_index("x") + 2 * lax.axis_index("y") + lax.axis_index("c")
    w_in_full = _all_gather_rows(shard_rows["w_in"].astype(BF16), name="w_in_all_gather", in_vmem=False)
    w_in_t = jnp.concatenate([w_in_full[768:], w_in_full[:768]], axis=0)
    late_rows = PACK_ROWS[1:]
    late_r = sum(r for _, r in late_rows)
    packed = jnp.concatenate([shard_rows[n].astype(BF16) for n, _ in late_rows], axis=0)
    w_started = _exchange_start(packed, (N_DEV * late_r, D_MODEL), gather=True, name="weight_gather_start")
    cw_pad = jnp.pad(conv_w[0], ((0, 5), (0, 32)))
    cw_all = _all_gather_rows(cw_pad, name="conv_w_all_gather", in_vmem=True).reshape(N_DEV, 8, 384)
    cw_full = jnp.transpose(cw_all[:, :3, :352], (1, 0, 2)).reshape(3, D_FF)
    cw8 = _pad_rows(cw_full, 8)

    tabs = _rope_tables(positions.astype(F32).reshape(s, 1) + w_started[4][0, 0])
    h0, h0b = _ln_fwd(xs, None, row(ln_in_g), row(ln_in_b), 1.0, "ln_in_fwd")
    zw, *zg = _proj_rope(h0b, w_in_t, tabs[0])
    oa, lse_a = _banded_fwd(zw, attn_sink, name="win_attn_fwd", **_WIN_CFG)
    og_views, lg_views = [], []
    for gi in range(3):
        o_g, l_g = _banded_fwd(zg[gi], None, name=f"dil_attn_fwd{gi}", **_dil_cfg(gi))
        og_views.append(o_g)
        lg_views.append(l_g)
    mixed, ob_views, lb_views = _mix_norm_fwd(oa, og_views, lg_views, g_win, g_dil)
    packed_thru, land = _exchange_wait(w_started, mixed, gather=True, name="weight_gather_wait")
    gathered = lax.dynamic_update_slice(land, packed_thru, (me_lin * late_r, 0)).reshape(N_DEV, late_r, D_MODEL)
    full = {}
    off = 0
    for n, r in late_rows:
        full[n] = gathered[:, off:off + r, :].reshape(N_DEV * r, D_MODEL)
        off += r
    mix, h1, h1b = _mm_ln_fwd(mixed, full["w_mix_out"], h0, ln1_g, ln1_b, ALPHA, "mm_mix_out_ln1")
    _, mem_nb = _ln_fwd(mems, None, mem_ln_g, mem_ln_b, 1.0, "mem_ln_fwd")
    kx = _mm(mem_nb, full["w_xk"], trans_b=False, out_dtype=BF16, name="mm_xk")
    vx = _mm(mem_nb, full["w_xv"], trans_b=False, out_dtype=BF16, name="mm_xv")
    qx = _mm(h1b, full["w_xq"], trans_b=False, out_dtype=BF16, name="mm_xq")
    ox, oxb = _xattn_fwd(qx, kx, vx)
    xa, h2, h2b = _mm_ln_fwd(oxb, full["w_xo"], h1, ln2_g, ln2_b, ALPHA, "mm_xo_ln2")
    gate, up, act = _ffn_fwd(h2b, full["w_gate"], full["w_up"], cw8, conv_b)
    ff = _mm(act, full["w_down"], trans_b=False, out_dtype=F32, name="mm_down")

    du3, du3b, d_ln3_g, d_ln3_b, loss_local = _ln_bwd(h2, ff, target, ln3_g, ln3_b, ALPHA, "ln3_bwd_loss",
                                                      loss_mode=True)
    dw_down = _mm_tn(act, du3b, name="mm_dw_down")
    dgate, dup, dcw8, d_conv_b = _ffn_bwd(du3b, full["w_down"], gate, up, cw8, conv_b)
    dw_gate_t = _mm_tn(dgate, h2b, name="mm_dw_gate")
    dw_up_t = _mm_tn(dup, h2b, name="mm_dw_up")
    rows_of = dict(PACK_ROWS)

    own_f32 = {}

    def start_grad_exchange(parts, name, payload=F32):
        gp = jnp.concatenate([g.reshape(N_DEV, rows_of[n], D_MODEL) for n, g in parts], axis=1)
        if payload != F32:
            own_f32[name] = lax.dynamic_index_in_dim(gp, me_lin, axis=0, keepdims=False)
            gp = gp.astype(payload)
        return _exchange_start(gp, (N_DEV - 1,) + gp.shape[1:], gather=False, name=name)

    ffn_parts = (("w_gate", dw_gate_t), ("w_up", dw_up_t), ("w_down", dw_down))
    ffn_started = start_grad_exchange(ffn_parts, "grad_start_ffn")
    du2, du2b, d_ln2_g, d_ln2_b = _mm_ln_bwd(((dgate, full["w_gate"], False), (dup, full["w_up"], False)), du3, ALPHA,
                                             h1, xa, ln2_g + ffn_started[4][0, 0], ALPHA, "mm_dh2_ln2_bwd")
    dox = _mm(du2b, full["w_xo"], trans_b=True, out_dtype=F32, name="mm_d_ox")
    dw_xo = _mm_tn(oxb, du2b, name="mm_dw_xo")
    dqx, dkx, dvx = _xattn_bwd(qx, kx, vx, ox, dox)
    dw_xq = _mm_tn(h1b, dqx, name="mm_dw_xq")
    dw_xk = _mm_tn(mem_nb, dkx, name="mm_dw_xk")
    dw_xv = _mm_tn(mem_nb, dvx, name="mm_dw_xv")
    dmem_n = _mm(dkx, full["w_xk"], trans_b=True, out_dtype=F32, name="mm_dmem", more=((dvx, full["w_xv"], True),))
    _, _, d_mem_ln_g, d_mem_ln_b = _ln_bwd(mems, None, dmem_n, mem_ln_g, mem_ln_b, 1.0, "mem_ln_bwd")
    du1, du1b, d_ln1_g, d_ln1_b = _mm_ln_bwd(((dqx, full["w_xq"], True),), du2, ALPHA, h0, mix, ln1_g, ALPHA,
                                             "mm_dh1_ln1_bwd")
    dmixed = _mm(du1b, full["w_mix_out"], trans_b=True, out_dtype=F32, name="mm_d_mixed")
    dw_mix_out = _mm_tn(mixed, du1b, name="mm_dw_mix_out")
    attn_parts = (("w_mix_out", dw_mix_out), ("w_xq", dw_xq), ("w_xk", dw_xk), ("w_xv", dw_xv), ("w_xo", dw_xo))
    attn_started = start_grad_exchange(attn_parts, "grad_start_attn")
    doa, dob_views, d_g_win, d_g_dil = _mix_norm_bwd(oa, ob_views[0], dmixed, g_win + attn_started[4][0, 0], g_dil)
    dqa, dkva, dsink8 = _banded_bwd(zw, oa, lse_a, doa, tabs[0], attn_sink, kv_heads=_WIN_KV, name="win_attn_bwd",
                                    **_WIN_CFG)
    dq_views, dkv_views = [], []
    for gi in range(3):
        dq_g, dkv_g = _banded_bwd(zg[gi], ob_views[gi], lb_views[gi], dob_views[gi], tabs[gi], None, kv_heads=_DIL_KV,
                                  name=f"dil_attn_bwd{gi}", **_dil_cfg(gi))
        dq_views.append(dq_g)
        dkv_views.append(dkv_g)
    dz = _dz_assemble(dq_views, dkv_views, dqa, dkva)
    dw_in_tz = _mm_tn(dz, h0b, name="mm_dw_in")
    dw_in_t = jnp.concatenate([dw_in_tz[4608:], dw_in_tz[:4608]], axis=0)
    in_parts = (("w_in", dw_in_t),)
    in_started = start_grad_exchange(in_parts, "grad_start_in", payload=BF16)
    dx, _, d_ln_in_g, d_ln_in_b = _mm_ln_bwd(((dz, w_in_t, False),), du1, ALPHA, xs, None, row(ln_in_g), 1.0,
                                             "mm_dh0_ln_in_bwd", after=in_started[4])

    grads, delta, new_m, new_v = {}, {}, {}, {}
    after = dx
    for parts, started, tag in ((ffn_parts, ffn_started, "ffn"), (attn_parts, attn_started, "attn"),
                                (in_parts, in_started, "in")):
        gp_thru, land = _exchange_wait(started, after, gather=False, name=f"grad_wait_{tag}")
        own = own_f32.get(f"grad_start_{tag}")
        if own is None:
            own = lax.dynamic_index_in_dim(gp_thru, me_lin, axis=0, keepdims=False)
        gsum = _sum_parts(own, land, f"grad_sum_{tag}")
        off = 0
        for n, _ in parts:
            blk = gsum[off:off + rows_of[n]]
            off += rows_of[n]
            grads[n] = (blk.T if n in ("w_in", "w_gate", "w_up") else blk)[None]
            shp = weights[n].shape
            d_, m_, v_ = _adamw(weights[n].reshape(shp[1:]), grads[n].reshape(shp[1:]), mom_m[n].reshape(shp[1:]),
                                mom_v[n].reshape(shp[1:]), f"adamw_{n}")
            delta[n], new_m[n], new_v[n] = d_.reshape(shp), m_.reshape(shp), v_.reshape(shp)
            after = d_

    small = jnp.concatenate([
        d_ln_in_g, d_ln_in_b, d_ln1_g, d_ln1_b, d_mem_ln_g, d_mem_ln_b, d_ln2_g, d_ln2_b, d_ln3_g, d_ln3_b,
        jnp.concatenate([d_g_win, d_g_dil], axis=1),
        jnp.pad(d_conv_b, ((0, 0), (0, 3072 - D_FF))).reshape(3, 1024),
        jnp.pad(dsink8[0:1, :], ((0, 0), (0, 1024 - 128))),
        jnp.pad(dcw8[0:3], ((0, 0), (0, 3072 - D_FF))).reshape(9, 1024),
    ], axis=0)
    _, ssum = _all_gather_rows(small, name="small_grad_all_reduce", in_vmem=True, sum_rows=True, after=after)
    names10 = ["ln_in_g", "ln_in_b", "ln1_g", "ln1_b", "mem_ln_g", "mem_ln_b", "ln2_g", "ln2_b", "ln3_g", "ln3_b"]
    small_g = {n: ssum[i:i + 1] for i, n in enumerate(names10)}
    small_g["g_win"] = ssum[10:11, :512]
    small_g["g_dil"] = ssum[10:11, 512:]
    small_g["conv_b"] = ssum[11:14].reshape(1, 3072)[:, :D_FF]
    small_g["attn_sink"] = ssum[14:15, :8]
    small_g["conv_w"] = lax.dynamic_slice_in_dim(ssum[15:24].reshape(3, 3072)[:, :D_FF], me_lin * 352, 352, axis=1)

    small_names = [n for n in order if n not in rows_of]
    two_d = lambda a: a.reshape(-1, a.shape[-1])
    d_s, m_s, v_s = _adamw_many([two_d(weights[n]) for n in small_names], [small_g[n] for n in small_names],
                                [two_d(mom_m[n]) for n in small_names], [two_d(mom_v[n]) for n in small_names],
                                "adamw_small")
    for k, n in enumerate(small_names):
        shp = weights[n].shape
        grads[n], delta[n], new_m[n], new_v[n] = (t.reshape(shp) for t in (small_g[n], d_s[k], m_s[k], v_s[k]))

    loss = lax.psum(loss_local[0, 0], MESH_AXES)
    return (loss, dx[None], *[grads[n] for n in order], *[delta[n] for n in order], *[new_m[n] for n in order],
            *[new_v[n] for n in order])
```

```python
import functools
import math

import jax
import jax.numpy as jnp
from jax import lax
from jax.experimental import pallas as pl
from jax.experimental.pallas import tpu as pltpu

F32 = jnp.float32
BF16 = jnp.bfloat16
SDS = jax.ShapeDtypeStruct
_PALLAS_CALL = pl.pallas_call

D_MODEL = 1024
HEAD_DIM = 64
WIN_HALF = 128
DIL_PAIRS = ((128, 1), (512, 4), (2048, 16))
DIL_SIDE = 64
ROT_DIM = 16
ROPE_THETA = 500000.0
MEM_LEN = 256
X_HEADS = 4
X_HEAD_DIM = 256
D_FF = 2816
IN_WIDTH = 5376
Z_QB, Z_KB, Z_VB, Z_QA, Z_KA, Z_VA = 0, 1536, 3072, 4608, 5120, 5248
ALPHA = (2.0) ** 0.25
LN_EPS = 1e-5
NEG_INF = -1e30
ADAM_LR, ADAM_B1, ADAM_B2, ADAM_EPS, ADAM_WD, ADAM_STEP = 0.001, 0.9, 0.999, 1e-08, 0.01, 10
N_DEV = 8
MESH_AXES = ("x", "y", "c")
VMEM_LIMIT_BYTES = 52 * 1024 * 1024
ATTN_TQ = 256
TABW = 384

PACK_ROWS = (("w_in", 672), ("w_gate", 352), ("w_up", 352), ("w_mix_out", 128), ("w_xq", 128), ("w_xk", 128),
             ("w_xv", 128), ("w_xo", 128), ("w_down", 352))
SMALL_ROWS = 24


def _pick(n, cands):
    for c in cands:
        if n % c == 0:
            return c
    return n


def _pcall(body, *, name, out_shape, grid=None, in_specs=None, out_specs=None, scratch_shapes=(), dims=None,
           aliases=None):
    kw = {}
    if grid is not None:
        kw["grid"] = grid
    if in_specs is not None:
        kw["in_specs"] = in_specs
    if out_specs is not None:
        kw["out_specs"] = out_specs
    if aliases:
        kw["input_output_aliases"] = aliases
    return _PALLAS_CALL(
        body, name=name, out_shape=out_shape, scratch_shapes=list(scratch_shapes),
        compiler_params=pltpu.CompilerParams(dimension_semantics=dims, vmem_limit_bytes=VMEM_LIMIT_BYTES), **kw)


MM_VMEM_BUDGET = 40 * 1024 * 1024


def _mm(a, b, *, trans_b, out_dtype, name, addends=(), coefs=(), after=None, more=()):
    pairs = ((a, b, trans_b),) + tuple(more)
    m = a.shape[0]
    n = b.shape[0] if trans_b else b.shape[1]
    n_add = len(addends)
    extra = [] if after is None else [after]
    out_bytes = jnp.dtype(out_dtype).itemsize

    def vmem(tm, tn):
        tot = tm * tn * (out_bytes + 4 * n_add)
        for pa, pb, _ in pairs:
            tot += tm * pa.shape[1] * pa.dtype.itemsize + pa.shape[1] * tn * pb.dtype.itemsize
        return 2 * tot

    tm, tn = next(((cm, cn) for cn in (n, 1408, 1024, 512, 256, 128) if n % cn == 0
                   for cm in (1024, 512, 256, 128) if m % cm == 0 and vmem(cm, cn) <= MM_VMEM_BUDGET))
    n_pairs = len(pairs)

    def body(*refs):
        o_ref = refs[2 * n_pairs + n_add + len(extra)]
        acc = None
        for p, (_, _, tb) in enumerate(pairs):
            dn = _NT if tb else _NN
            part = lax.dot_general(refs[2 * p][...].astype(BF16), refs[2 * p + 1][...].astype(BF16), dn,
                                   preferred_element_type=F32)
            acc = part if acc is None else acc + part
        for r_ref, c in zip(refs[2 * n_pairs:2 * n_pairs + n_add], coefs):
            acc = acc + (r_ref[...] if c == 1.0 else c * r_ref[...])
        o_ref[...] = acc.astype(out_dtype)

    in_specs, args = [], []
    for pa, pb, tb in pairs:
        k = pa.shape[1]
        in_specs.append(pl.BlockSpec((tm, k), lambda j, i: (i, 0)))
        in_specs.append(pl.BlockSpec((tn, k), lambda j, i: (j, 0)) if tb else pl.BlockSpec((k, tn), lambda j, i: (0, j)))
        args += [pa, pb]
    in_specs += [pl.BlockSpec((tm, tn), lambda j, i: (i, j)) for _ in addends]
    in_specs += [pl.BlockSpec((8, 128), lambda j, i: (0, 0)) for _ in extra]
    return _pcall(body, name=name, out_shape=SDS((m, n), out_dtype), grid=(n // tn, m // tm), in_specs=in_specs,
                  out_specs=pl.BlockSpec((tm, tn), lambda j, i: (i, j)),
                  dims=("parallel", "parallel"))(*args, *addends, *extra)


def _mm_tn(a, b, *, name):
    s, m = a.shape
    n = b.shape[1]
    tm = _pick(m, (768, 1408, 1024, 512, 256, 128))
    tk = _pick(s, (1024, 512, 256))
    nk = s // tk

    def body(a_ref, b_ref, o_ref, acc_ref):
        kk = pl.program_id(1)

        @pl.when(kk == 0)
        def _():
            acc_ref[...] = jnp.zeros_like(acc_ref)

        acc_ref[...] += lax.dot_general(a_ref[...].astype(BF16), b_ref[...].astype(BF16), (((0,), (0,)), ((), ())),
                                        preferred_element_type=F32)

        @pl.when(kk == nk - 1)
        def _():
            o_ref[...] = acc_ref[...]

    return _pcall(body, name=name, out_shape=SDS((m, n), F32), grid=(m // tm, nk),
                  in_specs=[pl.BlockSpec((tk, tm), lambda i, kk: (kk, i)), pl.BlockSpec((tk, n), lambda i, kk: (kk, 0))],
                  out_specs=pl.BlockSpec((tm, n), lambda i, kk: (i, 0)), scratch_shapes=[pltpu.VMEM((tm, n), F32)],
                  dims=("parallel", "arbitrary"))(a, b)


def _rope_lane_consts():
    lane = jnp.arange(128)
    j = lane % HEAD_DIM
    inv_freq = ROPE_THETA ** (-jnp.arange(0, ROT_DIM, 2, dtype=F32) / ROT_DIM)
    freq = jnp.where(j < ROT_DIM, inv_freq[j % (ROT_DIM // 2)], 0.0).astype(F32)
    lo = (j < ROT_DIM // 2).astype(F32)
    hi = ((j >= ROT_DIM // 2) & (j < ROT_DIM)).astype(F32)
    return jnp.stack([freq, lo, hi] + [jnp.zeros((128,), F32)] * 5)


def _to_classes(x, scr, d):
    if d == 1:
        return [x]
    scr[...] = x
    return [scr[pl.ds(c, x.shape[0] // d, stride=d), :] for c in range(d)]


def _from_classes(parts, scr):
    d = len(parts)
    if d == 1:
        return parts[0]
    for c, part in enumerate(parts):
        scr[pl.ds(c, part.shape[0], stride=d), :] = part
    return scr[...]


DILATIONS = tuple(d for _, d in DIL_PAIRS)


def _rope_tables(posf):
    s = posf.shape[0]
    tm = _pick(s, (1024, 512))

    def body(p_ref, c_ref, *rest):
        o_refs, scr = rest[:-1], rest[-1]
        ang = p_ref[...] * c_ref[0:1, :]
        lo = c_ref[1:2, :]
        hi = c_ref[2:3, :]
        cs = jnp.cos(ang)
        sn = jnp.sin(ang)
        for q, t in enumerate((jnp.where(lo + hi > 0.0, cs, 1.0), -sn * lo, sn * hi)):
            for o_ref, d in zip(o_refs, DILATIONS):
                for c, part in enumerate(_to_classes(t, scr, d)):
                    o_ref[:, c * TABW + q * 128:c * TABW + (q + 1) * 128] = part

    return _pcall(body, name="rope_tables", out_shape=tuple(SDS((s // d, d * TABW), F32) for d in DILATIONS),
                  grid=(s // tm,),
                  in_specs=[pl.BlockSpec((tm, 1), lambda i: (i, 0)), pl.BlockSpec((8, 128), lambda i: (0, 0))],
                  out_specs=tuple(pl.BlockSpec((tm // d, d * TABW), lambda i: (i, 0)) for d in DILATIONS),
                  scratch_shapes=[pltpu.VMEM((tm, 128), F32)], dims=("parallel",))(posf, _rope_lane_consts())


def _rope_apply(x, tab, sign):
    w = x.shape[1]
    rep = w // 128
    c = jnp.tile(tab[:, 0:128], (1, rep)) if rep > 1 else tab[:, 0:128]
    a = jnp.tile(tab[:, 128:256], (1, rep)) if rep > 1 else tab[:, 128:256]
    b = jnp.tile(tab[:, 256:384], (1, rep)) if rep > 1 else tab[:, 256:384]
    up = pltpu.roll(x, w - 8, 1)
    dn = pltpu.roll(x, 8, 1)
    if sign > 0:
        return x * c + up * a + dn * b
    return x * c - up * a - dn * b


def _proj_rope(h0b, w_t, tab):
    s = h0b.shape[0]
    tm = _pick(s, (512,))
    tn = 256

    def body(a_ref, w_ref, t_ref, zw_ref, z0_ref, z1_ref, z2_ref, scr):
        z_refs = (z0_ref, z1_ref, z2_ref)
        a = a_ref[...]
        tabv = t_ref[...]
        for c0 in range(0, IN_WIDTH, tn):
            z = lax.dot_general(a, w_ref[c0:c0 + tn, :], _NT, preferred_element_type=F32)
            for g0 in range(c0, c0 + tn, 128):
                zg = z[:, g0 - c0:g0 - c0 + 128]
                if g0 < Z_VB or Z_QA <= g0 < Z_VA:
                    zg = _rope_apply(zg, tabv, 1)
                if g0 >= Z_QA:
                    zw_ref[:, g0 - Z_QA:g0 - Z_QA + 128] = zg.astype(BF16)
                    continue
                kind, within = divmod(g0, 1536)
                grp, off = divmod(within, 512)
                col = kind * 512 + off
                for c, part in enumerate(_to_classes(zg, scr, DILATIONS[grp])):
                    z_refs[grp][:, c * 1536 + col:c * 1536 + col + 128] = part.astype(BF16)

    return _pcall(body, name="proj_rope",
                  out_shape=(SDS((s, 768), BF16),) + tuple(SDS((s // d, d * 1536), BF16) for d in DILATIONS),
                  grid=(s // tm,),
                  in_specs=[pl.BlockSpec((tm, D_MODEL), lambda i: (i, 0)), pl.BlockSpec((IN_WIDTH, D_MODEL), lambda i: (0, 0)),
                            pl.BlockSpec((tm, TABW), lambda i: (i, 0))],
                  out_specs=(pl.BlockSpec((tm, 768), lambda i: (i, 0)),)
                  + tuple(pl.BlockSpec((tm // d, d * 1536), lambda i: (i, 0)) for d in DILATIONS),
                  scratch_shapes=[pltpu.VMEM((tm, 128), F32)], dims=("parallel",))(h0b, w_t, tab)


def _band_specs(sd, blk, tq, width, per_tok, cb):
    r = tq // blk
    nbk = sd // blk
    prev = pl.BlockSpec((blk, width), lambda c, j: (jnp.maximum(j * r - 1, 0), c * per_tok + cb))
    cur = pl.BlockSpec((tq, width), lambda c, j: (j, c * per_tok + cb))
    nxt = pl.BlockSpec((blk, width), lambda c, j: (jnp.minimum((j + 1) * r, nbk - 1), c * per_tok + cb))
    return [prev, cur, nxt]


def _band_bias(q0, rows, blk, sd):
    shape = (rows, rows + 2 * blk)
    qpos = q0 + lax.broadcasted_iota(jnp.int32, shape, 0)
    kpos = q0 - blk + lax.broadcasted_iota(jnp.int32, shape, 1)
    ok = (jnp.abs(qpos - kpos) <= blk) & (kpos >= 0) & (kpos < sd)
    return jnp.where(ok, 0.0, NEG_INF)


_NT = (((1,), (1,)), ((), ()))
_NN = (((1,), (0,)), ((), ()))
_TN = (((0,), (0,)), ((), ()))


def _banded_fwd(zv, sink, *, d, blk, tq, rc, ptw, qw, kw, qcb, kcb, vcb, pairs, name):
    sd = zv.shape[0]
    tq = min(tq, sd)
    rc = min(rc, tq)
    has_sink = sink is not None
    scale = HEAD_DIM ** -0.5

    def body(q_ref, kp, kc, kn, vp, vc, vn, *rest):
        if has_sink:
            sink_ref, o_ref, lse_ref = rest
        else:
            o_ref, lse_ref = rest
        j = pl.program_id(1)
        q = q_ref[...] * scale
        k = jnp.concatenate([kp[...], kc[...], kn[...]], axis=0)
        v = jnp.concatenate([vp[...], vc[...], vn[...]], axis=0)
        biases = {r0: _band_bias(j * tq + r0, rc, blk, sd) for r0 in range(0, tq, rc)}
        low = lax.broadcasted_iota(jnp.int32, (1, 128), 1) < HEAD_DIM
        for qb, kb, vb, swaps, sinks in pairs:
            qp, kp_, vp_ = q[:, qb:qb + 128], k[:, kb:kb + 128], v[:, vb:vb + 128]
            if any(swaps):
                k_sw = jnp.concatenate([kp_[:, HEAD_DIM:], kp_[:, :HEAD_DIM]], axis=1)
                v_sw = jnp.concatenate([vp_[:, HEAD_DIM:], vp_[:, :HEAD_DIM]], axis=1)
            for r0 in range(0, tq, rc):
                outs, lses = [], []
                for half in range(2):
                    qm = jnp.where(low if half == 0 else ~low, qp[r0:r0 + rc], jnp.zeros((rc, 128), BF16))
                    kk, vv = (k_sw, v_sw) if swaps[half] else (kp_, vp_)
                    kk, vv = kk[r0:r0 + rc + 2 * blk], vv[r0:r0 + rc + 2 * blk]
                    sc = lax.dot_general(qm, kk, _NT, preferred_element_type=F32) + biases[r0]
                    m = jnp.max(sc, axis=-1, keepdims=True)
                    if has_sink:
                        m = jnp.maximum(m, sink_ref[0, sinks[half]])
                    p = jnp.exp(sc - m)
                    den = jnp.sum(p, axis=-1, keepdims=True)
                    if has_sink:
                        den = den + jnp.exp(sink_ref[0, sinks[half]] - m)
                    outs.append(lax.dot_general(p.astype(BF16), vv, _NN, preferred_element_type=F32) / den)
                    lses.append(m + jnp.log(den))
                o_ref[r0:r0 + rc, qb:qb + 128] = jnp.where(low, outs[0], outs[1])
                lse_ref[r0:r0 + rc, qb:qb + 128] = jnp.where(low, lses[0], lses[1])

    in_specs = ([pl.BlockSpec((tq, qw), lambda c, j: (j, c * (ptw // qw) + qcb))]
                + _band_specs(sd, blk, tq, kw, ptw // kw, kcb) + _band_specs(sd, blk, tq, kw, ptw // kw, vcb))
    args = [zv] * 7
    if has_sink:
        in_specs.append(pl.BlockSpec(memory_space=pltpu.SMEM))
        args.append(sink)
    o_spec = pl.BlockSpec((tq, qw), lambda c, j: (j, c))
    return _pcall(body, name=name, out_shape=(SDS((sd, d * qw), F32), SDS((sd, d * qw), F32)), grid=(d, sd // tq),
                  in_specs=in_specs, out_specs=(o_spec, o_spec), dims=("parallel", "parallel"))(*args)


def _banded_bwd(zv, ov, lv, dov, tv, sink, *, d, blk, tq, rc, ptw, qw, kw, qcb, kcb, vcb, pairs, name):
    sd = zv.shape[0]
    tq = min(tq, sd)
    nt = sd // tq
    r = tq // blk
    nbk = sd // blk
    has_sink = sink is not None
    scale = HEAD_DIM ** -0.5
    rc = min(rc, tq)
    kvw = 128 * len({kb for _, kb, _, _, _ in pairs})

    def add_rows(x, y, last):
        if tq == blk:
            return x + y
        if last:
            return jnp.concatenate([x[:tq - blk], x[tq - blk:] + y], axis=0)
        return jnp.concatenate([x[:blk] + y, x[blk:]], axis=0)

    def body(q_ref, kp, kc, kn, vp, vc, vn, o_ref, l_ref, do_ref, t_ref, tlag_ref, *rest):
        if has_sink:
            sink_ref, dq_ref, dkv_ref, dsink_ref, acck, accv, nxtk, nxtv = rest
        else:
            dq_ref, dkv_ref, acck, accv, nxtk, nxtv = rest
        j = pl.program_id(1)

        @pl.when(j == 0)
        def _():
            nxtk[...] = jnp.zeros_like(nxtk)
            nxtv[...] = jnp.zeros_like(nxtv)

        if has_sink:
            @pl.when((pl.program_id(0) == 0) & (j == 0))
            def _():
                dsink_ref[...] = jnp.zeros_like(dsink_ref)

        def emit(dk_rows, dv_rows):
            dkv_ref[...] = jnp.concatenate([_rope_apply(dk_rows, tlag_ref[...], -1), dv_rows], axis=1).astype(BF16)

        @pl.when(j < nt)
        def _():
            q = q_ref[...] * scale
            k3 = jnp.concatenate([kp[...], kc[...], kn[...]], axis=0)
            v3 = jnp.concatenate([vp[...], vc[...], vn[...]], axis=0)
            o_t, l_t, do_t = o_ref[...], l_ref[...], do_ref[...]
            biases = {r0: _band_bias(j * tq + r0, rc, blk, sd) for r0 in range(0, tq, rc)}
            lane = lax.broadcasted_iota(jnp.int32, (1, 128), 1)
            low = lane < HEAD_DIM
            wide = tq + 2 * blk
            cw = rc + 2 * blk

            def place(x, r0):
                parts = ([jnp.zeros((r0, 128), F32)] if r0 else []) + [x]
                if wide - r0 - cw:
                    parts.append(jnp.zeros((wide - r0 - cw, 128), F32))
                return jnp.concatenate(parts, axis=0) if len(parts) > 1 else x

            dqs = []
            wks, wvs = {}, {}
            dsink_row = jnp.zeros((1, 128), F32)
            for qb, kb, vb, swaps, sinks in pairs:
                qp, kp_, vp_ = q[:, qb:qb + 128], k3[:, kb:kb + 128], v3[:, vb:vb + 128]
                if any(swaps):
                    k_sw = jnp.concatenate([kp_[:, HEAD_DIM:], kp_[:, :HEAD_DIM]], axis=1)
                    v_sw = jnp.concatenate([vp_[:, HEAD_DIM:], vp_[:, :HEAD_DIM]], axis=1)
                dop, lp = do_t[:, qb:qb + 128], l_t[:, qb:qb + 128]
                prod = dop * o_t[:, qb:qb + 128]
                dq_rows = []
                for r0 in range(0, tq, rc):
                    rows = slice(r0, r0 + rc)
                    dq_half = []
                    for half in range(2):
                        mine = low if half == 0 else ~low
                        qm = jnp.where(mine, qp[rows], jnp.zeros((rc, 128), BF16))
                        dob = jnp.where(mine, dop[rows], 0.0).astype(BF16)
                        delta = jnp.sum(jnp.where(mine, prod[rows], 0.0), axis=-1, keepdims=True)
                        lse = lp[rows, half * HEAD_DIM:half * HEAD_DIM + 1]
                        kk, vv = (k_sw, v_sw) if swaps[half] else (kp_, vp_)
                        kk, vv = kk[r0:r0 + cw], vv[r0:r0 + cw]
                        sc = lax.dot_general(qm, kk, _NT, preferred_element_type=F32) + biases[r0]
                        p = jnp.exp(sc - lse)
                        dp = lax.dot_general(dob, vv, _NT, preferred_element_type=F32)
                        dsb = (p * (dp - delta)).astype(BF16)
                        dq_half.append(lax.dot_general(dsb, kk, _NN, preferred_element_type=F32))
                        dk = lax.dot_general(dsb, qm, _TN, preferred_element_type=F32)
                        dv = lax.dot_general(p.astype(BF16), dob, _TN, preferred_element_type=F32)
                        if swaps[half]:
                            dk, dv = pltpu.roll(dk, HEAD_DIM, 1), pltpu.roll(dv, HEAD_DIM, 1)
                        wks[kb] = place(dk, r0) if kb not in wks else wks[kb] + place(dk, r0)
                        wvs[vb] = place(dv, r0) if vb not in wvs else wvs[vb] + place(dv, r0)
                        if has_sink:
                            psink = jnp.exp(sink_ref[0, sinks[half]] - lse)
                            dsink_row = dsink_row + jnp.where(lane == sinks[half], -jnp.sum(psink * delta), 0.0)
                    dq_rows.append(jnp.where(low, dq_half[0], dq_half[1]) * scale)
                dqs.append(jnp.concatenate(dq_rows, axis=0) if len(dq_rows) > 1 else dq_rows[0])
            dq_ref[...] = _rope_apply(jnp.concatenate(dqs, axis=1), t_ref[...], -1).astype(BF16)
            wk = jnp.concatenate([wks[b] for b in sorted(wks)], axis=1) if len(wks) > 1 else wks[min(wks)]
            wv = jnp.concatenate([wvs[b] for b in sorted(wvs)], axis=1) if len(wvs) > 1 else wvs[min(wvs)]
            if has_sink:
                dsink_ref[0:1, :] += dsink_row

            @pl.when(j > 0)
            def _():
                emit(add_rows(acck[...], wk[:blk], True), add_rows(accv[...], wv[:blk], True))

            acck[...] = add_rows(wk[blk:blk + tq], nxtk[...], False)
            accv[...] = add_rows(wv[blk:blk + tq], nxtv[...], False)
            nxtk[...] = wk[blk + tq:]
            nxtv[...] = wv[blk + tq:]

        @pl.when(j == nt)
        def _():
            emit(acck[...], accv[...])

    def tile(width, per_tok, cb):
        return pl.BlockSpec((tq, width), lambda c, j: (jnp.minimum(j, nt - 1), c * per_tok + cb))

    def halos(width, per_tok, cb):
        before = pl.BlockSpec((blk, width), lambda c, j: (jnp.maximum(jnp.minimum(j, nt - 1) * r - 1, 0), c * per_tok + cb))
        after = pl.BlockSpec((blk, width),
                             lambda c, j: (jnp.minimum((jnp.minimum(j, nt - 1) + 1) * r, nbk - 1), c * per_tok + cb))
        return [before, tile(width, per_tok, cb), after]

    def lagged(width):
        return pl.BlockSpec((tq, width), lambda c, j: (jnp.maximum(j - 1, 0), c))

    in_specs = ([tile(qw, ptw // qw, qcb)] + halos(kw, ptw // kw, kcb) + halos(kw, ptw // kw, vcb)
                + [tile(qw, 1, 0)] * 3 + [tile(TABW, 1, 0), lagged(TABW)])
    args = [zv] * 7 + [ov, lv, dov, tv, tv]
    out_shape = [SDS((sd, d * qw), BF16), SDS((sd, d * 2 * kvw), BF16)]
    out_specs = [tile(qw, 1, 0), lagged(2 * kvw)]
    if has_sink:
        in_specs.append(pl.BlockSpec(memory_space=pltpu.SMEM))
        args.append(sink)
        out_shape.append(SDS((8, 128), F32))
        out_specs.append(pl.BlockSpec((8, 128), lambda c, j: (0, 0)))
    scratch = [pltpu.VMEM((tq, kvw), F32), pltpu.VMEM((tq, kvw), F32), pltpu.VMEM((blk, kvw), F32),
               pltpu.VMEM((blk, kvw), F32)]
    return _pcall(body, name=name, out_shape=tuple(out_shape), grid=(d, nt + 1), in_specs=in_specs,
                  out_specs=tuple(out_specs), scratch_shapes=scratch, dims=("arbitrary", "arbitrary"))(*args)


_WIN_PAIRS = tuple((128 * p, 0, 128, (False, True) if p < 2 else (True, False), (2 * p, 2 * p + 1)) for p in range(4))
_WIN_CFG = dict(d=1, blk=WIN_HALF, tq=ATTN_TQ, rc=256, ptw=768, qw=512, kw=256, qcb=0, kcb=2, vcb=2, pairs=_WIN_PAIRS)
_DIL_PAIRS = tuple((128 * p, 128 * p, 128 * p, (False, False), (2 * p, 2 * p + 1)) for p in range(4))


def _dil_cfg(gi):
    return dict(d=DILATIONS[gi], blk=DIL_SIDE, tq=ATTN_TQ, rc=128, ptw=1536, qw=512, kw=512, qcb=0, kcb=1, vcb=2,
                pairs=_DIL_PAIRS)


def _view_specs(tm, width):
    return tuple(pl.BlockSpec((tm // d, d * width), lambda i: (i, 0)) for d in DILATIONS)


def _mix_norm_fwd(oa, og_views, lg_views, g_win, g_dil):
    s = oa.shape[0]
    tm = _pick(s, (512,))

    def body(oa_ref, o0, o1, o2, l0, l1, l2, gw_ref, gd_ref, mixed_ref, ob0, ob1, ob2, lb0, lb1, lb2, scr, ob_s):
        o_refs, l_refs, ob_refs, lb_refs = (o0, o1, o2), (l0, l1, l2), (ob0, ob1, ob2), (lb0, lb1, lb2)
        ssq = jnp.zeros((tm, 1), F32)
        for q in range(4):
            os_, ls_ = [], []
            for g, d in enumerate(DILATIONS):
                cols = [slice(c * 512 + q * 128, c * 512 + (q + 1) * 128) for c in range(d)]
                os_.append(_from_classes([o_refs[g][:, cs] for cs in cols], scr))
                ls_.append(_from_classes([l_refs[g][:, cs] for cs in cols], scr))
            mx = jnp.maximum(jnp.maximum(ls_[0], ls_[1]), ls_[2])
            es = [jnp.exp(l - mx) for l in ls_]
            den = es[0] + es[1] + es[2]
            ob = (es[0] / den) * os_[0] + (es[1] / den) * os_[1] + (es[2] / den) * os_[2]
            lb = mx + jnp.log(den)
            ob_s[:, q * 128:(q + 1) * 128] = ob
            ssq = ssq + jnp.sum(ob * ob, axis=-1, keepdims=True)
            for g, d in enumerate(DILATIONS):
                for val, refs in ((ob, ob_refs), (lb, lb_refs)):
                    for c, part in enumerate(_to_classes(val, scr, d)):
                        refs[g][:, c * 512 + q * 128:c * 512 + (q + 1) * 128] = part
        a = oa_ref[...]
        ra = lax.rsqrt(jnp.mean(a * a, axis=-1, keepdims=True) + LN_EPS)
        rb = lax.rsqrt(ssq * (1.0 / 512) + LN_EPS)
        mixed_ref[...] = jnp.concatenate([a * ra * gw_ref[...], ob_s[...] * rb * gd_ref[...]], axis=1).astype(BF16)

    row = pl.BlockSpec((tm, 512), lambda i: (i, 0))
    vec = pl.BlockSpec((1, 512), lambda i: (0, 0))
    views = _view_specs(tm, 512)
    view_shapes = tuple(SDS((s // d, d * 512), F32) for d in DILATIONS)
    res = _pcall(body, name="mix_norm_fwd", out_shape=(SDS((s, 1024), BF16),) + view_shapes * 2, grid=(s // tm,),
                 in_specs=[row, *views, *views, vec, vec],
                 out_specs=(pl.BlockSpec((tm, 1024), lambda i: (i, 0)),) + views * 2,
                 scratch_shapes=[pltpu.VMEM((tm, 128), F32), pltpu.VMEM((tm, 512), F32)],
                 dims=("parallel",))(oa, *og_views, *lg_views, g_win, g_dil)
    return res[0], res[1:4], res[4:7]


def _mix_norm_bwd(oa, ob, dmixed, g_win, g_dil):
    s = oa.shape[0]
    tm = _pick(s, (512,))
    nt = s // tm

    def body(oa_ref, ob_ref, dm_ref, gw_ref, gd_ref, doa_ref, db0, db1, db2, dgw_ref, dgd_ref, acc_w, acc_d, scr):
        i = pl.program_id(0)

        @pl.when(i == 0)
        def _():
            acc_w[...] = jnp.zeros_like(acc_w)
            acc_d[...] = jnp.zeros_like(acc_d)

        dm = dm_ref[...]
        dxs = []
        for x_ref, g_ref, dy, acc in ((oa_ref, gw_ref, dm[:, :512], acc_w), (ob_ref, gd_ref, dm[:, 512:], acc_d)):
            x = x_ref[...]
            r = lax.rsqrt(jnp.mean(x * x, axis=-1, keepdims=True) + LN_EPS)
            dyg = dy * g_ref[...]
            dxs.append(r * dyg - x * (r * r * r) * jnp.mean(dyg * x, axis=-1, keepdims=True))
            acc[...] += jnp.sum((dy * x * r).reshape(tm // 8, 8, 512), axis=0)
        doa_ref[...] = dxs[0]
        for q in range(4):
            dq = dxs[1][:, q * 128:(q + 1) * 128]
            for db_ref, d in zip((db0, db1, db2), DILATIONS):
                for c, part in enumerate(_to_classes(dq, scr, d)):
                    db_ref[:, c * 512 + q * 128:c * 512 + (q + 1) * 128] = part

        @pl.when(i == nt - 1)
        def _():
            dgw_ref[...] = jnp.sum(acc_w[...], axis=0, keepdims=True)
            dgd_ref[...] = jnp.sum(acc_d[...], axis=0, keepdims=True)

    row = pl.BlockSpec((tm, 512), lambda i: (i, 0))
    vec = pl.BlockSpec((1, 512), lambda i: (0, 0))
    views = _view_specs(tm, 512)
    view_shapes = tuple(SDS((s // d, d * 512), F32) for d in DILATIONS)
    res = _pcall(body, name="mix_norm_bwd",
                 out_shape=(SDS((s, 512), F32),) + view_shapes + (SDS((1, 512), F32), SDS((1, 512), F32)),
                 grid=(nt,), in_specs=[row, row, pl.BlockSpec((tm, 1024), lambda i: (i, 0)), vec, vec],
                 out_specs=(row,) + views + (vec, vec),
                 scratch_shapes=[pltpu.VMEM((8, 512), F32), pltpu.VMEM((8, 512), F32), pltpu.VMEM((tm, 128), F32)],
                 dims=("arbitrary",))(oa, ob, dmixed, g_win, g_dil)
    return res[0], res[1:4], res[4], res[5]


def _dz_assemble(dq_views, dkv_views, dqa, dkva):
    s = dqa.shape[0]
    tm = _pick(s, (512,))

    def body(q0, q1, q2, kv0, kv1, kv2, qa_ref, kva_ref, o_ref, scr):
        for g, d in enumerate(DILATIONS):
            for kind, (ref, width, base) in enumerate((((q0, q1, q2)[g], 512, 0), ((kv0, kv1, kv2)[g], 1024, 0),
                                                       ((kv0, kv1, kv2)[g], 1024, 512))):
                for q in range(4):
                    src = base + q * 128
                    dst = kind * 1536 + g * 512 + q * 128
                    if d == 1:
                        o_ref[:, dst:dst + 128] = ref[:, src:src + 128]
                    else:
                        parts = [ref[:, c * width + src:c * width + src + 128].astype(F32) for c in range(d)]
                        o_ref[:, dst:dst + 128] = _from_classes(parts, scr).astype(BF16)
        o_ref[:, Z_QA:Z_QA + 512] = qa_ref[...]
        o_ref[:, Z_KA:Z_KA + 256] = kva_ref[...]

    return _pcall(body, name="dz_assemble", out_shape=SDS((s, IN_WIDTH), BF16), grid=(s // tm,),
                  in_specs=[*_view_specs(tm, 512), *_view_specs(tm, 1024), pl.BlockSpec((tm, 512), lambda i: (i, 0)),
                            pl.BlockSpec((tm, 256), lambda i: (i, 0))],
                  out_specs=pl.BlockSpec((tm, IN_WIDTH), lambda i: (i, 0)),
                  scratch_shapes=[pltpu.VMEM((tm, 128), F32)], dims=("parallel",))(*dq_views, *dkv_views, dqa, dkva)


def _ln_fwd(a, r, g, b, ca, name):
    s = a.shape[0]
    tm = _pick(s, (512, 256))
    has_r = r is not None

    def body(*refs):
        a_ref = refs[0]
        r_ref = refs[1] if has_r else None
        g_ref, b_ref, o_ref, ob_ref = refs[1 + has_r:]
        u = a_ref[...] if ca == 1.0 else ca * a_ref[...]
        if has_r:
            u = u + r_ref[...]
        mu = jnp.mean(u, axis=-1, keepdims=True)
        xc = u - mu
        var = jnp.mean(xc * xc, axis=-1, keepdims=True)
        y = xc * lax.rsqrt(var + LN_EPS) * g_ref[...] + b_ref[...]
        o_ref[...] = y
        ob_ref[...] = y.astype(BF16)

    row = pl.BlockSpec((tm, D_MODEL), lambda i: (i, 0))
    vec = pl.BlockSpec((1, D_MODEL), lambda i: (0, 0))
    args = [a] + ([r] if has_r else []) + [g, b]
    return _pcall(body, name=name, out_shape=(SDS((s, D_MODEL), F32), SDS((s, D_MODEL), BF16)), grid=(s // tm,),
                  in_specs=[row] * (1 + has_r) + [vec, vec], out_specs=(row, row), dims=("parallel",))(*args)


def _ln_bwd(a, r, dy, g, b, ca, name, loss_mode=False):
    s = a.shape[0]
    tm = _pick(s, (512, 256))
    nt = s // tm
    has_r = r is not None

    def body(*refs):
        a_ref = refs[0]
        r_ref = refs[1] if has_r else None
        dy_ref, g_ref, b_ref = refs[1 + has_r:4 + has_r]
        outs = refs[4 + has_r:]
        if loss_mode:
            du_ref, dub_ref, dg_ref, db_ref, loss_ref, acc_g, acc_b, acc_l = outs
        else:
            du_ref, dub_ref, dg_ref, db_ref, acc_g, acc_b = outs
        i = pl.program_id(0)

        @pl.when(i == 0)
        def _():
            acc_g[...] = jnp.zeros_like(acc_g)
            acc_b[...] = jnp.zeros_like(acc_b)
            if loss_mode:
                acc_l[...] = jnp.zeros_like(acc_l)

        u = a_ref[...] if ca == 1.0 else ca * a_ref[...]
        if has_r:
            u = u + r_ref[...]
        mu = jnp.mean(u, axis=-1, keepdims=True)
        xc = u - mu
        var = jnp.mean(xc * xc, axis=-1, keepdims=True)
        rstd = lax.rsqrt(var + LN_EPS)
        xhat = xc * rstd
        gv = g_ref[...]
        if loss_mode:
            err = (xhat * gv + b_ref[...]) - dy_ref[...]
            acc_l[...] += jnp.sum((err * err).reshape(tm // 8, 8, D_MODEL), axis=0)
            dyv = err * (1.0 / D_MODEL)
        else:
            dyv = dy_ref[...]
        dxh = dyv * gv
        du = rstd * (dxh - jnp.mean(dxh, axis=-1, keepdims=True) - xhat * jnp.mean(dxh * xhat, axis=-1, keepdims=True))
        du_ref[...] = du
        dub_ref[...] = du.astype(BF16)
        acc_g[...] += jnp.sum((dyv * xhat).reshape(tm // 8, 8, D_MODEL), axis=0)
        acc_b[...] += jnp.sum(dyv.reshape(tm // 8, 8, D_MODEL), axis=0)

        @pl.when(i == nt - 1)
        def _():
            dg_ref[...] = jnp.sum(acc_g[...], axis=0, keepdims=True)
            db_ref[...] = jnp.sum(acc_b[...], axis=0, keepdims=True)
            if loss_mode:
                tot = jnp.sum(jnp.sum(acc_l[...], axis=0, keepdims=True), axis=1, keepdims=True)
                loss_ref[...] = tot * (0.5 / D_MODEL)

    row = pl.BlockSpec((tm, D_MODEL), lambda i: (i, 0))
    vec = pl.BlockSpec((1, D_MODEL), lambda i: (0, 0))
    out_shape = [SDS((s, D_MODEL), F32), SDS((s, D_MODEL), BF16), SDS((1, D_MODEL), F32), SDS((1, D_MODEL), F32)]
    out_specs = [row, row, vec, vec]
    scratch = [pltpu.VMEM((8, D_MODEL), F32), pltpu.VMEM((8, D_MODEL), F32)]
    if loss_mode:
        out_shape.append(SDS((1, 1), F32))
        out_specs.append(pl.BlockSpec((1, 1), lambda i: (0, 0)))
        scratch.append(pltpu.VMEM((8, D_MODEL), F32))
    args = [a] + ([r] if has_r else []) + [dy, g, b]
    return _pcall(body, name=name, out_shape=tuple(out_shape), grid=(nt,), in_specs=[row] * (2 + has_r) + [vec, vec],
                  out_specs=tuple(out_specs), scratch_shapes=scratch, dims=("arbitrary",))(*args)


def _mm_ln_fwd(a, w, resid, g, b, ca, name):
    s, k = a.shape
    tm = _pick(s, (512, 256))

    def body(a_ref, w_ref, res_ref, g_ref, b_ref, r_ref, o_ref, ob_ref):
        rv = lax.dot_general(a_ref[...], w_ref[...], _NN, preferred_element_type=F32)
        r_ref[...] = rv
        u = ca * res_ref[...] + rv
        mu = jnp.mean(u, axis=-1, keepdims=True)
        xc = u - mu
        var = jnp.mean(xc * xc, axis=-1, keepdims=True)
        y = xc * lax.rsqrt(var + LN_EPS) * g_ref[...] + b_ref[...]
        o_ref[...] = y
        ob_ref[...] = y.astype(BF16)

    row = pl.BlockSpec((tm, D_MODEL), lambda i: (i, 0))
    vec = pl.BlockSpec((1, D_MODEL), lambda i: (0, 0))
    return _pcall(body, name=name, out_shape=(SDS((s, D_MODEL), F32), SDS((s, D_MODEL), F32), SDS((s, D_MODEL), BF16)),
                  grid=(s // tm,),
                  in_specs=[pl.BlockSpec((tm, k), lambda i: (i, 0)), pl.BlockSpec((k, D_MODEL), lambda i: (0, 0)), row, vec, vec],
                  out_specs=(row, row, row), dims=("parallel",))(a, w, resid, g, b)


def _mm_ln_bwd(pairs, addend, coef, a, r, g, ca, name, after=None):
    s = a.shape[0]
    has_r = r is not None
    extra = [] if after is None else [after]
    n_pairs = len(pairs)

    def vmem(tm):
        tot = tm * D_MODEL * (4 * (2 + has_r) + 6)
        for pa, pb, _ in pairs:
            tot += tm * pa.shape[1] * pa.dtype.itemsize + pb.size * pb.dtype.itemsize
        return 2 * tot

    tm = next(c for c in (512, 256, 128) if s % c == 0 and vmem(c) <= MM_VMEM_BUDGET)
    nt = s // tm

    def body(*refs):
        ins = refs[2 * n_pairs:]
        add_ref, a_ref = ins[0], ins[1]
        r_ref = ins[2] if has_r else None
        g_ref = ins[2 + has_r]
        du_ref, dub_ref, dg_ref, db_ref, acc_g, acc_b = ins[3 + has_r + len(extra):]
        i = pl.program_id(0)

        @pl.when(i == 0)
        def _():
            acc_g[...] = jnp.zeros_like(acc_g)
            acc_b[...] = jnp.zeros_like(acc_b)

        dyv = coef * add_ref[...]
        for p, (_, _, tb) in enumerate(pairs):
            dyv = dyv + lax.dot_general(refs[2 * p][...].astype(BF16), refs[2 * p + 1][...], _NT if tb else _NN,
                                        preferred_element_type=F32)
        u = a_ref[...] if ca == 1.0 else ca * a_ref[...]
        if has_r:
            u = u + r_ref[...]
        mu = jnp.mean(u, axis=-1, keepdims=True)
        xc = u - mu
        var = jnp.mean(xc * xc, axis=-1, keepdims=True)
        rstd = lax.rsqrt(var + LN_EPS)
        xhat = xc * rstd
        dxh = dyv * g_ref[...]
        du = rstd * (dxh - jnp.mean(dxh, axis=-1, keepdims=True) - xhat * jnp.mean(dxh * xhat, axis=-1, keepdims=True))
        du_ref[...] = du
        dub_ref[...] = du.astype(BF16)
        acc_g[...] += jnp.sum((dyv * xhat).reshape(tm // 8, 8, D_MODEL), axis=0)
        acc_b[...] += jnp.sum(dyv.reshape(tm // 8, 8, D_MODEL), axis=0)

        @pl.when(i == nt - 1)
        def _():
            dg_ref[...] = jnp.sum(acc_g[...], axis=0, keepdims=True)
            db_ref[...] = jnp.sum(acc_b[...], axis=0, keepdims=True)

    row = pl.BlockSpec((tm, D_MODEL), lambda i: (i, 0))
    vec = pl.BlockSpec((1, D_MODEL), lambda i: (0, 0))
    in_specs, args = [], []
    for pa, pb, _ in pairs:
        in_specs += [pl.BlockSpec((tm, pa.shape[1]), lambda i: (i, 0)), pl.BlockSpec(pb.shape, lambda i: (0, 0))]
        args += [pa, pb]
    in_specs += [row] * (2 + has_r) + [vec] + [pl.BlockSpec((8, 128), lambda i: (0, 0))] * len(extra)
    args += [addend, a] + ([r] if has_r else []) + [g] + extra
    return _pcall(body, name=name,
                  out_shape=(SDS((s, D_MODEL), F32), SDS((s, D_MODEL), BF16), SDS((1, D_MODEL), F32), SDS((1, D_MODEL), F32)),
                  grid=(nt,), in_specs=in_specs, out_specs=(row, row, vec, vec),
                  scratch_shapes=[pltpu.VMEM((8, D_MODEL), F32), pltpu.VMEM((8, D_MODEL), F32)],
                  dims=("arbitrary",))(*args)


def _xattn_fwd(q, k, v):
    s = q.shape[0]
    tq = _pick(s, (512,))
    scale = X_HEAD_DIM ** -0.5

    def body(q_ref, k_ref, v_ref, o_ref, ob_ref):
        qv, kv, vv = q_ref[...], k_ref[...], v_ref[...]
        outs = []
        for h in range(X_HEADS):
            sl = slice(h * X_HEAD_DIM, (h + 1) * X_HEAD_DIM)
            sc = lax.dot_general(qv[:, sl], kv[:, sl], _NT, preferred_element_type=F32) * scale
            e = jnp.exp(sc - jnp.max(sc, axis=-1, keepdims=True))
            p = e / jnp.sum(e, axis=-1, keepdims=True)
            outs.append(lax.dot_general(p.astype(BF16), vv[:, sl], _NN, preferred_element_type=F32))
        o = jnp.concatenate(outs, axis=1)
        o_ref[...] = o
        ob_ref[...] = o.astype(BF16)

    row = pl.BlockSpec((tq, D_MODEL), lambda i: (i, 0))
    full = pl.BlockSpec((MEM_LEN, D_MODEL), lambda i: (0, 0))
    return _pcall(body, name="xattn_fwd", out_shape=(SDS((s, D_MODEL), F32), SDS((s, D_MODEL), BF16)), grid=(s // tq,),
                  in_specs=[row, full, full], out_specs=(row, row), dims=("parallel",))(q, k, v)


def _xattn_bwd(q, k, v, o, do):
    s = q.shape[0]
    tq = _pick(s, (512,))
    scale = X_HEAD_DIM ** -0.5

    def body(q_ref, k_ref, v_ref, o_ref, do_ref, dq_ref, dk_ref, dv_ref):
        i = pl.program_id(0)

        @pl.when(i == 0)
        def _():
            dk_ref[...] = jnp.zeros_like(dk_ref)
            dv_ref[...] = jnp.zeros_like(dv_ref)

        qv, kv, vv, ov, dov = q_ref[...], k_ref[...], v_ref[...], o_ref[...], do_ref[...]
        dqs, dks, dvs = [], [], []
        for h in range(X_HEADS):
            sl = slice(h * X_HEAD_DIM, (h + 1) * X_HEAD_DIM)
            sc = lax.dot_general(qv[:, sl], kv[:, sl], _NT, preferred_element_type=F32) * scale
            e = jnp.exp(sc - jnp.max(sc, axis=-1, keepdims=True))
            p = e / jnp.sum(e, axis=-1, keepdims=True)
            doh = dov[:, sl]
            dob = doh.astype(BF16)
            delta = jnp.sum(doh * ov[:, sl], axis=-1, keepdims=True)
            dvs.append(lax.dot_general(p.astype(BF16), dob, _TN, preferred_element_type=F32))
            dp = lax.dot_general(dob, vv[:, sl], _NT, preferred_element_type=F32)
            ds = (p * (dp - delta)).astype(BF16)
            dqs.append(lax.dot_general(ds, kv[:, sl], _NN, preferred_element_type=F32) * scale)
            dks.append(lax.dot_general(ds, qv[:, sl], _TN, preferred_element_type=F32) * scale)
        dq_ref[...] = jnp.concatenate(dqs, axis=1).astype(BF16)
        dk_ref[...] += jnp.concatenate(dks, axis=1)
        dv_ref[...] += jnp.concatenate(dvs, axis=1)

    row = pl.BlockSpec((tq, D_MODEL), lambda i: (i, 0))
    full = pl.BlockSpec((MEM_LEN, D_MODEL), lambda i: (0, 0))
    return _pcall(body, name="xattn_bwd",
                  out_shape=(SDS((s, D_MODEL), BF16), SDS((MEM_LEN, D_MODEL), F32), SDS((MEM_LEN, D_MODEL), F32)),
                  grid=(s // tq,), in_specs=[row, full, full, row, row], out_specs=(row, full, full),
                  dims=("arbitrary",))(q, k, v, o, do)


_SQRT_HALF = 0.7071067811865476
_INV_SQRT_2PI = 0.3989422804014327


def _halo_specs(s, tm, width, rows=8):
    nb = s // rows
    r = tm // rows
    prev = pl.BlockSpec((rows, width), lambda i: (jnp.maximum(i * r - 1, 0), 0))
    nxt = pl.BlockSpec((rows, width), lambda i: (jnp.minimum((i + 1) * r, nb - 1), 0))
    return prev, nxt


def _shifted(x, before_row, after_row, i, nt):
    tm = x.shape[0]
    row = lax.broadcasted_iota(jnp.int32, x.shape, 0)
    first = jnp.where(i == 0, 0.0, 1.0) * before_row
    last = jnp.where(i == nt - 1, 0.0, 1.0) * after_row
    xm1 = jnp.where(row == 0, first, pltpu.roll(x, 1, 0))
    xp1 = jnp.where(row == tm - 1, last, pltpu.roll(x, tm - 1, 0))
    return xm1, xp1


BF16_ROWS = 16


def _ffn_fwd(hb, wg_t, wu_t, cw, cb):
    s = hb.shape[0]
    tm = _pick(s, (256,))
    nt = s // tm
    hr = BF16_ROWS

    def body(h_ref, hp_ref, hn_ref, wg_ref, wu_ref, cw_ref, cb_ref, g_ref, up_ref, act_ref):
        i = pl.program_id(0)
        hv = h_ref[...]
        g_ext = lax.dot_general(jnp.concatenate([hp_ref[...], hv, hn_ref[...]], axis=0), wg_ref[...], _NT,
                                preferred_element_type=F32)
        gv = g_ext[hr:hr + tm]
        upv = lax.dot_general(hv, wu_ref[...], _NT, preferred_element_type=F32)
        gm1, gp1 = _shifted(gv, g_ext[hr - 1:hr], g_ext[hr + tm:hr + tm + 1], i, nt)
        gc = gm1 * cw_ref[0:1, :] + gv * cw_ref[1:2, :] + gp1 * cw_ref[2:3, :] + cb_ref[...]
        gelu = 0.5 * gc * (1.0 + lax.erf(gc * _SQRT_HALF))
        g_ref[...] = gv
        up_ref[...] = upv
        act_ref[...] = (gelu * upv).astype(BF16)

    hrow = pl.BlockSpec((tm, D_MODEL), lambda i: (i, 0))
    prev, nxt = _halo_specs(s, tm, D_MODEL, hr)
    wfull = pl.BlockSpec((D_FF, D_MODEL), lambda i: (0, 0))
    row = pl.BlockSpec((tm, D_FF), lambda i: (i, 0))
    return _pcall(body, name="ffn_fwd", out_shape=(SDS((s, D_FF), F32), SDS((s, D_FF), F32), SDS((s, D_FF), BF16)),
                  grid=(nt,), in_specs=[hrow, prev, nxt, wfull, wfull, pl.BlockSpec((8, D_FF), lambda i: (0, 0)),
                                        pl.BlockSpec((1, D_FF), lambda i: (0, 0))],
                  out_specs=(row, row, row), dims=("parallel",))(hb, hb, hb, wg_t, wu_t, cw, cb)


def _ffn_bwd(dffb, w_down, g, up, cw, cb):
    s = g.shape[0]
    tm = _pick(s, (256,))
    nt = s // tm
    hr = BF16_ROWS

    def body(df_ref, dfp_ref, dfn_ref, wd_ref, g_ref, gp_ref, gn_ref, up_ref, upp_ref, upn_ref, cw_ref, cb_ref,
             dg_ref, dup_ref, dcw_ref, dcb_ref, a0, a1, a2, a3):
        i = pl.program_id(0)

        @pl.when(i == 0)
        def _():
            for a in (a0, a1, a2, a3):
                a[...] = jnp.zeros_like(a)

        def d_conv_out(gc_, up_, da_):
            cdf_ = 0.5 * (1.0 + lax.erf(gc_ * _SQRT_HALF))
            pdf_ = jnp.exp(-0.5 * gc_ * gc_) * _INV_SQRT_2PI
            return da_ * up_ * (cdf_ + gc_ * pdf_), cdf_

        df_ext = jnp.concatenate([dfp_ref[...], df_ref[...], dfn_ref[...]], axis=0)
        tn = 256
        for c0 in range(0, D_FF, tn):
            cs = slice(c0, c0 + tn)
            da_ext = lax.dot_general(df_ext, wd_ref[cs, :], _NT, preferred_element_type=F32)
            cw0, cw1, cw2, cbv = cw_ref[0:1, cs], cw_ref[1:2, cs], cw_ref[2:3, cs], cb_ref[:, cs]
            gv = g_ref[:, cs]
            g_before, g_after = gp_ref[:, cs], gn_ref[:, cs]
            gm1, gp1 = _shifted(gv, g_before[7:8, :], g_after[0:1, :], i, nt)
            gc = gm1 * cw0 + gv * cw1 + gp1 * cw2 + cbv
            da = da_ext[hr:hr + tm]
            dgc, cdf = d_conv_out(gc, up_ref[:, cs], da)
            dup_ref[:, cs] = (da * (gc * cdf)).astype(BF16)
            gc_b = g_before[6:7, :] * cw0 + g_before[7:8, :] * cw1 + gv[0:1, :] * cw2 + cbv
            gc_a = gv[tm - 1:tm, :] * cw0 + g_after[0:1, :] * cw1 + g_after[1:2, :] * cw2 + cbv
            dgc_b = jnp.where(i == 0, 0.0, 1.0) * d_conv_out(gc_b, upp_ref[7:8, cs], da_ext[hr - 1:hr])[0]
            dgc_a = jnp.where(i == nt - 1, 0.0, 1.0) * d_conv_out(gc_a, upn_ref[0:1, cs], da_ext[hr + tm:hr + tm + 1])[0]
            row = lax.broadcasted_iota(jnp.int32, dgc.shape, 0)
            dgc_m1 = jnp.where(row == 0, dgc_b, pltpu.roll(dgc, 1, 0))
            dgc_p1 = jnp.where(row == tm - 1, dgc_a, pltpu.roll(dgc, tm - 1, 0))
            dg_ref[:, cs] = (dgc_p1 * cw0 + dgc * cw1 + dgc_m1 * cw2).astype(BF16)

            def fold(t):
                return jnp.sum(t.reshape(tm // 8, 8, tn), axis=0)

            a0[:, cs] += fold(dgc * gm1)
            a1[:, cs] += fold(dgc * gv)
            a2[:, cs] += fold(dgc * gp1)
            a3[:, cs] += fold(dgc)

        @pl.when(i == nt - 1)
        def _():
            dcw_ref[...] = jnp.concatenate(
                [jnp.sum(a[...], axis=0, keepdims=True) for a in (a0, a1, a2)] + [jnp.zeros((5, D_FF), F32)], axis=0)
            dcb_ref[...] = jnp.sum(a3[...], axis=0, keepdims=True)

    row = pl.BlockSpec((tm, D_FF), lambda i: (i, 0))
    prev, nxt = _halo_specs(s, tm, D_FF)
    cw_spec = pl.BlockSpec((8, D_FF), lambda i: (0, 0))
    cb_spec = pl.BlockSpec((1, D_FF), lambda i: (0, 0))
    dprev, dnxt = _halo_specs(s, tm, D_MODEL, hr)
    return _pcall(body, name="ffn_bwd",
                  out_shape=(SDS((s, D_FF), BF16), SDS((s, D_FF), BF16), SDS((8, D_FF), F32), SDS((1, D_FF), F32)),
                  grid=(nt,),
                  in_specs=[pl.BlockSpec((tm, D_MODEL), lambda i: (i, 0)), dprev, dnxt,
                            pl.BlockSpec((D_FF, D_MODEL), lambda i: (0, 0))] + [row, prev, nxt] * 2 + [cw_spec, cb_spec],
                  out_specs=(row, row, cw_spec, cb_spec), scratch_shapes=[pltpu.VMEM((8, D_FF), F32)] * 4,
                  dims=("arbitrary",))(dffb, dffb, dffb, w_down, g, g, g, up, up, up, cw, cb)


def _adamw(w, g, m, v, name):
    rows, cols = w.shape
    tr = _pick(rows, (256, 128, 64, 32, 16, 8))
    c1 = 1.0 - ADAM_B1 ** ADAM_STEP
    c2 = 1.0 - ADAM_B2 ** ADAM_STEP

    def body(w_ref, g_ref, m_ref, v_ref, d_ref, nm_ref, nv_ref):
        gv = g_ref[...]
        nm = ADAM_B1 * m_ref[...] + (1.0 - ADAM_B1) * gv
        nv = ADAM_B2 * v_ref[...] + (1.0 - ADAM_B2) * (gv * gv)
        d_ref[...] = -ADAM_LR * ((nm / c1) / (jnp.sqrt(nv / c2) + ADAM_EPS) + ADAM_WD * w_ref[...])
        nm_ref[...] = nm
        nv_ref[...] = nv

    blk = pl.BlockSpec((tr, cols), lambda i: (i, 0))
    return _pcall(body, name=name, out_shape=(SDS(w.shape, F32),) * 3, grid=(rows // tr,), in_specs=[blk] * 4,
                  out_specs=(blk,) * 3, dims=("parallel",))(w, g, m, v)


def _adamw_many(ws, gs, ms, vs, name):
    n = len(ws)
    c1 = 1.0 - ADAM_B1 ** ADAM_STEP
    c2 = 1.0 - ADAM_B2 ** ADAM_STEP

    def body(*refs):
        outs = refs[4 * n:]
        for k in range(n):
            gv = refs[n + k][...]
            nm = ADAM_B1 * refs[2 * n + k][...] + (1.0 - ADAM_B1) * gv
            nv = ADAM_B2 * refs[3 * n + k][...] + (1.0 - ADAM_B2) * (gv * gv)
            outs[k][...] = -ADAM_LR * ((nm / c1) / (jnp.sqrt(nv / c2) + ADAM_EPS) + ADAM_WD * refs[k][...])
            outs[n + k][...] = nm
            outs[2 * n + k][...] = nv

    shapes = tuple(SDS(w.shape, F32) for w in ws)
    res = _pcall(body, name=name, out_shape=shapes * 3)(*ws, *gs, *ms, *vs)
    return res[:n], res[n:2 * n], res[2 * n:]


def _all_gather_rows(x_shard, *, name, in_vmem, sum_rows=False, after=None):
    m_per, n = x_shard.shape
    extra = [] if after is None else [after]

    def body(x_ref, *rest):
        out_ref, rest = rest[len(extra)], rest[len(extra) + 1:]
        if sum_rows:
            sum_ref, send_sems, recv_sems, local_sem = rest
        else:
            send_sems, recv_sems, local_sem = rest
        x, y, c = lax.axis_index("x"), lax.axis_index("y"), lax.axis_index("c")
        me, sibling = (x, y, c), (x, y, 1 - c)
        chips = [(1 - x, y), (x, 1 - y), (1 - x, 1 - y)]

        def rows(px, py, pc):
            return out_ref.at[pl.ds((4 * px + 2 * py + pc) * m_per, m_per), :]

        def copy(k, block, to, src=None):
            return pltpu.make_async_remote_copy(
                src_ref=rows(*block) if src is None else src, dst_ref=rows(*block), send_sem=send_sems.at[k],
                recv_sem=recv_sems.at[k], device_id=to, device_id_type=pl.DeviceIdType.MESH)

        mine = pltpu.make_async_copy(x_ref, rows(*me), local_sem)
        mine.start()
        first = [copy(0, me, sibling, src=x_ref)]
        first += [copy(1 + j, me, (*chip, c), src=x_ref) for j, chip in enumerate(chips)]
        for cp in first:
            cp.start()
        passed = [copy(4 + j, (*chip, c), sibling) for j, chip in enumerate(chips)]
        for j, chip in enumerate(chips):
            copy(1 + j, (*chip, c), me).wait_recv()
            passed[j].start()
        copy(0, sibling, me).wait_recv()
        for j, chip in enumerate(chips):
            copy(4 + j, (*chip, 1 - c), me).wait_recv()
        for cp in first + passed:
            cp.wait_send()
        mine.wait()
        if sum_rows:
            acc = out_ref[0:m_per, :]
            for dev in range(1, N_DEV):
                acc = acc + out_ref[dev * m_per:(dev + 1) * m_per, :]
            sum_ref[...] = acc

    space = pltpu.VMEM if in_vmem else pl.ANY
    out_shape = [SDS((N_DEV * m_per, n), x_shard.dtype)]
    out_specs = [pl.BlockSpec(memory_space=space)]
    if sum_rows:
        out_shape.append(SDS((m_per, n), x_shard.dtype))
        out_specs.append(pl.BlockSpec(memory_space=pltpu.VMEM))
    res = _PALLAS_CALL(
        body, name=name, out_shape=tuple(out_shape),
        in_specs=[pl.BlockSpec(memory_space=space)] + [pl.BlockSpec(memory_space=pl.ANY)] * len(extra),
        out_specs=tuple(out_specs),
        scratch_shapes=[pltpu.SemaphoreType.DMA((7,)), pltpu.SemaphoreType.DMA((7,)), pltpu.SemaphoreType.DMA],
        compiler_params=pltpu.CompilerParams(vmem_limit_bytes=VMEM_LIMIT_BYTES),
    )(x_shard, *extra)
    return res if sum_rows else res[0]


_HBM = pl.BlockSpec(memory_space=pltpu.HBM)
_SEM = pl.BlockSpec(memory_space=pltpu.SEMAPHORE)
_SPLIT_PARAMS = dict(has_side_effects=pltpu.SideEffectType.DATAFLOW_SIDE_EFFECTING)


def _split_copies(src_ref, land_ref, send_sems, recv_sems, gather):
    x, y, c = lax.axis_index("x"), lax.axis_index("y"), lax.axis_index("c")
    copies = []
    for k in range(1, N_DEV):
        px = 1 - x if k & 4 else x
        py = 1 - y if k & 2 else y
        pc = 1 - c if k & 1 else c
        if gather:
            rows = src_ref.shape[0]
            src, dst = src_ref, land_ref.at[pl.ds((4 * x + 2 * y + c) * rows, rows), :]
        else:
            src, dst = src_ref.at[4 * px + 2 * py + pc], land_ref.at[k - 1]
        copies.append(pltpu.make_async_remote_copy(
            src_ref=src, dst_ref=dst, send_sem=send_sems.at[k - 1], recv_sem=recv_sems.at[k - 1],
            device_id=(px, py, pc), device_id_type=pl.DeviceIdType.MESH))
    return copies


def _exchange_start(src, land_shape, *, gather, name):
    def body(src_ref, land_ref, send_sems, recv_sems, src_thru, land_thru, token):
        for cp in _split_copies(src_ref, land_ref, send_sems, recv_sems, gather):
            cp.start()
        token[...] = jnp.zeros_like(token)

    land = pltpu.with_memory_space_constraint(lax.empty(land_shape, src.dtype), pltpu.HBM)
    return _PALLAS_CALL(
        body, name=name,
        out_shape=(pltpu.SemaphoreType.DMA((N_DEV - 1,)), pltpu.SemaphoreType.DMA((N_DEV - 1,)),
                   pltpu.HBM(src.shape, src.dtype), pltpu.HBM(land_shape, src.dtype), SDS((8, 128), F32)),
        in_specs=(_HBM, _HBM), out_specs=(_SEM, _SEM, _HBM, _HBM, pl.BlockSpec(memory_space=pltpu.VMEM)),
        input_output_aliases={0: 2, 1: 3}, compiler_params=pltpu.CompilerParams(**_SPLIT_PARAMS),
    )(pltpu.with_memory_space_constraint(src, pltpu.HBM), land)


def _exchange_wait(started, after, *, gather, name):
    send_sems, recv_sems, src_thru, land_thru, _ = started

    def body(src_ref, land_ref, send_sems, recv_sems, after_ref, src_out, land_out):
        copies = _split_copies(src_ref, land_ref, send_sems, recv_sems, gather)
        for cp in copies:
            cp.wait_send()
        for cp in copies:
            cp.wait_recv()

    return _PALLAS_CALL(
        body, name=name,
        out_shape=(pltpu.HBM(src_thru.shape, src_thru.dtype), pltpu.HBM(land_thru.shape, land_thru.dtype)),
        in_specs=(_HBM, _HBM, _SEM, _SEM, pl.BlockSpec(memory_space=pl.ANY)), out_specs=(_HBM, _HBM),
        input_output_aliases={0: 0, 1: 1}, compiler_params=pltpu.CompilerParams(**_SPLIT_PARAMS),
    )(src_thru, land_thru, send_sems, recv_sems, after)


def _sum_parts(own, land, name):
    r, n = own.shape
    tr = _pick(r, (264, 320, 336, 128, 64, 32, 16, 8))

    def body(own_ref, x_ref, o_ref):
        acc = own_ref[...]
        for k in range(N_DEV - 1):
            acc = acc + x_ref[k].astype(F32)
        o_ref[...] = acc

    return _pcall(body, name=name, out_shape=SDS((r, n), F32), grid=(r // tr,),
                  in_specs=[pl.BlockSpec((tr, n), lambda i: (i, 0)), pl.BlockSpec((N_DEV - 1, tr, n), lambda i: (0, i, 0))],
                  out_specs=pl.BlockSpec((tr, n), lambda i: (i, 0)), dims=("parallel",))(own, land)


def _pad_rows(a, rows):
    return jnp.pad(a, ((0, rows - a.shape[0]), (0, 0)))


def kernel(x, mem, positions, ln_in_g, ln_in_b, w_in, attn_sink, g_win, g_dil, w_mix_out, ln1_g, ln1_b, mem_ln_g, mem_ln_b, w_xq, w_xk, w_xv, w_xo, ln2_g, ln2_b, w_gate, w_up, conv_w, conv_b, w_down, ln3_g, ln3_b, loss_target, m_ln_in_g, m_ln_in_b, m_w_in, m_attn_sink, m_g_win, m_g_dil, m_w_mix_out, m_ln1_g, m_ln1_b, m_mem_ln_g, m_mem_ln_b, m_w_xq, m_w_xk, m_w_xv, m_w_xo, m_ln2_g, m_ln2_b, m_w_gate, m_w_up, m_conv_w, m_conv_b, m_w_down, m_ln3_g, m_ln3_b, v_ln_in_g, v_ln_in_b, v_w_in, v_attn_sink, v_g_win, v_g_dil, v_w_mix_out, v_ln1_g, v_ln1_b, v_mem_ln_g, v_mem_ln_b, v_w_xq, v_w_xk, v_w_xv, v_w_xo, v_ln2_g, v_ln2_b, v_w_gate, v_w_up, v_conv_w, v_conv_b, v_w_down, v_ln3_g, v_ln3_b):
    weights = dict(ln_in_g=ln_in_g, ln_in_b=ln_in_b, w_in=w_in, attn_sink=attn_sink, g_win=g_win, g_dil=g_dil, w_mix_out=w_mix_out, ln1_g=ln1_g, ln1_b=ln1_b, mem_ln_g=mem_ln_g, mem_ln_b=mem_ln_b, w_xq=w_xq, w_xk=w_xk, w_xv=w_xv, w_xo=w_xo, ln2_g=ln2_g, ln2_b=ln2_b, w_gate=w_gate, w_up=w_up, conv_w=conv_w, conv_b=conv_b, w_down=w_down, ln3_g=ln3_g, ln3_b=ln3_b)
    mom_m = dict(ln_in_g=m_ln_in_g, ln_in_b=m_ln_in_b, w_in=m_w_in, attn_sink=m_attn_sink, g_win=m_g_win, g_dil=m_g_dil, w_mix_out=m_w_mix_out, ln1_g=m_ln1_g, ln1_b=m_ln1_b, mem_ln_g=m_mem_ln_g, mem_ln_b=m_mem_ln_b, w_xq=m_w_xq, w_xk=m_w_xk, w_xv=m_w_xv, w_xo=m_w_xo, ln2_g=m_ln2_g, ln2_b=m_ln2_b, w_gate=m_w_gate, w_up=m_w_up, conv_w=m_conv_w, conv_b=m_conv_b, w_down=m_w_down, ln3_g=m_ln3_g, ln3_b=m_ln3_b)
    mom_v = dict(ln_in_g=v_ln_in_g, ln_in_b=v_ln_in_b, w_in=v_w_in, attn_sink=v_attn_sink, g_win=v_g_win, g_dil=v_g_dil, w_mix_out=v_w_mix_out, ln1_g=v_ln1_g, ln1_b=v_ln1_b, mem_ln_g=v_mem_ln_g, mem_ln_b=v_mem_ln_b, w_xq=v_w_xq, w_xk=v_w_xk, w_xv=v_w_xv, w_xo=v_w_xo, ln2_g=v_ln2_g, ln2_b=v_ln2_b, w_gate=v_w_gate, w_up=v_w_up, conv_w=v_conv_w, conv_b=v_conv_b, w_down=v_w_down, ln3_g=v_ln3_g, ln3_b=v_ln3_b)
    order = list(weights)
    s = x.shape[1]
    xs = x[0]
    mems = mem[0]
    target = loss_target[0]
    row = lambda a: a.reshape(1, -1)

    shard_rows = dict(w_in=w_in[0].T, w_gate=w_gate[0].T, w_up=w_up[0].T, w_mix_out=w_mix_out[0], w_xq=w_xq[0],
                      w_xk=w_xk[0], w_xv=w_xv[0], w_xo=w_xo[0], w_down=w_down[0])
    me_lin = 4 * lax.axis_index("x") + 2 * lax.axis_index("y") + lax.axis_index("c")
    w_in_full = _all_gather_rows(shard_rows["w_in"].astype(BF16), name="w_in_all_gather", in_vmem=False)
    w_in_t = jnp.concatenate([w_in_full[768:], w_in_full[:768]], axis=0)
    late_rows = PACK_ROWS[1:]
    late_r = sum(r for _, r in late_rows)
    packed = jnp.concatenate([shard_rows[n].astype(BF16) for n, _ in late_rows], axis=0)
    w_started = _exchange_start(packed, (N_DEV * late_r, D_MODEL), gather=True, name="weight_gather_start")
    cw_pad = jnp.pad(conv_w[0], ((0, 5), (0, 32)))
    cw_all = _all_gather_rows(cw_pad, name="conv_w_all_gather", in_vmem=True).reshape(N_DEV, 8, 384)
    cw_full = jnp.transpose(cw_all[:, :3, :352], (1, 0, 2)).reshape(3, D_FF)
    cw8 = _pad_rows(cw_full, 8)

    tabs = _rope_tables(positions.astype(F32).reshape(s, 1) + w_started[4][0, 0])
    h0, h0b = _ln_fwd(xs, None, row(ln_in_g), row(ln_in_b), 1.0, "ln_in_fwd")
    zw, *zg = _proj_rope(h0b, w_in_t, tabs[0])
    oa, lse_a = _banded_fwd(zw, attn_sink, name="win_attn_fwd", **_WIN_CFG)
    og_views, lg_views = [], []
    for gi in range(3):
        o_g, l_g = _banded_fwd(zg[gi], None, name=f"dil_attn_fwd{gi}", **_dil_cfg(gi))
        og_views.append(o_g)
        lg_views.append(l_g)
    mixed, ob_views, lb_views = _mix_norm_fwd(oa, og_views, lg_views, g_win, g_dil)
    packed_thru, land = _exchange_wait(w_started, mixed, gather=True, name="weight_gather_wait")
    gathered = lax.dynamic_update_slice(land, packed_thru, (me_lin * late_r, 0)).reshape(N_DEV, late_r, D_MODEL)
    full = {}
    off = 0
    for n, r in late_rows:
        full[n] = gathered[:, off:off + r, :].reshape(N_DEV * r, D_MODEL)
        off += r
    mix, h1, h1b = _mm_ln_fwd(mixed, full["w_mix_out"], h0, ln1_g, ln1_b, ALPHA, "mm_mix_out_ln1")
    _, mem_nb = _ln_fwd(mems, None, mem_ln_g, mem_ln_b, 1.0, "mem_ln_fwd")
    kx = _mm(mem_nb, full["w_xk"], trans_b=False, out_dtype=BF16, name="mm_xk")
    vx = _mm(mem_nb, full["w_xv"], trans_b=False, out_dtype=BF16, name="mm_xv")
    qx = _mm(h1b, full["w_xq"], trans_b=False, out_dtype=BF16, name="mm_xq")
    ox, oxb = _xattn_fwd(qx, kx, vx)
    xa, h2, h2b = _mm_ln_fwd(oxb, full["w_xo"], h1, ln2_g, ln2_b, ALPHA, "mm_xo_ln2")
    gate, up, act = _ffn_fwd(h2b, full["w_gate"], full["w_up"], cw8, conv_b)
    ff = _mm(act, full["w_down"], trans_b=False, out_dtype=F32, name="mm_down")

    du3, du3b, d_ln3_g, d_ln3_b, loss_local = _ln_bwd(h2, ff, target, ln3_g, ln3_b, ALPHA, "ln3_bwd_loss",
                                                      loss_mode=True)
    dw_down = _mm_tn(act, du3b, name="mm_dw_down")
    dgate, dup, dcw8, d_conv_b = _ffn_bwd(du3b, full["w_down"], gate, up, cw8, conv_b)
    dw_gate_t = _mm_tn(dgate, h2b, name="mm_dw_gate")
    dw_up_t = _mm_tn(dup, h2b, name="mm_dw_up")
    rows_of = dict(PACK_ROWS)

    own_f32 = {}

    def start_grad_exchange(parts, name, payload=F32):
        gp = jnp.concatenate([g.reshape(N_DEV, rows_of[n], D_MODEL) for n, g in parts], axis=1)
        if payload != F32:
            own_f32[name] = lax.dynamic_index_in_dim(gp, me_lin, axis=0, keepdims=False)
            gp = gp.astype(payload)
        return _exchange_start(gp, (N_DEV - 1,) + gp.shape[1:], gather=False, name=name)

    ffn_parts = (("w_gate", dw_gate_t), ("w_up", dw_up_t), ("w_down", dw_down))
    ffn_started = start_grad_exchange(ffn_parts, "grad_start_ffn")
    du2, du2b, d_ln2_g, d_ln2_b = _mm_ln_bwd(((dgate, full["w_gate"], False), (dup, full["w_up"], False)), du3, ALPHA,
                                             h1, xa, ln2_g + ffn_started[4][0, 0], ALPHA, "mm_dh2_ln2_bwd")
    dox = _mm(du2b, full["w_xo"], trans_b=True, out_dtype=F32, name="mm_d_ox")
    dw_xo = _mm_tn(oxb, du2b, name="mm_dw_xo")
    dqx, dkx, dvx = _xattn_bwd(qx, kx, vx, ox, dox)
    dw_xq = _mm_tn(h1b, dqx, name="mm_dw_xq")
    dw_xk = _mm_tn(mem_nb, dkx, name="mm_dw_xk")
    dw_xv = _mm_tn(mem_nb, dvx, name="mm_dw_xv")
    dmem_n = _mm(dkx, full["w_xk"], trans_b=True, out_dtype=F32, name="mm_dmem", more=((dvx, full["w_xv"], True),))
    _, _, d_mem_ln_g, d_mem_ln_b = _ln_bwd(mems, None, dmem_n, mem_ln_g, mem_ln_b, 1.0, "mem_ln_bwd")
    du1, du1b, d_ln1_g, d_ln1_b = _mm_ln_bwd(((dqx, full["w_xq"], True),), du2, ALPHA, h0, mix, ln1_g, ALPHA,
                                             "mm_dh1_ln1_bwd")
    dmixed = _mm(du1b, full["w_mix_out"], trans_b=True, out_dtype=F32, name="mm_d_mixed")
    dw_mix_out = _mm_tn(mixed, du1b, name="mm_dw_mix_out")
    attn_parts = (("w_mix_out", dw_mix_out), ("w_xq", dw_xq), ("w_xk", dw_xk), ("w_xv", dw_xv), ("w_xo", dw_xo))
    attn_started = start_grad_exchange(attn_parts, "grad_start_attn")
    doa, dob_views, d_g_win, d_g_dil = _mix_norm_bwd(oa, ob_views[0], dmixed, g_win + attn_started[4][0, 0], g_dil)
    dqa, dkva, dsink8 = _banded_bwd(zw, oa, lse_a, doa, tabs[0], attn_sink, name="win_attn_bwd", **_WIN_CFG)
    dq_views, dkv_views = [], []
    for gi in range(3):
        dq_g, dkv_g = _banded_bwd(zg[gi], ob_views[gi], lb_views[gi], dob_views[gi], tabs[gi], None,
                                  name=f"dil_attn_bwd{gi}", **_dil_cfg(gi))
        dq_views.append(dq_g)
        dkv_views.append(dkv_g)
    dz = _dz_assemble(dq_views, dkv_views, dqa, dkva)
    dw_in_tz = _mm_tn(dz, h0b, name="mm_dw_in")
    dw_in_t = jnp.concatenate([dw_in_tz[4608:], dw_in_tz[:4608]], axis=0)
    in_parts = (("w_in", dw_in_t),)
    in_started = start_grad_exchange(in_parts, "grad_start_in", payload=BF16)
    dx, _, d_ln_in_g, d_ln_in_b = _mm_ln_bwd(((dz, w_in_t, False),), du1, ALPHA, xs, None, row(ln_in_g), 1.0,
                                             "mm_dh0_ln_in_bwd", after=in_started[4])

    grads, delta, new_m, new_v = {}, {}, {}, {}
    after = dx
    for parts, started, tag in ((ffn_parts, ffn_started, "ffn"), (attn_parts, attn_started, "attn"),
                                (in_parts, in_started, "in")):
        gp_thru, land = _exchange_wait(started, after, gather=False, name=f"grad_wait_{tag}")
        own = own_f32.get(f"grad_start_{tag}")
        if own is None:
            own = lax.dynamic_index_in_dim(gp_thru, me_lin, axis=0, keepdims=False)
        gsum = _sum_parts(own, land, f"grad_sum_{tag}")
        off = 0
        for n, _ in parts:
            blk = gsum[off:off + rows_of[n]]
            off += rows_of[n]
            grads[n] = (blk.T if n in ("w_in", "w_gate", "w_up") else blk)[None]
            shp = weights[n].shape
            d_, m_, v_ = _adamw(weights[n].reshape(shp[1:]), grads[n].reshape(shp[1:]), mom_m[n].reshape(shp[1:]),
                                mom_v[n].reshape(shp[1:]), f"adamw_{n}")
            delta[n], new_m[n], new_v[n] = d_.reshape(shp), m_.reshape(shp), v_.reshape(shp)
            after = d_

    small = jnp.concatenate([
        d_ln_in_g, d_ln_in_b, d_ln1_g, d_ln1_b, d_mem_ln_g, d_mem_ln_b, d_ln2_g, d_ln2_b, d_ln3_g, d_ln3_b,
        jnp.concatenate([d_g_win, d_g_dil], axis=1),
        jnp.pad(d_conv_b, ((0, 0), (0, 3072 - D_FF))).reshape(3, 1024),
        jnp.pad(dsink8[0:1, :], ((0, 0), (0, 1024 - 128))),
        jnp.pad(dcw8[0:3], ((0, 0), (0, 3072 - D_FF))).reshape(9, 1024),
    ], axis=0)
    _, ssum = _all_gather_rows(small, name="small_grad_all_reduce", in_vmem=True, sum_rows=True, after=after)
    names10 = ["ln_in_g", "ln_in_b", "ln1_g", "ln1_b", "mem_ln_g", "mem_ln_b", "ln2_g", "ln2_b", "ln3_g", "ln3_b"]
    small_g = {n: ssum[i:i + 1] for i, n in enumerate(names10)}
    small_g["g_win"] = ssum[10:11, :512]
    small_g["g_dil"] = ssum[10:11, 512:]
    small_g["conv_b"] = ssum[11:14].reshape(1, 3072)[:, :D_FF]
    small_g["attn_sink"] = ssum[14:15, :8]
    small_g["conv_w"] = lax.dynamic_slice_in_dim(ssum[15:24].reshape(3, 3072)[:, :D_FF], me_lin * 352, 352, axis=1)

    small_names = [n for n in order if n not in rows_of]
    two_d = lambda a: a.reshape(-1, a.shape[-1])
    d_s, m_s, v_s = _adamw_many([two_d(weights[n]) for n in small_names], [small_g[n] for n in small_names],
                                [two_d(mom_m[n]) for n in small_names], [two_d(mom_v[n]) for n in small_names],
                                "adamw_small")
    for k, n in enumerate(small_names):
        shp = weights[n].shape
        grads[n], delta[n], new_m[n], new_v[n] = (t.reshape(shp) for t in (small_g[n], d_s[k], m_s[k], v_s[k]))

    loss = lax.psum(loss_local[0, 0], MESH_AXES)
    return (loss, dx[None], *[grads[n] for n in order], *[delta[n] for n in order], *[new_m[n] for n in order],
            *[new_v[n] for n in order])
```

```python
import functools
import math

import jax
import jax.numpy as jnp
from jax import lax
from jax.experimental import pallas as pl
from jax.experimental.pallas import tpu as pltpu

F32 = jnp.float32
BF16 = jnp.bfloat16
SDS = jax.ShapeDtypeStruct
_PALLAS_CALL = pl.pallas_call

D_MODEL = 1024
HEAD_DIM = 64
WIN_HALF = 128
DIL_PAIRS = ((128, 1), (512, 4), (2048, 16))
DIL_SIDE = 64
ROT_DIM = 16
ROPE_THETA = 500000.0
MEM_LEN = 256
X_HEADS = 4
X_HEAD_DIM = 256
D_FF = 2816
IN_WIDTH = 5376
Z_QB, Z_KB, Z_VB, Z_QA, Z_KA, Z_VA = 0, 1536, 3072, 4608, 5120, 5248
ALPHA = (2.0) ** 0.25
LN_EPS = 1e-5
NEG_INF = -1e30
ADAM_LR, ADAM_B1, ADAM_B2, ADAM_EPS, ADAM_WD, ADAM_STEP = 0.001, 0.9, 0.999, 1e-08, 0.01, 10
N_DEV = 8
MESH_AXES = ("x", "y", "c")
VMEM_LIMIT_BYTES = 52 * 1024 * 1024
ATTN_TQ = 256
TABW = 384

PACK_ROWS = (("w_in", 672), ("w_gate", 352), ("w_up", 352), ("w_mix_out", 128), ("w_xq", 128), ("w_xk", 128),
             ("w_xv", 128), ("w_xo", 128), ("w_down", 352))
SMALL_ROWS = 24


def _pick(n, cands):
    for c in cands:
        if n % c == 0:
            return c
    return n


def _pcall(body, *, name, out_shape, grid=None, in_specs=None, out_specs=None, scratch_shapes=(), dims=None,
           aliases=None):
    kw = {}
    if grid is not None:
        kw["grid"] = grid
    if in_specs is not None:
        kw["in_specs"] = in_specs
    if out_specs is not None:
        kw["out_specs"] = out_specs
    if aliases:
        kw["input_output_aliases"] = aliases
    return _PALLAS_CALL(
        body, name=name, out_shape=out_shape, scratch_shapes=list(scratch_shapes),
        compiler_params=pltpu.CompilerParams(dimension_semantics=dims, vmem_limit_bytes=VMEM_LIMIT_BYTES), **kw)


MM_VMEM_BUDGET = 40 * 1024 * 1024


def _mm(a, b, *, trans_b, out_dtype, name, addends=(), coefs=(), after=None, more=()):
    pairs = ((a, b, trans_b),) + tuple(more)
    m = a.shape[0]
    n = b.shape[0] if trans_b else b.shape[1]
    n_add = len(addends)
    extra = [] if after is None else [after]
    out_bytes = jnp.dtype(out_dtype).itemsize

    def vmem(tm, tn):
        tot = tm * tn * (out_bytes + 4 * n_add)
        for pa, pb, _ in pairs:
            tot += tm * pa.shape[1] * pa.dtype.itemsize + pa.shape[1] * tn * pb.dtype.itemsize
        return 2 * tot

    tm, tn = next(((cm, cn) for cn in (n, 1408, 1024, 512, 256, 128) if n % cn == 0
                   for cm in (1024, 512, 256, 128) if m % cm == 0 and vmem(cm, cn) <= MM_VMEM_BUDGET))
    n_pairs = len(pairs)

    def body(*refs):
        o_ref = refs[2 * n_pairs + n_add + len(extra)]
        acc = None
        for p, (_, _, tb) in enumerate(pairs):
            dn = _NT if tb else _NN
            part = lax.dot_general(refs[2 * p][...].astype(BF16), refs[2 * p + 1][...].astype(BF16), dn,
                                   preferred_element_type=F32)
            acc = part if acc is None else acc + part
        for r_ref, c in zip(refs[2 * n_pairs:2 * n_pairs + n_add], coefs):
            acc = acc + (r_ref[...] if c == 1.0 else c * r_ref[...])
        o_ref[...] = acc.astype(out_dtype)

    in_specs, args = [], []
    for pa, pb, tb in pairs:
        k = pa.shape[1]
        in_specs.append(pl.BlockSpec((tm, k), lambda j, i: (i, 0)))
        in_specs.append(pl.BlockSpec((tn, k), lambda j, i: (j, 0)) if tb else pl.BlockSpec((k, tn), lambda j, i: (0, j)))
        args += [pa, pb]
    in_specs += [pl.BlockSpec((tm, tn), lambda j, i: (i, j)) for _ in addends]
    in_specs += [pl.BlockSpec((8, 128), lambda j, i: (0, 0)) for _ in extra]
    return _pcall(body, name=name, out_shape=SDS((m, n), out_dtype), grid=(n // tn, m // tm), in_specs=in_specs,
                  out_specs=pl.BlockSpec((tm, tn), lambda j, i: (i, j)),
                  dims=("parallel", "parallel"))(*args, *addends, *extra)


def _mm_tn(a, b, *, name):
    s, m = a.shape
    n = b.shape[1]
    tm = _pick(m, (768, 1408, 1024, 512, 256, 128))
    tk = _pick(s, (1024, 512, 256))
    nk = s // tk

    def body(a_ref, b_ref, o_ref, acc_ref):
        kk = pl.program_id(1)

        @pl.when(kk == 0)
        def _():
            acc_ref[...] = jnp.zeros_like(acc_ref)

        acc_ref[...] += lax.dot_general(a_ref[...].astype(BF16), b_ref[...].astype(BF16), (((0,), (0,)), ((), ())),
                                        preferred_element_type=F32)

        @pl.when(kk == nk - 1)
        def _():
            o_ref[...] = acc_ref[...]

    return _pcall(body, name=name, out_shape=SDS((m, n), F32), grid=(m // tm, nk),
                  in_specs=[pl.BlockSpec((tk, tm), lambda i, kk: (kk, i)), pl.BlockSpec((tk, n), lambda i, kk: (kk, 0))],
                  out_specs=pl.BlockSpec((tm, n), lambda i, kk: (i, 0)), scratch_shapes=[pltpu.VMEM((tm, n), F32)],
                  dims=("parallel", "arbitrary"))(a, b)


def _rope_lane_consts():
    lane = jnp.arange(128)
    j = lane % HEAD_DIM
    inv_freq = ROPE_THETA ** (-jnp.arange(0, ROT_DIM, 2, dtype=F32) / ROT_DIM)
    freq = jnp.where(j < ROT_DIM, inv_freq[j % (ROT_DIM // 2)], 0.0).astype(F32)
    lo = (j < ROT_DIM // 2).astype(F32)
    hi = ((j >= ROT_DIM // 2) & (j < ROT_DIM)).astype(F32)
    return jnp.stack([freq, lo, hi] + [jnp.zeros((128,), F32)] * 5)


def _to_classes(x, scr, d):
    if d == 1:
        return [x]
    scr[...] = x
    return [scr[pl.ds(c, x.shape[0] // d, stride=d), :] for c in range(d)]


def _from_classes(parts, scr):
    d = len(parts)
    if d == 1:
        return parts[0]
    for c, part in enumerate(parts):
        scr[pl.ds(c, part.shape[0], stride=d), :] = part
    return scr[...]


DILATIONS = tuple(d for _, d in DIL_PAIRS)


def _rope_tables(posf):
    s = posf.shape[0]
    tm = _pick(s, (1024, 512))

    def body(p_ref, c_ref, *rest):
        o_refs, scr = rest[:-1], rest[-1]
        ang = p_ref[...] * c_ref[0:1, :]
        lo = c_ref[1:2, :]
        hi = c_ref[2:3, :]
        cs = jnp.cos(ang)
        sn = jnp.sin(ang)
        for q, t in enumerate((jnp.where(lo + hi > 0.0, cs, 1.0), -sn * lo, sn * hi)):
            for o_ref, d in zip(o_refs, DILATIONS):
                for c, part in enumerate(_to_classes(t, scr, d)):
                    o_ref[:, c * TABW + q * 128:c * TABW + (q + 1) * 128] = part

    return _pcall(body, name="rope_tables", out_shape=tuple(SDS((s // d, d * TABW), F32) for d in DILATIONS),
                  grid=(s // tm,),
                  in_specs=[pl.BlockSpec((tm, 1), lambda i: (i, 0)), pl.BlockSpec((8, 128), lambda i: (0, 0))],
                  out_specs=tuple(pl.BlockSpec((tm // d, d * TABW), lambda i: (i, 0)) for d in DILATIONS),
                  scratch_shapes=[pltpu.VMEM((tm, 128), F32)], dims=("parallel",))(posf, _rope_lane_consts())


def _rope_apply(x, tab, sign):
    w = x.shape[1]
    rep = w // 128
    c = jnp.tile(tab[:, 0:128], (1, rep)) if rep > 1 else tab[:, 0:128]
    a = jnp.tile(tab[:, 128:256], (1, rep)) if rep > 1 else tab[:, 128:256]
    b = jnp.tile(tab[:, 256:384], (1, rep)) if rep > 1 else tab[:, 256:384]
    up = pltpu.roll(x, w - 8, 1)
    dn = pltpu.roll(x, 8, 1)
    if sign > 0:
        return x * c + up * a + dn * b
    return x * c - up * a - dn * b


def _proj_rope(h0b, w_t, tab):
    s = h0b.shape[0]
    tm = _pick(s, (512,))
    tn = 256

    def body(a_ref, w_ref, t_ref, zw_ref, z0_ref, z1_ref, z2_ref, scr):
        z_refs = (z0_ref, z1_ref, z2_ref)
        a = a_ref[...]
        tabv = t_ref[...]
        for c0 in range(0, IN_WIDTH, tn):
            z = lax.dot_general(a, w_ref[c0:c0 + tn, :], _NT, preferred_element_type=F32)
            for g0 in range(c0, c0 + tn, 128):
                zg = z[:, g0 - c0:g0 - c0 + 128]
                if g0 < Z_VB or Z_QA <= g0 < Z_VA:
                    zg = _rope_apply(zg, tabv, 1)
                if g0 >= Z_QA:
                    zw_ref[:, g0 - Z_QA:g0 - Z_QA + 128] = zg.astype(BF16)
                    continue
                kind, within = divmod(g0, 1536)
                grp, off = divmod(within, 512)
                col = kind * 512 + off
                for c, part in enumerate(_to_classes(zg, scr, DILATIONS[grp])):
                    z_refs[grp][:, c * 1536 + col:c * 1536 + col + 128] = part.astype(BF16)

    return _pcall(body, name="proj_rope",
                  out_shape=(SDS((s, 768), BF16),) + tuple(SDS((s // d, d * 1536), BF16) for d in DILATIONS),
                  grid=(s // tm,),
                  in_specs=[pl.BlockSpec((tm, D_MODEL), lambda i: (i, 0)), pl.BlockSpec((IN_WIDTH, D_MODEL), lambda i: (0, 0)),
                            pl.BlockSpec((tm, TABW), lambda i: (i, 0))],
                  out_specs=(pl.BlockSpec((tm, 768), lambda i: (i, 0)),)
                  + tuple(pl.BlockSpec((tm // d, d * 1536), lambda i: (i, 0)) for d in DILATIONS),
                  scratch_shapes=[pltpu.VMEM((tm, 128), F32)], dims=("parallel",))(h0b, w_t, tab)


def _band_specs(sd, blk, tq, width, per_tok, cb):
    r = tq // blk
    nbk = sd // blk
    prev = pl.BlockSpec((blk, width), lambda c, j: (jnp.maximum(j * r - 1, 0), c * per_tok + cb))
    cur = pl.BlockSpec((tq, width), lambda c, j: (j, c * per_tok + cb))
    nxt = pl.BlockSpec((blk, width), lambda c, j: (jnp.minimum((j + 1) * r, nbk - 1), c * per_tok + cb))
    return [prev, cur, nxt]


def _band_bias(q0, rows, blk, sd):
    shape = (rows, rows + 2 * blk)
    qpos = q0 + lax.broadcasted_iota(jnp.int32, shape, 0)
    kpos = q0 - blk + lax.broadcasted_iota(jnp.int32, shape, 1)
    ok = (jnp.abs(qpos - kpos) <= blk) & (kpos >= 0) & (kpos < sd)
    return jnp.where(ok, 0.0, NEG_INF)


_NT = (((1,), (1,)), ((), ()))
_NN = (((1,), (0,)), ((), ()))
_TN = (((0,), (0,)), ((), ()))


def _banded_fwd(zv, sink, *, d, blk, tq, rc, ptw, qw, kw, qcb, kcb, vcb, pairs, name):
    sd = zv.shape[0]
    tq = min(tq, sd)
    rc = min(rc, tq)
    has_sink = sink is not None
    scale = HEAD_DIM ** -0.5

    def body(q_ref, kp, kc, kn, vp, vc, vn, *rest):
        if has_sink:
            sink_ref, o_ref, lse_ref = rest
        else:
            o_ref, lse_ref = rest
        j = pl.program_id(1)
        q = q_ref[...] * scale
        k = jnp.concatenate([kp[...], kc[...], kn[...]], axis=0)
        v = jnp.concatenate([vp[...], vc[...], vn[...]], axis=0)
        biases = {r0: _band_bias(j * tq + r0, rc, blk, sd) for r0 in range(0, tq, rc)}
        low = lax.broadcasted_iota(jnp.int32, (1, 128), 1) < HEAD_DIM
        for qb, kb, vb, swaps, sinks in pairs:
            qp, kp_, vp_ = q[:, qb:qb + 128], k[:, kb:kb + 128], v[:, vb:vb + 128]
            if any(swaps):
                k_sw = jnp.concatenate([kp_[:, HEAD_DIM:], kp_[:, :HEAD_DIM]], axis=1)
                v_sw = jnp.concatenate([vp_[:, HEAD_DIM:], vp_[:, :HEAD_DIM]], axis=1)
            for r0 in range(0, tq, rc):
                outs, lses = [], []
                for half in range(2):
                    qm = jnp.where(low if half == 0 else ~low, qp[r0:r0 + rc], jnp.zeros((rc, 128), BF16))
                    kk, vv = (k_sw, v_sw) if swaps[half] else (kp_, vp_)
                    kk, vv = kk[r0:r0 + rc + 2 * blk], vv[r0:r0 + rc + 2 * blk]
                    sc = lax.dot_general(qm, kk, _NT, preferred_element_type=F32) + biases[r0]
                    m = jnp.max(sc, axis=-1, keepdims=True)
                    if has_sink:
                        m = jnp.maximum(m, sink_ref[0, sinks[half]])
                    p = jnp.exp(sc - m)
                    den = jnp.sum(p, axis=-1, keepdims=True)
                    if has_sink:
                        den = den + jnp.exp(sink_ref[0, sinks[half]] - m)
                    outs.append(lax.dot_general(p.astype(BF16), vv, _NN, preferred_element_type=F32) / den)
                    lses.append(m + jnp.log(den))
                o_ref[r0:r0 + rc, qb:qb + 128] = jnp.where(low, outs[0], outs[1])
                lse_ref[r0:r0 + rc, qb:qb + 128] = jnp.where(low, lses[0], lses[1])

    in_specs = ([pl.BlockSpec((tq, qw), lambda c, j: (j, c * (ptw // qw) + qcb))]
                + _band_specs(sd, blk, tq, kw, ptw // kw, kcb) + _band_specs(sd, blk, tq, kw, ptw // kw, vcb))
    args = [zv] * 7
    if has_sink:
        in_specs.append(pl.BlockSpec(memory_space=pltpu.SMEM))
        args.append(sink)
    o_spec = pl.BlockSpec((tq, qw), lambda c, j: (j, c))
    return _pcall(body, name=name, out_shape=(SDS((sd, d * qw), F32), SDS((sd, d * qw), F32)), grid=(d, sd // tq),
                  in_specs=in_specs, out_specs=(o_spec, o_spec), dims=("parallel", "parallel"))(*args)


def _banded_bwd(zv, ov, lv, dov, tv, sink, *, d, blk, tq, rc, ptw, qw, kw, qcb, kcb, vcb, pairs, name):
    sd = zv.shape[0]
    tq = min(tq, sd)
    nt = sd // tq
    r = tq // blk
    nbk = sd // blk
    has_sink = sink is not None
    scale = HEAD_DIM ** -0.5
    rc = min(rc, tq)
    kvw = 128 * len({kb for _, kb, _, _, _ in pairs})

    def add_rows(x, y, last):
        if tq == blk:
            return x + y
        if last:
            return jnp.concatenate([x[:tq - blk], x[tq - blk:] + y], axis=0)
        return jnp.concatenate([x[:blk] + y, x[blk:]], axis=0)

    def body(q_ref, kp, kc, kn, vp, vc, vn, o_ref, l_ref, do_ref, t_ref, tlag_ref, *rest):
        if has_sink:
            sink_ref, dq_ref, dkv_ref, dsink_ref, acck, accv, nxtk, nxtv = rest
        else:
            dq_ref, dkv_ref, acck, accv, nxtk, nxtv = rest
        j = pl.program_id(1)

        @pl.when(j == 0)
        def _():
            nxtk[...] = jnp.zeros_like(nxtk)
            nxtv[...] = jnp.zeros_like(nxtv)

        if has_sink:
            @pl.when((pl.program_id(0) == 0) & (j == 0))
            def _():
                dsink_ref[...] = jnp.zeros_like(dsink_ref)

        def emit(dk_rows, dv_rows):
            dkv_ref[...] = jnp.concatenate([_rope_apply(dk_rows, tlag_ref[...], -1), dv_rows], axis=1).astype(BF16)

        @pl.when(j < nt)
        def _():
            q = q_ref[...] * scale
            k3 = jnp.concatenate([kp[...], kc[...], kn[...]], axis=0)
            v3 = jnp.concatenate([vp[...], vc[...], vn[...]], axis=0)
            o_t, l_t, do_t = o_ref[...], l_ref[...], do_ref[...]
            biases = {r0: _band_bias(j * tq + r0, rc, blk, sd) for r0 in range(0, tq, rc)}
            lane = lax.broadcasted_iota(jnp.int32, (1, 128), 1)
            low = lane < HEAD_DIM
            wide = tq + 2 * blk
            cw = rc + 2 * blk

            def place(x, r0):
                parts = ([jnp.zeros((r0, 128), F32)] if r0 else []) + [x]
                if wide - r0 - cw:
                    parts.append(jnp.zeros((wide - r0 - cw, 128), F32))
                return jnp.concatenate(parts, axis=0) if len(parts) > 1 else x

            dqs = []
            wks, wvs = {}, {}
            dsink_row = jnp.zeros((1, 128), F32)
            for qb, kb, vb, swaps, sinks in pairs:
                qp, kp_, vp_ = q[:, qb:qb + 128], k3[:, kb:kb + 128], v3[:, vb:vb + 128]
                if any(swaps):
                    k_sw = jnp.concatenate([kp_[:, HEAD_DIM:], kp_[:, :HEAD_DIM]], axis=1)
                    v_sw = jnp.concatenate([vp_[:, HEAD_DIM:], vp_[:, :HEAD_DIM]], axis=1)
                dop, lp = do_t[:, qb:qb + 128], l_t[:, qb:qb + 128]
                prod = dop * o_t[:, qb:qb + 128]
                dq_rows = []
                for r0 in range(0, tq, rc):
                    rows = slice(r0, r0 + rc)
                    dq_half = []
                    for half in range(2):
                        mine = low if half == 0 else ~low
                        qm = jnp.where(mine, qp[rows], jnp.zeros((rc, 128), BF16))
                        dob = jnp.where(mine, dop[rows], 0.0).astype(BF16)
                        delta = jnp.sum(jnp.where(mine, prod[rows], 0.0), axis=-1, keepdims=True)
                        lse = lp[rows, half * HEAD_DIM:half * HEAD_DIM + 1]
                        kk, vv = (k_sw, v_sw) if swaps[half] else (kp_, vp_)
                        kk, vv = kk[r0:r0 + cw], vv[r0:r0 + cw]
                        sc = lax.dot_general(qm, kk, _NT, preferred_element_type=F32) + biases[r0]
                        p = jnp.exp(sc - lse)
                        dp = lax.dot_general(dob, vv, _NT, preferred_element_type=F32)
                        dsb = (p * (dp - delta)).astype(BF16)
                        dq_half.append(lax.dot_general(dsb, kk, _NN, preferred_element_type=F32))
                        dk = lax.dot_general(dsb, qm, _TN, preferred_element_type=F32)
                        dv = lax.dot_general(p.astype(BF16), dob, _TN, preferred_element_type=F32)
                        if swaps[half]:
                            dk, dv = pltpu.roll(dk, HEAD_DIM, 1), pltpu.roll(dv, HEAD_DIM, 1)
                        wks[kb] = place(dk, r0) if kb not in wks else wks[kb] + place(dk, r0)
                        wvs[vb] = place(dv, r0) if vb not in wvs else wvs[vb] + place(dv, r0)
                        if has_sink:
                            psink = jnp.exp(sink_ref[0, sinks[half]] - lse)
                            dsink_row = dsink_row + jnp.where(lane == sinks[half], -jnp.sum(psink * delta), 0.0)
                    dq_rows.append(jnp.where(low, dq_half[0], dq_half[1]) * scale)
                dqs.append(jnp.concatenate(dq_rows, axis=0) if len(dq_rows) > 1 else dq_rows[0])
            dq_ref[...] = _rope_apply(jnp.concatenate(dqs, axis=1), t_ref[...], -1).astype(BF16)
            wk = jnp.concatenate([wks[b] for b in sorted(wks)], axis=1) if len(wks) > 1 else wks[min(wks)]
            wv = jnp.concatenate([wvs[b] for b in sorted(wvs)], axis=1) if len(wvs) > 1 else wvs[min(wvs)]
            if has_sink:
                dsink_ref[0:1, :] += dsink_row

            @pl.when(j > 0)
            def _():
                emit(add_rows(acck[...], wk[:blk], True), add_rows(accv[...], wv[:blk], True))

            acck[...] = add_rows(wk[blk:blk + tq], nxtk[...], False)
            accv[...] = add_rows(wv[blk:blk + tq], nxtv[...], False)
            nxtk[...] = wk[blk + tq:]
            nxtv[...] = wv[blk + tq:]

        @pl.when(j == nt)
        def _():
            emit(acck[...], accv[...])

    def tile(width, per_tok, cb):
        return pl.BlockSpec((tq, width), lambda c, j: (jnp.minimum(j, nt - 1), c * per_tok + cb))

    def halos(width, per_tok, cb):
        before = pl.BlockSpec((blk, width), lambda c, j: (jnp.maximum(jnp.minimum(j, nt - 1) * r - 1, 0), c * per_tok + cb))
        after = pl.BlockSpec((blk, width),
                             lambda c, j: (jnp.minimum((jnp.minimum(j, nt - 1) + 1) * r, nbk - 1), c * per_tok + cb))
        return [before, tile(width, per_tok, cb), after]

    def lagged(width):
        return pl.BlockSpec((tq, width), lambda c, j: (jnp.maximum(j - 1, 0), c))

    in_specs = ([tile(qw, ptw // qw, qcb)] + halos(kw, ptw // kw, kcb) + halos(kw, ptw // kw, vcb)
                + [tile(qw, 1, 0)] * 3 + [tile(TABW, 1, 0), lagged(TABW)])
    args = [zv] * 7 + [ov, lv, dov, tv, tv]
    out_shape = [SDS((sd, d * qw), BF16), SDS((sd, d * 2 * kvw), BF16)]
    out_specs = [tile(qw, 1, 0), lagged(2 * kvw)]
    if has_sink:
        in_specs.append(pl.BlockSpec(memory_space=pltpu.SMEM))
        args.append(sink)
        out_shape.append(SDS((8, 128), F32))
        out_specs.append(pl.BlockSpec((8, 128), lambda c, j: (0, 0)))
    scratch = [pltpu.VMEM((tq, kvw), F32), pltpu.VMEM((tq, kvw), F32), pltpu.VMEM((blk, kvw), F32),
               pltpu.VMEM((blk, kvw), F32)]
    return _pcall(body, name=name, out_shape=tuple(out_shape), grid=(d, nt + 1), in_specs=in_specs,
                  out_specs=tuple(out_specs), scratch_shapes=scratch, dims=("arbitrary", "arbitrary"))(*args)


_WIN_PAIRS = tuple((128 * p, 0, 128, (False, True) if p < 2 else (True, False), (2 * p, 2 * p + 1)) for p in range(4))
_WIN_CFG = dict(d=1, blk=WIN_HALF, tq=ATTN_TQ, rc=256, ptw=768, qw=512, kw=256, qcb=0, kcb=2, vcb=2, pairs=_WIN_PAIRS)
_DIL_PAIRS = tuple((128 * p, 128 * p, 128 * p, (False, False), (2 * p, 2 * p + 1)) for p in range(4))


def _dil_cfg(gi):
    return dict(d=DILATIONS[gi], blk=DIL_SIDE, tq=ATTN_TQ, rc=128, ptw=1536, qw=512, kw=512, qcb=0, kcb=1, vcb=2,
                pairs=_DIL_PAIRS)


def _view_specs(tm, width):
    return tuple(pl.BlockSpec((tm // d, d * width), lambda i: (i, 0)) for d in DILATIONS)


def _mix_norm_fwd(oa, og_views, lg_views, g_win, g_dil):
    s = oa.shape[0]
    tm = _pick(s, (512,))

    def body(oa_ref, o0, o1, o2, l0, l1, l2, gw_ref, gd_ref, mixed_ref, ob0, ob1, ob2, lb0, lb1, lb2, scr, ob_s):
        o_refs, l_refs, ob_refs, lb_refs = (o0, o1, o2), (l0, l1, l2), (ob0, ob1, ob2), (lb0, lb1, lb2)
        ssq = jnp.zeros((tm, 1), F32)
        for q in range(4):
            os_, ls_ = [], []
            for g, d in enumerate(DILATIONS):
                cols = [slice(c * 512 + q * 128, c * 512 + (q + 1) * 128) for c in range(d)]
                os_.append(_from_classes([o_refs[g][:, cs] for cs in cols], scr))
                ls_.append(_from_classes([l_refs[g][:, cs] for cs in cols], scr))
            mx = jnp.maximum(jnp.maximum(ls_[0], ls_[1]), ls_[2])
            es = [jnp.exp(l - mx) for l in ls_]
            den = es[0] + es[1] + es[2]
            ob = (es[0] / den) * os_[0] + (es[1] / den) * os_[1] + (es[2] / den) * os_[2]
            lb = mx + jnp.log(den)
            ob_s[:, q * 128:(q + 1) * 128] = ob
            ssq = ssq + jnp.sum(ob * ob, axis=-1, keepdims=True)
            for g, d in enumerate(DILATIONS):
                for val, refs in ((ob, ob_refs), (lb, lb_refs)):
                    for c, part in enumerate(_to_classes(val, scr, d)):
                        refs[g][:, c * 512 + q * 128:c * 512 + (q + 1) * 128] = part
        a = oa_ref[...]
        ra = lax.rsqrt(jnp.mean(a * a, axis=-1, keepdims=True) + LN_EPS)
        rb = lax.rsqrt(ssq * (1.0 / 512) + LN_EPS)
        mixed_ref[...] = jnp.concatenate([a * ra * gw_ref[...], ob_s[...] * rb * gd_ref[...]], axis=1).astype(BF16)

    row = pl.BlockSpec((tm, 512), lambda i: (i, 0))
    vec = pl.BlockSpec((1, 512), lambda i: (0, 0))
    views = _view_specs(tm, 512)
    view_shapes = tuple(SDS((s // d, d * 512), F32) for d in DILATIONS)
    res = _pcall(body, name="mix_norm_fwd", out_shape=(SDS((s, 1024), BF16),) + view_shapes * 2, grid=(s // tm,),
                 in_specs=[row, *views, *views, vec, vec],
                 out_specs=(pl.BlockSpec((tm, 1024), lambda i: (i, 0)),) + views * 2,
                 scratch_shapes=[pltpu.VMEM((tm, 128), F32), pltpu.VMEM((tm, 512), F32)],
                 dims=("parallel",))(oa, *og_views, *lg_views, g_win, g_dil)
    return res[0], res[1:4], res[4:7]


def _mix_norm_bwd(oa, ob, dmixed, g_win, g_dil):
    s = oa.shape[0]
    tm = _pick(s, (512,))
    nt = s // tm

    def body(oa_ref, ob_ref, dm_ref, gw_ref, gd_ref, doa_ref, db0, db1, db2, dgw_ref, dgd_ref, acc_w, acc_d, scr):
        i = pl.program_id(0)

        @pl.when(i == 0)
        def _():
            acc_w[...] = jnp.zeros_like(acc_w)
            acc_d[...] = jnp.zeros_like(acc_d)

        dm = dm_ref[...]
        dxs = []
        for x_ref, g_ref, dy, acc in ((oa_ref, gw_ref, dm[:, :512], acc_w), (ob_ref, gd_ref, dm[:, 512:], acc_d)):
            x = x_ref[...]
            r = lax.rsqrt(jnp.mean(x * x, axis=-1, keepdims=True) + LN_EPS)
            dyg = dy * g_ref[...]
            dxs.append(r * dyg - x * (r * r * r) * jnp.mean(dyg * x, axis=-1, keepdims=True))
            acc[...] += jnp.sum((dy * x * r).reshape(tm // 8, 8, 512), axis=0)
        doa_ref[...] = dxs[0]
        for q in range(4):
            dq = dxs[1][:, q * 128:(q + 1) * 128]
            for db_ref, d in zip((db0, db1, db2), DILATIONS):
                for c, part in enumerate(_to_classes(dq, scr, d)):
                    db_ref[:, c * 512 + q * 128:c * 512 + (q + 1) * 128] = part

        @pl.when(i == nt - 1)
        def _():
            dgw_ref[...] = jnp.sum(acc_w[...], axis=0, keepdims=True)
            dgd_ref[...] = jnp.sum(acc_d[...], axis=0, keepdims=True)

    row = pl.BlockSpec((tm, 512), lambda i: (i, 0))
    vec = pl.BlockSpec((1, 512), lambda i: (0, 0))
    views = _view_specs(tm, 512)
    view_shapes = tuple(SDS((s // d, d * 512), F32) for d in DILATIONS)
    res = _pcall(body, name="mix_norm_bwd",
                 out_shape=(SDS((s, 512), F32),) + view_shapes + (SDS((1, 512), F32), SDS((1, 512), F32)),
                 grid=(nt,), in_specs=[row, row, pl.BlockSpec((tm, 1024), lambda i: (i, 0)), vec, vec],
                 out_specs=(row,) + views + (vec, vec),
                 scratch_shapes=[pltpu.VMEM((8, 512), F32), pltpu.VMEM((8, 512), F32), pltpu.VMEM((tm, 128), F32)],
                 dims=("arbitrary",))(oa, ob, dmixed, g_win, g_dil)
    return res[0], res[1:4], res[4], res[5]


def _dz_assemble(dq_views, dkv_views, dqa, dkva):
    s = dqa.shape[0]
    tm = _pick(s, (512,))

    def body(q0, q1, q2, kv0, kv1, kv2, qa_ref, kva_ref, o_ref, scr):
        for g, d in enumerate(DILATIONS):
            for kind, (ref, width, base) in enumerate((((q0, q1, q2)[g], 512, 0), ((kv0, kv1, kv2)[g], 1024, 0),
                                                       ((kv0, kv1, kv2)[g], 1024, 512))):
                for q in range(4):
                    src = base + q * 128
                    dst = kind * 1536 + g * 512 + q * 128
                    if d == 1:
                        o_ref[:, dst:dst + 128] = ref[:, src:src + 128]
                    else:
                        parts = [ref[:, c * width + src:c * width + src + 128].astype(F32) for c in range(d)]
                        o_ref[:, dst:dst + 128] = _from_classes(parts, scr).astype(BF16)
        o_ref[:, Z_QA:Z_QA + 512] = qa_ref[...]
        o_ref[:, Z_KA:Z_KA + 256] = kva_ref[...]

    return _pcall(body, name="dz_assemble", out_shape=SDS((s, IN_WIDTH), BF16), grid=(s // tm,),
                  in_specs=[*_view_specs(tm, 512), *_view_specs(tm, 1024), pl.BlockSpec((tm, 512), lambda i: (i, 0)),
                            pl.BlockSpec((tm, 256), lambda i: (i, 0))],
                  out_specs=pl.BlockSpec((tm, IN_WIDTH), lambda i: (i, 0)),
                  scratch_shapes=[pltpu.VMEM((tm, 128), F32)], dims=("parallel",))(*dq_views, *dkv_views, dqa, dkva)


def _ln_fwd(a, r, g, b, ca, name):
    s = a.shape[0]
    tm = _pick(s, (512, 256))
    has_r = r is not None

    def body(*refs):
        a_ref = refs[0]
        r_ref = refs[1] if has_r else None
        g_ref, b_ref, o_ref, ob_ref = refs[1 + has_r:]
        u = a_ref[...] if ca == 1.0 else ca * a_ref[...]
        if has_r:
            u = u + r_ref[...]
        mu = jnp.mean(u, axis=-1, keepdims=True)
        xc = u - mu
        var = jnp.mean(xc * xc, axis=-1, keepdims=True)
        y = xc * lax.rsqrt(var + LN_EPS) * g_ref[...] + b_ref[...]
        o_ref[...] = y
        ob_ref[...] = y.astype(BF16)

    row = pl.BlockSpec((tm, D_MODEL), lambda i: (i, 0))
    vec = pl.BlockSpec((1, D_MODEL), lambda i: (0, 0))
    args = [a] + ([r] if has_r else []) + [g, b]
    return _pcall(body, name=name, out_shape=(SDS((s, D_MODEL), F32), SDS((s, D_MODEL), BF16)), grid=(s // tm,),
                  in_specs=[row] * (1 + has_r) + [vec, vec], out_specs=(row, row), dims=("parallel",))(*args)


def _ln_bwd(a, r, dy, g, b, ca, name, loss_mode=False):
    s = a.shape[0]
    tm = _pick(s, (512, 256))
    nt = s // tm
    has_r = r is not None

    def body(*refs):
        a_ref = refs[0]
        r_ref = refs[1] if has_r else None
        dy_ref, g_ref, b_ref = refs[1 + has_r:4 + has_r]
        outs = refs[4 + has_r:]
        if loss_mode:
            du_ref, dub_ref, dg_ref, db_ref, loss_ref, acc_g, acc_b, acc_l = outs
        else:
            du_ref, dub_ref, dg_ref, db_ref, acc_g, acc_b = outs
        i = pl.program_id(0)

        @pl.when(i == 0)
        def _():
            acc_g[...] = jnp.zeros_like(acc_g)
            acc_b[...] = jnp.zeros_like(acc_b)
            if loss_mode:
                acc_l[...] = jnp.zeros_like(acc_l)

        u = a_ref[...] if ca == 1.0 else ca * a_ref[...]
        if has_r:
            u = u + r_ref[...]
        mu = jnp.mean(u, axis=-1, keepdims=True)
        xc = u - mu
        var = jnp.mean(xc * xc, axis=-1, keepdims=True)
        rstd = lax.rsqrt(var + LN_EPS)
        xhat = xc * rstd
        gv = g_ref[...]
        if loss_mode:
            err = (xhat * gv + b_ref[...]) - dy_ref[...]
            acc_l[...] += jnp.sum((err * err).reshape(tm // 8, 8, D_MODEL), axis=0)
            dyv = err * (1.0 / D_MODEL)
        else:
            dyv = dy_ref[...]
        dxh = dyv * gv
        du = rstd * (dxh - jnp.mean(dxh, axis=-1, keepdims=True) - xhat * jnp.mean(dxh * xhat, axis=-1, keepdims=True))
        du_ref[...] = du
        dub_ref[...] = du.astype(BF16)
        acc_g[...] += jnp.sum((dyv * xhat).reshape(tm // 8, 8, D_MODEL), axis=0)
        acc_b[...] += jnp.sum(dyv.reshape(tm // 8, 8, D_MODEL), axis=0)

        @pl.when(i == nt - 1)
        def _():
            dg_ref[...] = jnp.sum(acc_g[...], axis=0, keepdims=True)
            db_ref[...] = jnp.sum(acc_b[...], axis=0, keepdims=True)
            if loss_mode:
                tot = jnp.sum(jnp.sum(acc_l[...], axis=0, keepdims=True), axis=1, keepdims=True)
                loss_ref[...] = tot * (0.5 / D_MODEL)

    row = pl.BlockSpec((tm, D_MODEL), lambda i: (i, 0))
    vec = pl.BlockSpec((1, D_MODEL), lambda i: (0, 0))
    out_shape = [SDS((s, D_MODEL), F32), SDS((s, D_MODEL), BF16), SDS((1, D_MODEL), F32), SDS((1, D_MODEL), F32)]
    out_specs = [row, row, vec, vec]
    scratch = [pltpu.VMEM((8, D_MODEL), F32), pltpu.VMEM((8, D_MODEL), F32)]
    if loss_mode:
        out_shape.append(SDS((1, 1), F32))
        out_specs.append(pl.BlockSpec((1, 1), lambda i: (0, 0)))
        scratch.append(pltpu.VMEM((8, D_MODEL), F32))
    args = [a] + ([r] if has_r else []) + [dy, g, b]
    return _pcall(body, name=name, out_shape=tuple(out_shape), grid=(nt,), in_specs=[row] * (2 + has_r) + [vec, vec],
                  out_specs=tuple(out_specs), scratch_shapes=scratch, dims=("arbitrary",))(*args)


def _mm_ln_fwd(a, w, resid, g, b, ca, name):
    s, k = a.shape
    tm = _pick(s, (512, 256))

    def body(a_ref, w_ref, res_ref, g_ref, b_ref, r_ref, o_ref, ob_ref):
        rv = lax.dot_general(a_ref[...], w_ref[...], _NN, preferred_element_type=F32)
        r_ref[...] = rv
        u = ca * res_ref[...] + rv
        mu = jnp.mean(u, axis=-1, keepdims=True)
        xc = u - mu
        var = jnp.mean(xc * xc, axis=-1, keepdims=True)
        y = xc * lax.rsqrt(var + LN_EPS) * g_ref[...] + b_ref[...]
        o_ref[...] = y
        ob_ref[...] = y.astype(BF16)

    row = pl.BlockSpec((tm, D_MODEL), lambda i: (i, 0))
    vec = pl.BlockSpec((1, D_MODEL), lambda i: (0, 0))
    return _pcall(body, name=name, out_shape=(SDS((s, D_MODEL), F32), SDS((s, D_MODEL), F32), SDS((s, D_MODEL), BF16)),
                  grid=(s // tm,),
                  in_specs=[pl.BlockSpec((tm, k), lambda i: (i, 0)), pl.BlockSpec((k, D_MODEL), lambda i: (0, 0)), row, vec, vec],
                  out_specs=(row, row, row), dims=("parallel",))(a, w, resid, g, b)


def _mm_ln_bwd(pairs, addend, coef, a, r, g, ca, name, after=None):
    s = a.shape[0]
    has_r = r is not None
    extra = [] if after is None else [after]
    n_pairs = len(pairs)

    def vmem(tm):
        tot = tm * D_MODEL * (4 * (2 + has_r) + 6)
        for pa, pb, _ in pairs:
            tot += tm * pa.shape[1] * pa.dtype.itemsize + pb.size * pb.dtype.itemsize
        return 2 * tot

    tm = next(c for c in (512, 256, 128) if s % c == 0 and vmem(c) <= MM_VMEM_BUDGET)
    nt = s // tm

    def body(*refs):
        ins = refs[2 * n_pairs:]
        add_ref, a_ref = ins[0], ins[1]
        r_ref = ins[2] if has_r else None
        g_ref = ins[2 + has_r]
        du_ref, dub_ref, dg_ref, db_ref, acc_g, acc_b = ins[3 + has_r + len(extra):]
        i = pl.program_id(0)

        @pl.when(i == 0)
        def _():
            acc_g[...] = jnp.zeros_like(acc_g)
            acc_b[...] = jnp.zeros_like(acc_b)

        dyv = coef * add_ref[...]
        for p, (_, _, tb) in enumerate(pairs):
            dyv = dyv + lax.dot_general(refs[2 * p][...].astype(BF16), refs[2 * p + 1][...], _NT if tb else _NN,
                                        preferred_element_type=F32)
        u = a_ref[...] if ca == 1.0 else ca * a_ref[...]
        if has_r:
            u = u + r_ref[...]
        mu = jnp.mean(u, axis=-1, keepdims=True)
        xc = u - mu
        var = jnp.mean(xc * xc, axis=-1, keepdims=True)
        rstd = lax.rsqrt(var + LN_EPS)
        xhat = xc * rstd
        dxh = dyv * g_ref[...]
        du = rstd * (dxh - jnp.mean(dxh, axis=-1, keepdims=True) - xhat * jnp.mean(dxh * xhat, axis=-1, keepdims=True))
        du_ref[...] = du
        dub_ref[...] = du.astype(BF16)
        acc_g[...] += jnp.sum((dyv * xhat).reshape(tm // 8, 8, D_MODEL), axis=0)
        acc_b[...] += jnp.sum(dyv.reshape(tm // 8, 8, D_MODEL), axis=0)

        @pl.when(i == nt - 1)
        def _():
            dg_ref[...] = jnp.sum(acc_g[...], axis=0, keepdims=True)
            db_ref[...] = jnp.sum(acc_b[...], axis=0, keepdims=True)

    row = pl.BlockSpec((tm, D_MODEL), lambda i: (i, 0))
    vec = pl.BlockSpec((1, D_MODEL), lambda i: (0, 0))
    in_specs, args = [], []
    for pa, pb, _ in pairs:
        in_specs += [pl.BlockSpec((tm, pa.shape[1]), lambda i: (i, 0)), pl.BlockSpec(pb.shape, lambda i: (0, 0))]
        args += [pa, pb]
    in_specs += [row] * (2 + has_r) + [vec] + [pl.BlockSpec((8, 128), lambda i: (0, 0))] * len(extra)
    args += [addend, a] + ([r] if has_r else []) + [g] + extra
    return _pcall(body, name=name,
                  out_shape=(SDS((s, D_MODEL), F32), SDS((s, D_MODEL), BF16), SDS((1, D_MODEL), F32), SDS((1, D_MODEL), F32)),
                  grid=(nt,), in_specs=in_specs, out_specs=(row, row, vec, vec),
                  scratch_shapes=[pltpu.VMEM((8, D_MODEL), F32), pltpu.VMEM((8, D_MODEL), F32)],
                  dims=("arbitrary",))(*args)


def _xattn_fwd(q, k, v):
    s = q.shape[0]
    tq = _pick(s, (512,))
    scale = X_HEAD_DIM ** -0.5

    def body(q_ref, k_ref, v_ref, o_ref, ob_ref):
        qv, kv, vv = q_ref[...], k_ref[...], v_ref[...]
        outs = []
        for h in range(X_HEADS):
            sl = slice(h * X_HEAD_DIM, (h + 1) * X_HEAD_DIM)
            sc = lax.dot_general(qv[:, sl], kv[:, sl], _NT, preferred_element_type=F32) * scale
            e = jnp.exp(sc - jnp.max(sc, axis=-1, keepdims=True))
            p = e / jnp.sum(e, axis=-1, keepdims=True)
            outs.append(lax.dot_general(p.astype(BF16), vv[:, sl], _NN, preferred_element_type=F32))
        o = jnp.concatenate(outs, axis=1)
        o_ref[...] = o
        ob_ref[...] = o.astype(BF16)

    row = pl.BlockSpec((tq, D_MODEL), lambda i: (i, 0))
    full = pl.BlockSpec((MEM_LEN, D_MODEL), lambda i: (0, 0))
    return _pcall(body, name="xattn_fwd", out_shape=(SDS((s, D_MODEL), F32), SDS((s, D_MODEL), BF16)), grid=(s // tq,),
                  in_specs=[row, full, full], out_specs=(row, row), dims=("parallel",))(q, k, v)


def _xattn_bwd(q, k, v, o, do):
    s = q.shape[0]
    tq = _pick(s, (512,))
    scale = X_HEAD_DIM ** -0.5

    def body(q_ref, k_ref, v_ref, o_ref, do_ref, dq_ref, dk_ref, dv_ref):
        i = pl.program_id(0)

        @pl.when(i == 0)
        def _():
            dk_ref[...] = jnp.zeros_like(dk_ref)
            dv_ref[...] = jnp.zeros_like(dv_ref)

        qv, kv, vv, ov, dov = q_ref[...], k_ref[...], v_ref[...], o_ref[...], do_ref[...]
        dqs, dks, dvs = [], [], []
        for h in range(X_HEADS):
            sl = slice(h * X_HEAD_DIM, (h + 1) * X_HEAD_DIM)
            sc = lax.dot_general(qv[:, sl], kv[:, sl], _NT, preferred_element_type=F32) * scale
            e = jnp.exp(sc - jnp.max(sc, axis=-1, keepdims=True))
            p = e / jnp.sum(e, axis=-1, keepdims=True)
            doh = dov[:, sl]
            dob = doh.astype(BF16)
            delta = jnp.sum(doh * ov[:, sl], axis=-1, keepdims=True)
            dvs.append(lax.dot_general(p.astype(BF16), dob, _TN, preferred_element_type=F32))
            dp = lax.dot_general(dob, vv[:, sl], _NT, preferred_element_type=F32)
            ds = (p * (dp - delta)).astype(BF16)
            dqs.append(lax.dot_general(ds, kv[:, sl], _NN, preferred_element_type=F32) * scale)
            dks.append(lax.dot_general(ds, qv[:, sl], _TN, preferred_element_type=F32) * scale)
        dq_ref[...] = jnp.concatenate(dqs, axis=1).astype(BF16)
        dk_ref[...] += jnp.concatenate(dks, axis=1)
        dv_ref[...] += jnp.concatenate(dvs, axis=1)

    row = pl.BlockSpec((tq, D_MODEL), lambda i: (i, 0))
    full = pl.BlockSpec((MEM_LEN, D_MODEL), lambda i: (0, 0))
    return _pcall(body, name="xattn_bwd",
                  out_shape=(SDS((s, D_MODEL), BF16), SDS((MEM_LEN, D_MODEL), F32), SDS((MEM_LEN, D_MODEL), F32)),
                  grid=(s // tq,), in_specs=[row, full, full, row, row], out_specs=(row, full, full),
                  dims=("arbitrary",))(q, k, v, o, do)


_SQRT_HALF = 0.7071067811865476
_INV_SQRT_2PI = 0.3989422804014327


def _halo_specs(s, tm, width, rows=8):
    nb = s // rows
    r = tm // rows
    prev = pl.BlockSpec((rows, width), lambda i: (jnp.maximum(i * r - 1, 0), 0))
    nxt = pl.BlockSpec((rows, width), lambda i: (jnp.minimum((i + 1) * r, nb - 1), 0))
    return prev, nxt


def _shifted(x, before_row, after_row, i, nt):
    tm = x.shape[0]
    row = lax.broadcasted_iota(jnp.int32, x.shape, 0)
    first = jnp.where(i == 0, 0.0, 1.0) * before_row
    last = jnp.where(i == nt - 1, 0.0, 1.0) * after_row
    xm1 = jnp.where(row == 0, first, pltpu.roll(x, 1, 0))
    xp1 = jnp.where(row == tm - 1, last, pltpu.roll(x, tm - 1, 0))
    return xm1, xp1


BF16_ROWS = 16


def _ffn_fwd(hb, wg_t, wu_t, cw, cb):
    s = hb.shape[0]
    tm = _pick(s, (256,))
    nt = s // tm
    hr = BF16_ROWS

    def body(h_ref, hp_ref, hn_ref, wg_ref, wu_ref, cw_ref, cb_ref, g_ref, up_ref, cdf_ref, act_ref):
        i = pl.program_id(0)
        hv = h_ref[...]
        g_ext = lax.dot_general(jnp.concatenate([hp_ref[...], hv, hn_ref[...]], axis=0), wg_ref[...], _NT,
                                preferred_element_type=F32)
        gv = g_ext[hr:hr + tm]
        upv = lax.dot_general(hv, wu_ref[...], _NT, preferred_element_type=F32)
        gm1, gp1 = _shifted(gv, g_ext[hr - 1:hr], g_ext[hr + tm:hr + tm + 1], i, nt)
        gc = gm1 * cw_ref[0:1, :] + gv * cw_ref[1:2, :] + gp1 * cw_ref[2:3, :] + cb_ref[...]
        cdf = 0.5 * (1.0 + lax.erf(gc * _SQRT_HALF))
        g_ref[...] = gv
        up_ref[...] = upv
        cdf_ref[...] = cdf
        act_ref[...] = (gc * cdf * upv).astype(BF16)

    hrow = pl.BlockSpec((tm, D_MODEL), lambda i: (i, 0))
    prev, nxt = _halo_specs(s, tm, D_MODEL, hr)
    wfull = pl.BlockSpec((D_FF, D_MODEL), lambda i: (0, 0), pipeline_mode=pl.Buffered(1))
    row = pl.BlockSpec((tm, D_FF), lambda i: (i, 0))
    return _pcall(body, name="ffn_fwd", out_shape=(SDS((s, D_FF), F32),) * 3 + (SDS((s, D_FF), BF16),),
                  grid=(nt,), in_specs=[hrow, prev, nxt, wfull, wfull, pl.BlockSpec((8, D_FF), lambda i: (0, 0)),
                                        pl.BlockSpec((1, D_FF), lambda i: (0, 0))],
                  out_specs=(row, row, row, row), dims=("parallel",))(hb, hb, hb, wg_t, wu_t, cw, cb)


def _ffn_bwd(dffb, w_down, g, up, cdf, cw, cb):
    s = g.shape[0]
    tm = _pick(s, (256,))
    nt = s // tm
    hr = BF16_ROWS

    def body(df_ref, dfp_ref, dfn_ref, wd_ref, g_ref, gp_ref, gn_ref, up_ref, upp_ref, upn_ref, cdf_ref, cw_ref, cb_ref,
             dg_ref, dup_ref, dcw_ref, dcb_ref, a0, a1, a2, a3):
        i = pl.program_id(0)

        @pl.when(i == 0)
        def _():
            for a in (a0, a1, a2, a3):
                a[...] = jnp.zeros_like(a)

        def d_conv_out(gc_, up_, da_, cdf_):
            pdf_ = jnp.exp(-0.5 * gc_ * gc_) * _INV_SQRT_2PI
            return da_ * up_ * (cdf_ + gc_ * pdf_)

        def cdf_of(gc_):
            return 0.5 * (1.0 + lax.erf(gc_ * _SQRT_HALF))

        df_ext = jnp.concatenate([dfp_ref[...], df_ref[...], dfn_ref[...]], axis=0)
        tn = 256
        for c0 in range(0, D_FF, tn):
            cs = slice(c0, c0 + tn)
            da_ext = lax.dot_general(df_ext, wd_ref[cs, :], _NT, preferred_element_type=F32)
            cw0, cw1, cw2, cbv = cw_ref[0:1, cs], cw_ref[1:2, cs], cw_ref[2:3, cs], cb_ref[:, cs]
            gv = g_ref[:, cs]
            g_before, g_after = gp_ref[:, cs], gn_ref[:, cs]
            gm1, gp1 = _shifted(gv, g_before[7:8, :], g_after[0:1, :], i, nt)
            gc = gm1 * cw0 + gv * cw1 + gp1 * cw2 + cbv
            da = da_ext[hr:hr + tm]
            cdf = cdf_ref[:, cs]
            dgc = d_conv_out(gc, up_ref[:, cs], da, cdf)
            dup_ref[:, cs] = (da * (gc * cdf)).astype(BF16)
            gc_b = g_before[6:7, :] * cw0 + g_before[7:8, :] * cw1 + gv[0:1, :] * cw2 + cbv
            gc_a = gv[tm - 1:tm, :] * cw0 + g_after[0:1, :] * cw1 + g_after[1:2, :] * cw2 + cbv
            dgc_b = jnp.where(i == 0, 0.0, 1.0) * d_conv_out(gc_b, upp_ref[7:8, cs], da_ext[hr - 1:hr], cdf_of(gc_b))
            dgc_a = jnp.where(i == nt - 1, 0.0, 1.0) * d_conv_out(gc_a, upn_ref[0:1, cs], da_ext[hr + tm:hr + tm + 1],
                                                                  cdf_of(gc_a))
            row = lax.broadcasted_iota(jnp.int32, dgc.shape, 0)
            dgc_m1 = jnp.where(row == 0, dgc_b, pltpu.roll(dgc, 1, 0))
            dgc_p1 = jnp.where(row == tm - 1, dgc_a, pltpu.roll(dgc, tm - 1, 0))
            dg_ref[:, cs] = (dgc_p1 * cw0 + dgc * cw1 + dgc_m1 * cw2).astype(BF16)

            def fold(t):
                return jnp.sum(t.reshape(tm // 8, 8, tn), axis=0)

            a0[:, cs] += fold(dgc * gm1)
            a1[:, cs] += fold(dgc * gv)
            a2[:, cs] += fold(dgc * gp1)
            a3[:, cs] += fold(dgc)

        @pl.when(i == nt - 1)
        def _():
            dcw_ref[...] = jnp.concatenate(
                [jnp.sum(a[...], axis=0, keepdims=True) for a in (a0, a1, a2)] + [jnp.zeros((5, D_FF), F32)], axis=0)
            dcb_ref[...] = jnp.sum(a3[...], axis=0, keepdims=True)

    row = pl.BlockSpec((tm, D_FF), lambda i: (i, 0))
    prev, nxt = _halo_specs(s, tm, D_FF)
    cw_spec = pl.BlockSpec((8, D_FF), lambda i: (0, 0))
    cb_spec = pl.BlockSpec((1, D_FF), lambda i: (0, 0))
    dprev, dnxt = _halo_specs(s, tm, D_MODEL, hr)
    return _pcall(body, name="ffn_bwd",
                  out_shape=(SDS((s, D_FF), BF16), SDS((s, D_FF), BF16), SDS((8, D_FF), F32), SDS((1, D_FF), F32)),
                  grid=(nt,),
                  in_specs=[pl.BlockSpec((tm, D_MODEL), lambda i: (i, 0)), dprev, dnxt,
                            pl.BlockSpec((D_FF, D_MODEL), lambda i: (0, 0), pipeline_mode=pl.Buffered(1))]
                  + [row, prev, nxt] * 2 + [row, cw_spec, cb_spec],
                  out_specs=(row, row, cw_spec, cb_spec), scratch_shapes=[pltpu.VMEM((8, D_FF), F32)] * 4,
                  dims=("arbitrary",))(dffb, dffb, dffb, w_down, g, g, g, up, up, up, cdf, cw, cb)


def _adamw(w, g, m, v, name):
    rows, cols = w.shape
    tr = _pick(rows, (256, 128, 64, 32, 16, 8))
    c1 = 1.0 - ADAM_B1 ** ADAM_STEP
    c2 = 1.0 - ADAM_B2 ** ADAM_STEP

    def body(w_ref, g_ref, m_ref, v_ref, d_ref, nm_ref, nv_ref):
        gv = g_ref[...]
        nm = ADAM_B1 * m_ref[...] + (1.0 - ADAM_B1) * gv
        nv = ADAM_B2 * v_ref[...] + (1.0 - ADAM_B2) * (gv * gv)
        d_ref[...] = -ADAM_LR * ((nm / c1) / (jnp.sqrt(nv / c2) + ADAM_EPS) + ADAM_WD * w_ref[...])
        nm_ref[...] = nm
        nv_ref[...] = nv

    blk = pl.BlockSpec((tr, cols), lambda i: (i, 0))
    return _pcall(body, name=name, out_shape=(SDS(w.shape, F32),) * 3, grid=(rows // tr,), in_specs=[blk] * 4,
                  out_specs=(blk,) * 3, dims=("parallel",))(w, g, m, v)


def _adamw_many(ws, gs, ms, vs, name):
    n = len(ws)
    c1 = 1.0 - ADAM_B1 ** ADAM_STEP
    c2 = 1.0 - ADAM_B2 ** ADAM_STEP

    def body(*refs):
        outs = refs[4 * n:]
        for k in range(n):
            gv = refs[n + k][...]
            nm = ADAM_B1 * refs[2 * n + k][...] + (1.0 - ADAM_B1) * gv
            nv = ADAM_B2 * refs[3 * n + k][...] + (1.0 - ADAM_B2) * (gv * gv)
            outs[k][...] = -ADAM_LR * ((nm / c1) / (jnp.sqrt(nv / c2) + ADAM_EPS) + ADAM_WD * refs[k][...])
            outs[n + k][...] = nm
            outs[2 * n + k][...] = nv

    shapes = tuple(SDS(w.shape, F32) for w in ws)
    res = _pcall(body, name=name, out_shape=shapes * 3)(*ws, *gs, *ms, *vs)
    return res[:n], res[n:2 * n], res[2 * n:]


def _all_gather_rows(x_shard, *, name, in_vmem, sum_rows=False, after=None):
    m_per, n = x_shard.shape
    extra = [] if after is None else [after]

    def body(x_ref, *rest):
        out_ref, rest = rest[len(extra)], rest[len(extra) + 1:]
        if sum_rows:
            sum_ref, send_sems, recv_sems, local_sem = rest
        else:
            send_sems, recv_sems, local_sem = rest
        x, y, c = lax.axis_index("x"), lax.axis_index("y"), lax.axis_index("c")
        me, sibling = (x, y, c), (x, y, 1 - c)
        chips = [(1 - x, y), (x, 1 - y), (1 - x, 1 - y)]

        def rows(px, py, pc):
            return out_ref.at[pl.ds((4 * px + 2 * py + pc) * m_per, m_per), :]

        def copy(k, block, to, src=None):
            return pltpu.make_async_remote_copy(
                src_ref=rows(*block) if src is None else src, dst_ref=rows(*block), send_sem=send_sems.at[k],
                recv_sem=recv_sems.at[k], device_id=to, device_id_type=pl.DeviceIdType.MESH)

        mine = pltpu.make_async_copy(x_ref, rows(*me), local_sem)
        mine.start()
        first = [copy(0, me, sibling, src=x_ref)]
        first += [copy(1 + j, me, (*chip, c), src=x_ref) for j, chip in enumerate(chips)]
        for cp in first:
            cp.start()
        passed = [copy(4 + j, (*chip, c), sibling) for j, chip in enumerate(chips)]
        for j, chip in enumerate(chips):
            copy(1 + j, (*chip, c), me).wait_recv()
            passed[j].start()
        copy(0, sibling, me).wait_recv()
        for j, chip in enumerate(chips):
            copy(4 + j, (*chip, 1 - c), me).wait_recv()
        for cp in first + passed:
            cp.wait_send()
        mine.wait()
        if sum_rows:
            acc = out_ref[0:m_per, :]
            for dev in range(1, N_DEV):
                acc = acc + out_ref[dev * m_per:(dev + 1) * m_per, :]
            sum_ref[...] = acc

    space = pltpu.VMEM if in_vmem else pl.ANY
    out_shape = [SDS((N_DEV * m_per, n), x_shard.dtype)]
    out_specs = [pl.BlockSpec(memory_space=space)]
    if sum_rows:
        out_shape.append(SDS((m_per, n), x_shard.dtype))
        out_specs.append(pl.BlockSpec(memory_space=pltpu.VMEM))
    res = _PALLAS_CALL(
        body, name=name, out_shape=tuple(out_shape),
        in_specs=[pl.BlockSpec(memory_space=space)] + [pl.BlockSpec(memory_space=pl.ANY)] * len(extra),
        out_specs=tuple(out_specs),
        scratch_shapes=[pltpu.SemaphoreType.DMA((7,)), pltpu.SemaphoreType.DMA((7,)), pltpu.SemaphoreType.DMA],
        compiler_params=pltpu.CompilerParams(vmem_limit_bytes=VMEM_LIMIT_BYTES),
    )(x_shard, *extra)
    return res if sum_rows else res[0]


_HBM = pl.BlockSpec(memory_space=pltpu.HBM)
_SEM = pl.BlockSpec(memory_space=pltpu.SEMAPHORE)
_SPLIT_PARAMS = dict(has_side_effects=pltpu.SideEffectType.DATAFLOW_SIDE_EFFECTING)


def _split_copies(src_ref, land_ref, send_sems, recv_sems, gather):
    x, y, c = lax.axis_index("x"), lax.axis_index("y"), lax.axis_index("c")
    copies = []
    for k in range(1, N_DEV):
        px = 1 - x if k & 4 else x
        py = 1 - y if k & 2 else y
        pc = 1 - c if k & 1 else c
        if gather:
            rows = src_ref.shape[0]
            src, dst = src_ref, land_ref.at[pl.ds((4 * x + 2 * y + c) * rows, rows), :]
        else:
            src, dst = src_ref.at[4 * px + 2 * py + pc], land_ref.at[k - 1]
        copies.append(pltpu.make_async_remote_copy(
            src_ref=src, dst_ref=dst, send_sem=send_sems.at[k - 1], recv_sem=recv_sems.at[k - 1],
            device_id=(px, py, pc), device_id_type=pl.DeviceIdType.MESH))
    return copies


def _exchange_start(src, land_shape, *, gather, name):
    def body(src_ref, land_ref, send_sems, recv_sems, src_thru, land_thru, token):
        for cp in _split_copies(src_ref, land_ref, send_sems, recv_sems, gather):
            cp.start()
        token[...] = jnp.zeros_like(token)

    land = pltpu.with_memory_space_constraint(lax.empty(land_shape, src.dtype), pltpu.HBM)
    return _PALLAS_CALL(
        body, name=name,
        out_shape=(pltpu.SemaphoreType.DMA((N_DEV - 1,)), pltpu.SemaphoreType.DMA((N_DEV - 1,)),
                   pltpu.HBM(src.shape, src.dtype), pltpu.HBM(land_shape, src.dtype), SDS((8, 128), F32)),
        in_specs=(_HBM, _HBM), out_specs=(_SEM, _SEM, _HBM, _HBM, pl.BlockSpec(memory_space=pltpu.VMEM)),
        input_output_aliases={0: 2, 1: 3}, compiler_params=pltpu.CompilerParams(**_SPLIT_PARAMS),
    )(pltpu.with_memory_space_constraint(src, pltpu.HBM), land)


def _exchange_wait(started, after, *, gather, name):
    send_sems, recv_sems, src_thru, land_thru, _ = started

    def body(src_ref, land_ref, send_sems, recv_sems, after_ref, src_out, land_out):
        copies = _split_copies(src_ref, land_ref, send_sems, recv_sems, gather)
        for cp in copies:
            cp.wait_send()
        for cp in copies:
            cp.wait_recv()

    return _PALLAS_CALL(
        body, name=name,
        out_shape=(pltpu.HBM(src_thru.shape, src_thru.dtype), pltpu.HBM(land_thru.shape, land_thru.dtype)),
        in_specs=(_HBM, _HBM, _SEM, _SEM, pl.BlockSpec(memory_space=pl.ANY)), out_specs=(_HBM, _HBM),
        input_output_aliases={0: 0, 1: 1}, compiler_params=pltpu.CompilerParams(**_SPLIT_PARAMS),
    )(src_thru, land_thru, send_sems, recv_sems, after)


def _sum_parts(own, land, name):
    r, n = own.shape
    tr = _pick(r, (264, 320, 336, 128, 64, 32, 16, 8))

    def body(own_ref, x_ref, o_ref):
        acc = own_ref[...]
        for k in range(N_DEV - 1):
            acc = acc + x_ref[k].astype(F32)
        o_ref[...] = acc

    return _pcall(body, name=name, out_shape=SDS((r, n), F32), grid=(r // tr,),
                  in_specs=[pl.BlockSpec((tr, n), lambda i: (i, 0)), pl.BlockSpec((N_DEV - 1, tr, n), lambda i: (0, i, 0))],
                  out_specs=pl.BlockSpec((tr, n), lambda i: (i, 0)), dims=("parallel",))(own, land)


def _pad_rows(a, rows):
    return jnp.pad(a, ((0, rows - a.shape[0]), (0, 0)))


def kernel(x, mem, positions, ln_in_g, ln_in_b, w_in, attn_sink, g_win, g_dil, w_mix_out, ln1_g, ln1_b, mem_ln_g, mem_ln_b, w_xq, w_xk, w_xv, w_xo, ln2_g, ln2_b, w_gate, w_up, conv_w, conv_b, w_down, ln3_g, ln3_b, loss_target, m_ln_in_g, m_ln_in_b, m_w_in, m_attn_sink, m_g_win, m_g_dil, m_w_mix_out, m_ln1_g, m_ln1_b, m_mem_ln_g, m_mem_ln_b, m_w_xq, m_w_xk, m_w_xv, m_w_xo, m_ln2_g, m_ln2_b, m_w_gate, m_w_up, m_conv_w, m_conv_b, m_w_down, m_ln3_g, m_ln3_b, v_ln_in_g, v_ln_in_b, v_w_in, v_attn_sink, v_g_win, v_g_dil, v_w_mix_out, v_ln1_g, v_ln1_b, v_mem_ln_g, v_mem_ln_b, v_w_xq, v_w_xk, v_w_xv, v_w_xo, v_ln2_g, v_ln2_b, v_w_gate, v_w_up, v_conv_w, v_conv_b, v_w_down, v_ln3_g, v_ln3_b):
    weights = dict(ln_in_g=ln_in_g, ln_in_b=ln_in_b, w_in=w_in, attn_sink=attn_sink, g_win=g_win, g_dil=g_dil, w_mix_out=w_mix_out, ln1_g=ln1_g, ln1_b=ln1_b, mem_ln_g=mem_ln_g, mem_ln_b=mem_ln_b, w_xq=w_xq, w_xk=w_xk, w_xv=w_xv, w_xo=w_xo, ln2_g=ln2_g, ln2_b=ln2_b, w_gate=w_gate, w_up=w_up, conv_w=conv_w, conv_b=conv_b, w_down=w_down, ln3_g=ln3_g, ln3_b=ln3_b)
    mom_m = dict(ln_in_g=m_ln_in_g, ln_in_b=m_ln_in_b, w_in=m_w_in, attn_sink=m_attn_sink, g_win=m_g_win, g_dil=m_g_dil, w_mix_out=m_w_mix_out, ln1_g=m_ln1_g, ln1_b=m_ln1_b, mem_ln_g=m_mem_ln_g, mem_ln_b=m_mem_ln_b, w_xq=m_w_xq, w_xk=m_w_xk, w_xv=m_w_xv, w_xo=m_w_xo, ln2_g=m_ln2_g, ln2_b=m_ln2_b, w_gate=m_w_gate, w_up=m_w_up, conv_w=m_conv_w, conv_b=m_conv_b, w_down=m_w_down, ln3_g=m_ln3_g, ln3_b=m_ln3_b)
    mom_v = dict(ln_in_g=v_ln_in_g, ln_in_b=v_ln_in_b, w_in=v_w_in, attn_sink=v_attn_sink, g_win=v_g_win, g_dil=v_g_dil, w_mix_out=v_w_mix_out, ln1_g=v_ln1_g, ln1_b=v_ln1_b, mem_ln_g=v_mem_ln_g, mem_ln_b=v_mem_ln_b, w_xq=v_w_xq, w_xk=v_w_xk, w_xv=v_w_xv, w_xo=v_w_xo, ln2_g=v_ln2_g, ln2_b=v_ln2_b, w_gate=v_w_gate, w_up=v_w_up, conv_w=v_conv_w, conv_b=v_conv_b, w_down=v_w_down, ln3_g=v_ln3_g, ln3_b=v_ln3_b)
    order = list(weights)
    s = x.shape[1]
    xs = x[0]
    mems = mem[0]
    target = loss_target[0]
    row = lambda a: a.reshape(1, -1)

    shard_rows = dict(w_in=w_in[0].T, w_gate=w_gate[0].T, w_up=w_up[0].T, w_mix_out=w_mix_out[0], w_xq=w_xq[0],
                      w_xk=w_xk[0], w_xv=w_xv[0], w_xo=w_xo[0], w_down=w_down[0])
    me_lin = 4 * lax.axis_index("x") + 2 * lax.axis_index("y") + lax.axis_index("c")
    w_in_full = _all_gather_rows(shard_rows["w_in"].astype(BF16), name="w_in_all_gather", in_vmem=False)
    w_in_t = jnp.concatenate([w_in_full[768:], w_in_full[:768]], axis=0)
    late_rows = PACK_ROWS[1:]
    late_r = sum(r for _, r in late_rows)
    packed = jnp.concatenate([shard_rows[n].astype(BF16) for n, _ in late_rows], axis=0)
    w_started = _exchange_start(packed, (N_DEV * late_r, D_MODEL), gather=True, name="weight_gather_start")
    cw_pad = jnp.pad(conv_w[0], ((0, 5), (0, 32)))
    cw_all = _all_gather_rows(cw_pad, name="conv_w_all_gather", in_vmem=True).reshape(N_DEV, 8, 384)
    cw_full = jnp.transpose(cw_all[:, :3, :352], (1, 0, 2)).reshape(3, D_FF)
    cw8 = _pad_rows(cw_full, 8)

    tabs = _rope_tables(positions.astype(F32).reshape(s, 1) + w_started[4][0, 0])
    h0, h0b = _ln_fwd(xs, None, row(ln_in_g), row(ln_in_b), 1.0, "ln_in_fwd")
    zw, *zg = _proj_rope(h0b, w_in_t, tabs[0])
    oa, lse_a = _banded_fwd(zw, attn_sink, name="win_attn_fwd", **_WIN_CFG)
    og_views, lg_views = [], []
    for gi in range(3):
        o_g, l_g = _banded_fwd(zg[gi], None, name=f"dil_attn_fwd{gi}", **_dil_cfg(gi))
        og_views.append(o_g)
        lg_views.append(l_g)
    mixed, ob_views, lb_views = _mix_norm_fwd(oa, og_views, lg_views, g_win, g_dil)
    packed_thru, land = _exchange_wait(w_started, mixed, gather=True, name="weight_gather_wait")
    gathered = lax.dynamic_update_slice(land, packed_thru, (me_lin * late_r, 0)).reshape(N_DEV, late_r, D_MODEL)
    full = {}
    off = 0
    for n, r in late_rows:
        full[n] = gathered[:, off:off + r, :].reshape(N_DEV * r, D_MODEL)
        off += r
    mix, h1, h1b = _mm_ln_fwd(mixed, full["w_mix_out"], h0, ln1_g, ln1_b, ALPHA, "mm_mix_out_ln1")
    _, mem_nb = _ln_fwd(mems, None, mem_ln_g, mem_ln_b, 1.0, "mem_ln_fwd")
    kx = _mm(mem_nb, full["w_xk"], trans_b=False, out_dtype=BF16, name="mm_xk")
    vx = _mm(mem_nb, full["w_xv"], trans_b=False, out_dtype=BF16, name="mm_xv")
    qx = _mm(h1b, full["w_xq"], trans_b=False, out_dtype=BF16, name="mm_xq")
    ox, oxb = _xattn_fwd(qx, kx, vx)
    xa, h2, h2b = _mm_ln_fwd(oxb, full["w_xo"], h1, ln2_g, ln2_b, ALPHA, "mm_xo_ln2")
    gate, up, cdf, act = _ffn_fwd(h2b, full["w_gate"], full["w_up"], cw8, conv_b)
    ff = _mm(act, full["w_down"], trans_b=False, out_dtype=F32, name="mm_down")

    du3, du3b, d_ln3_g, d_ln3_b, loss_local = _ln_bwd(h2, ff, target, ln3_g, ln3_b, ALPHA, "ln3_bwd_loss",
                                                      loss_mode=True)
    dw_down = _mm_tn(act, du3b, name="mm_dw_down")
    dgate, dup, dcw8, d_conv_b = _ffn_bwd(du3b, full["w_down"], gate, up, cdf, cw8, conv_b)
    dw_gate_t = _mm_tn(dgate, h2b, name="mm_dw_gate")
    dw_up_t = _mm_tn(dup, h2b, name="mm_dw_up")
    rows_of = dict(PACK_ROWS)

    own_f32 = {}

    def start_grad_exchange(parts, name, payload=F32):
        gp = jnp.concatenate([g.reshape(N_DEV, rows_of[n], D_MODEL) for n, g in parts], axis=1)
        if payload != F32:
            own_f32[name] = lax.dynamic_index_in_dim(gp, me_lin, axis=0, keepdims=False)
            gp = gp.astype(payload)
        return _exchange_start(gp, (N_DEV - 1,) + gp.shape[1:], gather=False, name=name)

    ffn_parts = (("w_gate", dw_gate_t), ("w_up", dw_up_t), ("w_down", dw_down))
    ffn_started = start_grad_exchange(ffn_parts, "grad_start_ffn")
    du2, du2b, d_ln2_g, d_ln2_b = _mm_ln_bwd(((dgate, full["w_gate"], False), (dup, full["w_up"], False)), du3, ALPHA,
                                             h1, xa, ln2_g + ffn_started[4][0, 0], ALPHA, "mm_dh2_ln2_bwd")
    dox = _mm(du2b, full["w_xo"], trans_b=True, out_dtype=F32, name="mm_d_ox")
    dw_xo = _mm_tn(oxb, du2b, name="mm_dw_xo")
    dqx, dkx, dvx = _xattn_bwd(qx, kx, vx, ox, dox)
    dw_xq = _mm_tn(h1b, dqx, name="mm_dw_xq")
    dw_xk = _mm_tn(mem_nb, dkx, name="mm_dw_xk")
    dw_xv = _mm_tn(mem_nb, dvx, name="mm_dw_xv")
    dmem_n = _mm(dkx, full["w_xk"], trans_b=True, out_dtype=F32, name="mm_dmem", more=((dvx, full["w_xv"], True),))
    _, _, d_mem_ln_g, d_mem_ln_b = _ln_bwd(mems, None, dmem_n, mem_ln_g, mem_ln_b, 1.0, "mem_ln_bwd")
    du1, du1b, d_ln1_g, d_ln1_b = _mm_ln_bwd(((dqx, full["w_xq"], True),), du2, ALPHA, h0, mix, ln1_g, ALPHA,
                                             "mm_dh1_ln1_bwd")
    dmixed = _mm(du1b, full["w_mix_out"], trans_b=True, out_dtype=F32, name="mm_d_mixed")
    dw_mix_out = _mm_tn(mixed, du1b, name="mm_dw_mix_out")
    attn_parts = (("w_mix_out", dw_mix_out), ("w_xq", dw_xq), ("w_xk", dw_xk), ("w_xv", dw_xv), ("w_xo", dw_xo))
    attn_started = start_grad_exchange(attn_parts, "grad_start_attn")
    doa, dob_views, d_g_win, d_g_dil = _mix_norm_bwd(oa, ob_views[0], dmixed, g_win + attn_started[4][0, 0], g_dil)
    dqa, dkva, dsink8 = _banded_bwd(zw, oa, lse_a, doa, tabs[0], attn_sink, name="win_attn_bwd", **_WIN_CFG)
    dq_views, dkv_views = [], []
    for gi in range(3):
        dq_g, dkv_g = _banded_bwd(zg[gi], ob_views[gi], lb_views[gi], dob_views[gi], tabs[gi], None,
                                  name=f"dil_attn_bwd{gi}", **{**_dil_cfg(gi), "rc": ATTN_TQ})
        dq_views.append(dq_g)
        dkv_views.append(dkv_g)
    dz = _dz_assemble(dq_views, dkv_views, dqa, dkva)
    dw_in_tz = _mm_tn(dz, h0b, name="mm_dw_in")
    dw_in_t = jnp.concatenate([dw_in_tz[4608:], dw_in_tz[:4608]], axis=0)
    in_parts = (("w_in", dw_in_t),)
    in_started = start_grad_exchange(in_parts, "grad_start_in", payload=BF16)
    dx, _, d_ln_in_g, d_ln_in_b = _mm_ln_bwd(((dz, w_in_t, False),), du1, ALPHA, xs, None, row(ln_in_g), 1.0,
                                             "mm_dh0_ln_in_bwd", after=in_started[4])

    grads, delta, new_m, new_v = {}, {}, {}, {}
    after = dx
    for parts, started, tag in ((ffn_parts, ffn_started, "ffn"), (attn_parts, attn_started, "attn"),
                                (in_parts, in_started, "in")):
        gp_thru, land = _exchange_wait(started, after, gather=False, name=f"grad_wait_{tag}")
        own = own_f32.get(f"grad_start_{tag}")
        if own is None:
            own = lax.dynamic_index_in_dim(gp_thru, me_lin, axis=0, keepdims=False)
        gsum = _sum_parts(own, land, f"grad_sum_{tag}")
        off = 0
        for n, _ in parts:
            blk = gsum[off:off + rows_of[n]]
            off += rows_of[n]
            grads[n] = (blk.T if n in ("w_in", "w_gate", "w_up") else blk)[None]
            shp = weights[n].shape
            d_, m_, v_ = _adamw(weights[n].reshape(shp[1:]), grads[n].reshape(shp[1:]), mom_m[n].reshape(shp[1:]),
                                mom_v[n].reshape(shp[1:]), f"adamw_{n}")
            delta[n], new_m[n], new_v[n] = d_.reshape(shp), m_.reshape(shp), v_.reshape(shp)
            after = d_

    small = jnp.concatenate([
        d_ln_in_g, d_ln_in_b, d_ln1_g, d_ln1_b, d_mem_ln_g, d_mem_ln_b, d_ln2_g, d_ln2_b, d_ln3_g, d_ln3_b,
        jnp.concatenate([d_g_win, d_g_dil], axis=1),
        jnp.pad(d_conv_b, ((0, 0), (0, 3072 - D_FF))).reshape(3, 1024),
        jnp.pad(dsink8[0:1, :], ((0, 0), (0, 1024 - 128))),
        jnp.pad(dcw8[0:3], ((0, 0), (0, 3072 - D_FF))).reshape(9, 1024),
    ], axis=0)
    _, ssum = _all_gather_rows(small, name="small_grad_all_reduce", in_vmem=True, sum_rows=True, after=after)
    names10 = ["ln_in_g", "ln_in_b", "ln1_g", "ln1_b", "mem_ln_g", "mem_ln_b", "ln2_g", "ln2_b", "ln3_g", "ln3_b"]
    small_g = {n: ssum[i:i + 1] for i, n in enumerate(names10)}
    small_g["g_win"] = ssum[10:11, :512]
    small_g["g_dil"] = ssum[10:11, 512:]
    small_g["conv_b"] = ssum[11:14].reshape(1, 3072)[:, :D_FF]
    small_g["attn_sink"] = ssum[14:15, :8]
    small_g["conv_w"] = lax.dynamic_slice_in_dim(ssum[15:24].reshape(3, 3072)[:, :D_FF], me_lin * 352, 352, axis=1)

    small_names = [n for n in order if n not in rows_of]
    two_d = lambda a: a.reshape(-1, a.shape[-1])
    d_s, m_s, v_s = _adamw_many([two_d(weights[n]) for n in small_names], [small_g[n] for n in small_names],
                                [two_d(mom_m[n]) for n in small_names], [two_d(mom_v[n]) for n in small_names],
                                "adamw_small")
    for k, n in enumerate(small_names):
        shp = weights[n].shape
        grads[n], delta[n], new_m[n], new_v[n] = (t.reshape(shp) for t in (small_g[n], d_s[k], m_s[k], v_s[k]))

    loss = lax.psum(loss_local[0, 0], MESH_AXES)
    return (loss, dx[None], *[grads[n] for n in order], *[delta[n] for n in order], *[new_m[n] for n in order],
            *[new_v[n] for n in order])
```

```python
import functools
import math

import jax
import jax.numpy as jnp
from jax import lax
from jax.experimental import pallas as pl
from jax.experimental.pallas import tpu as pltpu

F32 = jnp.float32
BF16 = jnp.bfloat16
SDS = jax.ShapeDtypeStruct
_PALLAS_CALL = pl.pallas_call

D_MODEL = 1024
HEAD_DIM = 64
WIN_HALF = 128
DIL_PAIRS = ((128, 1), (512, 4), (2048, 16))
DIL_SIDE = 64
ROT_DIM = 16
ROPE_THETA = 500000.0
MEM_LEN = 256
X_HEADS = 4
X_HEAD_DIM = 256
D_FF = 2816
IN_WIDTH = 5376
Z_QB, Z_KB, Z_VB, Z_QA, Z_KA, Z_VA = 0, 1536, 3072, 4608, 5120, 5248
W_IN_QA, W_IN_KA, W_IN_QB = 0, 512, 768
ALPHA = (2.0) ** 0.25
LN_EPS = 1e-5
NEG_INF = -1e30
ADAM_LR, ADAM_B1, ADAM_B2, ADAM_EPS, ADAM_WD, ADAM_STEP = 0.001, 0.9, 0.999, 1e-08, 0.01, 10
N_DEV = 8
MESH_AXES = ("x", "y", "c")
VMEM_LIMIT_BYTES = 52 * 1024 * 1024
ATTN_TQ = 256
TABW = 384

PACK_ROWS = (("w_in", 672), ("w_gate", 352), ("w_up", 352), ("w_mix_out", 128), ("w_xq", 128), ("w_xk", 128),
             ("w_xv", 128), ("w_xo", 128), ("w_down", 352))
SMALL_ROWS = 24


def _pick(n, cands):
    for c in cands:
        if n % c == 0:
            return c
    return n


def _pcall(body, *, name, out_shape, grid=None, in_specs=None, out_specs=None, scratch_shapes=(), dims=None,
           aliases=None):
    kw = {}
    if grid is not None:
        kw["grid"] = grid
    if in_specs is not None:
        kw["in_specs"] = in_specs
    if out_specs is not None:
        kw["out_specs"] = out_specs
    if aliases:
        kw["input_output_aliases"] = aliases
    return _PALLAS_CALL(
        body, name=name, out_shape=out_shape, scratch_shapes=list(scratch_shapes),
        compiler_params=pltpu.CompilerParams(dimension_semantics=dims, vmem_limit_bytes=VMEM_LIMIT_BYTES), **kw)


MM_VMEM_BUDGET = 40 * 1024 * 1024


def _mm(a, b, *, trans_b, out_dtype, name, addends=(), coefs=(), after=None, more=()):
    pairs = ((a, b, trans_b),) + tuple(more)
    m = a.shape[0]
    n = b.shape[0] if trans_b else b.shape[1]
    n_add = len(addends)
    extra = [] if after is None else [after]
    out_bytes = jnp.dtype(out_dtype).itemsize

    def vmem(tm, tn):
        tot = tm * tn * (out_bytes + 4 * n_add)
        for pa, pb, _ in pairs:
            tot += tm * pa.shape[1] * pa.dtype.itemsize + pa.shape[1] * tn * pb.dtype.itemsize
        return 2 * tot

    tm, tn = next(((cm, cn) for cn in (n, 1408, 1024, 512, 256, 128) if n % cn == 0
                   for cm in (1024, 512, 256, 128) if m % cm == 0 and vmem(cm, cn) <= MM_VMEM_BUDGET))
    n_pairs = len(pairs)

    def body(*refs):
        o_ref = refs[2 * n_pairs + n_add + len(extra)]
        acc = None
        for p, (_, _, tb) in enumerate(pairs):
            dn = _NT if tb else _NN
            part = lax.dot_general(refs[2 * p][...].astype(BF16), refs[2 * p + 1][...].astype(BF16), dn,
                                   preferred_element_type=F32)
            acc = part if acc is None else acc + part
        for r_ref, c in zip(refs[2 * n_pairs:2 * n_pairs + n_add], coefs):
            acc = acc + (r_ref[...] if c == 1.0 else c * r_ref[...])
        o_ref[...] = acc.astype(out_dtype)

    in_specs, args = [], []
    for pa, pb, tb in pairs:
        k = pa.shape[1]
        in_specs.append(pl.BlockSpec((tm, k), lambda j, i: (i, 0)))
        in_specs.append(pl.BlockSpec((tn, k), lambda j, i: (j, 0)) if tb else pl.BlockSpec((k, tn), lambda j, i: (0, j)))
        args += [pa, pb]
    in_specs += [pl.BlockSpec((tm, tn), lambda j, i: (i, j)) for _ in addends]
    in_specs += [pl.BlockSpec((8, 128), lambda j, i: (0, 0)) for _ in extra]
    return _pcall(body, name=name, out_shape=SDS((m, n), out_dtype), grid=(n // tn, m // tm), in_specs=in_specs,
                  out_specs=pl.BlockSpec((tm, tn), lambda j, i: (i, j)),
                  dims=("parallel", "parallel"))(*args, *addends, *extra)


def _mm_tn(a, b, *, name):
    s, m = a.shape
    n = b.shape[1]
    tm = _pick(m, (768, 1408, 1024, 512, 256, 128))
    tk = _pick(s, (1024, 512, 256))
    nk = s // tk

    def body(a_ref, b_ref, o_ref, acc_ref):
        kk = pl.program_id(1)

        @pl.when(kk == 0)
        def _():
            acc_ref[...] = jnp.zeros_like(acc_ref)

        acc_ref[...] += lax.dot_general(a_ref[...].astype(BF16), b_ref[...].astype(BF16), (((0,), (0,)), ((), ())),
                                        preferred_element_type=F32)

        @pl.when(kk == nk - 1)
        def _():
            o_ref[...] = acc_ref[...]

    return _pcall(body, name=name, out_shape=SDS((m, n), F32), grid=(m // tm, nk),
                  in_specs=[pl.BlockSpec((tk, tm), lambda i, kk: (kk, i)), pl.BlockSpec((tk, n), lambda i, kk: (kk, 0))],
                  out_specs=pl.BlockSpec((tm, n), lambda i, kk: (i, 0)), scratch_shapes=[pltpu.VMEM((tm, n), F32)],
                  dims=("parallel", "arbitrary"))(a, b)


def _rope_lane_consts():
    lane = jnp.arange(128)
    j = lane % HEAD_DIM
    inv_freq = ROPE_THETA ** (-jnp.arange(0, ROT_DIM, 2, dtype=F32) / ROT_DIM)
    freq = jnp.where(j < ROT_DIM, inv_freq[j % (ROT_DIM // 2)], 0.0).astype(F32)
    lo = (j < ROT_DIM // 2).astype(F32)
    hi = ((j >= ROT_DIM // 2) & (j < ROT_DIM)).astype(F32)
    return jnp.stack([freq, lo, hi] + [jnp.zeros((128,), F32)] * 5)


def _to_classes(x, scr, d):
    if d == 1:
        return [x]
    scr[...] = x
    return [scr[pl.ds(c, x.shape[0] // d, stride=d), :] for c in range(d)]


def _from_classes(parts, scr):
    d = len(parts)
    if d == 1:
        return parts[0]
    for c, part in enumerate(parts):
        scr[pl.ds(c, part.shape[0], stride=d), :] = part
    return scr[...]


DILATIONS = tuple(d for _, d in DIL_PAIRS)


def _rope_tables(posf):
    s = posf.shape[0]
    tm = _pick(s, (1024, 512))

    def body(p_ref, c_ref, *rest):
        o_refs, scr = rest[:-1], rest[-1]
        ang = p_ref[...] * c_ref[0:1, :]
        lo = c_ref[1:2, :]
        hi = c_ref[2:3, :]
        cs = jnp.cos(ang)
        sn = jnp.sin(ang)
        for q, t in enumerate((jnp.where(lo + hi > 0.0, cs, 1.0), -sn * lo, sn * hi)):
            for o_ref, d in zip(o_refs, DILATIONS):
                for c, part in enumerate(_to_classes(t, scr, d)):
                    o_ref[:, c * TABW + q * 128:c * TABW + (q + 1) * 128] = part

    return _pcall(body, name="rope_tables", out_shape=tuple(SDS((s // d, d * TABW), F32) for d in DILATIONS),
                  grid=(s // tm,),
                  in_specs=[pl.BlockSpec((tm, 1), lambda i: (i, 0)), pl.BlockSpec((8, 128), lambda i: (0, 0))],
                  out_specs=tuple(pl.BlockSpec((tm // d, d * TABW), lambda i: (i, 0)) for d in DILATIONS),
                  scratch_shapes=[pltpu.VMEM((tm, 128), F32)], dims=("parallel",))(posf, _rope_lane_consts())


def _rope_apply(x, tab, sign):
    w = x.shape[1]
    rep = w // 128
    c = jnp.tile(tab[:, 0:128], (1, rep)) if rep > 1 else tab[:, 0:128]
    a = jnp.tile(tab[:, 128:256], (1, rep)) if rep > 1 else tab[:, 128:256]
    b = jnp.tile(tab[:, 256:384], (1, rep)) if rep > 1 else tab[:, 256:384]
    up = pltpu.roll(x, w - 8, 1)
    dn = pltpu.roll(x, 8, 1)
    if sign > 0:
        return x * c + up * a + dn * b
    return x * c - up * a - dn * b


def _proj_rope(h0b, w_t, tab):
    s = h0b.shape[0]
    tm = _pick(s, (512,))
    tn = 256

    def body(a_ref, w_ref, t_ref, zw_ref, z0_ref, z1_ref, z2_ref, scr):
        z_refs = (z0_ref, z1_ref, z2_ref)
        a = a_ref[...]
        tabv = t_ref[...]
        for c0 in range(0, IN_WIDTH, tn):
            w0 = (c0 + W_IN_QB) % IN_WIDTH
            z = lax.dot_general(a, w_ref[w0:w0 + tn, :], _NT, preferred_element_type=F32)
            for g0 in range(c0, c0 + tn, 128):
                zg = z[:, g0 - c0:g0 - c0 + 128]
                if g0 < Z_VB or Z_QA <= g0 < Z_VA:
                    zg = _rope_apply(zg, tabv, 1)
                if g0 >= Z_QA:
                    zw_ref[:, g0 - Z_QA:g0 - Z_QA + 128] = zg.astype(BF16)
                    continue
                kind, within = divmod(g0, 1536)
                grp, off = divmod(within, 512)
                col = kind * 512 + off
                for c, part in enumerate(_to_classes(zg, scr, DILATIONS[grp])):
                    z_refs[grp][:, c * 1536 + col:c * 1536 + col + 128] = part.astype(BF16)

    return _pcall(body, name="proj_rope",
                  out_shape=(SDS((s, 768), BF16),) + tuple(SDS((s // d, d * 1536), BF16) for d in DILATIONS),
                  grid=(s // tm,),
                  in_specs=[pl.BlockSpec((tm, D_MODEL), lambda i: (i, 0)), pl.BlockSpec((IN_WIDTH, D_MODEL), lambda i: (0, 0)),
                            pl.BlockSpec((tm, TABW), lambda i: (i, 0))],
                  out_specs=(pl.BlockSpec((tm, 768), lambda i: (i, 0)),)
                  + tuple(pl.BlockSpec((tm // d, d * 1536), lambda i: (i, 0)) for d in DILATIONS),
                  scratch_shapes=[pltpu.VMEM((tm, 128), F32)], dims=("parallel",))(h0b, w_t, tab)


def _band_specs(sd, blk, tq, width, per_tok, cb):
    r = tq // blk
    nbk = sd // blk
    prev = pl.BlockSpec((blk, width), lambda c, j: (jnp.maximum(j * r - 1, 0), c * per_tok + cb))
    cur = pl.BlockSpec((tq, width), lambda c, j: (j, c * per_tok + cb))
    nxt = pl.BlockSpec((blk, width), lambda c, j: (jnp.minimum((j + 1) * r, nbk - 1), c * per_tok + cb))
    return [prev, cur, nxt]


def _band_bias(q0, rows, blk, sd):
    shape = (rows, rows + 2 * blk)
    qpos = q0 + lax.broadcasted_iota(jnp.int32, shape, 0)
    kpos = q0 - blk + lax.broadcasted_iota(jnp.int32, shape, 1)
    ok = (jnp.abs(qpos - kpos) <= blk) & (kpos >= 0) & (kpos < sd)
    return jnp.where(ok, 0.0, NEG_INF)


_NT = (((1,), (1,)), ((), ()))
_NN = (((1,), (0,)), ((), ()))
_TN = (((0,), (0,)), ((), ()))


def _banded_fwd(zv, sink, *, d, blk, tq, rc, ptw, qw, kw, qcb, kcb, vcb, pairs, name):
    sd = zv.shape[0]
    tq = min(tq, sd)
    rc = min(rc, tq)
    has_sink = sink is not None
    scale = HEAD_DIM ** -0.5

    def body(q_ref, kp, kc, kn, vp, vc, vn, *rest):
        if has_sink:
            sink_ref, o_ref, lse_ref = rest
        else:
            o_ref, lse_ref = rest
        j = pl.program_id(1)
        q = q_ref[...] * scale
        k = jnp.concatenate([kp[...], kc[...], kn[...]], axis=0)
        v = jnp.concatenate([vp[...], vc[...], vn[...]], axis=0)
        biases = {r0: _band_bias(j * tq + r0, rc, blk, sd) for r0 in range(0, tq, rc)}
        low = lax.broadcasted_iota(jnp.int32, (1, 128), 1) < HEAD_DIM
        for qb, kb, vb, swaps, sinks in pairs:
            qp, kp_, vp_ = q[:, qb:qb + 128], k[:, kb:kb + 128], v[:, vb:vb + 128]
            if any(swaps):
                k_sw = jnp.concatenate([kp_[:, HEAD_DIM:], kp_[:, :HEAD_DIM]], axis=1)
                v_sw = jnp.concatenate([vp_[:, HEAD_DIM:], vp_[:, :HEAD_DIM]], axis=1)
            for r0 in range(0, tq, rc):
                outs, lses = [], []
                for half in range(2):
                    qm = jnp.where(low if half == 0 else ~low, qp[r0:r0 + rc], jnp.zeros((rc, 128), BF16))
                    kk, vv = (k_sw, v_sw) if swaps[half] else (kp_, vp_)
                    kk, vv = kk[r0:r0 + rc + 2 * blk], vv[r0:r0 + rc + 2 * blk]
                    sc = lax.dot_general(qm, kk, _NT, preferred_element_type=F32) + biases[r0]
                    m = jnp.max(sc, axis=-1, keepdims=True)
                    if has_sink:
                        m = jnp.maximum(m, sink_ref[0, sinks[half]])
                    p = jnp.exp(sc - m)
                    den = jnp.sum(p, axis=-1, keepdims=True)
                    if has_sink:
                        den = den + jnp.exp(sink_ref[0, sinks[half]] - m)
                    outs.append(lax.dot_general(p.astype(BF16), vv, _NN, preferred_element_type=F32) / den)
                    lses.append(m + jnp.log(den))
                o_ref[r0:r0 + rc, qb:qb + 128] = jnp.where(low, outs[0], outs[1])
                lse_ref[r0:r0 + rc, qb:qb + 128] = jnp.where(low, lses[0], lses[1])

    in_specs = ([pl.BlockSpec((tq, qw), lambda c, j: (j, c * (ptw // qw) + qcb))]
                + _band_specs(sd, blk, tq, kw, ptw // kw, kcb) + _band_specs(sd, blk, tq, kw, ptw // kw, vcb))
    args = [zv] * 7
    if has_sink:
        in_specs.append(pl.BlockSpec(memory_space=pltpu.SMEM))
        args.append(sink)
    o_spec = pl.BlockSpec((tq, qw), lambda c, j: (j, c))
    return _pcall(body, name=name, out_shape=(SDS((sd, d * qw), F32), SDS((sd, d * qw), F32)), grid=(d, sd // tq),
                  in_specs=in_specs, out_specs=(o_spec, o_spec), dims=("parallel", "parallel"))(*args)


def _banded_bwd(zv, ov, lv, dov, tv, sink, *, d, blk, tq, rc, ptw, qw, kw, qcb, kcb, vcb, pairs, name):
    sd = zv.shape[0]
    tq = min(tq, sd)
    nt = sd // tq
    r = tq // blk
    nbk = sd // blk
    has_sink = sink is not None
    scale = HEAD_DIM ** -0.5
    rc = min(rc, tq)
    kvw = 128 * len({kb for _, kb, _, _, _ in pairs})

    def add_rows(x, y, last):
        if tq == blk:
            return x + y
        if last:
            return jnp.concatenate([x[:tq - blk], x[tq - blk:] + y], axis=0)
        return jnp.concatenate([x[:blk] + y, x[blk:]], axis=0)

    def body(q_ref, kp, kc, kn, vp, vc, vn, o_ref, l_ref, do_ref, t_ref, tlag_ref, *rest):
        if has_sink:
            sink_ref, dq_ref, dkv_ref, dsink_ref, acck, accv, nxtk, nxtv = rest
        else:
            dq_ref, dkv_ref, acck, accv, nxtk, nxtv = rest
        j = pl.program_id(1)

        @pl.when(j == 0)
        def _():
            nxtk[...] = jnp.zeros_like(nxtk)
            nxtv[...] = jnp.zeros_like(nxtv)

        if has_sink:
            @pl.when((pl.program_id(0) == 0) & (j == 0))
            def _():
                dsink_ref[...] = jnp.zeros_like(dsink_ref)

        def emit(dk_rows, dv_rows):
            dkv_ref[...] = jnp.concatenate([_rope_apply(dk_rows, tlag_ref[...], -1), dv_rows], axis=1).astype(BF16)

        @pl.when(j < nt)
        def _():
            q = q_ref[...] * scale
            k3 = jnp.concatenate([kp[...], kc[...], kn[...]], axis=0)
            v3 = jnp.concatenate([vp[...], vc[...], vn[...]], axis=0)
            o_t, l_t, do_t = o_ref[...], l_ref[...], do_ref[...]
            biases = {r0: _band_bias(j * tq + r0, rc, blk, sd) for r0 in range(0, tq, rc)}
            lane = lax.broadcasted_iota(jnp.int32, (1, 128), 1)
            low = lane < HEAD_DIM
            wide = tq + 2 * blk
            cw = rc + 2 * blk

            def place(x, r0):
                parts = ([jnp.zeros((r0, 128), F32)] if r0 else []) + [x]
                if wide - r0 - cw:
                    parts.append(jnp.zeros((wide - r0 - cw, 128), F32))
                return jnp.concatenate(parts, axis=0) if len(parts) > 1 else x

            dqs = []
            wks, wvs = {}, {}
            dsink_row = jnp.zeros((1, 128), F32)
            for qb, kb, vb, swaps, sinks in pairs:
                qp, kp_, vp_ = q[:, qb:qb + 128], k3[:, kb:kb + 128], v3[:, vb:vb + 128]
                if any(swaps):
                    k_sw = jnp.concatenate([kp_[:, HEAD_DIM:], kp_[:, :HEAD_DIM]], axis=1)
                    v_sw = jnp.concatenate([vp_[:, HEAD_DIM:], vp_[:, :HEAD_DIM]], axis=1)
                dop, lp = do_t[:, qb:qb + 128], l_t[:, qb:qb + 128]
                prod = dop * o_t[:, qb:qb + 128]
                dq_rows = []
                for r0 in range(0, tq, rc):
                    rows = slice(r0, r0 + rc)
                    dq_half = []
                    for half in range(2):
                        mine = low if half == 0 else ~low
                        qm = jnp.where(mine, qp[rows], jnp.zeros((rc, 128), BF16))
                        dob = jnp.where(mine, dop[rows], 0.0).astype(BF16)
                        delta = jnp.sum(jnp.where(mine, prod[rows], 0.0), axis=-1, keepdims=True)
                        lse = lp[rows, half * HEAD_DIM:half * HEAD_DIM + 1]
                        kk, vv = (k_sw, v_sw) if swaps[half] else (kp_, vp_)
                        kk, vv = kk[r0:r0 + cw], vv[r0:r0 + cw]
                        sc = lax.dot_general(qm, kk, _NT, preferred_element_type=F32) + biases[r0]
                        p = jnp.exp(sc - lse)
                        dp = lax.dot_general(dob, vv, _NT, preferred_element_type=F32)
                        dsb = (p * (dp - delta)).astype(BF16)
                        dq_half.append(lax.dot_general(dsb, kk, _NN, preferred_element_type=F32))
                        dk = lax.dot_general(dsb, qm, _TN, preferred_element_type=F32)
                        dv = lax.dot_general(p.astype(BF16), dob, _TN, preferred_element_type=F32)
                        if swaps[half]:
                            dk, dv = pltpu.roll(dk, HEAD_DIM, 1), pltpu.roll(dv, HEAD_DIM, 1)
                        wks[kb] = place(dk, r0) if kb not in wks else wks[kb] + place(dk, r0)
                        wvs[vb] = place(dv, r0) if vb not in wvs else wvs[vb] + place(dv, r0)
                        if has_sink:
                            psink = jnp.exp(sink_ref[0, sinks[half]] - lse)
                            dsink_row = dsink_row + jnp.where(lane == sinks[half], -jnp.sum(psink * delta), 0.0)
                    dq_rows.append(jnp.where(low, dq_half[0], dq_half[1]) * scale)
                dqs.append(jnp.concatenate(dq_rows, axis=0) if len(dq_rows) > 1 else dq_rows[0])
            dq_ref[...] = _rope_apply(jnp.concatenate(dqs, axis=1), t_ref[...], -1).astype(BF16)
            wk = jnp.concatenate([wks[b] for b in sorted(wks)], axis=1) if len(wks) > 1 else wks[min(wks)]
            wv = jnp.concatenate([wvs[b] for b in sorted(wvs)], axis=1) if len(wvs) > 1 else wvs[min(wvs)]
            if has_sink:
                dsink_ref[0:1, :] += dsink_row

            @pl.when(j > 0)
            def _():
                emit(add_rows(acck[...], wk[:blk], True), add_rows(accv[...], wv[:blk], True))

            acck[...] = add_rows(wk[blk:blk + tq], nxtk[...], False)
            accv[...] = add_rows(wv[blk:blk + tq], nxtv[...], False)
            nxtk[...] = wk[blk + tq:]
            nxtv[...] = wv[blk + tq:]

        @pl.when(j == nt)
        def _():
            emit(acck[...], accv[...])

    def tile(width, per_tok, cb):
        return pl.BlockSpec((tq, width), lambda c, j: (jnp.minimum(j, nt - 1), c * per_tok + cb))

    def halos(width, per_tok, cb):
        before = pl.BlockSpec((blk, width), lambda c, j: (jnp.maximum(jnp.minimum(j, nt - 1) * r - 1, 0), c * per_tok + cb))
        after = pl.BlockSpec((blk, width),
                             lambda c, j: (jnp.minimum((jnp.minimum(j, nt - 1) + 1) * r, nbk - 1), c * per_tok + cb))
        return [before, tile(width, per_tok, cb), after]

    def lagged(width):
        return pl.BlockSpec((tq, width), lambda c, j: (jnp.maximum(j - 1, 0), c))

    in_specs = ([tile(qw, ptw // qw, qcb)] + halos(kw, ptw // kw, kcb) + halos(kw, ptw // kw, vcb)
                + [tile(qw, 1, 0)] * 3 + [tile(TABW, 1, 0), lagged(TABW)])
    args = [zv] * 7 + [ov, lv, dov, tv, tv]
    out_shape = [SDS((sd, d * qw), BF16), SDS((sd, d * 2 * kvw), BF16)]
    out_specs = [tile(qw, 1, 0), lagged(2 * kvw)]
    if has_sink:
        in_specs.append(pl.BlockSpec(memory_space=pltpu.SMEM))
        args.append(sink)
        out_shape.append(SDS((8, 128), F32))
        out_specs.append(pl.BlockSpec((8, 128), lambda c, j: (0, 0)))
    scratch = [pltpu.VMEM((tq, kvw), F32), pltpu.VMEM((tq, kvw), F32), pltpu.VMEM((blk, kvw), F32),
               pltpu.VMEM((blk, kvw), F32)]
    return _pcall(body, name=name, out_shape=tuple(out_shape), grid=(d, nt + 1), in_specs=in_specs,
                  out_specs=tuple(out_specs), scratch_shapes=scratch, dims=("arbitrary", "arbitrary"))(*args)


_WIN_PAIRS = tuple((128 * p, 0, 128, (False, True) if p < 2 else (True, False), (2 * p, 2 * p + 1)) for p in range(4))
_WIN_CFG = dict(d=1, blk=WIN_HALF, tq=ATTN_TQ, rc=256, ptw=768, qw=512, kw=256, qcb=0, kcb=2, vcb=2, pairs=_WIN_PAIRS)
_DIL_PAIRS = tuple((128 * p, 128 * p, 128 * p, (False, False), (2 * p, 2 * p + 1)) for p in range(4))


def _dil_cfg(gi):
    return dict(d=DILATIONS[gi], blk=DIL_SIDE, tq=ATTN_TQ, rc=128, ptw=1536, qw=512, kw=512, qcb=0, kcb=1, vcb=2,
                pairs=_DIL_PAIRS)


def _view_specs(tm, width):
    return tuple(pl.BlockSpec((tm // d, d * width), lambda i: (i, 0)) for d in DILATIONS)


def _mix_norm_fwd(oa, og_views, lg_views, g_win, g_dil):
    s = oa.shape[0]
    tm = _pick(s, (512,))

    def body(oa_ref, o0, o1, o2, l0, l1, l2, gw_ref, gd_ref, mixed_ref, ob0, ob1, ob2, lb0, lb1, lb2, scr, ob_s):
        o_refs, l_refs, ob_refs, lb_refs = (o0, o1, o2), (l0, l1, l2), (ob0, ob1, ob2), (lb0, lb1, lb2)
        ssq = jnp.zeros((tm, 1), F32)
        for q in range(4):
            os_, ls_ = [], []
            for g, d in enumerate(DILATIONS):
                cols = [slice(c * 512 + q * 128, c * 512 + (q + 1) * 128) for c in range(d)]
                os_.append(_from_classes([o_refs[g][:, cs] for cs in cols], scr))
                ls_.append(_from_classes([l_refs[g][:, cs] for cs in cols], scr))
            mx = jnp.maximum(jnp.maximum(ls_[0], ls_[1]), ls_[2])
            es = [jnp.exp(l - mx) for l in ls_]
            den = es[0] + es[1] + es[2]
            ob = (es[0] / den) * os_[0] + (es[1] / den) * os_[1] + (es[2] / den) * os_[2]
            lb = mx + jnp.log(den)
            ob_s[:, q * 128:(q + 1) * 128] = ob
            ssq = ssq + jnp.sum(ob * ob, axis=-1, keepdims=True)
            for g, d in enumerate(DILATIONS):
                for val, refs in ((ob, ob_refs), (lb, lb_refs)):
                    for c, part in enumerate(_to_classes(val, scr, d)):
                        refs[g][:, c * 512 + q * 128:c * 512 + (q + 1) * 128] = part
        a = oa_ref[...]
        ra = lax.rsqrt(jnp.mean(a * a, axis=-1, keepdims=True) + LN_EPS)
        rb = lax.rsqrt(ssq * (1.0 / 512) + LN_EPS)
        mixed_ref[...] = jnp.concatenate([a * ra * gw_ref[...], ob_s[...] * rb * gd_ref[...]], axis=1).astype(BF16)

    row = pl.BlockSpec((tm, 512), lambda i: (i, 0))
    vec = pl.BlockSpec((1, 512), lambda i: (0, 0))
    views = _view_specs(tm, 512)
    view_shapes = tuple(SDS((s // d, d * 512), F32) for d in DILATIONS)
    res = _pcall(body, name="mix_norm_fwd", out_shape=(SDS((s, 1024), BF16),) + view_shapes * 2, grid=(s // tm,),
                 in_specs=[row, *views, *views, vec, vec],
                 out_specs=(pl.BlockSpec((tm, 1024), lambda i: (i, 0)),) + views * 2,
                 scratch_shapes=[pltpu.VMEM((tm, 128), F32), pltpu.VMEM((tm, 512), F32)],
                 dims=("parallel",))(oa, *og_views, *lg_views, g_win, g_dil)
    return res[0], res[1:4], res[4:7]


def _mix_norm_bwd(oa, ob, dmixed, g_win, g_dil):
    s = oa.shape[0]
    tm = _pick(s, (512,))
    nt = s // tm

    def body(oa_ref, ob_ref, dm_ref, gw_ref, gd_ref, doa_ref, db0, db1, db2, dgw_ref, dgd_ref, acc_w, acc_d, scr):
        i = pl.program_id(0)

        @pl.when(i == 0)
        def _():
            acc_w[...] = jnp.zeros_like(acc_w)
            acc_d[...] = jnp.zeros_like(acc_d)

        dm = dm_ref[...]
        dxs = []
        for x_ref, g_ref, dy, acc in ((oa_ref, gw_ref, dm[:, :512], acc_w), (ob_ref, gd_ref, dm[:, 512:], acc_d)):
            x = x_ref[...]
            r = lax.rsqrt(jnp.mean(x * x, axis=-1, keepdims=True) + LN_EPS)
            dyg = dy * g_ref[...]
            dxs.append(r * dyg - x * (r * r * r) * jnp.mean(dyg * x, axis=-1, keepdims=True))
            acc[...] += jnp.sum((dy * x * r).reshape(tm // 8, 8, 512), axis=0)
        doa_ref[...] = dxs[0]
        for q in range(4):
            dq = dxs[1][:, q * 128:(q + 1) * 128]
            for db_ref, d in zip((db0, db1, db2), DILATIONS):
                for c, part in enumerate(_to_classes(dq, scr, d)):
                    db_ref[:, c * 512 + q * 128:c * 512 + (q + 1) * 128] = part

        @pl.when(i == nt - 1)
        def _():
            dgw_ref[...] = jnp.sum(acc_w[...], axis=0, keepdims=True)
            dgd_ref[...] = jnp.sum(acc_d[...], axis=0, keepdims=True)

    row = pl.BlockSpec((tm, 512), lambda i: (i, 0))
    vec = pl.BlockSpec((1, 512), lambda i: (0, 0))
    views = _view_specs(tm, 512)
    view_shapes = tuple(SDS((s // d, d * 512), F32) for d in DILATIONS)
    res = _pcall(body, name="mix_norm_bwd",
                 out_shape=(SDS((s, 512), F32),) + view_shapes + (SDS((1, 512), F32), SDS((1, 512), F32)),
                 grid=(nt,), in_specs=[row, row, pl.BlockSpec((tm, 1024), lambda i: (i, 0)), vec, vec],
                 out_specs=(row,) + views + (vec, vec),
                 scratch_shapes=[pltpu.VMEM((8, 512), F32), pltpu.VMEM((8, 512), F32), pltpu.VMEM((tm, 128), F32)],
                 dims=("arbitrary",))(oa, ob, dmixed, g_win, g_dil)
    return res[0], res[1:4], res[4], res[5]


def _dz_assemble(dq_views, dkv_views, dqa, dkva):
    s = dqa.shape[0]
    tm = _pick(s, (512,))

    def body(q0, q1, q2, kv0, kv1, kv2, qa_ref, kva_ref, o_ref, scr):
        for g, d in enumerate(DILATIONS):
            for kind, (ref, width, base) in enumerate((((q0, q1, q2)[g], 512, 0), ((kv0, kv1, kv2)[g], 1024, 0),
                                                       ((kv0, kv1, kv2)[g], 1024, 512))):
                for q in range(4):
                    src = base + q * 128
                    dst = W_IN_QB + kind * 1536 + g * 512 + q * 128
                    if d == 1:
                        o_ref[:, dst:dst + 128] = ref[:, src:src + 128]
                    else:
                        parts = [ref[:, c * width + src:c * width + src + 128].astype(F32) for c in range(d)]
                        o_ref[:, dst:dst + 128] = _from_classes(parts, scr).astype(BF16)
        o_ref[:, W_IN_QA:W_IN_QA + 512] = qa_ref[...]
        o_ref[:, W_IN_KA:W_IN_KA + 256] = kva_ref[...]

    return _pcall(body, name="dz_assemble", out_shape=SDS((s, IN_WIDTH), BF16), grid=(s // tm,),
                  in_specs=[*_view_specs(tm, 512), *_view_specs(tm, 1024), pl.BlockSpec((tm, 512), lambda i: (i, 0)),
                            pl.BlockSpec((tm, 256), lambda i: (i, 0))],
                  out_specs=pl.BlockSpec((tm, IN_WIDTH), lambda i: (i, 0)),
                  scratch_shapes=[pltpu.VMEM((tm, 128), F32)], dims=("parallel",))(*dq_views, *dkv_views, dqa, dkva)


def _ln_fwd(a, r, g, b, ca, name):
    s = a.shape[0]
    tm = _pick(s, (512, 256))
    has_r = r is not None

    def body(*refs):
        a_ref = refs[0]
        r_ref = refs[1] if has_r else None
        g_ref, b_ref, o_ref, ob_ref = refs[1 + has_r:]
        u = a_ref[...] if ca == 1.0 else ca * a_ref[...]
        if has_r:
            u = u + r_ref[...]
        mu = jnp.mean(u, axis=-1, keepdims=True)
        xc = u - mu
        var = jnp.mean(xc * xc, axis=-1, keepdims=True)
        y = xc * lax.rsqrt(var + LN_EPS) * g_ref[...] + b_ref[...]
        o_ref[...] = y
        ob_ref[...] = y.astype(BF16)

    row = pl.BlockSpec((tm, D_MODEL), lambda i: (i, 0))
    vec = pl.BlockSpec((1, D_MODEL), lambda i: (0, 0))
    args = [a] + ([r] if has_r else []) + [g, b]
    return _pcall(body, name=name, out_shape=(SDS((s, D_MODEL), F32), SDS((s, D_MODEL), BF16)), grid=(s // tm,),
                  in_specs=[row] * (1 + has_r) + [vec, vec], out_specs=(row, row), dims=("parallel",))(*args)


def _mm_ln_fwd(a, w, resid, g, b, ca, name):
    s, k = a.shape
    tm = _pick(s, (512, 256))

    def body(a_ref, w_ref, res_ref, g_ref, b_ref, r_ref, o_ref, ob_ref):
        rv = lax.dot_general(a_ref[...], w_ref[...], _NN, preferred_element_type=F32)
        r_ref[...] = rv
        u = ca * res_ref[...] + rv
        mu = jnp.mean(u, axis=-1, keepdims=True)
        xc = u - mu
        var = jnp.mean(xc * xc, axis=-1, keepdims=True)
        y = xc * lax.rsqrt(var + LN_EPS) * g_ref[...] + b_ref[...]
        o_ref[...] = y
        ob_ref[...] = y.astype(BF16)

    row = pl.BlockSpec((tm, D_MODEL), lambda i: (i, 0))
    vec = pl.BlockSpec((1, D_MODEL), lambda i: (0, 0))
    return _pcall(body, name=name, out_shape=(SDS((s, D_MODEL), F32), SDS((s, D_MODEL), F32), SDS((s, D_MODEL), BF16)),
                  grid=(s // tm,),
                  in_specs=[pl.BlockSpec((tm, k), lambda i: (i, 0)), pl.BlockSpec((k, D_MODEL), lambda i: (0, 0)), row, vec, vec],
                  out_specs=(row, row, row), dims=("parallel",))(a, w, resid, g, b)


def _mm_ln_loss(x, w, a, target, g, b, ca, name):
    s, k = x.shape
    tm = _pick(s, (256,))
    nt = s // tm

    def body(x_ref, w_ref, a_ref, t_ref, g_ref, b_ref, du_ref, dub_ref, dg_ref, db_ref, loss_ref, acc_g, acc_b, acc_l):
        i = pl.program_id(0)

        @pl.when(i == 0)
        def _():
            acc_g[...] = jnp.zeros_like(acc_g)
            acc_b[...] = jnp.zeros_like(acc_b)
            acc_l[...] = jnp.zeros_like(acc_l)

        u = ca * a_ref[...] + lax.dot_general(x_ref[...], w_ref[...], _NN, preferred_element_type=F32)
        mu = jnp.mean(u, axis=-1, keepdims=True)
        xc = u - mu
        var = jnp.mean(xc * xc, axis=-1, keepdims=True)
        rstd = lax.rsqrt(var + LN_EPS)
        xhat = xc * rstd
        gv = g_ref[...]
        err = (xhat * gv + b_ref[...]) - t_ref[...]
        acc_l[...] += jnp.sum((err * err).reshape(tm // 8, 8, D_MODEL), axis=0)
        dyv = err * (1.0 / D_MODEL)
        dxh = dyv * gv
        du = rstd * (dxh - jnp.mean(dxh, axis=-1, keepdims=True) - xhat * jnp.mean(dxh * xhat, axis=-1, keepdims=True))
        du_ref[...] = du
        dub_ref[...] = du.astype(BF16)
        acc_g[...] += jnp.sum((dyv * xhat).reshape(tm // 8, 8, D_MODEL), axis=0)
        acc_b[...] += jnp.sum(dyv.reshape(tm // 8, 8, D_MODEL), axis=0)

        @pl.when(i == nt - 1)
        def _():
            dg_ref[...] = jnp.sum(acc_g[...], axis=0, keepdims=True)
            db_ref[...] = jnp.sum(acc_b[...], axis=0, keepdims=True)
            tot = jnp.sum(jnp.sum(acc_l[...], axis=0, keepdims=True), axis=1, keepdims=True)
            loss_ref[...] = tot * (0.5 / D_MODEL)

    row = pl.BlockSpec((tm, D_MODEL), lambda i: (i, 0))
    vec = pl.BlockSpec((1, D_MODEL), lambda i: (0, 0))
    return _pcall(body, name=name,
                  out_shape=(SDS((s, D_MODEL), F32), SDS((s, D_MODEL), BF16), SDS((1, D_MODEL), F32), SDS((1, D_MODEL), F32),
                             SDS((1, 1), F32)),
                  grid=(nt,),
                  in_specs=[pl.BlockSpec((tm, k), lambda i: (i, 0)), pl.BlockSpec((k, D_MODEL), lambda i: (0, 0)), row, row,
                            vec, vec],
                  out_specs=(row, row, vec, vec, pl.BlockSpec((1, 1), lambda i: (0, 0))),
                  scratch_shapes=[pltpu.VMEM((8, D_MODEL), F32)] * 3, dims=("arbitrary",))(x, w, a, target, g, b)


def _mm_ln_bwd(pairs, addend, coef, a, r, g, ca, name, after=None):
    s = a.shape[0]
    has_r = r is not None
    has_add = addend is not None
    extra = [] if after is None else [after]
    n_pairs = len(pairs)

    def vmem(tm):
        tot = tm * D_MODEL * (4 * (2 + has_r) + 6)
        for pa, pb, _ in pairs:
            tot += tm * pa.shape[1] * pa.dtype.itemsize + pb.size * pb.dtype.itemsize
        return 2 * tot

    tm = next(c for c in (512, 256, 128) if s % c == 0 and vmem(c) <= MM_VMEM_BUDGET)
    nt = s // tm

    def body(*refs):
        ins = refs[2 * n_pairs:]
        add_ref = ins[0] if has_add else None
        ins = ins[has_add:]
        a_ref = ins[0]
        r_ref = ins[1] if has_r else None
        g_ref = ins[1 + has_r]
        du_ref, dub_ref, dg_ref, db_ref, acc_g, acc_b = ins[2 + has_r + len(extra):]
        i = pl.program_id(0)

        @pl.when(i == 0)
        def _():
            acc_g[...] = jnp.zeros_like(acc_g)
            acc_b[...] = jnp.zeros_like(acc_b)

        dyv = coef * add_ref[...] if has_add else None
        for p, (_, _, tb) in enumerate(pairs):
            part = lax.dot_general(refs[2 * p][...].astype(BF16), refs[2 * p + 1][...], _NT if tb else _NN,
                                   preferred_element_type=F32)
            dyv = part if dyv is None else dyv + part
        u = a_ref[...] if ca == 1.0 else ca * a_ref[...]
        if has_r:
            u = u + r_ref[...]
        mu = jnp.mean(u, axis=-1, keepdims=True)
        xc = u - mu
        var = jnp.mean(xc * xc, axis=-1, keepdims=True)
        rstd = lax.rsqrt(var + LN_EPS)
        xhat = xc * rstd
        dxh = dyv * g_ref[...]
        du = rstd * (dxh - jnp.mean(dxh, axis=-1, keepdims=True) - xhat * jnp.mean(dxh * xhat, axis=-1, keepdims=True))
        du_ref[...] = du
        dub_ref[...] = du.astype(BF16)
        acc_g[...] += jnp.sum((dyv * xhat).reshape(tm // 8, 8, D_MODEL), axis=0)
        acc_b[...] += jnp.sum(dyv.reshape(tm // 8, 8, D_MODEL), axis=0)

        @pl.when(i == nt - 1)
        def _():
            dg_ref[...] = jnp.sum(acc_g[...], axis=0, keepdims=True)
            db_ref[...] = jnp.sum(acc_b[...], axis=0, keepdims=True)

    row = pl.BlockSpec((tm, D_MODEL), lambda i: (i, 0))
    vec = pl.BlockSpec((1, D_MODEL), lambda i: (0, 0))
    in_specs, args = [], []
    for pa, pb, _ in pairs:
        in_specs += [pl.BlockSpec((tm, pa.shape[1]), lambda i: (i, 0)), pl.BlockSpec(pb.shape, lambda i: (0, 0))]
        args += [pa, pb]
    in_specs += [row] * (has_add + 1 + has_r) + [vec] + [pl.BlockSpec((8, 128), lambda i: (0, 0))] * len(extra)
    args += ([addend] if has_add else []) + [a] + ([r] if has_r else []) + [g] + extra
    return _pcall(body, name=name,
                  out_shape=(SDS((s, D_MODEL), F32), SDS((s, D_MODEL), BF16), SDS((1, D_MODEL), F32), SDS((1, D_MODEL), F32)),
                  grid=(nt,), in_specs=in_specs, out_specs=(row, row, vec, vec),
                  scratch_shapes=[pltpu.VMEM((8, D_MODEL), F32), pltpu.VMEM((8, D_MODEL), F32)],
                  dims=("arbitrary",))(*args)


def _xattn_fwd(q, k, v):
    s = q.shape[0]
    tq = _pick(s, (512,))
    scale = X_HEAD_DIM ** -0.5

    def body(q_ref, k_ref, v_ref, o_ref, ob_ref):
        qv, kv, vv = q_ref[...], k_ref[...], v_ref[...]
        outs = []
        for h in range(X_HEADS):
            sl = slice(h * X_HEAD_DIM, (h + 1) * X_HEAD_DIM)
            sc = lax.dot_general(qv[:, sl], kv[:, sl], _NT, preferred_element_type=F32) * scale
            e = jnp.exp(sc - jnp.max(sc, axis=-1, keepdims=True))
            p = e / jnp.sum(e, axis=-1, keepdims=True)
            outs.append(lax.dot_general(p.astype(BF16), vv[:, sl], _NN, preferred_element_type=F32))
        o = jnp.concatenate(outs, axis=1)
        o_ref[...] = o
        ob_ref[...] = o.astype(BF16)

    row = pl.BlockSpec((tq, D_MODEL), lambda i: (i, 0))
    full = pl.BlockSpec((MEM_LEN, D_MODEL), lambda i: (0, 0))
    return _pcall(body, name="xattn_fwd", out_shape=(SDS((s, D_MODEL), F32), SDS((s, D_MODEL), BF16)), grid=(s // tq,),
                  in_specs=[row, full, full], out_specs=(row, row), dims=("parallel",))(q, k, v)


def _xattn_bwd(q, k, v, o, do):
    s = q.shape[0]
    tq = _pick(s, (512,))
    scale = X_HEAD_DIM ** -0.5

    def body(q_ref, k_ref, v_ref, o_ref, do_ref, dq_ref, dk_ref, dv_ref):
        i = pl.program_id(0)

        @pl.when(i == 0)
        def _():
            dk_ref[...] = jnp.zeros_like(dk_ref)
            dv_ref[...] = jnp.zeros_like(dv_ref)

        qv, kv, vv, ov, dov = q_ref[...], k_ref[...], v_ref[...], o_ref[...], do_ref[...]
        dqs, dks, dvs = [], [], []
        for h in range(X_HEADS):
            sl = slice(h * X_HEAD_DIM, (h + 1) * X_HEAD_DIM)
            sc = lax.dot_general(qv[:, sl], kv[:, sl], _NT, preferred_element_type=F32) * scale
            e = jnp.exp(sc - jnp.max(sc, axis=-1, keepdims=True))
            p = e / jnp.sum(e, axis=-1, keepdims=True)
            doh = dov[:, sl]
            dob = doh.astype(BF16)
            delta = jnp.sum(doh * ov[:, sl], axis=-1, keepdims=True)
            dvs.append(lax.dot_general(p.astype(BF16), dob, _TN, preferred_element_type=F32))
            dp = lax.dot_general(dob, vv[:, sl], _NT, preferred_element_type=F32)
            ds = (p * (dp - delta)).astype(BF16)
            dqs.append(lax.dot_general(ds, kv[:, sl], _NN, preferred_element_type=F32) * scale)
            dks.append(lax.dot_general(ds, qv[:, sl], _TN, preferred_element_type=F32) * scale)
        dq_ref[...] = jnp.concatenate(dqs, axis=1).astype(BF16)
        dk_ref[...] += jnp.concatenate(dks, axis=1)
        dv_ref[...] += jnp.concatenate(dvs, axis=1)

    row = pl.BlockSpec((tq, D_MODEL), lambda i: (i, 0))
    full = pl.BlockSpec((MEM_LEN, D_MODEL), lambda i: (0, 0))
    return _pcall(body, name="xattn_bwd",
                  out_shape=(SDS((s, D_MODEL), BF16), SDS((MEM_LEN, D_MODEL), F32), SDS((MEM_LEN, D_MODEL), F32)),
                  grid=(s // tq,), in_specs=[row, full, full, row, row], out_specs=(row, full, full),
                  dims=("arbitrary",))(q, k, v, o, do)


_SQRT_HALF = 0.7071067811865476
_INV_SQRT_2PI = 0.3989422804014327


def _halo_specs(s, tm, width, rows=8):
    nb = s // rows
    r = tm // rows
    prev = pl.BlockSpec((rows, width), lambda i: (jnp.maximum(i * r - 1, 0), 0))
    nxt = pl.BlockSpec((rows, width), lambda i: (jnp.minimum((i + 1) * r, nb - 1), 0))
    return prev, nxt


def _shifted(x, before_row, after_row, i, nt):
    tm = x.shape[0]
    row = lax.broadcasted_iota(jnp.int32, x.shape, 0)
    first = jnp.where(i == 0, 0.0, 1.0) * before_row
    last = jnp.where(i == nt - 1, 0.0, 1.0) * after_row
    xm1 = jnp.where(row == 0, first, pltpu.roll(x, 1, 0))
    xp1 = jnp.where(row == tm - 1, last, pltpu.roll(x, tm - 1, 0))
    return xm1, xp1


BF16_ROWS = 16


def _ffn_fwd(hb, wg_t, wu_t, cw, cb):
    s = hb.shape[0]
    tm = _pick(s, (256,))
    nt = s // tm
    hr = BF16_ROWS

    def body(h_ref, hp_ref, hn_ref, wg_ref, wu_ref, cw_ref, cb_ref, g_ref, up_ref, cdf_ref, act_ref):
        i = pl.program_id(0)
        hv = h_ref[...]
        g_ext = lax.dot_general(jnp.concatenate([hp_ref[...], hv, hn_ref[...]], axis=0), wg_ref[...], _NT,
                                preferred_element_type=F32)
        gv = g_ext[hr:hr + tm]
        upv = lax.dot_general(hv, wu_ref[...], _NT, preferred_element_type=F32)
        gm1, gp1 = _shifted(gv, g_ext[hr - 1:hr], g_ext[hr + tm:hr + tm + 1], i, nt)
        gc = gm1 * cw_ref[0:1, :] + gv * cw_ref[1:2, :] + gp1 * cw_ref[2:3, :] + cb_ref[...]
        cdf = 0.5 * (1.0 + lax.erf(gc * _SQRT_HALF))
        g_ref[...] = gv
        up_ref[...] = upv
        cdf_ref[...] = cdf
        act_ref[...] = (gc * cdf * upv).astype(BF16)

    hrow = pl.BlockSpec((tm, D_MODEL), lambda i: (i, 0))
    prev, nxt = _halo_specs(s, tm, D_MODEL, hr)
    wfull = pl.BlockSpec((D_FF, D_MODEL), lambda i: (0, 0), pipeline_mode=pl.Buffered(1))
    row = pl.BlockSpec((tm, D_FF), lambda i: (i, 0))
    return _pcall(body, name="ffn_fwd", out_shape=(SDS((s, D_FF), F32),) * 3 + (SDS((s, D_FF), BF16),),
                  grid=(nt,), in_specs=[hrow, prev, nxt, wfull, wfull, pl.BlockSpec((8, D_FF), lambda i: (0, 0)),
                                        pl.BlockSpec((1, D_FF), lambda i: (0, 0))],
                  out_specs=(row, row, row, row), dims=("parallel",))(hb, hb, hb, wg_t, wu_t, cw, cb)


def _ffn_bwd(dffb, w_down, g, up, cdf, cw, cb):
    s = g.shape[0]
    tm = _pick(s, (256,))
    nt = s // tm
    hr = BF16_ROWS

    def body(df_ref, dfp_ref, dfn_ref, wd_ref, g_ref, gp_ref, gn_ref, up_ref, upp_ref, upn_ref, cdf_ref, cw_ref, cb_ref,
             dg_ref, dup_ref, dcw_ref, dcb_ref, a0, a1, a2, a3):
        i = pl.program_id(0)

        @pl.when(i == 0)
        def _():
            for a in (a0, a1, a2, a3):
                a[...] = jnp.zeros_like(a)

        def d_conv_out(gc_, up_, da_, cdf_):
            pdf_ = jnp.exp(-0.5 * gc_ * gc_) * _INV_SQRT_2PI
            return da_ * up_ * (cdf_ + gc_ * pdf_)

        def cdf_of(gc_):
            return 0.5 * (1.0 + lax.erf(gc_ * _SQRT_HALF))

        df_ext = jnp.concatenate([dfp_ref[...], df_ref[...], dfn_ref[...]], axis=0)
        tn = 256
        for c0 in range(0, D_FF, tn):
            cs = slice(c0, c0 + tn)
            da_ext = lax.dot_general(df_ext, wd_ref[cs, :], _NT, preferred_element_type=F32)
            cw0, cw1, cw2, cbv = cw_ref[0:1, cs], cw_ref[1:2, cs], cw_ref[2:3, cs], cb_ref[:, cs]
            gv = g_ref[:, cs]
            g_before, g_after = gp_ref[:, cs], gn_ref[:, cs]
            gm1, gp1 = _shifted(gv, g_before[7:8, :], g_after[0:1, :], i, nt)
            gc = gm1 * cw0 + gv * cw1 + gp1 * cw2 + cbv
            da = da_ext[hr:hr + tm]
            cdf = cdf_ref[:, cs]
            dgc = d_conv_out(gc, up_ref[:, cs], da, cdf)
            dup_ref[:, cs] = (da * (gc * cdf)).astype(BF16)
            gc_b = g_before[6:7, :] * cw0 + g_before[7:8, :] * cw1 + gv[0:1, :] * cw2 + cbv
            gc_a = gv[tm - 1:tm, :] * cw0 + g_after[0:1, :] * cw1 + g_after[1:2, :] * cw2 + cbv
            dgc_b = jnp.where(i == 0, 0.0, 1.0) * d_conv_out(gc_b, upp_ref[7:8, cs], da_ext[hr - 1:hr], cdf_of(gc_b))
            dgc_a = jnp.where(i == nt - 1, 0.0, 1.0) * d_conv_out(gc_a, upn_ref[0:1, cs], da_ext[hr + tm:hr + tm + 1],
                                                                  cdf_of(gc_a))
            row = lax.broadcasted_iota(jnp.int32, dgc.shape, 0)
            dgc_m1 = jnp.where(row == 0, dgc_b, pltpu.roll(dgc, 1, 0))
            dgc_p1 = jnp.where(row == tm - 1, dgc_a, pltpu.roll(dgc, tm - 1, 0))
            dg_ref[:, cs] = (dgc_p1 * cw0 + dgc * cw1 + dgc_m1 * cw2).astype(BF16)

            def fold(t):
                return jnp.sum(t.reshape(tm // 8, 8, tn), axis=0)

            a0[:, cs] += fold(dgc * gm1)
            a1[:, cs] += fold(dgc * gv)
            a2[:, cs] += fold(dgc * gp1)
            a3[:, cs] += fold(dgc)

        @pl.when(i == nt - 1)
        def _():
            dcw_ref[...] = jnp.concatenate(
                [jnp.sum(a[...], axis=0, keepdims=True) for a in (a0, a1, a2)] + [jnp.zeros((5, D_FF), F32)], axis=0)
            dcb_ref[...] = jnp.sum(a3[...], axis=0, keepdims=True)

    row = pl.BlockSpec((tm, D_FF), lambda i: (i, 0))
    prev, nxt = _halo_specs(s, tm, D_FF)
    cw_spec = pl.BlockSpec((8, D_FF), lambda i: (0, 0))
    cb_spec = pl.BlockSpec((1, D_FF), lambda i: (0, 0))
    dprev, dnxt = _halo_specs(s, tm, D_MODEL, hr)
    return _pcall(body, name="ffn_bwd",
                  out_shape=(SDS((s, D_FF), BF16), SDS((s, D_FF), BF16), SDS((8, D_FF), F32), SDS((1, D_FF), F32)),
                  grid=(nt,),
                  in_specs=[pl.BlockSpec((tm, D_MODEL), lambda i: (i, 0)), dprev, dnxt,
                            pl.BlockSpec((D_FF, D_MODEL), lambda i: (0, 0), pipeline_mode=pl.Buffered(1))]
                  + [row, prev, nxt] * 2 + [row, cw_spec, cb_spec],
                  out_specs=(row, row, cw_spec, cb_spec), scratch_shapes=[pltpu.VMEM((8, D_FF), F32)] * 4,
                  dims=("arbitrary",))(dffb, dffb, dffb, w_down, g, g, g, up, up, up, cdf, cw, cb)


def _adamw(w, g, m, v, name):
    rows, cols = w.shape
    tr = _pick(rows, (256, 128, 64, 32, 16, 8))
    c1 = 1.0 - ADAM_B1 ** ADAM_STEP
    c2 = 1.0 - ADAM_B2 ** ADAM_STEP

    def body(w_ref, g_ref, m_ref, v_ref, d_ref, nm_ref, nv_ref):
        gv = g_ref[...]
        nm = ADAM_B1 * m_ref[...] + (1.0 - ADAM_B1) * gv
        nv = ADAM_B2 * v_ref[...] + (1.0 - ADAM_B2) * (gv * gv)
        d_ref[...] = -ADAM_LR * ((nm / c1) / (jnp.sqrt(nv / c2) + ADAM_EPS) + ADAM_WD * w_ref[...])
        nm_ref[...] = nm
        nv_ref[...] = nv

    blk = pl.BlockSpec((tr, cols), lambda i: (i, 0))
    return _pcall(body, name=name, out_shape=(SDS(w.shape, F32),) * 3, grid=(rows // tr,), in_specs=[blk] * 4,
                  out_specs=(blk,) * 3, dims=("parallel",))(w, g, m, v)


def _adamw_many(ws, gs, ms, vs, name):
    n = len(ws)
    c1 = 1.0 - ADAM_B1 ** ADAM_STEP
    c2 = 1.0 - ADAM_B2 ** ADAM_STEP

    def body(*refs):
        outs = refs[4 * n:]
        for k in range(n):
            gv = refs[n + k][...]
            nm = ADAM_B1 * refs[2 * n + k][...] + (1.0 - ADAM_B1) * gv
            nv = ADAM_B2 * refs[3 * n + k][...] + (1.0 - ADAM_B2) * (gv * gv)
            outs[k][...] = -ADAM_LR * ((nm / c1) / (jnp.sqrt(nv / c2) + ADAM_EPS) + ADAM_WD * refs[k][...])
            outs[n + k][...] = nm
            outs[2 * n + k][...] = nv

    shapes = tuple(SDS(w.shape, F32) for w in ws)
    res = _pcall(body, name=name, out_shape=shapes * 3)(*ws, *gs, *ms, *vs)
    return res[:n], res[n:2 * n], res[2 * n:]


def _all_gather_rows(x_shard, *, name, in_vmem, sum_rows=False, after=None):
    m_per, n = x_shard.shape
    extra = [] if after is None else [after]

    def body(x_ref, *rest):
        out_ref, rest = rest[len(extra)], rest[len(extra) + 1:]
        if sum_rows:
            sum_ref, send_sems, recv_sems, local_sem = rest
        else:
            send_sems, recv_sems, local_sem = rest
        x, y, c = lax.axis_index("x"), lax.axis_index("y"), lax.axis_index("c")
        me, sibling = (x, y, c), (x, y, 1 - c)
        chips = [(1 - x, y), (x, 1 - y), (1 - x, 1 - y)]

        def rows(px, py, pc):
            return out_ref.at[pl.ds((4 * px + 2 * py + pc) * m_per, m_per), :]

        def copy(k, block, to, src=None):
            return pltpu.make_async_remote_copy(
                src_ref=rows(*block) if src is None else src, dst_ref=rows(*block), send_sem=send_sems.at[k],
                recv_sem=recv_sems.at[k], device_id=to, device_id_type=pl.DeviceIdType.MESH)

        mine = pltpu.make_async_copy(x_ref, rows(*me), local_sem)
        mine.start()
        first = [copy(0, me, sibling, src=x_ref)]
        first += [copy(1 + j, me, (*chip, c), src=x_ref) for j, chip in enumerate(chips)]
        for cp in first:
            cp.start()
        passed = [copy(4 + j, (*chip, c), sibling) for j, chip in enumerate(chips)]
        for j, chip in enumerate(chips):
            copy(1 + j, (*chip, c), me).wait_recv()
            passed[j].start()
        copy(0, sibling, me).wait_recv()
        for j, chip in enumerate(chips):
            copy(4 + j, (*chip, 1 - c), me).wait_recv()
        for cp in first + passed:
            cp.wait_send()
        mine.wait()
        if sum_rows:
            acc = out_ref[0:m_per, :]
            for dev in range(1, N_DEV):
                acc = acc + out_ref[dev * m_per:(dev + 1) * m_per, :]
            sum_ref[...] = acc

    space = pltpu.VMEM if in_vmem else pl.ANY
    out_shape = [SDS((N_DEV * m_per, n), x_shard.dtype)]
    out_specs = [pl.BlockSpec(memory_space=space)]
    if sum_rows:
        out_shape.append(SDS((m_per, n), x_shard.dtype))
        out_specs.append(pl.BlockSpec(memory_space=pltpu.VMEM))
    res = _PALLAS_CALL(
        body, name=name, out_shape=tuple(out_shape),
        in_specs=[pl.BlockSpec(memory_space=space)] + [pl.BlockSpec(memory_space=pl.ANY)] * len(extra),
        out_specs=tuple(out_specs),
        scratch_shapes=[pltpu.SemaphoreType.DMA((7,)), pltpu.SemaphoreType.DMA((7,)), pltpu.SemaphoreType.DMA],
        compiler_params=pltpu.CompilerParams(vmem_limit_bytes=VMEM_LIMIT_BYTES),
    )(x_shard, *extra)
    return res if sum_rows else res[0]


_HBM = pl.BlockSpec(memory_space=pltpu.HBM)
_SEM = pl.BlockSpec(memory_space=pltpu.SEMAPHORE)
_SPLIT_PARAMS = dict(has_side_effects=pltpu.SideEffectType.DATAFLOW_SIDE_EFFECTING)


def _split_copies(src_ref, land_ref, send_sems, recv_sems, gather):
    x, y, c = lax.axis_index("x"), lax.axis_index("y"), lax.axis_index("c")
    copies = []
    for k in range(1, N_DEV):
        px = 1 - x if k & 4 else x
        py = 1 - y if k & 2 else y
        pc = 1 - c if k & 1 else c
        if gather:
            rows = src_ref.shape[0]
            src, dst = src_ref, land_ref.at[pl.ds((4 * x + 2 * y + c) * rows, rows), :]
        else:
            src, dst = src_ref.at[4 * px + 2 * py + pc], land_ref.at[k - 1]
        copies.append(pltpu.make_async_remote_copy(
            src_ref=src, dst_ref=dst, send_sem=send_sems.at[k - 1], recv_sem=recv_sems.at[k - 1],
            device_id=(px, py, pc), device_id_type=pl.DeviceIdType.MESH))
    return copies


def _exchange_start(src, land_shape, *, gather, name):
    def body(src_ref, land_ref, send_sems, recv_sems, src_thru, land_thru, token):
        for cp in _split_copies(src_ref, land_ref, send_sems, recv_sems, gather):
            cp.start()
        token[...] = jnp.zeros_like(token)

    land = pltpu.with_memory_space_constraint(lax.empty(land_shape, src.dtype), pltpu.HBM)
    return _PALLAS_CALL(
        body, name=name,
        out_shape=(pltpu.SemaphoreType.DMA((N_DEV - 1,)), pltpu.SemaphoreType.DMA((N_DEV - 1,)),
                   pltpu.HBM(src.shape, src.dtype), pltpu.HBM(land_shape, src.dtype), SDS((8, 128), F32)),
        in_specs=(_HBM, _HBM), out_specs=(_SEM, _SEM, _HBM, _HBM, pl.BlockSpec(memory_space=pltpu.VMEM)),
        input_output_aliases={0: 2, 1: 3}, compiler_params=pltpu.CompilerParams(**_SPLIT_PARAMS),
    )(pltpu.with_memory_space_constraint(src, pltpu.HBM), land)


def _exchange_wait(started, after, *, gather, name):
    send_sems, recv_sems, src_thru, land_thru, _ = started

    def body(src_ref, land_ref, send_sems, recv_sems, after_ref, src_out, land_out):
        copies = _split_copies(src_ref, land_ref, send_sems, recv_sems, gather)
        for cp in copies:
            cp.wait_send()
        for cp in copies:
            cp.wait_recv()

    return _PALLAS_CALL(
        body, name=name,
        out_shape=(pltpu.HBM(src_thru.shape, src_thru.dtype), pltpu.HBM(land_thru.shape, land_thru.dtype)),
        in_specs=(_HBM, _HBM, _SEM, _SEM, pl.BlockSpec(memory_space=pl.ANY)), out_specs=(_HBM, _HBM),
        input_output_aliases={0: 0, 1: 1}, compiler_params=pltpu.CompilerParams(**_SPLIT_PARAMS),
    )(src_thru, land_thru, send_sems, recv_sems, after)


def _sum_parts(own, land, name):
    r, n = own.shape
    tr = _pick(r, (264, 320, 336, 128, 64, 32, 16, 8))

    def body(own_ref, x_ref, o_ref):
        acc = own_ref[...]
        for k in range(N_DEV - 1):
            acc = acc + x_ref[k].astype(F32)
        o_ref[...] = acc

    return _pcall(body, name=name, out_shape=SDS((r, n), F32), grid=(r // tr,),
                  in_specs=[pl.BlockSpec((tr, n), lambda i: (i, 0)), pl.BlockSpec((N_DEV - 1, tr, n), lambda i: (0, i, 0))],
                  out_specs=pl.BlockSpec((tr, n), lambda i: (i, 0)), dims=("parallel",))(own, land)


def _pad_rows(a, rows):
    return jnp.pad(a, ((0, rows - a.shape[0]), (0, 0)))


def kernel(x, mem, positions, ln_in_g, ln_in_b, w_in, attn_sink, g_win, g_dil, w_mix_out, ln1_g, ln1_b, mem_ln_g, mem_ln_b, w_xq, w_xk, w_xv, w_xo, ln2_g, ln2_b, w_gate, w_up, conv_w, conv_b, w_down, ln3_g, ln3_b, loss_target, m_ln_in_g, m_ln_in_b, m_w_in, m_attn_sink, m_g_win, m_g_dil, m_w_mix_out, m_ln1_g, m_ln1_b, m_mem_ln_g, m_mem_ln_b, m_w_xq, m_w_xk, m_w_xv, m_w_xo, m_ln2_g, m_ln2_b, m_w_gate, m_w_up, m_conv_w, m_conv_b, m_w_down, m_ln3_g, m_ln3_b, v_ln_in_g, v_ln_in_b, v_w_in, v_attn_sink, v_g_win, v_g_dil, v_w_mix_out, v_ln1_g, v_ln1_b, v_mem_ln_g, v_mem_ln_b, v_w_xq, v_w_xk, v_w_xv, v_w_xo, v_ln2_g, v_ln2_b, v_w_gate, v_w_up, v_conv_w, v_conv_b, v_w_down, v_ln3_g, v_ln3_b):
    weights = dict(ln_in_g=ln_in_g, ln_in_b=ln_in_b, w_in=w_in, attn_sink=attn_sink, g_win=g_win, g_dil=g_dil, w_mix_out=w_mix_out, ln1_g=ln1_g, ln1_b=ln1_b, mem_ln_g=mem_ln_g, mem_ln_b=mem_ln_b, w_xq=w_xq, w_xk=w_xk, w_xv=w_xv, w_xo=w_xo, ln2_g=ln2_g, ln2_b=ln2_b, w_gate=w_gate, w_up=w_up, conv_w=conv_w, conv_b=conv_b, w_down=w_down, ln3_g=ln3_g, ln3_b=ln3_b)
    mom_m = dict(ln_in_g=m_ln_in_g, ln_in_b=m_ln_in_b, w_in=m_w_in, attn_sink=m_attn_sink, g_win=m_g_win, g_dil=m_g_dil, w_mix_out=m_w_mix_out, ln1_g=m_ln1_g, ln1_b=m_ln1_b, mem_ln_g=m_mem_ln_g, mem_ln_b=m_mem_ln_b, w_xq=m_w_xq, w_xk=m_w_xk, w_xv=m_w_xv, w_xo=m_w_xo, ln2_g=m_ln2_g, ln2_b=m_ln2_b, w_gate=m_w_gate, w_up=m_w_up, conv_w=m_conv_w, conv_b=m_conv_b, w_down=m_w_down, ln3_g=m_ln3_g, ln3_b=m_ln3_b)
    mom_v = dict(ln_in_g=v_ln_in_g, ln_in_b=v_ln_in_b, w_in=v_w_in, attn_sink=v_attn_sink, g_win=v_g_win, g_dil=v_g_dil, w_mix_out=v_w_mix_out, ln1_g=v_ln1_g, ln1_b=v_ln1_b, mem_ln_g=v_mem_ln_g, mem_ln_b=v_mem_ln_b, w_xq=v_w_xq, w_xk=v_w_xk, w_xv=v_w_xv, w_xo=v_w_xo, ln2_g=v_ln2_g, ln2_b=v_ln2_b, w_gate=v_w_gate, w_up=v_w_up, conv_w=v_conv_w, conv_b=v_conv_b, w_down=v_w_down, ln3_g=v_ln3_g, ln3_b=v_ln3_b)
    order = list(weights)
    s = x.shape[1]
    xs = x[0]
    mems = mem[0]
    target = loss_target[0]
    row = lambda a: a.reshape(1, -1)

    shard_rows = dict(w_in=w_in[0].T, w_gate=w_gate[0].T, w_up=w_up[0].T, w_mix_out=w_mix_out[0], w_xq=w_xq[0],
                      w_xk=w_xk[0], w_xv=w_xv[0], w_xo=w_xo[0], w_down=w_down[0])
    me_lin = 4 * lax.axis_index("x") + 2 * lax.axis_index("y") + lax.axis_index("c")
    w_in_full = _all_gather_rows(shard_rows["w_in"].astype(BF16), name="w_in_all_gather", in_vmem=False)
    late_rows = PACK_ROWS[1:]
    late_r = sum(r for _, r in late_rows)
    packed = jnp.concatenate([shard_rows[n].astype(BF16) for n, _ in late_rows], axis=0)
    w_started = _exchange_start(packed, (N_DEV * late_r, D_MODEL), gather=True, name="weight_gather_start")
    cw_pad = jnp.pad(conv_w[0], ((0, 5), (0, 32)))
    cw_all = _all_gather_rows(cw_pad, name="conv_w_all_gather", in_vmem=True).reshape(N_DEV, 8, 384)
    cw_full = jnp.transpose(cw_all[:, :3, :352], (1, 0, 2)).reshape(3, D_FF)
    cw8 = _pad_rows(cw_full, 8)

    tabs = _rope_tables(positions.astype(F32).reshape(s, 1) + w_started[4][0, 0])
    h0, h0b = _ln_fwd(xs, None, row(ln_in_g), row(ln_in_b), 1.0, "ln_in_fwd")
    zw, *zg = _proj_rope(h0b, w_in_full, tabs[0])
    oa, lse_a = _banded_fwd(zw, attn_sink, name="win_attn_fwd", **_WIN_CFG)
    og_views, lg_views = [], []
    for gi in range(3):
        o_g, l_g = _banded_fwd(zg[gi], None, name=f"dil_attn_fwd{gi}", **_dil_cfg(gi))
        og_views.append(o_g)
        lg_views.append(l_g)
    mixed, ob_views, lb_views = _mix_norm_fwd(oa, og_views, lg_views, g_win, g_dil)
    packed_thru, land = _exchange_wait(w_started, mixed, gather=True, name="weight_gather_wait")
    gathered = lax.dynamic_update_slice(land, packed_thru, (me_lin * late_r, 0)).reshape(N_DEV, late_r, D_MODEL)
    full = {}
    off = 0
    for n, r in late_rows:
        full[n] = gathered[:, off:off + r, :].reshape(N_DEV * r, D_MODEL)
        off += r
    mix, h1, h1b = _mm_ln_fwd(mixed, full["w_mix_out"], h0, ln1_g, ln1_b, ALPHA, "mm_mix_out_ln1")
    _, mem_nb = _ln_fwd(mems, None, mem_ln_g, mem_ln_b, 1.0, "mem_ln_fwd")
    kx = _mm(mem_nb, full["w_xk"], trans_b=False, out_dtype=BF16, name="mm_xk")
    vx = _mm(mem_nb, full["w_xv"], trans_b=False, out_dtype=BF16, name="mm_xv")
    qx = _mm(h1b, full["w_xq"], trans_b=False, out_dtype=BF16, name="mm_xq")
    ox, oxb = _xattn_fwd(qx, kx, vx)
    xa, h2, h2b = _mm_ln_fwd(oxb, full["w_xo"], h1, ln2_g, ln2_b, ALPHA, "mm_xo_ln2")
    gate, up, cdf, act = _ffn_fwd(h2b, full["w_gate"], full["w_up"], cw8, conv_b)

    du3, du3b, d_ln3_g, d_ln3_b, loss_local = _mm_ln_loss(act, full["w_down"], h2, target, ln3_g, ln3_b, ALPHA,
                                                          "mm_down_ln3_loss")
    dw_down = _mm_tn(act, du3b, name="mm_dw_down")
    dgate, dup, dcw8, d_conv_b = _ffn_bwd(du3b, full["w_down"], gate, up, cdf, cw8, conv_b)
    dw_gate_t = _mm_tn(dgate, h2b, name="mm_dw_gate")
    dw_up_t = _mm_tn(dup, h2b, name="mm_dw_up")
    rows_of = dict(PACK_ROWS)

    own_f32 = {}

    def start_grad_exchange(parts, name, payload=F32):
        gp = jnp.concatenate([g.reshape(N_DEV, rows_of[n], D_MODEL) for n, g in parts], axis=1)
        if payload != F32:
            own_f32[name] = lax.dynamic_index_in_dim(gp, me_lin, axis=0, keepdims=False)
            gp = gp.astype(payload)
        return _exchange_start(gp, (N_DEV - 1,) + gp.shape[1:], gather=False, name=name)

    ffn_parts = (("w_gate", dw_gate_t), ("w_up", dw_up_t), ("w_down", dw_down))
    ffn_started = start_grad_exchange(ffn_parts, "grad_start_ffn")
    du2, du2b, d_ln2_g, d_ln2_b = _mm_ln_bwd(((dgate, full["w_gate"], False), (dup, full["w_up"], False)), du3, ALPHA,
                                             h1, xa, ln2_g + ffn_started[4][0, 0], ALPHA, "mm_dh2_ln2_bwd")
    dox = _mm(du2b, full["w_xo"], trans_b=True, out_dtype=F32, name="mm_d_ox")
    dw_xo = _mm_tn(oxb, du2b, name="mm_dw_xo")
    dqx, dkx, dvx = _xattn_bwd(qx, kx, vx, ox, dox)
    dw_xq = _mm_tn(h1b, dqx, name="mm_dw_xq")
    dw_xk = _mm_tn(mem_nb, dkx, name="mm_dw_xk")
    dw_xv = _mm_tn(mem_nb, dvx, name="mm_dw_xv")
    _, _, d_mem_ln_g, d_mem_ln_b = _mm_ln_bwd(((dkx, full["w_xk"], True), (dvx, full["w_xv"], True)), None, 1.0, mems,
                                              None, mem_ln_g, 1.0, "mm_dmem_ln_bwd")
    du1, du1b, d_ln1_g, d_ln1_b = _mm_ln_bwd(((dqx, full["w_xq"], True),), du2, ALPHA, h0, mix, ln1_g, ALPHA,
                                             "mm_dh1_ln1_bwd")
    dmixed = _mm(du1b, full["w_mix_out"], trans_b=True, out_dtype=F32, name="mm_d_mixed")
    dw_mix_out = _mm_tn(mixed, du1b, name="mm_dw_mix_out")
    attn_parts = (("w_mix_out", dw_mix_out), ("w_xq", dw_xq), ("w_xk", dw_xk), ("w_xv", dw_xv), ("w_xo", dw_xo))
    attn_started = start_grad_exchange(attn_parts, "grad_start_attn")
    doa, dob_views, d_g_win, d_g_dil = _mix_norm_bwd(oa, ob_views[0], dmixed, g_win + attn_started[4][0, 0], g_dil)
    dqa, dkva, dsink8 = _banded_bwd(zw, oa, lse_a, doa, tabs[0], attn_sink, name="win_attn_bwd", **_WIN_CFG)
    dq_views, dkv_views = [], []
    for gi in range(3):
        dq_g, dkv_g = _banded_bwd(zg[gi], ob_views[gi], lb_views[gi], dob_views[gi], tabs[gi], None,
                                  name=f"dil_attn_bwd{gi}", **{**_dil_cfg(gi), "rc": ATTN_TQ})
        dq_views.append(dq_g)
        dkv_views.append(dkv_g)
    dz = _dz_assemble(dq_views, dkv_views, dqa, dkva)
    dw_in_t = _mm_tn(dz, h0b, name="mm_dw_in")
    in_parts = (("w_in", dw_in_t),)
    in_started = start_grad_exchange(in_parts, "grad_start_in", payload=BF16)
    dx, _, d_ln_in_g, d_ln_in_b = _mm_ln_bwd(((dz, w_in_full, False),), du1, ALPHA, xs, None, row(ln_in_g), 1.0,
                                             "mm_dh0_ln_in_bwd", after=in_started[4])

    grads, delta, new_m, new_v = {}, {}, {}, {}
    after = dx
    for parts, started, tag in ((ffn_parts, ffn_started, "ffn"), (attn_parts, attn_started, "attn"),
                                (in_parts, in_started, "in")):
        gp_thru, land = _exchange_wait(started, after, gather=False, name=f"grad_wait_{tag}")
        own = own_f32.get(f"grad_start_{tag}")
        if own is None:
            own = lax.dynamic_index_in_dim(gp_thru, me_lin, axis=0, keepdims=False)
        gsum = _sum_parts(own, land, f"grad_sum_{tag}")
        off = 0
        for n, _ in parts:
            blk = gsum[off:off + rows_of[n]]
            off += rows_of[n]
            grads[n] = (blk.T if n in ("w_in", "w_gate", "w_up") else blk)[None]
            shp = weights[n].shape
            d_, m_, v_ = _adamw(weights[n].reshape(shp[1:]), grads[n].reshape(shp[1:]), mom_m[n].reshape(shp[1:]),
                                mom_v[n].reshape(shp[1:]), f"adamw_{n}")
            delta[n], new_m[n], new_v[n] = d_.reshape(shp), m_.reshape(shp), v_.reshape(shp)
            after = d_

    small = jnp.concatenate([
        d_ln_in_g, d_ln_in_b, d_ln1_g, d_ln1_b, d_mem_ln_g, d_mem_ln_b, d_ln2_g, d_ln2_b, d_ln3_g, d_ln3_b,
        jnp.concatenate([d_g_win, d_g_dil], axis=1),
        jnp.pad(d_conv_b, ((0, 0), (0, 3072 - D_FF))).reshape(3, 1024),
        jnp.pad(dsink8[0:1, :], ((0, 0), (0, 1024 - 128))),
        jnp.pad(dcw8[0:3], ((0, 0), (0, 3072 - D_FF))).reshape(9, 1024),
    ], axis=0)
    _, ssum = _all_gather_rows(small, name="small_grad_all_reduce", in_vmem=True, sum_rows=True, after=after)
    names10 = ["ln_in_g", "ln_in_b", "ln1_g", "ln1_b", "mem_ln_g", "mem_ln_b", "ln2_g", "ln2_b", "ln3_g", "ln3_b"]
    small_g = {n: ssum[i:i + 1] for i, n in enumerate(names10)}
    small_g["g_win"] = ssum[10:11, :512]
    small_g["g_dil"] = ssum[10:11, 512:]
    small_g["conv_b"] = ssum[11:14].reshape(1, 3072)[:, :D_FF]
    small_g["attn_sink"] = ssum[14:15, :8]
    small_g["conv_w"] = lax.dynamic_slice_in_dim(ssum[15:24].reshape(3, 3072)[:, :D_FF], me_lin * 352, 352, axis=1)

    small_names = [n for n in order if n not in rows_of]
    two_d = lambda a: a.reshape(-1, a.shape[-1])
    d_s, m_s, v_s = _adamw_many([two_d(weights[n]) for n in small_names], [small_g[n] for n in small_names],
                                [two_d(mom_m[n]) for n in small_names], [two_d(mom_v[n]) for n in small_names],
                                "adamw_small")
    for k, n in enumerate(small_names):
        shp = weights[n].shape
        grads[n], delta[n], new_m[n], new_v[n] = (t.reshape(shp) for t in (small_g[n], d_s[k], m_s[k], v_s[k]))

    loss = lax.psum(loss_local[0, 0], MESH_AXES)
    return (loss, dx[None], *[grads[n] for n in order], *[delta[n] for n in order], *[new_m[n] for n in order],
            *[new_v[n] for n in order])
```

```python
import functools
import math

import jax
import jax.numpy as jnp
from jax import lax
from jax.experimental import pallas as pl
from jax.experimental.pallas import tpu as pltpu

F32 = jnp.float32
BF16 = jnp.bfloat16
SDS = jax.ShapeDtypeStruct
_PALLAS_CALL = pl.pallas_call

D_MODEL = 1024
HEAD_DIM = 64
WIN_HALF = 128
DIL_PAIRS = ((128, 1), (512, 4), (2048, 16))
DIL_SIDE = 64
ROT_DIM = 16
ROPE_THETA = 500000.0
MEM_LEN = 256
X_HEADS = 4
X_HEAD_DIM = 256
D_FF = 2816
IN_WIDTH = 5376
Z_QB, Z_KB, Z_VB, Z_QA, Z_KA, Z_VA = 0, 1536, 3072, 4608, 5120, 5248
W_IN_QA, W_IN_KA, W_IN_QB = 0, 512, 768
ALPHA = (2.0) ** 0.25
LN_EPS = 1e-5
NEG_INF = -1e30
ADAM_LR, ADAM_B1, ADAM_B2, ADAM_EPS, ADAM_WD, ADAM_STEP = 0.001, 0.9, 0.999, 1e-08, 0.01, 10
N_DEV = 8
MESH_AXES = ("x", "y", "c")
VMEM_LIMIT_BYTES = 52 * 1024 * 1024
ATTN_TQ = 256
TABW = 384

PACK_ROWS = (("w_in", 672), ("w_gate", 352), ("w_up", 352), ("w_mix_out", 128), ("w_xq", 128), ("w_xk", 128),
             ("w_xv", 128), ("w_xo", 128), ("w_down", 352))
SMALL_ROWS = 24


def _pick(n, cands):
    for c in cands:
        if n % c == 0:
            return c
    return n


def _pcall(body, *, name, out_shape, grid=None, in_specs=None, out_specs=None, scratch_shapes=(), dims=None,
           aliases=None):
    kw = {}
    if grid is not None:
        kw["grid"] = grid
    if in_specs is not None:
        kw["in_specs"] = in_specs
    if out_specs is not None:
        kw["out_specs"] = out_specs
    if aliases:
        kw["input_output_aliases"] = aliases
    return _PALLAS_CALL(
        body, name=name, out_shape=out_shape, scratch_shapes=list(scratch_shapes),
        compiler_params=pltpu.CompilerParams(dimension_semantics=dims, vmem_limit_bytes=VMEM_LIMIT_BYTES), **kw)


MM_VMEM_BUDGET = 40 * 1024 * 1024


def _mm(a, b, *, trans_b, out_dtype, name, addends=(), coefs=(), after=None, more=()):
    pairs = ((a, b, trans_b),) + tuple(more)
    m = a.shape[0]
    n = b.shape[0] if trans_b else b.shape[1]
    n_add = len(addends)
    extra = [] if after is None else [after]
    out_bytes = jnp.dtype(out_dtype).itemsize

    def vmem(tm, tn):
        tot = tm * tn * (out_bytes + 4 * n_add)
        for pa, pb, _ in pairs:
            tot += tm * pa.shape[1] * pa.dtype.itemsize + pa.shape[1] * tn * pb.dtype.itemsize
        return 2 * tot

    tm, tn = next(((cm, cn) for cn in (n, 1408, 1024, 512, 256, 128) if n % cn == 0
                   for cm in (1024, 512, 256, 128) if m % cm == 0 and vmem(cm, cn) <= MM_VMEM_BUDGET))
    n_pairs = len(pairs)

    def body(*refs):
        o_ref = refs[2 * n_pairs + n_add + len(extra)]
        acc = None
        for p, (_, _, tb) in enumerate(pairs):
            dn = _NT if tb else _NN
            part = lax.dot_general(refs[2 * p][...].astype(BF16), refs[2 * p + 1][...].astype(BF16), dn,
                                   preferred_element_type=F32)
            acc = part if acc is None else acc + part
        for r_ref, c in zip(refs[2 * n_pairs:2 * n_pairs + n_add], coefs):
            acc = acc + (r_ref[...] if c == 1.0 else c * r_ref[...])
        o_ref[...] = acc.astype(out_dtype)

    in_specs, args = [], []
    for pa, pb, tb in pairs:
        k = pa.shape[1]
        in_specs.append(pl.BlockSpec((tm, k), lambda j, i: (i, 0)))
        in_specs.append(pl.BlockSpec((tn, k), lambda j, i: (j, 0)) if tb else pl.BlockSpec((k, tn), lambda j, i: (0, j)))
        args += [pa, pb]
    in_specs += [pl.BlockSpec((tm, tn), lambda j, i: (i, j)) for _ in addends]
    in_specs += [pl.BlockSpec((8, 128), lambda j, i: (0, 0)) for _ in extra]
    return _pcall(body, name=name, out_shape=SDS((m, n), out_dtype), grid=(n // tn, m // tm), in_specs=in_specs,
                  out_specs=pl.BlockSpec((tm, tn), lambda j, i: (i, j)),
                  dims=("parallel", "parallel"))(*args, *addends, *extra)


def _mm_tn(a, b, *, name):
    s, m = a.shape
    n = b.shape[1]
    tm = _pick(m, (768, 1408, 1024, 512, 256, 128))
    tk = _pick(s, (1024, 512, 256))
    nk = s // tk

    def body(a_ref, b_ref, o_ref, acc_ref):
        kk = pl.program_id(1)

        @pl.when(kk == 0)
        def _():
            acc_ref[...] = jnp.zeros_like(acc_ref)

        acc_ref[...] += lax.dot_general(a_ref[...].astype(BF16), b_ref[...].astype(BF16), (((0,), (0,)), ((), ())),
                                        preferred_element_type=F32)

        @pl.when(kk == nk - 1)
        def _():
            o_ref[...] = acc_ref[...]

    return _pcall(body, name=name, out_shape=SDS((m, n), F32), grid=(m // tm, nk),
                  in_specs=[pl.BlockSpec((tk, tm), lambda i, kk: (kk, i)), pl.BlockSpec((tk, n), lambda i, kk: (kk, 0))],
                  out_specs=pl.BlockSpec((tm, n), lambda i, kk: (i, 0)), scratch_shapes=[pltpu.VMEM((tm, n), F32)],
                  dims=("parallel", "arbitrary"))(a, b)


def _rope_lane_consts():
    lane = jnp.arange(128)
    j = lane % HEAD_DIM
    inv_freq = ROPE_THETA ** (-jnp.arange(0, ROT_DIM, 2, dtype=F32) / ROT_DIM)
    freq = jnp.where(j < ROT_DIM, inv_freq[j % (ROT_DIM // 2)], 0.0).astype(F32)
    lo = (j < ROT_DIM // 2).astype(F32)
    hi = ((j >= ROT_DIM // 2) & (j < ROT_DIM)).astype(F32)
    return jnp.stack([freq, lo, hi] + [jnp.zeros((128,), F32)] * 5)


def _to_classes(x, scr, d):
    if d == 1:
        return [x]
    scr[...] = x
    return [scr[pl.ds(c, x.shape[0] // d, stride=d), :] for c in range(d)]


def _from_classes(parts, scr):
    d = len(parts)
    if d == 1:
        return parts[0]
    for c, part in enumerate(parts):
        scr[pl.ds(c, part.shape[0], stride=d), :] = part
    return scr[...]


DILATIONS = tuple(d for _, d in DIL_PAIRS)


def _rope_tables(posf):
    s = posf.shape[0]
    tm = _pick(s, (1024, 512))

    def body(p_ref, c_ref, *rest):
        o_refs, scr = rest[:-1], rest[-1]
        ang = p_ref[...] * c_ref[0:1, :]
        lo = c_ref[1:2, :]
        hi = c_ref[2:3, :]
        cs = jnp.cos(ang)
        sn = jnp.sin(ang)
        for q, t in enumerate((jnp.where(lo + hi > 0.0, cs, 1.0), -sn * lo, sn * hi)):
            for o_ref, d in zip(o_refs, DILATIONS):
                for c, part in enumerate(_to_classes(t, scr, d)):
                    o_ref[:, c * TABW + q * 128:c * TABW + (q + 1) * 128] = part

    return _pcall(body, name="rope_tables", out_shape=tuple(SDS((s // d, d * TABW), F32) for d in DILATIONS),
                  grid=(s // tm,),
                  in_specs=[pl.BlockSpec((tm, 1), lambda i: (i, 0)), pl.BlockSpec((8, 128), lambda i: (0, 0))],
                  out_specs=tuple(pl.BlockSpec((tm // d, d * TABW), lambda i: (i, 0)) for d in DILATIONS),
                  scratch_shapes=[pltpu.VMEM((tm, 128), F32)], dims=("parallel",))(posf, _rope_lane_consts())


def _rope_apply(x, tab, sign):
    w = x.shape[1]
    rep = w // 128
    c = jnp.tile(tab[:, 0:128], (1, rep)) if rep > 1 else tab[:, 0:128]
    a = jnp.tile(tab[:, 128:256], (1, rep)) if rep > 1 else tab[:, 128:256]
    b = jnp.tile(tab[:, 256:384], (1, rep)) if rep > 1 else tab[:, 256:384]
    up = pltpu.roll(x, w - 8, 1)
    dn = pltpu.roll(x, 8, 1)
    if sign > 0:
        return x * c + up * a + dn * b
    return x * c - up * a - dn * b


def _proj_rope(h0b, w_t, tab):
    s = h0b.shape[0]
    tm = _pick(s, (512,))
    tn = 256

    def body(a_ref, w_ref, t_ref, zw_ref, z0_ref, z1_ref, z2_ref, scr):
        z_refs = (z0_ref, z1_ref, z2_ref)
        a = a_ref[...]
        tabv = t_ref[...]
        for c0 in range(0, IN_WIDTH, tn):
            w0 = (c0 + W_IN_QB) % IN_WIDTH
            z = lax.dot_general(a, w_ref[w0:w0 + tn, :], _NT, preferred_element_type=F32)
            for g0 in range(c0, c0 + tn, 128):
                zg = z[:, g0 - c0:g0 - c0 + 128]
                if g0 < Z_VB or Z_QA <= g0 < Z_VA:
                    zg = _rope_apply(zg, tabv, 1)
                if g0 >= Z_QA:
                    zw_ref[:, g0 - Z_QA:g0 - Z_QA + 128] = zg.astype(BF16)
                    continue
                kind, within = divmod(g0, 1536)
                grp, off = divmod(within, 512)
                col = kind * 512 + off
                for c, part in enumerate(_to_classes(zg, scr, DILATIONS[grp])):
                    z_refs[grp][:, c * 1536 + col:c * 1536 + col + 128] = part.astype(BF16)

    return _pcall(body, name="proj_rope",
                  out_shape=(SDS((s, 768), BF16),) + tuple(SDS((s // d, d * 1536), BF16) for d in DILATIONS),
                  grid=(s // tm,),
                  in_specs=[pl.BlockSpec((tm, D_MODEL), lambda i: (i, 0)), pl.BlockSpec((IN_WIDTH, D_MODEL), lambda i: (0, 0)),
                            pl.BlockSpec((tm, TABW), lambda i: (i, 0))],
                  out_specs=(pl.BlockSpec((tm, 768), lambda i: (i, 0)),)
                  + tuple(pl.BlockSpec((tm // d, d * 1536), lambda i: (i, 0)) for d in DILATIONS),
                  scratch_shapes=[pltpu.VMEM((tm, 128), F32)], dims=("parallel",))(h0b, w_t, tab)


def _band_specs(sd, blk, tq, width, per_tok, cb):
    r = tq // blk
    nbk = sd // blk
    prev = pl.BlockSpec((blk, width), lambda c, j: (jnp.maximum(j * r - 1, 0), c * per_tok + cb))
    cur = pl.BlockSpec((tq, width), lambda c, j: (j, c * per_tok + cb))
    nxt = pl.BlockSpec((blk, width), lambda c, j: (jnp.minimum((j + 1) * r, nbk - 1), c * per_tok + cb))
    return [prev, cur, nxt]


def _band_bias(q0, rows, blk, sd):
    shape = (rows, rows + 2 * blk)
    qpos = q0 + lax.broadcasted_iota(jnp.int32, shape, 0)
    kpos = q0 - blk + lax.broadcasted_iota(jnp.int32, shape, 1)
    ok = (jnp.abs(qpos - kpos) <= blk) & (kpos >= 0) & (kpos < sd)
    return jnp.where(ok, 0.0, NEG_INF)


_NT = (((1,), (1,)), ((), ()))
_NN = (((1,), (0,)), ((), ()))
_TN = (((0,), (0,)), ((), ()))


def _banded_fwd(zv, sink, *, d, blk, tq, rc, ptw, qw, kw, qcb, kcb, vcb, pairs, name):
    sd = zv.shape[0]
    tq = min(tq, sd)
    rc = min(rc, tq)
    has_sink = sink is not None
    scale = HEAD_DIM ** -0.5

    def body(q_ref, kp, kc, kn, vp, vc, vn, *rest):
        if has_sink:
            sink_ref, o_ref, lse_ref = rest
        else:
            o_ref, lse_ref = rest
        j = pl.program_id(1)
        q = q_ref[...] * scale
        k = jnp.concatenate([kp[...], kc[...], kn[...]], axis=0)
        v = jnp.concatenate([vp[...], vc[...], vn[...]], axis=0)
        biases = {r0: _band_bias(j * tq + r0, rc, blk, sd) for r0 in range(0, tq, rc)}
        low = lax.broadcasted_iota(jnp.int32, (1, 128), 1) < HEAD_DIM
        for qb, kb, vb, swaps, sinks in pairs:
            qp, kp_, vp_ = q[:, qb:qb + 128], k[:, kb:kb + 128], v[:, vb:vb + 128]
            if any(swaps):
                k_sw = jnp.concatenate([kp_[:, HEAD_DIM:], kp_[:, :HEAD_DIM]], axis=1)
                v_sw = jnp.concatenate([vp_[:, HEAD_DIM:], vp_[:, :HEAD_DIM]], axis=1)
            for r0 in range(0, tq, rc):
                outs, lses = [], []
                for half in range(2):
                    qm = jnp.where(low if half == 0 else ~low, qp[r0:r0 + rc], jnp.zeros((rc, 128), BF16))
                    kk, vv = (k_sw, v_sw) if swaps[half] else (kp_, vp_)
                    kk, vv = kk[r0:r0 + rc + 2 * blk], vv[r0:r0 + rc + 2 * blk]
                    sc = lax.dot_general(qm, kk, _NT, preferred_element_type=F32) + biases[r0]
                    m = jnp.max(sc, axis=-1, keepdims=True)
                    if has_sink:
                        m = jnp.maximum(m, sink_ref[0, sinks[half]])
                    p = jnp.exp(sc - m)
                    den = jnp.sum(p, axis=-1, keepdims=True)
                    if has_sink:
                        den = den + jnp.exp(sink_ref[0, sinks[half]] - m)
                    outs.append(lax.dot_general(p.astype(BF16), vv, _NN, preferred_element_type=F32) / den)
                    lses.append(m + jnp.log(den))
                o_ref[r0:r0 + rc, qb:qb + 128] = jnp.where(low, outs[0], outs[1])
                lse_ref[r0:r0 + rc, qb:qb + 128] = jnp.where(low, lses[0], lses[1])

    in_specs = ([pl.BlockSpec((tq, qw), lambda c, j: (j, c * (ptw // qw) + qcb))]
                + _band_specs(sd, blk, tq, kw, ptw // kw, kcb) + _band_specs(sd, blk, tq, kw, ptw // kw, vcb))
    args = [zv] * 7
    if has_sink:
        in_specs.append(pl.BlockSpec(memory_space=pltpu.SMEM))
        args.append(sink)
    o_spec = pl.BlockSpec((tq, qw), lambda c, j: (j, c))
    return _pcall(body, name=name, out_shape=(SDS((sd, d * qw), F32), SDS((sd, d * qw), F32)), grid=(d, sd // tq),
                  in_specs=in_specs, out_specs=(o_spec, o_spec), dims=("parallel", "parallel"))(*args)


def _banded_bwd(zv, ov, lv, dov, tv, sink, *, d, blk, tq, rc, ptw, qw, kw, qcb, kcb, vcb, pairs, name):
    sd = zv.shape[0]
    tq = min(tq, sd)
    nt = sd // tq
    r = tq // blk
    nbk = sd // blk
    has_sink = sink is not None
    scale = HEAD_DIM ** -0.5
    rc = min(rc, tq)
    kvw = 128 * len({kb for _, kb, _, _, _ in pairs})

    def add_rows(x, y, last):
        if tq == blk:
            return x + y
        if last:
            return jnp.concatenate([x[:tq - blk], x[tq - blk:] + y], axis=0)
        return jnp.concatenate([x[:blk] + y, x[blk:]], axis=0)

    def body(q_ref, kp, kc, kn, vp, vc, vn, o_ref, l_ref, do_ref, t_ref, tlag_ref, *rest):
        if has_sink:
            sink_ref, dq_ref, dkv_ref, dsink_ref, acck, accv, nxtk, nxtv = rest
        else:
            dq_ref, dkv_ref, acck, accv, nxtk, nxtv = rest
        j = pl.program_id(1)

        @pl.when(j == 0)
        def _():
            nxtk[...] = jnp.zeros_like(nxtk)
            nxtv[...] = jnp.zeros_like(nxtv)

        if has_sink:
            @pl.when((pl.program_id(0) == 0) & (j == 0))
            def _():
                dsink_ref[...] = jnp.zeros_like(dsink_ref)

        def emit(dk_rows, dv_rows):
            dkv_ref[...] = jnp.concatenate([_rope_apply(dk_rows, tlag_ref[...], -1), dv_rows], axis=1).astype(BF16)

        @pl.when(j < nt)
        def _():
            q = q_ref[...] * scale
            k3 = jnp.concatenate([kp[...], kc[...], kn[...]], axis=0)
            v3 = jnp.concatenate([vp[...], vc[...], vn[...]], axis=0)
            o_t, l_t, do_t = o_ref[...], l_ref[...], do_ref[...]
            biases = {r0: _band_bias(j * tq + r0, rc, blk, sd) for r0 in range(0, tq, rc)}
            lane = lax.broadcasted_iota(jnp.int32, (1, 128), 1)
            low = lane < HEAD_DIM
            wide = tq + 2 * blk
            cw = rc + 2 * blk

            def place(x, r0):
                parts = ([jnp.zeros((r0, 128), F32)] if r0 else []) + [x]
                if wide - r0 - cw:
                    parts.append(jnp.zeros((wide - r0 - cw, 128), F32))
                return jnp.concatenate(parts, axis=0) if len(parts) > 1 else x

            dqs = []
            wks, wvs = {}, {}
            dsink_row = jnp.zeros((1, 128), F32)
            for qb, kb, vb, swaps, sinks in pairs:
                qp, kp_, vp_ = q[:, qb:qb + 128], k3[:, kb:kb + 128], v3[:, vb:vb + 128]
                if any(swaps):
                    k_sw = jnp.concatenate([kp_[:, HEAD_DIM:], kp_[:, :HEAD_DIM]], axis=1)
                    v_sw = jnp.concatenate([vp_[:, HEAD_DIM:], vp_[:, :HEAD_DIM]], axis=1)
                dop, lp = do_t[:, qb:qb + 128], l_t[:, qb:qb + 128]
                prod = dop * o_t[:, qb:qb + 128]
                dq_rows = []
                for r0 in range(0, tq, rc):
                    rows = slice(r0, r0 + rc)
                    dq_half = []
                    for half in range(2):
                        mine = low if half == 0 else ~low
                        qm = jnp.where(mine, qp[rows], jnp.zeros((rc, 128), BF16))
                        dob = jnp.where(mine, dop[rows], 0.0).astype(BF16)
                        delta = jnp.sum(jnp.where(mine, prod[rows], 0.0), axis=-1, keepdims=True)
                        lse = lp[rows, half * HEAD_DIM:half * HEAD_DIM + 1]
                        kk, vv = (k_sw, v_sw) if swaps[half] else (kp_, vp_)
                        kk, vv = kk[r0:r0 + cw], vv[r0:r0 + cw]
                        sc = lax.dot_general(qm, kk, _NT, preferred_element_type=F32) + biases[r0]
                        p = jnp.exp(sc - lse)
                        dp = lax.dot_general(dob, vv, _NT, preferred_element_type=F32)
                        dsb = (p * (dp - delta)).astype(BF16)
                        dq_half.append(lax.dot_general(dsb, kk, _NN, preferred_element_type=F32))
                        dk = lax.dot_general(dsb, qm, _TN, preferred_element_type=F32)
                        dv = lax.dot_general(p.astype(BF16), dob, _TN, preferred_element_type=F32)
                        if swaps[half]:
                            dk, dv = pltpu.roll(dk, HEAD_DIM, 1), pltpu.roll(dv, HEAD_DIM, 1)
                        wks[kb] = place(dk, r0) if kb not in wks else wks[kb] + place(dk, r0)
                        wvs[vb] = place(dv, r0) if vb not in wvs else wvs[vb] + place(dv, r0)
                        if has_sink:
                            psink = jnp.exp(sink_ref[0, sinks[half]] - lse)
                            dsink_row = dsink_row + jnp.where(lane == sinks[half], -jnp.sum(psink * delta), 0.0)
                    dq_rows.append(jnp.where(low, dq_half[0], dq_half[1]) * scale)
                dqs.append(jnp.concatenate(dq_rows, axis=0) if len(dq_rows) > 1 else dq_rows[0])
            dq_ref[...] = _rope_apply(jnp.concatenate(dqs, axis=1), t_ref[...], -1).astype(BF16)
            wk = jnp.concatenate([wks[b] for b in sorted(wks)], axis=1) if len(wks) > 1 else wks[min(wks)]
            wv = jnp.concatenate([wvs[b] for b in sorted(wvs)], axis=1) if len(wvs) > 1 else wvs[min(wvs)]
            if has_sink:
                dsink_ref[0:1, :] += dsink_row

            @pl.when(j > 0)
            def _():
                emit(add_rows(acck[...], wk[:blk], True), add_rows(accv[...], wv[:blk], True))

            acck[...] = add_rows(wk[blk:blk + tq], nxtk[...], False)
            accv[...] = add_rows(wv[blk:blk + tq], nxtv[...], False)
            nxtk[...] = wk[blk + tq:]
            nxtv[...] = wv[blk + tq:]

        @pl.when(j == nt)
        def _():
            emit(acck[...], accv[...])

    def tile(width, per_tok, cb):
        return pl.BlockSpec((tq, width), lambda c, j: (jnp.minimum(j, nt - 1), c * per_tok + cb))

    def halos(width, per_tok, cb):
        before = pl.BlockSpec((blk, width), lambda c, j: (jnp.maximum(jnp.minimum(j, nt - 1) * r - 1, 0), c * per_tok + cb))
        after = pl.BlockSpec((blk, width),
                             lambda c, j: (jnp.minimum((jnp.minimum(j, nt - 1) + 1) * r, nbk - 1), c * per_tok + cb))
        return [before, tile(width, per_tok, cb), after]

    def lagged(width):
        return pl.BlockSpec((tq, width), lambda c, j: (jnp.maximum(j - 1, 0), c))

    in_specs = ([tile(qw, ptw // qw, qcb)] + halos(kw, ptw // kw, kcb) + halos(kw, ptw // kw, vcb)
                + [tile(qw, 1, 0)] * 3 + [tile(TABW, 1, 0), lagged(TABW)])
    args = [zv] * 7 + [ov, lv, dov, tv, tv]
    out_shape = [SDS((sd, d * qw), BF16), SDS((sd, d * 2 * kvw), BF16)]
    out_specs = [tile(qw, 1, 0), lagged(2 * kvw)]
    if has_sink:
        in_specs.append(pl.BlockSpec(memory_space=pltpu.SMEM))
        args.append(sink)
        out_shape.append(SDS((8, 128), F32))
        out_specs.append(pl.BlockSpec((8, 128), lambda c, j: (0, 0)))
    scratch = [pltpu.VMEM((tq, kvw), F32), pltpu.VMEM((tq, kvw), F32), pltpu.VMEM((blk, kvw), F32),
               pltpu.VMEM((blk, kvw), F32)]
    return _pcall(body, name=name, out_shape=tuple(out_shape), grid=(d, nt + 1), in_specs=in_specs,
                  out_specs=tuple(out_specs), scratch_shapes=scratch, dims=("arbitrary", "arbitrary"))(*args)


_WIN_PAIRS = tuple((128 * p, 0, 128, (False, True) if p < 2 else (True, False), (2 * p, 2 * p + 1)) for p in range(4))
_WIN_CFG = dict(d=1, blk=WIN_HALF, tq=ATTN_TQ, rc=256, ptw=768, qw=512, kw=256, qcb=0, kcb=2, vcb=2, pairs=_WIN_PAIRS)
_DIL_PAIRS = tuple((128 * p, 128 * p, 128 * p, (False, False), (2 * p, 2 * p + 1)) for p in range(4))


def _dil_cfg(gi):
    return dict(d=DILATIONS[gi], blk=DIL_SIDE, tq=ATTN_TQ, rc=128, ptw=1536, qw=512, kw=512, qcb=0, kcb=1, vcb=2,
                pairs=_DIL_PAIRS)


def _view_specs(tm, width):
    return tuple(pl.BlockSpec((tm // d, d * width), lambda i: (i, 0)) for d in DILATIONS)


def _mix_norm_fwd(oa, og_views, lg_views, g_win, g_dil):
    s = oa.shape[0]
    tm = _pick(s, (512,))

    def body(oa_ref, o0, o1, o2, l0, l1, l2, gw_ref, gd_ref, mixed_ref, ob0, ob1, ob2, lb0, lb1, lb2, scr, ob_s):
        o_refs, l_refs, ob_refs, lb_refs = (o0, o1, o2), (l0, l1, l2), (ob0, ob1, ob2), (lb0, lb1, lb2)
        ssq = jnp.zeros((tm, 1), F32)
        for q in range(4):
            os_, ls_ = [], []
            for g, d in enumerate(DILATIONS):
                cols = [slice(c * 512 + q * 128, c * 512 + (q + 1) * 128) for c in range(d)]
                os_.append(_from_classes([o_refs[g][:, cs] for cs in cols], scr))
                ls_.append(_from_classes([l_refs[g][:, cs] for cs in cols], scr))
            mx = jnp.maximum(jnp.maximum(ls_[0], ls_[1]), ls_[2])
            es = [jnp.exp(l - mx) for l in ls_]
            den = es[0] + es[1] + es[2]
            ob = (es[0] / den) * os_[0] + (es[1] / den) * os_[1] + (es[2] / den) * os_[2]
            lb = mx + jnp.log(den)
            ob_s[:, q * 128:(q + 1) * 128] = ob
            ssq = ssq + jnp.sum(ob * ob, axis=-1, keepdims=True)
            for g, d in enumerate(DILATIONS):
                for val, refs in ((ob, ob_refs), (lb, lb_refs)):
                    for c, part in enumerate(_to_classes(val, scr, d)):
                        refs[g][:, c * 512 + q * 128:c * 512 + (q + 1) * 128] = part
        a = oa_ref[...]
        ra = lax.rsqrt(jnp.mean(a * a, axis=-1, keepdims=True) + LN_EPS)
        rb = lax.rsqrt(ssq * (1.0 / 512) + LN_EPS)
        mixed_ref[...] = jnp.concatenate([a * ra * gw_ref[...], ob_s[...] * rb * gd_ref[...]], axis=1).astype(BF16)

    row = pl.BlockSpec((tm, 512), lambda i: (i, 0))
    vec = pl.BlockSpec((1, 512), lambda i: (0, 0))
    views = _view_specs(tm, 512)
    view_shapes = tuple(SDS((s // d, d * 512), F32) for d in DILATIONS)
    res = _pcall(body, name="mix_norm_fwd", out_shape=(SDS((s, 1024), BF16),) + view_shapes * 2, grid=(s // tm,),
                 in_specs=[row, *views, *views, vec, vec],
                 out_specs=(pl.BlockSpec((tm, 1024), lambda i: (i, 0)),) + views * 2,
                 scratch_shapes=[pltpu.VMEM((tm, 128), F32), pltpu.VMEM((tm, 512), F32)],
                 dims=("parallel",))(oa, *og_views, *lg_views, g_win, g_dil)
    return res[0], res[1:4], res[4:7]


def _mix_norm_bwd(oa, ob, dmixed, g_win, g_dil):
    s = oa.shape[0]
    tm = _pick(s, (512,))
    nt = s // tm

    def body(oa_ref, ob_ref, dm_ref, gw_ref, gd_ref, doa_ref, db0, db1, db2, dgw_ref, dgd_ref, acc_w, acc_d, scr):
        i = pl.program_id(0)

        @pl.when(i == 0)
        def _():
            acc_w[...] = jnp.zeros_like(acc_w)
            acc_d[...] = jnp.zeros_like(acc_d)

        dm = dm_ref[...]
        dxs = []
        for x_ref, g_ref, dy, acc in ((oa_ref, gw_ref, dm[:, :512], acc_w), (ob_ref, gd_ref, dm[:, 512:], acc_d)):
            x = x_ref[...]
            r = lax.rsqrt(jnp.mean(x * x, axis=-1, keepdims=True) + LN_EPS)
            dyg = dy * g_ref[...]
            dxs.append(r * dyg - x * (r * r * r) * jnp.mean(dyg * x, axis=-1, keepdims=True))
            acc[...] += jnp.sum((dy * x * r).reshape(tm // 8, 8, 512), axis=0)
        doa_ref[...] = dxs[0]
        for q in range(4):
            dq = dxs[1][:, q * 128:(q + 1) * 128]
            for db_ref, d in zip((db0, db1, db2), DILATIONS):
                for c, part in enumerate(_to_classes(dq, scr, d)):
                    db_ref[:, c * 512 + q * 128:c * 512 + (q + 1) * 128] = part

        @pl.when(i == nt - 1)
        def _():
            dgw_ref[...] = jnp.sum(acc_w[...], axis=0, keepdims=True)
            dgd_ref[...] = jnp.sum(acc_d[...], axis=0, keepdims=True)

    row = pl.BlockSpec((tm, 512), lambda i: (i, 0))
    vec = pl.BlockSpec((1, 512), lambda i: (0, 0))
    views = _view_specs(tm, 512)
    view_shapes = tuple(SDS((s // d, d * 512), F32) for d in DILATIONS)
    res = _pcall(body, name="mix_norm_bwd",
                 out_shape=(SDS((s, 512), F32),) + view_shapes + (SDS((1, 512), F32), SDS((1, 512), F32)),
                 grid=(nt,), in_specs=[row, row, pl.BlockSpec((tm, 1024), lambda i: (i, 0)), vec, vec],
                 out_specs=(row,) + views + (vec, vec),
                 scratch_shapes=[pltpu.VMEM((8, 512), F32), pltpu.VMEM((8, 512), F32), pltpu.VMEM((tm, 128), F32)],
                 dims=("arbitrary",))(oa, ob, dmixed, g_win, g_dil)
    return res[0], res[1:4], res[4], res[5]


def _dz_assemble(dq_views, dkv_views, dqa, dkva):
    s = dqa.shape[0]
    tm = _pick(s, (512,))

    def body(q0, q1, q2, kv0, kv1, kv2, qa_ref, kva_ref, o_ref, scr):
        for g, d in enumerate(DILATIONS):
            for kind, (ref, width, base) in enumerate((((q0, q1, q2)[g], 512, 0), ((kv0, kv1, kv2)[g], 1024, 0),
                                                       ((kv0, kv1, kv2)[g], 1024, 512))):
                for q in range(4):
                    src = base + q * 128
                    dst = W_IN_QB + kind * 1536 + g * 512 + q * 128
                    if d == 1:
                        o_ref[:, dst:dst + 128] = ref[:, src:src + 128]
                    else:
                        parts = [ref[:, c * width + src:c * width + src + 128].astype(F32) for c in range(d)]
                        o_ref[:, dst:dst + 128] = _from_classes(parts, scr).astype(BF16)
        o_ref[:, W_IN_QA:W_IN_QA + 512] = qa_ref[...]
        o_ref[:, W_IN_KA:W_IN_KA + 256] = kva_ref[...]

    return _pcall(body, name="dz_assemble", out_shape=SDS((s, IN_WIDTH), BF16), grid=(s // tm,),
                  in_specs=[*_view_specs(tm, 512), *_view_specs(tm, 1024), pl.BlockSpec((tm, 512), lambda i: (i, 0)),
                            pl.BlockSpec((tm, 256), lambda i: (i, 0))],
                  out_specs=pl.BlockSpec((tm, IN_WIDTH), lambda i: (i, 0)),
                  scratch_shapes=[pltpu.VMEM((tm, 128), F32)], dims=("parallel",))(*dq_views, *dkv_views, dqa, dkva)


def _ln_fwd(a, r, g, b, ca, name):
    s = a.shape[0]
    tm = _pick(s, (512, 256))
    has_r = r is not None

    def body(*refs):
        a_ref = refs[0]
        r_ref = refs[1] if has_r else None
        g_ref, b_ref, o_ref, ob_ref = refs[1 + has_r:]
        u = a_ref[...] if ca == 1.0 else ca * a_ref[...]
        if has_r:
            u = u + r_ref[...]
        mu = jnp.mean(u, axis=-1, keepdims=True)
        xc = u - mu
        var = jnp.mean(xc * xc, axis=-1, keepdims=True)
        y = xc * lax.rsqrt(var + LN_EPS) * g_ref[...] + b_ref[...]
        o_ref[...] = y
        ob_ref[...] = y.astype(BF16)

    row = pl.BlockSpec((tm, D_MODEL), lambda i: (i, 0))
    vec = pl.BlockSpec((1, D_MODEL), lambda i: (0, 0))
    args = [a] + ([r] if has_r else []) + [g, b]
    return _pcall(body, name=name, out_shape=(SDS((s, D_MODEL), F32), SDS((s, D_MODEL), BF16)), grid=(s // tm,),
                  in_specs=[row] * (1 + has_r) + [vec, vec], out_specs=(row, row), dims=("parallel",))(*args)


def _mm_ln_fwd(a, w, resid, g, b, ca, name):
    s, k = a.shape
    tm = _pick(s, (512, 256))

    def body(a_ref, w_ref, res_ref, g_ref, b_ref, r_ref, o_ref, ob_ref):
        rv = lax.dot_general(a_ref[...], w_ref[...], _NN, preferred_element_type=F32)
        r_ref[...] = rv
        u = ca * res_ref[...] + rv
        mu = jnp.mean(u, axis=-1, keepdims=True)
        xc = u - mu
        var = jnp.mean(xc * xc, axis=-1, keepdims=True)
        y = xc * lax.rsqrt(var + LN_EPS) * g_ref[...] + b_ref[...]
        o_ref[...] = y
        ob_ref[...] = y.astype(BF16)

    row = pl.BlockSpec((tm, D_MODEL), lambda i: (i, 0))
    vec = pl.BlockSpec((1, D_MODEL), lambda i: (0, 0))
    return _pcall(body, name=name, out_shape=(SDS((s, D_MODEL), F32), SDS((s, D_MODEL), F32), SDS((s, D_MODEL), BF16)),
                  grid=(s // tm,),
                  in_specs=[pl.BlockSpec((tm, k), lambda i: (i, 0)), pl.BlockSpec((k, D_MODEL), lambda i: (0, 0)), row, vec, vec],
                  out_specs=(row, row, row), dims=("parallel",))(a, w, resid, g, b)


def _mm_ln_loss(x, w, a, target, g, b, ca, name):
    s, k = x.shape
    tm = _pick(s, (256,))
    nt = s // tm

    def body(x_ref, w_ref, a_ref, t_ref, g_ref, b_ref, du_ref, dub_ref, dg_ref, db_ref, loss_ref, acc_g, acc_b, acc_l):
        i = pl.program_id(0)

        @pl.when(i == 0)
        def _():
            acc_g[...] = jnp.zeros_like(acc_g)
            acc_b[...] = jnp.zeros_like(acc_b)
            acc_l[...] = jnp.zeros_like(acc_l)

        u = ca * a_ref[...] + lax.dot_general(x_ref[...], w_ref[...], _NN, preferred_element_type=F32)
        mu = jnp.mean(u, axis=-1, keepdims=True)
        xc = u - mu
        var = jnp.mean(xc * xc, axis=-1, keepdims=True)
        rstd = lax.rsqrt(var + LN_EPS)
        xhat = xc * rstd
        gv = g_ref[...]
        err = (xhat * gv + b_ref[...]) - t_ref[...]
        acc_l[...] += jnp.sum((err * err).reshape(tm // 8, 8, D_MODEL), axis=0)
        dyv = err * (1.0 / D_MODEL)
        dxh = dyv * gv
        du = rstd * (dxh - jnp.mean(dxh, axis=-1, keepdims=True) - xhat * jnp.mean(dxh * xhat, axis=-1, keepdims=True))
        du_ref[...] = du
        dub_ref[...] = du.astype(BF16)
        acc_g[...] += jnp.sum((dyv * xhat).reshape(tm // 8, 8, D_MODEL), axis=0)
        acc_b[...] += jnp.sum(dyv.reshape(tm // 8, 8, D_MODEL), axis=0)

        @pl.when(i == nt - 1)
        def _():
            dg_ref[...] = jnp.sum(acc_g[...], axis=0, keepdims=True)
            db_ref[...] = jnp.sum(acc_b[...], axis=0, keepdims=True)
            tot = jnp.sum(jnp.sum(acc_l[...], axis=0, keepdims=True), axis=1, keepdims=True)
            loss_ref[...] = tot * (0.5 / D_MODEL)

    row = pl.BlockSpec((tm, D_MODEL), lambda i: (i, 0))
    vec = pl.BlockSpec((1, D_MODEL), lambda i: (0, 0))
    return _pcall(body, name=name,
                  out_shape=(SDS((s, D_MODEL), F32), SDS((s, D_MODEL), BF16), SDS((1, D_MODEL), F32), SDS((1, D_MODEL), F32),
                             SDS((1, 1), F32)),
                  grid=(nt,),
                  in_specs=[pl.BlockSpec((tm, k), lambda i: (i, 0)), pl.BlockSpec((k, D_MODEL), lambda i: (0, 0)), row, row,
                            vec, vec],
                  out_specs=(row, row, vec, vec, pl.BlockSpec((1, 1), lambda i: (0, 0))),
                  scratch_shapes=[pltpu.VMEM((8, D_MODEL), F32)] * 3, dims=("arbitrary",))(x, w, a, target, g, b)


def _mm_ln_bwd(pairs, addend, coef, a, r, g, ca, name, after=None):
    s = a.shape[0]
    has_r = r is not None
    has_add = addend is not None
    extra = [] if after is None else [after]
    n_pairs = len(pairs)

    def vmem(tm):
        tot = tm * D_MODEL * (4 * (2 + has_r) + 6)
        for pa, pb, _ in pairs:
            tot += tm * pa.shape[1] * pa.dtype.itemsize + pb.size * pb.dtype.itemsize
        return 2 * tot

    tm = next(c for c in (512, 256, 128) if s % c == 0 and vmem(c) <= MM_VMEM_BUDGET)
    nt = s // tm

    def body(*refs):
        ins = refs[2 * n_pairs:]
        add_ref = ins[0] if has_add else None
        ins = ins[has_add:]
        a_ref = ins[0]
        r_ref = ins[1] if has_r else None
        g_ref = ins[1 + has_r]
        du_ref, dub_ref, dg_ref, db_ref, acc_g, acc_b = ins[2 + has_r + len(extra):]
        i = pl.program_id(0)

        @pl.when(i == 0)
        def _():
            acc_g[...] = jnp.zeros_like(acc_g)
            acc_b[...] = jnp.zeros_like(acc_b)

        dyv = coef * add_ref[...] if has_add else None
        for p, (_, _, tb) in enumerate(pairs):
            part = lax.dot_general(refs[2 * p][...].astype(BF16), refs[2 * p + 1][...], _NT if tb else _NN,
                                   preferred_element_type=F32)
            dyv = part if dyv is None else dyv + part
        u = a_ref[...] if ca == 1.0 else ca * a_ref[...]
        if has_r:
            u = u + r_ref[...]
        mu = jnp.mean(u, axis=-1, keepdims=True)
        xc = u - mu
        var = jnp.mean(xc * xc, axis=-1, keepdims=True)
        rstd = lax.rsqrt(var + LN_EPS)
        xhat = xc * rstd
        dxh = dyv * g_ref[...]
        du = rstd * (dxh - jnp.mean(dxh, axis=-1, keepdims=True) - xhat * jnp.mean(dxh * xhat, axis=-1, keepdims=True))
        du_ref[...] = du
        dub_ref[...] = du.astype(BF16)
        acc_g[...] += jnp.sum((dyv * xhat).reshape(tm // 8, 8, D_MODEL), axis=0)
        acc_b[...] += jnp.sum(dyv.reshape(tm // 8, 8, D_MODEL), axis=0)

        @pl.when(i == nt - 1)
        def _():
            dg_ref[...] = jnp.sum(acc_g[...], axis=0, keepdims=True)
            db_ref[...] = jnp.sum(acc_b[...], axis=0, keepdims=True)

    row = pl.BlockSpec((tm, D_MODEL), lambda i: (i, 0))
    vec = pl.BlockSpec((1, D_MODEL), lambda i: (0, 0))
    in_specs, args = [], []
    for pa, pb, _ in pairs:
        in_specs += [pl.BlockSpec((tm, pa.shape[1]), lambda i: (i, 0)), pl.BlockSpec(pb.shape, lambda i: (0, 0))]
        args += [pa, pb]
    in_specs += [row] * (has_add + 1 + has_r) + [vec] + [pl.BlockSpec((8, 128), lambda i: (0, 0))] * len(extra)
    args += ([addend] if has_add else []) + [a] + ([r] if has_r else []) + [g] + extra
    return _pcall(body, name=name,
                  out_shape=(SDS((s, D_MODEL), F32), SDS((s, D_MODEL), BF16), SDS((1, D_MODEL), F32), SDS((1, D_MODEL), F32)),
                  grid=(nt,), in_specs=in_specs, out_specs=(row, row, vec, vec),
                  scratch_shapes=[pltpu.VMEM((8, D_MODEL), F32), pltpu.VMEM((8, D_MODEL), F32)],
                  dims=("arbitrary",))(*args)


def _xattn_fwd(q, k, v):
    s = q.shape[0]
    tq = _pick(s, (512,))
    scale = X_HEAD_DIM ** -0.5

    def body(q_ref, k_ref, v_ref, o_ref, ob_ref):
        qv, kv, vv = q_ref[...], k_ref[...], v_ref[...]
        outs = []
        for h in range(X_HEADS):
            sl = slice(h * X_HEAD_DIM, (h + 1) * X_HEAD_DIM)
            sc = lax.dot_general(qv[:, sl], kv[:, sl], _NT, preferred_element_type=F32) * scale
            e = jnp.exp(sc - jnp.max(sc, axis=-1, keepdims=True))
            p = e / jnp.sum(e, axis=-1, keepdims=True)
            outs.append(lax.dot_general(p.astype(BF16), vv[:, sl], _NN, preferred_element_type=F32))
        o = jnp.concatenate(outs, axis=1)
        o_ref[...] = o
        ob_ref[...] = o.astype(BF16)

    row = pl.BlockSpec((tq, D_MODEL), lambda i: (i, 0))
    full = pl.BlockSpec((MEM_LEN, D_MODEL), lambda i: (0, 0))
    return _pcall(body, name="xattn_fwd", out_shape=(SDS((s, D_MODEL), F32), SDS((s, D_MODEL), BF16)), grid=(s // tq,),
                  in_specs=[row, full, full], out_specs=(row, row), dims=("parallel",))(q, k, v)


def _xattn_bwd(q, k, v, o, do):
    s = q.shape[0]
    tq = _pick(s, (512,))
    scale = X_HEAD_DIM ** -0.5

    def body(q_ref, k_ref, v_ref, o_ref, do_ref, dq_ref, dk_ref, dv_ref):
        i = pl.program_id(0)

        @pl.when(i == 0)
        def _():
            dk_ref[...] = jnp.zeros_like(dk_ref)
            dv_ref[...] = jnp.zeros_like(dv_ref)

        qv, kv, vv, ov, dov = q_ref[...], k_ref[...], v_ref[...], o_ref[...], do_ref[...]
        dqs, dks, dvs = [], [], []
        for h in range(X_HEADS):
            sl = slice(h * X_HEAD_DIM, (h + 1) * X_HEAD_DIM)
            sc = lax.dot_general(qv[:, sl], kv[:, sl], _NT, preferred_element_type=F32) * scale
            e = jnp.exp(sc - jnp.max(sc, axis=-1, keepdims=True))
            p = e / jnp.sum(e, axis=-1, keepdims=True)
            doh = dov[:, sl]
            dob = doh.astype(BF16)
            delta = jnp.sum(doh * ov[:, sl], axis=-1, keepdims=True)
            dvs.append(lax.dot_general(p.astype(BF16), dob, _TN, preferred_element_type=F32))
            dp = lax.dot_general(dob, vv[:, sl], _NT, preferred_element_type=F32)
            ds = (p * (dp - delta)).astype(BF16)
            dqs.append(lax.dot_general(ds, kv[:, sl], _NN, preferred_element_type=F32) * scale)
            dks.append(lax.dot_general(ds, qv[:, sl], _TN, preferred_element_type=F32) * scale)
        dq_ref[...] = jnp.concatenate(dqs, axis=1).astype(BF16)
        dk_ref[...] += jnp.concatenate(dks, axis=1)
        dv_ref[...] += jnp.concatenate(dvs, axis=1)

    row = pl.BlockSpec((tq, D_MODEL), lambda i: (i, 0))
    full = pl.BlockSpec((MEM_LEN, D_MODEL), lambda i: (0, 0))
    return _pcall(body, name="xattn_bwd",
                  out_shape=(SDS((s, D_MODEL), BF16), SDS((MEM_LEN, D_MODEL), F32), SDS((MEM_LEN, D_MODEL), F32)),
                  grid=(s // tq,), in_specs=[row, full, full, row, row], out_specs=(row, full, full),
                  dims=("arbitrary",))(q, k, v, o, do)


_SQRT_HALF = 0.7071067811865476
_INV_SQRT_2PI = 0.3989422804014327


def _halo_specs(s, tm, width, rows=8):
    nb = s // rows
    r = tm // rows
    prev = pl.BlockSpec((rows, width), lambda i: (jnp.maximum(i * r - 1, 0), 0))
    nxt = pl.BlockSpec((rows, width), lambda i: (jnp.minimum((i + 1) * r, nb - 1), 0))
    return prev, nxt


def _shifted(x, before_row, after_row, i, nt):
    tm = x.shape[0]
    row = lax.broadcasted_iota(jnp.int32, x.shape, 0)
    first = jnp.where(i == 0, 0.0, 1.0) * before_row
    last = jnp.where(i == nt - 1, 0.0, 1.0) * after_row
    xm1 = jnp.where(row == 0, first, pltpu.roll(x, 1, 0))
    xp1 = jnp.where(row == tm - 1, last, pltpu.roll(x, tm - 1, 0))
    return xm1, xp1


BF16_ROWS = 16


def _ffn_fwd(hb, wg_t, wu_t, cw, cb):
    s = hb.shape[0]
    tm = _pick(s, (256,))
    nt = s // tm
    hr = BF16_ROWS

    def body(h_ref, hp_ref, hn_ref, wg_ref, wu_ref, cw_ref, cb_ref, g_ref, up_ref, gc_ref, act_ref):
        i = pl.program_id(0)
        hv = h_ref[...]
        g_ext = lax.dot_general(jnp.concatenate([hp_ref[...], hv, hn_ref[...]], axis=0), wg_ref[...], _NT,
                                preferred_element_type=F32)
        gv = g_ext[hr:hr + tm]
        upv = lax.dot_general(hv, wu_ref[...], _NT, preferred_element_type=F32)
        gm1, gp1 = _shifted(gv, g_ext[hr - 1:hr], g_ext[hr + tm:hr + tm + 1], i, nt)
        gc = gm1 * cw_ref[0:1, :] + gv * cw_ref[1:2, :] + gp1 * cw_ref[2:3, :] + cb_ref[...]
        cdf = 0.5 * (1.0 + lax.erf(gc * _SQRT_HALF))
        g_ref[...] = gv
        up_ref[...] = upv
        gc_ref[...] = gc
        act_ref[...] = (gc * cdf * upv).astype(BF16)

    hrow = pl.BlockSpec((tm, D_MODEL), lambda i: (i, 0))
    prev, nxt = _halo_specs(s, tm, D_MODEL, hr)
    wfull = pl.BlockSpec((D_FF, D_MODEL), lambda i: (0, 0), pipeline_mode=pl.Buffered(1))
    row = pl.BlockSpec((tm, D_FF), lambda i: (i, 0))
    return _pcall(body, name="ffn_fwd", out_shape=(SDS((s, D_FF), F32),) * 3 + (SDS((s, D_FF), BF16),),
                  grid=(nt,), in_specs=[hrow, prev, nxt, wfull, wfull, pl.BlockSpec((8, D_FF), lambda i: (0, 0)),
                                        pl.BlockSpec((1, D_FF), lambda i: (0, 0))],
                  out_specs=(row, row, row, row), dims=("parallel",))(hb, hb, hb, wg_t, wu_t, cw, cb)


def _ffn_bwd(dffb, w_down, g, gc, up, cw):
    s = g.shape[0]
    tm = _pick(s, (256,))
    nt = s // tm
    hr = BF16_ROWS

    def body(df_ref, dfp_ref, dfn_ref, wd_ref, g_ref, gc_ref, gcp_ref, gcn_ref, up_ref, upp_ref, upn_ref, cw_ref,
             dg_ref, dup_ref, dcw_ref, dcb_ref, a0, a1, a2, a3):
        i = pl.program_id(0)

        @pl.when(i == 0)
        def _():
            for a in (a0, a1, a2, a3):
                a[...] = jnp.zeros_like(a)

        def d_conv_out(gc_, up_, da_):
            cdf_ = 0.5 * (1.0 + lax.erf(gc_ * _SQRT_HALF))
            pdf_ = jnp.exp(-0.5 * gc_ * gc_) * _INV_SQRT_2PI
            return da_ * up_ * (cdf_ + gc_ * pdf_), cdf_

        df_ext = jnp.concatenate([dfp_ref[...], df_ref[...], dfn_ref[...]], axis=0)
        tn = 256
        for c0 in range(0, D_FF, tn):
            cs = slice(c0, c0 + tn)
            da_ext = lax.dot_general(df_ext, wd_ref[cs, :], _NT, preferred_element_type=F32)
            cw0, cw1, cw2 = cw_ref[0:1, cs], cw_ref[1:2, cs], cw_ref[2:3, cs]
            gc = gc_ref[:, cs]
            da = da_ext[hr:hr + tm]
            dgc, cdf = d_conv_out(gc, up_ref[:, cs], da)
            dup_ref[:, cs] = (da * (gc * cdf)).astype(BF16)
            dgc_b = jnp.where(i == 0, 0.0, 1.0) * d_conv_out(gcp_ref[7:8, cs], upp_ref[7:8, cs], da_ext[hr - 1:hr])[0]
            dgc_a = jnp.where(i == nt - 1, 0.0, 1.0) * d_conv_out(gcn_ref[0:1, cs], upn_ref[0:1, cs],
                                                                  da_ext[hr + tm:hr + tm + 1])[0]
            row = lax.broadcasted_iota(jnp.int32, dgc.shape, 0)
            dgc_m1 = jnp.where(row == 0, dgc_b, pltpu.roll(dgc, 1, 0))
            dgc_p1 = jnp.where(row == tm - 1, dgc_a, pltpu.roll(dgc, tm - 1, 0))
            dg_ref[:, cs] = (dgc_p1 * cw0 + dgc * cw1 + dgc_m1 * cw2).astype(BF16)

            def fold(t):
                return jnp.sum(t.reshape(tm // 8, 8, tn), axis=0)

            gv = g_ref[:, cs]
            a0[:, cs] += fold(dgc_p1 * gv)
            a1[:, cs] += fold(dgc * gv)
            a2[:, cs] += fold(dgc_m1 * gv)
            a3[:, cs] += fold(dgc)

        @pl.when(i == nt - 1)
        def _():
            dcw_ref[...] = jnp.concatenate(
                [jnp.sum(a[...], axis=0, keepdims=True) for a in (a0, a1, a2)] + [jnp.zeros((5, D_FF), F32)], axis=0)
            dcb_ref[...] = jnp.sum(a3[...], axis=0, keepdims=True)

    row = pl.BlockSpec((tm, D_FF), lambda i: (i, 0))
    prev, nxt = _halo_specs(s, tm, D_FF)
    cw_spec = pl.BlockSpec((8, D_FF), lambda i: (0, 0))
    cb_spec = pl.BlockSpec((1, D_FF), lambda i: (0, 0))
    dprev, dnxt = _halo_specs(s, tm, D_MODEL, hr)
    return _pcall(body, name="ffn_bwd",
                  out_shape=(SDS((s, D_FF), BF16), SDS((s, D_FF), BF16), SDS((8, D_FF), F32), SDS((1, D_FF), F32)),
                  grid=(nt,),
                  in_specs=[pl.BlockSpec((tm, D_MODEL), lambda i: (i, 0)), dprev, dnxt,
                            pl.BlockSpec((D_FF, D_MODEL), lambda i: (0, 0), pipeline_mode=pl.Buffered(1)), row]
                  + [row, prev, nxt] * 2 + [cw_spec],
                  out_specs=(row, row, cw_spec, cb_spec), scratch_shapes=[pltpu.VMEM((8, D_FF), F32)] * 4,
                  dims=("arbitrary",))(dffb, dffb, dffb, w_down, g, gc, gc, gc, up, up, up, cw)


def _adamw(w, g, m, v, name):
    rows, cols = w.shape
    tr = _pick(rows, (256, 128, 64, 32, 16, 8))
    c1 = 1.0 - ADAM_B1 ** ADAM_STEP
    c2 = 1.0 - ADAM_B2 ** ADAM_STEP

    def body(w_ref, g_ref, m_ref, v_ref, d_ref, nm_ref, nv_ref):
        gv = g_ref[...]
        nm = ADAM_B1 * m_ref[...] + (1.0 - ADAM_B1) * gv
        nv = ADAM_B2 * v_ref[...] + (1.0 - ADAM_B2) * (gv * gv)
        d_ref[...] = -ADAM_LR * ((nm / c1) / (jnp.sqrt(nv / c2) + ADAM_EPS) + ADAM_WD * w_ref[...])
        nm_ref[...] = nm
        nv_ref[...] = nv

    blk = pl.BlockSpec((tr, cols), lambda i: (i, 0))
    return _pcall(body, name=name, out_shape=(SDS(w.shape, F32),) * 3, grid=(rows // tr,), in_specs=[blk] * 4,
                  out_specs=(blk,) * 3, dims=("parallel",))(w, g, m, v)


def _adamw_many(ws, gs, ms, vs, name):
    n = len(ws)
    c1 = 1.0 - ADAM_B1 ** ADAM_STEP
    c2 = 1.0 - ADAM_B2 ** ADAM_STEP

    def body(*refs):
        outs = refs[4 * n:]
        for k in range(n):
            gv = refs[n + k][...]
            nm = ADAM_B1 * refs[2 * n + k][...] + (1.0 - ADAM_B1) * gv
            nv = ADAM_B2 * refs[3 * n + k][...] + (1.0 - ADAM_B2) * (gv * gv)
            outs[k][...] = -ADAM_LR * ((nm / c1) / (jnp.sqrt(nv / c2) + ADAM_EPS) + ADAM_WD * refs[k][...])
            outs[n + k][...] = nm
            outs[2 * n + k][...] = nv

    shapes = tuple(SDS(w.shape, F32) for w in ws)
    res = _pcall(body, name=name, out_shape=shapes * 3)(*ws, *gs, *ms, *vs)
    return res[:n], res[n:2 * n], res[2 * n:]


def _all_gather_rows(x_shard, *, name, in_vmem, sum_rows=False, after=None):
    m_per, n = x_shard.shape
    extra = [] if after is None else [after]

    def body(x_ref, *rest):
        out_ref, rest = rest[len(extra)], rest[len(extra) + 1:]
        if sum_rows:
            sum_ref, send_sems, recv_sems, local_sem = rest
        else:
            send_sems, recv_sems, local_sem = rest
        x, y, c = lax.axis_index("x"), lax.axis_index("y"), lax.axis_index("c")
        me, sibling = (x, y, c), (x, y, 1 - c)
        chips = [(1 - x, y), (x, 1 - y), (1 - x, 1 - y)]

        def rows(px, py, pc):
            return out_ref.at[pl.ds((4 * px + 2 * py + pc) * m_per, m_per), :]

        def copy(k, block, to, src=None):
            return pltpu.make_async_remote_copy(
                src_ref=rows(*block) if src is None else src, dst_ref=rows(*block), send_sem=send_sems.at[k],
                recv_sem=recv_sems.at[k], device_id=to, device_id_type=pl.DeviceIdType.MESH)

        mine = pltpu.make_async_copy(x_ref, rows(*me), local_sem)
        mine.start()
        first = [copy(0, me, sibling, src=x_ref)]
        first += [copy(1 + j, me, (*chip, c), src=x_ref) for j, chip in enumerate(chips)]
        for cp in first:
            cp.start()
        passed = [copy(4 + j, (*chip, c), sibling) for j, chip in enumerate(chips)]
        for j, chip in enumerate(chips):
            copy(1 + j, (*chip, c), me).wait_recv()
            passed[j].start()
        copy(0, sibling, me).wait_recv()
        for j, chip in enumerate(chips):
            copy(4 + j, (*chip, 1 - c), me).wait_recv()
        for cp in first + passed:
            cp.wait_send()
        mine.wait()
        if sum_rows:
            acc = out_ref[0:m_per, :]
            for dev in range(1, N_DEV):
                acc = acc + out_ref[dev * m_per:(dev + 1) * m_per, :]
            sum_ref[...] = acc

    space = pltpu.VMEM if in_vmem else pl.ANY
    out_shape = [SDS((N_DEV * m_per, n), x_shard.dtype)]
    out_specs = [pl.BlockSpec(memory_space=space)]
    if sum_rows:
        out_shape.append(SDS((m_per, n), x_shard.dtype))
        out_specs.append(pl.BlockSpec(memory_space=pltpu.VMEM))
    res = _PALLAS_CALL(
        body, name=name, out_shape=tuple(out_shape),
        in_specs=[pl.BlockSpec(memory_space=space)] + [pl.BlockSpec(memory_space=pl.ANY)] * len(extra),
        out_specs=tuple(out_specs),
        scratch_shapes=[pltpu.SemaphoreType.DMA((7,)), pltpu.SemaphoreType.DMA((7,)), pltpu.SemaphoreType.DMA],
        compiler_params=pltpu.CompilerParams(vmem_limit_bytes=VMEM_LIMIT_BYTES),
    )(x_shard, *extra)
    return res if sum_rows else res[0]


_HBM = pl.BlockSpec(memory_space=pltpu.HBM)
_SEM = pl.BlockSpec(memory_space=pltpu.SEMAPHORE)
_SPLIT_PARAMS = dict(has_side_effects=pltpu.SideEffectType.DATAFLOW_SIDE_EFFECTING)


def _split_copies(src_ref, land_ref, send_sems, recv_sems, gather):
    x, y, c = lax.axis_index("x"), lax.axis_index("y"), lax.axis_index("c")
    copies = []
    for k in range(1, N_DEV):
        px = 1 - x if k & 4 else x
        py = 1 - y if k & 2 else y
        pc = 1 - c if k & 1 else c
        if gather:
            rows = src_ref.shape[0]
            src, dst = src_ref, land_ref.at[pl.ds((4 * x + 2 * y + c) * rows, rows), :]
        else:
            src, dst = src_ref.at[4 * px + 2 * py + pc], land_ref.at[k - 1]
        copies.append(pltpu.make_async_remote_copy(
            src_ref=src, dst_ref=dst, send_sem=send_sems.at[k - 1], recv_sem=recv_sems.at[k - 1],
            device_id=(px, py, pc), device_id_type=pl.DeviceIdType.MESH))
    return copies


def _exchange_start(src, land_shape, *, gather, name):
    def body(src_ref, land_ref, send_sems, recv_sems, src_thru, land_thru, token):
        for cp in _split_copies(src_ref, land_ref, send_sems, recv_sems, gather):
            cp.start()
        token[...] = jnp.zeros_like(token)

    land = pltpu.with_memory_space_constraint(lax.empty(land_shape, src.dtype), pltpu.HBM)
    return _PALLAS_CALL(
        body, name=name,
        out_shape=(pltpu.SemaphoreType.DMA((N_DEV - 1,)), pltpu.SemaphoreType.DMA((N_DEV - 1,)),
                   pltpu.HBM(src.shape, src.dtype), pltpu.HBM(land_shape, src.dtype), SDS((8, 128), F32)),
        in_specs=(_HBM, _HBM), out_specs=(_SEM, _SEM, _HBM, _HBM, pl.BlockSpec(memory_space=pltpu.VMEM)),
        input_output_aliases={0: 2, 1: 3}, compiler_params=pltpu.CompilerParams(**_SPLIT_PARAMS),
    )(pltpu.with_memory_space_constraint(src, pltpu.HBM), land)


def _exchange_wait(started, after, *, gather, name):
    send_sems, recv_sems, src_thru, land_thru, _ = started

    def body(src_ref, land_ref, send_sems, recv_sems, after_ref, src_out, land_out):
        copies = _split_copies(src_ref, land_ref, send_sems, recv_sems, gather)
        for cp in copies:
            cp.wait_send()
        for cp in copies:
            cp.wait_recv()

    return _PALLAS_CALL(
        body, name=name,
        out_shape=(pltpu.HBM(src_thru.shape, src_thru.dtype), pltpu.HBM(land_thru.shape, land_thru.dtype)),
        in_specs=(_HBM, _HBM, _SEM, _SEM, pl.BlockSpec(memory_space=pl.ANY)), out_specs=(_HBM, _HBM),
        input_output_aliases={0: 0, 1: 1}, compiler_params=pltpu.CompilerParams(**_SPLIT_PARAMS),
    )(src_thru, land_thru, send_sems, recv_sems, after)


def _sum_parts(own, land, name):
    r, n = own.shape
    tr = _pick(r, (264, 320, 336, 128, 64, 32, 16, 8))

    def body(own_ref, x_ref, o_ref):
        acc = own_ref[...]
        for k in range(N_DEV - 1):
            acc = acc + x_ref[k].astype(F32)
        o_ref[...] = acc

    return _pcall(body, name=name, out_shape=SDS((r, n), F32), grid=(r // tr,),
                  in_specs=[pl.BlockSpec((tr, n), lambda i: (i, 0)), pl.BlockSpec((N_DEV - 1, tr, n), lambda i: (0, i, 0))],
                  out_specs=pl.BlockSpec((tr, n), lambda i: (i, 0)), dims=("parallel",))(own, land)


def _pad_rows(a, rows):
    return jnp.pad(a, ((0, rows - a.shape[0]), (0, 0)))


def kernel(x, mem, positions, ln_in_g, ln_in_b, w_in, attn_sink, g_win, g_dil, w_mix_out, ln1_g, ln1_b, mem_ln_g, mem_ln_b, w_xq, w_xk, w_xv, w_xo, ln2_g, ln2_b, w_gate, w_up, conv_w, conv_b, w_down, ln3_g, ln3_b, loss_target, m_ln_in_g, m_ln_in_b, m_w_in, m_attn_sink, m_g_win, m_g_dil, m_w_mix_out, m_ln1_g, m_ln1_b, m_mem_ln_g, m_mem_ln_b, m_w_xq, m_w_xk, m_w_xv, m_w_xo, m_ln2_g, m_ln2_b, m_w_gate, m_w_up, m_conv_w, m_conv_b, m_w_down, m_ln3_g, m_ln3_b, v_ln_in_g, v_ln_in_b, v_w_in, v_attn_sink, v_g_win, v_g_dil, v_w_mix_out, v_ln1_g, v_ln1_b, v_mem_ln_g, v_mem_ln_b, v_w_xq, v_w_xk, v_w_xv, v_w_xo, v_ln2_g, v_ln2_b, v_w_gate, v_w_up, v_conv_w, v_conv_b, v_w_down, v_ln3_g, v_ln3_b):
    weights = dict(ln_in_g=ln_in_g, ln_in_b=ln_in_b, w_in=w_in, attn_sink=attn_sink, g_win=g_win, g_dil=g_dil, w_mix_out=w_mix_out, ln1_g=ln1_g, ln1_b=ln1_b, mem_ln_g=mem_ln_g, mem_ln_b=mem_ln_b, w_xq=w_xq, w_xk=w_xk, w_xv=w_xv, w_xo=w_xo, ln2_g=ln2_g, ln2_b=ln2_b, w_gate=w_gate, w_up=w_up, conv_w=conv_w, conv_b=conv_b, w_down=w_down, ln3_g=ln3_g, ln3_b=ln3_b)
    mom_m = dict(ln_in_g=m_ln_in_g, ln_in_b=m_ln_in_b, w_in=m_w_in, attn_sink=m_attn_sink, g_win=m_g_win, g_dil=m_g_dil, w_mix_out=m_w_mix_out, ln1_g=m_ln1_g, ln1_b=m_ln1_b, mem_ln_g=m_mem_ln_g, mem_ln_b=m_mem_ln_b, w_xq=m_w_xq, w_xk=m_w_xk, w_xv=m_w_xv, w_xo=m_w_xo, ln2_g=m_ln2_g, ln2_b=m_ln2_b, w_gate=m_w_gate, w_up=m_w_up, conv_w=m_conv_w, conv_b=m_conv_b, w_down=m_w_down, ln3_g=m_ln3_g, ln3_b=m_ln3_b)
    mom_v = dict(ln_in_g=v_ln_in_g, ln_in_b=v_ln_in_b, w_in=v_w_in, attn_sink=v_attn_sink, g_win=v_g_win, g_dil=v_g_dil, w_mix_out=v_w_mix_out, ln1_g=v_ln1_g, ln1_b=v_ln1_b, mem_ln_g=v_mem_ln_g, mem_ln_b=v_mem_ln_b, w_xq=v_w_xq, w_xk=v_w_xk, w_xv=v_w_xv, w_xo=v_w_xo, ln2_g=v_ln2_g, ln2_b=v_ln2_b, w_gate=v_w_gate, w_up=v_w_up, conv_w=v_conv_w, conv_b=v_conv_b, w_down=v_w_down, ln3_g=v_ln3_g, ln3_b=v_ln3_b)
    order = list(weights)
    s = x.shape[1]
    xs = x[0]
    mems = mem[0]
    target = loss_target[0]
    row = lambda a: a.reshape(1, -1)

    shard_rows = dict(w_in=w_in[0].T, w_gate=w_gate[0].T, w_up=w_up[0].T, w_mix_out=w_mix_out[0], w_xq=w_xq[0],
                      w_xk=w_xk[0], w_xv=w_xv[0], w_xo=w_xo[0], w_down=w_down[0])
    me_lin = 4 * lax.axis_index("x") + 2 * lax.axis_index("y") + lax.axis_index("c")
    w_in_full = _all_gather_rows(shard_rows["w_in"].astype(BF16), name="w_in_all_gather", in_vmem=False)
    late_rows = PACK_ROWS[1:]
    late_r = sum(r for _, r in late_rows)
    packed = jnp.concatenate([shard_rows[n].astype(BF16) for n, _ in late_rows], axis=0)
    w_started = _exchange_start(packed, (N_DEV * late_r, D_MODEL), gather=True, name="weight_gather_start")
    cw_pad = jnp.pad(conv_w[0], ((0, 5), (0, 32)))
    cw_all = _all_gather_rows(cw_pad, name="conv_w_all_gather", in_vmem=True).reshape(N_DEV, 8, 384)
    cw_full = jnp.transpose(cw_all[:, :3, :352], (1, 0, 2)).reshape(3, D_FF)
    cw8 = _pad_rows(cw_full, 8)

    tabs = _rope_tables(positions.astype(F32).reshape(s, 1) + w_started[4][0, 0])
    h0, h0b = _ln_fwd(xs, None, row(ln_in_g), row(ln_in_b), 1.0, "ln_in_fwd")
    zw, *zg = _proj_rope(h0b, w_in_full, tabs[0])
    oa, lse_a = _banded_fwd(zw, attn_sink, name="win_attn_fwd", **_WIN_CFG)
    og_views, lg_views = [], []
    for gi in range(3):
        o_g, l_g = _banded_fwd(zg[gi], None, name=f"dil_attn_fwd{gi}", **_dil_cfg(gi))
        og_views.append(o_g)
        lg_views.append(l_g)
    mixed, ob_views, lb_views = _mix_norm_fwd(oa, og_views, lg_views, g_win, g_dil)
    packed_thru, land = _exchange_wait(w_started, mixed, gather=True, name="weight_gather_wait")
    gathered = lax.dynamic_update_slice(land, packed_thru, (me_lin * late_r, 0)).reshape(N_DEV, late_r, D_MODEL)
    full = {}
    off = 0
    for n, r in late_rows:
        full[n] = gathered[:, off:off + r, :].reshape(N_DEV * r, D_MODEL)
        off += r
    mix, h1, h1b = _mm_ln_fwd(mixed, full["w_mix_out"], h0, ln1_g, ln1_b, ALPHA, "mm_mix_out_ln1")
    _, mem_nb = _ln_fwd(mems, None, mem_ln_g, mem_ln_b, 1.0, "mem_ln_fwd")
    kx = _mm(mem_nb, full["w_xk"], trans_b=False, out_dtype=BF16, name="mm_xk")
    vx = _mm(mem_nb, full["w_xv"], trans_b=False, out_dtype=BF16, name="mm_xv")
    qx = _mm(h1b, full["w_xq"], trans_b=False, out_dtype=BF16, name="mm_xq")
    ox, oxb = _xattn_fwd(qx, kx, vx)
    xa, h2, h2b = _mm_ln_fwd(oxb, full["w_xo"], h1, ln2_g, ln2_b, ALPHA, "mm_xo_ln2")
    gate, up, gc, act = _ffn_fwd(h2b, full["w_gate"], full["w_up"], cw8, conv_b)

    du3, du3b, d_ln3_g, d_ln3_b, loss_local = _mm_ln_loss(act, full["w_down"], h2, target, ln3_g, ln3_b, ALPHA,
                                                          "mm_down_ln3_loss")
    dw_down = _mm_tn(act, du3b, name="mm_dw_down")
    dgate, dup, dcw8, d_conv_b = _ffn_bwd(du3b, full["w_down"], gate, gc, up, cw8)
    dw_gate_t = _mm_tn(dgate, h2b, name="mm_dw_gate")
    dw_up_t = _mm_tn(dup, h2b, name="mm_dw_up")
    rows_of = dict(PACK_ROWS)

    own_f32 = {}

    def start_grad_exchange(parts, name, payload=F32):
        gp = jnp.concatenate([g.reshape(N_DEV, rows_of[n], D_MODEL) for n, g in parts], axis=1)
        if payload != F32:
            own_f32[name] = lax.dynamic_index_in_dim(gp, me_lin, axis=0, keepdims=False)
            gp = gp.astype(payload)
        return _exchange_start(gp, (N_DEV - 1,) + gp.shape[1:], gather=False, name=name)

    ffn_parts = (("w_gate", dw_gate_t), ("w_up", dw_up_t), ("w_down", dw_down))
    ffn_started = start_grad_exchange(ffn_parts, "grad_start_ffn")
    du2, du2b, d_ln2_g, d_ln2_b = _mm_ln_bwd(((dgate, full["w_gate"], False), (dup, full["w_up"], False)), du3, ALPHA,
                                             h1, xa, ln2_g + ffn_started[4][0, 0], ALPHA, "mm_dh2_ln2_bwd")
    dox = _mm(du2b, full["w_xo"], trans_b=True, out_dtype=F32, name="mm_d_ox")
    dw_xo = _mm_tn(oxb, du2b, name="mm_dw_xo")
    dqx, dkx, dvx = _xattn_bwd(qx, kx, vx, ox, dox)
    dw_xq = _mm_tn(h1b, dqx, name="mm_dw_xq")
    dw_xk = _mm_tn(mem_nb, dkx, name="mm_dw_xk")
    dw_xv = _mm_tn(mem_nb, dvx, name="mm_dw_xv")
    _, _, d_mem_ln_g, d_mem_ln_b = _mm_ln_bwd(((dkx, full["w_xk"], True), (dvx, full["w_xv"], True)), None, 1.0, mems,
                                              None, mem_ln_g, 1.0, "mm_dmem_ln_bwd")
    du1, du1b, d_ln1_g, d_ln1_b = _mm_ln_bwd(((dqx, full["w_xq"], True),), du2, ALPHA, h0, mix, ln1_g, ALPHA,
                                             "mm_dh1_ln1_bwd")
    dmixed = _mm(du1b, full["w_mix_out"], trans_b=True, out_dtype=F32, name="mm_d_mixed")
    dw_mix_out = _mm_tn(mixed, du1b, name="mm_dw_mix_out")
    attn_parts = (("w_mix_out", dw_mix_out), ("w_xq", dw_xq), ("w_xk", dw_xk), ("w_xv", dw_xv), ("w_xo", dw_xo))
    attn_started = start_grad_exchange(attn_parts, "grad_start_attn")
    doa, dob_views, d_g_win, d_g_dil = _mix_norm_bwd(oa, ob_views[0], dmixed, g_win + attn_started[4][0, 0], g_dil)
    dqa, dkva, dsink8 = _banded_bwd(zw, oa, lse_a, doa, tabs[0], attn_sink, name="win_attn_bwd", **_WIN_CFG)
    dq_views, dkv_views = [], []
    for gi in range(3):
        dq_g, dkv_g = _banded_bwd(zg[gi], ob_views[gi], lb_views[gi], dob_views[gi], tabs[gi], None,
                                  name=f"dil_attn_bwd{gi}", **{**_dil_cfg(gi), "rc": ATTN_TQ})
        dq_views.append(dq_g)
        dkv_views.append(dkv_g)
    dz = _dz_assemble(dq_views, dkv_views, dqa, dkva)
    dw_in_t = _mm_tn(dz, h0b, name="mm_dw_in")
    in_parts = (("w_in", dw_in_t),)
    in_started = start_grad_exchange(in_parts, "grad_start_in", payload=BF16)
    dx, _, d_ln_in_g, d_ln_in_b = _mm_ln_bwd(((dz, w_in_full, False),), du1, ALPHA, xs, None, row(ln_in_g), 1.0,
                                             "mm_dh0_ln_in_bwd", after=in_started[4])

    grads, delta, new_m, new_v = {}, {}, {}, {}
    after = dx
    for parts, started, tag in ((ffn_parts, ffn_started, "ffn"), (attn_parts, attn_started, "attn"),
                                (in_parts, in_started, "in")):
        gp_thru, land = _exchange_wait(started, after, gather=False, name=f"grad_wait_{tag}")
        own = own_f32.get(f"grad_start_{tag}")
        if own is None:
            own = lax.dynamic_index_in_dim(gp_thru, me_lin, axis=0, keepdims=False)
        gsum = _sum_parts(own, land, f"grad_sum_{tag}")
        off = 0
        for n, _ in parts:
            blk = gsum[off:off + rows_of[n]]
            off += rows_of[n]
            grads[n] = (blk.T if n in ("w_in", "w_gate", "w_up") else blk)[None]
            shp = weights[n].shape
            d_, m_, v_ = _adamw(weights[n].reshape(shp[1:]), grads[n].reshape(shp[1:]), mom_m[n].reshape(shp[1:]),
                                mom_v[n].reshape(shp[1:]), f"adamw_{n}")
            delta[n], new_m[n], new_v[n] = d_.reshape(shp), m_.reshape(shp), v_.reshape(shp)
            after = d_

    small = jnp.concatenate([
        d_ln_in_g, d_ln_in_b, d_ln1_g, d_ln1_b, d_mem_ln_g, d_mem_ln_b, d_ln2_g, d_ln2_b, d_ln3_g, d_ln3_b,
        jnp.concatenate([d_g_win, d_g_dil], axis=1),
        jnp.pad(d_conv_b, ((0, 0), (0, 3072 - D_FF))).reshape(3, 1024),
        jnp.pad(dsink8[0:1, :], ((0, 0), (0, 1024 - 128))),
        jnp.pad(dcw8[0:3], ((0, 0), (0, 3072 - D_FF))).reshape(9, 1024),
    ], axis=0)
    _, ssum = _all_gather_rows(small, name="small_grad_all_reduce", in_vmem=True, sum_rows=True, after=after)
    names10 = ["ln_in_g", "ln_in_b", "ln1_g", "ln1_b", "mem_ln_g", "mem_ln_b", "ln2_g", "ln2_b", "ln3_g", "ln3_b"]
    small_g = {n: ssum[i:i + 1] for i, n in enumerate(names10)}
    small_g["g_win"] = ssum[10:11, :512]
    small_g["g_dil"] = ssum[10:11, 512:]
    small_g["conv_b"] = ssum[11:14].reshape(1, 3072)[:, :D_FF]
    small_g["attn_sink"] = ssum[14:15, :8]
    small_g["conv_w"] = lax.dynamic_slice_in_dim(ssum[15:24].reshape(3, 3072)[:, :D_FF], me_lin * 352, 352, axis=1)

    small_names = [n for n in order if n not in rows_of]
    two_d = lambda a: a.reshape(-1, a.shape[-1])
    d_s, m_s, v_s = _adamw_many([two_d(weights[n]) for n in small_names], [small_g[n] for n in small_names],
                                [two_d(mom_m[n]) for n in small_names], [two_d(mom_v[n]) for n in small_names],
                                "adamw_small")
    for k, n in enumerate(small_names):
        shp = weights[n].shape
        grads[n], delta[n], new_m[n], new_v[n] = (t.reshape(shp) for t in (small_g[n], d_s[k], m_s[k], v_s[k]))

    loss = lax.psum(loss_local[0, 0], MESH_AXES)
    return (loss, dx[None], *[grads[n] for n in order], *[delta[n] for n in order], *[new_m[n] for n in order],
            *[new_v[n] for n in order])
```

```python
import functools
import math

import jax
import jax.numpy as jnp
from jax import lax
from jax.experimental import pallas as pl
from jax.experimental.pallas import tpu as pltpu

F32 = jnp.float32
BF16 = jnp.bfloat16
SDS = jax.ShapeDtypeStruct
_PALLAS_CALL = pl.pallas_call

D_MODEL = 1024
HEAD_DIM = 64
WIN_HALF = 128
DIL_PAIRS = ((128, 1), (512, 4), (2048, 16))
DIL_SIDE = 64
ROT_DIM = 16
ROPE_THETA = 500000.0
MEM_LEN = 256
X_HEADS = 4
X_HEAD_DIM = 256
D_FF = 2816
IN_WIDTH = 5376
Z_QB, Z_KB, Z_VB, Z_QA, Z_KA, Z_VA = 0, 1536, 3072, 4608, 5120, 5248
W_IN_QA, W_IN_KA, W_IN_QB = 0, 512, 768
ALPHA = (2.0) ** 0.25
LN_EPS = 1e-5
NEG_INF = -1e30
ADAM_LR, ADAM_B1, ADAM_B2, ADAM_EPS, ADAM_WD, ADAM_STEP = 0.001, 0.9, 0.999, 1e-08, 0.01, 10
N_DEV = 8
MESH_AXES = ("x", "y", "c")
VMEM_LIMIT_BYTES = 52 * 1024 * 1024
ATTN_TQ = 256
WHOLE_CLASS_BYTES = 4 * 1024 * 1024
TABW = 384

PACK_ROWS = (("w_in", 672), ("w_gate", 352), ("w_up", 352), ("w_mix_out", 128), ("w_xq", 128), ("w_xk", 128),
             ("w_xv", 128), ("w_xo", 128), ("w_down", 352))
SMALL_ROWS = 24


def _pick(n, cands):
    for c in cands:
        if n % c == 0:
            return c
    return n


def _pcall(body, *, name, out_shape, grid=None, in_specs=None, out_specs=None, scratch_shapes=(), dims=None,
           aliases=None):
    kw = {}
    if grid is not None:
        kw["grid"] = grid
    if in_specs is not None:
        kw["in_specs"] = in_specs
    if out_specs is not None:
        kw["out_specs"] = out_specs
    if aliases:
        kw["input_output_aliases"] = aliases
    return _PALLAS_CALL(
        body, name=name, out_shape=out_shape, scratch_shapes=list(scratch_shapes),
        compiler_params=pltpu.CompilerParams(dimension_semantics=dims, vmem_limit_bytes=VMEM_LIMIT_BYTES), **kw)


MM_VMEM_BUDGET = 40 * 1024 * 1024


def _mm(a, b, *, trans_b, out_dtype, name, addends=(), coefs=(), after=None, more=()):
    pairs = ((a, b, trans_b),) + tuple(more)
    m = a.shape[0]
    n = b.shape[0] if trans_b else b.shape[1]
    n_add = len(addends)
    extra = [] if after is None else [after]
    out_bytes = jnp.dtype(out_dtype).itemsize

    def vmem(tm, tn):
        tot = tm * tn * (out_bytes + 4 * n_add)
        for pa, pb, _ in pairs:
            tot += tm * pa.shape[1] * pa.dtype.itemsize + pa.shape[1] * tn * pb.dtype.itemsize
        return 2 * tot

    tm, tn = next(((cm, cn) for cn in (n, 1408, 1024, 512, 256, 128) if n % cn == 0
                   for cm in (1024, 512, 256, 128) if m % cm == 0 and vmem(cm, cn) <= MM_VMEM_BUDGET))
    n_pairs = len(pairs)

    def body(*refs):
        o_ref = refs[2 * n_pairs + n_add + len(extra)]
        acc = None
        for p, (_, _, tb) in enumerate(pairs):
            dn = _NT if tb else _NN
            part = lax.dot_general(refs[2 * p][...].astype(BF16), refs[2 * p + 1][...].astype(BF16), dn,
                                   preferred_element_type=F32)
            acc = part if acc is None else acc + part
        for r_ref, c in zip(refs[2 * n_pairs:2 * n_pairs + n_add], coefs):
            acc = acc + (r_ref[...] if c == 1.0 else c * r_ref[...])
        o_ref[...] = acc.astype(out_dtype)

    in_specs, args = [], []
    for pa, pb, tb in pairs:
        k = pa.shape[1]
        in_specs.append(pl.BlockSpec((tm, k), lambda j, i: (i, 0)))
        in_specs.append(pl.BlockSpec((tn, k), lambda j, i: (j, 0)) if tb else pl.BlockSpec((k, tn), lambda j, i: (0, j)))
        args += [pa, pb]
    in_specs += [pl.BlockSpec((tm, tn), lambda j, i: (i, j)) for _ in addends]
    in_specs += [pl.BlockSpec((8, 128), lambda j, i: (0, 0)) for _ in extra]
    return _pcall(body, name=name, out_shape=SDS((m, n), out_dtype), grid=(n // tn, m // tm), in_specs=in_specs,
                  out_specs=pl.BlockSpec((tm, tn), lambda j, i: (i, j)),
                  dims=("parallel", "parallel"))(*args, *addends, *extra)


def _mm_tn(a, b, *, name):
    s, m = a.shape
    n = b.shape[1]
    tm = _pick(m, (768, 1408, 1024, 512, 256, 128))
    tk = _pick(s, (1024, 512, 256))
    nk = s // tk

    def body(a_ref, b_ref, o_ref, acc_ref):
        kk = pl.program_id(1)

        @pl.when(kk == 0)
        def _():
            acc_ref[...] = jnp.zeros_like(acc_ref)

        acc_ref[...] += lax.dot_general(a_ref[...].astype(BF16), b_ref[...].astype(BF16), (((0,), (0,)), ((), ())),
                                        preferred_element_type=F32)

        @pl.when(kk == nk - 1)
        def _():
            o_ref[...] = acc_ref[...]

    return _pcall(body, name=name, out_shape=SDS((m, n), F32), grid=(m // tm, nk),
                  in_specs=[pl.BlockSpec((tk, tm), lambda i, kk: (kk, i)), pl.BlockSpec((tk, n), lambda i, kk: (kk, 0))],
                  out_specs=pl.BlockSpec((tm, n), lambda i, kk: (i, 0)), scratch_shapes=[pltpu.VMEM((tm, n), F32)],
                  dims=("parallel", "arbitrary"))(a, b)


def _rope_lane_consts():
    lane = jnp.arange(128)
    j = lane % HEAD_DIM
    inv_freq = ROPE_THETA ** (-jnp.arange(0, ROT_DIM, 2, dtype=F32) / ROT_DIM)
    freq = jnp.where(j < ROT_DIM, inv_freq[j % (ROT_DIM // 2)], 0.0).astype(F32)
    lo = (j < ROT_DIM // 2).astype(F32)
    hi = ((j >= ROT_DIM // 2) & (j < ROT_DIM)).astype(F32)
    return jnp.stack([freq, lo, hi] + [jnp.zeros((128,), F32)] * 5)


def _to_classes(x, scr, d):
    if d == 1:
        return [x]
    scr[...] = x
    return [scr[pl.ds(c, x.shape[0] // d, stride=d), :] for c in range(d)]


def _from_classes(parts, scr):
    d = len(parts)
    if d == 1:
        return parts[0]
    for c, part in enumerate(parts):
        scr[pl.ds(c, part.shape[0], stride=d), :] = part
    return scr[...]


DILATIONS = tuple(d for _, d in DIL_PAIRS)


def _rope_tables(posf):
    s = posf.shape[0]
    tm = _pick(s, (1024, 512))

    def body(p_ref, c_ref, *rest):
        o_refs, scr = rest[:-1], rest[-1]
        ang = p_ref[...] * c_ref[0:1, :]
        lo = c_ref[1:2, :]
        hi = c_ref[2:3, :]
        cs = jnp.cos(ang)
        sn = jnp.sin(ang)
        for q, t in enumerate((jnp.where(lo + hi > 0.0, cs, 1.0), -sn * lo, sn * hi)):
            for o_ref, d in zip(o_refs, DILATIONS):
                for c, part in enumerate(_to_classes(t, scr, d)):
                    o_ref[:, c * TABW + q * 128:c * TABW + (q + 1) * 128] = part

    return _pcall(body, name="rope_tables", out_shape=tuple(SDS((s // d, d * TABW), F32) for d in DILATIONS),
                  grid=(s // tm,),
                  in_specs=[pl.BlockSpec((tm, 1), lambda i: (i, 0)), pl.BlockSpec((8, 128), lambda i: (0, 0))],
                  out_specs=tuple(pl.BlockSpec((tm // d, d * TABW), lambda i: (i, 0)) for d in DILATIONS),
                  scratch_shapes=[pltpu.VMEM((tm, 128), F32)], dims=("parallel",))(posf, _rope_lane_consts())


def _rope_apply(x, tab, sign):
    w = x.shape[1]
    rep = w // 128
    c = jnp.tile(tab[:, 0:128], (1, rep)) if rep > 1 else tab[:, 0:128]
    a = jnp.tile(tab[:, 128:256], (1, rep)) if rep > 1 else tab[:, 128:256]
    b = jnp.tile(tab[:, 256:384], (1, rep)) if rep > 1 else tab[:, 256:384]
    up = pltpu.roll(x, w - 8, 1)
    dn = pltpu.roll(x, 8, 1)
    if sign > 0:
        return x * c + up * a + dn * b
    return x * c - up * a - dn * b


def _proj_rope(h0b, w_t, tab):
    s = h0b.shape[0]
    tm = _pick(s, (512,))
    tn = 256

    def body(a_ref, w_ref, t_ref, zw_ref, z0_ref, z1_ref, z2_ref, scr):
        z_refs = (z0_ref, z1_ref, z2_ref)
        a = a_ref[...]
        tabv = t_ref[...]
        for c0 in range(0, IN_WIDTH, tn):
            w0 = (c0 + W_IN_QB) % IN_WIDTH
            z = lax.dot_general(a, w_ref[w0:w0 + tn, :], _NT, preferred_element_type=F32)
            for g0 in range(c0, c0 + tn, 128):
                zg = z[:, g0 - c0:g0 - c0 + 128]
                if g0 < Z_VB or Z_QA <= g0 < Z_VA:
                    zg = _rope_apply(zg, tabv, 1)
                if g0 >= Z_QA:
                    zw_ref[:, g0 - Z_QA:g0 - Z_QA + 128] = zg.astype(BF16)
                    continue
                kind, within = divmod(g0, 1536)
                grp, off = divmod(within, 512)
                col = kind * 512 + off
                for c, part in enumerate(_to_classes(zg, scr, DILATIONS[grp])):
                    z_refs[grp][:, c * 1536 + col:c * 1536 + col + 128] = part.astype(BF16)

    return _pcall(body, name="proj_rope",
                  out_shape=(SDS((s, 768), BF16),) + tuple(SDS((s // d, d * 1536), BF16) for d in DILATIONS),
                  grid=(s // tm,),
                  in_specs=[pl.BlockSpec((tm, D_MODEL), lambda i: (i, 0)), pl.BlockSpec((IN_WIDTH, D_MODEL), lambda i: (0, 0)),
                            pl.BlockSpec((tm, TABW), lambda i: (i, 0))],
                  out_specs=(pl.BlockSpec((tm, 768), lambda i: (i, 0)),)
                  + tuple(pl.BlockSpec((tm // d, d * 1536), lambda i: (i, 0)) for d in DILATIONS),
                  scratch_shapes=[pltpu.VMEM((tm, 128), F32)], dims=("parallel",))(h0b, w_t, tab)


def _band_specs(sd, blk, tq, width, per_tok, cb):
    r = tq // blk
    nbk = sd // blk
    prev = pl.BlockSpec((blk, width), lambda c, j: (jnp.maximum(j * r - 1, 0), c * per_tok + cb))
    cur = pl.BlockSpec((tq, width), lambda c, j: (j, c * per_tok + cb))
    nxt = pl.BlockSpec((blk, width), lambda c, j: (jnp.minimum((j + 1) * r, nbk - 1), c * per_tok + cb))
    return [prev, cur, nxt]


def _band_bias(q0, rows, blk, sd):
    shape = (rows, rows + 2 * blk)
    qpos = q0 + lax.broadcasted_iota(jnp.int32, shape, 0)
    kpos = q0 - blk + lax.broadcasted_iota(jnp.int32, shape, 1)
    ok = (jnp.abs(qpos - kpos) <= blk) & (kpos >= 0) & (kpos < sd)
    return jnp.where(ok, 0.0, NEG_INF)


_NT = (((1,), (1,)), ((), ()))
_NN = (((1,), (0,)), ((), ()))
_TN = (((0,), (0,)), ((), ()))


def _banded_fwd(zv, sink, *, d, blk, tq, rc, ptw, qw, kw, qcb, kcb, vcb, pairs, name):
    sd = zv.shape[0]
    tq = min(tq, sd)
    rc = min(rc, tq)
    has_sink = sink is not None
    scale = HEAD_DIM ** -0.5

    def body(q_ref, kp, kc, kn, vp, vc, vn, *rest):
        if has_sink:
            sink_ref, o_ref, lse_ref = rest
        else:
            o_ref, lse_ref = rest
        j = pl.program_id(1)
        q = q_ref[...] * scale
        k = jnp.concatenate([kp[...], kc[...], kn[...]], axis=0)
        v = jnp.concatenate([vp[...], vc[...], vn[...]], axis=0)
        biases = {r0: _band_bias(j * tq + r0, rc, blk, sd) for r0 in range(0, tq, rc)}
        low = lax.broadcasted_iota(jnp.int32, (1, 128), 1) < HEAD_DIM
        for qb, kb, vb, swaps, sinks in pairs:
            qp, kp_, vp_ = q[:, qb:qb + 128], k[:, kb:kb + 128], v[:, vb:vb + 128]
            if any(swaps):
                k_sw = jnp.concatenate([kp_[:, HEAD_DIM:], kp_[:, :HEAD_DIM]], axis=1)
                v_sw = jnp.concatenate([vp_[:, HEAD_DIM:], vp_[:, :HEAD_DIM]], axis=1)
            for r0 in range(0, tq, rc):
                outs, lses = [], []
                for half in range(2):
                    qm = jnp.where(low if half == 0 else ~low, qp[r0:r0 + rc], jnp.zeros((rc, 128), BF16))
                    kk, vv = (k_sw, v_sw) if swaps[half] else (kp_, vp_)
                    kk, vv = kk[r0:r0 + rc + 2 * blk], vv[r0:r0 + rc + 2 * blk]
                    sc = lax.dot_general(qm, kk, _NT, preferred_element_type=F32) + biases[r0]
                    m = jnp.max(sc, axis=-1, keepdims=True)
                    if has_sink:
                        m = jnp.maximum(m, sink_ref[0, sinks[half]])
                    p = jnp.exp(sc - m)
                    den = jnp.sum(p, axis=-1, keepdims=True)
                    if has_sink:
                        den = den + jnp.exp(sink_ref[0, sinks[half]] - m)
                    outs.append(lax.dot_general(p.astype(BF16), vv, _NN, preferred_element_type=F32) / den)
                    lses.append(m + jnp.log(den))
                o_ref[r0:r0 + rc, qb:qb + 128] = jnp.where(low, outs[0], outs[1])
                lse_ref[r0:r0 + rc, qb:qb + 128] = jnp.where(low, lses[0], lses[1])

    in_specs = ([pl.BlockSpec((tq, qw), lambda c, j: (j, c * (ptw // qw) + qcb))]
                + _band_specs(sd, blk, tq, kw, ptw // kw, kcb) + _band_specs(sd, blk, tq, kw, ptw // kw, vcb))
    args = [zv] * 7
    if has_sink:
        in_specs.append(pl.BlockSpec(memory_space=pltpu.SMEM))
        args.append(sink)
    o_spec = pl.BlockSpec((tq, qw), lambda c, j: (j, c))
    return _pcall(body, name=name, out_shape=(SDS((sd, d * qw), F32), SDS((sd, d * qw), F32)), grid=(d, sd // tq),
                  in_specs=in_specs, out_specs=(o_spec, o_spec), dims=("parallel", "parallel"))(*args)


def _banded_bwd(zv, ov, lv, dov, tv, sink, *, d, blk, tq, rc, ptw, qw, kw, qcb, kcb, vcb, pairs, name):
    sd = zv.shape[0]
    tq = min(tq, sd)
    nt = sd // tq
    r = tq // blk
    nbk = sd // blk
    has_sink = sink is not None
    scale = HEAD_DIM ** -0.5
    rc = min(rc, tq)
    kvw = 128 * len({kb for _, kb, _, _, _ in pairs})
    whole_class = sd * 2 * kvw * 2 <= WHOLE_CLASS_BYTES

    def add_rows(x, y, last):
        if tq == blk:
            return x + y
        if last:
            return jnp.concatenate([x[:tq - blk], x[tq - blk:] + y], axis=0)
        return jnp.concatenate([x[:blk] + y, x[blk:]], axis=0)

    def body(q_ref, kp, kc, kn, vp, vc, vn, o_ref, l_ref, do_ref, t_ref, tlag_ref, *rest):
        if has_sink:
            sink_ref, dq_ref, dkv_ref, dsink_ref, acck, accv, nxtk, nxtv = rest
        else:
            dq_ref, dkv_ref, acck, accv, nxtk, nxtv = rest
        j = pl.program_id(1)

        @pl.when(j == 0)
        def _():
            nxtk[...] = jnp.zeros_like(nxtk)
            nxtv[...] = jnp.zeros_like(nxtv)

        if has_sink:
            @pl.when((pl.program_id(0) == 0) & (j == 0))
            def _():
                dsink_ref[...] = jnp.zeros_like(dsink_ref)

        def emit(dk_rows, dv_rows, tabv, tile_idx):
            val = jnp.concatenate([_rope_apply(dk_rows, tabv, -1), dv_rows], axis=1).astype(BF16)
            if whole_class:
                dkv_ref[pl.ds(pl.multiple_of(tile_idx * tq, tq), tq), :] = val
            else:
                dkv_ref[...] = val

        @pl.when(j < nt)
        def _():
            q = q_ref[...] * scale
            k3 = jnp.concatenate([kp[...], kc[...], kn[...]], axis=0)
            v3 = jnp.concatenate([vp[...], vc[...], vn[...]], axis=0)
            o_t, l_t, do_t = o_ref[...], l_ref[...], do_ref[...]
            biases = {r0: _band_bias(j * tq + r0, rc, blk, sd) for r0 in range(0, tq, rc)}
            lane = lax.broadcasted_iota(jnp.int32, (1, 128), 1)
            low = lane < HEAD_DIM
            wide = tq + 2 * blk
            cw = rc + 2 * blk

            def place(x, r0):
                parts = ([jnp.zeros((r0, 128), F32)] if r0 else []) + [x]
                if wide - r0 - cw:
                    parts.append(jnp.zeros((wide - r0 - cw, 128), F32))
                return jnp.concatenate(parts, axis=0) if len(parts) > 1 else x

            dqs = []
            wks, wvs = {}, {}
            dsink_row = jnp.zeros((1, 128), F32)
            for qb, kb, vb, swaps, sinks in pairs:
                qp, kp_, vp_ = q[:, qb:qb + 128], k3[:, kb:kb + 128], v3[:, vb:vb + 128]
                if any(swaps):
                    k_sw = jnp.concatenate([kp_[:, HEAD_DIM:], kp_[:, :HEAD_DIM]], axis=1)
                    v_sw = jnp.concatenate([vp_[:, HEAD_DIM:], vp_[:, :HEAD_DIM]], axis=1)
                dop, lp = do_t[:, qb:qb + 128], l_t[:, qb:qb + 128]
                prod = dop * o_t[:, qb:qb + 128]
                dq_rows = []
                for r0 in range(0, tq, rc):
                    rows = slice(r0, r0 + rc)
                    dq_half = []
                    for half in range(2):
                        mine = low if half == 0 else ~low
                        qm = jnp.where(mine, qp[rows], jnp.zeros((rc, 128), BF16))
                        dob = jnp.where(mine, dop[rows], 0.0).astype(BF16)
                        delta = jnp.sum(jnp.where(mine, prod[rows], 0.0), axis=-1, keepdims=True)
                        lse = lp[rows, half * HEAD_DIM:half * HEAD_DIM + 1]
                        kk, vv = (k_sw, v_sw) if swaps[half] else (kp_, vp_)
                        kk, vv = kk[r0:r0 + cw], vv[r0:r0 + cw]
                        sc = lax.dot_general(qm, kk, _NT, preferred_element_type=F32) + biases[r0]
                        p = jnp.exp(sc - lse)
                        dp = lax.dot_general(dob, vv, _NT, preferred_element_type=F32)
                        dsb = (p * (dp - delta)).astype(BF16)
                        dq_half.append(lax.dot_general(dsb, kk, _NN, preferred_element_type=F32))
                        dk = lax.dot_general(dsb, qm, _TN, preferred_element_type=F32)
                        dv = lax.dot_general(p.astype(BF16), dob, _TN, preferred_element_type=F32)
                        if swaps[half]:
                            dk, dv = pltpu.roll(dk, HEAD_DIM, 1), pltpu.roll(dv, HEAD_DIM, 1)
                        wks[kb] = place(dk, r0) if kb not in wks else wks[kb] + place(dk, r0)
                        wvs[vb] = place(dv, r0) if vb not in wvs else wvs[vb] + place(dv, r0)
                        if has_sink:
                            psink = jnp.exp(sink_ref[0, sinks[half]] - lse)
                            dsink_row = dsink_row + jnp.where(lane == sinks[half], -jnp.sum(psink * delta), 0.0)
                    dq_rows.append(jnp.where(low, dq_half[0], dq_half[1]) * scale)
                dqs.append(jnp.concatenate(dq_rows, axis=0) if len(dq_rows) > 1 else dq_rows[0])
            dq_ref[...] = _rope_apply(jnp.concatenate(dqs, axis=1), t_ref[...], -1).astype(BF16)
            wk = jnp.concatenate([wks[b] for b in sorted(wks)], axis=1) if len(wks) > 1 else wks[min(wks)]
            wv = jnp.concatenate([wvs[b] for b in sorted(wvs)], axis=1) if len(wvs) > 1 else wvs[min(wvs)]
            if has_sink:
                dsink_ref[0:1, :] += dsink_row

            @pl.when(j > 0)
            def _():
                emit(add_rows(acck[...], wk[:blk], True), add_rows(accv[...], wv[:blk], True), tlag_ref[...], j - 1)

            acck[...] = add_rows(wk[blk:blk + tq], nxtk[...], False)
            accv[...] = add_rows(wv[blk:blk + tq], nxtv[...], False)
            nxtk[...] = wk[blk + tq:]
            nxtv[...] = wv[blk + tq:]
            if whole_class:
                @pl.when(j == nt - 1)
                def _():
                    emit(acck[...], accv[...], t_ref[...], j)

        if not whole_class:
            @pl.when(j == nt)
            def _():
                emit(acck[...], accv[...], tlag_ref[...], j - 1)

    def tile(width, per_tok, cb):
        return pl.BlockSpec((tq, width), lambda c, j: (jnp.minimum(j, nt - 1), c * per_tok + cb))

    def halos(width, per_tok, cb):
        before = pl.BlockSpec((blk, width), lambda c, j: (jnp.maximum(jnp.minimum(j, nt - 1) * r - 1, 0), c * per_tok + cb))
        after = pl.BlockSpec((blk, width),
                             lambda c, j: (jnp.minimum((jnp.minimum(j, nt - 1) + 1) * r, nbk - 1), c * per_tok + cb))
        return [before, tile(width, per_tok, cb), after]

    def lagged(width):
        return pl.BlockSpec((tq, width), lambda c, j: (jnp.maximum(j - 1, 0), c))

    in_specs = ([tile(qw, ptw // qw, qcb)] + halos(kw, ptw // kw, kcb) + halos(kw, ptw // kw, vcb)
                + [tile(qw, 1, 0)] * 3 + [tile(TABW, 1, 0), lagged(TABW)])
    args = [zv] * 7 + [ov, lv, dov, tv, tv]
    out_shape = [SDS((sd, d * qw), BF16), SDS((sd, d * 2 * kvw), BF16)]
    out_specs = [tile(qw, 1, 0),
                 pl.BlockSpec((sd, 2 * kvw), lambda c, j: (0, c)) if whole_class else lagged(2 * kvw)]
    if has_sink:
        in_specs.append(pl.BlockSpec(memory_space=pltpu.SMEM))
        args.append(sink)
        out_shape.append(SDS((8, 128), F32))
        out_specs.append(pl.BlockSpec((8, 128), lambda c, j: (0, 0)))
    scratch = [pltpu.VMEM((tq, kvw), F32), pltpu.VMEM((tq, kvw), F32), pltpu.VMEM((blk, kvw), F32),
               pltpu.VMEM((blk, kvw), F32)]
    return _pcall(body, name=name, out_shape=tuple(out_shape), grid=(d, nt + (0 if whole_class else 1)), in_specs=in_specs,
                  out_specs=tuple(out_specs), scratch_shapes=scratch, dims=("arbitrary", "arbitrary"))(*args)


_WIN_PAIRS = tuple((128 * p, 0, 128, (False, True) if p < 2 else (True, False), (2 * p, 2 * p + 1)) for p in range(4))
_WIN_CFG = dict(d=1, blk=WIN_HALF, tq=ATTN_TQ, rc=256, ptw=768, qw=512, kw=256, qcb=0, kcb=2, vcb=2, pairs=_WIN_PAIRS)
_DIL_PAIRS = tuple((128 * p, 128 * p, 128 * p, (False, False), (2 * p, 2 * p + 1)) for p in range(4))


def _dil_cfg(gi):
    return dict(d=DILATIONS[gi], blk=DIL_SIDE, tq=ATTN_TQ, rc=128, ptw=1536, qw=512, kw=512, qcb=0, kcb=1, vcb=2,
                pairs=_DIL_PAIRS)


def _view_specs(tm, width):
    return tuple(pl.BlockSpec((tm // d, d * width), lambda i: (i, 0)) for d in DILATIONS)


def _mix_norm_fwd(oa, og_views, lg_views, g_win, g_dil):
    s = oa.shape[0]
    tm = _pick(s, (512,))

    def body(oa_ref, o0, o1, o2, l0, l1, l2, gw_ref, gd_ref, mixed_ref, ob0, ob1, ob2, lb0, lb1, lb2, scr, ob_s):
        o_refs, l_refs, ob_refs, lb_refs = (o0, o1, o2), (l0, l1, l2), (ob0, ob1, ob2), (lb0, lb1, lb2)
        ssq = jnp.zeros((tm, 1), F32)
        for q in range(4):
            os_, ls_ = [], []
            for g, d in enumerate(DILATIONS):
                cols = [slice(c * 512 + q * 128, c * 512 + (q + 1) * 128) for c in range(d)]
                os_.append(_from_classes([o_refs[g][:, cs] for cs in cols], scr))
                ls_.append(_from_classes([l_refs[g][:, cs] for cs in cols], scr))
            mx = jnp.maximum(jnp.maximum(ls_[0], ls_[1]), ls_[2])
            es = [jnp.exp(l - mx) for l in ls_]
            den = es[0] + es[1] + es[2]
            ob = (es[0] / den) * os_[0] + (es[1] / den) * os_[1] + (es[2] / den) * os_[2]
            lb = mx + jnp.log(den)
            ob_s[:, q * 128:(q + 1) * 128] = ob
            ssq = ssq + jnp.sum(ob * ob, axis=-1, keepdims=True)
            for g, d in enumerate(DILATIONS):
                for val, refs in ((ob, ob_refs), (lb, lb_refs)):
                    for c, part in enumerate(_to_classes(val, scr, d)):
                        refs[g][:, c * 512 + q * 128:c * 512 + (q + 1) * 128] = part
        a = oa_ref[...]
        ra = lax.rsqrt(jnp.mean(a * a, axis=-1, keepdims=True) + LN_EPS)
        rb = lax.rsqrt(ssq * (1.0 / 512) + LN_EPS)
        mixed_ref[...] = jnp.concatenate([a * ra * gw_ref[...], ob_s[...] * rb * gd_ref[...]], axis=1).astype(BF16)

    row = pl.BlockSpec((tm, 512), lambda i: (i, 0))
    vec = pl.BlockSpec((1, 512), lambda i: (0, 0))
    views = _view_specs(tm, 512)
    view_shapes = tuple(SDS((s // d, d * 512), F32) for d in DILATIONS)
    res = _pcall(body, name="mix_norm_fwd", out_shape=(SDS((s, 1024), BF16),) + view_shapes * 2, grid=(s // tm,),
                 in_specs=[row, *views, *views, vec, vec],
                 out_specs=(pl.BlockSpec((tm, 1024), lambda i: (i, 0)),) + views * 2,
                 scratch_shapes=[pltpu.VMEM((tm, 128), F32), pltpu.VMEM((tm, 512), F32)],
                 dims=("parallel",))(oa, *og_views, *lg_views, g_win, g_dil)
    return res[0], res[1:4], res[4:7]


def _mix_norm_bwd(oa, ob, dmixed, g_win, g_dil):
    s = oa.shape[0]
    tm = _pick(s, (512,))
    nt = s // tm

    def body(oa_ref, ob_ref, dm_ref, gw_ref, gd_ref, doa_ref, db0, db1, db2, dgw_ref, dgd_ref, acc_w, acc_d, scr):
        i = pl.program_id(0)

        @pl.when(i == 0)
        def _():
            acc_w[...] = jnp.zeros_like(acc_w)
            acc_d[...] = jnp.zeros_like(acc_d)

        dm = dm_ref[...]
        dxs = []
        for x_ref, g_ref, dy, acc in ((oa_ref, gw_ref, dm[:, :512], acc_w), (ob_ref, gd_ref, dm[:, 512:], acc_d)):
            x = x_ref[...]
            r = lax.rsqrt(jnp.mean(x * x, axis=-1, keepdims=True) + LN_EPS)
            dyg = dy * g_ref[...]
            dxs.append(r * dyg - x * (r * r * r) * jnp.mean(dyg * x, axis=-1, keepdims=True))
            acc[...] += jnp.sum((dy * x * r).reshape(tm // 8, 8, 512), axis=0)
        doa_ref[...] = dxs[0]
        for q in range(4):
            dq = dxs[1][:, q * 128:(q + 1) * 128]
            for db_ref, d in zip((db0, db1, db2), DILATIONS):
                for c, part in enumerate(_to_classes(dq, scr, d)):
                    db_ref[:, c * 512 + q * 128:c * 512 + (q + 1) * 128] = part

        @pl.when(i == nt - 1)
        def _():
            dgw_ref[...] = jnp.sum(acc_w[...], axis=0, keepdims=True)
            dgd_ref[...] = jnp.sum(acc_d[...], axis=0, keepdims=True)

    row = pl.BlockSpec((tm, 512), lambda i: (i, 0))
    vec = pl.BlockSpec((1, 512), lambda i: (0, 0))
    views = _view_specs(tm, 512)
    view_shapes = tuple(SDS((s // d, d * 512), F32) for d in DILATIONS)
    res = _pcall(body, name="mix_norm_bwd",
                 out_shape=(SDS((s, 512), F32),) + view_shapes + (SDS((1, 512), F32), SDS((1, 512), F32)),
                 grid=(nt,), in_specs=[row, row, pl.BlockSpec((tm, 1024), lambda i: (i, 0)), vec, vec],
                 out_specs=(row,) + views + (vec, vec),
                 scratch_shapes=[pltpu.VMEM((8, 512), F32), pltpu.VMEM((8, 512), F32), pltpu.VMEM((tm, 128), F32)],
                 dims=("arbitrary",))(oa, ob, dmixed, g_win, g_dil)
    return res[0], res[1:4], res[4], res[5]


def _dz_assemble(dq_views, dkv_views, dqa, dkva):
    s = dqa.shape[0]
    tm = _pick(s, (512,))

    def body(q0, q1, q2, kv0, kv1, kv2, qa_ref, kva_ref, o_ref, scr):
        for g, d in enumerate(DILATIONS):
            for kind, (ref, width, base) in enumerate((((q0, q1, q2)[g], 512, 0), ((kv0, kv1, kv2)[g], 1024, 0),
                                                       ((kv0, kv1, kv2)[g], 1024, 512))):
                for q in range(4):
                    src = base + q * 128
                    dst = W_IN_QB + kind * 1536 + g * 512 + q * 128
                    if d == 1:
                        o_ref[:, dst:dst + 128] = ref[:, src:src + 128]
                    else:
                        parts = [ref[:, c * width + src:c * width + src + 128].astype(F32) for c in range(d)]
                        o_ref[:, dst:dst + 128] = _from_classes(parts, scr).astype(BF16)
        o_ref[:, W_IN_QA:W_IN_QA + 512] = qa_ref[...]
        o_ref[:, W_IN_KA:W_IN_KA + 256] = kva_ref[...]

    return _pcall(body, name="dz_assemble", out_shape=SDS((s, IN_WIDTH), BF16), grid=(s // tm,),
                  in_specs=[*_view_specs(tm, 512), *_view_specs(tm, 1024), pl.BlockSpec((tm, 512), lambda i: (i, 0)),
                            pl.BlockSpec((tm, 256), lambda i: (i, 0))],
                  out_specs=pl.BlockSpec((tm, IN_WIDTH), lambda i: (i, 0)),
                  scratch_shapes=[pltpu.VMEM((tm, 128), F32)], dims=("parallel",))(*dq_views, *dkv_views, dqa, dkva)


def _ln_fwd(a, r, g, b, ca, name):
    s = a.shape[0]
    tm = _pick(s, (512, 256))
    has_r = r is not None

    def body(*refs):
        a_ref = refs[0]
        r_ref = refs[1] if has_r else None
        g_ref, b_ref, o_ref, ob_ref = refs[1 + has_r:]
        u = a_ref[...] if ca == 1.0 else ca * a_ref[...]
        if has_r:
            u = u + r_ref[...]
        mu = jnp.mean(u, axis=-1, keepdims=True)
        xc = u - mu
        var = jnp.mean(xc * xc, axis=-1, keepdims=True)
        y = xc * lax.rsqrt(var + LN_EPS) * g_ref[...] + b_ref[...]
        o_ref[...] = y
        ob_ref[...] = y.astype(BF16)

    row = pl.BlockSpec((tm, D_MODEL), lambda i: (i, 0))
    vec = pl.BlockSpec((1, D_MODEL), lambda i: (0, 0))
    args = [a] + ([r] if has_r else []) + [g, b]
    return _pcall(body, name=name, out_shape=(SDS((s, D_MODEL), F32), SDS((s, D_MODEL), BF16)), grid=(s // tm,),
                  in_specs=[row] * (1 + has_r) + [vec, vec], out_specs=(row, row), dims=("parallel",))(*args)


def _mm_ln_fwd(a, w, resid, g, b, ca, name):
    s, k = a.shape
    tm = _pick(s, (512, 256))

    def body(a_ref, w_ref, res_ref, g_ref, b_ref, r_ref, o_ref, ob_ref):
        rv = lax.dot_general(a_ref[...], w_ref[...], _NN, preferred_element_type=F32)
        r_ref[...] = rv
        u = ca * res_ref[...] + rv
        mu = jnp.mean(u, axis=-1, keepdims=True)
        xc = u - mu
        var = jnp.mean(xc * xc, axis=-1, keepdims=True)
        y = xc * lax.rsqrt(var + LN_EPS) * g_ref[...] + b_ref[...]
        o_ref[...] = y
        ob_ref[...] = y.astype(BF16)

    row = pl.BlockSpec((tm, D_MODEL), lambda i: (i, 0))
    vec = pl.BlockSpec((1, D_MODEL), lambda i: (0, 0))
    return _pcall(body, name=name, out_shape=(SDS((s, D_MODEL), F32), SDS((s, D_MODEL), F32), SDS((s, D_MODEL), BF16)),
                  grid=(s // tm,),
                  in_specs=[pl.BlockSpec((tm, k), lambda i: (i, 0)), pl.BlockSpec((k, D_MODEL), lambda i: (0, 0)), row, vec, vec],
                  out_specs=(row, row, row), dims=("parallel",))(a, w, resid, g, b)


def _mm_ln_loss(x, w, a, target, g, b, ca, name):
    s, k = x.shape
    tm = _pick(s, (256,))
    nt = s // tm

    def body(x_ref, w_ref, a_ref, t_ref, g_ref, b_ref, du_ref, dub_ref, dg_ref, db_ref, loss_ref, acc_g, acc_b, acc_l):
        i = pl.program_id(0)

        @pl.when(i == 0)
        def _():
            acc_g[...] = jnp.zeros_like(acc_g)
            acc_b[...] = jnp.zeros_like(acc_b)
            acc_l[...] = jnp.zeros_like(acc_l)

        u = ca * a_ref[...] + lax.dot_general(x_ref[...], w_ref[...], _NN, preferred_element_type=F32)
        mu = jnp.mean(u, axis=-1, keepdims=True)
        xc = u - mu
        var = jnp.mean(xc * xc, axis=-1, keepdims=True)
        rstd = lax.rsqrt(var + LN_EPS)
        xhat = xc * rstd
        gv = g_ref[...]
        err = (xhat * gv + b_ref[...]) - t_ref[...]
        acc_l[...] += jnp.sum((err * err).reshape(tm // 8, 8, D_MODEL), axis=0)
        dyv = err * (1.0 / D_MODEL)
        dxh = dyv * gv
        du = rstd * (dxh - jnp.mean(dxh, axis=-1, keepdims=True) - xhat * jnp.mean(dxh * xhat, axis=-1, keepdims=True))
        du_ref[...] = du
        dub_ref[...] = du.astype(BF16)
        acc_g[...] += jnp.sum((dyv * xhat).reshape(tm // 8, 8, D_MODEL), axis=0)
        acc_b[...] += jnp.sum(dyv.reshape(tm // 8, 8, D_MODEL), axis=0)

        @pl.when(i == nt - 1)
        def _():
            dg_ref[...] = jnp.sum(acc_g[...], axis=0, keepdims=True)
            db_ref[...] = jnp.sum(acc_b[...], axis=0, keepdims=True)
            tot = jnp.sum(jnp.sum(acc_l[...], axis=0, keepdims=True), axis=1, keepdims=True)
            loss_ref[...] = tot * (0.5 / D_MODEL)

    row = pl.BlockSpec((tm, D_MODEL), lambda i: (i, 0))
    vec = pl.BlockSpec((1, D_MODEL), lambda i: (0, 0))
    return _pcall(body, name=name,
                  out_shape=(SDS((s, D_MODEL), F32), SDS((s, D_MODEL), BF16), SDS((1, D_MODEL), F32), SDS((1, D_MODEL), F32),
                             SDS((1, 1), F32)),
                  grid=(nt,),
                  in_specs=[pl.BlockSpec((tm, k), lambda i: (i, 0)), pl.BlockSpec((k, D_MODEL), lambda i: (0, 0)), row, row,
                            vec, vec],
                  out_specs=(row, row, vec, vec, pl.BlockSpec((1, 1), lambda i: (0, 0))),
                  scratch_shapes=[pltpu.VMEM((8, D_MODEL), F32)] * 3, dims=("arbitrary",))(x, w, a, target, g, b)


def _mm_ln_bwd(pairs, addend, coef, a, r, g, ca, name, after=None):
    s = a.shape[0]
    has_r = r is not None
    has_add = addend is not None
    extra = [] if after is None else [after]
    n_pairs = len(pairs)

    def vmem(tm):
        tot = tm * D_MODEL * (4 * (2 + has_r) + 6)
        for pa, pb, _ in pairs:
            tot += tm * pa.shape[1] * pa.dtype.itemsize + pb.size * pb.dtype.itemsize
        return 2 * tot

    tm = next(c for c in (512, 256, 128) if s % c == 0 and vmem(c) <= MM_VMEM_BUDGET)
    nt = s // tm

    def body(*refs):
        ins = refs[2 * n_pairs:]
        add_ref = ins[0] if has_add else None
        ins = ins[has_add:]
        a_ref = ins[0]
        r_ref = ins[1] if has_r else None
        g_ref = ins[1 + has_r]
        du_ref, dub_ref, dg_ref, db_ref, acc_g, acc_b = ins[2 + has_r + len(extra):]
        i = pl.program_id(0)

        @pl.when(i == 0)
        def _():
            acc_g[...] = jnp.zeros_like(acc_g)
            acc_b[...] = jnp.zeros_like(acc_b)

        dyv = coef * add_ref[...] if has_add else None
        for p, (_, _, tb) in enumerate(pairs):
            part = lax.dot_general(refs[2 * p][...].astype(BF16), refs[2 * p + 1][...], _NT if tb else _NN,
                                   preferred_element_type=F32)
            dyv = part if dyv is None else dyv + part
        u = a_ref[...] if ca == 1.0 else ca * a_ref[...]
        if has_r:
            u = u + r_ref[...]
        mu = jnp.mean(u, axis=-1, keepdims=True)
        xc = u - mu
        var = jnp.mean(xc * xc, axis=-1, keepdims=True)
        rstd = lax.rsqrt(var + LN_EPS)
        xhat = xc * rstd
        dxh = dyv * g_ref[...]
        du = rstd * (dxh - jnp.mean(dxh, axis=-1, keepdims=True) - xhat * jnp.mean(dxh * xhat, axis=-1, keepdims=True))
        du_ref[...] = du
        dub_ref[...] = du.astype(BF16)
        acc_g[...] += jnp.sum((dyv * xhat).reshape(tm // 8, 8, D_MODEL), axis=0)
        acc_b[...] += jnp.sum(dyv.reshape(tm // 8, 8, D_MODEL), axis=0)

        @pl.when(i == nt - 1)
        def _():
            dg_ref[...] = jnp.sum(acc_g[...], axis=0, keepdims=True)
            db_ref[...] = jnp.sum(acc_b[...], axis=0, keepdims=True)

    row = pl.BlockSpec((tm, D_MODEL), lambda i: (i, 0))
    vec = pl.BlockSpec((1, D_MODEL), lambda i: (0, 0))
    in_specs, args = [], []
    for pa, pb, _ in pairs:
        in_specs += [pl.BlockSpec((tm, pa.shape[1]), lambda i: (i, 0)), pl.BlockSpec(pb.shape, lambda i: (0, 0))]
        args += [pa, pb]
    in_specs += [row] * (has_add + 1 + has_r) + [vec] + [pl.BlockSpec((8, 128), lambda i: (0, 0))] * len(extra)
    args += ([addend] if has_add else []) + [a] + ([r] if has_r else []) + [g] + extra
    return _pcall(body, name=name,
                  out_shape=(SDS((s, D_MODEL), F32), SDS((s, D_MODEL), BF16), SDS((1, D_MODEL), F32), SDS((1, D_MODEL), F32)),
                  grid=(nt,), in_specs=in_specs, out_specs=(row, row, vec, vec),
                  scratch_shapes=[pltpu.VMEM((8, D_MODEL), F32), pltpu.VMEM((8, D_MODEL), F32)],
                  dims=("arbitrary",))(*args)


def _xattn_fwd(q, k, v):
    s = q.shape[0]
    tq = _pick(s, (512,))
    scale = X_HEAD_DIM ** -0.5

    def body(q_ref, k_ref, v_ref, o_ref, ob_ref):
        qv, kv, vv = q_ref[...], k_ref[...], v_ref[...]
        outs = []
        for h in range(X_HEADS):
            sl = slice(h * X_HEAD_DIM, (h + 1) * X_HEAD_DIM)
            sc = lax.dot_general(qv[:, sl], kv[:, sl], _NT, preferred_element_type=F32) * scale
            e = jnp.exp(sc - jnp.max(sc, axis=-1, keepdims=True))
            p = e / jnp.sum(e, axis=-1, keepdims=True)
            outs.append(lax.dot_general(p.astype(BF16), vv[:, sl], _NN, preferred_element_type=F32))
        o = jnp.concatenate(outs, axis=1)
        o_ref[...] = o
        ob_ref[...] = o.astype(BF16)

    row = pl.BlockSpec((tq, D_MODEL), lambda i: (i, 0))
    full = pl.BlockSpec((MEM_LEN, D_MODEL), lambda i: (0, 0))
    return _pcall(body, name="xattn_fwd", out_shape=(SDS((s, D_MODEL), F32), SDS((s, D_MODEL), BF16)), grid=(s // tq,),
                  in_specs=[row, full, full], out_specs=(row, row), dims=("parallel",))(q, k, v)


def _xattn_bwd(q, k, v, o, do):
    s = q.shape[0]
    tq = _pick(s, (512,))
    scale = X_HEAD_DIM ** -0.5

    def body(q_ref, k_ref, v_ref, o_ref, do_ref, dq_ref, dk_ref, dv_ref):
        i = pl.program_id(0)

        @pl.when(i == 0)
        def _():
            dk_ref[...] = jnp.zeros_like(dk_ref)
            dv_ref[...] = jnp.zeros_like(dv_ref)

        qv, kv, vv, ov, dov = q_ref[...], k_ref[...], v_ref[...], o_ref[...], do_ref[...]
        dqs, dks, dvs = [], [], []
        for h in range(X_HEADS):
            sl = slice(h * X_HEAD_DIM, (h + 1) * X_HEAD_DIM)
            sc = lax.dot_general(qv[:, sl], kv[:, sl], _NT, preferred_element_type=F32) * scale
            e = jnp.exp(sc - jnp.max(sc, axis=-1, keepdims=True))
            p = e / jnp.sum(e, axis=-1, keepdims=True)
            doh = dov[:, sl]
            dob = doh.astype(BF16)
            delta = jnp.sum(doh * ov[:, sl], axis=-1, keepdims=True)
            dvs.append(lax.dot_general(p.astype(BF16), dob, _TN, preferred_element_type=F32))
            dp = lax.dot_general(dob, vv[:, sl], _NT, preferred_element_type=F32)
            ds = (p * (dp - delta)).astype(BF16)
            dqs.append(lax.dot_general(ds, kv[:, sl], _NN, preferred_element_type=F32) * scale)
            dks.append(lax.dot_general(ds, qv[:, sl], _TN, preferred_element_type=F32) * scale)
        dq_ref[...] = jnp.concatenate(dqs, axis=1).astype(BF16)
        dk_ref[...] += jnp.concatenate(dks, axis=1)
        dv_ref[...] += jnp.concatenate(dvs, axis=1)

    row = pl.BlockSpec((tq, D_MODEL), lambda i: (i, 0))
    full = pl.BlockSpec((MEM_LEN, D_MODEL), lambda i: (0, 0))
    return _pcall(body, name="xattn_bwd",
                  out_shape=(SDS((s, D_MODEL), BF16), SDS((MEM_LEN, D_MODEL), F32), SDS((MEM_LEN, D_MODEL), F32)),
                  grid=(s // tq,), in_specs=[row, full, full, row, row], out_specs=(row, full, full),
                  dims=("arbitrary",))(q, k, v, o, do)


_SQRT_HALF = 0.7071067811865476
_INV_SQRT_2PI = 0.3989422804014327


def _halo_specs(s, tm, width, rows=8):
    nb = s // rows
    r = tm // rows
    prev = pl.BlockSpec((rows, width), lambda i: (jnp.maximum(i * r - 1, 0), 0))
    nxt = pl.BlockSpec((rows, width), lambda i: (jnp.minimum((i + 1) * r, nb - 1), 0))
    return prev, nxt


def _shifted(x, before_row, after_row, i, nt):
    tm = x.shape[0]
    row = lax.broadcasted_iota(jnp.int32, x.shape, 0)
    first = jnp.where(i == 0, 0.0, 1.0) * before_row
    last = jnp.where(i == nt - 1, 0.0, 1.0) * after_row
    xm1 = jnp.where(row == 0, first, pltpu.roll(x, 1, 0))
    xp1 = jnp.where(row == tm - 1, last, pltpu.roll(x, tm - 1, 0))
    return xm1, xp1


BF16_ROWS = 16


def _ffn_fwd(hb, wg_t, wu_t, cw, cb):
    s = hb.shape[0]
    tm = _pick(s, (256,))
    nt = s // tm
    hr = BF16_ROWS

    def body(h_ref, hp_ref, hn_ref, wg_ref, wu_ref, cw_ref, cb_ref, g_ref, up_ref, gc_ref, act_ref):
        i = pl.program_id(0)
        hv = h_ref[...]
        g_ext = lax.dot_general(jnp.concatenate([hp_ref[...], hv, hn_ref[...]], axis=0), wg_ref[...], _NT,
                                preferred_element_type=F32)
        gv = g_ext[hr:hr + tm]
        upv = lax.dot_general(hv, wu_ref[...], _NT, preferred_element_type=F32)
        gm1, gp1 = _shifted(gv, g_ext[hr - 1:hr], g_ext[hr + tm:hr + tm + 1], i, nt)
        gc = gm1 * cw_ref[0:1, :] + gv * cw_ref[1:2, :] + gp1 * cw_ref[2:3, :] + cb_ref[...]
        cdf = 0.5 * (1.0 + lax.erf(gc * _SQRT_HALF))
        g_ref[...] = gv
        up_ref[...] = upv
        gc_ref[...] = gc
        act_ref[...] = (gc * cdf * upv).astype(BF16)

    hrow = pl.BlockSpec((tm, D_MODEL), lambda i: (i, 0))
    prev, nxt = _halo_specs(s, tm, D_MODEL, hr)
    wfull = pl.BlockSpec((D_FF, D_MODEL), lambda i: (0, 0), pipeline_mode=pl.Buffered(1))
    row = pl.BlockSpec((tm, D_FF), lambda i: (i, 0))
    return _pcall(body, name="ffn_fwd", out_shape=(SDS((s, D_FF), F32),) * 3 + (SDS((s, D_FF), BF16),),
                  grid=(nt,), in_specs=[hrow, prev, nxt, wfull, wfull, pl.BlockSpec((8, D_FF), lambda i: (0, 0)),
                                        pl.BlockSpec((1, D_FF), lambda i: (0, 0))],
                  out_specs=(row, row, row, row), dims=("parallel",))(hb, hb, hb, wg_t, wu_t, cw, cb)


def _ffn_bwd(dffb, w_down, g, gc, up, cw):
    s = g.shape[0]
    tm = _pick(s, (256,))
    nt = s // tm
    hr = BF16_ROWS

    def body(df_ref, dfp_ref, dfn_ref, wd_ref, g_ref, gc_ref, gcp_ref, gcn_ref, up_ref, upp_ref, upn_ref, cw_ref,
             dg_ref, dup_ref, dcw_ref, dcb_ref, a0, a1, a2, a3):
        i = pl.program_id(0)

        @pl.when(i == 0)
        def _():
            for a in (a0, a1, a2, a3):
                a[...] = jnp.zeros_like(a)

        def d_conv_out(gc_, up_, da_):
            cdf_ = 0.5 * (1.0 + lax.erf(gc_ * _SQRT_HALF))
            pdf_ = jnp.exp(-0.5 * gc_ * gc_) * _INV_SQRT_2PI
            return da_ * up_ * (cdf_ + gc_ * pdf_), cdf_

        df_ext = jnp.concatenate([dfp_ref[...], df_ref[...], dfn_ref[...]], axis=0)
        tn = 256
        for c0 in range(0, D_FF, tn):
            cs = slice(c0, c0 + tn)
            da_ext = lax.dot_general(df_ext, wd_ref[cs, :], _NT, preferred_element_type=F32)
            cw0, cw1, cw2 = cw_ref[0:1, cs], cw_ref[1:2, cs], cw_ref[2:3, cs]
            gc = gc_ref[:, cs]
            da = da_ext[hr:hr + tm]
            dgc, cdf = d_conv_out(gc, up_ref[:, cs], da)
            dup_ref[:, cs] = (da * (gc * cdf)).astype(BF16)
            dgc_b = jnp.where(i == 0, 0.0, 1.0) * d_conv_out(gcp_ref[7:8, cs], upp_ref[7:8, cs], da_ext[hr - 1:hr])[0]
            dgc_a = jnp.where(i == nt - 1, 0.0, 1.0) * d_conv_out(gcn_ref[0:1, cs], upn_ref[0:1, cs],
                                                                  da_ext[hr + tm:hr + tm + 1])[0]
            row = lax.broadcasted_iota(jnp.int32, dgc.shape, 0)
            dgc_m1 = jnp.where(row == 0, dgc_b, pltpu.roll(dgc, 1, 0))
            dgc_p1 = jnp.where(row == tm - 1, dgc_a, pltpu.roll(dgc, tm - 1, 0))
            dg_ref[:, cs] = (dgc_p1 * cw0 + dgc * cw1 + dgc_m1 * cw2).astype(BF16)

            def fold(t):
                return jnp.sum(t.reshape(tm // 8, 8, tn), axis=0)

            gv = g_ref[:, cs]
            a0[:, cs] += fold(dgc_p1 * gv)
            a1[:, cs] += fold(dgc * gv)
            a2[:, cs] += fold(dgc_m1 * gv)
            a3[:, cs] += fold(dgc)

        @pl.when(i == nt - 1)
        def _():
            dcw_ref[...] = jnp.concatenate(
                [jnp.sum(a[...], axis=0, keepdims=True) for a in (a0, a1, a2)] + [jnp.zeros((5, D_FF), F32)], axis=0)
            dcb_ref[...] = jnp.sum(a3[...], axis=0, keepdims=True)

    row = pl.BlockSpec((tm, D_FF), lambda i: (i, 0))
    prev, nxt = _halo_specs(s, tm, D_FF)
    cw_spec = pl.BlockSpec((8, D_FF), lambda i: (0, 0))
    cb_spec = pl.BlockSpec((1, D_FF), lambda i: (0, 0))
    dprev, dnxt = _halo_specs(s, tm, D_MODEL, hr)
    return _pcall(body, name="ffn_bwd",
                  out_shape=(SDS((s, D_FF), BF16), SDS((s, D_FF), BF16), SDS((8, D_FF), F32), SDS((1, D_FF), F32)),
                  grid=(nt,),
                  in_specs=[pl.BlockSpec((tm, D_MODEL), lambda i: (i, 0)), dprev, dnxt,
                            pl.BlockSpec((D_FF, D_MODEL), lambda i: (0, 0), pipeline_mode=pl.Buffered(1)), row]
                  + [row, prev, nxt] * 2 + [cw_spec],
                  out_specs=(row, row, cw_spec, cb_spec), scratch_shapes=[pltpu.VMEM((8, D_FF), F32)] * 4,
                  dims=("arbitrary",))(dffb, dffb, dffb, w_down, g, gc, gc, gc, up, up, up, cw)


def _adamw(w, g, m, v, name):
    rows, cols = w.shape
    tr = _pick(rows, (256, 128, 64, 32, 16, 8))
    c1 = 1.0 - ADAM_B1 ** ADAM_STEP
    c2 = 1.0 - ADAM_B2 ** ADAM_STEP

    def body(w_ref, g_ref, m_ref, v_ref, d_ref, nm_ref, nv_ref):
        gv = g_ref[...]
        nm = ADAM_B1 * m_ref[...] + (1.0 - ADAM_B1) * gv
        nv = ADAM_B2 * v_ref[...] + (1.0 - ADAM_B2) * (gv * gv)
        d_ref[...] = -ADAM_LR * ((nm / c1) / (jnp.sqrt(nv / c2) + ADAM_EPS) + ADAM_WD * w_ref[...])
        nm_ref[...] = nm
        nv_ref[...] = nv

    blk = pl.BlockSpec((tr, cols), lambda i: (i, 0))
    return _pcall(body, name=name, out_shape=(SDS(w.shape, F32),) * 3, grid=(rows // tr,), in_specs=[blk] * 4,
                  out_specs=(blk,) * 3, dims=("parallel",))(w, g, m, v)


def _adamw_many(ws, gs, ms, vs, name):
    n = len(ws)
    c1 = 1.0 - ADAM_B1 ** ADAM_STEP
    c2 = 1.0 - ADAM_B2 ** ADAM_STEP

    def body(*refs):
        outs = refs[4 * n:]
        for k in range(n):
            gv = refs[n + k][...]
            nm = ADAM_B1 * refs[2 * n + k][...] + (1.0 - ADAM_B1) * gv
            nv = ADAM_B2 * refs[3 * n + k][...] + (1.0 - ADAM_B2) * (gv * gv)
            outs[k][...] = -ADAM_LR * ((nm / c1) / (jnp.sqrt(nv / c2) + ADAM_EPS) + ADAM_WD * refs[k][...])
            outs[n + k][...] = nm
            outs[2 * n + k][...] = nv

    shapes = tuple(SDS(w.shape, F32) for w in ws)
    res = _pcall(body, name=name, out_shape=shapes * 3)(*ws, *gs, *ms, *vs)
    return res[:n], res[n:2 * n], res[2 * n:]


def _all_gather_rows(x_shard, *, name, in_vmem, sum_rows=False, after=None):
    m_per, n = x_shard.shape
    extra = [] if after is None else [after]

    def body(x_ref, *rest):
        out_ref, rest = rest[len(extra)], rest[len(extra) + 1:]
        if sum_rows:
            sum_ref, send_sems, recv_sems, local_sem = rest
        else:
            send_sems, recv_sems, local_sem = rest
        x, y, c = lax.axis_index("x"), lax.axis_index("y"), lax.axis_index("c")
        me, sibling = (x, y, c), (x, y, 1 - c)
        chips = [(1 - x, y), (x, 1 - y), (1 - x, 1 - y)]

        def rows(px, py, pc):
            return out_ref.at[pl.ds((4 * px + 2 * py + pc) * m_per, m_per), :]

        def copy(k, block, to, src=None):
            return pltpu.make_async_remote_copy(
                src_ref=rows(*block) if src is None else src, dst_ref=rows(*block), send_sem=send_sems.at[k],
                recv_sem=recv_sems.at[k], device_id=to, device_id_type=pl.DeviceIdType.MESH)

        mine = pltpu.make_async_copy(x_ref, rows(*me), local_sem)
        mine.start()
        first = [copy(0, me, sibling, src=x_ref)]
        first += [copy(1 + j, me, (*chip, c), src=x_ref) for j, chip in enumerate(chips)]
        for cp in first:
            cp.start()
        passed = [copy(4 + j, (*chip, c), sibling) for j, chip in enumerate(chips)]
        for j, chip in enumerate(chips):
            copy(1 + j, (*chip, c), me).wait_recv()
            passed[j].start()
        copy(0, sibling, me).wait_recv()
        for j, chip in enumerate(chips):
            copy(4 + j, (*chip, 1 - c), me).wait_recv()
        for cp in first + passed:
            cp.wait_send()
        mine.wait()
        if sum_rows:
            acc = out_ref[0:m_per, :]
            for dev in range(1, N_DEV):
                acc = acc + out_ref[dev * m_per:(dev + 1) * m_per, :]
            sum_ref[...] = acc

    space = pltpu.VMEM if in_vmem else pl.ANY
    out_shape = [SDS((N_DEV * m_per, n), x_shard.dtype)]
    out_specs = [pl.BlockSpec(memory_space=space)]
    if sum_rows:
        out_shape.append(SDS((m_per, n), x_shard.dtype))
        out_specs.append(pl.BlockSpec(memory_space=pltpu.VMEM))
    res = _PALLAS_CALL(
        body, name=name, out_shape=tuple(out_shape),
        in_specs=[pl.BlockSpec(memory_space=space)] + [pl.BlockSpec(memory_space=pl.ANY)] * len(extra),
        out_specs=tuple(out_specs),
        scratch_shapes=[pltpu.SemaphoreType.DMA((7,)), pltpu.SemaphoreType.DMA((7,)), pltpu.SemaphoreType.DMA],
        compiler_params=pltpu.CompilerParams(vmem_limit_bytes=VMEM_LIMIT_BYTES),
    )(x_shard, *extra)
    return res if sum_rows else res[0]


_HBM = pl.BlockSpec(memory_space=pltpu.HBM)
_SEM = pl.BlockSpec(memory_space=pltpu.SEMAPHORE)
_SPLIT_PARAMS = dict(has_side_effects=pltpu.SideEffectType.DATAFLOW_SIDE_EFFECTING)


def _split_copies(src_ref, land_ref, send_sems, recv_sems, gather):
    x, y, c = lax.axis_index("x"), lax.axis_index("y"), lax.axis_index("c")
    first = 0 if gather else 1
    copies = []
    for k in range(first, N_DEV):
        px = 1 - x if k & 4 else x
        py = 1 - y if k & 2 else y
        pc = 1 - c if k & 1 else c
        if gather:
            rows = src_ref.shape[0]
            src, dst = src_ref, land_ref.at[pl.ds((4 * x + 2 * y + c) * rows, rows), :]
        else:
            src, dst = src_ref.at[4 * px + 2 * py + pc], land_ref.at[k - 1]
        copies.append(pltpu.make_async_remote_copy(
            src_ref=src, dst_ref=dst, send_sem=send_sems.at[k - first], recv_sem=recv_sems.at[k - first],
            device_id=(px, py, pc), device_id_type=pl.DeviceIdType.MESH))
    return copies


def _exchange_start(src, land_shape, *, gather, name):
    def body(src_ref, land_ref, send_sems, recv_sems, src_thru, land_thru, token):
        for cp in _split_copies(src_ref, land_ref, send_sems, recv_sems, gather):
            cp.start()
        token[...] = jnp.zeros_like(token)

    land = pltpu.with_memory_space_constraint(lax.empty(land_shape, src.dtype), pltpu.HBM)
    n_copies = N_DEV if gather else N_DEV - 1
    return _PALLAS_CALL(
        body, name=name,
        out_shape=(pltpu.SemaphoreType.DMA((n_copies,)), pltpu.SemaphoreType.DMA((n_copies,)),
                   pltpu.HBM(src.shape, src.dtype), pltpu.HBM(land_shape, src.dtype), SDS((8, 128), F32)),
        in_specs=(_HBM, _HBM), out_specs=(_SEM, _SEM, _HBM, _HBM, pl.BlockSpec(memory_space=pltpu.VMEM)),
        input_output_aliases={0: 2, 1: 3}, compiler_params=pltpu.CompilerParams(**_SPLIT_PARAMS),
    )(pltpu.with_memory_space_constraint(src, pltpu.HBM), land)


def _exchange_wait(started, after, *, gather, name):
    send_sems, recv_sems, src_thru, land_thru, _ = started

    def body(src_ref, land_ref, send_sems, recv_sems, after_ref, src_out, land_out):
        copies = _split_copies(src_ref, land_ref, send_sems, recv_sems, gather)
        for cp in copies:
            cp.wait_send()
        for cp in copies:
            cp.wait_recv()

    return _PALLAS_CALL(
        body, name=name,
        out_shape=(pltpu.HBM(src_thru.shape, src_thru.dtype), pltpu.HBM(land_thru.shape, land_thru.dtype)),
        in_specs=(_HBM, _HBM, _SEM, _SEM, pl.BlockSpec(memory_space=pl.ANY)), out_specs=(_HBM, _HBM),
        input_output_aliases={0: 0, 1: 1}, compiler_params=pltpu.CompilerParams(**_SPLIT_PARAMS),
    )(src_thru, land_thru, send_sems, recv_sems, after)


def _sum_parts(own, land, name):
    r, n = own.shape
    tr = _pick(r, (264, 320, 336, 128, 64, 32, 16, 8))

    def body(own_ref, x_ref, o_ref):
        acc = own_ref[...]
        for k in range(N_DEV - 1):
            acc = acc + x_ref[k].astype(F32)
        o_ref[...] = acc

    return _pcall(body, name=name, out_shape=SDS((r, n), F32), grid=(r // tr,),
                  in_specs=[pl.BlockSpec((tr, n), lambda i: (i, 0)), pl.BlockSpec((N_DEV - 1, tr, n), lambda i: (0, i, 0))],
                  out_specs=pl.BlockSpec((tr, n), lambda i: (i, 0)), dims=("parallel",))(own, land)


def _pad_rows(a, rows):
    return jnp.pad(a, ((0, rows - a.shape[0]), (0, 0)))


def kernel(x, mem, positions, ln_in_g, ln_in_b, w_in, attn_sink, g_win, g_dil, w_mix_out, ln1_g, ln1_b, mem_ln_g, mem_ln_b, w_xq, w_xk, w_xv, w_xo, ln2_g, ln2_b, w_gate, w_up, conv_w, conv_b, w_down, ln3_g, ln3_b, loss_target, m_ln_in_g, m_ln_in_b, m_w_in, m_attn_sink, m_g_win, m_g_dil, m_w_mix_out, m_ln1_g, m_ln1_b, m_mem_ln_g, m_mem_ln_b, m_w_xq, m_w_xk, m_w_xv, m_w_xo, m_ln2_g, m_ln2_b, m_w_gate, m_w_up, m_conv_w, m_conv_b, m_w_down, m_ln3_g, m_ln3_b, v_ln_in_g, v_ln_in_b, v_w_in, v_attn_sink, v_g_win, v_g_dil, v_w_mix_out, v_ln1_g, v_ln1_b, v_mem_ln_g, v_mem_ln_b, v_w_xq, v_w_xk, v_w_xv, v_w_xo, v_ln2_g, v_ln2_b, v_w_gate, v_w_up, v_conv_w, v_conv_b, v_w_down, v_ln3_g, v_ln3_b):
    weights = dict(ln_in_g=ln_in_g, ln_in_b=ln_in_b, w_in=w_in, attn_sink=attn_sink, g_win=g_win, g_dil=g_dil, w_mix_out=w_mix_out, ln1_g=ln1_g, ln1_b=ln1_b, mem_ln_g=mem_ln_g, mem_ln_b=mem_ln_b, w_xq=w_xq, w_xk=w_xk, w_xv=w_xv, w_xo=w_xo, ln2_g=ln2_g, ln2_b=ln2_b, w_gate=w_gate, w_up=w_up, conv_w=conv_w, conv_b=conv_b, w_down=w_down, ln3_g=ln3_g, ln3_b=ln3_b)
    mom_m = dict(ln_in_g=m_ln_in_g, ln_in_b=m_ln_in_b, w_in=m_w_in, attn_sink=m_attn_sink, g_win=m_g_win, g_dil=m_g_dil, w_mix_out=m_w_mix_out, ln1_g=m_ln1_g, ln1_b=m_ln1_b, mem_ln_g=m_mem_ln_g, mem_ln_b=m_mem_ln_b, w_xq=m_w_xq, w_xk=m_w_xk, w_xv=m_w_xv, w_xo=m_w_xo, ln2_g=m_ln2_g, ln2_b=m_ln2_b, w_gate=m_w_gate, w_up=m_w_up, conv_w=m_conv_w, conv_b=m_conv_b, w_down=m_w_down, ln3_g=m_ln3_g, ln3_b=m_ln3_b)
    mom_v = dict(ln_in_g=v_ln_in_g, ln_in_b=v_ln_in_b, w_in=v_w_in, attn_sink=v_attn_sink, g_win=v_g_win, g_dil=v_g_dil, w_mix_out=v_w_mix_out, ln1_g=v_ln1_g, ln1_b=v_ln1_b, mem_ln_g=v_mem_ln_g, mem_ln_b=v_mem_ln_b, w_xq=v_w_xq, w_xk=v_w_xk, w_xv=v_w_xv, w_xo=v_w_xo, ln2_g=v_ln2_g, ln2_b=v_ln2_b, w_gate=v_w_gate, w_up=v_w_up, conv_w=v_conv_w, conv_b=v_conv_b, w_down=v_w_down, ln3_g=v_ln3_g, ln3_b=v_ln3_b)
    order = list(weights)
    s = x.shape[1]
    xs = x[0]
    mems = mem[0]
    target = loss_target[0]
    row = lambda a: a.reshape(1, -1)

    shard_rows = dict(w_in=w_in[0].T, w_gate=w_gate[0].T, w_up=w_up[0].T, w_mix_out=w_mix_out[0], w_xq=w_xq[0],
                      w_xk=w_xk[0], w_xv=w_xv[0], w_xo=w_xo[0], w_down=w_down[0])
    me_lin = 4 * lax.axis_index("x") + 2 * lax.axis_index("y") + lax.axis_index("c")
    w_in_full = _all_gather_rows(shard_rows["w_in"].astype(BF16), name="w_in_all_gather", in_vmem=False)
    late_rows = PACK_ROWS[1:]
    late_r = sum(r for _, r in late_rows)
    packed = jnp.concatenate([shard_rows[n].astype(BF16) for n, _ in late_rows], axis=0)
    w_started = _exchange_start(packed, (N_DEV * late_r, D_MODEL), gather=True, name="weight_gather_start")
    cw_pad = jnp.pad(conv_w[0], ((0, 5), (0, 32)))
    cw_all = _all_gather_rows(cw_pad, name="conv_w_all_gather", in_vmem=True).reshape(N_DEV, 8, 384)
    cw_full = jnp.transpose(cw_all[:, :3, :352], (1, 0, 2)).reshape(3, D_FF)
    cw8 = _pad_rows(cw_full, 8)

    tabs = _rope_tables(positions.astype(F32).reshape(s, 1) + w_started[4][0, 0])
    h0, h0b = _ln_fwd(xs, None, row(ln_in_g), row(ln_in_b), 1.0, "ln_in_fwd")
    zw, *zg = _proj_rope(h0b, w_in_full, tabs[0])
    oa, lse_a = _banded_fwd(zw, attn_sink, name="win_attn_fwd", **_WIN_CFG)
    og_views, lg_views = [], []
    for gi in range(3):
        o_g, l_g = _banded_fwd(zg[gi], None, name=f"dil_attn_fwd{gi}", **_dil_cfg(gi))
        og_views.append(o_g)
        lg_views.append(l_g)
    mixed, ob_views, lb_views = _mix_norm_fwd(oa, og_views, lg_views, g_win, g_dil)
    _, land = _exchange_wait(w_started, mixed, gather=True, name="weight_gather_wait")
    gathered = land.reshape(N_DEV, late_r, D_MODEL)
    full = {}
    off = 0
    for n, r in late_rows:
        full[n] = gathered[:, off:off + r, :].reshape(N_DEV * r, D_MODEL)
        off += r
    mix, h1, h1b = _mm_ln_fwd(mixed, full["w_mix_out"], h0, ln1_g, ln1_b, ALPHA, "mm_mix_out_ln1")
    _, mem_nb = _ln_fwd(mems, None, mem_ln_g, mem_ln_b, 1.0, "mem_ln_fwd")
    kx = _mm(mem_nb, full["w_xk"], trans_b=False, out_dtype=BF16, name="mm_xk")
    vx = _mm(mem_nb, full["w_xv"], trans_b=False, out_dtype=BF16, name="mm_xv")
    qx = _mm(h1b, full["w_xq"], trans_b=False, out_dtype=BF16, name="mm_xq")
    ox, oxb = _xattn_fwd(qx, kx, vx)
    xa, h2, h2b = _mm_ln_fwd(oxb, full["w_xo"], h1, ln2_g, ln2_b, ALPHA, "mm_xo_ln2")
    gate, up, gc, act = _ffn_fwd(h2b, full["w_gate"], full["w_up"], cw8, conv_b)

    du3, du3b, d_ln3_g, d_ln3_b, loss_local = _mm_ln_loss(act, full["w_down"], h2, target, ln3_g, ln3_b, ALPHA,
                                                          "mm_down_ln3_loss")
    dw_down = _mm_tn(act, du3b, name="mm_dw_down")
    dgate, dup, dcw8, d_conv_b = _ffn_bwd(du3b, full["w_down"], gate, gc, up, cw8)
    dw_gate_t = _mm_tn(dgate, h2b, name="mm_dw_gate")
    dw_up_t = _mm_tn(dup, h2b, name="mm_dw_up")
    rows_of = dict(PACK_ROWS)

    own_f32 = {}

    def start_grad_exchange(parts, name, payload=F32):
        gp = jnp.concatenate([g.reshape(N_DEV, rows_of[n], D_MODEL) for n, g in parts], axis=1)
        if payload != F32:
            own_f32[name] = lax.dynamic_index_in_dim(gp, me_lin, axis=0, keepdims=False)
            gp = gp.astype(payload)
        return _exchange_start(gp, (N_DEV - 1,) + gp.shape[1:], gather=False, name=name)

    ffn_parts = (("w_gate", dw_gate_t), ("w_up", dw_up_t), ("w_down", dw_down))
    ffn_started = start_grad_exchange(ffn_parts, "grad_start_ffn")
    du2, du2b, d_ln2_g, d_ln2_b = _mm_ln_bwd(((dgate, full["w_gate"], False), (dup, full["w_up"], False)), du3, ALPHA,
                                             h1, xa, ln2_g + ffn_started[4][0, 0], ALPHA, "mm_dh2_ln2_bwd")
    dox = _mm(du2b, full["w_xo"], trans_b=True, out_dtype=F32, name="mm_d_ox")
    dw_xo = _mm_tn(oxb, du2b, name="mm_dw_xo")
    dqx, dkx, dvx = _xattn_bwd(qx, kx, vx, ox, dox)
    dw_xq = _mm_tn(h1b, dqx, name="mm_dw_xq")
    dw_xk = _mm_tn(mem_nb, dkx, name="mm_dw_xk")
    dw_xv = _mm_tn(mem_nb, dvx, name="mm_dw_xv")
    _, _, d_mem_ln_g, d_mem_ln_b = _mm_ln_bwd(((dkx, full["w_xk"], True), (dvx, full["w_xv"], True)), None, 1.0, mems,
                                              None, mem_ln_g, 1.0, "mm_dmem_ln_bwd")
    du1, du1b, d_ln1_g, d_ln1_b = _mm_ln_bwd(((dqx, full["w_xq"], True),), du2, ALPHA, h0, mix, ln1_g, ALPHA,
                                             "mm_dh1_ln1_bwd")
    dmixed = _mm(du1b, full["w_mix_out"], trans_b=True, out_dtype=F32, name="mm_d_mixed")
    dw_mix_out = _mm_tn(mixed, du1b, name="mm_dw_mix_out")
    attn_parts = (("w_mix_out", dw_mix_out), ("w_xq", dw_xq), ("w_xk", dw_xk), ("w_xv", dw_xv), ("w_xo", dw_xo))
    attn_started = start_grad_exchange(attn_parts, "grad_start_attn")
    doa, dob_views, d_g_win, d_g_dil = _mix_norm_bwd(oa, ob_views[0], dmixed, g_win + attn_started[4][0, 0], g_dil)
    dqa, dkva, dsink8 = _banded_bwd(zw, oa, lse_a, doa, tabs[0], attn_sink, name="win_attn_bwd", **_WIN_CFG)
    dq_views, dkv_views = [], []
    for gi in range(3):
        dq_g, dkv_g = _banded_bwd(zg[gi], ob_views[gi], lb_views[gi], dob_views[gi], tabs[gi], None,
                                  name=f"dil_attn_bwd{gi}", **{**_dil_cfg(gi), "rc": ATTN_TQ})
        dq_views.append(dq_g)
        dkv_views.append(dkv_g)
    dz = _dz_assemble(dq_views, dkv_views, dqa, dkva)
    dw_in_t = _mm_tn(dz, h0b, name="mm_dw_in")
    in_parts = (("w_in", dw_in_t),)
    in_started = start_grad_exchange(in_parts, "grad_start_in", payload=BF16)
    dx, _, d_ln_in_g, d_ln_in_b = _mm_ln_bwd(((dz, w_in_full, False),), du1, ALPHA, xs, None, row(ln_in_g), 1.0,
                                             "mm_dh0_ln_in_bwd", after=in_started[4])

    grads, delta, new_m, new_v = {}, {}, {}, {}
    after = dx
    for parts, started, tag in ((ffn_parts, ffn_started, "ffn"), (attn_parts, attn_started, "attn"),
                                (in_parts, in_started, "in")):
        gp_thru, land = _exchange_wait(started, after, gather=False, name=f"grad_wait_{tag}")
        own = own_f32.get(f"grad_start_{tag}")
        if own is None:
            own = lax.dynamic_index_in_dim(gp_thru, me_lin, axis=0, keepdims=False)
        gsum = _sum_parts(own, land, f"grad_sum_{tag}")
        off = 0
        for n, _ in parts:
            blk = gsum[off:off + rows_of[n]]
            off += rows_of[n]
            grads[n] = (blk.T if n in ("w_in", "w_gate", "w_up") else blk)[None]
            shp = weights[n].shape
            d_, m_, v_ = _adamw(weights[n].reshape(shp[1:]), grads[n].reshape(shp[1:]), mom_m[n].reshape(shp[1:]),
                                mom_v[n].reshape(shp[1:]), f"adamw_{n}")
            delta[n], new_m[n], new_v[n] = d_.reshape(shp), m_.reshape(shp), v_.reshape(shp)
            after = d_

    small = jnp.concatenate([
        d_ln_in_g, d_ln_in_b, d_ln1_g, d_ln1_b, d_mem_ln_g, d_mem_ln_b, d_ln2_g, d_ln2_b, d_ln3_g, d_ln3_b,
        jnp.concatenate([d_g_win, d_g_dil], axis=1),
        jnp.pad(d_conv_b, ((0, 0), (0, 3072 - D_FF))).reshape(3, 1024),
        jnp.pad(dsink8[0:1, :], ((0, 0), (0, 1024 - 128))),
        jnp.pad(dcw8[0:3], ((0, 0), (0, 3072 - D_FF))).reshape(9, 1024),
    ], axis=0)
    _, ssum = _all_gather_rows(small, name="small_grad_all_reduce", in_vmem=True, sum_rows=True, after=after)
    names10 = ["ln_in_g", "ln_in_b", "ln1_g", "ln1_b", "mem_ln_g", "mem_ln_b", "ln2_g", "ln2_b", "ln3_g", "ln3_b"]
    small_g = {n: ssum[i:i + 1] for i, n in enumerate(names10)}
    small_g["g_win"] = ssum[10:11, :512]
    small_g["g_dil"] = ssum[10:11, 512:]
    small_g["conv_b"] = ssum[11:14].reshape(1, 3072)[:, :D_FF]
    small_g["attn_sink"] = ssum[14:15, :8]
    small_g["conv_w"] = lax.dynamic_slice_in_dim(ssum[15:24].reshape(3, 3072)[:, :D_FF], me_lin * 352, 352, axis=1)

    small_names = [n for n in order if n not in rows_of]
    two_d = lambda a: a.reshape(-1, a.shape[-1])
    d_s, m_s, v_s = _adamw_many([two_d(weights[n]) for n in small_names], [small_g[n] for n in small_names],
                                [two_d(mom_m[n]) for n in small_names], [two_d(mom_v[n]) for n in small_names],
                                "adamw_small")
    for k, n in enumerate(small_names):
        shp = weights[n].shape
        grads[n], delta[n], new_m[n], new_v[n] = (t.reshape(shp) for t in (small_g[n], d_s[k], m_s[k], v_s[k]))

    loss = lax.psum(loss_local[0, 0], MESH_AXES)
    return (loss, dx[None], *[grads[n] for n in order], *[delta[n] for n in order], *[new_m[n] for n in order],
            *[new_v[n] for n in order])
```

```python
import functools
import math

import jax
import jax.numpy as jnp
from jax import lax
from jax.experimental import pallas as pl
from jax.experimental.pallas import tpu as pltpu

F32 = jnp.float32
BF16 = jnp.bfloat16
SDS = jax.ShapeDtypeStruct
_PALLAS_CALL = pl.pallas_call

D_MODEL = 1024
HEAD_DIM = 64
WIN_HALF = 128
DIL_PAIRS = ((128, 1), (512, 4), (2048, 16))
DIL_SIDE = 64
ROT_DIM = 16
ROPE_THETA = 500000.0
MEM_LEN = 256
X_HEADS = 4
X_HEAD_DIM = 256
D_FF = 2816
IN_WIDTH = 5376
Z_QB, Z_KB, Z_VB, Z_QA, Z_KA, Z_VA = 0, 1536, 3072, 4608, 5120, 5248
W_IN_QA, W_IN_KA, W_IN_QB = 0, 512, 768
ALPHA = (2.0) ** 0.25
LN_EPS = 1e-5
NEG_INF = -1e30
ADAM_LR, ADAM_B1, ADAM_B2, ADAM_EPS, ADAM_WD, ADAM_STEP = 0.001, 0.9, 0.999, 1e-08, 0.01, 10
N_DEV = 8
MESH_AXES = ("x", "y", "c")
VMEM_LIMIT_BYTES = 52 * 1024 * 1024
ATTN_TQ = 256
WHOLE_CLASS_BYTES = 4 * 1024 * 1024
TABW = 384

PACK_ROWS = (("w_in", 672), ("w_gate", 352), ("w_up", 352), ("w_mix_out", 128), ("w_xq", 128), ("w_xk", 128),
             ("w_xv", 128), ("w_xo", 128), ("w_down", 352))
SMALL_ROWS = 24


def _pick(n, cands):
    for c in cands:
        if n % c == 0:
            return c
    return n


def _pcall(body, *, name, out_shape, grid=None, in_specs=None, out_specs=None, scratch_shapes=(), dims=None,
           aliases=None):
    kw = {}
    if grid is not None:
        kw["grid"] = grid
    if in_specs is not None:
        kw["in_specs"] = in_specs
    if out_specs is not None:
        kw["out_specs"] = out_specs
    if aliases:
        kw["input_output_aliases"] = aliases
    return _PALLAS_CALL(
        body, name=name, out_shape=out_shape, scratch_shapes=list(scratch_shapes),
        compiler_params=pltpu.CompilerParams(dimension_semantics=dims, vmem_limit_bytes=VMEM_LIMIT_BYTES), **kw)


MM_VMEM_BUDGET = 40 * 1024 * 1024


def _mm(a, b, *, trans_b, out_dtype, name, addends=(), coefs=(), after=None, more=()):
    pairs = ((a, b, trans_b),) + tuple(more)
    m = a.shape[0]
    n = b.shape[0] if trans_b else b.shape[1]
    n_add = len(addends)
    extra = [] if after is None else [after]
    out_bytes = jnp.dtype(out_dtype).itemsize

    def vmem(tm, tn):
        tot = tm * tn * (out_bytes + 4 * n_add)
        for pa, pb, _ in pairs:
            tot += tm * pa.shape[1] * pa.dtype.itemsize + pa.shape[1] * tn * pb.dtype.itemsize
        return 2 * tot

    tm, tn = next(((cm, cn) for cn in (n, 1408, 1024, 512, 256, 128) if n % cn == 0
                   for cm in (1024, 512, 256, 128) if m % cm == 0 and vmem(cm, cn) <= MM_VMEM_BUDGET))
    n_pairs = len(pairs)

    def body(*refs):
        o_ref = refs[2 * n_pairs + n_add + len(extra)]
        acc = None
        for p, (_, _, tb) in enumerate(pairs):
            dn = _NT if tb else _NN
            part = lax.dot_general(refs[2 * p][...].astype(BF16), refs[2 * p + 1][...].astype(BF16), dn,
                                   preferred_element_type=F32)
            acc = part if acc is None else acc + part
        for r_ref, c in zip(refs[2 * n_pairs:2 * n_pairs + n_add], coefs):
            acc = acc + (r_ref[...] if c == 1.0 else c * r_ref[...])
        o_ref[...] = acc.astype(out_dtype)

    in_specs, args = [], []
    for pa, pb, tb in pairs:
        k = pa.shape[1]
        in_specs.append(pl.BlockSpec((tm, k), lambda j, i: (i, 0)))
        in_specs.append(pl.BlockSpec((tn, k), lambda j, i: (j, 0)) if tb else pl.BlockSpec((k, tn), lambda j, i: (0, j)))
        args += [pa, pb]
    in_specs += [pl.BlockSpec((tm, tn), lambda j, i: (i, j)) for _ in addends]
    in_specs += [pl.BlockSpec((8, 128), lambda j, i: (0, 0)) for _ in extra]
    return _pcall(body, name=name, out_shape=SDS((m, n), out_dtype), grid=(n // tn, m // tm), in_specs=in_specs,
                  out_specs=pl.BlockSpec((tm, tn), lambda j, i: (i, j)),
                  dims=("parallel", "parallel"))(*args, *addends, *extra)


def _mm_tn(a, b, *, name):
    s, m = a.shape
    n = b.shape[1]
    tm = _pick(m, (768, 1408, 1024, 512, 256, 128))
    tk = _pick(s, (1024, 512, 256))
    nk = s // tk

    def body(a_ref, b_ref, o_ref, acc_ref):
        kk = pl.program_id(1)

        @pl.when(kk == 0)
        def _():
            acc_ref[...] = jnp.zeros_like(acc_ref)

        acc_ref[...] += lax.dot_general(a_ref[...].astype(BF16), b_ref[...].astype(BF16), (((0,), (0,)), ((), ())),
                                        preferred_element_type=F32)

        @pl.when(kk == nk - 1)
        def _():
            o_ref[...] = acc_ref[...]

    return _pcall(body, name=name, out_shape=SDS((m, n), F32), grid=(m // tm, nk),
                  in_specs=[pl.BlockSpec((tk, tm), lambda i, kk: (kk, i)), pl.BlockSpec((tk, n), lambda i, kk: (kk, 0))],
                  out_specs=pl.BlockSpec((tm, n), lambda i, kk: (i, 0)), scratch_shapes=[pltpu.VMEM((tm, n), F32)],
                  dims=("parallel", "arbitrary"))(a, b)


def _rope_lane_consts():
    lane = jnp.arange(128)
    j = lane % HEAD_DIM
    inv_freq = ROPE_THETA ** (-jnp.arange(0, ROT_DIM, 2, dtype=F32) / ROT_DIM)
    freq = jnp.where(j < ROT_DIM, inv_freq[j % (ROT_DIM // 2)], 0.0).astype(F32)
    lo = (j < ROT_DIM // 2).astype(F32)
    hi = ((j >= ROT_DIM // 2) & (j < ROT_DIM)).astype(F32)
    return jnp.stack([freq, lo, hi] + [jnp.zeros((128,), F32)] * 5)


def _to_classes(x, scr, d):
    if d == 1:
        return [x]
    scr[...] = x
    return [scr[pl.ds(c, x.shape[0] // d, stride=d), :] for c in range(d)]


def _from_classes(parts, scr):
    d = len(parts)
    if d == 1:
        return parts[0]
    for c, part in enumerate(parts):
        scr[pl.ds(c, part.shape[0], stride=d), :] = part
    return scr[...]


DILATIONS = tuple(d for _, d in DIL_PAIRS)


def _rope_tables(posf):
    s = posf.shape[0]
    tm = _pick(s, (1024, 512))

    def body(p_ref, c_ref, *rest):
        o_refs, scr = rest[:-1], rest[-1]
        ang = p_ref[...] * c_ref[0:1, :]
        lo = c_ref[1:2, :]
        hi = c_ref[2:3, :]
        cs = jnp.cos(ang)
        sn = jnp.sin(ang)
        for q, t in enumerate((jnp.where(lo + hi > 0.0, cs, 1.0), -sn * lo, sn * hi)):
            for o_ref, d in zip(o_refs, DILATIONS):
                for c, part in enumerate(_to_classes(t, scr, d)):
                    o_ref[:, c * TABW + q * 128:c * TABW + (q + 1) * 128] = part

    return _pcall(body, name="rope_tables", out_shape=tuple(SDS((s // d, d * TABW), F32) for d in DILATIONS),
                  grid=(s // tm,),
                  in_specs=[pl.BlockSpec((tm, 1), lambda i: (i, 0)), pl.BlockSpec((8, 128), lambda i: (0, 0))],
                  out_specs=tuple(pl.BlockSpec((tm // d, d * TABW), lambda i: (i, 0)) for d in DILATIONS),
                  scratch_shapes=[pltpu.VMEM((tm, 128), F32)], dims=("parallel",))(posf, _rope_lane_consts())


def _rope_apply(x, tab, sign):
    w = x.shape[1]
    rep = w // 128
    c = jnp.tile(tab[:, 0:128], (1, rep)) if rep > 1 else tab[:, 0:128]
    a = jnp.tile(tab[:, 128:256], (1, rep)) if rep > 1 else tab[:, 128:256]
    b = jnp.tile(tab[:, 256:384], (1, rep)) if rep > 1 else tab[:, 256:384]
    up = pltpu.roll(x, w - 8, 1)
    dn = pltpu.roll(x, 8, 1)
    if sign > 0:
        return x * c + up * a + dn * b
    return x * c - up * a - dn * b


def _proj_rope(h0b, w_t, tab, after):
    s = h0b.shape[0]
    tm = _pick(s, (512,))
    tn = 256

    def body(a_ref, w_ref, t_ref, _after_ref, zw_ref, z0_ref, z1_ref, z2_ref, scr):
        z_refs = (z0_ref, z1_ref, z2_ref)
        a = a_ref[...]
        tabv = t_ref[...]
        for c0 in range(0, IN_WIDTH, tn):
            w0 = (c0 + W_IN_QB) % IN_WIDTH
            z = lax.dot_general(a, w_ref[w0:w0 + tn, :], _NT, preferred_element_type=F32)
            for g0 in range(c0, c0 + tn, 128):
                zg = z[:, g0 - c0:g0 - c0 + 128]
                if g0 < Z_VB or Z_QA <= g0 < Z_VA:
                    zg = _rope_apply(zg, tabv, 1)
                if g0 >= Z_QA:
                    zw_ref[:, g0 - Z_QA:g0 - Z_QA + 128] = zg.astype(BF16)
                    continue
                kind, within = divmod(g0, 1536)
                grp, off = divmod(within, 512)
                col = kind * 512 + off
                for c, part in enumerate(_to_classes(zg, scr, DILATIONS[grp])):
                    z_refs[grp][:, c * 1536 + col:c * 1536 + col + 128] = part.astype(BF16)

    return _pcall(body, name="proj_rope",
                  out_shape=(SDS((s, 768), BF16),) + tuple(SDS((s // d, d * 1536), BF16) for d in DILATIONS),
                  grid=(s // tm,),
                  in_specs=[pl.BlockSpec((tm, D_MODEL), lambda i: (i, 0)), pl.BlockSpec((IN_WIDTH, D_MODEL), lambda i: (0, 0)),
                            pl.BlockSpec((tm, TABW), lambda i: (i, 0)), pl.BlockSpec((8, 128), lambda i: (0, 0))],
                  out_specs=(pl.BlockSpec((tm, 768), lambda i: (i, 0)),)
                  + tuple(pl.BlockSpec((tm // d, d * 1536), lambda i: (i, 0)) for d in DILATIONS),
                  scratch_shapes=[pltpu.VMEM((tm, 128), F32)], dims=("parallel",))(h0b, w_t, tab, after)


def _band_specs(sd, blk, tq, width, per_tok, cb):
    r = tq // blk
    nbk = sd // blk
    prev = pl.BlockSpec((blk, width), lambda c, j: (jnp.maximum(j * r - 1, 0), c * per_tok + cb))
    cur = pl.BlockSpec((tq, width), lambda c, j: (j, c * per_tok + cb))
    nxt = pl.BlockSpec((blk, width), lambda c, j: (jnp.minimum((j + 1) * r, nbk - 1), c * per_tok + cb))
    return [prev, cur, nxt]


def _band_bias(q0, rows, blk, sd):
    shape = (rows, rows + 2 * blk)
    qpos = q0 + lax.broadcasted_iota(jnp.int32, shape, 0)
    kpos = q0 - blk + lax.broadcasted_iota(jnp.int32, shape, 1)
    ok = (jnp.abs(qpos - kpos) <= blk) & (kpos >= 0) & (kpos < sd)
    return jnp.where(ok, 0.0, NEG_INF)


_NT = (((1,), (1,)), ((), ()))
_NN = (((1,), (0,)), ((), ()))
_TN = (((0,), (0,)), ((), ()))


def _banded_fwd(zv, sink, *, d, blk, tq, rc, ptw, qw, kw, qcb, kcb, vcb, pairs, name):
    sd = zv.shape[0]
    tq = min(tq, sd)
    rc = min(rc, tq)
    has_sink = sink is not None
    scale = HEAD_DIM ** -0.5

    def body(q_ref, kp, kc, kn, vp, vc, vn, *rest):
        if has_sink:
            sink_ref, o_ref, lse_ref = rest
        else:
            o_ref, lse_ref = rest
        j = pl.program_id(1)
        q = q_ref[...] * scale
        k = jnp.concatenate([kp[...], kc[...], kn[...]], axis=0)
        v = jnp.concatenate([vp[...], vc[...], vn[...]], axis=0)
        biases = {r0: _band_bias(j * tq + r0, rc, blk, sd) for r0 in range(0, tq, rc)}
        low = lax.broadcasted_iota(jnp.int32, (1, 128), 1) < HEAD_DIM
        for qb, kb, vb, swaps, sinks in pairs:
            qp, kp_, vp_ = q[:, qb:qb + 128], k[:, kb:kb + 128], v[:, vb:vb + 128]
            if any(swaps):
                k_sw = jnp.concatenate([kp_[:, HEAD_DIM:], kp_[:, :HEAD_DIM]], axis=1)
                v_sw = jnp.concatenate([vp_[:, HEAD_DIM:], vp_[:, :HEAD_DIM]], axis=1)
            for r0 in range(0, tq, rc):
                outs, lses = [], []
                for half in range(2):
                    qm = jnp.where(low if half == 0 else ~low, qp[r0:r0 + rc], jnp.zeros((rc, 128), BF16))
                    kk, vv = (k_sw, v_sw) if swaps[half] else (kp_, vp_)
                    kk, vv = kk[r0:r0 + rc + 2 * blk], vv[r0:r0 + rc + 2 * blk]
                    sc = lax.dot_general(qm, kk, _NT, preferred_element_type=F32) + biases[r0]
                    m = jnp.max(sc, axis=-1, keepdims=True)
                    if has_sink:
                        m = jnp.maximum(m, sink_ref[0, sinks[half]])
                    p = jnp.exp(sc - m)
                    den = jnp.sum(p, axis=-1, keepdims=True)
                    if has_sink:
                        den = den + jnp.exp(sink_ref[0, sinks[half]] - m)
                    outs.append(lax.dot_general(p.astype(BF16), vv, _NN, preferred_element_type=F32) / den)
                    lses.append(m + jnp.log(den))
                o_ref[r0:r0 + rc, qb:qb + 128] = jnp.where(low, outs[0], outs[1])
                lse_ref[r0:r0 + rc, qb:qb + 128] = jnp.where(low, lses[0], lses[1])

    in_specs = ([pl.BlockSpec((tq, qw), lambda c, j: (j, c * (ptw // qw) + qcb))]
                + _band_specs(sd, blk, tq, kw, ptw // kw, kcb) + _band_specs(sd, blk, tq, kw, ptw // kw, vcb))
    args = [zv] * 7
    if has_sink:
        in_specs.append(pl.BlockSpec(memory_space=pltpu.SMEM))
        args.append(sink)
    o_spec = pl.BlockSpec((tq, qw), lambda c, j: (j, c))
    return _pcall(body, name=name, out_shape=(SDS((sd, d * qw), F32), SDS((sd, d * qw), F32)), grid=(d, sd // tq),
                  in_specs=in_specs, out_specs=(o_spec, o_spec), dims=("parallel", "parallel"))(*args)


def _banded_bwd(zv, ov, lv, dov, tv, sink, *, d, blk, tq, rc, ptw, qw, kw, qcb, kcb, vcb, pairs, name):
    sd = zv.shape[0]
    tq = min(tq, sd)
    nt = sd // tq
    r = tq // blk
    nbk = sd // blk
    has_sink = sink is not None
    scale = HEAD_DIM ** -0.5
    rc = min(rc, tq)
    kvw = 128 * len({kb for _, kb, _, _, _ in pairs})
    whole_class = sd * 2 * kvw * 2 <= WHOLE_CLASS_BYTES

    def add_rows(x, y, last):
        if tq == blk:
            return x + y
        if last:
            return jnp.concatenate([x[:tq - blk], x[tq - blk:] + y], axis=0)
        return jnp.concatenate([x[:blk] + y, x[blk:]], axis=0)

    def body(q_ref, kp, kc, kn, vp, vc, vn, o_ref, l_ref, do_ref, t_ref, tlag_ref, *rest):
        if has_sink:
            sink_ref, dq_ref, dkv_ref, dsink_ref, acck, accv, nxtk, nxtv = rest
        else:
            dq_ref, dkv_ref, acck, accv, nxtk, nxtv = rest
        j = pl.program_id(1)

        @pl.when(j == 0)
        def _():
            nxtk[...] = jnp.zeros_like(nxtk)
            nxtv[...] = jnp.zeros_like(nxtv)

        if has_sink:
            @pl.when((pl.program_id(0) == 0) & (j == 0))
            def _():
                dsink_ref[...] = jnp.zeros_like(dsink_ref)

        def emit(dk_rows, dv_rows, tabv, tile_idx):
            val = jnp.concatenate([_rope_apply(dk_rows, tabv, -1), dv_rows], axis=1).astype(BF16)
            if whole_class:
                dkv_ref[pl.ds(pl.multiple_of(tile_idx * tq, tq), tq), :] = val
            else:
                dkv_ref[...] = val

        @pl.when(j < nt)
        def _():
            q = q_ref[...] * scale
            k3 = jnp.concatenate([kp[...], kc[...], kn[...]], axis=0)
            v3 = jnp.concatenate([vp[...], vc[...], vn[...]], axis=0)
            o_t, l_t, do_t = o_ref[...], l_ref[...], do_ref[...]
            biases = {r0: _band_bias(j * tq + r0, rc, blk, sd) for r0 in range(0, tq, rc)}
            lane = lax.broadcasted_iota(jnp.int32, (1, 128), 1)
            low = lane < HEAD_DIM
            wide = tq + 2 * blk
            cw = rc + 2 * blk

            def place(x, r0):
                parts = ([jnp.zeros((r0, 128), F32)] if r0 else []) + [x]
                if wide - r0 - cw:
                    parts.append(jnp.zeros((wide - r0 - cw, 128), F32))
                return jnp.concatenate(parts, axis=0) if len(parts) > 1 else x

            dqs = []
            wks, wvs = {}, {}
            dsink_row = jnp.zeros((1, 128), F32)
            for qb, kb, vb, swaps, sinks in pairs:
                qp, kp_, vp_ = q[:, qb:qb + 128], k3[:, kb:kb + 128], v3[:, vb:vb + 128]
                if any(swaps):
                    k_sw = jnp.concatenate([kp_[:, HEAD_DIM:], kp_[:, :HEAD_DIM]], axis=1)
                    v_sw = jnp.concatenate([vp_[:, HEAD_DIM:], vp_[:, :HEAD_DIM]], axis=1)
                dop, lp = do_t[:, qb:qb + 128], l_t[:, qb:qb + 128]
                prod = dop * o_t[:, qb:qb + 128]
                dq_rows = []
                for r0 in range(0, tq, rc):
                    rows = slice(r0, r0 + rc)
                    dq_half = []
                    for half in range(2):
                        mine = low if half == 0 else ~low
                        qm = jnp.where(mine, qp[rows], jnp.zeros((rc, 128), BF16))
                        dob = jnp.where(mine, dop[rows], 0.0).astype(BF16)
                        delta = jnp.sum(jnp.where(mine, prod[rows], 0.0), axis=-1, keepdims=True)
                        lse = lp[rows, half * HEAD_DIM:half * HEAD_DIM + 1]
                        kk, vv = (k_sw, v_sw) if swaps[half] else (kp_, vp_)
                        kk, vv = kk[r0:r0 + cw], vv[r0:r0 + cw]
                        sc = lax.dot_general(qm, kk, _NT, preferred_element_type=F32) + biases[r0]
                        p = jnp.exp(sc - lse)
                        dp = lax.dot_general(dob, vv, _NT, preferred_element_type=F32)
                        dsb = (p * (dp - delta)).astype(BF16)
                        dq_half.append(lax.dot_general(dsb, kk, _NN, preferred_element_type=F32))
                        dk = lax.dot_general(dsb, qm, _TN, preferred_element_type=F32)
                        dv = lax.dot_general(p.astype(BF16), dob, _TN, preferred_element_type=F32)
                        if swaps[half]:
                            dk, dv = pltpu.roll(dk, HEAD_DIM, 1), pltpu.roll(dv, HEAD_DIM, 1)
                        wks[kb] = place(dk, r0) if kb not in wks else wks[kb] + place(dk, r0)
                        wvs[vb] = place(dv, r0) if vb not in wvs else wvs[vb] + place(dv, r0)
                        if has_sink:
                            psink = jnp.exp(sink_ref[0, sinks[half]] - lse)
                            dsink_row = dsink_row + jnp.where(lane == sinks[half], -jnp.sum(psink * delta), 0.0)
                    dq_rows.append(jnp.where(low, dq_half[0], dq_half[1]) * scale)
                dqs.append(jnp.concatenate(dq_rows, axis=0) if len(dq_rows) > 1 else dq_rows[0])
            dq_ref[...] = _rope_apply(jnp.concatenate(dqs, axis=1), t_ref[...], -1).astype(BF16)
            wk = jnp.concatenate([wks[b] for b in sorted(wks)], axis=1) if len(wks) > 1 else wks[min(wks)]
            wv = jnp.concatenate([wvs[b] for b in sorted(wvs)], axis=1) if len(wvs) > 1 else wvs[min(wvs)]
            if has_sink:
                dsink_ref[0:1, :] += dsink_row

            @pl.when(j > 0)
            def _():
                emit(add_rows(acck[...], wk[:blk], True), add_rows(accv[...], wv[:blk], True), tlag_ref[...], j - 1)

            acck[...] = add_rows(wk[blk:blk + tq], nxtk[...], False)
            accv[...] = add_rows(wv[blk:blk + tq], nxtv[...], False)
            nxtk[...] = wk[blk + tq:]
            nxtv[...] = wv[blk + tq:]
            if whole_class:
                @pl.when(j == nt - 1)
                def _():
                    emit(acck[...], accv[...], t_ref[...], j)

        if not whole_class:
            @pl.when(j == nt)
            def _():
                emit(acck[...], accv[...], tlag_ref[...], j - 1)

    def tile(width, per_tok, cb):
        return pl.BlockSpec((tq, width), lambda c, j: (jnp.minimum(j, nt - 1), c * per_tok + cb))

    def halos(width, per_tok, cb):
        before = pl.BlockSpec((blk, width), lambda c, j: (jnp.maximum(jnp.minimum(j, nt - 1) * r - 1, 0), c * per_tok + cb))
        after = pl.BlockSpec((blk, width),
                             lambda c, j: (jnp.minimum((jnp.minimum(j, nt - 1) + 1) * r, nbk - 1), c * per_tok + cb))
        return [before, tile(width, per_tok, cb), after]

    def lagged(width):
        return pl.BlockSpec((tq, width), lambda c, j: (jnp.maximum(j - 1, 0), c))

    in_specs = ([tile(qw, ptw // qw, qcb)] + halos(kw, ptw // kw, kcb) + halos(kw, ptw // kw, vcb)
                + [tile(qw, 1, 0)] * 3 + [tile(TABW, 1, 0), lagged(TABW)])
    args = [zv] * 7 + [ov, lv, dov, tv, tv]
    out_shape = [SDS((sd, d * qw), BF16), SDS((sd, d * 2 * kvw), BF16)]
    out_specs = [tile(qw, 1, 0),
                 pl.BlockSpec((sd, 2 * kvw), lambda c, j: (0, c)) if whole_class else lagged(2 * kvw)]
    if has_sink:
        in_specs.append(pl.BlockSpec(memory_space=pltpu.SMEM))
        args.append(sink)
        out_shape.append(SDS((8, 128), F32))
        out_specs.append(pl.BlockSpec((8, 128), lambda c, j: (0, 0)))
    scratch = [pltpu.VMEM((tq, kvw), F32), pltpu.VMEM((tq, kvw), F32), pltpu.VMEM((blk, kvw), F32),
               pltpu.VMEM((blk, kvw), F32)]
    return _pcall(body, name=name, out_shape=tuple(out_shape), grid=(d, nt + (0 if whole_class else 1)), in_specs=in_specs,
                  out_specs=tuple(out_specs), scratch_shapes=scratch, dims=("arbitrary", "arbitrary"))(*args)


_WIN_PAIRS = tuple((128 * p, 0, 128, (False, True) if p < 2 else (True, False), (2 * p, 2 * p + 1)) for p in range(4))
_WIN_CFG = dict(d=1, blk=WIN_HALF, tq=ATTN_TQ, rc=256, ptw=768, qw=512, kw=256, qcb=0, kcb=2, vcb=2, pairs=_WIN_PAIRS)
_DIL_PAIRS = tuple((128 * p, 128 * p, 128 * p, (False, False), (2 * p, 2 * p + 1)) for p in range(4))


def _dil_cfg(gi):
    return dict(d=DILATIONS[gi], blk=DIL_SIDE, tq=ATTN_TQ, rc=128, ptw=1536, qw=512, kw=512, qcb=0, kcb=1, vcb=2,
                pairs=_DIL_PAIRS)


def _view_specs(tm, width):
    return tuple(pl.BlockSpec((tm // d, d * width), lambda i: (i, 0)) for d in DILATIONS)


def _mix_norm_fwd(oa, og_views, lg_views, g_win, g_dil):
    s = oa.shape[0]
    tm = _pick(s, (512,))

    def body(oa_ref, o0, o1, o2, l0, l1, l2, gw_ref, gd_ref, mixed_ref, ob0, ob1, ob2, lb0, lb1, lb2, scr, ob_s):
        o_refs, l_refs, ob_refs, lb_refs = (o0, o1, o2), (l0, l1, l2), (ob0, ob1, ob2), (lb0, lb1, lb2)
        ssq = jnp.zeros((tm, 1), F32)
        for q in range(4):
            os_, ls_ = [], []
            for g, d in enumerate(DILATIONS):
                cols = [slice(c * 512 + q * 128, c * 512 + (q + 1) * 128) for c in range(d)]
                os_.append(_from_classes([o_refs[g][:, cs] for cs in cols], scr))
                ls_.append(_from_classes([l_refs[g][:, cs] for cs in cols], scr))
            mx = jnp.maximum(jnp.maximum(ls_[0], ls_[1]), ls_[2])
            es = [jnp.exp(l - mx) for l in ls_]
            den = es[0] + es[1] + es[2]
            ob = (es[0] / den) * os_[0] + (es[1] / den) * os_[1] + (es[2] / den) * os_[2]
            lb = mx + jnp.log(den)
            ob_s[:, q * 128:(q + 1) * 128] = ob
            ssq = ssq + jnp.sum(ob * ob, axis=-1, keepdims=True)
            for g, d in enumerate(DILATIONS):
                for val, refs in ((ob, ob_refs), (lb, lb_refs)):
                    for c, part in enumerate(_to_classes(val, scr, d)):
                        refs[g][:, c * 512 + q * 128:c * 512 + (q + 1) * 128] = part
        a = oa_ref[...]
        ra = lax.rsqrt(jnp.mean(a * a, axis=-1, keepdims=True) + LN_EPS)
        rb = lax.rsqrt(ssq * (1.0 / 512) + LN_EPS)
        mixed_ref[...] = jnp.concatenate([a * ra * gw_ref[...], ob_s[...] * rb * gd_ref[...]], axis=1).astype(BF16)

    row = pl.BlockSpec((tm, 512), lambda i: (i, 0))
    vec = pl.BlockSpec((1, 512), lambda i: (0, 0))
    views = _view_specs(tm, 512)
    view_shapes = tuple(SDS((s // d, d * 512), F32) for d in DILATIONS)
    res = _pcall(body, name="mix_norm_fwd", out_shape=(SDS((s, 1024), BF16),) + view_shapes * 2, grid=(s // tm,),
                 in_specs=[row, *views, *views, vec, vec],
                 out_specs=(pl.BlockSpec((tm, 1024), lambda i: (i, 0)),) + views * 2,
                 scratch_shapes=[pltpu.VMEM((tm, 128), F32), pltpu.VMEM((tm, 512), F32)],
                 dims=("parallel",))(oa, *og_views, *lg_views, g_win, g_dil)
    return res[0], res[1:4], res[4:7]


def _mix_norm_bwd(oa, ob, dmixed, g_win, g_dil):
    s = oa.shape[0]
    tm = _pick(s, (512,))
    nt = s // tm

    def body(oa_ref, ob_ref, dm_ref, gw_ref, gd_ref, doa_ref, db0, db1, db2, dgw_ref, dgd_ref, acc_w, acc_d, scr):
        i = pl.program_id(0)

        @pl.when(i == 0)
        def _():
            acc_w[...] = jnp.zeros_like(acc_w)
            acc_d[...] = jnp.zeros_like(acc_d)

        dm = dm_ref[...]
        dxs = []
        for x_ref, g_ref, dy, acc in ((oa_ref, gw_ref, dm[:, :512], acc_w), (ob_ref, gd_ref, dm[:, 512:], acc_d)):
            x = x_ref[...]
            r = lax.rsqrt(jnp.mean(x * x, axis=-1, keepdims=True) + LN_EPS)
            dyg = dy * g_ref[...]
            dxs.append(r * dyg - x * (r * r * r) * jnp.mean(dyg * x, axis=-1, keepdims=True))
            acc[...] += jnp.sum((dy * x * r).reshape(tm // 8, 8, 512), axis=0)
        doa_ref[...] = dxs[0]
        for q in range(4):
            dq = dxs[1][:, q * 128:(q + 1) * 128]
            for db_ref, d in zip((db0, db1, db2), DILATIONS):
                for c, part in enumerate(_to_classes(dq, scr, d)):
                    db_ref[:, c * 512 + q * 128:c * 512 + (q + 1) * 128] = part

        @pl.when(i == nt - 1)
        def _():
            dgw_ref[...] = jnp.sum(acc_w[...], axis=0, keepdims=True)
            dgd_ref[...] = jnp.sum(acc_d[...], axis=0, keepdims=True)

    row = pl.BlockSpec((tm, 512), lambda i: (i, 0))
    vec = pl.BlockSpec((1, 512), lambda i: (0, 0))
    views = _view_specs(tm, 512)
    view_shapes = tuple(SDS((s // d, d * 512), F32) for d in DILATIONS)
    res = _pcall(body, name="mix_norm_bwd",
                 out_shape=(SDS((s, 512), F32),) + view_shapes + (SDS((1, 512), F32), SDS((1, 512), F32)),
                 grid=(nt,), in_specs=[row, row, pl.BlockSpec((tm, 1024), lambda i: (i, 0)), vec, vec],
                 out_specs=(row,) + views + (vec, vec),
                 scratch_shapes=[pltpu.VMEM((8, 512), F32), pltpu.VMEM((8, 512), F32), pltpu.VMEM((tm, 128), F32)],
                 dims=("arbitrary",))(oa, ob, dmixed, g_win, g_dil)
    return res[0], res[1:4], res[4], res[5]


def _dz_assemble(dq_views, dkv_views, dqa, dkva):
    s = dqa.shape[0]
    tm = _pick(s, (512,))

    def body(q0, q1, q2, kv0, kv1, kv2, qa_ref, kva_ref, o_ref, scr):
        for g, d in enumerate(DILATIONS):
            for kind, (ref, width, base) in enumerate((((q0, q1, q2)[g], 512, 0), ((kv0, kv1, kv2)[g], 1024, 0),
                                                       ((kv0, kv1, kv2)[g], 1024, 512))):
                for q in range(4):
                    src = base + q * 128
                    dst = W_IN_QB + kind * 1536 + g * 512 + q * 128
                    if d == 1:
                        o_ref[:, dst:dst + 128] = ref[:, src:src + 128]
                    else:
                        parts = [ref[:, c * width + src:c * width + src + 128].astype(F32) for c in range(d)]
                        o_ref[:, dst:dst + 128] = _from_classes(parts, scr).astype(BF16)
        o_ref[:, W_IN_QA:W_IN_QA + 512] = qa_ref[...]
        o_ref[:, W_IN_KA:W_IN_KA + 256] = kva_ref[...]

    return _pcall(body, name="dz_assemble", out_shape=SDS((s, IN_WIDTH), BF16), grid=(s // tm,),
                  in_specs=[*_view_specs(tm, 512), *_view_specs(tm, 1024), pl.BlockSpec((tm, 512), lambda i: (i, 0)),
                            pl.BlockSpec((tm, 256), lambda i: (i, 0))],
                  out_specs=pl.BlockSpec((tm, IN_WIDTH), lambda i: (i, 0)),
                  scratch_shapes=[pltpu.VMEM((tm, 128), F32)], dims=("parallel",))(*dq_views, *dkv_views, dqa, dkva)


def _ln_fwd(a, r, g, b, ca, name):
    s = a.shape[0]
    tm = _pick(s, (512, 256))
    has_r = r is not None

    def body(*refs):
        a_ref = refs[0]
        r_ref = refs[1] if has_r else None
        g_ref, b_ref, o_ref, ob_ref = refs[1 + has_r:]
        u = a_ref[...] if ca == 1.0 else ca * a_ref[...]
        if has_r:
            u = u + r_ref[...]
        mu = jnp.mean(u, axis=-1, keepdims=True)
        xc = u - mu
        var = jnp.mean(xc * xc, axis=-1, keepdims=True)
        y = xc * lax.rsqrt(var + LN_EPS) * g_ref[...] + b_ref[...]
        o_ref[...] = y
        ob_ref[...] = y.astype(BF16)

    row = pl.BlockSpec((tm, D_MODEL), lambda i: (i, 0))
    vec = pl.BlockSpec((1, D_MODEL), lambda i: (0, 0))
    args = [a] + ([r] if has_r else []) + [g, b]
    return _pcall(body, name=name, out_shape=(SDS((s, D_MODEL), F32), SDS((s, D_MODEL), BF16)), grid=(s // tm,),
                  in_specs=[row] * (1 + has_r) + [vec, vec], out_specs=(row, row), dims=("parallel",))(*args)


def _mm_ln_fwd(a, w, resid, g, b, ca, name):
    s, k = a.shape
    tm = _pick(s, (512, 256))

    def body(a_ref, w_ref, res_ref, g_ref, b_ref, r_ref, o_ref, ob_ref):
        rv = lax.dot_general(a_ref[...], w_ref[...], _NN, preferred_element_type=F32)
        r_ref[...] = rv
        u = ca * res_ref[...] + rv
        mu = jnp.mean(u, axis=-1, keepdims=True)
        xc = u - mu
        var = jnp.mean(xc * xc, axis=-1, keepdims=True)
        y = xc * lax.rsqrt(var + LN_EPS) * g_ref[...] + b_ref[...]
        o_ref[...] = y
        ob_ref[...] = y.astype(BF16)

    row = pl.BlockSpec((tm, D_MODEL), lambda i: (i, 0))
    vec = pl.BlockSpec((1, D_MODEL), lambda i: (0, 0))
    return _pcall(body, name=name, out_shape=(SDS((s, D_MODEL), F32), SDS((s, D_MODEL), F32), SDS((s, D_MODEL), BF16)),
                  grid=(s // tm,),
                  in_specs=[pl.BlockSpec((tm, k), lambda i: (i, 0)), pl.BlockSpec((k, D_MODEL), lambda i: (0, 0)), row, vec, vec],
                  out_specs=(row, row, row), dims=("parallel",))(a, w, resid, g, b)


def _mm_ln_loss(x, w, a, target, g, b, ca, name):
    s, k = x.shape
    tm = _pick(s, (256,))
    nt = s // tm

    def body(x_ref, w_ref, a_ref, t_ref, g_ref, b_ref, du_ref, dub_ref, dg_ref, db_ref, loss_ref, acc_g, acc_b, acc_l):
        i = pl.program_id(0)

        @pl.when(i == 0)
        def _():
            acc_g[...] = jnp.zeros_like(acc_g)
            acc_b[...] = jnp.zeros_like(acc_b)
            acc_l[...] = jnp.zeros_like(acc_l)

        u = ca * a_ref[...] + lax.dot_general(x_ref[...], w_ref[...], _NN, preferred_element_type=F32)
        mu = jnp.mean(u, axis=-1, keepdims=True)
        xc = u - mu
        var = jnp.mean(xc * xc, axis=-1, keepdims=True)
        rstd = lax.rsqrt(var + LN_EPS)
        xhat = xc * rstd
        gv = g_ref[...]
        err = (xhat * gv + b_ref[...]) - t_ref[...]
        acc_l[...] += jnp.sum((err * err).reshape(tm // 8, 8, D_MODEL), axis=0)
        dyv = err * (1.0 / D_MODEL)
        dxh = dyv * gv
        du = rstd * (dxh - jnp.mean(dxh, axis=-1, keepdims=True) - xhat * jnp.mean(dxh * xhat, axis=-1, keepdims=True))
        du_ref[...] = du
        dub_ref[...] = du.astype(BF16)
        acc_g[...] += jnp.sum((dyv * xhat).reshape(tm // 8, 8, D_MODEL), axis=0)
        acc_b[...] += jnp.sum(dyv.reshape(tm // 8, 8, D_MODEL), axis=0)

        @pl.when(i == nt - 1)
        def _():
            dg_ref[...] = jnp.sum(acc_g[...], axis=0, keepdims=True)
            db_ref[...] = jnp.sum(acc_b[...], axis=0, keepdims=True)
            tot = jnp.sum(jnp.sum(acc_l[...], axis=0, keepdims=True), axis=1, keepdims=True)
            loss_ref[...] = tot * (0.5 / D_MODEL)

    row = pl.BlockSpec((tm, D_MODEL), lambda i: (i, 0))
    vec = pl.BlockSpec((1, D_MODEL), lambda i: (0, 0))
    return _pcall(body, name=name,
                  out_shape=(SDS((s, D_MODEL), F32), SDS((s, D_MODEL), BF16), SDS((1, D_MODEL), F32), SDS((1, D_MODEL), F32),
                             SDS((1, 1), F32)),
                  grid=(nt,),
                  in_specs=[pl.BlockSpec((tm, k), lambda i: (i, 0)), pl.BlockSpec((k, D_MODEL), lambda i: (0, 0)), row, row,
                            vec, vec],
                  out_specs=(row, row, vec, vec, pl.BlockSpec((1, 1), lambda i: (0, 0))),
                  scratch_shapes=[pltpu.VMEM((8, D_MODEL), F32)] * 3, dims=("arbitrary",))(x, w, a, target, g, b)


def _mm_ln_bwd(pairs, addend, coef, a, r, g, ca, name, after=None):
    s = a.shape[0]
    has_r = r is not None
    has_add = addend is not None
    extra = [] if after is None else [after]
    n_pairs = len(pairs)

    def vmem(tm):
        tot = tm * D_MODEL * (4 * (2 + has_r) + 6)
        for pa, pb, _ in pairs:
            tot += tm * pa.shape[1] * pa.dtype.itemsize + pb.size * pb.dtype.itemsize
        return 2 * tot

    tm = next(c for c in (512, 256, 128) if s % c == 0 and vmem(c) <= MM_VMEM_BUDGET)
    nt = s // tm

    def body(*refs):
        ins = refs[2 * n_pairs:]
        add_ref = ins[0] if has_add else None
        ins = ins[has_add:]
        a_ref = ins[0]
        r_ref = ins[1] if has_r else None
        g_ref = ins[1 + has_r]
        du_ref, dub_ref, dg_ref, db_ref, acc_g, acc_b = ins[2 + has_r + len(extra):]
        i = pl.program_id(0)

        @pl.when(i == 0)
        def _():
            acc_g[...] = jnp.zeros_like(acc_g)
            acc_b[...] = jnp.zeros_like(acc_b)

        dyv = coef * add_ref[...] if has_add else None
        for p, (_, _, tb) in enumerate(pairs):
            part = lax.dot_general(refs[2 * p][...].astype(BF16), refs[2 * p + 1][...], _NT if tb else _NN,
                                   preferred_element_type=F32)
            dyv = part if dyv is None else dyv + part
        u = a_ref[...] if ca == 1.0 else ca * a_ref[...]
        if has_r:
            u = u + r_ref[...]
        mu = jnp.mean(u, axis=-1, keepdims=True)
        xc = u - mu
        var = jnp.mean(xc * xc, axis=-1, keepdims=True)
        rstd = lax.rsqrt(var + LN_EPS)
        xhat = xc * rstd
        dxh = dyv * g_ref[...]
        du = rstd * (dxh - jnp.mean(dxh, axis=-1, keepdims=True) - xhat * jnp.mean(dxh * xhat, axis=-1, keepdims=True))
        du_ref[...] = du
        dub_ref[...] = du.astype(BF16)
        acc_g[...] += jnp.sum((dyv * xhat).reshape(tm // 8, 8, D_MODEL), axis=0)
        acc_b[...] += jnp.sum(dyv.reshape(tm // 8, 8, D_MODEL), axis=0)

        @pl.when(i == nt - 1)
        def _():
            dg_ref[...] = jnp.sum(acc_g[...], axis=0, keepdims=True)
            db_ref[...] = jnp.sum(acc_b[...], axis=0, keepdims=True)

    row = pl.BlockSpec((tm, D_MODEL), lambda i: (i, 0))
    vec = pl.BlockSpec((1, D_MODEL), lambda i: (0, 0))
    in_specs, args = [], []
    for pa, pb, _ in pairs:
        in_specs += [pl.BlockSpec((tm, pa.shape[1]), lambda i: (i, 0)), pl.BlockSpec(pb.shape, lambda i: (0, 0))]
        args += [pa, pb]
    in_specs += [row] * (has_add + 1 + has_r) + [vec] + [pl.BlockSpec((8, 128), lambda i: (0, 0))] * len(extra)
    args += ([addend] if has_add else []) + [a] + ([r] if has_r else []) + [g] + extra
    return _pcall(body, name=name,
                  out_shape=(SDS((s, D_MODEL), F32), SDS((s, D_MODEL), BF16), SDS((1, D_MODEL), F32), SDS((1, D_MODEL), F32)),
                  grid=(nt,), in_specs=in_specs, out_specs=(row, row, vec, vec),
                  scratch_shapes=[pltpu.VMEM((8, D_MODEL), F32), pltpu.VMEM((8, D_MODEL), F32)],
                  dims=("arbitrary",))(*args)


def _xattn_fwd(q, k, v):
    s = q.shape[0]
    tq = _pick(s, (512,))
    scale = X_HEAD_DIM ** -0.5

    def body(q_ref, k_ref, v_ref, o_ref, ob_ref):
        qv, kv, vv = q_ref[...], k_ref[...], v_ref[...]
        outs = []
        for h in range(X_HEADS):
            sl = slice(h * X_HEAD_DIM, (h + 1) * X_HEAD_DIM)
            sc = lax.dot_general(qv[:, sl], kv[:, sl], _NT, preferred_element_type=F32) * scale
            e = jnp.exp(sc - jnp.max(sc, axis=-1, keepdims=True))
            p = e / jnp.sum(e, axis=-1, keepdims=True)
            outs.append(lax.dot_general(p.astype(BF16), vv[:, sl], _NN, preferred_element_type=F32))
        o = jnp.concatenate(outs, axis=1)
        o_ref[...] = o
        ob_ref[...] = o.astype(BF16)

    row = pl.BlockSpec((tq, D_MODEL), lambda i: (i, 0))
    full = pl.BlockSpec((MEM_LEN, D_MODEL), lambda i: (0, 0))
    return _pcall(body, name="xattn_fwd", out_shape=(SDS((s, D_MODEL), F32), SDS((s, D_MODEL), BF16)), grid=(s // tq,),
                  in_specs=[row, full, full], out_specs=(row, row), dims=("parallel",))(q, k, v)


def _xattn_bwd(q, k, v, o, do):
    s = q.shape[0]
    tq = _pick(s, (512,))
    scale = X_HEAD_DIM ** -0.5

    def body(q_ref, k_ref, v_ref, o_ref, do_ref, dq_ref, dk_ref, dv_ref):
        i = pl.program_id(0)

        @pl.when(i == 0)
        def _():
            dk_ref[...] = jnp.zeros_like(dk_ref)
            dv_ref[...] = jnp.zeros_like(dv_ref)

        qv, kv, vv, ov, dov = q_ref[...], k_ref[...], v_ref[...], o_ref[...], do_ref[...]
        dqs, dks, dvs = [], [], []
        for h in range(X_HEADS):
            sl = slice(h * X_HEAD_DIM, (h + 1) * X_HEAD_DIM)
            sc = lax.dot_general(qv[:, sl], kv[:, sl], _NT, preferred_element_type=F32) * scale
            e = jnp.exp(sc - jnp.max(sc, axis=-1, keepdims=True))
            p = e / jnp.sum(e, axis=-1, keepdims=True)
            doh = dov[:, sl]
            dob = doh.astype(BF16)
            delta = jnp.sum(doh * ov[:, sl], axis=-1, keepdims=True)
            dvs.append(lax.dot_general(p.astype(BF16), dob, _TN, preferred_element_type=F32))
            dp = lax.dot_general(dob, vv[:, sl], _NT, preferred_element_type=F32)
            ds = (p * (dp - delta)).astype(BF16)
            dqs.append(lax.dot_general(ds, kv[:, sl], _NN, preferred_element_type=F32) * scale)
            dks.append(lax.dot_general(ds, qv[:, sl], _TN, preferred_element_type=F32) * scale)
        dq_ref[...] = jnp.concatenate(dqs, axis=1).astype(BF16)
        dk_ref[...] += jnp.concatenate(dks, axis=1)
        dv_ref[...] += jnp.concatenate(dvs, axis=1)

    row = pl.BlockSpec((tq, D_MODEL), lambda i: (i, 0))
    full = pl.BlockSpec((MEM_LEN, D_MODEL), lambda i: (0, 0))
    return _pcall(body, name="xattn_bwd",
                  out_shape=(SDS((s, D_MODEL), BF16), SDS((MEM_LEN, D_MODEL), F32), SDS((MEM_LEN, D_MODEL), F32)),
                  grid=(s // tq,), in_specs=[row, full, full, row, row], out_specs=(row, full, full),
                  dims=("arbitrary",))(q, k, v, o, do)


_SQRT_HALF = 0.7071067811865476
_INV_SQRT_2PI = 0.3989422804014327


def _halo_specs(s, tm, width, rows=8):
    nb = s // rows
    r = tm // rows
    prev = pl.BlockSpec((rows, width), lambda i: (jnp.maximum(i * r - 1, 0), 0))
    nxt = pl.BlockSpec((rows, width), lambda i: (jnp.minimum((i + 1) * r, nb - 1), 0))
    return prev, nxt


def _shifted(x, before_row, after_row, i, nt):
    tm = x.shape[0]
    row = lax.broadcasted_iota(jnp.int32, x.shape, 0)
    first = jnp.where(i == 0, 0.0, 1.0) * before_row
    last = jnp.where(i == nt - 1, 0.0, 1.0) * after_row
    xm1 = jnp.where(row == 0, first, pltpu.roll(x, 1, 0))
    xp1 = jnp.where(row == tm - 1, last, pltpu.roll(x, tm - 1, 0))
    return xm1, xp1


BF16_ROWS = 16


def _ffn_fwd(hb, wg_t, wu_t, cw, cb):
    s = hb.shape[0]
    tm = _pick(s, (256,))
    nt = s // tm
    hr = BF16_ROWS

    def body(h_ref, hp_ref, hn_ref, wg_ref, wu_ref, cw_ref, cb_ref, g_ref, up_ref, gc_ref, act_ref):
        i = pl.program_id(0)
        hv = h_ref[...]
        g_ext = lax.dot_general(jnp.concatenate([hp_ref[...], hv, hn_ref[...]], axis=0), wg_ref[...], _NT,
                                preferred_element_type=F32)
        gv = g_ext[hr:hr + tm]
        upv = lax.dot_general(hv, wu_ref[...], _NT, preferred_element_type=F32)
        gm1, gp1 = _shifted(gv, g_ext[hr - 1:hr], g_ext[hr + tm:hr + tm + 1], i, nt)
        gc = gm1 * cw_ref[0:1, :] + gv * cw_ref[1:2, :] + gp1 * cw_ref[2:3, :] + cb_ref[...]
        cdf = 0.5 * (1.0 + lax.erf(gc * _SQRT_HALF))
        g_ref[...] = gv
        up_ref[...] = upv
        gc_ref[...] = gc
        act_ref[...] = (gc * cdf * upv).astype(BF16)

    hrow = pl.BlockSpec((tm, D_MODEL), lambda i: (i, 0))
    prev, nxt = _halo_specs(s, tm, D_MODEL, hr)
    wfull = pl.BlockSpec((D_FF, D_MODEL), lambda i: (0, 0), pipeline_mode=pl.Buffered(1))
    row = pl.BlockSpec((tm, D_FF), lambda i: (i, 0))
    return _pcall(body, name="ffn_fwd", out_shape=(SDS((s, D_FF), F32),) * 3 + (SDS((s, D_FF), BF16),),
                  grid=(nt,), in_specs=[hrow, prev, nxt, wfull, wfull, pl.BlockSpec((8, D_FF), lambda i: (0, 0)),
                                        pl.BlockSpec((1, D_FF), lambda i: (0, 0))],
                  out_specs=(row, row, row, row), dims=("parallel",))(hb, hb, hb, wg_t, wu_t, cw, cb)


def _ffn_bwd(dffb, w_down, g, gc, up, cw):
    s = g.shape[0]
    tm = _pick(s, (256,))
    nt = s // tm
    hr = BF16_ROWS

    def body(df_ref, dfp_ref, dfn_ref, wd_ref, g_ref, gc_ref, gcp_ref, gcn_ref, up_ref, upp_ref, upn_ref, cw_ref,
             dg_ref, dup_ref, dcw_ref, dcb_ref, a0, a1, a2, a3):
        i = pl.program_id(0)

        @pl.when(i == 0)
        def _():
            for a in (a0, a1, a2, a3):
                a[...] = jnp.zeros_like(a)

        def d_conv_out(gc_, up_, da_):
            cdf_ = 0.5 * (1.0 + lax.erf(gc_ * _SQRT_HALF))
            pdf_ = jnp.exp(-0.5 * gc_ * gc_) * _INV_SQRT_2PI
            return da_ * up_ * (cdf_ + gc_ * pdf_), cdf_

        df_ext = jnp.concatenate([dfp_ref[...], df_ref[...], dfn_ref[...]], axis=0)
        tn = 256
        for c0 in range(0, D_FF, tn):
            cs = slice(c0, c0 + tn)
            da_ext = lax.dot_general(df_ext, wd_ref[cs, :], _NT, preferred_element_type=F32)
            cw0, cw1, cw2 = cw_ref[0:1, cs], cw_ref[1:2, cs], cw_ref[2:3, cs]
            gc = gc_ref[:, cs]
            da = da_ext[hr:hr + tm]
            dgc, cdf = d_conv_out(gc, up_ref[:, cs], da)
            dup_ref[:, cs] = (da * (gc * cdf)).astype(BF16)
            dgc_b = jnp.where(i == 0, 0.0, 1.0) * d_conv_out(gcp_ref[7:8, cs], upp_ref[7:8, cs], da_ext[hr - 1:hr])[0]
            dgc_a = jnp.where(i == nt - 1, 0.0, 1.0) * d_conv_out(gcn_ref[0:1, cs], upn_ref[0:1, cs],
                                                                  da_ext[hr + tm:hr + tm + 1])[0]
            row = lax.broadcasted_iota(jnp.int32, dgc.shape, 0)
            dgc_m1 = jnp.where(row == 0, dgc_b, pltpu.roll(dgc, 1, 0))
            dgc_p1 = jnp.where(row == tm - 1, dgc_a, pltpu.roll(dgc, tm - 1, 0))
            dg_ref[:, cs] = (dgc_p1 * cw0 + dgc * cw1 + dgc_m1 * cw2).astype(BF16)

            def fold(t):
                return jnp.sum(t.reshape(tm // 8, 8, tn), axis=0)

            gv = g_ref[:, cs]
            a0[:, cs] += fold(dgc_p1 * gv)
            a1[:, cs] += fold(dgc * gv)
            a2[:, cs] += fold(dgc_m1 * gv)
            a3[:, cs] += fold(dgc)

        @pl.when(i == nt - 1)
        def _():
            dcw_ref[...] = jnp.concatenate(
                [jnp.sum(a[...], axis=0, keepdims=True) for a in (a0, a1, a2)] + [jnp.zeros((5, D_FF), F32)], axis=0)
            dcb_ref[...] = jnp.sum(a3[...], axis=0, keepdims=True)

    row = pl.BlockSpec((tm, D_FF), lambda i: (i, 0))
    prev, nxt = _halo_specs(s, tm, D_FF)
    cw_spec = pl.BlockSpec((8, D_FF), lambda i: (0, 0))
    cb_spec = pl.BlockSpec((1, D_FF), lambda i: (0, 0))
    dprev, dnxt = _halo_specs(s, tm, D_MODEL, hr)
    return _pcall(body, name="ffn_bwd",
                  out_shape=(SDS((s, D_FF), BF16), SDS((s, D_FF), BF16), SDS((8, D_FF), F32), SDS((1, D_FF), F32)),
                  grid=(nt,),
                  in_specs=[pl.BlockSpec((tm, D_MODEL), lambda i: (i, 0)), dprev, dnxt,
                            pl.BlockSpec((D_FF, D_MODEL), lambda i: (0, 0), pipeline_mode=pl.Buffered(1)), row]
                  + [row, prev, nxt] * 2 + [cw_spec],
                  out_specs=(row, row, cw_spec, cb_spec), scratch_shapes=[pltpu.VMEM((8, D_FF), F32)] * 4,
                  dims=("arbitrary",))(dffb, dffb, dffb, w_down, g, gc, gc, gc, up, up, up, cw)


def _adamw(w, g, m, v, name):
    rows, cols = w.shape
    tr = _pick(rows, (256, 128, 64, 32, 16, 8))
    c1 = 1.0 - ADAM_B1 ** ADAM_STEP
    c2 = 1.0 - ADAM_B2 ** ADAM_STEP

    def body(w_ref, g_ref, m_ref, v_ref, d_ref, nm_ref, nv_ref):
        gv = g_ref[...]
        nm = ADAM_B1 * m_ref[...] + (1.0 - ADAM_B1) * gv
        nv = ADAM_B2 * v_ref[...] + (1.0 - ADAM_B2) * (gv * gv)
        d_ref[...] = -ADAM_LR * ((nm / c1) / (jnp.sqrt(nv / c2) + ADAM_EPS) + ADAM_WD * w_ref[...])
        nm_ref[...] = nm
        nv_ref[...] = nv

    blk = pl.BlockSpec((tr, cols), lambda i: (i, 0))
    return _pcall(body, name=name, out_shape=(SDS(w.shape, F32),) * 3, grid=(rows // tr,), in_specs=[blk] * 4,
                  out_specs=(blk,) * 3, dims=("parallel",))(w, g, m, v)


def _adamw_many(ws, gs, ms, vs, name):
    n = len(ws)
    c1 = 1.0 - ADAM_B1 ** ADAM_STEP
    c2 = 1.0 - ADAM_B2 ** ADAM_STEP

    def body(*refs):
        outs = refs[4 * n:]
        for k in range(n):
            gv = refs[n + k][...]
            nm = ADAM_B1 * refs[2 * n + k][...] + (1.0 - ADAM_B1) * gv
            nv = ADAM_B2 * refs[3 * n + k][...] + (1.0 - ADAM_B2) * (gv * gv)
            outs[k][...] = -ADAM_LR * ((nm / c1) / (jnp.sqrt(nv / c2) + ADAM_EPS) + ADAM_WD * refs[k][...])
            outs[n + k][...] = nm
            outs[2 * n + k][...] = nv

    shapes = tuple(SDS(w.shape, F32) for w in ws)
    res = _pcall(body, name=name, out_shape=shapes * 3)(*ws, *gs, *ms, *vs)
    return res[:n], res[n:2 * n], res[2 * n:]


def _all_gather_rows(x_shard, *, name, in_vmem, sum_rows=False, after=None):
    m_per, n = x_shard.shape
    extra = [] if after is None else [after]

    def body(x_ref, *rest):
        out_ref, rest = rest[len(extra)], rest[len(extra) + 1:]
        if sum_rows:
            sum_ref, send_sems, recv_sems, local_sem = rest
        else:
            send_sems, recv_sems, local_sem = rest
        x, y, c = lax.axis_index("x"), lax.axis_index("y"), lax.axis_index("c")
        me, sibling = (x, y, c), (x, y, 1 - c)
        chips = [(1 - x, y), (x, 1 - y), (1 - x, 1 - y)]

        def rows(px, py, pc):
            return out_ref.at[pl.ds((4 * px + 2 * py + pc) * m_per, m_per), :]

        def copy(k, block, to, src=None):
            return pltpu.make_async_remote_copy(
                src_ref=rows(*block) if src is None else src, dst_ref=rows(*block), send_sem=send_sems.at[k],
                recv_sem=recv_sems.at[k], device_id=to, device_id_type=pl.DeviceIdType.MESH)

        mine = pltpu.make_async_copy(x_ref, rows(*me), local_sem)
        mine.start()
        first = [copy(0, me, sibling, src=x_ref)]
        first += [copy(1 + j, me, (*chip, c), src=x_ref) for j, chip in enumerate(chips)]
        for cp in first:
            cp.start()
        passed = [copy(4 + j, (*chip, c), sibling) for j, chip in enumerate(chips)]
        for j, chip in enumerate(chips):
            copy(1 + j, (*chip, c), me).wait_recv()
            passed[j].start()
        copy(0, sibling, me).wait_recv()
        for j, chip in enumerate(chips):
            copy(4 + j, (*chip, 1 - c), me).wait_recv()
        for cp in first + passed:
            cp.wait_send()
        mine.wait()
        if sum_rows:
            acc = out_ref[0:m_per, :]
            for dev in range(1, N_DEV):
                acc = acc + out_ref[dev * m_per:(dev + 1) * m_per, :]
            sum_ref[...] = acc

    space = pltpu.VMEM if in_vmem else pl.ANY
    out_shape = [SDS((N_DEV * m_per, n), x_shard.dtype)]
    out_specs = [pl.BlockSpec(memory_space=space)]
    if sum_rows:
        out_shape.append(SDS((m_per, n), x_shard.dtype))
        out_specs.append(pl.BlockSpec(memory_space=pltpu.VMEM))
    res = _PALLAS_CALL(
        body, name=name, out_shape=tuple(out_shape),
        in_specs=[pl.BlockSpec(memory_space=space)] + [pl.BlockSpec(memory_space=pl.ANY)] * len(extra),
        out_specs=tuple(out_specs),
        scratch_shapes=[pltpu.SemaphoreType.DMA((7,)), pltpu.SemaphoreType.DMA((7,)), pltpu.SemaphoreType.DMA],
        compiler_params=pltpu.CompilerParams(vmem_limit_bytes=VMEM_LIMIT_BYTES),
    )(x_shard, *extra)
    return res if sum_rows else res[0]


_HBM = pl.BlockSpec(memory_space=pltpu.HBM)
_SEM = pl.BlockSpec(memory_space=pltpu.SEMAPHORE)
_SPLIT_PARAMS = dict(has_side_effects=pltpu.SideEffectType.DATAFLOW_SIDE_EFFECTING)


def _split_copies(src_ref, land_ref, send_sems, recv_sems, gather):
    x, y, c = lax.axis_index("x"), lax.axis_index("y"), lax.axis_index("c")
    first = 0 if gather else 1
    copies = []
    for k in range(first, N_DEV):
        px = 1 - x if k & 4 else x
        py = 1 - y if k & 2 else y
        pc = 1 - c if k & 1 else c
        if gather:
            rows = src_ref.shape[0]
            src, dst = src_ref, land_ref.at[pl.ds((4 * x + 2 * y + c) * rows, rows), :]
        else:
            src, dst = src_ref.at[4 * px + 2 * py + pc], land_ref.at[k - 1]
        copies.append(pltpu.make_async_remote_copy(
            src_ref=src, dst_ref=dst, send_sem=send_sems.at[k - first], recv_sem=recv_sems.at[k - first],
            device_id=(px, py, pc), device_id_type=pl.DeviceIdType.MESH))
    return copies


def _exchange_start(src, land_shape, *, gather, name, after=None):
    extra = [] if after is None else [after]

    def body(src_ref, land_ref, *rest):
        send_sems, recv_sems, _, _, token = rest[len(extra):]
        for cp in _split_copies(src_ref, land_ref, send_sems, recv_sems, gather):
            cp.start()
        token[...] = jnp.zeros_like(token)

    land = pltpu.with_memory_space_constraint(lax.empty(land_shape, src.dtype), pltpu.HBM)
    n_copies = N_DEV if gather else N_DEV - 1
    return _PALLAS_CALL(
        body, name=name,
        out_shape=(pltpu.SemaphoreType.DMA((n_copies,)), pltpu.SemaphoreType.DMA((n_copies,)),
                   pltpu.HBM(src.shape, src.dtype), pltpu.HBM(land_shape, src.dtype), SDS((8, 128), F32)),
        in_specs=(_HBM, _HBM) + (pl.BlockSpec(memory_space=pl.ANY),) * len(extra),
        out_specs=(_SEM, _SEM, _HBM, _HBM, pl.BlockSpec(memory_space=pltpu.VMEM)),
        input_output_aliases={0: 2, 1: 3}, compiler_params=pltpu.CompilerParams(**_SPLIT_PARAMS),
    )(pltpu.with_memory_space_constraint(src, pltpu.HBM), land, *extra)


def _exchange_wait(started, after, *, gather, name):
    send_sems, recv_sems, src_thru, land_thru, _ = started

    def body(src_ref, land_ref, send_sems, recv_sems, after_ref, src_out, land_out):
        copies = _split_copies(src_ref, land_ref, send_sems, recv_sems, gather)
        for cp in copies:
            cp.wait_send()
        for cp in copies:
            cp.wait_recv()

    return _PALLAS_CALL(
        body, name=name,
        out_shape=(pltpu.HBM(src_thru.shape, src_thru.dtype), pltpu.HBM(land_thru.shape, land_thru.dtype)),
        in_specs=(_HBM, _HBM, _SEM, _SEM, pl.BlockSpec(memory_space=pl.ANY)), out_specs=(_HBM, _HBM),
        input_output_aliases={0: 0, 1: 1}, compiler_params=pltpu.CompilerParams(**_SPLIT_PARAMS),
    )(src_thru, land_thru, send_sems, recv_sems, after)


def _sum_parts(own, land, name):
    r, n = own.shape
    tr = _pick(r, (264, 320, 336, 128, 64, 32, 16, 8))

    def body(own_ref, x_ref, o_ref):
        acc = own_ref[...]
        for k in range(N_DEV - 1):
            acc = acc + x_ref[k].astype(F32)
        o_ref[...] = acc

    return _pcall(body, name=name, out_shape=SDS((r, n), F32), grid=(r // tr,),
                  in_specs=[pl.BlockSpec((tr, n), lambda i: (i, 0)), pl.BlockSpec((N_DEV - 1, tr, n), lambda i: (0, i, 0))],
                  out_specs=pl.BlockSpec((tr, n), lambda i: (i, 0)), dims=("parallel",))(own, land)


def _pad_rows(a, rows):
    return jnp.pad(a, ((0, rows - a.shape[0]), (0, 0)))


def kernel(x, mem, positions, ln_in_g, ln_in_b, w_in, attn_sink, g_win, g_dil, w_mix_out, ln1_g, ln1_b, mem_ln_g, mem_ln_b, w_xq, w_xk, w_xv, w_xo, ln2_g, ln2_b, w_gate, w_up, conv_w, conv_b, w_down, ln3_g, ln3_b, loss_target, m_ln_in_g, m_ln_in_b, m_w_in, m_attn_sink, m_g_win, m_g_dil, m_w_mix_out, m_ln1_g, m_ln1_b, m_mem_ln_g, m_mem_ln_b, m_w_xq, m_w_xk, m_w_xv, m_w_xo, m_ln2_g, m_ln2_b, m_w_gate, m_w_up, m_conv_w, m_conv_b, m_w_down, m_ln3_g, m_ln3_b, v_ln_in_g, v_ln_in_b, v_w_in, v_attn_sink, v_g_win, v_g_dil, v_w_mix_out, v_ln1_g, v_ln1_b, v_mem_ln_g, v_mem_ln_b, v_w_xq, v_w_xk, v_w_xv, v_w_xo, v_ln2_g, v_ln2_b, v_w_gate, v_w_up, v_conv_w, v_conv_b, v_w_down, v_ln3_g, v_ln3_b):
    weights = dict(ln_in_g=ln_in_g, ln_in_b=ln_in_b, w_in=w_in, attn_sink=attn_sink, g_win=g_win, g_dil=g_dil, w_mix_out=w_mix_out, ln1_g=ln1_g, ln1_b=ln1_b, mem_ln_g=mem_ln_g, mem_ln_b=mem_ln_b, w_xq=w_xq, w_xk=w_xk, w_xv=w_xv, w_xo=w_xo, ln2_g=ln2_g, ln2_b=ln2_b, w_gate=w_gate, w_up=w_up, conv_w=conv_w, conv_b=conv_b, w_down=w_down, ln3_g=ln3_g, ln3_b=ln3_b)
    mom_m = dict(ln_in_g=m_ln_in_g, ln_in_b=m_ln_in_b, w_in=m_w_in, attn_sink=m_attn_sink, g_win=m_g_win, g_dil=m_g_dil, w_mix_out=m_w_mix_out, ln1_g=m_ln1_g, ln1_b=m_ln1_b, mem_ln_g=m_mem_ln_g, mem_ln_b=m_mem_ln_b, w_xq=m_w_xq, w_xk=m_w_xk, w_xv=m_w_xv, w_xo=m_w_xo, ln2_g=m_ln2_g, ln2_b=m_ln2_b, w_gate=m_w_gate, w_up=m_w_up, conv_w=m_conv_w, conv_b=m_conv_b, w_down=m_w_down, ln3_g=m_ln3_g, ln3_b=m_ln3_b)
    mom_v = dict(ln_in_g=v_ln_in_g, ln_in_b=v_ln_in_b, w_in=v_w_in, attn_sink=v_attn_sink, g_win=v_g_win, g_dil=v_g_dil, w_mix_out=v_w_mix_out, ln1_g=v_ln1_g, ln1_b=v_ln1_b, mem_ln_g=v_mem_ln_g, mem_ln_b=v_mem_ln_b, w_xq=v_w_xq, w_xk=v_w_xk, w_xv=v_w_xv, w_xo=v_w_xo, ln2_g=v_ln2_g, ln2_b=v_ln2_b, w_gate=v_w_gate, w_up=v_w_up, conv_w=v_conv_w, conv_b=v_conv_b, w_down=v_w_down, ln3_g=v_ln3_g, ln3_b=v_ln3_b)
    order = list(weights)
    s = x.shape[1]
    xs = x[0]
    mems = mem[0]
    target = loss_target[0]
    row = lambda a: a.reshape(1, -1)

    shard_rows = dict(w_in=w_in[0].T, w_gate=w_gate[0].T, w_up=w_up[0].T, w_mix_out=w_mix_out[0], w_xq=w_xq[0],
                      w_xk=w_xk[0], w_xv=w_xv[0], w_xo=w_xo[0], w_down=w_down[0])
    me_lin = 4 * lax.axis_index("x") + 2 * lax.axis_index("y") + lax.axis_index("c")
    in_r = dict(PACK_ROWS)["w_in"]
    w_in_started = _exchange_start(shard_rows["w_in"].astype(BF16), (N_DEV * in_r, D_MODEL), gather=True,
                                   name="w_in_gather_start")
    late_rows = PACK_ROWS[1:]
    late_r = sum(r for _, r in late_rows)
    packed = jnp.concatenate([shard_rows[n].astype(BF16) for n, _ in late_rows], axis=0)

    tabs = _rope_tables(positions.astype(F32).reshape(s, 1) + w_in_started[4][0, 0])
    h0, h0b = _ln_fwd(xs, None, row(ln_in_g), row(ln_in_b), 1.0, "ln_in_fwd")
    _, w_in_full = _exchange_wait(w_in_started, h0b, gather=True, name="w_in_gather_wait")
    w_started = _exchange_start(packed, (N_DEV * late_r, D_MODEL), gather=True, name="weight_gather_start",
                                after=w_in_full)
    zw, *zg = _proj_rope(h0b, w_in_full, tabs[0], w_started[4])
    cw_pad = jnp.pad(conv_w[0], ((0, 5), (0, 32)))
    cw_all = _all_gather_rows(cw_pad, name="conv_w_all_gather", in_vmem=True, after=zw).reshape(N_DEV, 8, 384)
    cw_full = jnp.transpose(cw_all[:, :3, :352], (1, 0, 2)).reshape(3, D_FF)
    cw8 = _pad_rows(cw_full, 8)
    oa, lse_a = _banded_fwd(zw, attn_sink, name="win_attn_fwd", **_WIN_CFG)
    og_views, lg_views = [], []
    for gi in range(3):
        o_g, l_g = _banded_fwd(zg[gi], None, name=f"dil_attn_fwd{gi}", **{**_dil_cfg(gi), "tq": 2 * ATTN_TQ})
        og_views.append(o_g)
        lg_views.append(l_g)
    mixed, ob_views, lb_views = _mix_norm_fwd(oa, og_views, lg_views, g_win, g_dil)
    _, land = _exchange_wait(w_started, mixed, gather=True, name="weight_gather_wait")
    gathered = land.reshape(N_DEV, late_r, D_MODEL)
    full = {}
    off = 0
    for n, r in late_rows:
        full[n] = gathered[:, off:off + r, :].reshape(N_DEV * r, D_MODEL)
        off += r
    mix, h1, h1b = _mm_ln_fwd(mixed, full["w_mix_out"], h0, ln1_g, ln1_b, ALPHA, "mm_mix_out_ln1")
    _, mem_nb = _ln_fwd(mems, None, mem_ln_g, mem_ln_b, 1.0, "mem_ln_fwd")
    kx = _mm(mem_nb, full["w_xk"], trans_b=False, out_dtype=BF16, name="mm_xk")
    vx = _mm(mem_nb, full["w_xv"], trans_b=False, out_dtype=BF16, name="mm_xv")
    qx = _mm(h1b, full["w_xq"], trans_b=False, out_dtype=BF16, name="mm_xq")
    ox, oxb = _xattn_fwd(qx, kx, vx)
    xa, h2, h2b = _mm_ln_fwd(oxb, full["w_xo"], h1, ln2_g, ln2_b, ALPHA, "mm_xo_ln2")
    gate, up, gc, act = _ffn_fwd(h2b, full["w_gate"], full["w_up"], cw8, conv_b)

    du3, du3b, d_ln3_g, d_ln3_b, loss_local = _mm_ln_loss(act, full["w_down"], h2, target, ln3_g, ln3_b, ALPHA,
                                                          "mm_down_ln3_loss")
    dw_down = _mm_tn(act, du3b, name="mm_dw_down")
    dgate, dup, dcw8, d_conv_b = _ffn_bwd(du3b, full["w_down"], gate, gc, up, cw8)
    dw_gate_t = _mm_tn(dgate, h2b, name="mm_dw_gate")
    dw_up_t = _mm_tn(dup, h2b, name="mm_dw_up")
    rows_of = dict(PACK_ROWS)

    own_f32 = {}

    def start_grad_exchange(parts, name, payload=F32):
        gp = jnp.concatenate([g.reshape(N_DEV, rows_of[n], D_MODEL) for n, g in parts], axis=1)
        if payload != F32:
            own_f32[name] = lax.dynamic_index_in_dim(gp, me_lin, axis=0, keepdims=False)
            gp = gp.astype(payload)
        return _exchange_start(gp, (N_DEV - 1,) + gp.shape[1:], gather=False, name=name)

    ffn_parts = (("w_gate", dw_gate_t), ("w_up", dw_up_t), ("w_down", dw_down))
    ffn_started = start_grad_exchange(ffn_parts, "grad_start_ffn")
    du2, du2b, d_ln2_g, d_ln2_b = _mm_ln_bwd(((dgate, full["w_gate"], False), (dup, full["w_up"], False)), du3, ALPHA,
                                             h1, xa, ln2_g + ffn_started[4][0, 0], ALPHA, "mm_dh2_ln2_bwd")
    dox = _mm(du2b, full["w_xo"], trans_b=True, out_dtype=F32, name="mm_d_ox")
    dw_xo = _mm_tn(oxb, du2b, name="mm_dw_xo")
    dqx, dkx, dvx = _xattn_bwd(qx, kx, vx, ox, dox)
    dw_xq = _mm_tn(h1b, dqx, name="mm_dw_xq")
    dw_xk = _mm_tn(mem_nb, dkx, name="mm_dw_xk")
    dw_xv = _mm_tn(mem_nb, dvx, name="mm_dw_xv")
    _, _, d_mem_ln_g, d_mem_ln_b = _mm_ln_bwd(((dkx, full["w_xk"], True), (dvx, full["w_xv"], True)), None, 1.0, mems,
                                              None, mem_ln_g, 1.0, "mm_dmem_ln_bwd")
    du1, du1b, d_ln1_g, d_ln1_b = _mm_ln_bwd(((dqx, full["w_xq"], True),), du2, ALPHA, h0, mix, ln1_g, ALPHA,
                                             "mm_dh1_ln1_bwd")
    dmixed = _mm(du1b, full["w_mix_out"], trans_b=True, out_dtype=F32, name="mm_d_mixed")
    dw_mix_out = _mm_tn(mixed, du1b, name="mm_dw_mix_out")
    attn_parts = (("w_mix_out", dw_mix_out), ("w_xq", dw_xq), ("w_xk", dw_xk), ("w_xv", dw_xv), ("w_xo", dw_xo))
    attn_started = start_grad_exchange(attn_parts, "grad_start_attn")
    doa, dob_views, d_g_win, d_g_dil = _mix_norm_bwd(oa, ob_views[0], dmixed, g_win + attn_started[4][0, 0], g_dil)
    dqa, dkva, dsink8 = _banded_bwd(zw, oa, lse_a, doa, tabs[0], attn_sink, name="win_attn_bwd", **_WIN_CFG)
    dq_views, dkv_views = [], []
    for gi in range(3):
        dq_g, dkv_g = _banded_bwd(zg[gi], ob_views[gi], lb_views[gi], dob_views[gi], tabs[gi], None,
                                  name=f"dil_attn_bwd{gi}", **{**_dil_cfg(gi), "rc": ATTN_TQ})
        dq_views.append(dq_g)
        dkv_views.append(dkv_g)
    dz = _dz_assemble(dq_views, dkv_views, dqa, dkva)
    dw_in_t = _mm_tn(dz, h0b, name="mm_dw_in")
    in_parts = (("w_in", dw_in_t),)
    in_started = start_grad_exchange(in_parts, "grad_start_in", payload=BF16)
    dx, _, d_ln_in_g, d_ln_in_b = _mm_ln_bwd(((dz, w_in_full, False),), du1, ALPHA, xs, None, row(ln_in_g), 1.0,
                                             "mm_dh0_ln_in_bwd", after=in_started[4])

    grads, delta, new_m, new_v = {}, {}, {}, {}
    after = dx
    for parts, started, tag in ((ffn_parts, ffn_started, "ffn"), (attn_parts, attn_started, "attn"),
                                (in_parts, in_started, "in")):
        gp_thru, land = _exchange_wait(started, after, gather=False, name=f"grad_wait_{tag}")
        own = own_f32.get(f"grad_start_{tag}")
        if own is None:
            own = lax.dynamic_index_in_dim(gp_thru, me_lin, axis=0, keepdims=False)
        gsum = _sum_parts(own, land, f"grad_sum_{tag}")
        off = 0
        for n, _ in parts:
            blk = gsum[off:off + rows_of[n]]
            off += rows_of[n]
            grads[n] = (blk.T if n in ("w_in", "w_gate", "w_up") else blk)[None]
            shp = weights[n].shape
            d_, m_, v_ = _adamw(weights[n].reshape(shp[1:]), grads[n].reshape(shp[1:]), mom_m[n].reshape(shp[1:]),
                                mom_v[n].reshape(shp[1:]), f"adamw_{n}")
            delta[n], new_m[n], new_v[n] = d_.reshape(shp), m_.reshape(shp), v_.reshape(shp)
            after = d_

    small = jnp.concatenate([
        d_ln_in_g, d_ln_in_b, d_ln1_g, d_ln1_b, d_mem_ln_g, d_mem_ln_b, d_ln2_g, d_ln2_b, d_ln3_g, d_ln3_b,
        jnp.concatenate([d_g_win, d_g_dil], axis=1),
        jnp.pad(d_conv_b, ((0, 0), (0, 3072 - D_FF))).reshape(3, 1024),
        jnp.pad(dsink8[0:1, :], ((0, 0), (0, 1024 - 128))),
        jnp.pad(dcw8[0:3], ((0, 0), (0, 3072 - D_FF))).reshape(9, 1024),
    ], axis=0)
    _, ssum = _all_gather_rows(small, name="small_grad_all_reduce", in_vmem=True, sum_rows=True, after=after)
    names10 = ["ln_in_g", "ln_in_b", "ln1_g", "ln1_b", "mem_ln_g", "mem_ln_b", "ln2_g", "ln2_b", "ln3_g", "ln3_b"]
    small_g = {n: ssum[i:i + 1] for i, n in enumerate(names10)}
    small_g["g_win"] = ssum[10:11, :512]
    small_g["g_dil"] = ssum[10:11, 512:]
    small_g["conv_b"] = ssum[11:14].reshape(1, 3072)[:, :D_FF]
    small_g["attn_sink"] = ssum[14:15, :8]
    small_g["conv_w"] = lax.dynamic_slice_in_dim(ssum[15:24].reshape(3, 3072)[:, :D_FF], me_lin * 352, 352, axis=1)

    small_names = [n for n in order if n not in rows_of]
    two_d = lambda a: a.reshape(-1, a.shape[-1])
    d_s, m_s, v_s = _adamw_many([two_d(weights[n]) for n in small_names], [small_g[n] for n in small_names],
                                [two_d(mom_m[n]) for n in small_names], [two_d(mom_v[n]) for n in small_names],
                                "adamw_small")
    for k, n in enumerate(small_names):
        shp = weights[n].shape
        grads[n], delta[n], new_m[n], new_v[n] = (t.reshape(shp) for t in (small_g[n], d_s[k], m_s[k], v_s[k]))

    loss = lax.psum(loss_local[0, 0], MESH_AXES)
    return (loss, dx[None], *[grads[n] for n in order], *[delta[n] for n in order], *[new_m[n] for n in order],
            *[new_v[n] for n in order])
```

```python
import functools
import math

import jax
import jax.numpy as jnp
from jax import lax
from jax.experimental import pallas as pl
from jax.experimental.pallas import tpu as pltpu

F32 = jnp.float32
BF16 = jnp.bfloat16
SDS = jax.ShapeDtypeStruct
_PALLAS_CALL = pl.pallas_call

D_MODEL = 1024
HEAD_DIM = 64
WIN_HALF = 128
DIL_PAIRS = ((128, 1), (512, 4), (2048, 16))
DIL_SIDE = 64
ROT_DIM = 16
ROPE_THETA = 500000.0
MEM_LEN = 256
X_HEADS = 4
X_HEAD_DIM = 256
D_FF = 2816
IN_WIDTH = 5376
Z_QB, Z_KB, Z_VB, Z_QA, Z_KA, Z_VA = 0, 1536, 3072, 4608, 5120, 5248
W_IN_QA, W_IN_KA, W_IN_QB = 0, 512, 768
ALPHA = (2.0) ** 0.25
LN_EPS = 1e-5
NEG_INF = -1e30
ADAM_LR, ADAM_B1, ADAM_B2, ADAM_EPS, ADAM_WD, ADAM_STEP = 0.001, 0.9, 0.999, 1e-08, 0.01, 10
N_DEV = 8
MESH_AXES = ("x", "y", "c")
VMEM_LIMIT_BYTES = 52 * 1024 * 1024
ATTN_TQ = 256
WHOLE_CLASS_BYTES = 4 * 1024 * 1024
TABW = 384

PACK_ROWS = (("w_in", 672), ("w_gate", 352), ("w_up", 352), ("w_mix_out", 128), ("w_xq", 128), ("w_xk", 128),
             ("w_xv", 128), ("w_xo", 128), ("w_down", 352))
SMALL_ROWS = 24


def _pick(n, cands):
    for c in cands:
        if n % c == 0:
            return c
    return n


def _pcall(body, *, name, out_shape, grid=None, in_specs=None, out_specs=None, scratch_shapes=(), dims=None,
           aliases=None):
    kw = {}
    if grid is not None:
        kw["grid"] = grid
    if in_specs is not None:
        kw["in_specs"] = in_specs
    if out_specs is not None:
        kw["out_specs"] = out_specs
    if aliases:
        kw["input_output_aliases"] = aliases
    return _PALLAS_CALL(
        body, name=name, out_shape=out_shape, scratch_shapes=list(scratch_shapes),
        compiler_params=pltpu.CompilerParams(dimension_semantics=dims, vmem_limit_bytes=VMEM_LIMIT_BYTES), **kw)


MM_VMEM_BUDGET = 40 * 1024 * 1024


def _mm(a, b, *, trans_b, out_dtype, name, addends=(), coefs=(), after=None, more=()):
    pairs = ((a, b, trans_b),) + tuple(more)
    m = a.shape[0]
    n = b.shape[0] if trans_b else b.shape[1]
    n_add = len(addends)
    extra = [] if after is None else [after]
    out_bytes = jnp.dtype(out_dtype).itemsize

    def vmem(tm, tn):
        tot = tm * tn * (out_bytes + 4 * n_add)
        for pa, pb, _ in pairs:
            tot += tm * pa.shape[1] * pa.dtype.itemsize + pa.shape[1] * tn * pb.dtype.itemsize
        return 2 * tot

    tm, tn = next(((cm, cn) for cn in (n, 1408, 1024, 512, 256, 128) if n % cn == 0
                   for cm in (1024, 512, 256, 128) if m % cm == 0 and vmem(cm, cn) <= MM_VMEM_BUDGET))
    n_pairs = len(pairs)

    def body(*refs):
        o_ref = refs[2 * n_pairs + n_add + len(extra)]
        acc = None
        for p, (_, _, tb) in enumerate(pairs):
            dn = _NT if tb else _NN
            part = lax.dot_general(refs[2 * p][...].astype(BF16), refs[2 * p + 1][...].astype(BF16), dn,
                                   preferred_element_type=F32)
            acc = part if acc is None else acc + part
        for r_ref, c in zip(refs[2 * n_pairs:2 * n_pairs + n_add], coefs):
            acc = acc + (r_ref[...] if c == 1.0 else c * r_ref[...])
        o_ref[...] = acc.astype(out_dtype)

    in_specs, args = [], []
    for pa, pb, tb in pairs:
        k = pa.shape[1]
        in_specs.append(pl.BlockSpec((tm, k), lambda j, i: (i, 0)))
        in_specs.append(pl.BlockSpec((tn, k), lambda j, i: (j, 0)) if tb else pl.BlockSpec((k, tn), lambda j, i: (0, j)))
        args += [pa, pb]
    in_specs += [pl.BlockSpec((tm, tn), lambda j, i: (i, j)) for _ in addends]
    in_specs += [pl.BlockSpec((8, 128), lambda j, i: (0, 0)) for _ in extra]
    return _pcall(body, name=name, out_shape=SDS((m, n), out_dtype), grid=(n // tn, m // tm), in_specs=in_specs,
                  out_specs=pl.BlockSpec((tm, tn), lambda j, i: (i, j)),
                  dims=("parallel", "parallel"))(*args, *addends, *extra)


def _mm_tn(a, b, *, name):
    s, m = a.shape
    n = b.shape[1]
    tm = _pick(m, (768, 1408, 1024, 512, 256, 128))
    tk = _pick(s, (1024, 512, 256))
    nk = s // tk

    def body(a_ref, b_ref, o_ref, acc_ref):
        kk = pl.program_id(1)

        @pl.when(kk == 0)
        def _():
            acc_ref[...] = jnp.zeros_like(acc_ref)

        acc_ref[...] += lax.dot_general(a_ref[...].astype(BF16), b_ref[...].astype(BF16), (((0,), (0,)), ((), ())),
                                        preferred_element_type=F32)

        @pl.when(kk == nk - 1)
        def _():
            o_ref[...] = acc_ref[...]

    return _pcall(body, name=name, out_shape=SDS((m, n), F32), grid=(m // tm, nk),
                  in_specs=[pl.BlockSpec((tk, tm), lambda i, kk: (kk, i)), pl.BlockSpec((tk, n), lambda i, kk: (kk, 0))],
                  out_specs=pl.BlockSpec((tm, n), lambda i, kk: (i, 0)), scratch_shapes=[pltpu.VMEM((tm, n), F32)],
                  dims=("parallel", "arbitrary"))(a, b)


def _rope_lane_consts():
    lane = jnp.arange(128)
    j = lane % HEAD_DIM
    inv_freq = ROPE_THETA ** (-jnp.arange(0, ROT_DIM, 2, dtype=F32) / ROT_DIM)
    freq = jnp.where(j < ROT_DIM, inv_freq[j % (ROT_DIM // 2)], 0.0).astype(F32)
    lo = (j < ROT_DIM // 2).astype(F32)
    hi = ((j >= ROT_DIM // 2) & (j < ROT_DIM)).astype(F32)
    return jnp.stack([freq, lo, hi] + [jnp.zeros((128,), F32)] * 5)


def _to_classes(x, scr, d):
    if d == 1:
        return [x]
    scr[...] = x
    return [scr[pl.ds(c, x.shape[0] // d, stride=d), :] for c in range(d)]


def _from_classes(parts, scr):
    d = len(parts)
    if d == 1:
        return parts[0]
    for c, part in enumerate(parts):
        scr[pl.ds(c, part.shape[0], stride=d), :] = part
    return scr[...]


DILATIONS = tuple(d for _, d in DIL_PAIRS)


def _rope_tables(posf):
    s = posf.shape[0]
    tm = _pick(s, (1024, 512))

    def body(p_ref, c_ref, *rest):
        o_refs, scr = rest[:-1], rest[-1]
        ang = p_ref[...] * c_ref[0:1, :]
        lo = c_ref[1:2, :]
        hi = c_ref[2:3, :]
        cs = jnp.cos(ang)
        sn = jnp.sin(ang)
        for q, t in enumerate((jnp.where(lo + hi > 0.0, cs, 1.0), -sn * lo, sn * hi)):
            for o_ref, d in zip(o_refs, DILATIONS):
                for c, part in enumerate(_to_classes(t, scr, d)):
                    o_ref[:, c * TABW + q * 128:c * TABW + (q + 1) * 128] = part

    return _pcall(body, name="rope_tables", out_shape=tuple(SDS((s // d, d * TABW), F32) for d in DILATIONS),
                  grid=(s // tm,),
                  in_specs=[pl.BlockSpec((tm, 1), lambda i: (i, 0)), pl.BlockSpec((8, 128), lambda i: (0, 0))],
                  out_specs=tuple(pl.BlockSpec((tm // d, d * TABW), lambda i: (i, 0)) for d in DILATIONS),
                  scratch_shapes=[pltpu.VMEM((tm, 128), F32)], dims=("parallel",))(posf, _rope_lane_consts())


def _rope_apply(x, tab, sign):
    w = x.shape[1]
    rep = w // 128
    c = jnp.tile(tab[:, 0:128], (1, rep)) if rep > 1 else tab[:, 0:128]
    a = jnp.tile(tab[:, 128:256], (1, rep)) if rep > 1 else tab[:, 128:256]
    b = jnp.tile(tab[:, 256:384], (1, rep)) if rep > 1 else tab[:, 256:384]
    up = pltpu.roll(x, w - 8, 1)
    dn = pltpu.roll(x, 8, 1)
    if sign > 0:
        return x * c + up * a + dn * b
    return x * c - up * a - dn * b


def _proj_rope(h0b, w_t, tab):
    s = h0b.shape[0]
    tm = _pick(s, (512,))
    tn = 256

    def body(a_ref, w_ref, t_ref, zw_ref, z0_ref, z1_ref, z2_ref, scr):
        z_refs = (z0_ref, z1_ref, z2_ref)
        a = a_ref[...]
        tabv = t_ref[...]
        for c0 in range(0, IN_WIDTH, tn):
            w0 = (c0 + W_IN_QB) % IN_WIDTH
            z = lax.dot_general(a, w_ref[w0:w0 + tn, :], _NT, preferred_element_type=F32)
            for g0 in range(c0, c0 + tn, 128):
                zg = z[:, g0 - c0:g0 - c0 + 128]
                if g0 < Z_VB or Z_QA <= g0 < Z_VA:
                    zg = _rope_apply(zg, tabv, 1)
                if g0 >= Z_QA:
                    zw_ref[:, g0 - Z_QA:g0 - Z_QA + 128] = zg.astype(BF16)
                    continue
                kind, within = divmod(g0, 1536)
                grp, off = divmod(within, 512)
                col = kind * 512 + off
                for c, part in enumerate(_to_classes(zg, scr, DILATIONS[grp])):
                    z_refs[grp][:, c * 1536 + col:c * 1536 + col + 128] = part.astype(BF16)

    return _pcall(body, name="proj_rope",
                  out_shape=(SDS((s, 768), BF16),) + tuple(SDS((s // d, d * 1536), BF16) for d in DILATIONS),
                  grid=(s // tm,),
                  in_specs=[pl.BlockSpec((tm, D_MODEL), lambda i: (i, 0)), pl.BlockSpec((IN_WIDTH, D_MODEL), lambda i: (0, 0)),
                            pl.BlockSpec((tm, TABW), lambda i: (i, 0))],
                  out_specs=(pl.BlockSpec((tm, 768), lambda i: (i, 0)),)
                  + tuple(pl.BlockSpec((tm // d, d * 1536), lambda i: (i, 0)) for d in DILATIONS),
                  scratch_shapes=[pltpu.VMEM((tm, 128), F32)], dims=("parallel",))(h0b, w_t, tab)


def _band_specs(sd, blk, tq, width, per_tok, cb):
    r = tq // blk
    nbk = sd // blk
    prev = pl.BlockSpec((blk, width), lambda c, j: (jnp.maximum(j * r - 1, 0), c * per_tok + cb))
    cur = pl.BlockSpec((tq, width), lambda c, j: (j, c * per_tok + cb))
    nxt = pl.BlockSpec((blk, width), lambda c, j: (jnp.minimum((j + 1) * r, nbk - 1), c * per_tok + cb))
    return [prev, cur, nxt]


def _band_bias(q0, rows, blk, sd):
    shape = (rows, rows + 2 * blk)
    qpos = q0 + lax.broadcasted_iota(jnp.int32, shape, 0)
    kpos = q0 - blk + lax.broadcasted_iota(jnp.int32, shape, 1)
    ok = (jnp.abs(qpos - kpos) <= blk) & (kpos >= 0) & (kpos < sd)
    return jnp.where(ok, 0.0, NEG_INF)


_NT = (((1,), (1,)), ((), ()))
_NN = (((1,), (0,)), ((), ()))
_TN = (((0,), (0,)), ((), ()))


def _banded_fwd(zv, sink, *, d, blk, tq, rc, ptw, qw, kw, qcb, kcb, vcb, pairs, name):
    sd = zv.shape[0]
    tq = min(tq, sd)
    rc = min(rc, tq)
    has_sink = sink is not None
    scale = HEAD_DIM ** -0.5

    def body(q_ref, kp, kc, kn, vp, vc, vn, *rest):
        if has_sink:
            sink_ref, o_ref, lse_ref = rest
        else:
            o_ref, lse_ref = rest
        j = pl.program_id(1)
        q = q_ref[...] * scale
        k = jnp.concatenate([kp[...], kc[...], kn[...]], axis=0)
        v = jnp.concatenate([vp[...], vc[...], vn[...]], axis=0)
        biases = {r0: _band_bias(j * tq + r0, rc, blk, sd) for r0 in range(0, tq, rc)}
        low = lax.broadcasted_iota(jnp.int32, (1, 128), 1) < HEAD_DIM
        for qb, kb, vb, swaps, sinks in pairs:
            qp, kp_, vp_ = q[:, qb:qb + 128], k[:, kb:kb + 128], v[:, vb:vb + 128]
            if any(swaps):
                k_sw = jnp.concatenate([kp_[:, HEAD_DIM:], kp_[:, :HEAD_DIM]], axis=1)
                v_sw = jnp.concatenate([vp_[:, HEAD_DIM:], vp_[:, :HEAD_DIM]], axis=1)
            for r0 in range(0, tq, rc):
                outs, lses = [], []
                for half in range(2):
                    qm = jnp.where(low if half == 0 else ~low, qp[r0:r0 + rc], jnp.zeros((rc, 128), BF16))
                    kk, vv = (k_sw, v_sw) if swaps[half] else (kp_, vp_)
                    kk, vv = kk[r0:r0 + rc + 2 * blk], vv[r0:r0 + rc + 2 * blk]
                    sc = lax.dot_general(qm, kk, _NT, preferred_element_type=F32) + biases[r0]
                    m = jnp.max(sc, axis=-1, keepdims=True)
                    if has_sink:
                        m = jnp.maximum(m, sink_ref[0, sinks[half]])
                    p = jnp.exp(sc - m)
                    den = jnp.sum(p, axis=-1, keepdims=True)
                    if has_sink:
                        den = den + jnp.exp(sink_ref[0, sinks[half]] - m)
                    outs.append(lax.dot_general(p.astype(BF16), vv, _NN, preferred_element_type=F32) / den)
                    lses.append(m + jnp.log(den))
                o_ref[r0:r0 + rc, qb:qb + 128] = jnp.where(low, outs[0], outs[1])
                lse_ref[r0:r0 + rc, qb:qb + 128] = jnp.where(low, lses[0], lses[1])

    in_specs = ([pl.BlockSpec((tq, qw), lambda c, j: (j, c * (ptw // qw) + qcb))]
                + _band_specs(sd, blk, tq, kw, ptw // kw, kcb) + _band_specs(sd, blk, tq, kw, ptw // kw, vcb))
    args = [zv] * 7
    if has_sink:
        in_specs.append(pl.BlockSpec(memory_space=pltpu.SMEM))
        args.append(sink)
    o_spec = pl.BlockSpec((tq, qw), lambda c, j: (j, c))
    return _pcall(body, name=name, out_shape=(SDS((sd, d * qw), F32), SDS((sd, d * qw), F32)), grid=(d, sd // tq),
                  in_specs=in_specs, out_specs=(o_spec, o_spec), dims=("parallel", "parallel"))(*args)


def _banded_bwd(zv, ov, lv, dov, tv, sink, *, d, blk, tq, rc, ptw, qw, kw, qcb, kcb, vcb, pairs, name):
    sd = zv.shape[0]
    tq = min(tq, sd)
    nt = sd // tq
    r = tq // blk
    nbk = sd // blk
    has_sink = sink is not None
    scale = HEAD_DIM ** -0.5
    rc = min(rc, tq)
    kvw = 128 * len({kb for _, kb, _, _, _ in pairs})
    whole_class = sd * 2 * kvw * 2 <= WHOLE_CLASS_BYTES

    def add_rows(x, y, last):
        if tq == blk:
            return x + y
        if last:
            return jnp.concatenate([x[:tq - blk], x[tq - blk:] + y], axis=0)
        return jnp.concatenate([x[:blk] + y, x[blk:]], axis=0)

    def body(q_ref, kp, kc, kn, vp, vc, vn, o_ref, l_ref, do_ref, t_ref, tlag_ref, *rest):
        if has_sink:
            sink_ref, dq_ref, dkv_ref, dsink_ref, acck, accv, nxtk, nxtv = rest
        else:
            dq_ref, dkv_ref, acck, accv, nxtk, nxtv = rest
        j = pl.program_id(1)

        @pl.when(j == 0)
        def _():
            nxtk[...] = jnp.zeros_like(nxtk)
            nxtv[...] = jnp.zeros_like(nxtv)

        if has_sink:
            @pl.when((pl.program_id(0) == 0) & (j == 0))
            def _():
                dsink_ref[...] = jnp.zeros_like(dsink_ref)

        def emit(dk_rows, dv_rows, tabv, tile_idx):
            val = jnp.concatenate([_rope_apply(dk_rows, tabv, -1), dv_rows], axis=1).astype(BF16)
            if whole_class:
                dkv_ref[pl.ds(pl.multiple_of(tile_idx * tq, tq), tq), :] = val
            else:
                dkv_ref[...] = val

        @pl.when(j < nt)
        def _():
            q = q_ref[...] * scale
            k3 = jnp.concatenate([kp[...], kc[...], kn[...]], axis=0)
            v3 = jnp.concatenate([vp[...], vc[...], vn[...]], axis=0)
            o_t, l_t, do_t = o_ref[...], l_ref[...], do_ref[...]
            biases = {r0: _band_bias(j * tq + r0, rc, blk, sd) for r0 in range(0, tq, rc)}
            lane = lax.broadcasted_iota(jnp.int32, (1, 128), 1)
            low = lane < HEAD_DIM
            wide = tq + 2 * blk
            cw = rc + 2 * blk

            def place(x, r0):
                parts = ([jnp.zeros((r0, 128), F32)] if r0 else []) + [x]
                if wide - r0 - cw:
                    parts.append(jnp.zeros((wide - r0 - cw, 128), F32))
                return jnp.concatenate(parts, axis=0) if len(parts) > 1 else x

            dqs = []
            wks, wvs = {}, {}
            dsink_row = jnp.zeros((1, 128), F32)
            for qb, kb, vb, swaps, sinks in pairs:
                qp, kp_, vp_ = q[:, qb:qb + 128], k3[:, kb:kb + 128], v3[:, vb:vb + 128]
                if any(swaps):
                    k_sw = jnp.concatenate([kp_[:, HEAD_DIM:], kp_[:, :HEAD_DIM]], axis=1)
                    v_sw = jnp.concatenate([vp_[:, HEAD_DIM:], vp_[:, :HEAD_DIM]], axis=1)
                dop, lp = do_t[:, qb:qb + 128], l_t[:, qb:qb + 128]
                prod = dop * o_t[:, qb:qb + 128]
                dq_rows = []
                for r0 in range(0, tq, rc):
                    rows = slice(r0, r0 + rc)
                    dq_half = []
                    for half in range(2):
                        mine = low if half == 0 else ~low
                        qm = jnp.where(mine, qp[rows], jnp.zeros((rc, 128), BF16))
                        dob = jnp.where(mine, dop[rows], 0.0).astype(BF16)
                        delta = jnp.sum(jnp.where(mine, prod[rows], 0.0), axis=-1, keepdims=True)
                        lse = lp[rows, half * HEAD_DIM:half * HEAD_DIM + 1]
                        kk, vv = (k_sw, v_sw) if swaps[half] else (kp_, vp_)
                        kk, vv = kk[r0:r0 + cw], vv[r0:r0 + cw]
                        sc = lax.dot_general(qm, kk, _NT, preferred_element_type=F32) + biases[r0]
                        p = jnp.exp(sc - lse)
                        dp = lax.dot_general(dob, vv, _NT, preferred_element_type=F32)
                        dsb = (p * (dp - delta)).astype(BF16)
                        dq_half.append(lax.dot_general(dsb, kk, _NN, preferred_element_type=F32))
                        dk = lax.dot_general(dsb, qm, _TN, preferred_element_type=F32)
                        dv = lax.dot_general(p.astype(BF16), dob, _TN, preferred_element_type=F32)
                        if swaps[half]:
                            dk, dv = pltpu.roll(dk, HEAD_DIM, 1), pltpu.roll(dv, HEAD_DIM, 1)
                        wks[kb] = place(dk, r0) if kb not in wks else wks[kb] + place(dk, r0)
                        wvs[vb] = place(dv, r0) if vb not in wvs else wvs[vb] + place(dv, r0)
                        if has_sink:
                            psink = jnp.exp(sink_ref[0, sinks[half]] - lse)
                            dsink_row = dsink_row + jnp.where(lane == sinks[half], -jnp.sum(psink * delta), 0.0)
                    dq_rows.append(jnp.where(low, dq_half[0], dq_half[1]) * scale)
                dqs.append(jnp.concatenate(dq_rows, axis=0) if len(dq_rows) > 1 else dq_rows[0])
            dq_ref[...] = _rope_apply(jnp.concatenate(dqs, axis=1), t_ref[...], -1).astype(BF16)
            wk = jnp.concatenate([wks[b] for b in sorted(wks)], axis=1) if len(wks) > 1 else wks[min(wks)]
            wv = jnp.concatenate([wvs[b] for b in sorted(wvs)], axis=1) if len(wvs) > 1 else wvs[min(wvs)]
            if has_sink:
                dsink_ref[0:1, :] += dsink_row

            @pl.when(j > 0)
            def _():
                emit(add_rows(acck[...], wk[:blk], True), add_rows(accv[...], wv[:blk], True), tlag_ref[...], j - 1)

            acck[...] = add_rows(wk[blk:blk + tq], nxtk[...], False)
            accv[...] = add_rows(wv[blk:blk + tq], nxtv[...], False)
            nxtk[...] = wk[blk + tq:]
            nxtv[...] = wv[blk + tq:]
            if whole_class:
                @pl.when(j == nt - 1)
                def _():
                    emit(acck[...], accv[...], t_ref[...], j)

        if not whole_class:
            @pl.when(j == nt)
            def _():
                emit(acck[...], accv[...], tlag_ref[...], j - 1)

    def tile(width, per_tok, cb):
        return pl.BlockSpec((tq, width), lambda c, j: (jnp.minimum(j, nt - 1), c * per_tok + cb))

    def halos(width, per_tok, cb):
        before = pl.BlockSpec((blk, width), lambda c, j: (jnp.maximum(jnp.minimum(j, nt - 1) * r - 1, 0), c * per_tok + cb))
        after = pl.BlockSpec((blk, width),
                             lambda c, j: (jnp.minimum((jnp.minimum(j, nt - 1) + 1) * r, nbk - 1), c * per_tok + cb))
        return [before, tile(width, per_tok, cb), after]

    def lagged(width):
        return pl.BlockSpec((tq, width), lambda c, j: (jnp.maximum(j - 1, 0), c))

    in_specs = ([tile(qw, ptw // qw, qcb)] + halos(kw, ptw // kw, kcb) + halos(kw, ptw // kw, vcb)
                + [tile(qw, 1, 0)] * 3 + [tile(TABW, 1, 0), lagged(TABW)])
    args = [zv] * 7 + [ov, lv, dov, tv, tv]
    out_shape = [SDS((sd, d * qw), BF16), SDS((sd, d * 2 * kvw), BF16)]
    out_specs = [tile(qw, 1, 0),
                 pl.BlockSpec((sd, 2 * kvw), lambda c, j: (0, c)) if whole_class else lagged(2 * kvw)]
    if has_sink:
        in_specs.append(pl.BlockSpec(memory_space=pltpu.SMEM))
        args.append(sink)
        out_shape.append(SDS((8, 128), F32))
        out_specs.append(pl.BlockSpec((8, 128), lambda c, j: (0, 0)))
    scratch = [pltpu.VMEM((tq, kvw), F32), pltpu.VMEM((tq, kvw), F32), pltpu.VMEM((blk, kvw), F32),
               pltpu.VMEM((blk, kvw), F32)]
    return _pcall(body, name=name, out_shape=tuple(out_shape), grid=(d, nt + (0 if whole_class else 1)), in_specs=in_specs,
                  out_specs=tuple(out_specs), scratch_shapes=scratch, dims=("arbitrary", "arbitrary"))(*args)


_WIN_PAIRS = tuple((128 * p, 0, 128, (False, True) if p < 2 else (True, False), (2 * p, 2 * p + 1)) for p in range(4))
_WIN_CFG = dict(d=1, blk=WIN_HALF, tq=ATTN_TQ, rc=256, ptw=768, qw=512, kw=256, qcb=0, kcb=2, vcb=2, pairs=_WIN_PAIRS)
_DIL_PAIRS = tuple((128 * p, 128 * p, 128 * p, (False, False), (2 * p, 2 * p + 1)) for p in range(4))


def _dil_cfg(gi):
    return dict(d=DILATIONS[gi], blk=DIL_SIDE, tq=ATTN_TQ, rc=128, ptw=1536, qw=512, kw=512, qcb=0, kcb=1, vcb=2,
                pairs=_DIL_PAIRS)


def _view_specs(tm, width):
    return tuple(pl.BlockSpec((tm // d, d * width), lambda i: (i, 0)) for d in DILATIONS)


def _mix_norm_fwd(oa, og_views, lg_views, g_win, g_dil):
    s = oa.shape[0]
    tm = _pick(s, (512,))

    def body(oa_ref, o0, o1, o2, l0, l1, l2, gw_ref, gd_ref, mixed_ref, ob0, ob1, ob2, lb0, lb1, lb2, scr, ob_s):
        o_refs, l_refs, ob_refs, lb_refs = (o0, o1, o2), (l0, l1, l2), (ob0, ob1, ob2), (lb0, lb1, lb2)
        ssq = jnp.zeros((tm, 1), F32)
        for q in range(4):
            os_, ls_ = [], []
            for g, d in enumerate(DILATIONS):
                cols = [slice(c * 512 + q * 128, c * 512 + (q + 1) * 128) for c in range(d)]
                os_.append(_from_classes([o_refs[g][:, cs] for cs in cols], scr))
                ls_.append(_from_classes([l_refs[g][:, cs] for cs in cols], scr))
            mx = jnp.maximum(jnp.maximum(ls_[0], ls_[1]), ls_[2])
            es = [jnp.exp(l - mx) for l in ls_]
            den = es[0] + es[1] + es[2]
            ob = (es[0] / den) * os_[0] + (es[1] / den) * os_[1] + (es[2] / den) * os_[2]
            lb = mx + jnp.log(den)
            ob_s[:, q * 128:(q + 1) * 128] = ob
            ssq = ssq + jnp.sum(ob * ob, axis=-1, keepdims=True)
            for g, d in enumerate(DILATIONS):
                for val, refs in ((ob, ob_refs), (lb, lb_refs)):
                    for c, part in enumerate(_to_classes(val, scr, d)):
                        refs[g][:, c * 512 + q * 128:c * 512 + (q + 1) * 128] = part
        a = oa_ref[...]
        ra = lax.rsqrt(jnp.mean(a * a, axis=-1, keepdims=True) + LN_EPS)
        rb = lax.rsqrt(ssq * (1.0 / 512) + LN_EPS)
        mixed_ref[...] = jnp.concatenate([a * ra * gw_ref[...], ob_s[...] * rb * gd_ref[...]], axis=1).astype(BF16)

    row = pl.BlockSpec((tm, 512), lambda i: (i, 0))
    vec = pl.BlockSpec((1, 512), lambda i: (0, 0))
    views = _view_specs(tm, 512)
    view_shapes = tuple(SDS((s // d, d * 512), F32) for d in DILATIONS)
    res = _pcall(body, name="mix_norm_fwd", out_shape=(SDS((s, 1024), BF16),) + view_shapes * 2, grid=(s // tm,),
                 in_specs=[row, *views, *views, vec, vec],
                 out_specs=(pl.BlockSpec((tm, 1024), lambda i: (i, 0)),) + views * 2,
                 scratch_shapes=[pltpu.VMEM((tm, 128), F32), pltpu.VMEM((tm, 512), F32)],
                 dims=("parallel",))(oa, *og_views, *lg_views, g_win, g_dil)
    return res[0], res[1:4], res[4:7]


def _mix_norm_bwd(oa, ob, dmixed, g_win, g_dil):
    s = oa.shape[0]
    tm = _pick(s, (512,))
    nt = s // tm

    def body(oa_ref, ob_ref, dm_ref, gw_ref, gd_ref, doa_ref, db0, db1, db2, dgw_ref, dgd_ref, acc_w, acc_d, scr):
        i = pl.program_id(0)

        @pl.when(i == 0)
        def _():
            acc_w[...] = jnp.zeros_like(acc_w)
            acc_d[...] = jnp.zeros_like(acc_d)

        dm = dm_ref[...]
        dxs = []
        for x_ref, g_ref, dy, acc in ((oa_ref, gw_ref, dm[:, :512], acc_w), (ob_ref, gd_ref, dm[:, 512:], acc_d)):
            x = x_ref[...]
            r = lax.rsqrt(jnp.mean(x * x, axis=-1, keepdims=True) + LN_EPS)
            dyg = dy * g_ref[...]
            dxs.append(r * dyg - x * (r * r * r) * jnp.mean(dyg * x, axis=-1, keepdims=True))
            acc[...] += jnp.sum((dy * x * r).reshape(tm // 8, 8, 512), axis=0)
        doa_ref[...] = dxs[0]
        for q in range(4):
            dq = dxs[1][:, q * 128:(q + 1) * 128]
            for db_ref, d in zip((db0, db1, db2), DILATIONS):
                for c, part in enumerate(_to_classes(dq, scr, d)):
                    db_ref[:, c * 512 + q * 128:c * 512 + (q + 1) * 128] = part

        @pl.when(i == nt - 1)
        def _():
            dgw_ref[...] = jnp.sum(acc_w[...], axis=0, keepdims=True)
            dgd_ref[...] = jnp.sum(acc_d[...], axis=0, keepdims=True)

    row = pl.BlockSpec((tm, 512), lambda i: (i, 0))
    vec = pl.BlockSpec((1, 512), lambda i: (0, 0))
    views = _view_specs(tm, 512)
    view_shapes = tuple(SDS((s // d, d * 512), F32) for d in DILATIONS)
    res = _pcall(body, name="mix_norm_bwd",
                 out_shape=(SDS((s, 512), F32),) + view_shapes + (SDS((1, 512), F32), SDS((1, 512), F32)),
                 grid=(nt,), in_specs=[row, row, pl.BlockSpec((tm, 1024), lambda i: (i, 0)), vec, vec],
                 out_specs=(row,) + views + (vec, vec),
                 scratch_shapes=[pltpu.VMEM((8, 512), F32), pltpu.VMEM((8, 512), F32), pltpu.VMEM((tm, 128), F32)],
                 dims=("arbitrary",))(oa, ob, dmixed, g_win, g_dil)
    return res[0], res[1:4], res[4], res[5]


def _dz_assemble(dq_views, dkv_views, dqa, dkva):
    s = dqa.shape[0]
    tm = _pick(s, (512,))

    def body(q0, q1, q2, kv0, kv1, kv2, qa_ref, kva_ref, o_ref, scr):
        for g, d in enumerate(DILATIONS):
            for kind, (ref, width, base) in enumerate((((q0, q1, q2)[g], 512, 0), ((kv0, kv1, kv2)[g], 1024, 0),
                                                       ((kv0, kv1, kv2)[g], 1024, 512))):
                for q in range(4):
                    src = base + q * 128
                    dst = W_IN_QB + kind * 1536 + g * 512 + q * 128
                    if d == 1:
                        o_ref[:, dst:dst + 128] = ref[:, src:src + 128]
                    else:
                        parts = [ref[:, c * width + src:c * width + src + 128].astype(F32) for c in range(d)]
                        o_ref[:, dst:dst + 128] = _from_classes(parts, scr).astype(BF16)
        o_ref[:, W_IN_QA:W_IN_QA + 512] = qa_ref[...]
        o_ref[:, W_IN_KA:W_IN_KA + 256] = kva_ref[...]

    return _pcall(body, name="dz_assemble", out_shape=SDS((s, IN_WIDTH), BF16), grid=(s // tm,),
                  in_specs=[*_view_specs(tm, 512), *_view_specs(tm, 1024), pl.BlockSpec((tm, 512), lambda i: (i, 0)),
                            pl.BlockSpec((tm, 256), lambda i: (i, 0))],
                  out_specs=pl.BlockSpec((tm, IN_WIDTH), lambda i: (i, 0)),
                  scratch_shapes=[pltpu.VMEM((tm, 128), F32)], dims=("parallel",))(*dq_views, *dkv_views, dqa, dkva)


def _ln_fwd(a, r, g, b, ca, name):
    s = a.shape[0]
    tm = _pick(s, (512, 256))
    has_r = r is not None

    def body(*refs):
        a_ref = refs[0]
        r_ref = refs[1] if has_r else None
        g_ref, b_ref, o_ref, ob_ref = refs[1 + has_r:]
        u = a_ref[...] if ca == 1.0 else ca * a_ref[...]
        if has_r:
            u = u + r_ref[...]
        mu = jnp.mean(u, axis=-1, keepdims=True)
        xc = u - mu
        var = jnp.mean(xc * xc, axis=-1, keepdims=True)
        y = xc * lax.rsqrt(var + LN_EPS) * g_ref[...] + b_ref[...]
        o_ref[...] = y
        ob_ref[...] = y.astype(BF16)

    row = pl.BlockSpec((tm, D_MODEL), lambda i: (i, 0))
    vec = pl.BlockSpec((1, D_MODEL), lambda i: (0, 0))
    args = [a] + ([r] if has_r else []) + [g, b]
    return _pcall(body, name=name, out_shape=(SDS((s, D_MODEL), F32), SDS((s, D_MODEL), BF16)), grid=(s // tm,),
                  in_specs=[row] * (1 + has_r) + [vec, vec], out_specs=(row, row), dims=("parallel",))(*args)


def _mm_ln_fwd(a, w, resid, g, b, ca, name):
    s, k = a.shape
    tm = _pick(s, (512, 256))

    def body(a_ref, w_ref, res_ref, g_ref, b_ref, r_ref, o_ref, ob_ref):
        rv = lax.dot_general(a_ref[...], w_ref[...], _NN, preferred_element_type=F32)
        r_ref[...] = rv
        u = ca * res_ref[...] + rv
        mu = jnp.mean(u, axis=-1, keepdims=True)
        xc = u - mu
        var = jnp.mean(xc * xc, axis=-1, keepdims=True)
        y = xc * lax.rsqrt(var + LN_EPS) * g_ref[...] + b_ref[...]
        o_ref[...] = y
        ob_ref[...] = y.astype(BF16)

    row = pl.BlockSpec((tm, D_MODEL), lambda i: (i, 0))
    vec = pl.BlockSpec((1, D_MODEL), lambda i: (0, 0))
    return _pcall(body, name=name, out_shape=(SDS((s, D_MODEL), F32), SDS((s, D_MODEL), F32), SDS((s, D_MODEL), BF16)),
                  grid=(s // tm,),
                  in_specs=[pl.BlockSpec((tm, k), lambda i: (i, 0)), pl.BlockSpec((k, D_MODEL), lambda i: (0, 0)), row, vec, vec],
                  out_specs=(row, row, row), dims=("parallel",))(a, w, resid, g, b)


def _mm_ln_loss(x, w, a, target, g, b, ca, name):
    s, k = x.shape
    tm = _pick(s, (256,))
    nt = s // tm

    def body(x_ref, w_ref, a_ref, t_ref, g_ref, b_ref, du_ref, dub_ref, dg_ref, db_ref, loss_ref, acc_g, acc_b, acc_l):
        i = pl.program_id(0)

        @pl.when(i == 0)
        def _():
            acc_g[...] = jnp.zeros_like(acc_g)
            acc_b[...] = jnp.zeros_like(acc_b)
            acc_l[...] = jnp.zeros_like(acc_l)

        u = ca * a_ref[...] + lax.dot_general(x_ref[...], w_ref[...], _NN, preferred_element_type=F32)
        mu = jnp.mean(u, axis=-1, keepdims=True)
        xc = u - mu
        var = jnp.mean(xc * xc, axis=-1, keepdims=True)
        rstd = lax.rsqrt(var + LN_EPS)
        xhat = xc * rstd
        gv = g_ref[...]
        err = (xhat * gv + b_ref[...]) - t_ref[...]
        acc_l[...] += jnp.sum((err * err).reshape(tm // 8, 8, D_MODEL), axis=0)
        dyv = err * (1.0 / D_MODEL)
        dxh = dyv * gv
        du = rstd * (dxh - jnp.mean(dxh, axis=-1, keepdims=True) - xhat * jnp.mean(dxh * xhat, axis=-1, keepdims=True))
        du_ref[...] = du
        dub_ref[...] = du.astype(BF16)
        acc_g[...] += jnp.sum((dyv * xhat).reshape(tm // 8, 8, D_MODEL), axis=0)
        acc_b[...] += jnp.sum(dyv.reshape(tm // 8, 8, D_MODEL), axis=0)

        @pl.when(i == nt - 1)
        def _():
            dg_ref[...] = jnp.sum(acc_g[...], axis=0, keepdims=True)
            db_ref[...] = jnp.sum(acc_b[...], axis=0, keepdims=True)
            tot = jnp.sum(jnp.sum(acc_l[...], axis=0, keepdims=True), axis=1, keepdims=True)
            loss_ref[...] = tot * (0.5 / D_MODEL)

    row = pl.BlockSpec((tm, D_MODEL), lambda i: (i, 0))
    vec = pl.BlockSpec((1, D_MODEL), lambda i: (0, 0))
    return _pcall(body, name=name,
                  out_shape=(SDS((s, D_MODEL), F32), SDS((s, D_MODEL), BF16), SDS((1, D_MODEL), F32), SDS((1, D_MODEL), F32),
                             SDS((1, 1), F32)),
                  grid=(nt,),
                  in_specs=[pl.BlockSpec((tm, k), lambda i: (i, 0)), pl.BlockSpec((k, D_MODEL), lambda i: (0, 0)), row, row,
                            vec, vec],
                  out_specs=(row, row, vec, vec, pl.BlockSpec((1, 1), lambda i: (0, 0))),
                  scratch_shapes=[pltpu.VMEM((8, D_MODEL), F32)] * 3, dims=("arbitrary",))(x, w, a, target, g, b)


def _mm_ln_bwd(pairs, addend, coef, a, r, g, ca, name, after=None):
    s = a.shape[0]
    has_r = r is not None
    has_add = addend is not None
    extra = [] if after is None else [after]
    n_pairs = len(pairs)

    def vmem(tm):
        tot = tm * D_MODEL * (4 * (2 + has_r) + 6)
        for pa, pb, _ in pairs:
            tot += tm * pa.shape[1] * pa.dtype.itemsize + pb.size * pb.dtype.itemsize
        return 2 * tot

    tm = next(c for c in (512, 256, 128) if s % c == 0 and vmem(c) <= MM_VMEM_BUDGET)
    nt = s // tm

    def body(*refs):
        ins = refs[2 * n_pairs:]
        add_ref = ins[0] if has_add else None
        ins = ins[has_add:]
        a_ref = ins[0]
        r_ref = ins[1] if has_r else None
        g_ref = ins[1 + has_r]
        du_ref, dub_ref, dg_ref, db_ref, acc_g, acc_b = ins[2 + has_r + len(extra):]
        i = pl.program_id(0)

        @pl.when(i == 0)
        def _():
            acc_g[...] = jnp.zeros_like(acc_g)
            acc_b[...] = jnp.zeros_like(acc_b)

        dyv = coef * add_ref[...] if has_add else None
        for p, (_, _, tb) in enumerate(pairs):
            part = lax.dot_general(refs[2 * p][...].astype(BF16), refs[2 * p + 1][...], _NT if tb else _NN,
                                   preferred_element_type=F32)
            dyv = part if dyv is None else dyv + part
        u = a_ref[...] if ca == 1.0 else ca * a_ref[...]
        if has_r:
            u = u + r_ref[...]
        mu = jnp.mean(u, axis=-1, keepdims=True)
        xc = u - mu
        var = jnp.mean(xc * xc, axis=-1, keepdims=True)
        rstd = lax.rsqrt(var + LN_EPS)
        xhat = xc * rstd
        dxh = dyv * g_ref[...]
        du = rstd * (dxh - jnp.mean(dxh, axis=-1, keepdims=True) - xhat * jnp.mean(dxh * xhat, axis=-1, keepdims=True))
        du_ref[...] = du
        dub_ref[...] = du.astype(BF16)
        acc_g[...] += jnp.sum((dyv * xhat).reshape(tm // 8, 8, D_MODEL), axis=0)
        acc_b[...] += jnp.sum(dyv.reshape(tm // 8, 8, D_MODEL), axis=0)

        @pl.when(i == nt - 1)
        def _():
            dg_ref[...] = jnp.sum(acc_g[...], axis=0, keepdims=True)
            db_ref[...] = jnp.sum(acc_b[...], axis=0, keepdims=True)

    row = pl.BlockSpec((tm, D_MODEL), lambda i: (i, 0))
    vec = pl.BlockSpec((1, D_MODEL), lambda i: (0, 0))
    in_specs, args = [], []
    for pa, pb, _ in pairs:
        in_specs += [pl.BlockSpec((tm, pa.shape[1]), lambda i: (i, 0)), pl.BlockSpec(pb.shape, lambda i: (0, 0))]
        args += [pa, pb]
    in_specs += [row] * (has_add + 1 + has_r) + [vec] + [pl.BlockSpec((8, 128), lambda i: (0, 0))] * len(extra)
    args += ([addend] if has_add else []) + [a] + ([r] if has_r else []) + [g] + extra
    return _pcall(body, name=name,
                  out_shape=(SDS((s, D_MODEL), F32), SDS((s, D_MODEL), BF16), SDS((1, D_MODEL), F32), SDS((1, D_MODEL), F32)),
                  grid=(nt,), in_specs=in_specs, out_specs=(row, row, vec, vec),
                  scratch_shapes=[pltpu.VMEM((8, D_MODEL), F32), pltpu.VMEM((8, D_MODEL), F32)],
                  dims=("arbitrary",))(*args)


def _xattn_fwd(q, k, v):
    s = q.shape[0]
    tq = _pick(s, (512,))
    scale = X_HEAD_DIM ** -0.5

    def body(q_ref, k_ref, v_ref, o_ref, ob_ref):
        qv, kv, vv = q_ref[...], k_ref[...], v_ref[...]
        outs = []
        for h in range(X_HEADS):
            sl = slice(h * X_HEAD_DIM, (h + 1) * X_HEAD_DIM)
            sc = lax.dot_general(qv[:, sl], kv[:, sl], _NT, preferred_element_type=F32) * scale
            e = jnp.exp(sc - jnp.max(sc, axis=-1, keepdims=True))
            p = e / jnp.sum(e, axis=-1, keepdims=True)
            outs.append(lax.dot_general(p.astype(BF16), vv[:, sl], _NN, preferred_element_type=F32))
        o = jnp.concatenate(outs, axis=1)
        o_ref[...] = o
        ob_ref[...] = o.astype(BF16)

    row = pl.BlockSpec((tq, D_MODEL), lambda i: (i, 0))
    full = pl.BlockSpec((MEM_LEN, D_MODEL), lambda i: (0, 0))
    return _pcall(body, name="xattn_fwd", out_shape=(SDS((s, D_MODEL), F32), SDS((s, D_MODEL), BF16)), grid=(s // tq,),
                  in_specs=[row, full, full], out_specs=(row, row), dims=("parallel",))(q, k, v)


def _xattn_bwd(q, k, v, o, do):
    s = q.shape[0]
    tq = _pick(s, (512,))
    scale = X_HEAD_DIM ** -0.5

    def body(q_ref, k_ref, v_ref, o_ref, do_ref, dq_ref, dk_ref, dv_ref):
        i = pl.program_id(0)

        @pl.when(i == 0)
        def _():
            dk_ref[...] = jnp.zeros_like(dk_ref)
            dv_ref[...] = jnp.zeros_like(dv_ref)

        qv, kv, vv, ov, dov = q_ref[...], k_ref[...], v_ref[...], o_ref[...], do_ref[...]
        dqs, dks, dvs = [], [], []
        for h in range(X_HEADS):
            sl = slice(h * X_HEAD_DIM, (h + 1) * X_HEAD_DIM)
            sc = lax.dot_general(qv[:, sl], kv[:, sl], _NT, preferred_element_type=F32) * scale
            e = jnp.exp(sc - jnp.max(sc, axis=-1, keepdims=True))
            p = e / jnp.sum(e, axis=-1, keepdims=True)
            doh = dov[:, sl]
            dob = doh.astype(BF16)
            delta = jnp.sum(doh * ov[:, sl], axis=-1, keepdims=True)
            dvs.append(lax.dot_general(p.astype(BF16), dob, _TN, preferred_element_type=F32))
            dp = lax.dot_general(dob, vv[:, sl], _NT, preferred_element_type=F32)
            ds = (p * (dp - delta)).astype(BF16)
            dqs.append(lax.dot_general(ds, kv[:, sl], _NN, preferred_element_type=F32) * scale)
            dks.append(lax.dot_general(ds, qv[:, sl], _TN, preferred_element_type=F32) * scale)
        dq_ref[...] = jnp.concatenate(dqs, axis=1).astype(BF16)
        dk_ref[...] += jnp.concatenate(dks, axis=1)
        dv_ref[...] += jnp.concatenate(dvs, axis=1)

    row = pl.BlockSpec((tq, D_MODEL), lambda i: (i, 0))
    full = pl.BlockSpec((MEM_LEN, D_MODEL), lambda i: (0, 0))
    return _pcall(body, name="xattn_bwd",
                  out_shape=(SDS((s, D_MODEL), BF16), SDS((MEM_LEN, D_MODEL), F32), SDS((MEM_LEN, D_MODEL), F32)),
                  grid=(s // tq,), in_specs=[row, full, full, row, row], out_specs=(row, full, full),
                  dims=("arbitrary",))(q, k, v, o, do)


_SQRT_HALF = 0.7071067811865476
_INV_SQRT_2PI = 0.3989422804014327


def _halo_specs(s, tm, width, rows=8):
    nb = s // rows
    r = tm // rows
    prev = pl.BlockSpec((rows, width), lambda i: (jnp.maximum(i * r - 1, 0), 0))
    nxt = pl.BlockSpec((rows, width), lambda i: (jnp.minimum((i + 1) * r, nb - 1), 0))
    return prev, nxt


def _shifted(x, before_row, after_row, i, nt):
    tm = x.shape[0]
    row = lax.broadcasted_iota(jnp.int32, x.shape, 0)
    first = jnp.where(i == 0, 0.0, 1.0) * before_row
    last = jnp.where(i == nt - 1, 0.0, 1.0) * after_row
    xm1 = jnp.where(row == 0, first, pltpu.roll(x, 1, 0))
    xp1 = jnp.where(row == tm - 1, last, pltpu.roll(x, tm - 1, 0))
    return xm1, xp1


BF16_ROWS = 16


def _ffn_fwd(hb, wg_t, wu_t, cw, cb):
    s = hb.shape[0]
    tm = _pick(s, (256,))
    nt = s // tm
    hr = BF16_ROWS

    def body(h_ref, hp_ref, hn_ref, wg_ref, wu_ref, cw_ref, cb_ref, g_ref, up_ref, gc_ref, act_ref):
        i = pl.program_id(0)
        hv = h_ref[...]
        g_ext = lax.dot_general(jnp.concatenate([hp_ref[...], hv, hn_ref[...]], axis=0), wg_ref[...], _NT,
                                preferred_element_type=F32)
        gv = g_ext[hr:hr + tm]
        upv = lax.dot_general(hv, wu_ref[...], _NT, preferred_element_type=F32)
        gm1, gp1 = _shifted(gv, g_ext[hr - 1:hr], g_ext[hr + tm:hr + tm + 1], i, nt)
        gc = gm1 * cw_ref[0:1, :] + gv * cw_ref[1:2, :] + gp1 * cw_ref[2:3, :] + cb_ref[...]
        cdf = 0.5 * (1.0 + lax.erf(gc * _SQRT_HALF))
        g_ref[...] = gv
        up_ref[...] = upv
        gc_ref[...] = gc
        act_ref[...] = (gc * cdf * upv).astype(BF16)

    hrow = pl.BlockSpec((tm, D_MODEL), lambda i: (i, 0))
    prev, nxt = _halo_specs(s, tm, D_MODEL, hr)
    wfull = pl.BlockSpec((D_FF, D_MODEL), lambda i: (0, 0), pipeline_mode=pl.Buffered(1))
    row = pl.BlockSpec((tm, D_FF), lambda i: (i, 0))
    return _pcall(body, name="ffn_fwd", out_shape=(SDS((s, D_FF), F32),) * 3 + (SDS((s, D_FF), BF16),),
                  grid=(nt,), in_specs=[hrow, prev, nxt, wfull, wfull, pl.BlockSpec((8, D_FF), lambda i: (0, 0)),
                                        pl.BlockSpec((1, D_FF), lambda i: (0, 0))],
                  out_specs=(row, row, row, row), dims=("parallel",))(hb, hb, hb, wg_t, wu_t, cw, cb)


def _ffn_bwd(dffb, w_down, g, gc, up, cw):
    s = g.shape[0]
    tm = _pick(s, (256,))
    nt = s // tm
    hr = BF16_ROWS

    def body(df_ref, dfp_ref, dfn_ref, wd_ref, g_ref, gc_ref, gcp_ref, gcn_ref, up_ref, upp_ref, upn_ref, cw_ref,
             dg_ref, dup_ref, dcw_ref, dcb_ref, a0, a1, a2, a3):
        i = pl.program_id(0)

        @pl.when(i == 0)
        def _():
            for a in (a0, a1, a2, a3):
                a[...] = jnp.zeros_like(a)

        def d_conv_out(gc_, up_, da_):
            cdf_ = 0.5 * (1.0 + lax.erf(gc_ * _SQRT_HALF))
            pdf_ = jnp.exp(-0.5 * gc_ * gc_) * _INV_SQRT_2PI
            return da_ * up_ * (cdf_ + gc_ * pdf_), cdf_

        df_ext = jnp.concatenate([dfp_ref[...], df_ref[...], dfn_ref[...]], axis=0)
        tn = 256
        for c0 in range(0, D_FF, tn):
            cs = slice(c0, c0 + tn)
            da_ext = lax.dot_general(df_ext, wd_ref[cs, :], _NT, preferred_element_type=F32)
            cw0, cw1, cw2 = cw_ref[0:1, cs], cw_ref[1:2, cs], cw_ref[2:3, cs]
            gc = gc_ref[:, cs]
            da = da_ext[hr:hr + tm]
            dgc, cdf = d_conv_out(gc, up_ref[:, cs], da)
            dup_ref[:, cs] = (da * (gc * cdf)).astype(BF16)
            dgc_b = jnp.where(i == 0, 0.0, 1.0) * d_conv_out(gcp_ref[7:8, cs], upp_ref[7:8, cs], da_ext[hr - 1:hr])[0]
            dgc_a = jnp.where(i == nt - 1, 0.0, 1.0) * d_conv_out(gcn_ref[0:1, cs], upn_ref[0:1, cs],
                                                                  da_ext[hr + tm:hr + tm + 1])[0]
            row = lax.broadcasted_iota(jnp.int32, dgc.shape, 0)
            dgc_m1 = jnp.where(row == 0, dgc_b, pltpu.roll(dgc, 1, 0))
            dgc_p1 = jnp.where(row == tm - 1, dgc_a, pltpu.roll(dgc, tm - 1, 0))
            dg_ref[:, cs] = (dgc_p1 * cw0 + dgc * cw1 + dgc_m1 * cw2).astype(BF16)

            def fold(t):
                return jnp.sum(t.reshape(tm // 8, 8, tn), axis=0)

            gv = g_ref[:, cs]
            a0[:, cs] += fold(dgc_p1 * gv)
            a1[:, cs] += fold(dgc * gv)
            a2[:, cs] += fold(dgc_m1 * gv)
            a3[:, cs] += fold(dgc)

        @pl.when(i == nt - 1)
        def _():
            dcw_ref[...] = jnp.concatenate(
                [jnp.sum(a[...], axis=0, keepdims=True) for a in (a0, a1, a2)] + [jnp.zeros((5, D_FF), F32)], axis=0)
            dcb_ref[...] = jnp.sum(a3[...], axis=0, keepdims=True)

    row = pl.BlockSpec((tm, D_FF), lambda i: (i, 0))
    prev, nxt = _halo_specs(s, tm, D_FF)
    cw_spec = pl.BlockSpec((8, D_FF), lambda i: (0, 0))
    cb_spec = pl.BlockSpec((1, D_FF), lambda i: (0, 0))
    dprev, dnxt = _halo_specs(s, tm, D_MODEL, hr)
    return _pcall(body, name="ffn_bwd",
                  out_shape=(SDS((s, D_FF), BF16), SDS((s, D_FF), BF16), SDS((8, D_FF), F32), SDS((1, D_FF), F32)),
                  grid=(nt,),
                  in_specs=[pl.BlockSpec((tm, D_MODEL), lambda i: (i, 0)), dprev, dnxt,
                            pl.BlockSpec((D_FF, D_MODEL), lambda i: (0, 0), pipeline_mode=pl.Buffered(1)), row]
                  + [row, prev, nxt] * 2 + [cw_spec],
                  out_specs=(row, row, cw_spec, cb_spec), scratch_shapes=[pltpu.VMEM((8, D_FF), F32)] * 4,
                  dims=("arbitrary",))(dffb, dffb, dffb, w_down, g, gc, gc, gc, up, up, up, cw)


def _adamw(w, g, m, v, name):
    rows, cols = w.shape
    tr = _pick(rows, (256, 128, 64, 32, 16, 8))
    c1 = 1.0 - ADAM_B1 ** ADAM_STEP
    c2 = 1.0 - ADAM_B2 ** ADAM_STEP

    def body(w_ref, g_ref, m_ref, v_ref, d_ref, nm_ref, nv_ref):
        gv = g_ref[...]
        nm = ADAM_B1 * m_ref[...] + (1.0 - ADAM_B1) * gv
        nv = ADAM_B2 * v_ref[...] + (1.0 - ADAM_B2) * (gv * gv)
        d_ref[...] = -ADAM_LR * ((nm / c1) / (jnp.sqrt(nv / c2) + ADAM_EPS) + ADAM_WD * w_ref[...])
        nm_ref[...] = nm
        nv_ref[...] = nv

    blk = pl.BlockSpec((tr, cols), lambda i: (i, 0))
    return _pcall(body, name=name, out_shape=(SDS(w.shape, F32),) * 3, grid=(rows // tr,), in_specs=[blk] * 4,
                  out_specs=(blk,) * 3, dims=("parallel",))(w, g, m, v)


def _adamw_many(ws, gs, ms, vs, name):
    n = len(ws)
    c1 = 1.0 - ADAM_B1 ** ADAM_STEP
    c2 = 1.0 - ADAM_B2 ** ADAM_STEP

    def body(*refs):
        outs = refs[4 * n:]
        for k in range(n):
            gv = refs[n + k][...]
            nm = ADAM_B1 * refs[2 * n + k][...] + (1.0 - ADAM_B1) * gv
            nv = ADAM_B2 * refs[3 * n + k][...] + (1.0 - ADAM_B2) * (gv * gv)
            outs[k][...] = -ADAM_LR * ((nm / c1) / (jnp.sqrt(nv / c2) + ADAM_EPS) + ADAM_WD * refs[k][...])
            outs[n + k][...] = nm
            outs[2 * n + k][...] = nv

    shapes = tuple(SDS(w.shape, F32) for w in ws)
    res = _pcall(body, name=name, out_shape=shapes * 3)(*ws, *gs, *ms, *vs)
    return res[:n], res[n:2 * n], res[2 * n:]


def _all_gather_rows(x_shard, *, name, in_vmem, sum_rows=False, after=None):
    m_per, n = x_shard.shape
    extra = [] if after is None else [after]

    def body(x_ref, *rest):
        out_ref, rest = rest[len(extra)], rest[len(extra) + 1:]
        if sum_rows:
            sum_ref, send_sems, recv_sems, local_sem = rest
        else:
            send_sems, recv_sems, local_sem = rest
        x, y, c = lax.axis_index("x"), lax.axis_index("y"), lax.axis_index("c")
        me, sibling = (x, y, c), (x, y, 1 - c)
        chips = [(1 - x, y), (x, 1 - y), (1 - x, 1 - y)]

        def rows(px, py, pc):
            return out_ref.at[pl.ds((4 * px + 2 * py + pc) * m_per, m_per), :]

        def copy(k, block, to, src=None):
            return pltpu.make_async_remote_copy(
                src_ref=rows(*block) if src is None else src, dst_ref=rows(*block), send_sem=send_sems.at[k],
                recv_sem=recv_sems.at[k], device_id=to, device_id_type=pl.DeviceIdType.MESH)

        mine = pltpu.make_async_copy(x_ref, rows(*me), local_sem)
        mine.start()
        first = [copy(0, me, sibling, src=x_ref)]
        first += [copy(1 + j, me, (*chip, c), src=x_ref) for j, chip in enumerate(chips)]
        for cp in first:
            cp.start()
        passed = [copy(4 + j, (*chip, c), sibling) for j, chip in enumerate(chips)]
        for j, chip in enumerate(chips):
            copy(1 + j, (*chip, c), me).wait_recv()
            passed[j].start()
        copy(0, sibling, me).wait_recv()
        for j, chip in enumerate(chips):
            copy(4 + j, (*chip, 1 - c), me).wait_recv()
        for cp in first + passed:
            cp.wait_send()
        mine.wait()
        if sum_rows:
            acc = out_ref[0:m_per, :]
            for dev in range(1, N_DEV):
                acc = acc + out_ref[dev * m_per:(dev + 1) * m_per, :]
            sum_ref[...] = acc

    space = pltpu.VMEM if in_vmem else pl.ANY
    out_shape = [SDS((N_DEV * m_per, n), x_shard.dtype)]
    out_specs = [pl.BlockSpec(memory_space=space)]
    if sum_rows:
        out_shape.append(SDS((m_per, n), x_shard.dtype))
        out_specs.append(pl.BlockSpec(memory_space=pltpu.VMEM))
    res = _PALLAS_CALL(
        body, name=name, out_shape=tuple(out_shape),
        in_specs=[pl.BlockSpec(memory_space=space)] + [pl.BlockSpec(memory_space=pl.ANY)] * len(extra),
        out_specs=tuple(out_specs),
        scratch_shapes=[pltpu.SemaphoreType.DMA((7,)), pltpu.SemaphoreType.DMA((7,)), pltpu.SemaphoreType.DMA],
        compiler_params=pltpu.CompilerParams(vmem_limit_bytes=VMEM_LIMIT_BYTES),
    )(x_shard, *extra)
    return res if sum_rows else res[0]


_HBM = pl.BlockSpec(memory_space=pltpu.HBM)
_SEM = pl.BlockSpec(memory_space=pltpu.SEMAPHORE)
_SPLIT_PARAMS = dict(has_side_effects=pltpu.SideEffectType.DATAFLOW_SIDE_EFFECTING)


def _split_copies(src_ref, land_ref, send_sems, recv_sems, gather):
    x, y, c = lax.axis_index("x"), lax.axis_index("y"), lax.axis_index("c")
    first = 0 if gather else 1
    copies = []
    for k in range(first, N_DEV):
        px = 1 - x if k & 4 else x
        py = 1 - y if k & 2 else y
        pc = 1 - c if k & 1 else c
        if gather:
            rows = src_ref.shape[0]
            src, dst = src_ref, land_ref.at[pl.ds((4 * x + 2 * y + c) * rows, rows), :]
        else:
            src, dst = src_ref.at[4 * px + 2 * py + pc], land_ref.at[k - 1]
        copies.append(pltpu.make_async_remote_copy(
            src_ref=src, dst_ref=dst, send_sem=send_sems.at[k - first], recv_sem=recv_sems.at[k - first],
            device_id=(px, py, pc), device_id_type=pl.DeviceIdType.MESH))
    return copies


def _exchange_start(src, land_shape, *, gather, name):
    def body(src_ref, land_ref, send_sems, recv_sems, src_thru, land_thru, token):
        for cp in _split_copies(src_ref, land_ref, send_sems, recv_sems, gather):
            cp.start()
        token[...] = jnp.zeros_like(token)

    land = pltpu.with_memory_space_constraint(lax.empty(land_shape, src.dtype), pltpu.HBM)
    n_copies = N_DEV if gather else N_DEV - 1
    return _PALLAS_CALL(
        body, name=name,
        out_shape=(pltpu.SemaphoreType.DMA((n_copies,)), pltpu.SemaphoreType.DMA((n_copies,)),
                   pltpu.HBM(src.shape, src.dtype), pltpu.HBM(land_shape, src.dtype), SDS((8, 128), F32)),
        in_specs=(_HBM, _HBM), out_specs=(_SEM, _SEM, _HBM, _HBM, pl.BlockSpec(memory_space=pltpu.VMEM)),
        input_output_aliases={0: 2, 1: 3}, compiler_params=pltpu.CompilerParams(**_SPLIT_PARAMS),
    )(pltpu.with_memory_space_constraint(src, pltpu.HBM), land)


def _exchange_wait(started, after, *, gather, name):
    send_sems, recv_sems, src_thru, land_thru, _ = started

    def body(src_ref, land_ref, send_sems, recv_sems, after_ref, src_out, land_out):
        copies = _split_copies(src_ref, land_ref, send_sems, recv_sems, gather)
        for cp in copies:
            cp.wait_send()
        for cp in copies:
            cp.wait_recv()

    return _PALLAS_CALL(
        body, name=name,
        out_shape=(pltpu.HBM(src_thru.shape, src_thru.dtype), pltpu.HBM(land_thru.shape, land_thru.dtype)),
        in_specs=(_HBM, _HBM, _SEM, _SEM, pl.BlockSpec(memory_space=pl.ANY)), out_specs=(_HBM, _HBM),
        input_output_aliases={0: 0, 1: 1}, compiler_params=pltpu.CompilerParams(**_SPLIT_PARAMS),
    )(src_thru, land_thru, send_sems, recv_sems, after)


def _sum_parts(own, land, name):
    r, n = own.shape
    tr = _pick(r, (264, 320, 336, 128, 64, 32, 16, 8))

    def body(own_ref, x_ref, o_ref):
        acc = own_ref[...]
        for k in range(N_DEV - 1):
            acc = acc + x_ref[k].astype(F32)
        o_ref[...] = acc

    return _pcall(body, name=name, out_shape=SDS((r, n), F32), grid=(r // tr,),
                  in_specs=[pl.BlockSpec((tr, n), lambda i: (i, 0)), pl.BlockSpec((N_DEV - 1, tr, n), lambda i: (0, i, 0))],
                  out_specs=pl.BlockSpec((tr, n), lambda i: (i, 0)), dims=("parallel",))(own, land)


def _pad_rows(a, rows):
    return jnp.pad(a, ((0, rows - a.shape[0]), (0, 0)))


def kernel(x, mem, positions, ln_in_g, ln_in_b, w_in, attn_sink, g_win, g_dil, w_mix_out, ln1_g, ln1_b, mem_ln_g, mem_ln_b, w_xq, w_xk, w_xv, w_xo, ln2_g, ln2_b, w_gate, w_up, conv_w, conv_b, w_down, ln3_g, ln3_b, loss_target, m_ln_in_g, m_ln_in_b, m_w_in, m_attn_sink, m_g_win, m_g_dil, m_w_mix_out, m_ln1_g, m_ln1_b, m_mem_ln_g, m_mem_ln_b, m_w_xq, m_w_xk, m_w_xv, m_w_xo, m_ln2_g, m_ln2_b, m_w_gate, m_w_up, m_conv_w, m_conv_b, m_w_down, m_ln3_g, m_ln3_b, v_ln_in_g, v_ln_in_b, v_w_in, v_attn_sink, v_g_win, v_g_dil, v_w_mix_out, v_ln1_g, v_ln1_b, v_mem_ln_g, v_mem_ln_b, v_w_xq, v_w_xk, v_w_xv, v_w_xo, v_ln2_g, v_ln2_b, v_w_gate, v_w_up, v_conv_w, v_conv_b, v_w_down, v_ln3_g, v_ln3_b):
    weights = dict(ln_in_g=ln_in_g, ln_in_b=ln_in_b, w_in=w_in, attn_sink=attn_sink, g_win=g_win, g_dil=g_dil, w_mix_out=w_mix_out, ln1_g=ln1_g, ln1_b=ln1_b, mem_ln_g=mem_ln_g, mem_ln_b=mem_ln_b, w_xq=w_xq, w_xk=w_xk, w_xv=w_xv, w_xo=w_xo, ln2_g=ln2_g, ln2_b=ln2_b, w_gate=w_gate, w_up=w_up, conv_w=conv_w, conv_b=conv_b, w_down=w_down, ln3_g=ln3_g, ln3_b=ln3_b)
    mom_m = dict(ln_in_g=m_ln_in_g, ln_in_b=m_ln_in_b, w_in=m_w_in, attn_sink=m_attn_sink, g_win=m_g_win, g_dil=m_g_dil, w_mix_out=m_w_mix_out, ln1_g=m_ln1_g, ln1_b=m_ln1_b, mem_ln_g=m_mem_ln_g, mem_ln_b=m_mem_ln_b, w_xq=m_w_xq, w_xk=m_w_xk, w_xv=m_w_xv, w_xo=m_w_xo, ln2_g=m_ln2_g, ln2_b=m_ln2_b, w_gate=m_w_gate, w_up=m_w_up, conv_w=m_conv_w, conv_b=m_conv_b, w_down=m_w_down, ln3_g=m_ln3_g, ln3_b=m_ln3_b)
    mom_v = dict(ln_in_g=v_ln_in_g, ln_in_b=v_ln_in_b, w_in=v_w_in, attn_sink=v_attn_sink, g_win=v_g_win, g_dil=v_g_dil, w_mix_out=v_w_mix_out, ln1_g=v_ln1_g, ln1_b=v_ln1_b, mem_ln_g=v_mem_ln_g, mem_ln_b=v_mem_ln_b, w_xq=v_w_xq, w_xk=v_w_xk, w_xv=v_w_xv, w_xo=v_w_xo, ln2_g=v_ln2_g, ln2_b=v_ln2_b, w_gate=v_w_gate, w_up=v_w_up, conv_w=v_conv_w, conv_b=v_conv_b, w_down=v_w_down, ln3_g=v_ln3_g, ln3_b=v_ln3_b)
    order = list(weights)
    s = x.shape[1]
    xs = x[0]
    mems = mem[0]
    target = loss_target[0]
    row = lambda a: a.reshape(1, -1)

    shard_rows = dict(w_in=w_in[0].T, w_gate=w_gate[0].T, w_up=w_up[0].T, w_mix_out=w_mix_out[0], w_xq=w_xq[0],
                      w_xk=w_xk[0], w_xv=w_xv[0], w_xo=w_xo[0], w_down=w_down[0])
    me_lin = 4 * lax.axis_index("x") + 2 * lax.axis_index("y") + lax.axis_index("c")
    w_in_full = _all_gather_rows(shard_rows["w_in"].astype(BF16), name="w_in_all_gather", in_vmem=False)
    late_rows = PACK_ROWS[1:]
    late_r = sum(r for _, r in late_rows)
    packed = jnp.concatenate([shard_rows[n].astype(BF16) for n, _ in late_rows], axis=0)
    w_started = _exchange_start(packed, (N_DEV * late_r, D_MODEL), gather=True, name="weight_gather_start")

    tabs = _rope_tables(positions.astype(F32).reshape(s, 1) + w_started[4][0, 0])
    h0, h0b = _ln_fwd(xs, None, row(ln_in_g), row(ln_in_b), 1.0, "ln_in_fwd")
    zw, *zg = _proj_rope(h0b, w_in_full, tabs[0])
    cw_pad = jnp.pad(conv_w[0], ((0, 5), (0, 32)))
    cw_all = _all_gather_rows(cw_pad, name="conv_w_all_gather", in_vmem=True, after=zw).reshape(N_DEV, 8, 384)
    cw_full = jnp.transpose(cw_all[:, :3, :352], (1, 0, 2)).reshape(3, D_FF)
    cw8 = _pad_rows(cw_full, 8)
    oa, lse_a = _banded_fwd(zw, attn_sink, name="win_attn_fwd", **{**_WIN_CFG, "tq": 2 * ATTN_TQ})
    og_views, lg_views = [], []
    for gi in range(3):
        o_g, l_g = _banded_fwd(zg[gi], None, name=f"dil_attn_fwd{gi}", **{**_dil_cfg(gi), "tq": 2 * ATTN_TQ})
        og_views.append(o_g)
        lg_views.append(l_g)
    mixed, ob_views, lb_views = _mix_norm_fwd(oa, og_views, lg_views, g_win, g_dil)
    _, land = _exchange_wait(w_started, mixed, gather=True, name="weight_gather_wait")
    gathered = land.reshape(N_DEV, late_r, D_MODEL)
    full = {}
    off = 0
    for n, r in late_rows:
        full[n] = gathered[:, off:off + r, :].reshape(N_DEV * r, D_MODEL)
        off += r
    mix, h1, h1b = _mm_ln_fwd(mixed, full["w_mix_out"], h0, ln1_g, ln1_b, ALPHA, "mm_mix_out_ln1")
    _, mem_nb = _ln_fwd(mems, None, mem_ln_g, mem_ln_b, 1.0, "mem_ln_fwd")
    kx = _mm(mem_nb, full["w_xk"], trans_b=False, out_dtype=BF16, name="mm_xk")
    vx = _mm(mem_nb, full["w_xv"], trans_b=False, out_dtype=BF16, name="mm_xv")
    qx = _mm(h1b, full["w_xq"], trans_b=False, out_dtype=BF16, name="mm_xq")
    ox, oxb = _xattn_fwd(qx, kx, vx)
    xa, h2, h2b = _mm_ln_fwd(oxb, full["w_xo"], h1, ln2_g, ln2_b, ALPHA, "mm_xo_ln2")
    gate, up, gc, act = _ffn_fwd(h2b, full["w_gate"], full["w_up"], cw8, conv_b)

    du3, du3b, d_ln3_g, d_ln3_b, loss_local = _mm_ln_loss(act, full["w_down"], h2, target, ln3_g, ln3_b, ALPHA,
                                                          "mm_down_ln3_loss")
    dw_down = _mm_tn(act, du3b, name="mm_dw_down")
    dgate, dup, dcw8, d_conv_b = _ffn_bwd(du3b, full["w_down"], gate, gc, up, cw8)
    dw_gate_t = _mm_tn(dgate, h2b, name="mm_dw_gate")
    dw_up_t = _mm_tn(dup, h2b, name="mm_dw_up")
    rows_of = dict(PACK_ROWS)

    own_f32 = {}

    def start_grad_exchange(parts, name, payload=F32):
        gp = jnp.concatenate([g.reshape(N_DEV, rows_of[n], D_MODEL) for n, g in parts], axis=1)
        if payload != F32:
            own_f32[name] = lax.dynamic_index_in_dim(gp, me_lin, axis=0, keepdims=False)
            gp = gp.astype(payload)
        return _exchange_start(gp, (N_DEV - 1,) + gp.shape[1:], gather=False, name=name)

    ffn_parts = (("w_gate", dw_gate_t), ("w_up", dw_up_t), ("w_down", dw_down))
    ffn_started = start_grad_exchange(ffn_parts, "grad_start_ffn")
    du2, du2b, d_ln2_g, d_ln2_b = _mm_ln_bwd(((dgate, full["w_gate"], False), (dup, full["w_up"], False)), du3, ALPHA,
                                             h1, xa, ln2_g + ffn_started[4][0, 0], ALPHA, "mm_dh2_ln2_bwd")
    dox = _mm(du2b, full["w_xo"], trans_b=True, out_dtype=F32, name="mm_d_ox")
    dw_xo = _mm_tn(oxb, du2b, name="mm_dw_xo")
    dqx, dkx, dvx = _xattn_bwd(qx, kx, vx, ox, dox)
    dw_xq = _mm_tn(h1b, dqx, name="mm_dw_xq")
    dw_xk = _mm_tn(mem_nb, dkx, name="mm_dw_xk")
    dw_xv = _mm_tn(mem_nb, dvx, name="mm_dw_xv")
    _, _, d_mem_ln_g, d_mem_ln_b = _mm_ln_bwd(((dkx, full["w_xk"], True), (dvx, full["w_xv"], True)), None, 1.0, mems,
                                              None, mem_ln_g, 1.0, "mm_dmem_ln_bwd")
    du1, du1b, d_ln1_g, d_ln1_b = _mm_ln_bwd(((dqx, full["w_xq"], True),), du2, ALPHA, h0, mix, ln1_g, ALPHA,
                                             "mm_dh1_ln1_bwd")
    dmixed = _mm(du1b, full["w_mix_out"], trans_b=True, out_dtype=F32, name="mm_d_mixed")
    dw_mix_out = _mm_tn(mixed, du1b, name="mm_dw_mix_out")
    attn_parts = (("w_mix_out", dw_mix_out), ("w_xq", dw_xq), ("w_xk", dw_xk), ("w_xv", dw_xv), ("w_xo", dw_xo))
    attn_started = start_grad_exchange(attn_parts, "grad_start_attn")
    doa, dob_views, d_g_win, d_g_dil = _mix_norm_bwd(oa, ob_views[0], dmixed, g_win + attn_started[4][0, 0], g_dil)
    dqa, dkva, dsink8 = _banded_bwd(zw, oa, lse_a, doa, tabs[0], attn_sink, name="win_attn_bwd", **_WIN_CFG)
    dq_views, dkv_views = [], []
    for gi in range(3):
        dq_g, dkv_g = _banded_bwd(zg[gi], ob_views[gi], lb_views[gi], dob_views[gi], tabs[gi], None,
                                  name=f"dil_attn_bwd{gi}", **{**_dil_cfg(gi), "rc": ATTN_TQ})
        dq_views.append(dq_g)
        dkv_views.append(dkv_g)
    dz = _dz_assemble(dq_views, dkv_views, dqa, dkva)
    dw_in_t = _mm_tn(dz, h0b, name="mm_dw_in")
    in_parts = (("w_in", dw_in_t),)
    in_started = start_grad_exchange(in_parts, "grad_start_in", payload=BF16)
    dx, _, d_ln_in_g, d_ln_in_b = _mm_ln_bwd(((dz, w_in_full, False),), du1, ALPHA, xs, None, row(ln_in_g), 1.0,
                                             "mm_dh0_ln_in_bwd", after=in_started[4])

    grads, delta, new_m, new_v = {}, {}, {}, {}
    after = dx
    for parts, started, tag in ((ffn_parts, ffn_started, "ffn"), (attn_parts, attn_started, "attn"),
                                (in_parts, in_started, "in")):
        gp_thru, land = _exchange_wait(started, after, gather=False, name=f"grad_wait_{tag}")
        own = own_f32.get(f"grad_start_{tag}")
        if own is None:
            own = lax.dynamic_index_in_dim(gp_thru, me_lin, axis=0, keepdims=False)
        gsum = _sum_parts(own, land, f"grad_sum_{tag}")
        off = 0
        for n, _ in parts:
            blk = gsum[off:off + rows_of[n]]
            off += rows_of[n]
            grads[n] = (blk.T if n in ("w_in", "w_gate", "w_up") else blk)[None]
            shp = weights[n].shape
            d_, m_, v_ = _adamw(weights[n].reshape(shp[1:]), grads[n].reshape(shp[1:]), mom_m[n].reshape(shp[1:]),
                                mom_v[n].reshape(shp[1:]), f"adamw_{n}")
            delta[n], new_m[n], new_v[n] = d_.reshape(shp), m_.reshape(shp), v_.reshape(shp)
            after = d_

    small = jnp.concatenate([
        d_ln_in_g, d_ln_in_b, d_ln1_g, d_ln1_b, d_mem_ln_g, d_mem_ln_b, d_ln2_g, d_ln2_b, d_ln3_g, d_ln3_b,
        jnp.concatenate([d_g_win, d_g_dil], axis=1),
        jnp.pad(d_conv_b, ((0, 0), (0, 3072 - D_FF))).reshape(3, 1024),
        jnp.pad(dsink8[0:1, :], ((0, 0), (0, 1024 - 128))),
        jnp.pad(dcw8[0:3], ((0, 0), (0, 3072 - D_FF))).reshape(9, 1024),
    ], axis=0)
    _, ssum = _all_gather_rows(small, name="small_grad_all_reduce", in_vmem=True, sum_rows=True, after=after)
    names10 = ["ln_in_g", "ln_in_b", "ln1_g", "ln1_b", "mem_ln_g", "mem_ln_b", "ln2_g", "ln2_b", "ln3_g", "ln3_b"]
    small_g = {n: ssum[i:i + 1] for i, n in enumerate(names10)}
    small_g["g_win"] = ssum[10:11, :512]
    small_g["g_dil"] = ssum[10:11, 512:]
    small_g["conv_b"] = ssum[11:14].reshape(1, 3072)[:, :D_FF]
    small_g["attn_sink"] = ssum[14:15, :8]
    small_g["conv_w"] = lax.dynamic_slice_in_dim(ssum[15:24].reshape(3, 3072)[:, :D_FF], me_lin * 352, 352, axis=1)

    small_names = [n for n in order if n not in rows_of]
    two_d = lambda a: a.reshape(-1, a.shape[-1])
    d_s, m_s, v_s = _adamw_many([two_d(weights[n]) for n in small_names], [small_g[n] for n in small_names],
                                [two_d(mom_m[n]) for n in small_names], [two_d(mom_v[n]) for n in small_names],
                                "adamw_small")
    for k, n in enumerate(small_names):
        shp = weights[n].shape
        grads[n], delta[n], new_m[n], new_v[n] = (t.reshape(shp) for t in (small_g[n], d_s[k], m_s[k], v_s[k]))

    loss = lax.psum(loss_local[0, 0], MESH_AXES)
    return (loss, dx[None], *[grads[n] for n in order], *[delta[n] for n in order], *[new_m[n] for n in order],
            *[new_v[n] for n in order])
```

```python
import jax
import jax.numpy as jnp
from jax import lax
from jax.experimental import pallas as pl
from jax.experimental.pallas import tpu as pltpu

F32 = jnp.float32
BF16 = jnp.bfloat16
SDS = jax.ShapeDtypeStruct
_PALLAS_CALL = pl.pallas_call

D_MODEL = 1024
HEAD_DIM = 64
WIN_HALF = 128
DIL_PAIRS = ((128, 1), (512, 4), (2048, 16))
DIL_SIDE = 64
ROT_DIM = 16
ROPE_THETA = 500000.0
MEM_LEN = 256
X_HEADS = 4
X_HEAD_DIM = 256
D_FF = 2816
IN_WIDTH = 5376
Z_VB, Z_QA, Z_VA = 3072, 4608, 5248
W_IN_QA, W_IN_KA, W_IN_QB = 0, 512, 768
ALPHA = (2.0) ** 0.25
LN_EPS = 1e-5
NEG_INF = -1e30
ADAM_LR, ADAM_B1, ADAM_B2, ADAM_EPS, ADAM_WD, ADAM_STEP = 0.001, 0.9, 0.999, 1e-08, 0.01, 10
N_DEV = 8
MESH_AXES = ("x", "y", "c")
VMEM_LIMIT_BYTES = 52 * 1024 * 1024
ATTN_TQ = 256
WHOLE_CLASS_BYTES = 4 * 1024 * 1024
TABW = 384

PACK_ROWS = (("w_in", 672), ("w_gate", 352), ("w_up", 352), ("w_mix_out", 128), ("w_xq", 128), ("w_xk", 128),
             ("w_xv", 128), ("w_xo", 128), ("w_down", 352))
FF_SHARD = D_FF // N_DEV
FF_PAD = 3 * 1024


def _pick(n, cands):
    for c in cands:
        if n % c == 0:
            return c
    return n


def _pcall(body, *, name, out_shape, grid=None, in_specs=None, out_specs=None, scratch_shapes=(), dims=None,
           aliases=None):
    kw = {}
    if grid is not None:
        kw["grid"] = grid
    if in_specs is not None:
        kw["in_specs"] = in_specs
    if out_specs is not None:
        kw["out_specs"] = out_specs
    if aliases:
        kw["input_output_aliases"] = aliases
    return _PALLAS_CALL(
        body, name=name, out_shape=out_shape, scratch_shapes=list(scratch_shapes),
        compiler_params=pltpu.CompilerParams(dimension_semantics=dims, vmem_limit_bytes=VMEM_LIMIT_BYTES), **kw)


MM_VMEM_BUDGET = 40 * 1024 * 1024


def _mm(a, b, *, trans_b, out_dtype, name):
    m, k = a.shape
    n = b.shape[0] if trans_b else b.shape[1]
    out_bytes = jnp.dtype(out_dtype).itemsize

    def vmem(tm, tn):
        return 2 * (tm * tn * out_bytes + tm * k * a.dtype.itemsize + k * tn * b.dtype.itemsize)

    tm, tn = next(((cm, cn) for cn in (n, 1408, 1024, 512, 256, 128) if n % cn == 0
                   for cm in (1024, 512, 256, 128) if m % cm == 0 and vmem(cm, cn) <= MM_VMEM_BUDGET))

    def body(a_ref, b_ref, o_ref):
        o_ref[...] = lax.dot_general(a_ref[...], b_ref[...], _NT if trans_b else _NN,
                                     preferred_element_type=F32).astype(out_dtype)

    in_specs = [pl.BlockSpec((tm, k), lambda j, i: (i, 0)),
                pl.BlockSpec((tn, k), lambda j, i: (j, 0)) if trans_b else pl.BlockSpec((k, tn), lambda j, i: (0, j))]
    return _pcall(body, name=name, out_shape=SDS((m, n), out_dtype), grid=(n // tn, m // tm), in_specs=in_specs,
                  out_specs=pl.BlockSpec((tm, tn), lambda j, i: (i, j)), dims=("parallel", "parallel"))(a, b)


def _mm_tn(a, b, *, name):
    s, m = a.shape
    n = b.shape[1]
    tm = _pick(m, (768, 1408, 1024, 512, 256, 128))
    tk = _pick(s, (1024, 512, 256))
    nk = s // tk

    def body(a_ref, b_ref, o_ref, acc_ref):
        kk = pl.program_id(1)

        @pl.when(kk == 0)
        def _():
            acc_ref[...] = jnp.zeros_like(acc_ref)

        acc_ref[...] += lax.dot_general(a_ref[...].astype(BF16), b_ref[...].astype(BF16), (((0,), (0,)), ((), ())),
                                        preferred_element_type=F32)

        @pl.when(kk == nk - 1)
        def _():
            o_ref[...] = acc_ref[...]

    return _pcall(body, name=name, out_shape=SDS((m, n), F32), grid=(m // tm, nk),
                  in_specs=[pl.BlockSpec((tk, tm), lambda i, kk: (kk, i)), pl.BlockSpec((tk, n), lambda i, kk: (kk, 0))],
                  out_specs=pl.BlockSpec((tm, n), lambda i, kk: (i, 0)), scratch_shapes=[pltpu.VMEM((tm, n), F32)],
                  dims=("parallel", "arbitrary"))(a, b)


def _rope_lane_consts():
    lane = jnp.arange(128)
    j = lane % HEAD_DIM
    inv_freq = ROPE_THETA ** (-jnp.arange(0, ROT_DIM, 2, dtype=F32) / ROT_DIM)
    freq = jnp.where(j < ROT_DIM, inv_freq[j % (ROT_DIM // 2)], 0.0).astype(F32)
    lo = (j < ROT_DIM // 2).astype(F32)
    hi = ((j >= ROT_DIM // 2) & (j < ROT_DIM)).astype(F32)
    return jnp.stack([freq, lo, hi] + [jnp.zeros((128,), F32)] * 5)


def _to_classes(x, scr, d):
    if d == 1:
        return [x]
    scr[...] = x
    return [scr[pl.ds(c, x.shape[0] // d, stride=d), :] for c in range(d)]


def _from_classes(parts, scr):
    d = len(parts)
    if d == 1:
        return parts[0]
    for c, part in enumerate(parts):
        scr[pl.ds(c, part.shape[0], stride=d), :] = part
    return scr[...]


DILATIONS = tuple(d for _, d in DIL_PAIRS)


def _rope_tables(posf):
    s = posf.shape[0]
    tm = _pick(s, (1024, 512))

    def body(p_ref, c_ref, *rest):
        o_refs, scr = rest[:-1], rest[-1]
        ang = p_ref[...] * c_ref[0:1, :]
        lo = c_ref[1:2, :]
        hi = c_ref[2:3, :]
        cs = jnp.cos(ang)
        sn = jnp.sin(ang)
        for q, t in enumerate((jnp.where(lo + hi > 0.0, cs, 1.0), -sn * lo, sn * hi)):
            for o_ref, d in zip(o_refs, DILATIONS):
                for c, part in enumerate(_to_classes(t, scr, d)):
                    o_ref[:, c * TABW + q * 128:c * TABW + (q + 1) * 128] = part

    return _pcall(body, name="rope_tables", out_shape=tuple(SDS((s // d, d * TABW), F32) for d in DILATIONS),
                  grid=(s // tm,),
                  in_specs=[pl.BlockSpec((tm, 1), lambda i: (i, 0)), pl.BlockSpec((8, 128), lambda i: (0, 0))],
                  out_specs=tuple(pl.BlockSpec((tm // d, d * TABW), lambda i: (i, 0)) for d in DILATIONS),
                  scratch_shapes=[pltpu.VMEM((tm, 128), F32)], dims=("parallel",))(posf, _rope_lane_consts())


def _rope_apply(x, tab, sign):
    w = x.shape[1]
    rep = w // 128
    c = jnp.tile(tab[:, 0:128], (1, rep)) if rep > 1 else tab[:, 0:128]
    a = jnp.tile(tab[:, 128:256], (1, rep)) if rep > 1 else tab[:, 128:256]
    b = jnp.tile(tab[:, 256:384], (1, rep)) if rep > 1 else tab[:, 256:384]
    up = pltpu.roll(x, w - 8, 1)
    dn = pltpu.roll(x, 8, 1)
    if sign > 0:
        return x * c + up * a + dn * b
    return x * c - up * a - dn * b


def _proj_rope(h0b, w_t, tab):
    s = h0b.shape[0]
    tm = _pick(s, (512,))
    tn = 256

    def body(a_ref, w_ref, t_ref, zw_ref, z0_ref, z1_ref, z2_ref, scr):
        z_refs = (z0_ref, z1_ref, z2_ref)
        a = a_ref[...]
        tabv = t_ref[...]
        for c0 in range(0, IN_WIDTH, tn):
            w0 = (c0 + W_IN_QB) % IN_WIDTH
            z = lax.dot_general(a, w_ref[w0:w0 + tn, :], _NT, preferred_element_type=F32)
            for g0 in range(c0, c0 + tn, 128):
                zg = z[:, g0 - c0:g0 - c0 + 128]
                if g0 < Z_VB or Z_QA <= g0 < Z_VA:
                    zg = _rope_apply(zg, tabv, 1)
                if g0 >= Z_QA:
                    zw_ref[:, g0 - Z_QA:g0 - Z_QA + 128] = zg.astype(BF16)
                    continue
                kind, within = divmod(g0, 1536)
                grp, off = divmod(within, 512)
                col = kind * 512 + off
                for c, part in enumerate(_to_classes(zg, scr, DILATIONS[grp])):
                    z_refs[grp][:, c * 1536 + col:c * 1536 + col + 128] = part.astype(BF16)

    return _pcall(body, name="proj_rope",
                  out_shape=(SDS((s, 768), BF16),) + tuple(SDS((s // d, d * 1536), BF16) for d in DILATIONS),
                  grid=(s // tm,),
                  in_specs=[pl.BlockSpec((tm, D_MODEL), lambda i: (i, 0)), pl.BlockSpec((IN_WIDTH, D_MODEL), lambda i: (0, 0)),
                            pl.BlockSpec((tm, TABW), lambda i: (i, 0))],
                  out_specs=(pl.BlockSpec((tm, 768), lambda i: (i, 0)),)
                  + tuple(pl.BlockSpec((tm // d, d * 1536), lambda i: (i, 0)) for d in DILATIONS),
                  scratch_shapes=[pltpu.VMEM((tm, 128), F32)], dims=("parallel",))(h0b, w_t, tab)


def _band_specs(sd, blk, tq, width, per_tok, cb):
    r = tq // blk
    nbk = sd // blk
    prev = pl.BlockSpec((blk, width), lambda c, j: (jnp.maximum(j * r - 1, 0), c * per_tok + cb))
    cur = pl.BlockSpec((tq, width), lambda c, j: (j, c * per_tok + cb))
    nxt = pl.BlockSpec((blk, width), lambda c, j: (jnp.minimum((j + 1) * r, nbk - 1), c * per_tok + cb))
    return [prev, cur, nxt]


def _band_bias(q0, rows, blk, sd):
    shape = (rows, rows + 2 * blk)
    qpos = q0 + lax.broadcasted_iota(jnp.int32, shape, 0)
    kpos = q0 - blk + lax.broadcasted_iota(jnp.int32, shape, 1)
    ok = (jnp.abs(qpos - kpos) <= blk) & (kpos >= 0) & (kpos < sd)
    return jnp.where(ok, 0.0, NEG_INF)


_NT = (((1,), (1,)), ((), ()))
_NN = (((1,), (0,)), ((), ()))
_TN = (((0,), (0,)), ((), ()))


def _banded_fwd(zv, sink, *, d, blk, tq, rc, ptw, qw, kw, qcb, kcb, vcb, pairs, name):
    sd = zv.shape[0]
    tq = min(tq, sd)
    rc = min(rc, tq)
    has_sink = sink is not None
    scale = HEAD_DIM ** -0.5

    def body(q_ref, kp, kc, kn, vp, vc, vn, *rest):
        if has_sink:
            sink_ref, o_ref, lse_ref = rest
        else:
            o_ref, lse_ref = rest
        j = pl.program_id(1)
        q = q_ref[...] * scale
        k = jnp.concatenate([kp[...], kc[...], kn[...]], axis=0)
        v = jnp.concatenate([vp[...], vc[...], vn[...]], axis=0)
        biases = {r0: _band_bias(j * tq + r0, rc, blk, sd) for r0 in range(0, tq, rc)}
        low = lax.broadcasted_iota(jnp.int32, (1, 128), 1) < HEAD_DIM
        for qb, kb, vb, swaps, sinks in pairs:
            qp, kp_, vp_ = q[:, qb:qb + 128], k[:, kb:kb + 128], v[:, vb:vb + 128]
            if any(swaps):
                k_sw = jnp.concatenate([kp_[:, HEAD_DIM:], kp_[:, :HEAD_DIM]], axis=1)
                v_sw = jnp.concatenate([vp_[:, HEAD_DIM:], vp_[:, :HEAD_DIM]], axis=1)
            for r0 in range(0, tq, rc):
                outs, lses = [], []
                for half in range(2):
                    qm = jnp.where(low if half == 0 else ~low, qp[r0:r0 + rc], jnp.zeros((rc, 128), BF16))
                    kk, vv = (k_sw, v_sw) if swaps[half] else (kp_, vp_)
                    kk, vv = kk[r0:r0 + rc + 2 * blk], vv[r0:r0 + rc + 2 * blk]
                    sc = lax.dot_general(qm, kk, _NT, preferred_element_type=F32) + biases[r0]
                    m = jnp.max(sc, axis=-1, keepdims=True)
                    if has_sink:
                        m = jnp.maximum(m, sink_ref[0, sinks[half]])
                    p = jnp.exp(sc - m)
                    den = jnp.sum(p, axis=-1, keepdims=True)
                    if has_sink:
                        den = den + jnp.exp(sink_ref[0, sinks[half]] - m)
                    outs.append(lax.dot_general(p.astype(BF16), vv, _NN, preferred_element_type=F32) / den)
                    lses.append(m + jnp.log(den))
                o_ref[r0:r0 + rc, qb:qb + 128] = jnp.where(low, outs[0], outs[1])
                lse_ref[r0:r0 + rc, qb:qb + 128] = jnp.where(low, lses[0], lses[1])

    in_specs = ([pl.BlockSpec((tq, qw), lambda c, j: (j, c * (ptw // qw) + qcb))]
                + _band_specs(sd, blk, tq, kw, ptw // kw, kcb) + _band_specs(sd, blk, tq, kw, ptw // kw, vcb))
    args = [zv] * 7
    if has_sink:
        in_specs.append(pl.BlockSpec(memory_space=pltpu.SMEM))
        args.append(sink)
    o_spec = pl.BlockSpec((tq, qw), lambda c, j: (j, c))
    return _pcall(body, name=name, out_shape=(SDS((sd, d * qw), F32), SDS((sd, d * qw), F32)), grid=(d, sd // tq),
                  in_specs=in_specs, out_specs=(o_spec, o_spec), dims=("parallel", "parallel"))(*args)


def _banded_bwd(zv, ov, lv, dov, tv, sink, *, d, blk, tq, rc, ptw, qw, kw, qcb, kcb, vcb, pairs, name):
    sd = zv.shape[0]
    tq = min(tq, sd)
    nt = sd // tq
    r = tq // blk
    nbk = sd // blk
    has_sink = sink is not None
    scale = HEAD_DIM ** -0.5
    rc = min(rc, tq)
    kvw = 128 * len({kb for _, kb, _, _, _ in pairs})
    whole_class = sd * 2 * kvw * 2 <= WHOLE_CLASS_BYTES

    def add_rows(x, y, last):
        if tq == blk:
            return x + y
        if last:
            return jnp.concatenate([x[:tq - blk], x[tq - blk:] + y], axis=0)
        return jnp.concatenate([x[:blk] + y, x[blk:]], axis=0)

    def body(q_ref, kp, kc, kn, vp, vc, vn, o_ref, l_ref, do_ref, t_ref, tlag_ref, *rest):
        if has_sink:
            sink_ref, dq_ref, dkv_ref, dsink_ref, acck, accv, nxtk, nxtv = rest
        else:
            dq_ref, dkv_ref, acck, accv, nxtk, nxtv = rest
        j = pl.program_id(1)

        @pl.when(j == 0)
        def _():
            nxtk[...] = jnp.zeros_like(nxtk)
            nxtv[...] = jnp.zeros_like(nxtv)

        if has_sink:
            @pl.when((pl.program_id(0) == 0) & (j == 0))
            def _():
                dsink_ref[...] = jnp.zeros_like(dsink_ref)

        def emit(dk_rows, dv_rows, tabv, tile_idx):
            val = jnp.concatenate([_rope_apply(dk_rows, tabv, -1), dv_rows], axis=1).astype(BF16)
            if whole_class:
                dkv_ref[pl.ds(pl.multiple_of(tile_idx * tq, tq), tq), :] = val
            else:
                dkv_ref[...] = val

        @pl.when(j < nt)
        def _():
            q = q_ref[...] * scale
            k3 = jnp.concatenate([kp[...], kc[...], kn[...]], axis=0)
            v3 = jnp.concatenate([vp[...], vc[...], vn[...]], axis=0)
            o_t, l_t, do_t = o_ref[...], l_ref[...], do_ref[...]
            biases = {r0: _band_bias(j * tq + r0, rc, blk, sd) for r0 in range(0, tq, rc)}
            lane = lax.broadcasted_iota(jnp.int32, (1, 128), 1)
            low = lane < HEAD_DIM
            wide = tq + 2 * blk
            cw = rc + 2 * blk

            def place(x, r0):
                parts = ([jnp.zeros((r0, 128), F32)] if r0 else []) + [x]
                if wide - r0 - cw:
                    parts.append(jnp.zeros((wide - r0 - cw, 128), F32))
                return jnp.concatenate(parts, axis=0) if len(parts) > 1 else x

            dqs = []
            wks, wvs = {}, {}
            dsink_row = jnp.zeros((1, 128), F32)
            for qb, kb, vb, swaps, sinks in pairs:
                qp, kp_, vp_ = q[:, qb:qb + 128], k3[:, kb:kb + 128], v3[:, vb:vb + 128]
                if any(swaps):
                    k_sw = jnp.concatenate([kp_[:, HEAD_DIM:], kp_[:, :HEAD_DIM]], axis=1)
                    v_sw = jnp.concatenate([vp_[:, HEAD_DIM:], vp_[:, :HEAD_DIM]], axis=1)
                dop, lp = do_t[:, qb:qb + 128], l_t[:, qb:qb + 128]
                prod = dop * o_t[:, qb:qb + 128]
                dq_rows = []
                for r0 in range(0, tq, rc):
                    rows = slice(r0, r0 + rc)
                    dq_half = []
                    for half in range(2):
                        mine = low if half == 0 else ~low
                        qm = jnp.where(mine, qp[rows], jnp.zeros((rc, 128), BF16))
                        dob = jnp.where(mine, dop[rows], 0.0).astype(BF16)
                        delta = jnp.sum(jnp.where(mine, prod[rows], 0.0), axis=-1, keepdims=True)
                        lse = lp[rows, half * HEAD_DIM:half * HEAD_DIM + 1]
                        kk, vv = (k_sw, v_sw) if swaps[half] else (kp_, vp_)
                        kk, vv = kk[r0:r0 + cw], vv[r0:r0 + cw]
                        sc = lax.dot_general(qm, kk, _NT, preferred_element_type=F32) + biases[r0]
                        p = jnp.exp(sc - lse)
                        dp = lax.dot_general(dob, vv, _NT, preferred_element_type=F32)
                        dsb = (p * (dp - delta)).astype(BF16)
                        dq_half.append(lax.dot_general(dsb, kk, _NN, preferred_element_type=F32))
                        dk = lax.dot_general(dsb, qm, _TN, preferred_element_type=F32)
                        dv = lax.dot_general(p.astype(BF16), dob, _TN, preferred_element_type=F32)
                        if swaps[half]:
                            dk, dv = pltpu.roll(dk, HEAD_DIM, 1), pltpu.roll(dv, HEAD_DIM, 1)
                        wks[kb] = place(dk, r0) if kb not in wks else wks[kb] + place(dk, r0)
                        wvs[vb] = place(dv, r0) if vb not in wvs else wvs[vb] + place(dv, r0)
                        if has_sink:
                            psink = jnp.exp(sink_ref[0, sinks[half]] - lse)
                            dsink_row = dsink_row + jnp.where(lane == sinks[half], -jnp.sum(psink * delta), 0.0)
                    dq_rows.append(jnp.where(low, dq_half[0], dq_half[1]) * scale)
                dqs.append(jnp.concatenate(dq_rows, axis=0) if len(dq_rows) > 1 else dq_rows[0])
            dq_ref[...] = _rope_apply(jnp.concatenate(dqs, axis=1), t_ref[...], -1).astype(BF16)
            wk = jnp.concatenate([wks[b] for b in sorted(wks)], axis=1) if len(wks) > 1 else wks[min(wks)]
            wv = jnp.concatenate([wvs[b] for b in sorted(wvs)], axis=1) if len(wvs) > 1 else wvs[min(wvs)]
            if has_sink:
                dsink_ref[0:1, :] += dsink_row

            @pl.when(j > 0)
            def _():
                emit(add_rows(acck[...], wk[:blk], True), add_rows(accv[...], wv[:blk], True), tlag_ref[...], j - 1)

            acck[...] = add_rows(wk[blk:blk + tq], nxtk[...], False)
            accv[...] = add_rows(wv[blk:blk + tq], nxtv[...], False)
            nxtk[...] = wk[blk + tq:]
            nxtv[...] = wv[blk + tq:]
            if whole_class:
                @pl.when(j == nt - 1)
                def _():
                    emit(acck[...], accv[...], t_ref[...], j)

        if not whole_class:
            @pl.when(j == nt)
            def _():
                emit(acck[...], accv[...], tlag_ref[...], j - 1)

    def tile(width, per_tok, cb):
        return pl.BlockSpec((tq, width), lambda c, j: (jnp.minimum(j, nt - 1), c * per_tok + cb))

    def halos(width, per_tok, cb):
        before = pl.BlockSpec((blk, width), lambda c, j: (jnp.maximum(jnp.minimum(j, nt - 1) * r - 1, 0), c * per_tok + cb))
        after = pl.BlockSpec((blk, width),
                             lambda c, j: (jnp.minimum((jnp.minimum(j, nt - 1) + 1) * r, nbk - 1), c * per_tok + cb))
        return [before, tile(width, per_tok, cb), after]

    def lagged(width):
        return pl.BlockSpec((tq, width), lambda c, j: (jnp.maximum(j - 1, 0), c))

    in_specs = ([tile(qw, ptw // qw, qcb)] + halos(kw, ptw // kw, kcb) + halos(kw, ptw // kw, vcb)
                + [tile(qw, 1, 0)] * 3 + [tile(TABW, 1, 0), lagged(TABW)])
    args = [zv] * 7 + [ov, lv, dov, tv, tv]
    out_shape = [SDS((sd, d * qw), BF16), SDS((sd, d * 2 * kvw), BF16)]
    out_specs = [tile(qw, 1, 0),
                 pl.BlockSpec((sd, 2 * kvw), lambda c, j: (0, c)) if whole_class else lagged(2 * kvw)]
    if has_sink:
        in_specs.append(pl.BlockSpec(memory_space=pltpu.SMEM))
        args.append(sink)
        out_shape.append(SDS((8, 128), F32))
        out_specs.append(pl.BlockSpec((8, 128), lambda c, j: (0, 0)))
    scratch = [pltpu.VMEM((tq, kvw), F32), pltpu.VMEM((tq, kvw), F32), pltpu.VMEM((blk, kvw), F32),
               pltpu.VMEM((blk, kvw), F32)]
    return _pcall(body, name=name, out_shape=tuple(out_shape), grid=(d, nt + (0 if whole_class else 1)), in_specs=in_specs,
                  out_specs=tuple(out_specs), scratch_shapes=scratch, dims=("arbitrary", "arbitrary"))(*args)


_WIN_PAIRS = tuple((128 * p, 0, 128, (False, True) if p < 2 else (True, False), (2 * p, 2 * p + 1)) for p in range(4))
_WIN_CFG = dict(d=1, blk=WIN_HALF, tq=ATTN_TQ, rc=256, ptw=768, qw=512, kw=256, qcb=0, kcb=2, vcb=2, pairs=_WIN_PAIRS)
_DIL_PAIRS = tuple((128 * p, 128 * p, 128 * p, (False, False), (2 * p, 2 * p + 1)) for p in range(4))


def _dil_cfg(gi):
    return dict(d=DILATIONS[gi], blk=DIL_SIDE, tq=ATTN_TQ, rc=128, ptw=1536, qw=512, kw=512, qcb=0, kcb=1, vcb=2,
                pairs=_DIL_PAIRS)


def _view_specs(tm, width):
    return tuple(pl.BlockSpec((tm // d, d * width), lambda i: (i, 0)) for d in DILATIONS)


def _mix_norm_fwd(oa, og_views, lg_views, g_win, g_dil):
    s = oa.shape[0]
    tm = _pick(s, (512,))

    def body(oa_ref, o0, o1, o2, l0, l1, l2, gw_ref, gd_ref, mixed_ref, ob0, ob1, ob2, lb0, lb1, lb2, scr, ob_s):
        o_refs, l_refs, ob_refs, lb_refs = (o0, o1, o2), (l0, l1, l2), (ob0, ob1, ob2), (lb0, lb1, lb2)
        ssq = jnp.zeros((tm, 1), F32)
        for q in range(4):
            os_, ls_ = [], []
            for g, d in enumerate(DILATIONS):
                cols = [slice(c * 512 + q * 128, c * 512 + (q + 1) * 128) for c in range(d)]
                os_.append(_from_classes([o_refs[g][:, cs] for cs in cols], scr))
                ls_.append(_from_classes([l_refs[g][:, cs] for cs in cols], scr))
            mx = jnp.maximum(jnp.maximum(ls_[0], ls_[1]), ls_[2])
            es = [jnp.exp(l - mx) for l in ls_]
            den = es[0] + es[1] + es[2]
            ob = (es[0] / den) * os_[0] + (es[1] / den) * os_[1] + (es[2] / den) * os_[2]
            lb = mx + jnp.log(den)
            ob_s[:, q * 128:(q + 1) * 128] = ob
            ssq = ssq + jnp.sum(ob * ob, axis=-1, keepdims=True)
            for g, d in enumerate(DILATIONS):
                for val, refs in ((ob, ob_refs), (lb, lb_refs)):
                    for c, part in enumerate(_to_classes(val, scr, d)):
                        refs[g][:, c * 512 + q * 128:c * 512 + (q + 1) * 128] = part
        a = oa_ref[...]
        ra = lax.rsqrt(jnp.mean(a * a, axis=-1, keepdims=True) + LN_EPS)
        rb = lax.rsqrt(ssq * (1.0 / 512) + LN_EPS)
        mixed_ref[...] = jnp.concatenate([a * ra * gw_ref[...], ob_s[...] * rb * gd_ref[...]], axis=1).astype(BF16)

    row = pl.BlockSpec((tm, 512), lambda i: (i, 0))
    vec = pl.BlockSpec((1, 512), lambda i: (0, 0))
    views = _view_specs(tm, 512)
    view_shapes = tuple(SDS((s // d, d * 512), F32) for d in DILATIONS)
    res = _pcall(body, name="mix_norm_fwd", out_shape=(SDS((s, 1024), BF16),) + view_shapes * 2, grid=(s // tm,),
                 in_specs=[row, *views, *views, vec, vec],
                 out_specs=(pl.BlockSpec((tm, 1024), lambda i: (i, 0)),) + views * 2,
                 scratch_shapes=[pltpu.VMEM((tm, 128), F32), pltpu.VMEM((tm, 512), F32)],
                 dims=("parallel",))(oa, *og_views, *lg_views, g_win, g_dil)
    return res[0], res[1:4], res[4:7]


def _mix_norm_bwd(oa, ob, dmixed, g_win, g_dil):
    s = oa.shape[0]
    tm = _pick(s, (512,))
    nt = s // tm

    def body(oa_ref, ob_ref, dm_ref, gw_ref, gd_ref, doa_ref, db0, db1, db2, dgw_ref, dgd_ref, acc_w, acc_d, scr):
        i = pl.program_id(0)

        @pl.when(i == 0)
        def _():
            acc_w[...] = jnp.zeros_like(acc_w)
            acc_d[...] = jnp.zeros_like(acc_d)

        dm = dm_ref[...]
        dxs = []
        for x_ref, g_ref, dy, acc in ((oa_ref, gw_ref, dm[:, :512], acc_w), (ob_ref, gd_ref, dm[:, 512:], acc_d)):
            x = x_ref[...]
            r = lax.rsqrt(jnp.mean(x * x, axis=-1, keepdims=True) + LN_EPS)
            dyg = dy * g_ref[...]
            dxs.append(r * dyg - x * (r * r * r) * jnp.mean(dyg * x, axis=-1, keepdims=True))
            acc[...] += jnp.sum((dy * x * r).reshape(tm // 8, 8, 512), axis=0)
        doa_ref[...] = dxs[0]
        for q in range(4):
            dq = dxs[1][:, q * 128:(q + 1) * 128]
            for db_ref, d in zip((db0, db1, db2), DILATIONS):
                for c, part in enumerate(_to_classes(dq, scr, d)):
                    db_ref[:, c * 512 + q * 128:c * 512 + (q + 1) * 128] = part

        @pl.when(i == nt - 1)
        def _():
            dgw_ref[...] = jnp.sum(acc_w[...], axis=0, keepdims=True)
            dgd_ref[...] = jnp.sum(acc_d[...], axis=0, keepdims=True)

    row = pl.BlockSpec((tm, 512), lambda i: (i, 0))
    vec = pl.BlockSpec((1, 512), lambda i: (0, 0))
    views = _view_specs(tm, 512)
    view_shapes = tuple(SDS((s // d, d * 512), F32) for d in DILATIONS)
    res = _pcall(body, name="mix_norm_bwd",
                 out_shape=(SDS((s, 512), F32),) + view_shapes + (SDS((1, 512), F32), SDS((1, 512), F32)),
                 grid=(nt,), in_specs=[row, row, pl.BlockSpec((tm, 1024), lambda i: (i, 0)), vec, vec],
                 out_specs=(row,) + views + (vec, vec),
                 scratch_shapes=[pltpu.VMEM((8, 512), F32), pltpu.VMEM((8, 512), F32), pltpu.VMEM((tm, 128), F32)],
                 dims=("arbitrary",))(oa, ob, dmixed, g_win, g_dil)
    return res[0], res[1:4], res[4], res[5]


def _dz_assemble(dq_views, dkv_views, dqa, dkva):
    s = dqa.shape[0]
    tm = _pick(s, (512,))

    def body(q0, q1, q2, kv0, kv1, kv2, qa_ref, kva_ref, o_ref, scr):
        for g, d in enumerate(DILATIONS):
            for kind, (ref, width, base) in enumerate((((q0, q1, q2)[g], 512, 0), ((kv0, kv1, kv2)[g], 1024, 0),
                                                       ((kv0, kv1, kv2)[g], 1024, 512))):
                for q in range(4):
                    src = base + q * 128
                    dst = W_IN_QB + kind * 1536 + g * 512 + q * 128
                    if d == 1:
                        o_ref[:, dst:dst + 128] = ref[:, src:src + 128]
                    else:
                        parts = [ref[:, c * width + src:c * width + src + 128].astype(F32) for c in range(d)]
                        o_ref[:, dst:dst + 128] = _from_classes(parts, scr).astype(BF16)
        o_ref[:, W_IN_QA:W_IN_QA + 512] = qa_ref[...]
        o_ref[:, W_IN_KA:W_IN_KA + 256] = kva_ref[...]

    return _pcall(body, name="dz_assemble", out_shape=SDS((s, IN_WIDTH), BF16), grid=(s // tm,),
                  in_specs=[*_view_specs(tm, 512), *_view_specs(tm, 1024), pl.BlockSpec((tm, 512), lambda i: (i, 0)),
                            pl.BlockSpec((tm, 256), lambda i: (i, 0))],
                  out_specs=pl.BlockSpec((tm, IN_WIDTH), lambda i: (i, 0)),
                  scratch_shapes=[pltpu.VMEM((tm, 128), F32)], dims=("parallel",))(*dq_views, *dkv_views, dqa, dkva)


def _ln_fwd(a, r, g, b, ca, name):
    s = a.shape[0]
    tm = _pick(s, (512, 256))
    has_r = r is not None

    def body(*refs):
        a_ref = refs[0]
        r_ref = refs[1] if has_r else None
        g_ref, b_ref, o_ref, ob_ref = refs[1 + has_r:]
        u = a_ref[...] if ca == 1.0 else ca * a_ref[...]
        if has_r:
            u = u + r_ref[...]
        mu = jnp.mean(u, axis=-1, keepdims=True)
        xc = u - mu
        var = jnp.mean(xc * xc, axis=-1, keepdims=True)
        y = xc * lax.rsqrt(var + LN_EPS) * g_ref[...] + b_ref[...]
        o_ref[...] = y
        ob_ref[...] = y.astype(BF16)

    row = pl.BlockSpec((tm, D_MODEL), lambda i: (i, 0))
    vec = pl.BlockSpec((1, D_MODEL), lambda i: (0, 0))
    args = [a] + ([r] if has_r else []) + [g, b]
    return _pcall(body, name=name, out_shape=(SDS((s, D_MODEL), F32), SDS((s, D_MODEL), BF16)), grid=(s // tm,),
                  in_specs=[row] * (1 + has_r) + [vec, vec], out_specs=(row, row), dims=("parallel",))(*args)


def _mm_ln_fwd(a, w, resid, g, b, ca, name):
    s, k = a.shape
    tm = _pick(s, (512, 256))

    def body(a_ref, w_ref, res_ref, g_ref, b_ref, r_ref, o_ref, ob_ref):
        rv = lax.dot_general(a_ref[...], w_ref[...], _NN, preferred_element_type=F32)
        r_ref[...] = rv
        u = ca * res_ref[...] + rv
        mu = jnp.mean(u, axis=-1, keepdims=True)
        xc = u - mu
        var = jnp.mean(xc * xc, axis=-1, keepdims=True)
        y = xc * lax.rsqrt(var + LN_EPS) * g_ref[...] + b_ref[...]
        o_ref[...] = y
        ob_ref[...] = y.astype(BF16)

    row = pl.BlockSpec((tm, D_MODEL), lambda i: (i, 0))
    vec = pl.BlockSpec((1, D_MODEL), lambda i: (0, 0))
    return _pcall(body, name=name, out_shape=(SDS((s, D_MODEL), F32), SDS((s, D_MODEL), F32), SDS((s, D_MODEL), BF16)),
                  grid=(s // tm,),
                  in_specs=[pl.BlockSpec((tm, k), lambda i: (i, 0)), pl.BlockSpec((k, D_MODEL), lambda i: (0, 0)), row, vec, vec],
                  out_specs=(row, row, row), dims=("parallel",))(a, w, resid, g, b)


def _mm_ln_loss(x, w, a, target, g, b, ca, name):
    s, k = x.shape
    tm = _pick(s, (512, 256))
    nt = s // tm

    def body(x_ref, w_ref, a_ref, t_ref, g_ref, b_ref, du_ref, dub_ref, dg_ref, db_ref, loss_ref, acc_g, acc_b, acc_l):
        i = pl.program_id(0)

        @pl.when(i == 0)
        def _():
            acc_g[...] = jnp.zeros_like(acc_g)
            acc_b[...] = jnp.zeros_like(acc_b)
            acc_l[...] = jnp.zeros_like(acc_l)

        u = ca * a_ref[...] + lax.dot_general(x_ref[...], w_ref[...], _NN, preferred_element_type=F32)
        mu = jnp.mean(u, axis=-1, keepdims=True)
        xc = u - mu
        var = jnp.mean(xc * xc, axis=-1, keepdims=True)
        rstd = lax.rsqrt(var + LN_EPS)
        xhat = xc * rstd
        gv = g_ref[...]
        err = (xhat * gv + b_ref[...]) - t_ref[...]
        acc_l[...] += jnp.sum((err * err).reshape(tm // 8, 8, D_MODEL), axis=0)
        dyv = err * (1.0 / D_MODEL)
        dxh = dyv * gv
        du = rstd * (dxh - jnp.mean(dxh, axis=-1, keepdims=True) - xhat * jnp.mean(dxh * xhat, axis=-1, keepdims=True))
        du_ref[...] = du
        dub_ref[...] = du.astype(BF16)
        acc_g[...] += jnp.sum((dyv * xhat).reshape(tm // 8, 8, D_MODEL), axis=0)
        acc_b[...] += jnp.sum(dyv.reshape(tm // 8, 8, D_MODEL), axis=0)

        @pl.when(i == nt - 1)
        def _():
            dg_ref[...] = jnp.sum(acc_g[...], axis=0, keepdims=True)
            db_ref[...] = jnp.sum(acc_b[...], axis=0, keepdims=True)
            tot = jnp.sum(jnp.sum(acc_l[...], axis=0, keepdims=True), axis=1, keepdims=True)
            loss_ref[...] = tot * (0.5 / D_MODEL)

    row = pl.BlockSpec((tm, D_MODEL), lambda i: (i, 0))
    vec = pl.BlockSpec((1, D_MODEL), lambda i: (0, 0))
    return _pcall(body, name=name,
                  out_shape=(SDS((s, D_MODEL), F32), SDS((s, D_MODEL), BF16), SDS((1, D_MODEL), F32), SDS((1, D_MODEL), F32),
                             SDS((1, 1), F32)),
                  grid=(nt,),
                  in_specs=[pl.BlockSpec((tm, k), lambda i: (i, 0)), pl.BlockSpec((k, D_MODEL), lambda i: (0, 0)), row, row,
                            vec, vec],
                  out_specs=(row, row, vec, vec, pl.BlockSpec((1, 1), lambda i: (0, 0))),
                  scratch_shapes=[pltpu.VMEM((8, D_MODEL), F32)] * 3, dims=("arbitrary",))(x, w, a, target, g, b)


def _mm_ln_bwd(pairs, addend, coef, a, r, g, ca, name, after=None):
    s = a.shape[0]
    has_r = r is not None
    has_add = addend is not None
    extra = [] if after is None else [after]
    n_pairs = len(pairs)

    def vmem(tm):
        tot = tm * D_MODEL * (4 * (2 + has_r) + 6)
        for pa, _, _ in pairs:
            tot += tm * pa.shape[1] * pa.dtype.itemsize
        return 2 * tot + sum(pb.size * pb.dtype.itemsize for _, pb, _ in pairs)

    tm = next(c for c in (512, 256, 128) if s % c == 0 and vmem(c) <= MM_VMEM_BUDGET)
    nt = s // tm

    def body(*refs):
        ins = refs[2 * n_pairs:]
        add_ref = ins[0] if has_add else None
        ins = ins[has_add:]
        a_ref = ins[0]
        r_ref = ins[1] if has_r else None
        g_ref = ins[1 + has_r]
        du_ref, dub_ref, dg_ref, db_ref, acc_g, acc_b = ins[2 + has_r + len(extra):]
        i = pl.program_id(0)

        @pl.when(i == 0)
        def _():
            acc_g[...] = jnp.zeros_like(acc_g)
            acc_b[...] = jnp.zeros_like(acc_b)

        dyv = coef * add_ref[...] if has_add else None
        for p, (_, _, tb) in enumerate(pairs):
            part = lax.dot_general(refs[2 * p][...].astype(BF16), refs[2 * p + 1][...], _NT if tb else _NN,
                                   preferred_element_type=F32)
            dyv = part if dyv is None else dyv + part
        u = a_ref[...] if ca == 1.0 else ca * a_ref[...]
        if has_r:
            u = u + r_ref[...]
        mu = jnp.mean(u, axis=-1, keepdims=True)
        xc = u - mu
        var = jnp.mean(xc * xc, axis=-1, keepdims=True)
        rstd = lax.rsqrt(var + LN_EPS)
        xhat = xc * rstd
        dxh = dyv * g_ref[...]
        du = rstd * (dxh - jnp.mean(dxh, axis=-1, keepdims=True) - xhat * jnp.mean(dxh * xhat, axis=-1, keepdims=True))
        du_ref[...] = du
        dub_ref[...] = du.astype(BF16)
        acc_g[...] += jnp.sum((dyv * xhat).reshape(tm // 8, 8, D_MODEL), axis=0)
        acc_b[...] += jnp.sum(dyv.reshape(tm // 8, 8, D_MODEL), axis=0)

        @pl.when(i == nt - 1)
        def _():
            dg_ref[...] = jnp.sum(acc_g[...], axis=0, keepdims=True)
            db_ref[...] = jnp.sum(acc_b[...], axis=0, keepdims=True)

    row = pl.BlockSpec((tm, D_MODEL), lambda i: (i, 0))
    vec = pl.BlockSpec((1, D_MODEL), lambda i: (0, 0))
    in_specs, args = [], []
    for pa, pb, _ in pairs:
        in_specs += [pl.BlockSpec((tm, pa.shape[1]), lambda i: (i, 0)),
                     pl.BlockSpec(pb.shape, lambda i: (0, 0), pipeline_mode=pl.Buffered(1))]
        args += [pa, pb]
    in_specs += [row] * (has_add + 1 + has_r) + [vec] + [pl.BlockSpec((8, 128), lambda i: (0, 0))] * len(extra)
    args += ([addend] if has_add else []) + [a] + ([r] if has_r else []) + [g] + extra
    return _pcall(body, name=name,
                  out_shape=(SDS((s, D_MODEL), F32), SDS((s, D_MODEL), BF16), SDS((1, D_MODEL), F32), SDS((1, D_MODEL), F32)),
                  grid=(nt,), in_specs=in_specs, out_specs=(row, row, vec, vec),
                  scratch_shapes=[pltpu.VMEM((8, D_MODEL), F32), pltpu.VMEM((8, D_MODEL), F32)],
                  dims=("arbitrary",))(*args)


def _xattn_fwd(q, k, v):
    s = q.shape[0]
    tq = _pick(s, (512,))
    scale = X_HEAD_DIM ** -0.5

    def body(q_ref, k_ref, v_ref, o_ref, ob_ref):
        qv, kv, vv = q_ref[...], k_ref[...], v_ref[...]
        outs = []
        for h in range(X_HEADS):
            sl = slice(h * X_HEAD_DIM, (h + 1) * X_HEAD_DIM)
            sc = lax.dot_general(qv[:, sl], kv[:, sl], _NT, preferred_element_type=F32) * scale
            e = jnp.exp(sc - jnp.max(sc, axis=-1, keepdims=True))
            p = e / jnp.sum(e, axis=-1, keepdims=True)
            outs.append(lax.dot_general(p.astype(BF16), vv[:, sl], _NN, preferred_element_type=F32))
        o = jnp.concatenate(outs, axis=1)
        o_ref[...] = o
        ob_ref[...] = o.astype(BF16)

    row = pl.BlockSpec((tq, D_MODEL), lambda i: (i, 0))
    full = pl.BlockSpec((MEM_LEN, D_MODEL), lambda i: (0, 0))
    return _pcall(body, name="xattn_fwd", out_shape=(SDS((s, D_MODEL), F32), SDS((s, D_MODEL), BF16)), grid=(s // tq,),
                  in_specs=[row, full, full], out_specs=(row, row), dims=("parallel",))(q, k, v)


def _xattn_bwd(q, k, v, o, do):
    s = q.shape[0]
    tq = _pick(s, (512,))
    scale = X_HEAD_DIM ** -0.5

    def body(q_ref, k_ref, v_ref, o_ref, do_ref, dq_ref, dk_ref, dv_ref):
        i = pl.program_id(0)

        @pl.when(i == 0)
        def _():
            dk_ref[...] = jnp.zeros_like(dk_ref)
            dv_ref[...] = jnp.zeros_like(dv_ref)

        qv, kv, vv, ov, dov = q_ref[...], k_ref[...], v_ref[...], o_ref[...], do_ref[...]
        dqs, dks, dvs = [], [], []
        for h in range(X_HEADS):
            sl = slice(h * X_HEAD_DIM, (h + 1) * X_HEAD_DIM)
            sc = lax.dot_general(qv[:, sl], kv[:, sl], _NT, preferred_element_type=F32) * scale
            e = jnp.exp(sc - jnp.max(sc, axis=-1, keepdims=True))
            p = e / jnp.sum(e, axis=-1, keepdims=True)
            doh = dov[:, sl]
            dob = doh.astype(BF16)
            delta = jnp.sum(doh * ov[:, sl], axis=-1, keepdims=True)
            dvs.append(lax.dot_general(p.astype(BF16), dob, _TN, preferred_element_type=F32))
            dp = lax.dot_general(dob, vv[:, sl], _NT, preferred_element_type=F32)
            ds = (p * (dp - delta)).astype(BF16)
            dqs.append(lax.dot_general(ds, kv[:, sl], _NN, preferred_element_type=F32) * scale)
            dks.append(lax.dot_general(ds, qv[:, sl], _TN, preferred_element_type=F32) * scale)
        dq_ref[...] = jnp.concatenate(dqs, axis=1).astype(BF16)
        dk_ref[...] += jnp.concatenate(dks, axis=1)
        dv_ref[...] += jnp.concatenate(dvs, axis=1)

    row = pl.BlockSpec((tq, D_MODEL), lambda i: (i, 0))
    full = pl.BlockSpec((MEM_LEN, D_MODEL), lambda i: (0, 0))
    return _pcall(body, name="xattn_bwd",
                  out_shape=(SDS((s, D_MODEL), BF16), SDS((MEM_LEN, D_MODEL), F32), SDS((MEM_LEN, D_MODEL), F32)),
                  grid=(s // tq,), in_specs=[row, full, full, row, row], out_specs=(row, full, full),
                  dims=("arbitrary",))(q, k, v, o, do)


_SQRT_HALF = 0.7071067811865476
_INV_SQRT_2PI = 0.3989422804014327


def _halo_specs(s, tm, width, rows=8):
    nb = s // rows
    r = tm // rows
    prev = pl.BlockSpec((rows, width), lambda i: (jnp.maximum(i * r - 1, 0), 0))
    nxt = pl.BlockSpec((rows, width), lambda i: (jnp.minimum((i + 1) * r, nb - 1), 0))
    return prev, nxt


def _shifted(x, before_row, after_row, i, nt):
    tm = x.shape[0]
    row = lax.broadcasted_iota(jnp.int32, x.shape, 0)
    first = jnp.where(i == 0, 0.0, 1.0) * before_row
    last = jnp.where(i == nt - 1, 0.0, 1.0) * after_row
    xm1 = jnp.where(row == 0, first, pltpu.roll(x, 1, 0))
    xp1 = jnp.where(row == tm - 1, last, pltpu.roll(x, tm - 1, 0))
    return xm1, xp1


BF16_ROWS = 16


def _ffn_fwd(hb, wg_t, wu_t, cw, cb):
    s = hb.shape[0]
    tm = _pick(s, (256,))
    nt = s // tm
    hr = BF16_ROWS

    def body(h_ref, hp_ref, hn_ref, wg_ref, wu_ref, cw_ref, cb_ref, g_ref, up_ref, gc_ref, act_ref):
        i = pl.program_id(0)
        hv = h_ref[...]
        g_ext = lax.dot_general(jnp.concatenate([hp_ref[...], hv, hn_ref[...]], axis=0), wg_ref[...], _NT,
                                preferred_element_type=F32)
        gv = g_ext[hr:hr + tm]
        upv = lax.dot_general(hv, wu_ref[...], _NT, preferred_element_type=F32)
        gm1, gp1 = _shifted(gv, g_ext[hr - 1:hr], g_ext[hr + tm:hr + tm + 1], i, nt)
        gc = gm1 * cw_ref[0:1, :] + gv * cw_ref[1:2, :] + gp1 * cw_ref[2:3, :] + cb_ref[...]
        cdf = 0.5 * (1.0 + lax.erf(gc * _SQRT_HALF))
        g_ref[...] = gv
        up_ref[...] = upv
        gc_ref[...] = gc
        act_ref[...] = (gc * cdf * upv).astype(BF16)

    hrow = pl.BlockSpec((tm, D_MODEL), lambda i: (i, 0))
    prev, nxt = _halo_specs(s, tm, D_MODEL, hr)
    wfull = pl.BlockSpec((D_FF, D_MODEL), lambda i: (0, 0), pipeline_mode=pl.Buffered(1))
    row = pl.BlockSpec((tm, D_FF), lambda i: (i, 0))
    return _pcall(body, name="ffn_fwd", out_shape=(SDS((s, D_FF), F32),) * 3 + (SDS((s, D_FF), BF16),),
                  grid=(nt,), in_specs=[hrow, prev, nxt, wfull, wfull, pl.BlockSpec((8, D_FF), lambda i: (0, 0)),
                                        pl.BlockSpec((1, D_FF), lambda i: (0, 0))],
                  out_specs=(row, row, row, row), dims=("parallel",))(hb, hb, hb, wg_t, wu_t, cw, cb)


def _ffn_bwd(dffb, w_down, g, gc, up, cw):
    s = g.shape[0]
    tm = _pick(s, (256,))
    nt = s // tm
    hr = BF16_ROWS

    def body(df_ref, dfp_ref, dfn_ref, wd_ref, g_ref, gc_ref, gcp_ref, gcn_ref, up_ref, upp_ref, upn_ref, cw_ref,
             dg_ref, dup_ref, dcw_ref, dcb_ref, a0, a1, a2, a3):
        i = pl.program_id(0)

        @pl.when(i == 0)
        def _():
            for a in (a0, a1, a2, a3):
                a[...] = jnp.zeros_like(a)

        def d_conv_out(gc_, up_, da_):
            cdf_ = 0.5 * (1.0 + lax.erf(gc_ * _SQRT_HALF))
            pdf_ = jnp.exp(-0.5 * gc_ * gc_) * _INV_SQRT_2PI
            return da_ * up_ * (cdf_ + gc_ * pdf_), cdf_

        df_ext = jnp.concatenate([dfp_ref[...], df_ref[...], dfn_ref[...]], axis=0)
        tn = 256
        for c0 in range(0, D_FF, tn):
            cs = slice(c0, c0 + tn)
            da_ext = lax.dot_general(df_ext, wd_ref[cs, :], _NT, preferred_element_type=F32)
            cw0, cw1, cw2 = cw_ref[0:1, cs], cw_ref[1:2, cs], cw_ref[2:3, cs]
            gc = gc_ref[:, cs]
            da = da_ext[hr:hr + tm]
            dgc, cdf = d_conv_out(gc, up_ref[:, cs], da)
            dup_ref[:, cs] = (da * (gc * cdf)).astype(BF16)
            dgc_b = jnp.where(i == 0, 0.0, 1.0) * d_conv_out(gcp_ref[7:8, cs], upp_ref[7:8, cs], da_ext[hr - 1:hr])[0]
            dgc_a = jnp.where(i == nt - 1, 0.0, 1.0) * d_conv_out(gcn_ref[0:1, cs], upn_ref[0:1, cs],
                                                                  da_ext[hr + tm:hr + tm + 1])[0]
            row = lax.broadcasted_iota(jnp.int32, dgc.shape, 0)
            dgc_m1 = jnp.where(row == 0, dgc_b, pltpu.roll(dgc, 1, 0))
            dgc_p1 = jnp.where(row == tm - 1, dgc_a, pltpu.roll(dgc, tm - 1, 0))
            dg_ref[:, cs] = (dgc_p1 * cw0 + dgc * cw1 + dgc_m1 * cw2).astype(BF16)

            def fold(t):
                return jnp.sum(t.reshape(tm // 8, 8, tn), axis=0)

            gv = g_ref[:, cs]
            a0[:, cs] += fold(dgc_p1 * gv)
            a1[:, cs] += fold(dgc * gv)
            a2[:, cs] += fold(dgc_m1 * gv)
            a3[:, cs] += fold(dgc)

        @pl.when(i == nt - 1)
        def _():
            dcw_ref[...] = jnp.concatenate(
                [jnp.sum(a[...], axis=0, keepdims=True) for a in (a0, a1, a2)] + [jnp.zeros((5, D_FF), F32)], axis=0)
            dcb_ref[...] = jnp.sum(a3[...], axis=0, keepdims=True)

    row = pl.BlockSpec((tm, D_FF), lambda i: (i, 0))
    prev, nxt = _halo_specs(s, tm, D_FF)
    cw_spec = pl.BlockSpec((8, D_FF), lambda i: (0, 0))
    cb_spec = pl.BlockSpec((1, D_FF), lambda i: (0, 0))
    dprev, dnxt = _halo_specs(s, tm, D_MODEL, hr)
    return _pcall(body, name="ffn_bwd",
                  out_shape=(SDS((s, D_FF), BF16), SDS((s, D_FF), BF16), SDS((8, D_FF), F32), SDS((1, D_FF), F32)),
                  grid=(nt,),
                  in_specs=[pl.BlockSpec((tm, D_MODEL), lambda i: (i, 0)), dprev, dnxt,
                            pl.BlockSpec((D_FF, D_MODEL), lambda i: (0, 0), pipeline_mode=pl.Buffered(1)), row]
                  + [row, prev, nxt] * 2 + [cw_spec],
                  out_specs=(row, row, cw_spec, cb_spec), scratch_shapes=[pltpu.VMEM((8, D_FF), F32)] * 4,
                  dims=("arbitrary",))(dffb, dffb, dffb, w_down, g, gc, gc, gc, up, up, up, cw)


def _adamw(w, g, m, v, name):
    rows, cols = w.shape
    tr = _pick(rows, (256, 128, 64, 32, 16, 8))
    c1 = 1.0 - ADAM_B1 ** ADAM_STEP
    c2 = 1.0 - ADAM_B2 ** ADAM_STEP

    def body(w_ref, g_ref, m_ref, v_ref, d_ref, nm_ref, nv_ref):
        gv = g_ref[...]
        nm = ADAM_B1 * m_ref[...] + (1.0 - ADAM_B1) * gv
        nv = ADAM_B2 * v_ref[...] + (1.0 - ADAM_B2) * (gv * gv)
        d_ref[...] = -ADAM_LR * ((nm / c1) / (jnp.sqrt(nv / c2) + ADAM_EPS) + ADAM_WD * w_ref[...])
        nm_ref[...] = nm
        nv_ref[...] = nv

    blk = pl.BlockSpec((tr, cols), lambda i: (i, 0))
    return _pcall(body, name=name, out_shape=(SDS(w.shape, F32),) * 3, grid=(rows // tr,), in_specs=[blk] * 4,
                  out_specs=(blk,) * 3, dims=("parallel",))(w, g, m, v)


def _adamw_many(ws, gs, ms, vs, name):
    n = len(ws)
    c1 = 1.0 - ADAM_B1 ** ADAM_STEP
    c2 = 1.0 - ADAM_B2 ** ADAM_STEP

    def body(*refs):
        outs = refs[4 * n:]
        for k in range(n):
            gv = refs[n + k][...]
            nm = ADAM_B1 * refs[2 * n + k][...] + (1.0 - ADAM_B1) * gv
            nv = ADAM_B2 * refs[3 * n + k][...] + (1.0 - ADAM_B2) * (gv * gv)
            outs[k][...] = -ADAM_LR * ((nm / c1) / (jnp.sqrt(nv / c2) + ADAM_EPS) + ADAM_WD * refs[k][...])
            outs[n + k][...] = nm
            outs[2 * n + k][...] = nv

    shapes = tuple(SDS(w.shape, F32) for w in ws)
    res = _pcall(body, name=name, out_shape=shapes * 3)(*ws, *gs, *ms, *vs)
    return res[:n], res[n:2 * n], res[2 * n:]


def _all_gather_rows(x_shard, *, name, in_vmem, sum_rows=False, after=None):
    m_per, n = x_shard.shape
    extra = [] if after is None else [after]

    def body(x_ref, *rest):
        out_ref, rest = rest[len(extra)], rest[len(extra) + 1:]
        if sum_rows:
            sum_ref, send_sems, recv_sems, local_sem = rest
        else:
            send_sems, recv_sems, local_sem = rest
        x, y, c = lax.axis_index("x"), lax.axis_index("y"), lax.axis_index("c")
        me, sibling = (x, y, c), (x, y, 1 - c)
        chips = [(1 - x, y), (x, 1 - y), (1 - x, 1 - y)]

        def rows(px, py, pc):
            return out_ref.at[pl.ds((4 * px + 2 * py + pc) * m_per, m_per), :]

        def copy(k, block, to, src=None):
            return pltpu.make_async_remote_copy(
                src_ref=rows(*block) if src is None else src, dst_ref=rows(*block), send_sem=send_sems.at[k],
                recv_sem=recv_sems.at[k], device_id=to, device_id_type=pl.DeviceIdType.MESH)

        mine = pltpu.make_async_copy(x_ref, rows(*me), local_sem)
        mine.start()
        first = [copy(0, me, sibling, src=x_ref)]
        first += [copy(1 + j, me, (*chip, c), src=x_ref) for j, chip in enumerate(chips)]
        for cp in first:
            cp.start()
        passed = [copy(4 + j, (*chip, c), sibling) for j, chip in enumerate(chips)]
        for j, chip in enumerate(chips):
            copy(1 + j, (*chip, c), me).wait_recv()
            passed[j].start()
        copy(0, sibling, me).wait_recv()
        for j, chip in enumerate(chips):
            copy(4 + j, (*chip, 1 - c), me).wait_recv()
        for cp in first + passed:
            cp.wait_send()
        mine.wait()
        if sum_rows:
            acc = out_ref[0:m_per, :]
            for dev in range(1, N_DEV):
                acc = acc + out_ref[dev * m_per:(dev + 1) * m_per, :]
            sum_ref[...] = acc

    space = pltpu.VMEM if in_vmem else pl.ANY
    out_shape = [SDS((N_DEV * m_per, n), x_shard.dtype)]
    out_specs = [pl.BlockSpec(memory_space=space)]
    if sum_rows:
        out_shape.append(SDS((m_per, n), x_shard.dtype))
        out_specs.append(pl.BlockSpec(memory_space=pltpu.VMEM))
    res = _PALLAS_CALL(
        body, name=name, out_shape=tuple(out_shape),
        in_specs=[pl.BlockSpec(memory_space=space)] + [pl.BlockSpec(memory_space=pl.ANY)] * len(extra),
        out_specs=tuple(out_specs),
        scratch_shapes=[pltpu.SemaphoreType.DMA((7,)), pltpu.SemaphoreType.DMA((7,)), pltpu.SemaphoreType.DMA],
        compiler_params=pltpu.CompilerParams(vmem_limit_bytes=VMEM_LIMIT_BYTES),
    )(x_shard, *extra)
    return res if sum_rows else res[0]


_HBM = pl.BlockSpec(memory_space=pltpu.HBM)
_SEM = pl.BlockSpec(memory_space=pltpu.SEMAPHORE)
_SPLIT_PARAMS = dict(has_side_effects=pltpu.SideEffectType.DATAFLOW_SIDE_EFFECTING)


def _split_copies(src_ref, land_ref, send_sems, recv_sems, gather):
    x, y, c = lax.axis_index("x"), lax.axis_index("y"), lax.axis_index("c")
    first = 0 if gather else 1
    copies = []
    for k in range(first, N_DEV):
        px = 1 - x if k & 4 else x
        py = 1 - y if k & 2 else y
        pc = 1 - c if k & 1 else c
        if gather:
            rows = src_ref.shape[0]
            src, dst = src_ref, land_ref.at[pl.ds((4 * x + 2 * y + c) * rows, rows), :]
        else:
            src, dst = src_ref.at[4 * px + 2 * py + pc], land_ref.at[k - 1]
        copies.append(pltpu.make_async_remote_copy(
            src_ref=src, dst_ref=dst, send_sem=send_sems.at[k - first], recv_sem=recv_sems.at[k - first],
            device_id=(px, py, pc), device_id_type=pl.DeviceIdType.MESH))
    return copies


def _exchange_start(src, land_shape, *, gather, name):
    def body(src_ref, land_ref, send_sems, recv_sems, src_thru, land_thru, token):
        for cp in _split_copies(src_ref, land_ref, send_sems, recv_sems, gather):
            cp.start()
        token[...] = jnp.zeros_like(token)

    land = pltpu.with_memory_space_constraint(lax.empty(land_shape, src.dtype), pltpu.HBM)
    n_copies = N_DEV if gather else N_DEV - 1
    return _PALLAS_CALL(
        body, name=name,
        out_shape=(pltpu.SemaphoreType.DMA((n_copies,)), pltpu.SemaphoreType.DMA((n_copies,)),
                   pltpu.HBM(src.shape, src.dtype), pltpu.HBM(land_shape, src.dtype), SDS((8, 128), F32)),
        in_specs=(_HBM, _HBM), out_specs=(_SEM, _SEM, _HBM, _HBM, pl.BlockSpec(memory_space=pltpu.VMEM)),
        input_output_aliases={0: 2, 1: 3}, compiler_params=pltpu.CompilerParams(**_SPLIT_PARAMS),
    )(pltpu.with_memory_space_constraint(src, pltpu.HBM), land)


def _exchange_wait(started, after, *, gather, name):
    send_sems, recv_sems, src_thru, land_thru, _ = started

    def body(src_ref, land_ref, send_sems, recv_sems, after_ref, src_out, land_out):
        copies = _split_copies(src_ref, land_ref, send_sems, recv_sems, gather)
        for cp in copies:
            cp.wait_send()
        for cp in copies:
            cp.wait_recv()

    return _PALLAS_CALL(
        body, name=name,
        out_shape=(pltpu.HBM(src_thru.shape, src_thru.dtype), pltpu.HBM(land_thru.shape, land_thru.dtype)),
        in_specs=(_HBM, _HBM, _SEM, _SEM, pl.BlockSpec(memory_space=pl.ANY)), out_specs=(_HBM, _HBM),
        input_output_aliases={0: 0, 1: 1}, compiler_params=pltpu.CompilerParams(**_SPLIT_PARAMS),
    )(src_thru, land_thru, send_sems, recv_sems, after)


def _sum_parts(own, land, name):
    r, n = own.shape
    tr = _pick(r, (264, 320, 336, 128, 64, 32, 16, 8))

    def body(own_ref, x_ref, o_ref):
        acc = own_ref[...]
        for k in range(N_DEV - 1):
            acc = acc + x_ref[k].astype(F32)
        o_ref[...] = acc

    return _pcall(body, name=name, out_shape=SDS((r, n), F32), grid=(r // tr,),
                  in_specs=[pl.BlockSpec((tr, n), lambda i: (i, 0)), pl.BlockSpec((N_DEV - 1, tr, n), lambda i: (0, i, 0))],
                  out_specs=pl.BlockSpec((tr, n), lambda i: (i, 0)), dims=("parallel",))(own, land)


def _pad_rows(a, rows):
    return jnp.pad(a, ((0, rows - a.shape[0]), (0, 0)))


def kernel(x, mem, positions, ln_in_g, ln_in_b, w_in, attn_sink, g_win, g_dil, w_mix_out, ln1_g, ln1_b, mem_ln_g, mem_ln_b, w_xq, w_xk, w_xv, w_xo, ln2_g, ln2_b, w_gate, w_up, conv_w, conv_b, w_down, ln3_g, ln3_b, loss_target, m_ln_in_g, m_ln_in_b, m_w_in, m_attn_sink, m_g_win, m_g_dil, m_w_mix_out, m_ln1_g, m_ln1_b, m_mem_ln_g, m_mem_ln_b, m_w_xq, m_w_xk, m_w_xv, m_w_xo, m_ln2_g, m_ln2_b, m_w_gate, m_w_up, m_conv_w, m_conv_b, m_w_down, m_ln3_g, m_ln3_b, v_ln_in_g, v_ln_in_b, v_w_in, v_attn_sink, v_g_win, v_g_dil, v_w_mix_out, v_ln1_g, v_ln1_b, v_mem_ln_g, v_mem_ln_b, v_w_xq, v_w_xk, v_w_xv, v_w_xo, v_ln2_g, v_ln2_b, v_w_gate, v_w_up, v_conv_w, v_conv_b, v_w_down, v_ln3_g, v_ln3_b):
    weights = dict(ln_in_g=ln_in_g, ln_in_b=ln_in_b, w_in=w_in, attn_sink=attn_sink, g_win=g_win, g_dil=g_dil, w_mix_out=w_mix_out, ln1_g=ln1_g, ln1_b=ln1_b, mem_ln_g=mem_ln_g, mem_ln_b=mem_ln_b, w_xq=w_xq, w_xk=w_xk, w_xv=w_xv, w_xo=w_xo, ln2_g=ln2_g, ln2_b=ln2_b, w_gate=w_gate, w_up=w_up, conv_w=conv_w, conv_b=conv_b, w_down=w_down, ln3_g=ln3_g, ln3_b=ln3_b)
    mom_m = dict(ln_in_g=m_ln_in_g, ln_in_b=m_ln_in_b, w_in=m_w_in, attn_sink=m_attn_sink, g_win=m_g_win, g_dil=m_g_dil, w_mix_out=m_w_mix_out, ln1_g=m_ln1_g, ln1_b=m_ln1_b, mem_ln_g=m_mem_ln_g, mem_ln_b=m_mem_ln_b, w_xq=m_w_xq, w_xk=m_w_xk, w_xv=m_w_xv, w_xo=m_w_xo, ln2_g=m_ln2_g, ln2_b=m_ln2_b, w_gate=m_w_gate, w_up=m_w_up, conv_w=m_conv_w, conv_b=m_conv_b, w_down=m_w_down, ln3_g=m_ln3_g, ln3_b=m_ln3_b)
    mom_v = dict(ln_in_g=v_ln_in_g, ln_in_b=v_ln_in_b, w_in=v_w_in, attn_sink=v_attn_sink, g_win=v_g_win, g_dil=v_g_dil, w_mix_out=v_w_mix_out, ln1_g=v_ln1_g, ln1_b=v_ln1_b, mem_ln_g=v_mem_ln_g, mem_ln_b=v_mem_ln_b, w_xq=v_w_xq, w_xk=v_w_xk, w_xv=v_w_xv, w_xo=v_w_xo, ln2_g=v_ln2_g, ln2_b=v_ln2_b, w_gate=v_w_gate, w_up=v_w_up, conv_w=v_conv_w, conv_b=v_conv_b, w_down=v_w_down, ln3_g=v_ln3_g, ln3_b=v_ln3_b)
    order = list(weights)
    s = x.shape[1]
    xs = x[0]
    mems = mem[0]
    target = loss_target[0]
    row = lambda a: a.reshape(1, -1)

    shard_rows = dict(w_in=w_in[0].T, w_gate=w_gate[0].T, w_up=w_up[0].T, w_mix_out=w_mix_out[0], w_xq=w_xq[0],
                      w_xk=w_xk[0], w_xv=w_xv[0], w_xo=w_xo[0], w_down=w_down[0])
    me_lin = 4 * lax.axis_index("x") + 2 * lax.axis_index("y") + lax.axis_index("c")
    w_in_full = _all_gather_rows(shard_rows["w_in"].astype(BF16), name="w_in_all_gather", in_vmem=False)
    late_rows = PACK_ROWS[1:]
    late_r = sum(r for _, r in late_rows)
    packed = jnp.concatenate([shard_rows[n].astype(BF16) for n, _ in late_rows], axis=0)
    w_started = _exchange_start(packed, (N_DEV * late_r, D_MODEL), gather=True, name="weight_gather_start")

    tabs = _rope_tables(positions.astype(F32).reshape(s, 1) + w_started[4][0, 0])
    h0, h0b = _ln_fwd(xs, None, row(ln_in_g), row(ln_in_b), 1.0, "ln_in_fwd")
    zw, *zg = _proj_rope(h0b, w_in_full, tabs[0])
    cw_cols = -(-FF_SHARD // 128) * 128
    cw_pad = jnp.pad(conv_w[0], ((0, 5), (0, cw_cols - FF_SHARD)))
    cw_all = _all_gather_rows(cw_pad, name="conv_w_all_gather", in_vmem=True, after=zw).reshape(N_DEV, 8, cw_cols)
    cw_full = jnp.transpose(cw_all[:, :3, :FF_SHARD], (1, 0, 2)).reshape(3, D_FF)
    cw8 = _pad_rows(cw_full, 8)
    oa, lse_a = _banded_fwd(zw, attn_sink, name="win_attn_fwd", **{**_WIN_CFG, "tq": 2 * ATTN_TQ})
    og_views, lg_views = [], []
    for gi in range(3):
        o_g, l_g = _banded_fwd(zg[gi], None, name=f"dil_attn_fwd{gi}", **{**_dil_cfg(gi), "tq": 2 * ATTN_TQ})
        og_views.append(o_g)
        lg_views.append(l_g)
    mixed, ob_views, lb_views = _mix_norm_fwd(oa, og_views, lg_views, g_win, g_dil)
    _, land = _exchange_wait(w_started, mixed, gather=True, name="weight_gather_wait")
    gathered = land.reshape(N_DEV, late_r, D_MODEL)
    full = {}
    off = 0
    for n, r in late_rows:
        full[n] = gathered[:, off:off + r, :].reshape(N_DEV * r, D_MODEL)
        off += r
    mix, h1, h1b = _mm_ln_fwd(mixed, full["w_mix_out"], h0, ln1_g, ln1_b, ALPHA, "mm_mix_out_ln1")
    _, mem_nb = _ln_fwd(mems, None, mem_ln_g, mem_ln_b, 1.0, "mem_ln_fwd")
    kx = _mm(mem_nb, full["w_xk"], trans_b=False, out_dtype=BF16, name="mm_xk")
    vx = _mm(mem_nb, full["w_xv"], trans_b=False, out_dtype=BF16, name="mm_xv")
    qx = _mm(h1b, full["w_xq"], trans_b=False, out_dtype=BF16, name="mm_xq")
    ox, oxb = _xattn_fwd(qx, kx, vx)
    xa, h2, h2b = _mm_ln_fwd(oxb, full["w_xo"], h1, ln2_g, ln2_b, ALPHA, "mm_xo_ln2")
    gate, up, gc, act = _ffn_fwd(h2b, full["w_gate"], full["w_up"], cw8, conv_b)

    du3, du3b, d_ln3_g, d_ln3_b, loss_local = _mm_ln_loss(act, full["w_down"], h2, target, ln3_g, ln3_b, ALPHA,
                                                          "mm_down_ln3_loss")
    dw_down = _mm_tn(act, du3b, name="mm_dw_down")
    dgate, dup, dcw8, d_conv_b = _ffn_bwd(du3b, full["w_down"], gate, gc, up, cw8)
    dw_gate_t = _mm_tn(dgate, h2b, name="mm_dw_gate")
    dw_up_t = _mm_tn(dup, h2b, name="mm_dw_up")
    rows_of = dict(PACK_ROWS)

    own_f32 = {}

    def start_grad_exchange(parts, name, payload=F32):
        gp = jnp.concatenate([g.reshape(N_DEV, rows_of[n], D_MODEL) for n, g in parts], axis=1)
        if payload != F32:
            own_f32[name] = lax.dynamic_index_in_dim(gp, me_lin, axis=0, keepdims=False)
            gp = gp.astype(payload)
        return _exchange_start(gp, (N_DEV - 1,) + gp.shape[1:], gather=False, name=name)

    ffn_parts = (("w_gate", dw_gate_t), ("w_up", dw_up_t), ("w_down", dw_down))
    ffn_started = start_grad_exchange(ffn_parts, "grad_start_ffn")
    du2, du2b, d_ln2_g, d_ln2_b = _mm_ln_bwd(((dgate, full["w_gate"], False), (dup, full["w_up"], False)), du3, ALPHA,
                                             h1, xa, ln2_g + ffn_started[4][0, 0], ALPHA, "mm_dh2_ln2_bwd")
    dox = _mm(du2b, full["w_xo"], trans_b=True, out_dtype=F32, name="mm_d_ox")
    dw_xo = _mm_tn(oxb, du2b, name="mm_dw_xo")
    dqx, dkx, dvx = _xattn_bwd(qx, kx, vx, ox, dox)
    dw_xq = _mm_tn(h1b, dqx, name="mm_dw_xq")
    dw_xk = _mm_tn(mem_nb, dkx, name="mm_dw_xk")
    dw_xv = _mm_tn(mem_nb, dvx, name="mm_dw_xv")
    _, _, d_mem_ln_g, d_mem_ln_b = _mm_ln_bwd(((dkx, full["w_xk"], True), (dvx, full["w_xv"], True)), None, 1.0, mems,
                                              None, mem_ln_g, 1.0, "mm_dmem_ln_bwd")
    du1, du1b, d_ln1_g, d_ln1_b = _mm_ln_bwd(((dqx, full["w_xq"], True),), du2, ALPHA, h0, mix, ln1_g, ALPHA,
                                             "mm_dh1_ln1_bwd")
    dmixed = _mm(du1b, full["w_mix_out"], trans_b=True, out_dtype=F32, name="mm_d_mixed")
    dw_mix_out = _mm_tn(mixed, du1b, name="mm_dw_mix_out")
    attn_parts = (("w_mix_out", dw_mix_out), ("w_xq", dw_xq), ("w_xk", dw_xk), ("w_xv", dw_xv), ("w_xo", dw_xo))
    attn_started = start_grad_exchange(attn_parts, "grad_start_attn")
    doa, dob_views, d_g_win, d_g_dil = _mix_norm_bwd(oa, ob_views[0], dmixed, g_win + attn_started[4][0, 0], g_dil)
    dqa, dkva, dsink8 = _banded_bwd(zw, oa, lse_a, doa, tabs[0], attn_sink, name="win_attn_bwd", **_WIN_CFG)
    dq_views, dkv_views = [], []
    for gi in range(3):
        dq_g, dkv_g = _banded_bwd(zg[gi], ob_views[gi], lb_views[gi], dob_views[gi], tabs[gi], None,
                                  name=f"dil_attn_bwd{gi}", **{**_dil_cfg(gi), "rc": ATTN_TQ})
        dq_views.append(dq_g)
        dkv_views.append(dkv_g)
    dz = _dz_assemble(dq_views, dkv_views, dqa, dkva)
    dw_in_t = _mm_tn(dz, h0b, name="mm_dw_in")
    in_parts = (("w_in", dw_in_t),)
    in_started = start_grad_exchange(in_parts, "grad_start_in", payload=BF16)
    dx, _, d_ln_in_g, d_ln_in_b = _mm_ln_bwd(((dz, w_in_full, False),), du1, ALPHA, xs, None, row(ln_in_g), 1.0,
                                             "mm_dh0_ln_in_bwd", after=in_started[4])

    grads, delta, new_m, new_v = {}, {}, {}, {}
    after = dx
    for parts, started, tag in ((ffn_parts, ffn_started, "ffn"), (attn_parts, attn_started, "attn"),
                                (in_parts, in_started, "in")):
        gp_thru, land = _exchange_wait(started, after, gather=False, name=f"grad_wait_{tag}")
        own = own_f32.get(f"grad_start_{tag}")
        if own is None:
            own = lax.dynamic_index_in_dim(gp_thru, me_lin, axis=0, keepdims=False)
        gsum = _sum_parts(own, land, f"grad_sum_{tag}")
        off = 0
        for n, _ in parts:
            blk = gsum[off:off + rows_of[n]]
            off += rows_of[n]
            grads[n] = (blk.T if n in ("w_in", "w_gate", "w_up") else blk)[None]
            shp = weights[n].shape
            d_, m_, v_ = _adamw(weights[n].reshape(shp[1:]), grads[n].reshape(shp[1:]), mom_m[n].reshape(shp[1:]),
                                mom_v[n].reshape(shp[1:]), f"adamw_{n}")
            delta[n], new_m[n], new_v[n] = d_.reshape(shp), m_.reshape(shp), v_.reshape(shp)
            after = d_

    small = jnp.concatenate([
        d_ln_in_g, d_ln_in_b, d_ln1_g, d_ln1_b, d_mem_ln_g, d_mem_ln_b, d_ln2_g, d_ln2_b, d_ln3_g, d_ln3_b,
        jnp.concatenate([d_g_win, d_g_dil], axis=1),
        jnp.pad(d_conv_b, ((0, 0), (0, FF_PAD - D_FF))).reshape(3, 1024),
        jnp.pad(dsink8[0:1, :], ((0, 0), (0, 1024 - 128))),
        jnp.pad(dcw8[0:3], ((0, 0), (0, FF_PAD - D_FF))).reshape(9, 1024),
    ], axis=0)
    _, ssum = _all_gather_rows(small, name="small_grad_all_reduce", in_vmem=True, sum_rows=True, after=after)
    names10 = ["ln_in_g", "ln_in_b", "ln1_g", "ln1_b", "mem_ln_g", "mem_ln_b", "ln2_g", "ln2_b", "ln3_g", "ln3_b"]
    small_g = {n: ssum[i:i + 1] for i, n in enumerate(names10)}
    small_g["g_win"] = ssum[10:11, :512]
    small_g["g_dil"] = ssum[10:11, 512:]
    small_g["conv_b"] = ssum[11:14].reshape(1, FF_PAD)[:, :D_FF]
    small_g["attn_sink"] = ssum[14:15, :8]
    small_g["conv_w"] = lax.dynamic_slice_in_dim(ssum[15:24].reshape(3, FF_PAD)[:, :D_FF], me_lin * FF_SHARD, FF_SHARD,
                                                 axis=1)

    small_names = [n for n in order if n not in rows_of]
    two_d = lambda a: a.reshape(-1, a.shape[-1])
    d_s, m_s, v_s = _adamw_many([two_d(weights[n]) for n in small_names], [small_g[n] for n in small_names],
                                [two_d(mom_m[n]) for n in small_names], [two_d(mom_v[n]) for n in small_names],
                                "adamw_small")
    for k, n in enumerate(small_names):
        shp = weights[n].shape
        grads[n], delta[n], new_m[n], new_v[n] = (t.reshape(shp) for t in (small_g[n], d_s[k], m_s[k], v_s[k]))

    loss = lax.psum(loss_local[0, 0], MESH_AXES)
    return (loss, dx[None], *[grads[n] for n in order], *[delta[n] for n in order], *[new_m[n] for n in order],
            *[new_v[n] for n in order])
```

```python
import jax
import jax.numpy as jnp
from jax import lax
from jax.experimental import pallas as pl
from jax.experimental.pallas import tpu as pltpu

F32 = jnp.float32
BF16 = jnp.bfloat16
SDS = jax.ShapeDtypeStruct
_PALLAS_CALL = pl.pallas_call

D_MODEL = 1024
HEAD_DIM = 64
WIN_HALF = 128
DIL_PAIRS = ((128, 1), (512, 4), (2048, 16))
DIL_SIDE = 64
ROT_DIM = 16
ROPE_THETA = 500000.0
MEM_LEN = 256
X_HEADS = 4
X_HEAD_DIM = 256
D_FF = 2816
IN_WIDTH = 5376
Z_VB, Z_QA, Z_VA = 3072, 4608, 5248
W_IN_QA, W_IN_KA, W_IN_QB = 0, 512, 768
ALPHA = (2.0) ** 0.25
LN_EPS = 1e-5
NEG_INF = -1e30
ADAM_LR, ADAM_B1, ADAM_B2, ADAM_EPS, ADAM_WD, ADAM_STEP = 0.001, 0.9, 0.999, 1e-08, 0.01, 10
N_DEV = 8
MESH_AXES = ("x", "y", "c")
VMEM_LIMIT_BYTES = 52 * 1024 * 1024
ATTN_TQ = 256
WHOLE_CLASS_BYTES = 4 * 1024 * 1024
TABW = 384

PACK_ROWS = (("w_in", 672), ("w_gate", 352), ("w_up", 352), ("w_mix_out", 128), ("w_xq", 128), ("w_xk", 128),
             ("w_xv", 128), ("w_xo", 128), ("w_down", 352))
FF_SHARD = D_FF // N_DEV
FF_PAD = 3 * 1024


def _pick(n, cands):
    for c in cands:
        if n % c == 0:
            return c
    return n


def _pcall(body, *, name, out_shape, grid=None, in_specs=None, out_specs=None, scratch_shapes=(), dims=None,
           aliases=None):
    kw = {}
    if grid is not None:
        kw["grid"] = grid
    if in_specs is not None:
        kw["in_specs"] = in_specs
    if out_specs is not None:
        kw["out_specs"] = out_specs
    if aliases:
        kw["input_output_aliases"] = aliases
    return _PALLAS_CALL(
        body, name=name, out_shape=out_shape, scratch_shapes=list(scratch_shapes),
        compiler_params=pltpu.CompilerParams(dimension_semantics=dims, vmem_limit_bytes=VMEM_LIMIT_BYTES), **kw)


MM_VMEM_BUDGET = 40 * 1024 * 1024


def _mm(a, b, *, trans_b, out_dtype, name):
    m, k = a.shape
    n = b.shape[0] if trans_b else b.shape[1]
    out_bytes = jnp.dtype(out_dtype).itemsize

    def vmem(tm, tn):
        return 2 * (tm * tn * out_bytes + tm * k * a.dtype.itemsize + k * tn * b.dtype.itemsize)

    tm, tn = next(((cm, cn) for cn in (n, 1408, 1024, 512, 256, 128) if n % cn == 0
                   for cm in (1024, 512, 256, 128) if m % cm == 0 and vmem(cm, cn) <= MM_VMEM_BUDGET))

    def body(a_ref, b_ref, o_ref):
        o_ref[...] = lax.dot_general(a_ref[...], b_ref[...], _NT if trans_b else _NN,
                                     preferred_element_type=F32).astype(out_dtype)

    in_specs = [pl.BlockSpec((tm, k), lambda j, i: (i, 0)),
                pl.BlockSpec((tn, k), lambda j, i: (j, 0)) if trans_b else pl.BlockSpec((k, tn), lambda j, i: (0, j))]
    return _pcall(body, name=name, out_shape=SDS((m, n), out_dtype), grid=(n // tn, m // tm), in_specs=in_specs,
                  out_specs=pl.BlockSpec((tm, tn), lambda j, i: (i, j)), dims=("parallel", "parallel"))(a, b)


def _mm_tn(a, b, *, name):
    s, m = a.shape
    n = b.shape[1]
    tm = _pick(m, (768, 1408, 1024, 512, 256, 128))
    tk = _pick(s, (1024, 512, 256))
    nk = s // tk

    def body(a_ref, b_ref, o_ref, acc_ref):
        kk = pl.program_id(1)

        @pl.when(kk == 0)
        def _():
            acc_ref[...] = jnp.zeros_like(acc_ref)

        acc_ref[...] += lax.dot_general(a_ref[...].astype(BF16), b_ref[...].astype(BF16), (((0,), (0,)), ((), ())),
                                        preferred_element_type=F32)

        @pl.when(kk == nk - 1)
        def _():
            o_ref[...] = acc_ref[...]

    return _pcall(body, name=name, out_shape=SDS((m, n), F32), grid=(m // tm, nk),
                  in_specs=[pl.BlockSpec((tk, tm), lambda i, kk: (kk, i)), pl.BlockSpec((tk, n), lambda i, kk: (kk, 0))],
                  out_specs=pl.BlockSpec((tm, n), lambda i, kk: (i, 0)), scratch_shapes=[pltpu.VMEM((tm, n), F32)],
                  dims=("parallel", "arbitrary"))(a, b)


def _rope_lane_consts():
    lane = jnp.arange(128)
    j = lane % HEAD_DIM
    inv_freq = ROPE_THETA ** (-jnp.arange(0, ROT_DIM, 2, dtype=F32) / ROT_DIM)
    freq = jnp.where(j < ROT_DIM, inv_freq[j % (ROT_DIM // 2)], 0.0).astype(F32)
    lo = (j < ROT_DIM // 2).astype(F32)
    hi = ((j >= ROT_DIM // 2) & (j < ROT_DIM)).astype(F32)
    return jnp.stack([freq, lo, hi] + [jnp.zeros((128,), F32)] * 5)


def _to_classes(x, scr, d):
    if d == 1:
        return [x]
    scr[...] = x
    return [scr[pl.ds(c, x.shape[0] // d, stride=d), :] for c in range(d)]


def _from_classes(parts, scr):
    d = len(parts)
    if d == 1:
        return parts[0]
    for c, part in enumerate(parts):
        scr[pl.ds(c, part.shape[0], stride=d), :] = part
    return scr[...]


DILATIONS = tuple(d for _, d in DIL_PAIRS)


def _rope_tables(posf):
    s = posf.shape[0]
    tm = _pick(s, (1024, 512))

    def body(p_ref, c_ref, *rest):
        o_refs, scr = rest[:-1], rest[-1]
        ang = p_ref[...] * c_ref[0:1, :]
        lo = c_ref[1:2, :]
        hi = c_ref[2:3, :]
        cs = jnp.cos(ang)
        sn = jnp.sin(ang)
        for q, t in enumerate((jnp.where(lo + hi > 0.0, cs, 1.0), -sn * lo, sn * hi)):
            for o_ref, d in zip(o_refs, DILATIONS):
                for c, part in enumerate(_to_classes(t, scr, d)):
                    o_ref[:, c * TABW + q * 128:c * TABW + (q + 1) * 128] = part

    return _pcall(body, name="rope_tables", out_shape=tuple(SDS((s // d, d * TABW), F32) for d in DILATIONS),
                  grid=(s // tm,),
                  in_specs=[pl.BlockSpec((tm, 1), lambda i: (i, 0)), pl.BlockSpec((8, 128), lambda i: (0, 0))],
                  out_specs=tuple(pl.BlockSpec((tm // d, d * TABW), lambda i: (i, 0)) for d in DILATIONS),
                  scratch_shapes=[pltpu.VMEM((tm, 128), F32)], dims=("parallel",))(posf, _rope_lane_consts())


def _rope_apply(x, tab, sign):
    w = x.shape[1]
    rep = w // 128
    c = jnp.tile(tab[:, 0:128], (1, rep)) if rep > 1 else tab[:, 0:128]
    a = jnp.tile(tab[:, 128:256], (1, rep)) if rep > 1 else tab[:, 128:256]
    b = jnp.tile(tab[:, 256:384], (1, rep)) if rep > 1 else tab[:, 256:384]
    up = pltpu.roll(x, w - 8, 1)
    dn = pltpu.roll(x, 8, 1)
    if sign > 0:
        return x * c + up * a + dn * b
    return x * c - up * a - dn * b


def _proj_rope(h0b, w_t, tab):
    s = h0b.shape[0]
    tm = _pick(s, (512,))
    tn = 256

    def body(a_ref, w_ref, t_ref, zw_ref, z0_ref, z1_ref, z2_ref, scr):
        z_refs = (z0_ref, z1_ref, z2_ref)
        a = a_ref[...]
        tabv = t_ref[...]
        for c0 in range(0, IN_WIDTH, tn):
            w0 = (c0 + W_IN_QB) % IN_WIDTH
            z = lax.dot_general(a, w_ref[w0:w0 + tn, :], _NT, preferred_element_type=F32)
            for g0 in range(c0, c0 + tn, 128):
                zg = z[:, g0 - c0:g0 - c0 + 128]
                if g0 < Z_VB or Z_QA <= g0 < Z_VA:
                    zg = _rope_apply(zg, tabv, 1)
                if g0 >= Z_QA:
                    zw_ref[:, g0 - Z_QA:g0 - Z_QA + 128] = zg.astype(BF16)
                    continue
                kind, within = divmod(g0, 1536)
                grp, off = divmod(within, 512)
                col = kind * 512 + off
                for c, part in enumerate(_to_classes(zg, scr, DILATIONS[grp])):
                    z_refs[grp][:, c * 1536 + col:c * 1536 + col + 128] = part.astype(BF16)

    return _pcall(body, name="proj_rope",
                  out_shape=(SDS((s, 768), BF16),) + tuple(SDS((s // d, d * 1536), BF16) for d in DILATIONS),
                  grid=(s // tm,),
                  in_specs=[pl.BlockSpec((tm, D_MODEL), lambda i: (i, 0)), pl.BlockSpec((IN_WIDTH, D_MODEL), lambda i: (0, 0)),
                            pl.BlockSpec((tm, TABW), lambda i: (i, 0))],
                  out_specs=(pl.BlockSpec((tm, 768), lambda i: (i, 0)),)
                  + tuple(pl.BlockSpec((tm // d, d * 1536), lambda i: (i, 0)) for d in DILATIONS),
                  scratch_shapes=[pltpu.VMEM((tm, 128), F32)], dims=("parallel",))(h0b, w_t, tab)


def _band_specs(sd, blk, tq, width, per_tok, cb):
    r = tq // blk
    nbk = sd // blk
    prev = pl.BlockSpec((blk, width), lambda c, j: (jnp.maximum(j * r - 1, 0), c * per_tok + cb))
    cur = pl.BlockSpec((tq, width), lambda c, j: (j, c * per_tok + cb))
    nxt = pl.BlockSpec((blk, width), lambda c, j: (jnp.minimum((j + 1) * r, nbk - 1), c * per_tok + cb))
    return [prev, cur, nxt]


def _band_bias(q0, rows, blk, sd):
    shape = (rows, rows + 2 * blk)
    qpos = q0 + lax.broadcasted_iota(jnp.int32, shape, 0)
    kpos = q0 - blk + lax.broadcasted_iota(jnp.int32, shape, 1)
    ok = (jnp.abs(qpos - kpos) <= blk) & (kpos >= 0) & (kpos < sd)
    return jnp.where(ok, 0.0, NEG_INF)


_NT = (((1,), (1,)), ((), ()))
_NN = (((1,), (0,)), ((), ()))
_TN = (((0,), (0,)), ((), ()))


def _banded_fwd(zv, sink, *, d, blk, tq, rc, ptw, qw, kw, qcb, kcb, vcb, pairs, name):
    sd = zv.shape[0]
    tq = min(tq, sd)
    rc = min(rc, tq)
    has_sink = sink is not None
    scale = HEAD_DIM ** -0.5

    def body(q_ref, kp, kc, kn, vp, vc, vn, *rest):
        if has_sink:
            sink_ref, o_ref, lse_ref = rest
        else:
            o_ref, lse_ref = rest
        j = pl.program_id(1)
        q = q_ref[...] * scale
        k = jnp.concatenate([kp[...], kc[...], kn[...]], axis=0)
        v = jnp.concatenate([vp[...], vc[...], vn[...]], axis=0)
        biases = {r0: _band_bias(j * tq + r0, rc, blk, sd) for r0 in range(0, tq, rc)}
        low = lax.broadcasted_iota(jnp.int32, (1, 128), 1) < HEAD_DIM
        for qb, kb, vb, swaps, sinks in pairs:
            qp, kp_, vp_ = q[:, qb:qb + 128], k[:, kb:kb + 128], v[:, vb:vb + 128]
            if any(swaps):
                k_sw = jnp.concatenate([kp_[:, HEAD_DIM:], kp_[:, :HEAD_DIM]], axis=1)
                v_sw = jnp.concatenate([vp_[:, HEAD_DIM:], vp_[:, :HEAD_DIM]], axis=1)
            for r0 in range(0, tq, rc):
                outs, lses = [], []
                for half in range(2):
                    qm = jnp.where(low if half == 0 else ~low, qp[r0:r0 + rc], jnp.zeros((rc, 128), BF16))
                    kk, vv = (k_sw, v_sw) if swaps[half] else (kp_, vp_)
                    kk, vv = kk[r0:r0 + rc + 2 * blk], vv[r0:r0 + rc + 2 * blk]
                    sc = lax.dot_general(qm, kk, _NT, preferred_element_type=F32) + biases[r0]
                    m = jnp.max(sc, axis=-1, keepdims=True)
                    if has_sink:
                        m = jnp.maximum(m, sink_ref[0, sinks[half]])
                    p = jnp.exp(sc - m)
                    den = jnp.sum(p, axis=-1, keepdims=True)
                    if has_sink:
                        den = den + jnp.exp(sink_ref[0, sinks[half]] - m)
                    outs.append(lax.dot_general(p.astype(BF16), vv, _NN, preferred_element_type=F32) / den)
                    lses.append(m + jnp.log(den))
                o_ref[r0:r0 + rc, qb:qb + 128] = jnp.where(low, outs[0], outs[1])
                lse_ref[r0:r0 + rc, qb:qb + 128] = jnp.where(low, lses[0], lses[1])

    in_specs = ([pl.BlockSpec((tq, qw), lambda c, j: (j, c * (ptw // qw) + qcb))]
                + _band_specs(sd, blk, tq, kw, ptw // kw, kcb) + _band_specs(sd, blk, tq, kw, ptw // kw, vcb))
    args = [zv] * 7
    if has_sink:
        in_specs.append(pl.BlockSpec(memory_space=pltpu.SMEM))
        args.append(sink)
    o_spec = pl.BlockSpec((tq, qw), lambda c, j: (j, c))
    return _pcall(body, name=name, out_shape=(SDS((sd, d * qw), F32), SDS((sd, d * qw), F32)), grid=(d, sd // tq),
                  in_specs=in_specs, out_specs=(o_spec, o_spec), dims=("parallel", "parallel"))(*args)


def _banded_bwd(zv, ov, lv, dov, tv, sink, *, d, blk, tq, rc, ptw, qw, kw, qcb, kcb, vcb, pairs, name):
    sd = zv.shape[0]
    tq = min(tq, sd)
    nt = sd // tq
    r = tq // blk
    nbk = sd // blk
    has_sink = sink is not None
    scale = HEAD_DIM ** -0.5
    rc = min(rc, tq)
    kvw = 128 * len({kb for _, kb, _, _, _ in pairs})
    whole_class = sd * 2 * kvw * 2 <= WHOLE_CLASS_BYTES

    def add_rows(x, y, last):
        if tq == blk:
            return x + y
        if last:
            return jnp.concatenate([x[:tq - blk], x[tq - blk:] + y], axis=0)
        return jnp.concatenate([x[:blk] + y, x[blk:]], axis=0)

    def body(q_ref, kp, kc, kn, vp, vc, vn, o_ref, l_ref, do_ref, t_ref, tlag_ref, *rest):
        if has_sink:
            sink_ref, dq_ref, dkv_ref, dsink_ref, acck, accv, nxtk, nxtv = rest
        else:
            dq_ref, dkv_ref, acck, accv, nxtk, nxtv = rest
        j = pl.program_id(1)

        @pl.when(j == 0)
        def _():
            nxtk[...] = jnp.zeros_like(nxtk)
            nxtv[...] = jnp.zeros_like(nxtv)

        if has_sink:
            @pl.when((pl.program_id(0) == 0) & (j == 0))
            def _():
                dsink_ref[...] = jnp.zeros_like(dsink_ref)

        def emit(dk_rows, dv_rows, tabv, tile_idx):
            val = jnp.concatenate([_rope_apply(dk_rows, tabv, -1), dv_rows], axis=1).astype(BF16)
            if whole_class:
                dkv_ref[pl.ds(pl.multiple_of(tile_idx * tq, tq), tq), :] = val
            else:
                dkv_ref[...] = val

        @pl.when(j < nt)
        def _():
            q = q_ref[...] * scale
            k3 = jnp.concatenate([kp[...], kc[...], kn[...]], axis=0)
            v3 = jnp.concatenate([vp[...], vc[...], vn[...]], axis=0)
            o_t, l_t, do_t = o_ref[...], l_ref[...], do_ref[...]
            biases = {r0: _band_bias(j * tq + r0, rc, blk, sd) for r0 in range(0, tq, rc)}
            lane = lax.broadcasted_iota(jnp.int32, (1, 128), 1)
            low = lane < HEAD_DIM
            wide = tq + 2 * blk
            cw = rc + 2 * blk

            def place(x, r0):
                parts = ([jnp.zeros((r0, 128), F32)] if r0 else []) + [x]
                if wide - r0 - cw:
                    parts.append(jnp.zeros((wide - r0 - cw, 128), F32))
                return jnp.concatenate(parts, axis=0) if len(parts) > 1 else x

            dqs = []
            wks, wvs = {}, {}
            dsink_row = jnp.zeros((1, 128), F32)
            for qb, kb, vb, swaps, sinks in pairs:
                qp, kp_, vp_ = q[:, qb:qb + 128], k3[:, kb:kb + 128], v3[:, vb:vb + 128]
                if any(swaps):
                    k_sw = jnp.concatenate([kp_[:, HEAD_DIM:], kp_[:, :HEAD_DIM]], axis=1)
                    v_sw = jnp.concatenate([vp_[:, HEAD_DIM:], vp_[:, :HEAD_DIM]], axis=1)
                dop, lp = do_t[:, qb:qb + 128], l_t[:, qb:qb + 128]
                prod = dop * o_t[:, qb:qb + 128]
                dq_rows = []
                for r0 in range(0, tq, rc):
                    rows = slice(r0, r0 + rc)
                    dq_half = []
                    for half in range(2):
                        mine = low if half == 0 else ~low
                        qm = jnp.where(mine, qp[rows], jnp.zeros((rc, 128), BF16))
                        dob = jnp.where(mine, dop[rows], 0.0).astype(BF16)
                        delta = jnp.sum(jnp.where(mine, prod[rows], 0.0), axis=-1, keepdims=True)
                        lse = lp[rows, half * HEAD_DIM:half * HEAD_DIM + 1]
                        kk, vv = (k_sw, v_sw) if swaps[half] else (kp_, vp_)
                        kk, vv = kk[r0:r0 + cw], vv[r0:r0 + cw]
                        sc = lax.dot_general(qm, kk, _NT, preferred_element_type=F32) + biases[r0]
                        p = jnp.exp(sc - lse)
                        dp = lax.dot_general(dob, vv, _NT, preferred_element_type=F32)
                        dsb = (p * (dp - delta)).astype(BF16)
                        dq_half.append(lax.dot_general(dsb, kk, _NN, preferred_element_type=F32))
                        dk = lax.dot_general(dsb, qm, _TN, preferred_element_type=F32)
                        dv = lax.dot_general(p.astype(BF16), dob, _TN, preferred_element_type=F32)
                        if swaps[half]:
                            dk, dv = pltpu.roll(dk, HEAD_DIM, 1), pltpu.roll(dv, HEAD_DIM, 1)
                        wks[kb] = place(dk, r0) if kb not in wks else wks[kb] + place(dk, r0)
                        wvs[vb] = place(dv, r0) if vb not in wvs else wvs[vb] + place(dv, r0)
                        if has_sink:
                            psink = jnp.exp(sink_ref[0, sinks[half]] - lse)
                            dsink_row = dsink_row + jnp.where(lane == sinks[half], -jnp.sum(psink * delta), 0.0)
                    dq_rows.append(jnp.where(low, dq_half[0], dq_half[1]) * scale)
                dqs.append(jnp.concatenate(dq_rows, axis=0) if len(dq_rows) > 1 else dq_rows[0])
            dq_ref[...] = _rope_apply(jnp.concatenate(dqs, axis=1), t_ref[...], -1).astype(BF16)
            wk = jnp.concatenate([wks[b] for b in sorted(wks)], axis=1) if len(wks) > 1 else wks[min(wks)]
            wv = jnp.concatenate([wvs[b] for b in sorted(wvs)], axis=1) if len(wvs) > 1 else wvs[min(wvs)]
            if has_sink:
                dsink_ref[0:1, :] += dsink_row

            @pl.when(j > 0)
            def _():
                emit(add_rows(acck[...], wk[:blk], True), add_rows(accv[...], wv[:blk], True), tlag_ref[...], j - 1)

            acck[...] = add_rows(wk[blk:blk + tq], nxtk[...], False)
            accv[...] = add_rows(wv[blk:blk + tq], nxtv[...], False)
            nxtk[...] = wk[blk + tq:]
            nxtv[...] = wv[blk + tq:]
            if whole_class:
                @pl.when(j == nt - 1)
                def _():
                    emit(acck[...], accv[...], t_ref[...], j)

        if not whole_class:
            @pl.when(j == nt)
            def _():
                emit(acck[...], accv[...], tlag_ref[...], j - 1)

    def tile(width, per_tok, cb):
        return pl.BlockSpec((tq, width), lambda c, j: (jnp.minimum(j, nt - 1), c * per_tok + cb))

    def halos(width, per_tok, cb):
        before = pl.BlockSpec((blk, width), lambda c, j: (jnp.maximum(jnp.minimum(j, nt - 1) * r - 1, 0), c * per_tok + cb))
        after = pl.BlockSpec((blk, width),
                             lambda c, j: (jnp.minimum((jnp.minimum(j, nt - 1) + 1) * r, nbk - 1), c * per_tok + cb))
        return [before, tile(width, per_tok, cb), after]

    def lagged(width):
        return pl.BlockSpec((tq, width), lambda c, j: (jnp.maximum(j - 1, 0), c))

    in_specs = ([tile(qw, ptw // qw, qcb)] + halos(kw, ptw // kw, kcb) + halos(kw, ptw // kw, vcb)
                + [tile(qw, 1, 0)] * 3 + [tile(TABW, 1, 0), lagged(TABW)])
    args = [zv] * 7 + [ov, lv, dov, tv, tv]
    out_shape = [SDS((sd, d * qw), BF16), SDS((sd, d * 2 * kvw), BF16)]
    out_specs = [tile(qw, 1, 0),
                 pl.BlockSpec((sd, 2 * kvw), lambda c, j: (0, c)) if whole_class else lagged(2 * kvw)]
    if has_sink:
        in_specs.append(pl.BlockSpec(memory_space=pltpu.SMEM))
        args.append(sink)
        out_shape.append(SDS((8, 128), F32))
        out_specs.append(pl.BlockSpec((8, 128), lambda c, j: (0, 0)))
    scratch = [pltpu.VMEM((tq, kvw), F32), pltpu.VMEM((tq, kvw), F32), pltpu.VMEM((blk, kvw), F32),
               pltpu.VMEM((blk, kvw), F32)]
    return _pcall(body, name=name, out_shape=tuple(out_shape), grid=(d, nt + (0 if whole_class else 1)), in_specs=in_specs,
                  out_specs=tuple(out_specs), scratch_shapes=scratch, dims=("arbitrary", "arbitrary"))(*args)


_WIN_PAIRS = tuple((128 * p, 0, 128, (False, True) if p < 2 else (True, False), (2 * p, 2 * p + 1)) for p in range(4))
_WIN_CFG = dict(d=1, blk=WIN_HALF, tq=ATTN_TQ, rc=256, ptw=768, qw=512, kw=256, qcb=0, kcb=2, vcb=2, pairs=_WIN_PAIRS)
_DIL_PAIRS = tuple((128 * p, 128 * p, 128 * p, (False, False), (2 * p, 2 * p + 1)) for p in range(4))


def _dil_cfg(gi):
    return dict(d=DILATIONS[gi], blk=DIL_SIDE, tq=ATTN_TQ, rc=128, ptw=1536, qw=512, kw=512, qcb=0, kcb=1, vcb=2,
                pairs=_DIL_PAIRS)


def _view_specs(tm, width):
    return tuple(pl.BlockSpec((tm // d, d * width), lambda i: (i, 0)) for d in DILATIONS)


def _mix_norm_fwd(oa, og_views, lg_views, g_win, g_dil):
    s = oa.shape[0]
    tm = _pick(s, (512,))

    def body(oa_ref, o0, o1, o2, l0, l1, l2, gw_ref, gd_ref, mixed_ref, ob0, ob1, ob2, lb0, lb1, lb2, scr, ob_s):
        o_refs, l_refs, ob_refs, lb_refs = (o0, o1, o2), (l0, l1, l2), (ob0, ob1, ob2), (lb0, lb1, lb2)
        ssq = jnp.zeros((tm, 1), F32)
        for q in range(4):
            os_, ls_ = [], []
            for g, d in enumerate(DILATIONS):
                cols = [slice(c * 512 + q * 128, c * 512 + (q + 1) * 128) for c in range(d)]
                os_.append(_from_classes([o_refs[g][:, cs] for cs in cols], scr))
                ls_.append(_from_classes([l_refs[g][:, cs] for cs in cols], scr))
            mx = jnp.maximum(jnp.maximum(ls_[0], ls_[1]), ls_[2])
            es = [jnp.exp(l - mx) for l in ls_]
            den = es[0] + es[1] + es[2]
            ob = (es[0] / den) * os_[0] + (es[1] / den) * os_[1] + (es[2] / den) * os_[2]
            lb = mx + jnp.log(den)
            ob_s[:, q * 128:(q + 1) * 128] = ob
            ssq = ssq + jnp.sum(ob * ob, axis=-1, keepdims=True)
            for g, d in enumerate(DILATIONS):
                for val, refs in ((ob, ob_refs), (lb, lb_refs)):
                    for c, part in enumerate(_to_classes(val, scr, d)):
                        refs[g][:, c * 512 + q * 128:c * 512 + (q + 1) * 128] = part
        a = oa_ref[...]
        ra = lax.rsqrt(jnp.mean(a * a, axis=-1, keepdims=True) + LN_EPS)
        rb = lax.rsqrt(ssq * (1.0 / 512) + LN_EPS)
        mixed_ref[...] = jnp.concatenate([a * ra * gw_ref[...], ob_s[...] * rb * gd_ref[...]], axis=1).astype(BF16)

    row = pl.BlockSpec((tm, 512), lambda i: (i, 0))
    vec = pl.BlockSpec((1, 512), lambda i: (0, 0))
    views = _view_specs(tm, 512)
    view_shapes = tuple(SDS((s // d, d * 512), F32) for d in DILATIONS)
    res = _pcall(body, name="mix_norm_fwd", out_shape=(SDS((s, 1024), BF16),) + view_shapes * 2, grid=(s // tm,),
                 in_specs=[row, *views, *views, vec, vec],
                 out_specs=(pl.BlockSpec((tm, 1024), lambda i: (i, 0)),) + views * 2,
                 scratch_shapes=[pltpu.VMEM((tm, 128), F32), pltpu.VMEM((tm, 512), F32)],
                 dims=("parallel",))(oa, *og_views, *lg_views, g_win, g_dil)
    return res[0], res[1:4], res[4:7]


def _mix_norm_bwd(oa, ob, dmixed, g_win, g_dil):
    s = oa.shape[0]
    tm = _pick(s, (512,))
    nt = s // tm

    def body(oa_ref, ob_ref, dm_ref, gw_ref, gd_ref, doa_ref, db0, db1, db2, dgw_ref, dgd_ref, acc_w, acc_d, scr):
        i = pl.program_id(0)

        @pl.when(i == 0)
        def _():
            acc_w[...] = jnp.zeros_like(acc_w)
            acc_d[...] = jnp.zeros_like(acc_d)

        dm = dm_ref[...]
        dxs = []
        for x_ref, g_ref, dy, acc in ((oa_ref, gw_ref, dm[:, :512], acc_w), (ob_ref, gd_ref, dm[:, 512:], acc_d)):
            x = x_ref[...]
            r = lax.rsqrt(jnp.mean(x * x, axis=-1, keepdims=True) + LN_EPS)
            dyg = dy * g_ref[...]
            dxs.append(r * dyg - x * (r * r * r) * jnp.mean(dyg * x, axis=-1, keepdims=True))
            acc[...] += jnp.sum((dy * x * r).reshape(tm // 8, 8, 512), axis=0)
        doa_ref[...] = dxs[0]
        for q in range(4):
            dq = dxs[1][:, q * 128:(q + 1) * 128]
            for db_ref, d in zip((db0, db1, db2), DILATIONS):
                for c, part in enumerate(_to_classes(dq, scr, d)):
                    db_ref[:, c * 512 + q * 128:c * 512 + (q + 1) * 128] = part

        @pl.when(i == nt - 1)
        def _():
            dgw_ref[...] = jnp.sum(acc_w[...], axis=0, keepdims=True)
            dgd_ref[...] = jnp.sum(acc_d[...], axis=0, keepdims=True)

    row = pl.BlockSpec((tm, 512), lambda i: (i, 0))
    vec = pl.BlockSpec((1, 512), lambda i: (0, 0))
    views = _view_specs(tm, 512)
    view_shapes = tuple(SDS((s // d, d * 512), F32) for d in DILATIONS)
    res = _pcall(body, name="mix_norm_bwd",
                 out_shape=(SDS((s, 512), F32),) + view_shapes + (SDS((1, 512), F32), SDS((1, 512), F32)),
                 grid=(nt,), in_specs=[row, row, pl.BlockSpec((tm, 1024), lambda i: (i, 0)), vec, vec],
                 out_specs=(row,) + views + (vec, vec),
                 scratch_shapes=[pltpu.VMEM((8, 512), F32), pltpu.VMEM((8, 512), F32), pltpu.VMEM((tm, 128), F32)],
                 dims=("arbitrary",))(oa, ob, dmixed, g_win, g_dil)
    return res[0], res[1:4], res[4], res[5]


def _dz_assemble(dq_views, dkv_views, dqa, dkva):
    s = dqa.shape[0]
    tm = _pick(s, (512,))

    def body(q0, q1, q2, kv0, kv1, kv2, qa_ref, kva_ref, o_ref, scr):
        for g, d in enumerate(DILATIONS):
            for kind, (ref, width, base) in enumerate((((q0, q1, q2)[g], 512, 0), ((kv0, kv1, kv2)[g], 1024, 0),
                                                       ((kv0, kv1, kv2)[g], 1024, 512))):
                for q in range(4):
                    src = base + q * 128
                    dst = W_IN_QB + kind * 1536 + g * 512 + q * 128
                    if d == 1:
                        o_ref[:, dst:dst + 128] = ref[:, src:src + 128]
                    else:
                        parts = [ref[:, c * width + src:c * width + src + 128].astype(F32) for c in range(d)]
                        o_ref[:, dst:dst + 128] = _from_classes(parts, scr).astype(BF16)
        o_ref[:, W_IN_QA:W_IN_QA + 512] = qa_ref[...]
        o_ref[:, W_IN_KA:W_IN_KA + 256] = kva_ref[...]

    return _pcall(body, name="dz_assemble", out_shape=SDS((s, IN_WIDTH), BF16), grid=(s // tm,),
                  in_specs=[*_view_specs(tm, 512), *_view_specs(tm, 1024), pl.BlockSpec((tm, 512), lambda i: (i, 0)),
                            pl.BlockSpec((tm, 256), lambda i: (i, 0))],
                  out_specs=pl.BlockSpec((tm, IN_WIDTH), lambda i: (i, 0)),
                  scratch_shapes=[pltpu.VMEM((tm, 128), F32)], dims=("parallel",))(*dq_views, *dkv_views, dqa, dkva)


def _ln_fwd(a, r, g, b, ca, name):
    s = a.shape[0]
    tm = _pick(s, (512, 256))
    has_r = r is not None

    def body(*refs):
        a_ref = refs[0]
        r_ref = refs[1] if has_r else None
        g_ref, b_ref, o_ref, ob_ref = refs[1 + has_r:]
        u = a_ref[...] if ca == 1.0 else ca * a_ref[...]
        if has_r:
            u = u + r_ref[...]
        mu = jnp.mean(u, axis=-1, keepdims=True)
        xc = u - mu
        var = jnp.mean(xc * xc, axis=-1, keepdims=True)
        y = xc * lax.rsqrt(var + LN_EPS) * g_ref[...] + b_ref[...]
        o_ref[...] = y
        ob_ref[...] = y.astype(BF16)

    row = pl.BlockSpec((tm, D_MODEL), lambda i: (i, 0))
    vec = pl.BlockSpec((1, D_MODEL), lambda i: (0, 0))
    args = [a] + ([r] if has_r else []) + [g, b]
    return _pcall(body, name=name, out_shape=(SDS((s, D_MODEL), F32), SDS((s, D_MODEL), BF16)), grid=(s // tm,),
                  in_specs=[row] * (1 + has_r) + [vec, vec], out_specs=(row, row), dims=("parallel",))(*args)


def _mm_ln_fwd(a, w, resid, g, b, ca, name):
    s, k = a.shape
    tm = _pick(s, (512, 256))

    def body(a_ref, w_ref, res_ref, g_ref, b_ref, r_ref, o_ref, ob_ref):
        rv = lax.dot_general(a_ref[...], w_ref[...], _NN, preferred_element_type=F32)
        r_ref[...] = rv
        u = ca * res_ref[...] + rv
        mu = jnp.mean(u, axis=-1, keepdims=True)
        xc = u - mu
        var = jnp.mean(xc * xc, axis=-1, keepdims=True)
        y = xc * lax.rsqrt(var + LN_EPS) * g_ref[...] + b_ref[...]
        o_ref[...] = y
        ob_ref[...] = y.astype(BF16)

    row = pl.BlockSpec((tm, D_MODEL), lambda i: (i, 0))
    vec = pl.BlockSpec((1, D_MODEL), lambda i: (0, 0))
    return _pcall(body, name=name, out_shape=(SDS((s, D_MODEL), F32), SDS((s, D_MODEL), F32), SDS((s, D_MODEL), BF16)),
                  grid=(s // tm,),
                  in_specs=[pl.BlockSpec((tm, k), lambda i: (i, 0)), pl.BlockSpec((k, D_MODEL), lambda i: (0, 0)), row, vec, vec],
                  out_specs=(row, row, row), dims=("parallel",))(a, w, resid, g, b)


def _mm_ln_loss(x, w, a, target, g, b, ca, name):
    s, k = x.shape
    tm = _pick(s, (256,))
    nt = s // tm

    def body(x_ref, w_ref, a_ref, t_ref, g_ref, b_ref, du_ref, dub_ref, dg_ref, db_ref, loss_ref, acc_g, acc_b, acc_l):
        i = pl.program_id(0)

        @pl.when(i == 0)
        def _():
            acc_g[...] = jnp.zeros_like(acc_g)
            acc_b[...] = jnp.zeros_like(acc_b)
            acc_l[...] = jnp.zeros_like(acc_l)

        u = ca * a_ref[...] + lax.dot_general(x_ref[...], w_ref[...], _NN, preferred_element_type=F32)
        mu = jnp.mean(u, axis=-1, keepdims=True)
        xc = u - mu
        var = jnp.mean(xc * xc, axis=-1, keepdims=True)
        rstd = lax.rsqrt(var + LN_EPS)
        xhat = xc * rstd
        gv = g_ref[...]
        err = (xhat * gv + b_ref[...]) - t_ref[...]
        acc_l[...] += jnp.sum((err * err).reshape(tm // 8, 8, D_MODEL), axis=0)
        dyv = err * (1.0 / D_MODEL)
        dxh = dyv * gv
        du = rstd * (dxh - jnp.mean(dxh, axis=-1, keepdims=True) - xhat * jnp.mean(dxh * xhat, axis=-1, keepdims=True))
        du_ref[...] = du
        dub_ref[...] = du.astype(BF16)
        acc_g[...] += jnp.sum((dyv * xhat).reshape(tm // 8, 8, D_MODEL), axis=0)
        acc_b[...] += jnp.sum(dyv.reshape(tm // 8, 8, D_MODEL), axis=0)

        @pl.when(i == nt - 1)
        def _():
            dg_ref[...] = jnp.sum(acc_g[...], axis=0, keepdims=True)
            db_ref[...] = jnp.sum(acc_b[...], axis=0, keepdims=True)
            tot = jnp.sum(jnp.sum(acc_l[...], axis=0, keepdims=True), axis=1, keepdims=True)
            loss_ref[...] = tot * (0.5 / D_MODEL)

    row = pl.BlockSpec((tm, D_MODEL), lambda i: (i, 0))
    vec = pl.BlockSpec((1, D_MODEL), lambda i: (0, 0))
    return _pcall(body, name=name,
                  out_shape=(SDS((s, D_MODEL), F32), SDS((s, D_MODEL), BF16), SDS((1, D_MODEL), F32), SDS((1, D_MODEL), F32),
                             SDS((1, 1), F32)),
                  grid=(nt,),
                  in_specs=[pl.BlockSpec((tm, k), lambda i: (i, 0)), pl.BlockSpec((k, D_MODEL), lambda i: (0, 0)), row, row,
                            vec, vec],
                  out_specs=(row, row, vec, vec, pl.BlockSpec((1, 1), lambda i: (0, 0))),
                  scratch_shapes=[pltpu.VMEM((8, D_MODEL), F32)] * 3, dims=("arbitrary",))(x, w, a, target, g, b)


def _mm_ln_bwd(pairs, addend, coef, a, r, g, ca, name, after=None):
    s = a.shape[0]
    has_r = r is not None
    has_add = addend is not None
    extra = [] if after is None else [after]
    n_pairs = len(pairs)

    def vmem(tm):
        tot = tm * D_MODEL * (4 * (2 + has_r) + 6)
        for pa, pb, _ in pairs:
            tot += tm * pa.shape[1] * pa.dtype.itemsize + pb.size * pb.dtype.itemsize
        return 2 * tot

    tm = next(c for c in (512, 256, 128) if s % c == 0 and vmem(c) <= MM_VMEM_BUDGET)
    nt = s // tm

    def body(*refs):
        ins = refs[2 * n_pairs:]
        add_ref = ins[0] if has_add else None
        ins = ins[has_add:]
        a_ref = ins[0]
        r_ref = ins[1] if has_r else None
        g_ref = ins[1 + has_r]
        du_ref, dub_ref, dg_ref, db_ref, acc_g, acc_b = ins[2 + has_r + len(extra):]
        i = pl.program_id(0)

        @pl.when(i == 0)
        def _():
            acc_g[...] = jnp.zeros_like(acc_g)
            acc_b[...] = jnp.zeros_like(acc_b)

        dyv = coef * add_ref[...] if has_add else None
        for p, (_, _, tb) in enumerate(pairs):
            part = lax.dot_general(refs[2 * p][...].astype(BF16), refs[2 * p + 1][...], _NT if tb else _NN,
                                   preferred_element_type=F32)
            dyv = part if dyv is None else dyv + part
        u = a_ref[...] if ca == 1.0 else ca * a_ref[...]
        if has_r:
            u = u + r_ref[...]
        mu = jnp.mean(u, axis=-1, keepdims=True)
        xc = u - mu
        var = jnp.mean(xc * xc, axis=-1, keepdims=True)
        rstd = lax.rsqrt(var + LN_EPS)
        xhat = xc * rstd
        dxh = dyv * g_ref[...]
        du = rstd * (dxh - jnp.mean(dxh, axis=-1, keepdims=True) - xhat * jnp.mean(dxh * xhat, axis=-1, keepdims=True))
        du_ref[...] = du
        dub_ref[...] = du.astype(BF16)
        acc_g[...] += jnp.sum((dyv * xhat).reshape(tm // 8, 8, D_MODEL), axis=0)
        acc_b[...] += jnp.sum(dyv.reshape(tm // 8, 8, D_MODEL), axis=0)

        @pl.when(i == nt - 1)
        def _():
            dg_ref[...] = jnp.sum(acc_g[...], axis=0, keepdims=True)
            db_ref[...] = jnp.sum(acc_b[...], axis=0, keepdims=True)

    row = pl.BlockSpec((tm, D_MODEL), lambda i: (i, 0))
    vec = pl.BlockSpec((1, D_MODEL), lambda i: (0, 0))
    in_specs, args = [], []
    for pa, pb, _ in pairs:
        in_specs += [pl.BlockSpec((tm, pa.shape[1]), lambda i: (i, 0)), pl.BlockSpec(pb.shape, lambda i: (0, 0))]
        args += [pa, pb]
    in_specs += [row] * (has_add + 1 + has_r) + [vec] + [pl.BlockSpec((8, 128), lambda i: (0, 0))] * len(extra)
    args += ([addend] if has_add else []) + [a] + ([r] if has_r else []) + [g] + extra
    return _pcall(body, name=name,
                  out_shape=(SDS((s, D_MODEL), F32), SDS((s, D_MODEL), BF16), SDS((1, D_MODEL), F32), SDS((1, D_MODEL), F32)),
                  grid=(nt,), in_specs=in_specs, out_specs=(row, row, vec, vec),
                  scratch_shapes=[pltpu.VMEM((8, D_MODEL), F32), pltpu.VMEM((8, D_MODEL), F32)],
                  dims=("arbitrary",))(*args)


def _xattn_fwd(q, k, v):
    s = q.shape[0]
    tq = _pick(s, (1024, 512))
    scale = X_HEAD_DIM ** -0.5

    def body(q_ref, k_ref, v_ref, o_ref, ob_ref):
        qv, kv, vv = q_ref[...], k_ref[...], v_ref[...]
        outs = []
        for h in range(X_HEADS):
            sl = slice(h * X_HEAD_DIM, (h + 1) * X_HEAD_DIM)
            sc = lax.dot_general(qv[:, sl], kv[:, sl], _NT, preferred_element_type=F32) * scale
            e = jnp.exp(sc - jnp.max(sc, axis=-1, keepdims=True))
            p = e / jnp.sum(e, axis=-1, keepdims=True)
            outs.append(lax.dot_general(p.astype(BF16), vv[:, sl], _NN, preferred_element_type=F32))
        o = jnp.concatenate(outs, axis=1)
        o_ref[...] = o
        ob_ref[...] = o.astype(BF16)

    row = pl.BlockSpec((tq, D_MODEL), lambda i: (i, 0))
    full = pl.BlockSpec((MEM_LEN, D_MODEL), lambda i: (0, 0))
    return _pcall(body, name="xattn_fwd", out_shape=(SDS((s, D_MODEL), F32), SDS((s, D_MODEL), BF16)), grid=(s // tq,),
                  in_specs=[row, full, full], out_specs=(row, row), dims=("parallel",))(q, k, v)


def _xattn_bwd(q, k, v, o, do):
    s = q.shape[0]
    tq = _pick(s, (1024, 512))
    scale = X_HEAD_DIM ** -0.5

    def body(q_ref, k_ref, v_ref, o_ref, do_ref, dq_ref, dk_ref, dv_ref):
        i = pl.program_id(0)

        @pl.when(i == 0)
        def _():
            dk_ref[...] = jnp.zeros_like(dk_ref)
            dv_ref[...] = jnp.zeros_like(dv_ref)

        qv, kv, vv, ov, dov = q_ref[...], k_ref[...], v_ref[...], o_ref[...], do_ref[...]
        dqs, dks, dvs = [], [], []
        for h in range(X_HEADS):
            sl = slice(h * X_HEAD_DIM, (h + 1) * X_HEAD_DIM)
            sc = lax.dot_general(qv[:, sl], kv[:, sl], _NT, preferred_element_type=F32) * scale
            e = jnp.exp(sc - jnp.max(sc, axis=-1, keepdims=True))
            p = e / jnp.sum(e, axis=-1, keepdims=True)
            doh = dov[:, sl]
            dob = doh.astype(BF16)
            delta = jnp.sum(doh * ov[:, sl], axis=-1, keepdims=True)
            dvs.append(lax.dot_general(p.astype(BF16), dob, _TN, preferred_element_type=F32))
            dp = lax.dot_general(dob, vv[:, sl], _NT, preferred_element_type=F32)
            ds = (p * (dp - delta)).astype(BF16)
            dqs.append(lax.dot_general(ds, kv[:, sl], _NN, preferred_element_type=F32) * scale)
            dks.append(lax.dot_general(ds, qv[:, sl], _TN, preferred_element_type=F32) * scale)
        dq_ref[...] = jnp.concatenate(dqs, axis=1).astype(BF16)
        dk_ref[...] += jnp.concatenate(dks, axis=1)
        dv_ref[...] += jnp.concatenate(dvs, axis=1)

    row = pl.BlockSpec((tq, D_MODEL), lambda i: (i, 0))
    full = pl.BlockSpec((MEM_LEN, D_MODEL), lambda i: (0, 0))
    return _pcall(body, name="xattn_bwd",
                  out_shape=(SDS((s, D_MODEL), BF16), SDS((MEM_LEN, D_MODEL), F32), SDS((MEM_LEN, D_MODEL), F32)),
                  grid=(s // tq,), in_specs=[row, full, full, row, row], out_specs=(row, full, full),
                  dims=("arbitrary",))(q, k, v, o, do)


_SQRT_HALF = 0.7071067811865476
_INV_SQRT_2PI = 0.3989422804014327


def _halo_specs(s, tm, width, rows=8):
    nb = s // rows
    r = tm // rows
    prev = pl.BlockSpec((rows, width), lambda i: (jnp.maximum(i * r - 1, 0), 0))
    nxt = pl.BlockSpec((rows, width), lambda i: (jnp.minimum((i + 1) * r, nb - 1), 0))
    return prev, nxt


def _shifted(x, before_row, after_row, i, nt):
    tm = x.shape[0]
    row = lax.broadcasted_iota(jnp.int32, x.shape, 0)
    first = jnp.where(i == 0, 0.0, 1.0) * before_row
    last = jnp.where(i == nt - 1, 0.0, 1.0) * after_row
    xm1 = jnp.where(row == 0, first, pltpu.roll(x, 1, 0))
    xp1 = jnp.where(row == tm - 1, last, pltpu.roll(x, tm - 1, 0))
    return xm1, xp1


BF16_ROWS = 16


def _ffn_fwd(hb, wg_t, wu_t, cw, cb):
    s = hb.shape[0]
    tm = _pick(s, (256,))
    nt = s // tm
    hr = BF16_ROWS

    def body(h_ref, hp_ref, hn_ref, wg_ref, wu_ref, cw_ref, cb_ref, g_ref, up_ref, gc_ref, act_ref):
        i = pl.program_id(0)
        hv = h_ref[...]
        g_ext = lax.dot_general(jnp.concatenate([hp_ref[...], hv, hn_ref[...]], axis=0), wg_ref[...], _NT,
                                preferred_element_type=F32)
        gv = g_ext[hr:hr + tm]
        upv = lax.dot_general(hv, wu_ref[...], _NT, preferred_element_type=F32)
        gm1, gp1 = _shifted(gv, g_ext[hr - 1:hr], g_ext[hr + tm:hr + tm + 1], i, nt)
        gc = gm1 * cw_ref[0:1, :] + gv * cw_ref[1:2, :] + gp1 * cw_ref[2:3, :] + cb_ref[...]
        cdf = 0.5 * (1.0 + lax.erf(gc * _SQRT_HALF))
        g_ref[...] = gv
        up_ref[...] = upv
        gc_ref[...] = gc
        act_ref[...] = (gc * cdf * upv).astype(BF16)

    hrow = pl.BlockSpec((tm, D_MODEL), lambda i: (i, 0))
    prev, nxt = _halo_specs(s, tm, D_MODEL, hr)
    wfull = pl.BlockSpec((D_FF, D_MODEL), lambda i: (0, 0), pipeline_mode=pl.Buffered(1))
    row = pl.BlockSpec((tm, D_FF), lambda i: (i, 0))
    return _pcall(body, name="ffn_fwd", out_shape=(SDS((s, D_FF), F32),) * 3 + (SDS((s, D_FF), BF16),),
                  grid=(nt,), in_specs=[hrow, prev, nxt, wfull, wfull, pl.BlockSpec((8, D_FF), lambda i: (0, 0)),
                                        pl.BlockSpec((1, D_FF), lambda i: (0, 0))],
                  out_specs=(row, row, row, row), dims=("parallel",))(hb, hb, hb, wg_t, wu_t, cw, cb)


def _ffn_bwd(dffb, w_down, g, gc, up, cw):
    s = g.shape[0]
    tm = _pick(s, (256,))
    nt = s // tm
    hr = BF16_ROWS

    def body(df_ref, dfp_ref, dfn_ref, wd_ref, g_ref, gc_ref, gcp_ref, gcn_ref, up_ref, upp_ref, upn_ref, cw_ref,
             dg_ref, dup_ref, dcw_ref, dcb_ref, a0, a1, a2, a3):
        i = pl.program_id(0)

        @pl.when(i == 0)
        def _():
            for a in (a0, a1, a2, a3):
                a[...] = jnp.zeros_like(a)

        def d_conv_out(gc_, up_, da_):
            cdf_ = 0.5 * (1.0 + lax.erf(gc_ * _SQRT_HALF))
            pdf_ = jnp.exp(-0.5 * gc_ * gc_) * _INV_SQRT_2PI
            return da_ * up_ * (cdf_ + gc_ * pdf_), cdf_

        df_ext = jnp.concatenate([dfp_ref[...], df_ref[...], dfn_ref[...]], axis=0)
        tn = 256
        for c0 in range(0, D_FF, tn):
            cs = slice(c0, c0 + tn)
            da_ext = lax.dot_general(df_ext, wd_ref[cs, :], _NT, preferred_element_type=F32)
            cw0, cw1, cw2 = cw_ref[0:1, cs], cw_ref[1:2, cs], cw_ref[2:3, cs]
            gc = gc_ref[:, cs]
            da = da_ext[hr:hr + tm]
            dgc, cdf = d_conv_out(gc, up_ref[:, cs], da)
            dup_ref[:, cs] = (da * (gc * cdf)).astype(BF16)
            dgc_b = jnp.where(i == 0, 0.0, 1.0) * d_conv_out(gcp_ref[7:8, cs], upp_ref[7:8, cs], da_ext[hr - 1:hr])[0]
            dgc_a = jnp.where(i == nt - 1, 0.0, 1.0) * d_conv_out(gcn_ref[0:1, cs], upn_ref[0:1, cs],
                                                                  da_ext[hr + tm:hr + tm + 1])[0]
            row = lax.broadcasted_iota(jnp.int32, dgc.shape, 0)
            dgc_m1 = jnp.where(row == 0, dgc_b, pltpu.roll(dgc, 1, 0))
            dgc_p1 = jnp.where(row == tm - 1, dgc_a, pltpu.roll(dgc, tm - 1, 0))
            dg_ref[:, cs] = (dgc_p1 * cw0 + dgc * cw1 + dgc_m1 * cw2).astype(BF16)

            def fold(t):
                return jnp.sum(t.reshape(tm // 8, 8, tn), axis=0)

            gv = g_ref[:, cs]
            a0[:, cs] += fold(dgc_p1 * gv)
            a1[:, cs] += fold(dgc * gv)
            a2[:, cs] += fold(dgc_m1 * gv)
            a3[:, cs] += fold(dgc)

        @pl.when(i == nt - 1)
        def _():
            dcw_ref[...] = jnp.concatenate(
                [jnp.sum(a[...], axis=0, keepdims=True) for a in (a0, a1, a2)] + [jnp.zeros((5, D_FF), F32)], axis=0)
            dcb_ref[...] = jnp.sum(a3[...], axis=0, keepdims=True)

    row = pl.BlockSpec((tm, D_FF), lambda i: (i, 0))
    prev, nxt = _halo_specs(s, tm, D_FF)
    cw_spec = pl.BlockSpec((8, D_FF), lambda i: (0, 0))
    cb_spec = pl.BlockSpec((1, D_FF), lambda i: (0, 0))
    dprev, dnxt = _halo_specs(s, tm, D_MODEL, hr)
    return _pcall(body, name="ffn_bwd",
                  out_shape=(SDS((s, D_FF), BF16), SDS((s, D_FF), BF16), SDS((8, D_FF), F32), SDS((1, D_FF), F32)),
                  grid=(nt,),
                  in_specs=[pl.BlockSpec((tm, D_MODEL), lambda i: (i, 0)), dprev, dnxt,
                            pl.BlockSpec((D_FF, D_MODEL), lambda i: (0, 0), pipeline_mode=pl.Buffered(1)), row]
                  + [row, prev, nxt] * 2 + [cw_spec],
                  out_specs=(row, row, cw_spec, cb_spec), scratch_shapes=[pltpu.VMEM((8, D_FF), F32)] * 4,
                  dims=("arbitrary",))(dffb, dffb, dffb, w_down, g, gc, gc, gc, up, up, up, cw)


def _adamw(w, g, m, v, name):
    rows, cols = w.shape
    tr = _pick(rows, (256, 128, 64, 32, 16, 8))
    c1 = 1.0 - ADAM_B1 ** ADAM_STEP
    c2 = 1.0 - ADAM_B2 ** ADAM_STEP

    def body(w_ref, g_ref, m_ref, v_ref, d_ref, nm_ref, nv_ref):
        gv = g_ref[...]
        nm = ADAM_B1 * m_ref[...] + (1.0 - ADAM_B1) * gv
        nv = ADAM_B2 * v_ref[...] + (1.0 - ADAM_B2) * (gv * gv)
        d_ref[...] = -ADAM_LR * ((nm / c1) / (jnp.sqrt(nv / c2) + ADAM_EPS) + ADAM_WD * w_ref[...])
        nm_ref[...] = nm
        nv_ref[...] = nv

    blk = pl.BlockSpec((tr, cols), lambda i: (i, 0))
    return _pcall(body, name=name, out_shape=(SDS(w.shape, F32),) * 3, grid=(rows // tr,), in_specs=[blk] * 4,
                  out_specs=(blk,) * 3, dims=("parallel",))(w, g, m, v)


def _adamw_many(ws, gs, ms, vs, name):
    n = len(ws)
    c1 = 1.0 - ADAM_B1 ** ADAM_STEP
    c2 = 1.0 - ADAM_B2 ** ADAM_STEP

    def body(*refs):
        outs = refs[4 * n:]
        for k in range(n):
            gv = refs[n + k][...]
            nm = ADAM_B1 * refs[2 * n + k][...] + (1.0 - ADAM_B1) * gv
            nv = ADAM_B2 * refs[3 * n + k][...] + (1.0 - ADAM_B2) * (gv * gv)
            outs[k][...] = -ADAM_LR * ((nm / c1) / (jnp.sqrt(nv / c2) + ADAM_EPS) + ADAM_WD * refs[k][...])
            outs[n + k][...] = nm
            outs[2 * n + k][...] = nv

    shapes = tuple(SDS(w.shape, F32) for w in ws)
    res = _pcall(body, name=name, out_shape=shapes * 3)(*ws, *gs, *ms, *vs)
    return res[:n], res[n:2 * n], res[2 * n:]


def _all_gather_rows(x_shard, *, name, in_vmem, sum_rows=False, after=None):
    m_per, n = x_shard.shape
    extra = [] if after is None else [after]

    def body(x_ref, *rest):
        out_ref, rest = rest[len(extra)], rest[len(extra) + 1:]
        if sum_rows:
            sum_ref, send_sems, recv_sems, local_sem = rest
        else:
            send_sems, recv_sems, local_sem = rest
        x, y, c = lax.axis_index("x"), lax.axis_index("y"), lax.axis_index("c")
        me, sibling = (x, y, c), (x, y, 1 - c)
        chips = [(1 - x, y), (x, 1 - y), (1 - x, 1 - y)]

        def rows(px, py, pc):
            return out_ref.at[pl.ds((4 * px + 2 * py + pc) * m_per, m_per), :]

        def copy(k, block, to, src=None):
            return pltpu.make_async_remote_copy(
                src_ref=rows(*block) if src is None else src, dst_ref=rows(*block), send_sem=send_sems.at[k],
                recv_sem=recv_sems.at[k], device_id=to, device_id_type=pl.DeviceIdType.MESH)

        mine = pltpu.make_async_copy(x_ref, rows(*me), local_sem)
        mine.start()
        first = [copy(0, me, sibling, src=x_ref)]
        first += [copy(1 + j, me, (*chip, c), src=x_ref) for j, chip in enumerate(chips)]
        for cp in first:
            cp.start()
        passed = [copy(4 + j, (*chip, c), sibling) for j, chip in enumerate(chips)]
        for j, chip in enumerate(chips):
            copy(1 + j, (*chip, c), me).wait_recv()
            passed[j].start()
        copy(0, sibling, me).wait_recv()
        for j, chip in enumerate(chips):
            copy(4 + j, (*chip, 1 - c), me).wait_recv()
        for cp in first + passed:
            cp.wait_send()
        mine.wait()
        if sum_rows:
            acc = out_ref[0:m_per, :]
            for dev in range(1, N_DEV):
                acc = acc + out_ref[dev * m_per:(dev + 1) * m_per, :]
            sum_ref[...] = acc

    space = pltpu.VMEM if in_vmem else pl.ANY
    out_shape = [SDS((N_DEV * m_per, n), x_shard.dtype)]
    out_specs = [pl.BlockSpec(memory_space=space)]
    if sum_rows:
        out_shape.append(SDS((m_per, n), x_shard.dtype))
        out_specs.append(pl.BlockSpec(memory_space=pltpu.VMEM))
    res = _PALLAS_CALL(
        body, name=name, out_shape=tuple(out_shape),
        in_specs=[pl.BlockSpec(memory_space=space)] + [pl.BlockSpec(memory_space=pl.ANY)] * len(extra),
        out_specs=tuple(out_specs),
        scratch_shapes=[pltpu.SemaphoreType.DMA((7,)), pltpu.SemaphoreType.DMA((7,)), pltpu.SemaphoreType.DMA],
        compiler_params=pltpu.CompilerParams(vmem_limit_bytes=VMEM_LIMIT_BYTES),
    )(x_shard, *extra)
    return res if sum_rows else res[0]


_HBM = pl.BlockSpec(memory_space=pltpu.HBM)
_SEM = pl.BlockSpec(memory_space=pltpu.SEMAPHORE)
_SPLIT_PARAMS = dict(has_side_effects=pltpu.SideEffectType.DATAFLOW_SIDE_EFFECTING)


def _split_copies(src_ref, land_ref, send_sems, recv_sems, gather):
    x, y, c = lax.axis_index("x"), lax.axis_index("y"), lax.axis_index("c")
    first = 0 if gather else 1
    copies = []
    for k in range(first, N_DEV):
        px = 1 - x if k & 4 else x
        py = 1 - y if k & 2 else y
        pc = 1 - c if k & 1 else c
        if gather:
            rows = src_ref.shape[0]
            src, dst = src_ref, land_ref.at[pl.ds((4 * x + 2 * y + c) * rows, rows), :]
        else:
            src, dst = src_ref.at[4 * px + 2 * py + pc], land_ref.at[k - 1]
        copies.append(pltpu.make_async_remote_copy(
            src_ref=src, dst_ref=dst, send_sem=send_sems.at[k - first], recv_sem=recv_sems.at[k - first],
            device_id=(px, py, pc), device_id_type=pl.DeviceIdType.MESH))
    return copies


def _exchange_start(src, land_shape, *, gather, name):
    def body(src_ref, land_ref, send_sems, recv_sems, src_thru, land_thru, token):
        for cp in _split_copies(src_ref, land_ref, send_sems, recv_sems, gather):
            cp.start()
        token[...] = jnp.zeros_like(token)

    land = pltpu.with_memory_space_constraint(lax.empty(land_shape, src.dtype), pltpu.HBM)
    n_copies = N_DEV if gather else N_DEV - 1
    return _PALLAS_CALL(
        body, name=name,
        out_shape=(pltpu.SemaphoreType.DMA((n_copies,)), pltpu.SemaphoreType.DMA((n_copies,)),
                   pltpu.HBM(src.shape, src.dtype), pltpu.HBM(land_shape, src.dtype), SDS((8, 128), F32)),
        in_specs=(_HBM, _HBM), out_specs=(_SEM, _SEM, _HBM, _HBM, pl.BlockSpec(memory_space=pltpu.VMEM)),
        input_output_aliases={0: 2, 1: 3}, compiler_params=pltpu.CompilerParams(**_SPLIT_PARAMS),
    )(pltpu.with_memory_space_constraint(src, pltpu.HBM), land)


def _exchange_wait(started, after, *, gather, name):
    send_sems, recv_sems, src_thru, land_thru, _ = started

    def body(src_ref, land_ref, send_sems, recv_sems, after_ref, src_out, land_out):
        copies = _split_copies(src_ref, land_ref, send_sems, recv_sems, gather)
        for cp in copies:
            cp.wait_send()
        for cp in copies:
            cp.wait_recv()

    return _PALLAS_CALL(
        body, name=name,
        out_shape=(pltpu.HBM(src_thru.shape, src_thru.dtype), pltpu.HBM(land_thru.shape, land_thru.dtype)),
        in_specs=(_HBM, _HBM, _SEM, _SEM, pl.BlockSpec(memory_space=pl.ANY)), out_specs=(_HBM, _HBM),
        input_output_aliases={0: 0, 1: 1}, compiler_params=pltpu.CompilerParams(**_SPLIT_PARAMS),
    )(src_thru, land_thru, send_sems, recv_sems, after)


def _sum_parts(own, land, name):
    r, n = own.shape
    tr = _pick(r, (264, 320, 336, 128, 64, 32, 16, 8))

    def body(own_ref, x_ref, o_ref):
        acc = own_ref[...]
        for k in range(N_DEV - 1):
            acc = acc + x_ref[k].astype(F32)
        o_ref[...] = acc

    return _pcall(body, name=name, out_shape=SDS((r, n), F32), grid=(r // tr,),
                  in_specs=[pl.BlockSpec((tr, n), lambda i: (i, 0)), pl.BlockSpec((N_DEV - 1, tr, n), lambda i: (0, i, 0))],
                  out_specs=pl.BlockSpec((tr, n), lambda i: (i, 0)), dims=("parallel",))(own, land)


def _pad_rows(a, rows):
    return jnp.pad(a, ((0, rows - a.shape[0]), (0, 0)))


def kernel(x, mem, positions, ln_in_g, ln_in_b, w_in, attn_sink, g_win, g_dil, w_mix_out, ln1_g, ln1_b, mem_ln_g, mem_ln_b, w_xq, w_xk, w_xv, w_xo, ln2_g, ln2_b, w_gate, w_up, conv_w, conv_b, w_down, ln3_g, ln3_b, loss_target, m_ln_in_g, m_ln_in_b, m_w_in, m_attn_sink, m_g_win, m_g_dil, m_w_mix_out, m_ln1_g, m_ln1_b, m_mem_ln_g, m_mem_ln_b, m_w_xq, m_w_xk, m_w_xv, m_w_xo, m_ln2_g, m_ln2_b, m_w_gate, m_w_up, m_conv_w, m_conv_b, m_w_down, m_ln3_g, m_ln3_b, v_ln_in_g, v_ln_in_b, v_w_in, v_attn_sink, v_g_win, v_g_dil, v_w_mix_out, v_ln1_g, v_ln1_b, v_mem_ln_g, v_mem_ln_b, v_w_xq, v_w_xk, v_w_xv, v_w_xo, v_ln2_g, v_ln2_b, v_w_gate, v_w_up, v_conv_w, v_conv_b, v_w_down, v_ln3_g, v_ln3_b):
    weights = dict(ln_in_g=ln_in_g, ln_in_b=ln_in_b, w_in=w_in, attn_sink=attn_sink, g_win=g_win, g_dil=g_dil, w_mix_out=w_mix_out, ln1_g=ln1_g, ln1_b=ln1_b, mem_ln_g=mem_ln_g, mem_ln_b=mem_ln_b, w_xq=w_xq, w_xk=w_xk, w_xv=w_xv, w_xo=w_xo, ln2_g=ln2_g, ln2_b=ln2_b, w_gate=w_gate, w_up=w_up, conv_w=conv_w, conv_b=conv_b, w_down=w_down, ln3_g=ln3_g, ln3_b=ln3_b)
    mom_m = dict(ln_in_g=m_ln_in_g, ln_in_b=m_ln_in_b, w_in=m_w_in, attn_sink=m_attn_sink, g_win=m_g_win, g_dil=m_g_dil, w_mix_out=m_w_mix_out, ln1_g=m_ln1_g, ln1_b=m_ln1_b, mem_ln_g=m_mem_ln_g, mem_ln_b=m_mem_ln_b, w_xq=m_w_xq, w_xk=m_w_xk, w_xv=m_w_xv, w_xo=m_w_xo, ln2_g=m_ln2_g, ln2_b=m_ln2_b, w_gate=m_w_gate, w_up=m_w_up, conv_w=m_conv_w, conv_b=m_conv_b, w_down=m_w_down, ln3_g=m_ln3_g, ln3_b=m_ln3_b)
    mom_v = dict(ln_in_g=v_ln_in_g, ln_in_b=v_ln_in_b, w_in=v_w_in, attn_sink=v_attn_sink, g_win=v_g_win, g_dil=v_g_dil, w_mix_out=v_w_mix_out, ln1_g=v_ln1_g, ln1_b=v_ln1_b, mem_ln_g=v_mem_ln_g, mem_ln_b=v_mem_ln_b, w_xq=v_w_xq, w_xk=v_w_xk, w_xv=v_w_xv, w_xo=v_w_xo, ln2_g=v_ln2_g, ln2_b=v_ln2_b, w_gate=v_w_gate, w_up=v_w_up, conv_w=v_conv_w, conv_b=v_conv_b, w_down=v_w_down, ln3_g=v_ln3_g, ln3_b=v_ln3_b)
    order = list(weights)
    s = x.shape[1]
    xs = x[0]
    mems = mem[0]
    target = loss_target[0]
    row = lambda a: a.reshape(1, -1)

    shard_rows = dict(w_in=w_in[0].T, w_gate=w_gate[0].T, w_up=w_up[0].T, w_mix_out=w_mix_out[0], w_xq=w_xq[0],
                      w_xk=w_xk[0], w_xv=w_xv[0], w_xo=w_xo[0], w_down=w_down[0])
    me_lin = 4 * lax.axis_index("x") + 2 * lax.axis_index("y") + lax.axis_index("c")
    w_in_full = _all_gather_rows(shard_rows["w_in"].astype(BF16), name="w_in_all_gather", in_vmem=False)
    late_rows = PACK_ROWS[1:]
    late_r = sum(r for _, r in late_rows)
    packed = jnp.concatenate([shard_rows[n].astype(BF16) for n, _ in late_rows], axis=0)
    w_started = _exchange_start(packed, (N_DEV * late_r, D_MODEL), gather=True, name="weight_gather_start")

    tabs = _rope_tables(positions.astype(F32).reshape(s, 1) + w_started[4][0, 0])
    h0, h0b = _ln_fwd(xs, None, row(ln_in_g), row(ln_in_b), 1.0, "ln_in_fwd")
    zw, *zg = _proj_rope(h0b, w_in_full, tabs[0])
    cw_cols = -(-FF_SHARD // 128) * 128
    cw_pad = jnp.pad(conv_w[0], ((0, 5), (0, cw_cols - FF_SHARD)))
    cw_all = _all_gather_rows(cw_pad, name="conv_w_all_gather", in_vmem=True, after=zw).reshape(N_DEV, 8, cw_cols)
    cw_full = jnp.transpose(cw_all[:, :3, :FF_SHARD], (1, 0, 2)).reshape(3, D_FF)
    cw8 = _pad_rows(cw_full, 8)
    oa, lse_a = _banded_fwd(zw, attn_sink, name="win_attn_fwd", **{**_WIN_CFG, "tq": 2 * ATTN_TQ})
    og_views, lg_views = [], []
    for gi in range(3):
        o_g, l_g = _banded_fwd(zg[gi], None, name=f"dil_attn_fwd{gi}", **{**_dil_cfg(gi), "tq": 2 * ATTN_TQ})
        og_views.append(o_g)
        lg_views.append(l_g)
    mixed, ob_views, lb_views = _mix_norm_fwd(oa, og_views, lg_views, g_win, g_dil)
    _, land = _exchange_wait(w_started, mixed, gather=True, name="weight_gather_wait")
    gathered = land.reshape(N_DEV, late_r, D_MODEL)
    full = {}
    off = 0
    for n, r in late_rows:
        full[n] = gathered[:, off:off + r, :].reshape(N_DEV * r, D_MODEL)
        off += r
    mix, h1, h1b = _mm_ln_fwd(mixed, full["w_mix_out"], h0, ln1_g, ln1_b, ALPHA, "mm_mix_out_ln1")
    _, mem_nb = _ln_fwd(mems, None, mem_ln_g, mem_ln_b, 1.0, "mem_ln_fwd")
    kx = _mm(mem_nb, full["w_xk"], trans_b=False, out_dtype=BF16, name="mm_xk")
    vx = _mm(mem_nb, full["w_xv"], trans_b=False, out_dtype=BF16, name="mm_xv")
    qx = _mm(h1b, full["w_xq"], trans_b=False, out_dtype=BF16, name="mm_xq")
    ox, oxb = _xattn_fwd(qx, kx, vx)
    xa, h2, h2b = _mm_ln_fwd(oxb, full["w_xo"], h1, ln2_g, ln2_b, ALPHA, "mm_xo_ln2")
    gate, up, gc, act = _ffn_fwd(h2b, full["w_gate"], full["w_up"], cw8, conv_b)

    du3, du3b, d_ln3_g, d_ln3_b, loss_local = _mm_ln_loss(act, full["w_down"], h2, target, ln3_g, ln3_b, ALPHA,
                                                          "mm_down_ln3_loss")
    dw_down = _mm_tn(act, du3b, name="mm_dw_down")
    dgate, dup, dcw8, d_conv_b = _ffn_bwd(du3b, full["w_down"], gate, gc, up, cw8)
    dw_gate_t = _mm_tn(dgate, h2b, name="mm_dw_gate")
    dw_up_t = _mm_tn(dup, h2b, name="mm_dw_up")
    rows_of = dict(PACK_ROWS)

    own_f32 = {}

    def start_grad_exchange(parts, name, payload=F32):
        gp = jnp.concatenate([g.reshape(N_DEV, rows_of[n], D_MODEL) for n, g in parts], axis=1)
        if payload != F32:
            own_f32[name] = lax.dynamic_index_in_dim(gp, me_lin, axis=0, keepdims=False)
            gp = gp.astype(payload)
        return _exchange_start(gp, (N_DEV - 1,) + gp.shape[1:], gather=False, name=name)

    ffn_parts = (("w_gate", dw_gate_t), ("w_up", dw_up_t), ("w_down", dw_down))
    ffn_started = start_grad_exchange(ffn_parts, "grad_start_ffn")
    du2, du2b, d_ln2_g, d_ln2_b = _mm_ln_bwd(((dgate, full["w_gate"], False), (dup, full["w_up"], False)), du3, ALPHA,
                                             h1, xa, ln2_g + ffn_started[4][0, 0], ALPHA, "mm_dh2_ln2_bwd")
    dox = _mm(du2b, full["w_xo"], trans_b=True, out_dtype=F32, name="mm_d_ox")
    dw_xo = _mm_tn(oxb, du2b, name="mm_dw_xo")
    dqx, dkx, dvx = _xattn_bwd(qx, kx, vx, ox, dox)
    dw_xq = _mm_tn(h1b, dqx, name="mm_dw_xq")
    dw_xk = _mm_tn(mem_nb, dkx, name="mm_dw_xk")
    dw_xv = _mm_tn(mem_nb, dvx, name="mm_dw_xv")
    _, _, d_mem_ln_g, d_mem_ln_b = _mm_ln_bwd(((dkx, full["w_xk"], True), (dvx, full["w_xv"], True)), None, 1.0, mems,
                                              None, mem_ln_g, 1.0, "mm_dmem_ln_bwd")
    du1, du1b, d_ln1_g, d_ln1_b = _mm_ln_bwd(((dqx, full["w_xq"], True),), du2, ALPHA, h0, mix, ln1_g, ALPHA,
                                             "mm_dh1_ln1_bwd")
    dmixed = _mm(du1b, full["w_mix_out"], trans_b=True, out_dtype=F32, name="mm_d_mixed")
    dw_mix_out = _mm_tn(mixed, du1b, name="mm_dw_mix_out")
    attn_parts = (("w_mix_out", dw_mix_out), ("w_xq", dw_xq), ("w_xk", dw_xk), ("w_xv", dw_xv), ("w_xo", dw_xo))
    attn_started = start_grad_exchange(attn_parts, "grad_start_attn")
    doa, dob_views, d_g_win, d_g_dil = _mix_norm_bwd(oa, ob_views[0], dmixed, g_win + attn_started[4][0, 0], g_dil)
    dqa, dkva, dsink8 = _banded_bwd(zw, oa, lse_a, doa, tabs[0], attn_sink, name="win_attn_bwd", **_WIN_CFG)
    dq_views, dkv_views = [], []
    for gi in range(3):
        dq_g, dkv_g = _banded_bwd(zg[gi], ob_views[gi], lb_views[gi], dob_views[gi], tabs[gi], None,
                                  name=f"dil_attn_bwd{gi}", **{**_dil_cfg(gi), "rc": ATTN_TQ})
        dq_views.append(dq_g)
        dkv_views.append(dkv_g)
    dz = _dz_assemble(dq_views, dkv_views, dqa, dkva)
    dw_in_t = _mm_tn(dz, h0b, name="mm_dw_in")
    in_parts = (("w_in", dw_in_t),)
    in_started = start_grad_exchange(in_parts, "grad_start_in", payload=BF16)
    dx, _, d_ln_in_g, d_ln_in_b = _mm_ln_bwd(((dz, w_in_full, False),), du1, ALPHA, xs, None, row(ln_in_g), 1.0,
                                             "mm_dh0_ln_in_bwd", after=in_started[4])

    grads, delta, new_m, new_v = {}, {}, {}, {}
    after = dx
    for parts, started, tag in ((ffn_parts, ffn_started, "ffn"), (attn_parts, attn_started, "attn"),
                                (in_parts, in_started, "in")):
        gp_thru, land = _exchange_wait(started, after, gather=False, name=f"grad_wait_{tag}")
        own = own_f32.get(f"grad_start_{tag}")
        if own is None:
            own = lax.dynamic_index_in_dim(gp_thru, me_lin, axis=0, keepdims=False)
        gsum = _sum_parts(own, land, f"grad_sum_{tag}")
        off = 0
        for n, _ in parts:
            blk = gsum[off:off + rows_of[n]]
            off += rows_of[n]
            grads[n] = (blk.T if n in ("w_in", "w_gate", "w_up") else blk)[None]
            shp = weights[n].shape
            d_, m_, v_ = _adamw(weights[n].reshape(shp[1:]), grads[n].reshape(shp[1:]), mom_m[n].reshape(shp[1:]),
                                mom_v[n].reshape(shp[1:]), f"adamw_{n}")
            delta[n], new_m[n], new_v[n] = d_.reshape(shp), m_.reshape(shp), v_.reshape(shp)
            after = d_

    small = jnp.concatenate([
        d_ln_in_g, d_ln_in_b, d_ln1_g, d_ln1_b, d_mem_ln_g, d_mem_ln_b, d_ln2_g, d_ln2_b, d_ln3_g, d_ln3_b,
        jnp.concatenate([d_g_win, d_g_dil], axis=1),
        jnp.pad(d_conv_b, ((0, 0), (0, FF_PAD - D_FF))).reshape(3, 1024),
        jnp.pad(dsink8[0:1, :], ((0, 0), (0, 1024 - 128))),
        jnp.pad(dcw8[0:3], ((0, 0), (0, FF_PAD - D_FF))).reshape(9, 1024),
    ], axis=0)
    _, ssum = _all_gather_rows(small, name="small_grad_all_reduce", in_vmem=True, sum_rows=True, after=after)
    names10 = ["ln_in_g", "ln_in_b", "ln1_g", "ln1_b", "mem_ln_g", "mem_ln_b", "ln2_g", "ln2_b", "ln3_g", "ln3_b"]
    small_g = {n: ssum[i:i + 1] for i, n in enumerate(names10)}
    small_g["g_win"] = ssum[10:11, :512]
    small_g["g_dil"] = ssum[10:11, 512:]
    small_g["conv_b"] = ssum[11:14].reshape(1, FF_PAD)[:, :D_FF]
    small_g["attn_sink"] = ssum[14:15, :8]
    small_g["conv_w"] = lax.dynamic_slice_in_dim(ssum[15:24].reshape(3, FF_PAD)[:, :D_FF], me_lin * FF_SHARD, FF_SHARD,
                                                 axis=1)

    small_names = [n for n in order if n not in rows_of]
    two_d = lambda a: a.reshape(-1, a.shape[-1])
    d_s, m_s, v_s = _adamw_many([two_d(weights[n]) for n in small_names], [small_g[n] for n in small_names],
                                [two_d(mom_m[n]) for n in small_names], [two_d(mom_v[n]) for n in small_names],
                                "adamw_small")
    for k, n in enumerate(small_names):
        shp = weights[n].shape
        grads[n], delta[n], new_m[n], new_v[n] = (t.reshape(shp) for t in (small_g[n], d_s[k], m_s[k], v_s[k]))

    loss = lax.psum(loss_local[0, 0], MESH_AXES)
    return (loss, dx[None], *[grads[n] for n in order], *[delta[n] for n in order], *[new_m[n] for n in order],
            *[new_v[n] for n in order])
```
